```python
import math
import jax, jax.numpy as jnp
from jax import lax
import numpy as np

D_MODEL = 1024
BATCH = 8
SEQ = 4096
DEPTH = 2

N_A_LAYERS = DEPTH // 2
N_B_LAYERS = DEPTH - N_A_LAYERS
N_DENSE = (DEPTH + 1) // 2
N_MOE = DEPTH // 2
MLA_HEADS = 8
QK_NOPE = 128
QK_ROPE = 64
V_DIM = 128
Q_LORA = 384
KV_LORA = 256
ROPE_THETA = 10000.0
SB_HEADS = 8
SB_HEAD_DIM = 128
D_FF = 3584
N_EXPERTS = 8
TOP_K = 2
Q_BLOCK = 128
EPS = 1e-6
COND_MULT = 6

kernel_name = "yoco_mla_stickbreaking_moe_adaln"


def rms_norm(x, g):
    xf = x.astype(jnp.float32)
    y = xf * lax.rsqrt(jnp.mean(xf * xf, axis=-1, keepdims=True) + EPS)
    return (y * g.astype(jnp.float32)).astype(x.dtype)


def modulate(h, shift, scale):
    return h * (1 + scale[:, None, :]) + shift[:, None, :]


def rope(x, pos):
    half = x.shape[-1] // 2
    inv = ROPE_THETA ** (-jnp.arange(half, dtype=jnp.float32) / half)
    ang = pos.astype(jnp.float32)[..., None] * inv
    cos = jnp.cos(ang)[:, :, None, :]
    sin = jnp.sin(ang)[:, :, None, :]
    xf = x.astype(jnp.float32)
    x1, x2 = xf[..., :half], xf[..., half:]
    return jnp.concatenate([x1 * cos - x2 * sin, x1 * sin + x2 * cos], axis=-1).astype(x.dtype)


def _split_query_blocks(q):
    b, s, h, d = q.shape
    nb = s // Q_BLOCK
    return q.reshape(b, nb, Q_BLOCK, h, d).swapaxes(0, 1), nb


def _merge_query_blocks(o):
    nb, b, qb, h, d = o.shape
    return o.swapaxes(0, 1).reshape(b, nb * qb, h, d)


def causal_softmax_attention(q, k, v):
    scale = 1.0 / math.sqrt(q.shape[-1])
    qb_all, nb = _split_query_blocks(q)
    kpos = jnp.arange(k.shape[1])

    def block(args):
        qb, bi = args
        s = jnp.einsum('bqhd,bkhd->bhqk', qb, k).astype(jnp.float32) * scale
        qpos = bi * Q_BLOCK + jnp.arange(Q_BLOCK)
        mask = kpos[None, :] <= qpos[:, None]
        p = jax.nn.softmax(jnp.where(mask, s, -jnp.inf), axis=-1)
        return jnp.einsum('bhqk,bkhd->bqhd', p.astype(v.dtype), v)

    return _merge_query_blocks(lax.map(block, (qb_all, jnp.arange(nb))))


def stick_breaking_attention(q, k, v):
    scale = 1.0 / math.sqrt(q.shape[-1])
    qb_all, nb = _split_query_blocks(q)
    kpos = jnp.arange(k.shape[1])

    def block(args):
        qb, bi = args
        z = jnp.einsum('bqhd,bkhd->bhqk', qb, k).astype(jnp.float32) * scale
        qpos = bi * Q_BLOCK + jnp.arange(Q_BLOCK)
        mask = kpos[None, :] < qpos[:, None]
        log_beta = jax.nn.log_sigmoid(z)
        log_1m = jnp.where(mask, log_beta - z, 0.0)
        suffix = lax.cumsum(log_1m, axis=3, reverse=True) - log_1m
        a = jnp.where(mask, jnp.exp(log_beta + suffix), 0.0)
        return jnp.einsum('bhqk,bkhd->bqhd', a.astype(v.dtype), v)

    return _merge_query_blocks(lax.map(block, (qb_all, jnp.arange(nb))))


def mla_mixer(h, positions, w_a_down, g_q_lat, g_kv_lat, w_uq, w_ukv, w_oa):
    b, s, _ = h.shape
    lat = h @ w_a_down
    c_q = rms_norm(lat[..., :Q_LORA], g_q_lat)
    c_kv = rms_norm(lat[..., Q_LORA:Q_LORA + KV_LORA], g_kv_lat)
    k_rot = lat[..., Q_LORA + KV_LORA:][:, :, None, :]
    q = (c_q @ w_uq).reshape(b, s, MLA_HEADS, QK_NOPE + QK_ROPE)
    q = jnp.concatenate([q[..., :QK_NOPE], rope(q[..., QK_NOPE:], positions)], axis=-1)
    kv = (c_kv @ w_ukv).reshape(b, s, MLA_HEADS, QK_NOPE + V_DIM)
    k_rot = jnp.broadcast_to(rope(k_rot, positions), (b, s, MLA_HEADS, QK_ROPE))
    k = jnp.concatenate([kv[..., :QK_NOPE], k_rot], axis=-1)
    v = kv[..., QK_NOPE:]
    o = causal_softmax_attention(q, k, v)
    return o.reshape(b, s, MLA_HEADS * V_DIM) @ w_oa


def shared_kv(x, silu_c, w_mod_kv, b_mod_kv, g_kv, w_kv_sb):
    b, s, _ = x.shape
    shift, scale = jnp.split(silu_c @ w_mod_kv + b_mod_kv, 2, axis=-1)
    hk = modulate(rms_norm(x, g_kv), shift, scale)
    kv = (hk @ w_kv_sb).reshape(b, s, 2, SB_HEADS, SB_HEAD_DIM)
    return kv[:, :, 0], kv[:, :, 1]


def stick_breaking_mixer(h, k_sh, v_sh, w_q_sb, w_o_sb):
    b, s, _ = h.shape
    q = (h @ w_q_sb).reshape(b, s, SB_HEADS, SB_HEAD_DIM)
    o = stick_breaking_attention(q, k_sh, v_sh)
    return o.reshape(b, s, SB_HEADS * SB_HEAD_DIM) @ w_o_sb


def swiglu(h, w_gu, w_down):
    gu = h @ w_gu
    return (jax.nn.silu(gu[..., :D_FF]) * gu[..., D_FF:]) @ w_down


def moe_swiglu(h, w_router, b_router, w_exp_gu, w_exp_down):
    b, s, d = h.shape
    hf = h.reshape(b * s, d)
    logits = (hf @ w_router).astype(jnp.float32) + b_router.astype(jnp.float32)
    top_val, top_idx = lax.top_k(logits, TOP_K)
    wts = jax.nn.softmax(top_val, axis=-1)
    comb = jnp.sum(jax.nn.one_hot(top_idx, N_EXPERTS, dtype=h.dtype)
                   * wts[..., None].astype(h.dtype), axis=1)
    y = jnp.zeros_like(hf)
    for e in range(N_EXPERTS):
        y = y + comb[:, e:e + 1] * swiglu(hf, w_exp_gu[e], w_exp_down[e])
    return y.reshape(b, s, d)


def setup_inputs(seed: int = 0) -> dict:
    key = jax.random.key(seed)
    ks = jax.random.split(key, 32)
    D = D_MODEL

    def nrm(k, shape, fan_in, mult=1.0):
        return jax.random.normal(k, shape, jnp.float32) * (mult * fan_in ** -0.5)

    def gain(k, shape):
        return 1.0 + 0.02 * jax.random.normal(k, shape, jnp.float32)

    offsets = jax.random.randint(ks[2], (BATCH, 1), 0, 1024)
    positions = (jnp.arange(SEQ, dtype=jnp.int32)[None, :] + offsets).astype(jnp.int32)
    return {
        "x": jax.random.normal(ks[0], (BATCH, SEQ, D), jnp.float32),
        "c": jax.random.normal(ks[1], (BATCH, D), jnp.float32),
        "positions": positions,
        "w_mod": nrm(ks[3], (DEPTH, D, COND_MULT * D), D, 0.5),
        "b_mod": 0.02 * jax.random.normal(ks[4], (DEPTH, COND_MULT * D), jnp.float32),
        "g_mix": gain(ks[5], (DEPTH, D)),
        "g_ffn": gain(ks[6], (DEPTH, D)),
        "w_a_down": nrm(ks[7], (N_A_LAYERS, D, Q_LORA + KV_LORA + QK_ROPE), D),
        "g_q_lat": gain(ks[8], (N_A_LAYERS, Q_LORA)),
        "g_kv_lat": gain(ks[9], (N_A_LAYERS, KV_LORA)),
        "w_uq": nrm(ks[10], (N_A_LAYERS, Q_LORA, MLA_HEADS * (QK_NOPE + QK_ROPE)), Q_LORA),
        "w_ukv": nrm(ks[11], (N_A_LAYERS, KV_LORA, MLA_HEADS * (QK_NOPE + V_DIM)), KV_LORA),
        "w_oa": nrm(ks[12], (N_A_LAYERS, MLA_HEADS * V_DIM, D), MLA_HEADS * V_DIM),
        "w_mod_kv": nrm(ks[13], (D, 2 * D), D, 0.5),
        "b_mod_kv": 0.02 * jax.random.normal(ks[14], (2 * D,), jnp.float32),
        "g_kv": gain(ks[15], (D,)),
        "w_kv_sb": nrm(ks[16], (D, 2 * SB_HEADS * SB_HEAD_DIM), D),
        "w_q_sb": nrm(ks[17], (N_B_LAYERS, D, SB_HEADS * SB_HEAD_DIM), D),
        "w_o_sb": nrm(ks[18], (N_B_LAYERS, SB_HEADS * SB_HEAD_DIM, D), SB_HEADS * SB_HEAD_DIM),
        "w_ffn_gu": nrm(ks[19], (N_DENSE, D, 2 * D_FF), D),
        "w_ffn_down": nrm(ks[20], (N_DENSE, D_FF, D), D_FF),
        "w_router": nrm(ks[21], (N_MOE, D, N_EXPERTS), D),
        "b_router": 0.01 * jax.random.normal(ks[22], (N_MOE, N_EXPERTS), jnp.float32),
        "w_exp_gu": nrm(ks[23], (N_MOE, N_EXPERTS, D, 2 * D_FF), D),
        "w_exp_down": nrm(ks[24], (N_MOE, N_EXPERTS, D_FF, D), D_FF),
        "g_final": gain(ks[25], (D,)),
    }


def reference(x, c, positions, w_mod, b_mod, g_mix, g_ffn, w_a_down, g_q_lat, g_kv_lat,
              w_uq, w_ukv, w_oa, w_mod_kv, b_mod_kv, g_kv, w_kv_sb, w_q_sb, w_o_sb,
              w_ffn_gu, w_ffn_down, w_router, b_router, w_exp_gu, w_exp_down, g_final):
    silu_c = jax.nn.silu(c)
    k_sh = None
    v_sh = None
    for i in range(DEPTH):
        mod = silu_c @ w_mod[i] + b_mod[i]
        sh1, sc1, gt1, sh2, sc2, gt2 = jnp.split(mod, COND_MULT, axis=-1)
        h = modulate(rms_norm(x, g_mix[i]), sh1, sc1)
        if i < N_A_LAYERS:
            mix = mla_mixer(h, positions, w_a_down[i], g_q_lat[i], g_kv_lat[i],
                            w_uq[i], w_ukv[i], w_oa[i])
        else:
            j = i - N_A_LAYERS
            mix = stick_breaking_mixer(h, k_sh, v_sh, w_q_sb[j], w_o_sb[j])
        x = x + gt1[:, None, :] * mix
        h = modulate(rms_norm(x, g_ffn[i]), sh2, sc2)
        if i % 2 == 0:
            f = swiglu(h, w_ffn_gu[i // 2], w_ffn_down[i // 2])
        else:
            f = moe_swiglu(h, w_router[i // 2], b_router[i // 2],
                           w_exp_gu[i // 2], w_exp_down[i // 2])
        x = x + gt2[:, None, :] * f
        if i == N_A_LAYERS - 1:
            k_sh, v_sh = shared_kv(x, silu_c, w_mod_kv, b_mod_kv, g_kv, w_kv_sb)
    return rms_norm(x, g_final)
```

```python
import functools
import math

import jax
import jax.numpy as jnp
from jax import lax
from jax.experimental import pallas as pl
from jax.experimental.pallas import tpu as pltpu

F32 = jnp.float32
BF16 = jnp.bfloat16

EPS = 1e-6
MLA_HEADS = 8
QK_NOPE = 128
QK_ROPE = 64
V_DIM = 128
ROPE_THETA = 10000.0
SB_HEADS = 8
SB_HEAD_DIM = 128
TOP_K = 2

LANES = 128
QK_PAD = 256
VMEM_LIMIT = 48 * 1024 * 1024


def _cparams(sem):
    return pltpu.CompilerParams(dimension_semantics=sem, vmem_limit_bytes=VMEM_LIMIT)


def _rms(x, g):
    return x * lax.rsqrt(jnp.mean(x * x, axis=-1, keepdims=True) + EPS) * g


def _modulate(h, shift, scale):
    return h * (1.0 + scale) + shift


def _split_bf16(a):
    hi = a.astype(BF16)
    lo = (a - hi.astype(F32)).astype(BF16)
    return hi, lo


def _dot(a, b):
    return jnp.dot(a, b, preferred_element_type=F32)


def _dot_nt(a, b):
    return lax.dot_general(a, b, (((1,), (1,)), ((), ())), preferred_element_type=F32)


def _modvec_kernel(c_ref, w_ref, b_ref, o_ref):
    c = c_ref[...]
    sc = c * (1.0 / (1.0 + jnp.exp(-c)))
    a_hi, a_lo = _split_bf16(sc)
    w_hi, w_lo = _split_bf16(w_ref[...])
    o_ref[...] = _dot(a_hi, w_hi) + _dot(a_lo, w_hi) + _dot(a_hi, w_lo) + b_ref[...]


def _modvec(c, w, b, tn=512):
    bsz, d = c.shape
    n = w.shape[1]
    return pl.pallas_call(
        _modvec_kernel,
        out_shape=jax.ShapeDtypeStruct((bsz, n), F32),
        grid=(n // tn,),
        in_specs=[pl.BlockSpec((bsz, d), lambda j: (0, 0)),
                  pl.BlockSpec((d, tn), lambda j: (0, j)),
                  pl.BlockSpec((1, tn), lambda j: (0, j))],
        out_specs=pl.BlockSpec((bsz, tn), lambda j: (0, j)),
        compiler_params=_cparams(("arbitrary",)),
        name="modvec",
    )(c, w, b.reshape(1, n))


def _mod_spec(chunk, tiles_per_batch, d):
    return pl.BlockSpec((None, None, 1, d), lambda i, *_: (i // tiles_per_batch, chunk, 0, 0))


def _mla_proj_kernel(x_ref, pos_ref, sh_ref, sc_ref, g_ref, wd_ref, gq_ref, gkv_ref,
                     wqn_ref, wqr_ref, wkn_ref, wv_ref, inv_ref,
                     q_ref, k_ref, v_ref, *, q_lora, kv_lora):
    x = x_ref[...]
    h = _modulate(_rms(x, g_ref[...]), sh_ref[...], sc_ref[...]).astype(BF16)
    lat = _dot(h, wd_ref[...])
    c_q = _rms(lat[:, :q_lora], gq_ref[...]).astype(BF16)
    c_kv = _rms(lat[:, q_lora:q_lora + kv_lora], gkv_ref[...]).astype(BF16)
    k_rot = lat[:, q_lora + kv_lora:]

    half = QK_ROPE // 2
    ang = pos_ref[...].astype(F32) * inv_ref[...]
    cos = jnp.cos(ang)
    sin = jnp.sin(ang)
    lane = lax.broadcasted_iota(jnp.int32, ang.shape, 1)
    s_a = jnp.where(lane < half, -sin, 0.0)
    s_b = jnp.where((lane >= half) & (lane < 2 * half), sin, 0.0)

    def rope(r):
        return (r * cos + pltpu.roll(r, LANES - half, axis=1) * s_a
                + pltpu.roll(r, half, axis=1) * s_b)

    k_rot = rope(k_rot).astype(BF16)
    q_nope = _dot(c_q, wqn_ref[...])
    q_rope = _dot(c_q, wqr_ref[...])
    k_nope = _dot(c_kv, wkn_ref[...])
    v_ref[...] = _dot(c_kv, wv_ref[...]).astype(BF16)
    for hd in range(MLA_HEADS):
        a, b = hd * LANES, (hd + 1) * LANES
        q_ref[:, hd * QK_PAD:hd * QK_PAD + LANES] = q_nope[:, a:b].astype(BF16)
        q_ref[:, hd * QK_PAD + LANES:(hd + 1) * QK_PAD] = rope(q_rope[:, a:b]).astype(BF16)
        k_ref[:, hd * QK_PAD:hd * QK_PAD + LANES] = k_nope[:, a:b].astype(BF16)
        k_ref[:, hd * QK_PAD + LANES:(hd + 1) * QK_PAD] = k_rot


def _mla_proj(x, pos, mod, g_mix, wd, gq, gkv, wqn, wqr, wkn, wv, inv, *, seq, tm):
    n, d = x.shape
    tpb = seq // tm
    q_lora, kv_lora = gq.shape[1], gkv.shape[1]
    hq = MLA_HEADS * QK_PAD
    hv = MLA_HEADS * V_DIM
    full = lambda a: pl.BlockSpec(a.shape, lambda i: (0,) * a.ndim)
    row = lambda w: pl.BlockSpec((tm, w), lambda i: (i, 0))
    return pl.pallas_call(
        functools.partial(_mla_proj_kernel, q_lora=q_lora, kv_lora=kv_lora),
        out_shape=(jax.ShapeDtypeStruct((n, hq), BF16), jax.ShapeDtypeStruct((n, hq), BF16),
                   jax.ShapeDtypeStruct((n, hv), BF16)),
        grid=(n // tm,),
        in_specs=[row(d), row(1), _mod_spec(0, tpb, d), _mod_spec(1, tpb, d), full(g_mix), full(wd),
                  full(gq), full(gkv), full(wqn), full(wqr), full(wkn), full(wv), full(inv)],
        out_specs=(row(hq), row(hq), row(hv)),
        compiler_params=_cparams(("parallel",)),
        name="mla_proj",
    )(x, pos, mod, mod, g_mix, wd, gq, gkv, wqn, wqr, wkn, wv, inv)


def _mla_attn_kernel(q_ref, k_ref, v_ref, o_ref, m_ref, l_ref, acc_ref, *, t, scale):
    i = pl.program_id(2)
    q = q_ref[0]
    m_ref[...] = jnp.full(m_ref.shape, -jnp.inf, F32)
    l_ref[...] = jnp.zeros(l_ref.shape, F32)
    acc_ref[...] = jnp.zeros(acc_ref.shape, F32)

    def step(j, masked):
        start = pl.multiple_of(j * t, t)
        kb = k_ref[0, pl.ds(start, t), :]
        vb = v_ref[0, pl.ds(start, t), :]
        s = _dot_nt(q, kb) * scale
        if masked:
            r = lax.broadcasted_iota(jnp.int32, s.shape, 0)
            c = lax.broadcasted_iota(jnp.int32, s.shape, 1)
            s = jnp.where(c <= r, s, -jnp.inf)
        m_old = m_ref[...]
        m_new = jnp.maximum(m_old, jnp.max(s, axis=1, keepdims=True))
        p = jnp.exp(s - m_new)
        alpha = jnp.exp(m_old - m_new)
        l_ref[...] = alpha * l_ref[...] + jnp.sum(p, axis=1, keepdims=True)
        acc_ref[...] = alpha * acc_ref[...] + _dot(p.astype(BF16), vb)
        m_ref[...] = m_new

    def body(j, carry):
        step(j, False)
        return carry

    lax.fori_loop(0, i, body, 0)
    step(i, True)
    o_ref[0] = (acc_ref[...] / l_ref[...]).astype(o_ref.dtype)


def _mla_attn(q, k, v, *, t):
    bsz, seq, _ = q.shape
    scale = 1.0 / math.sqrt(QK_NOPE + QK_ROPE)
    return pl.pallas_call(
        functools.partial(_mla_attn_kernel, t=t, scale=scale),
        out_shape=jax.ShapeDtypeStruct((bsz, seq, MLA_HEADS * V_DIM), BF16),
        grid=(bsz, MLA_HEADS, seq // t),
        in_specs=[pl.BlockSpec((1, t, QK_PAD), lambda b, h, i: (b, i, h)),
                  pl.BlockSpec((1, seq, QK_PAD), lambda b, h, i: (b, 0, h)),
                  pl.BlockSpec((1, seq, V_DIM), lambda b, h, i: (b, 0, h))],
        out_specs=pl.BlockSpec((1, t, V_DIM), lambda b, h, i: (b, i, h)),
        scratch_shapes=[pltpu.VMEM((t, 1), F32), pltpu.VMEM((t, 1), F32), pltpu.VMEM((t, V_DIM), F32)],
        compiler_params=_cparams(("parallel", "parallel", "arbitrary")),
        name="mla_attn",
    )(q, k, v)


def _sb_attn_kernel(q_ref, k_ref, v_ref, o_ref, c_ref, acc_ref, *, t, scale):
    i = pl.program_id(2)
    q = q_ref[0]
    c_ref[...] = jnp.zeros(c_ref.shape, F32)
    acc_ref[...] = jnp.zeros(acc_ref.shape, F32)
    r = lax.broadcasted_iota(jnp.int32, (t, t), 0)
    c = lax.broadcasted_iota(jnp.int32, (t, t), 1)
    tri = jnp.where(r > c, 1.0, 0.0).astype(BF16)
    strict = c < r

    def step(j, masked):
        start = pl.multiple_of(j * t, t)
        kb = k_ref[0, pl.ds(start, t), :]
        vb = v_ref[0, pl.ds(start, t), :]
        z = _dot_nt(q, kb) * scale
        sp = jnp.log1p(jnp.exp(-jnp.abs(z)))
        log_beta = jnp.minimum(z, 0.0) - sp
        log_1m = log_beta - z
        if masked:
            log_1m = jnp.where(strict, log_1m, 0.0)
        hi, lo = _split_bf16(log_1m)
        suffix = _dot(hi, tri) + _dot(lo, tri) + c_ref[...]
        a = jnp.exp(log_beta + suffix)
        if masked:
            a = jnp.where(strict, a, 0.0)
        acc_ref[...] += _dot(a.astype(BF16), vb)
        c_ref[...] += jnp.sum(log_1m, axis=1, keepdims=True)

    step(i, True)

    def body(n, carry):
        step(i - 1 - n, False)
        return carry

    lax.fori_loop(0, i, body, 0)
    o_ref[0] = acc_ref[...].astype(o_ref.dtype)


def _sb_attn(q, kv, *, t):
    bsz, seq, _ = q.shape
    scale = 1.0 / math.sqrt(SB_HEAD_DIM)
    return pl.pallas_call(
        functools.partial(_sb_attn_kernel, t=t, scale=scale),
        out_shape=jax.ShapeDtypeStruct((bsz, seq, SB_HEADS * SB_HEAD_DIM), BF16),
        grid=(bsz, SB_HEADS, seq // t),
        in_specs=[pl.BlockSpec((1, t, SB_HEAD_DIM), lambda b, h, i: (b, i, h)),
                  pl.BlockSpec((1, seq, SB_HEAD_DIM), lambda b, h, i: (b, 0, h)),
                  pl.BlockSpec((1, seq, SB_HEAD_DIM), lambda b, h, i: (b, 0, SB_HEADS + h))],
        out_specs=pl.BlockSpec((1, t, SB_HEAD_DIM), lambda b, h, i: (b, i, h)),
        scratch_shapes=[pltpu.VMEM((t, 1), F32), pltpu.VMEM((t, SB_HEAD_DIM), F32)],
        compiler_params=_cparams(("parallel", "parallel", "arbitrary")),
        name="sb_attn",
    )(q, kv, kv)


def _top2_weights(logits, n_experts):
    lane = lax.broadcasted_iota(jnp.int32, logits.shape, 1).astype(F32)
    lg = jnp.where(lane < n_experts, logits, -jnp.inf)
    m1 = jnp.max(lg, axis=1, keepdims=True)
    i1 = jnp.min(jnp.where(lg == m1, lane, float(LANES)), axis=1, keepdims=True)
    lg2 = jnp.where(lane == i1, -jnp.inf, lg)
    m2 = jnp.max(lg2, axis=1, keepdims=True)
    i2 = jnp.min(jnp.where(lg2 == m2, lane, float(LANES)), axis=1, keepdims=True)
    e2 = jnp.exp(m2 - m1)
    den = 1.0 + e2
    return jnp.where(lane == i1, 1.0 / den, 0.0) + jnp.where(lane == i2, e2 / den, 0.0)


def _out_proj_kernel(*refs, n_experts):
    if n_experts:
        (o_ref, w_ref, x_ref, gt_ref, g_ref, sh_ref, sc_ref, wr_hi_ref, wr_lo_ref, br_ref,
         xo_ref, h_ref, comb_ref) = refs
    else:
        o_ref, w_ref, x_ref, gt_ref, g_ref, sh_ref, sc_ref, xo_ref, h_ref = refs
    x = x_ref[...] + gt_ref[...] * _dot(o_ref[...], w_ref[...])
    xo_ref[...] = x
    h = _modulate(_rms(x, g_ref[...]), sh_ref[...], sc_ref[...])
    h_hi = h.astype(BF16)
    h_ref[...] = h_hi
    if n_experts:
        h_lo = (h - h_hi.astype(F32)).astype(BF16)
        logits = (_dot(h_hi, wr_hi_ref[...]) + _dot(h_lo, wr_hi_ref[...]) + _dot(h_hi, wr_lo_ref[...])
                  + br_ref[...])
        comb_ref[...] = _top2_weights(logits, n_experts)


def _out_proj(o, w, x, mod, g, router=None, *, seq, tm):
    n, d = x.shape
    tpb = seq // tm
    full = lambda a: pl.BlockSpec(a.shape, lambda i: (0,) * a.ndim)
    row = lambda wd: pl.BlockSpec((tm, wd), lambda i: (i, 0))
    in_specs = [row(o.shape[1]), full(w), row(d), _mod_spec(2, tpb, d), full(g),
                _mod_spec(3, tpb, d), _mod_spec(4, tpb, d)]
    args = [o, w, x, mod, g, mod, mod]
    out_shape = [jax.ShapeDtypeStruct((n, d), F32), jax.ShapeDtypeStruct((n, d), BF16)]
    out_specs = [row(d), row(d)]
    n_experts = 0
    if router is not None:
        wr_hi, wr_lo, br, n_experts = router
        in_specs += [full(wr_hi), full(wr_lo), full(br)]
        args += [wr_hi, wr_lo, br]
        out_shape.append(jax.ShapeDtypeStruct((n, LANES), F32))
        out_specs.append(row(LANES))
    return pl.pallas_call(
        functools.partial(_out_proj_kernel, n_experts=n_experts),
        out_shape=tuple(out_shape),
        grid=(n // tm,),
        in_specs=in_specs,
        out_specs=tuple(out_specs),
        compiler_params=_cparams(("parallel",)),
        name="out_proj_router" if n_experts else "out_proj",
    )(*args)


def _silu(g):
    return g * (1.0 / (1.0 + jnp.exp(-g)))


def _ffn_kernel(h_ref, wg_ref, wu_ref, wd_ref, x_ref, gt_ref, gkv_ref, shk_ref, sck_ref,
                gm_ref, shm_ref, scm_ref, xo_ref, hk_ref, hm_ref, acc_ref):
    f = pl.program_id(1)

    @pl.when(f == 0)
    def _():
        acc_ref[...] = jnp.zeros(acc_ref.shape, F32)

    h = h_ref[...]
    a = _silu(_dot(h, wg_ref[...])) * _dot(h, wu_ref[...])
    acc_ref[...] += _dot(a.astype(BF16), wd_ref[...])

    @pl.when(f == pl.num_programs(1) - 1)
    def _():
        x = x_ref[...] + gt_ref[...] * acc_ref[...]
        xo_ref[...] = x
        hk_ref[...] = _modulate(_rms(x, gkv_ref[...]), shk_ref[...], sck_ref[...]).astype(BF16)
        hm_ref[...] = _modulate(_rms(x, gm_ref[...]), shm_ref[...], scm_ref[...]).astype(BF16)


def _ffn(h, w_gu, w_down, x, mod0, g_kv, mod_kv, g_mix1, mod1, *, seq, tm, tf):
    n, d = x.shape
    d_ff = w_down.shape[0]
    nf = d_ff // tf
    tpb = seq // tm
    full = lambda a: pl.BlockSpec(a.shape, lambda i, f: (0,) * a.ndim)
    row = lambda: pl.BlockSpec((tm, d), lambda i, f: (i, 0))
    return pl.pallas_call(
        _ffn_kernel,
        out_shape=(jax.ShapeDtypeStruct((n, d), F32), jax.ShapeDtypeStruct((n, d), BF16),
                   jax.ShapeDtypeStruct((n, d), BF16)),
        grid=(n // tm, nf),
        in_specs=[row(),
                  pl.BlockSpec((d, tf), lambda i, f: (0, f)),
                  pl.BlockSpec((d, tf), lambda i, f: (0, nf + f)),
                  pl.BlockSpec((tf, d), lambda i, f: (f, 0)),
                  row(), _mod_spec(5, tpb, d),
                  full(g_kv), _mod_spec(0, tpb, d), _mod_spec(1, tpb, d),
                  full(g_mix1), _mod_spec(0, tpb, d), _mod_spec(1, tpb, d)],
        out_specs=(row(), row(), row()),
        scratch_shapes=[pltpu.VMEM((tm, d), F32)],
        compiler_params=_cparams(("parallel", "arbitrary")),
        name="ffn_dense",
    )(h, w_gu, w_gu, w_down, x, mod0, g_kv, mod_kv, mod_kv, g_mix1, mod1, mod1)


def _linear_kernel(x_ref, w_ref, o_ref):
    o_ref[...] = _dot(x_ref[...], w_ref[...]).astype(o_ref.dtype)


def _linear(x, w, *, tm, tn, name):
    n, k = x.shape
    m = w.shape[1]
    return pl.pallas_call(
        _linear_kernel,
        out_shape=jax.ShapeDtypeStruct((n, m), BF16),
        grid=(m // tn, n // tm),
        in_specs=[pl.BlockSpec((tm, k), lambda j, i: (i, 0)), pl.BlockSpec((k, tn), lambda j, i: (0, j))],
        out_specs=pl.BlockSpec((tm, tn), lambda j, i: (i, j)),
        compiler_params=_cparams(("parallel", "parallel")),
        name=name,
    )(x, w)


def _moe_kernel(h_ref, comb_ref, wg_ref, wu_ref, wd_ref, x_ref, gt_ref, gf_ref, o_ref, acc_ref, tot_ref):
    e = pl.program_id(1)
    f = pl.program_id(2)
    last_f = f == pl.num_programs(2) - 1

    @pl.when((e == 0) & (f == 0))
    def _():
        tot_ref[...] = jnp.zeros(tot_ref.shape, F32)

    @pl.when(f == 0)
    def _():
        acc_ref[...] = jnp.zeros(acc_ref.shape, F32)

    h = h_ref[...]
    a = _silu(_dot(h, wg_ref[0])) * _dot(h, wu_ref[0])
    acc_ref[...] += _dot(a.astype(BF16), wd_ref[0])

    @pl.when(last_f)
    def _():
        comb = comb_ref[...]
        lane = lax.broadcasted_iota(jnp.int32, comb.shape, 1)
        w_e = jnp.sum(jnp.where(lane == e, comb, 0.0), axis=1, keepdims=True)
        tot_ref[...] += w_e * acc_ref[...]

    @pl.when(last_f & (e == pl.num_programs(1) - 1))
    def _():
        x = x_ref[...] + gt_ref[...] * tot_ref[...]
        o_ref[...] = _rms(x, gf_ref[...])


def _moe(h, comb, w_gu, w_down, x, mod1, g_final, *, seq, tm, tf):
    n, d = x.shape
    n_exp, d_ff, _ = w_down.shape
    nf = d_ff // tf
    tpb = seq // tm
    row = lambda w: pl.BlockSpec((tm, w), lambda i, e, f: (i, 0))
    return pl.pallas_call(
        _moe_kernel,
        out_shape=jax.ShapeDtypeStruct((n, d), F32),
        grid=(n // tm, n_exp, nf),
        in_specs=[row(d), row(LANES),
                  pl.BlockSpec((1, d, tf), lambda i, e, f: (e, 0, f)),
                  pl.BlockSpec((1, d, tf), lambda i, e, f: (e, 0, nf + f)),
                  pl.BlockSpec((1, tf, d), lambda i, e, f: (e, f, 0)),
                  row(d), _mod_spec(5, tpb, d),
                  pl.BlockSpec(g_final.shape, lambda i, e, f: (0, 0))],
        out_specs=row(d),
        scratch_shapes=[pltpu.VMEM((tm, d), F32), pltpu.VMEM((tm, d), F32)],
        compiler_params=_cparams(("parallel", "arbitrary", "arbitrary")),
        name="moe_ffn",
    )(h, comb, w_gu, w_gu, w_down, x, mod1, g_final)


def _pad_last(a, width):
    return jnp.pad(a, [(0, 0)] * (a.ndim - 1) + [(0, width - a.shape[-1])])


def kernel(x, c, positions, w_mod, b_mod, g_mix, g_ffn, w_a_down, g_q_lat, g_kv_lat, w_uq, w_ukv, w_oa,
           w_mod_kv, b_mod_kv, g_kv, w_kv_sb, w_q_sb, w_o_sb, w_ffn_gu, w_ffn_down, w_router, b_router,
           w_exp_gu, w_exp_down, g_final):
    bsz, seq, d = x.shape
    n = bsz * seq
    q_lora, kv_lora = g_q_lat.shape[1], g_kv_lat.shape[1]
    n_exp = w_router.shape[-1]
    d_ff = w_ffn_down.shape[1]
    tm = min(512, seq)
    t_attn = min(256, seq)
    tf = 512 if d_ff % 512 == 0 else d_ff

    mod0 = _modvec(c, w_mod[0], b_mod[0]).reshape(bsz, 6, 1, d)
    mod1 = _modvec(c, w_mod[1], b_mod[1]).reshape(bsz, 6, 1, d)
    mod_kv = _modvec(c, w_mod_kv, b_mod_kv).reshape(bsz, 2, 1, d)

    lat_w = q_lora + kv_lora + LANES
    wd = _pad_last(w_a_down[0], lat_w).astype(BF16)
    wq = w_uq[0].reshape(q_lora, MLA_HEADS, QK_NOPE + QK_ROPE)
    wqn = wq[:, :, :QK_NOPE].reshape(q_lora, MLA_HEADS * QK_NOPE).astype(BF16)
    wqr = _pad_last(wq[:, :, QK_NOPE:], LANES).reshape(q_lora, MLA_HEADS * LANES).astype(BF16)
    wkv = w_ukv[0].reshape(kv_lora, MLA_HEADS, QK_NOPE + V_DIM)
    wkn = wkv[:, :, :QK_NOPE].reshape(kv_lora, MLA_HEADS * QK_NOPE).astype(BF16)
    wv = wkv[:, :, QK_NOPE:].reshape(kv_lora, MLA_HEADS * V_DIM).astype(BF16)
    half = QK_ROPE // 2
    inv = ROPE_THETA ** (-jnp.arange(half, dtype=F32) / half)
    inv = _pad_last(jnp.concatenate([inv, inv]), LANES).reshape(1, LANES)
    wr = _pad_last(w_router[0], LANES)
    wr_hi = wr.astype(BF16)
    wr_lo = (wr - wr_hi.astype(F32)).astype(BF16)
    br = _pad_last(b_router[0], LANES).reshape(1, LANES)

    xf = x.reshape(n, d)
    pos = positions.reshape(n, 1)
    row1 = lambda a: a.reshape(1, -1)

    q, k, v = _mla_proj(xf, pos, mod0, row1(g_mix[0]), wd, row1(g_q_lat[0]), row1(g_kv_lat[0]),
                        wqn, wqr, wkn, wv, inv, seq=seq, tm=tm)
    o = _mla_attn(q.reshape(bsz, seq, -1), k.reshape(bsz, seq, -1), v.reshape(bsz, seq, -1), t=t_attn)
    x1, h = _out_proj(o.reshape(n, -1), w_oa[0].astype(BF16), xf, mod0, row1(g_ffn[0]), seq=seq, tm=tm)
    x2, hk, hm = _ffn(h, w_ffn_gu[0].astype(BF16), w_ffn_down[0].astype(BF16), x1, mod0,
                      row1(g_kv), mod_kv, row1(g_mix[1]), mod1, seq=seq, tm=tm, tf=tf)
    kv = _linear(hk, w_kv_sb.astype(BF16), tm=tm, tn=1024, name="kv_proj")
    qs = _linear(hm, w_q_sb[0].astype(BF16), tm=tm, tn=1024, name="q_proj")
    o = _sb_attn(qs.reshape(bsz, seq, -1), kv.reshape(bsz, seq, -1), t=t_attn)
    x3, h, comb = _out_proj(o.reshape(n, -1), w_o_sb[0].astype(BF16), x2, mod1, row1(g_ffn[1]),
                            router=(wr_hi, wr_lo, br, n_exp), seq=seq, tm=tm)
    out = _moe(h, comb, w_exp_gu[0].astype(BF16), w_exp_down[0].astype(BF16), x3, mod1, row1(g_final),
               seq=seq, tm=tm, tf=tf)
    return out.reshape(bsz, seq, d)
```

```python
import functools
import math

import jax
import jax.numpy as jnp
from jax import lax
from jax.experimental import pallas as pl
from jax.experimental.pallas import tpu as pltpu

F32 = jnp.float32
BF16 = jnp.bfloat16

EPS = 1e-6
MLA_HEADS = 8
QK_NOPE = 128
QK_ROPE = 64
V_DIM = 128
ROPE_THETA = 10000.0
SB_HEADS = 8
SB_HEAD_DIM = 128
TOP_K = 2

LANES = 128
QK_PAD = 256
VMEM_LIMIT = 48 * 1024 * 1024
TOKEN_TILE = 512
MLA_TILE = 512
SB_TILE = 512


def _cparams(sem):
    return pltpu.CompilerParams(dimension_semantics=sem, vmem_limit_bytes=VMEM_LIMIT)


def _rms(x, g):
    return x * lax.rsqrt(jnp.mean(x * x, axis=-1, keepdims=True) + EPS) * g


def _modulate(h, shift, scale):
    return h * (1.0 + scale) + shift


def _split_bf16(a):
    hi = a.astype(BF16)
    lo = (a - hi.astype(F32)).astype(BF16)
    return hi, lo


def _dot(a, b):
    return jnp.dot(a, b, preferred_element_type=F32)


def _dot_nt(a, b):
    return lax.dot_general(a, b, (((1,), (1,)), ((), ())), preferred_element_type=F32)


def _modvec_kernel(c_ref, w_ref, b_ref, o_ref):
    c = c_ref[...]
    sc = c * (1.0 / (1.0 + jnp.exp(-c)))
    a_hi, a_lo = _split_bf16(sc)
    w_hi, w_lo = _split_bf16(w_ref[...])
    o_ref[...] = _dot(a_hi, w_hi) + _dot(a_lo, w_hi) + _dot(a_hi, w_lo) + b_ref[...]


def _modvec(c, w, b, tn=512):
    bsz, d = c.shape
    n = w.shape[1]
    return pl.pallas_call(
        _modvec_kernel,
        out_shape=jax.ShapeDtypeStruct((bsz, n), F32),
        grid=(n // tn,),
        in_specs=[pl.BlockSpec((bsz, d), lambda j: (0, 0)),
                  pl.BlockSpec((d, tn), lambda j: (0, j)),
                  pl.BlockSpec((1, tn), lambda j: (0, j))],
        out_specs=pl.BlockSpec((bsz, tn), lambda j: (0, j)),
        compiler_params=_cparams(("arbitrary",)),
        name="modvec",
    )(c, w, b.reshape(1, n))


def _mod_spec(chunk, tiles_per_batch, d):
    return pl.BlockSpec((None, None, 1, d), lambda i, *_: (i // tiles_per_batch, chunk, 0, 0))


def _mla_proj_kernel(x_ref, pos_ref, sh_ref, sc_ref, g_ref, wd_ref, gq_ref, gkv_ref,
                     wqn_ref, wqr_ref, wkn_ref, wv_ref, inv_ref,
                     q_ref, k_ref, v_ref, *, q_lora, kv_lora):
    x = x_ref[...]
    h = _modulate(_rms(x, g_ref[...]), sh_ref[...], sc_ref[...]).astype(BF16)
    lat = _dot(h, wd_ref[...])
    c_q = _rms(lat[:, :q_lora], gq_ref[...]).astype(BF16)
    c_kv = _rms(lat[:, q_lora:q_lora + kv_lora], gkv_ref[...]).astype(BF16)
    k_rot = lat[:, q_lora + kv_lora:]

    half = QK_ROPE // 2
    ang = pos_ref[...].astype(F32) * inv_ref[...]
    cos = jnp.cos(ang)
    sin = jnp.sin(ang)
    lane = lax.broadcasted_iota(jnp.int32, ang.shape, 1)
    s_a = jnp.where(lane < half, -sin, 0.0)
    s_b = jnp.where((lane >= half) & (lane < 2 * half), sin, 0.0)

    def rope(r):
        return (r * cos + pltpu.roll(r, LANES - half, axis=1) * s_a
                + pltpu.roll(r, half, axis=1) * s_b)

    k_rot = rope(k_rot).astype(BF16)
    q_nope = _dot(c_q, wqn_ref[...])
    q_rope = _dot(c_q, wqr_ref[...])
    k_nope = _dot(c_kv, wkn_ref[...])
    v_ref[...] = _dot(c_kv, wv_ref[...]).astype(BF16)
    for hd in range(MLA_HEADS):
        a, b = hd * LANES, (hd + 1) * LANES
        q_ref[:, hd * QK_PAD:hd * QK_PAD + LANES] = q_nope[:, a:b].astype(BF16)
        q_ref[:, hd * QK_PAD + LANES:(hd + 1) * QK_PAD] = rope(q_rope[:, a:b]).astype(BF16)
        k_ref[:, hd * QK_PAD:hd * QK_PAD + LANES] = k_nope[:, a:b].astype(BF16)
        k_ref[:, hd * QK_PAD + LANES:(hd + 1) * QK_PAD] = k_rot


def _mla_proj(x, pos, mod, g_mix, wd, gq, gkv, wqn, wqr, wkn, wv, inv, *, seq, tm):
    n, d = x.shape
    tpb = seq // tm
    q_lora, kv_lora = gq.shape[1], gkv.shape[1]
    hq = MLA_HEADS * QK_PAD
    hv = MLA_HEADS * V_DIM
    full = lambda a: pl.BlockSpec(a.shape, lambda i: (0,) * a.ndim)
    row = lambda w: pl.BlockSpec((tm, w), lambda i: (i, 0))
    return pl.pallas_call(
        functools.partial(_mla_proj_kernel, q_lora=q_lora, kv_lora=kv_lora),
        out_shape=(jax.ShapeDtypeStruct((n, hq), BF16), jax.ShapeDtypeStruct((n, hq), BF16),
                   jax.ShapeDtypeStruct((n, hv), BF16)),
        grid=(n // tm,),
        in_specs=[row(d), row(1), _mod_spec(0, tpb, d), _mod_spec(1, tpb, d), full(g_mix), full(wd),
                  full(gq), full(gkv), full(wqn), full(wqr), full(wkn), full(wv), full(inv)],
        out_specs=(row(hq), row(hq), row(hv)),
        compiler_params=_cparams(("parallel",)),
        name="mla_proj",
    )(x, pos, mod, mod, g_mix, wd, gq, gkv, wqn, wqr, wkn, wv, inv)


def _lane_chunks(a):
    return [a[:, c * LANES:(c + 1) * LANES] for c in range(a.shape[1] // LANES)]


def _mla_attn_kernel(q_ref, k_ref, v_ref, o_ref, s_ref, m_ref, acc_ref, *, t, scale, heads):
    i = pl.program_id(2)

    def lane_max(s):
        m = None
        for sc in _lane_chunks(s):
            m = sc if m is None else jnp.maximum(m, sc)
        return m

    def scores(h, j):
        start = pl.multiple_of(j * t, t)
        hs = slice(h * QK_PAD, (h + 1) * QK_PAD)
        return _dot_nt(q_ref[0, :, hs], k_ref[0, pl.ds(start, t), hs])

    r = lax.broadcasted_iota(jnp.int32, (t, t), 0)
    c = lax.broadcasted_iota(jnp.int32, (t, t), 1)
    for h in range(heads):
        s = jnp.where(c <= r, scores(h, i), -jnp.inf)
        s_ref[h, i] = s
        m_ref[h] = lane_max(s)

    def pass1(j, carry):
        for h in range(heads):
            s = scores(h, j)
            s_ref[h, j] = s
            m_ref[h] = jnp.maximum(m_ref[h], lane_max(s))
        return carry

    lax.fori_loop(0, i, pass1, 0)
    for h in range(heads):
        m_ref[h] = jnp.broadcast_to(jnp.max(m_ref[h], axis=1, keepdims=True), (t, LANES))
    acc_ref[...] = jnp.zeros(acc_ref.shape, F32)
    ones = jnp.ones((t, LANES), BF16)
    cst = scale * math.log2(math.e)

    def pass2(j, carry):
        start = pl.multiple_of(j * t, t)
        for h in range(heads):
            m = m_ref[h]
            p = jnp.concatenate([jnp.exp2((sc - m) * cst).astype(BF16) for sc in _lane_chunks(s_ref[h, j])],
                                axis=1)
            v_ext = jnp.concatenate([v_ref[0, pl.ds(start, t), h * V_DIM:(h + 1) * V_DIM], ones], axis=1)
            acc_ref[h] += _dot(p, v_ext)
        return carry

    lax.fori_loop(0, i + 1, pass2, 0)
    for h in range(heads):
        acc = acc_ref[h]
        o_ref[0, :, h * V_DIM:(h + 1) * V_DIM] = (acc[:, :V_DIM] / acc[:, V_DIM:]).astype(o_ref.dtype)


def _mla_attn(q, k, v, *, t, heads=2):
    bsz, seq, _ = q.shape
    scale = 1.0 / math.sqrt(QK_NOPE + QK_ROPE)
    wq, wv = heads * QK_PAD, heads * V_DIM
    return pl.pallas_call(
        functools.partial(_mla_attn_kernel, t=t, scale=scale, heads=heads),
        out_shape=jax.ShapeDtypeStruct((bsz, seq, MLA_HEADS * V_DIM), BF16),
        grid=(bsz, MLA_HEADS // heads, seq // t),
        in_specs=[pl.BlockSpec((1, t, wq), lambda b, g, i: (b, i, g)),
                  pl.BlockSpec((1, seq, wq), lambda b, g, i: (b, 0, g)),
                  pl.BlockSpec((1, seq, wv), lambda b, g, i: (b, 0, g))],
        out_specs=pl.BlockSpec((1, t, wv), lambda b, g, i: (b, i, g)),
        scratch_shapes=[pltpu.VMEM((heads, seq // t, t, t), F32), pltpu.VMEM((heads, t, LANES), F32),
                        pltpu.VMEM((heads, t, V_DIM + LANES), F32)],
        compiler_params=_cparams(("parallel", "parallel", "arbitrary")),
        name="mla_attn",
    )(q, k, v)


def _sb_attn_kernel(q_ref, k_ref, v_ref, w_ref, o_ref, c_ref, acc_ref, *, t, scale, heads):
    i = pl.program_id(2)
    nl = t // LANES
    d = SB_HEAD_DIM
    c_ref[...] = jnp.zeros(c_ref.shape, F32)
    acc_ref[...] = jnp.zeros(acc_ref.shape, F32)
    r = lax.broadcasted_iota(jnp.int32, (t, t), 0)
    c = lax.broadcasted_iota(jnp.int32, (t, t), 1)
    strict = c < r
    cst = scale * math.log2(math.e)

    def step(j, masked):
        start = pl.multiple_of(j * t, t)
        for h in range(heads):
            hs = slice(h * d, (h + 1) * d)
            zz = _dot_nt(q_ref[0, :, hs], k_ref[0, pl.ds(start, t), hs]) * cst
            sp = jnp.log2(1.0 + jnp.exp2(-jnp.abs(zz)))
            log_beta = jnp.minimum(zz, 0.0) - sp
            log_1m = log_beta - zz
            if masked:
                log_1m = jnp.where(strict, log_1m, 0.0)
            hi, lo = _split_bf16(log_1m)
            carry = c_ref[h]
            a_chunks = [None] * nl
            for cc in reversed(range(nl)):
                cs = slice(cc * LANES, (cc + 1) * LANES)
                y = _dot(jnp.concatenate([hi[:, cs], lo[:, cs]], axis=1), w_ref[...])
                a = jnp.exp2(log_beta[:, cs] + y[:, :LANES] + carry)
                carry = carry + y[:, LANES:]
                if masked:
                    a = jnp.where(strict[:, cs], a, 0.0)
                a_chunks[cc] = a.astype(BF16)
            c_ref[h] = carry
            acc_ref[h] += _dot(jnp.concatenate(a_chunks, axis=1), v_ref[0, pl.ds(start, t), hs])

    step(i, True)

    def body(n, carry):
        step(i - 1 - n, False)
        return carry

    lax.fori_loop(0, i, body, 0)
    for h in range(heads):
        o_ref[0, :, h * d:(h + 1) * d] = acc_ref[h].astype(o_ref.dtype)


def _sb_attn(q, kv, *, t, heads=2):
    bsz, seq, _ = q.shape
    scale = 1.0 / math.sqrt(SB_HEAD_DIM)
    w = heads * SB_HEAD_DIM
    groups = SB_HEADS // heads
    tri = (jnp.arange(LANES)[:, None] > jnp.arange(LANES)[None, :]).astype(BF16)
    half = jnp.concatenate([tri, jnp.ones((LANES, LANES), BF16)], axis=1)
    w_sum = jnp.concatenate([half, half], axis=0)
    return pl.pallas_call(
        functools.partial(_sb_attn_kernel, t=t, scale=scale, heads=heads),
        out_shape=jax.ShapeDtypeStruct((bsz, seq, SB_HEADS * SB_HEAD_DIM), BF16),
        grid=(bsz, groups, seq // t),
        in_specs=[pl.BlockSpec((1, t, w), lambda b, g, i: (b, i, g)),
                  pl.BlockSpec((1, seq, w), lambda b, g, i: (b, 0, g)),
                  pl.BlockSpec((1, seq, w), lambda b, g, i: (b, 0, groups + g)),
                  pl.BlockSpec(w_sum.shape, lambda b, g, i: (0, 0))],
        out_specs=pl.BlockSpec((1, t, w), lambda b, g, i: (b, i, g)),
        scratch_shapes=[pltpu.VMEM((heads, t, LANES), F32), pltpu.VMEM((heads, t, SB_HEAD_DIM), F32)],
        compiler_params=_cparams(("parallel", "parallel", "arbitrary")),
        name="sb_attn",
    )(q, kv, kv, w_sum)


def _out_proj_kernel(*refs, n_experts):
    if n_experts:
        (o_ref, w_ref, x_ref, gt_ref, g_ref, sh_ref, sc_ref, wr_hi_ref, wr_lo_ref, br_ref, tri_ref,
         xo_ref, h_ref, rt_ref, cnt_ref) = refs
    else:
        o_ref, w_ref, x_ref, gt_ref, g_ref, sh_ref, sc_ref, xo_ref, h_ref = refs
    x = x_ref[...] + gt_ref[...] * _dot(o_ref[...], w_ref[...])
    xo_ref[...] = x
    h = _modulate(_rms(x, g_ref[...]), sh_ref[...], sc_ref[...])
    h_hi = h.astype(BF16)
    if not n_experts:
        h_ref[...] = h_hi
        return
    for cc in range(h.shape[1] // LANES):
        h_ref[:, cc, :] = h[:, cc * LANES:(cc + 1) * LANES]
    h_lo = (h - h_hi.astype(F32)).astype(BF16)
    logits = (_dot(h_hi, wr_hi_ref[...]) + _dot(h_lo, wr_hi_ref[...]) + _dot(h_hi, wr_lo_ref[...])
              + br_ref[...])
    lane = lax.broadcasted_iota(jnp.int32, logits.shape, 1).astype(F32)
    lg = jnp.where(lane < n_experts, logits, -jnp.inf)
    m1 = jnp.max(lg, axis=1, keepdims=True)
    i1 = jnp.min(jnp.where(lg == m1, lane, float(LANES)), axis=1, keepdims=True)
    lg2 = jnp.where(lane == i1, -jnp.inf, lg)
    m2 = jnp.max(lg2, axis=1, keepdims=True)
    i2 = jnp.min(jnp.where(lg2 == m2, lane, float(LANES)), axis=1, keepdims=True)
    e2 = jnp.exp(m2 - m1)
    den = 1.0 + e2
    sel = jnp.where((lane == i1) | (lane == i2), 1.0, 0.0)
    prefix = _dot(tri_ref[...], sel.astype(BF16))
    r1 = jnp.sum(jnp.where(lane == i1, prefix, 0.0), axis=1, keepdims=True)
    r2 = jnp.sum(jnp.where(lane == i2, prefix, 0.0), axis=1, keepdims=True)
    cnt_ref[...] = jnp.sum(sel, axis=0, keepdims=True)
    rt = jnp.zeros_like(logits)
    for k, val in enumerate((i1, i2, 1.0 / den, e2 / den, r1, r2)):
        rt = jnp.where(lane == k, val, rt)
    rt_ref[...] = rt


def _out_proj(o, w, x, mod, g, router=None, *, seq, tm):
    n, d = x.shape
    tpb = seq // tm
    full = lambda a: pl.BlockSpec(a.shape, lambda i: (0,) * a.ndim)
    row = lambda wd: pl.BlockSpec((tm, wd), lambda i: (i, 0))
    in_specs = [row(o.shape[1]), full(w), row(d), _mod_spec(2, tpb, d), full(g),
                _mod_spec(3, tpb, d), _mod_spec(4, tpb, d)]
    args = [o, w, x, mod, g, mod, mod]
    out_shape = [jax.ShapeDtypeStruct((n, d), F32), jax.ShapeDtypeStruct((n, d), BF16)]
    out_specs = [row(d), row(d)]
    n_experts = 0
    if router is not None:
        wr_hi, wr_lo, br, n_experts = router
        tri = (jnp.arange(tm)[:, None] > jnp.arange(tm)[None, :]).astype(BF16)
        in_specs += [full(wr_hi), full(wr_lo), full(br), full(tri)]
        args += [wr_hi, wr_lo, br, tri]
        out_shape[1] = jax.ShapeDtypeStruct((n, d // LANES, LANES), F32)
        out_specs[1] = pl.BlockSpec((tm, d // LANES, LANES), lambda i: (i, 0, 0))
        out_shape += [jax.ShapeDtypeStruct((n, LANES), F32), jax.ShapeDtypeStruct((n // tm, 1, LANES), F32)]
        out_specs += [row(LANES), pl.BlockSpec((None, 1, LANES), lambda i: (i, 0, 0))]
    return pl.pallas_call(
        functools.partial(_out_proj_kernel, n_experts=n_experts),
        out_shape=tuple(out_shape),
        grid=(n // tm,),
        in_specs=in_specs,
        out_specs=tuple(out_specs),
        compiler_params=_cparams(("parallel",)),
        name="out_proj_router" if n_experts else "out_proj",
    )(*args)


def _silu(g):
    return g * (1.0 / (1.0 + jnp.exp(-g)))


def _ffn_kernel(h_ref, wg_ref, wu_ref, wd_ref, x_ref, gt_ref, gkv_ref, shk_ref, sck_ref,
                gm_ref, shm_ref, scm_ref, xo_ref, hk_ref, hm_ref, acc_ref):
    f = pl.program_id(1)

    @pl.when(f == 0)
    def _():
        acc_ref[...] = jnp.zeros(acc_ref.shape, F32)

    h = h_ref[...]
    a = _silu(_dot(h, wg_ref[...])) * _dot(h, wu_ref[...])
    acc_ref[...] += _dot(a.astype(BF16), wd_ref[...])

    @pl.when(f == pl.num_programs(1) - 1)
    def _():
        x = x_ref[...] + gt_ref[...] * acc_ref[...]
        xo_ref[...] = x
        hk_ref[...] = _modulate(_rms(x, gkv_ref[...]), shk_ref[...], sck_ref[...]).astype(BF16)
        hm_ref[...] = _modulate(_rms(x, gm_ref[...]), shm_ref[...], scm_ref[...]).astype(BF16)


def _ffn(h, w_gu, w_down, x, mod0, g_kv, mod_kv, g_mix1, mod1, *, seq, tm, tf):
    n, d = x.shape
    d_ff = w_down.shape[0]
    nf = d_ff // tf
    tpb = seq // tm
    full = lambda a: pl.BlockSpec(a.shape, lambda i, f: (0,) * a.ndim)
    row = lambda: pl.BlockSpec((tm, d), lambda i, f: (i, 0))
    return pl.pallas_call(
        _ffn_kernel,
        out_shape=(jax.ShapeDtypeStruct((n, d), F32), jax.ShapeDtypeStruct((n, d), BF16),
                   jax.ShapeDtypeStruct((n, d), BF16)),
        grid=(n // tm, nf),
        in_specs=[row(),
                  pl.BlockSpec((d, tf), lambda i, f: (0, f)),
                  pl.BlockSpec((d, tf), lambda i, f: (0, nf + f)),
                  pl.BlockSpec((tf, d), lambda i, f: (f, 0)),
                  row(), _mod_spec(5, tpb, d),
                  full(g_kv), _mod_spec(0, tpb, d), _mod_spec(1, tpb, d),
                  full(g_mix1), _mod_spec(0, tpb, d), _mod_spec(1, tpb, d)],
        out_specs=(row(), row(), row()),
        scratch_shapes=[pltpu.VMEM((tm, d), F32)],
        compiler_params=_cparams(("parallel", "arbitrary")),
        name="ffn_dense",
    )(h, w_gu, w_gu, w_down, x, mod0, g_kv, mod_kv, mod_kv, g_mix1, mod1, mod1)


def _linear_kernel(x_ref, w_ref, o_ref):
    o_ref[...] = _dot(x_ref[...], w_ref[...]).astype(o_ref.dtype)


def _linear(x, w, *, tm, tn, name):
    n, k = x.shape
    m = w.shape[1]
    return pl.pallas_call(
        _linear_kernel,
        out_shape=jax.ShapeDtypeStruct((n, m), BF16),
        grid=(m // tn, n // tm),
        in_specs=[pl.BlockSpec((tm, k), lambda j, i: (i, 0)), pl.BlockSpec((k, tn), lambda j, i: (0, j))],
        out_specs=pl.BlockSpec((tm, tn), lambda j, i: (i, j)),
        compiler_params=_cparams(("parallel", "parallel")),
        name=name,
    )(x, w)


def _route_plan(rt, counts, n_exp, tm, n_tiles):
    cnt = counts[:, 0, :n_exp].astype(jnp.int32)
    sizes = jnp.sum(cnt, axis=0)
    padded = (sizes + tm - 1) // tm * tm
    ends = jnp.cumsum(padded)
    tile_base = (ends - padded)[None, :] + jnp.cumsum(cnt, axis=0) - cnt
    base = jnp.repeat(tile_base, tm, axis=0)
    experts = jnp.arange(n_exp, dtype=jnp.int32)[None, :]
    dest = []
    for k in range(TOP_K):
        e_k = rt[:, k].astype(jnp.int32)
        r_k = rt[:, 2 * TOP_K + k].astype(jnp.int32)
        dest.append(jnp.sum(jnp.where(e_k[:, None] == experts, base, 0), axis=1) + r_k)
    dest = jnp.stack(dest, axis=1).reshape(-1)
    tile_start = jnp.arange(n_tiles, dtype=jnp.int32) * tm
    tile_expert = jnp.minimum(jnp.sum(tile_start[:, None] >= ends[None, :], axis=1), n_exp - 1)
    n_used = (ends[-1] // tm).reshape(1)
    return dest, tile_expert.astype(jnp.int32), n_used.astype(jnp.int32)


def _dispatch_kernel(dest_ref, h_ref, xg_in_ref, xg_ref, sem, *, tm):
    del xg_in_ref
    base = pl.program_id(0) * (TOP_K * tm)

    def issue(r, carry):
        for k in range(TOP_K):
            d = dest_ref[base + TOP_K * r + k]
            pltpu.make_async_copy(h_ref.at[pl.ds(r, 1)], xg_ref.at[pl.ds(d, 1)], sem).start()
        return carry

    lax.fori_loop(0, tm, issue, 0)
    rows = xg_ref.at[pl.ds(0, TOP_K * tm)]
    pltpu.make_async_copy(rows, rows, sem).wait()


def _dispatch(dest, h, n_rows, *, tm):
    n, s, l = h.shape
    return pl.pallas_call(
        functools.partial(_dispatch_kernel, tm=tm),
        out_shape=jax.ShapeDtypeStruct((n_rows, s, l), F32),
        grid_spec=pltpu.PrefetchScalarGridSpec(
            num_scalar_prefetch=1,
            grid=(n // tm,),
            in_specs=[pl.BlockSpec((tm, s, l), lambda i, dest: (i, 0, 0)),
                      pl.BlockSpec(memory_space=pl.ANY)],
            out_specs=pl.BlockSpec(memory_space=pl.ANY),
            scratch_shapes=[pltpu.SemaphoreType.DMA(())]),
        input_output_aliases={2: 0},
        compiler_params=_cparams(("arbitrary",)),
        name="moe_dispatch",
    )(dest, h, jnp.zeros((n_rows, s, l), F32))


def _grouped_ffn_kernel(te_ref, nu_ref, x_ref, wg_ref, wu_ref, wd_ref, y_ref, hb_ref, acc_ref):
    del te_ref
    i = pl.program_id(0)
    f = pl.program_id(1)
    nc = x_ref.shape[1]
    used = i < nu_ref[0]
    last_f = f == pl.num_programs(1) - 1

    @pl.when(used & (f == 0))
    def _():
        hb_ref[...] = jnp.concatenate([x_ref[:, cc, :] for cc in range(nc)], axis=1).astype(BF16)
        acc_ref[...] = jnp.zeros(acc_ref.shape, F32)

    @pl.when(used)
    def _():
        h = hb_ref[...]
        a = _silu(_dot(h, wg_ref[0])) * _dot(h, wu_ref[0])
        acc_ref[...] += _dot(a.astype(BF16), wd_ref[0])

    @pl.when(used & last_f)
    def _():
        for cc in range(nc):
            y_ref[:, cc, :] = acc_ref[:, cc * LANES:(cc + 1) * LANES]

    @pl.when(jnp.logical_not(used) & last_f)
    def _():
        y_ref[...] = jnp.zeros(y_ref.shape, F32)


def _grouped_ffn(tile_expert, n_used, xg, w_gu, w_down, *, tm, tf):
    n_rows, s, l = xg.shape
    n_exp, d_ff, d = w_down.shape
    nf = d_ff // tf

    def wspec(shape, index):
        def index_map(i, f, te, nu):
            return index(te[i], jnp.where(i < nu[0], f, nf - 1))
        return pl.BlockSpec(shape, index_map)

    return pl.pallas_call(
        _grouped_ffn_kernel,
        out_shape=jax.ShapeDtypeStruct((n_rows, s, l), F32),
        grid_spec=pltpu.PrefetchScalarGridSpec(
            num_scalar_prefetch=2,
            grid=(n_rows // tm, nf),
            in_specs=[pl.BlockSpec((tm, s, l), lambda i, f, te, nu: (i, 0, 0)),
                      wspec((1, d, tf), lambda e, f: (e, 0, f)),
                      wspec((1, d, tf), lambda e, f: (e, 0, nf + f)),
                      wspec((1, tf, d), lambda e, f: (e, f, 0))],
            out_specs=pl.BlockSpec((tm, s, l), lambda i, f, te, nu: (i, 0, 0)),
            scratch_shapes=[pltpu.VMEM((tm, d), BF16), pltpu.VMEM((tm, d), F32)]),
        compiler_params=_cparams(("arbitrary", "arbitrary")),
        name="moe_grouped_ffn",
    )(tile_expert, n_used, xg, w_gu, w_gu, w_down)


def _combine_kernel(dest_ref, x_ref, rt_ref, gt_ref, gf_ref, yg_ref, o_ref, y_ref, sem, *, tm):
    base = pl.program_id(0) * (TOP_K * tm)

    def issue(r, carry):
        for k in range(TOP_K):
            d = dest_ref[base + TOP_K * r + k]
            pltpu.make_async_copy(yg_ref.at[pl.ds(d, 1)], y_ref.at[k, pl.ds(r, 1)], sem.at[k]).start()
        return carry

    lax.fori_loop(0, tm, issue, 0)
    rt = rt_ref[...]
    tot = None
    for k in range(TOP_K):
        pltpu.make_async_copy(y_ref.at[k], y_ref.at[k], sem.at[k]).wait()
        y = jnp.concatenate([y_ref[k, :, cc, :] for cc in range(y_ref.shape[2])], axis=1)
        term = rt[:, TOP_K + k:TOP_K + k + 1] * y
        tot = term if tot is None else tot + term
    x = x_ref[...] + gt_ref[...] * tot
    o_ref[...] = _rms(x, gf_ref[...])


def _combine(dest, x, rt, mod1, g_final, yg, *, seq, tm):
    n, d = x.shape
    _, s, l = yg.shape
    tpb = seq // tm
    row = lambda w: pl.BlockSpec((tm, w), lambda i, dest: (i, 0))
    return pl.pallas_call(
        functools.partial(_combine_kernel, tm=tm),
        out_shape=jax.ShapeDtypeStruct((n, d), F32),
        grid_spec=pltpu.PrefetchScalarGridSpec(
            num_scalar_prefetch=1,
            grid=(n // tm,),
            in_specs=[row(d), row(LANES), _mod_spec(5, tpb, d),
                      pl.BlockSpec(g_final.shape, lambda i, dest: (0, 0)),
                      pl.BlockSpec(memory_space=pl.ANY)],
            out_specs=row(d),
            scratch_shapes=[pltpu.VMEM((TOP_K, tm, s, l), F32), pltpu.SemaphoreType.DMA((TOP_K,))]),
        compiler_params=_cparams(("arbitrary",)),
        name="moe_combine",
    )(dest, x, rt, mod1, g_final, yg)


def _pad_last(a, width):
    return jnp.pad(a, [(0, 0)] * (a.ndim - 1) + [(0, width - a.shape[-1])])


def kernel(x, c, positions, w_mod, b_mod, g_mix, g_ffn, w_a_down, g_q_lat, g_kv_lat, w_uq, w_ukv, w_oa,
           w_mod_kv, b_mod_kv, g_kv, w_kv_sb, w_q_sb, w_o_sb, w_ffn_gu, w_ffn_down, w_router, b_router,
           w_exp_gu, w_exp_down, g_final):
    bsz, seq, d = x.shape
    n = bsz * seq
    q_lora, kv_lora = g_q_lat.shape[1], g_kv_lat.shape[1]
    n_exp = w_router.shape[-1]
    d_ff = w_ffn_down.shape[1]
    tm = min(TOKEN_TILE, seq)
    t_mla = min(MLA_TILE, seq)
    t_sb = min(SB_TILE, seq)
    tf = 512 if d_ff % 512 == 0 else d_ff

    mod0 = _modvec(c, w_mod[0], b_mod[0]).reshape(bsz, 6, 1, d)
    mod1 = _modvec(c, w_mod[1], b_mod[1]).reshape(bsz, 6, 1, d)
    mod_kv = _modvec(c, w_mod_kv, b_mod_kv).reshape(bsz, 2, 1, d)

    lat_w = q_lora + kv_lora + LANES
    wd = _pad_last(w_a_down[0], lat_w).astype(BF16)
    wq = w_uq[0].reshape(q_lora, MLA_HEADS, QK_NOPE + QK_ROPE)
    wqn = wq[:, :, :QK_NOPE].reshape(q_lora, MLA_HEADS * QK_NOPE).astype(BF16)
    wqr = _pad_last(wq[:, :, QK_NOPE:], LANES).reshape(q_lora, MLA_HEADS * LANES).astype(BF16)
    wkv = w_ukv[0].reshape(kv_lora, MLA_HEADS, QK_NOPE + V_DIM)
    wkn = wkv[:, :, :QK_NOPE].reshape(kv_lora, MLA_HEADS * QK_NOPE).astype(BF16)
    wv = wkv[:, :, QK_NOPE:].reshape(kv_lora, MLA_HEADS * V_DIM).astype(BF16)
    half = QK_ROPE // 2
    inv = ROPE_THETA ** (-jnp.arange(half, dtype=F32) / half)
    inv = _pad_last(jnp.concatenate([inv, inv]), LANES).reshape(1, LANES)
    wr = _pad_last(w_router[0], LANES)
    wr_hi = wr.astype(BF16)
    wr_lo = (wr - wr_hi.astype(F32)).astype(BF16)
    br = _pad_last(b_router[0], LANES).reshape(1, LANES)

    xf = x.reshape(n, d)
    pos = positions.reshape(n, 1)
    row1 = lambda a: a.reshape(1, -1)

    q, k, v = _mla_proj(xf, pos, mod0, row1(g_mix[0]), wd, row1(g_q_lat[0]), row1(g_kv_lat[0]),
                        wqn, wqr, wkn, wv, inv, seq=seq, tm=tm)
    o = _mla_attn(q.reshape(bsz, seq, -1), k.reshape(bsz, seq, -1), v.reshape(bsz, seq, -1), t=t_mla)
    x1, h = _out_proj(o.reshape(n, -1), w_oa[0].astype(BF16), xf, mod0, row1(g_ffn[0]), seq=seq, tm=tm)
    x2, hk, hm = _ffn(h, w_ffn_gu[0].astype(BF16), w_ffn_down[0].astype(BF16), x1, mod0,
                      row1(g_kv), mod_kv, row1(g_mix[1]), mod1, seq=seq, tm=tm, tf=tf)
    kv = _linear(hk, w_kv_sb.astype(BF16), tm=tm, tn=1024, name="kv_proj")
    qs = _linear(hm, w_q_sb[0].astype(BF16), tm=tm, tn=1024, name="q_proj")
    o = _sb_attn(qs.reshape(bsz, seq, -1), kv.reshape(bsz, seq, -1), t=t_sb)
    x3, h, rt, counts = _out_proj(o.reshape(n, -1), w_o_sb[0].astype(BF16), x2, mod1, row1(g_ffn[1]),
                                  router=(wr_hi, wr_lo, br, n_exp), seq=seq, tm=tm)
    n_tiles = TOP_K * n // tm + n_exp
    dest, tile_expert, n_used = _route_plan(rt, counts, n_exp, tm, n_tiles)
    xg = _dispatch(dest, h, n_tiles * tm, tm=tm)
    yg = _grouped_ffn(tile_expert, n_used, xg, w_exp_gu[0].astype(BF16), w_exp_down[0].astype(BF16),
                      tm=tm, tf=tf)
    out = _combine(dest, x3, rt, mod1, row1(g_final), yg, seq=seq, tm=tm)
    return out.reshape(bsz, seq, d)
```

```python
import functools
import math

import jax
import jax.numpy as jnp
from jax import lax
from jax.experimental import pallas as pl
from jax.experimental.pallas import tpu as pltpu

F32 = jnp.float32
BF16 = jnp.bfloat16

EPS = 1e-6
MLA_HEADS = 8
QK_NOPE = 128
QK_ROPE = 64
V_DIM = 128
ROPE_THETA = 10000.0
SB_HEADS = 8
SB_HEAD_DIM = 128
TOP_K = 2

LANES = 128
QK_PAD = 256
VMEM_LIMIT = 48 * 1024 * 1024
TOKEN_TILE = 512
MLA_TILE = 512
SB_TILE = 512
FFN_CHUNK = 1792


def _cparams(sem):
    return pltpu.CompilerParams(dimension_semantics=sem, vmem_limit_bytes=VMEM_LIMIT)


def _rms(x, g):
    return x * lax.rsqrt(jnp.mean(x * x, axis=-1, keepdims=True) + EPS) * g


def _modulate(h, shift, scale):
    return h * (1.0 + scale) + shift


def _split_bf16(a):
    hi = a.astype(BF16)
    lo = (a - hi.astype(F32)).astype(BF16)
    return hi, lo


def _dot(a, b):
    return jnp.dot(a, b, preferred_element_type=F32)


def _dot_nt(a, b):
    return lax.dot_general(a, b, (((1,), (1,)), ((), ())), preferred_element_type=F32)


def _modvec_kernel(c_ref, w_ref, b_ref, o_ref):
    c = c_ref[...]
    sc = c * (1.0 / (1.0 + jnp.exp(-c)))
    a_hi, a_lo = _split_bf16(sc)
    w_hi, w_lo = _split_bf16(w_ref[...])
    o_ref[...] = _dot(a_hi, w_hi) + _dot(a_lo, w_hi) + _dot(a_hi, w_lo) + b_ref[...]


def _modvec(c, w, b, tn=512):
    bsz, d = c.shape
    n = w.shape[1]
    return pl.pallas_call(
        _modvec_kernel,
        out_shape=jax.ShapeDtypeStruct((bsz, n), F32),
        grid=(n // tn,),
        in_specs=[pl.BlockSpec((bsz, d), lambda j: (0, 0)),
                  pl.BlockSpec((d, tn), lambda j: (0, j)),
                  pl.BlockSpec((1, tn), lambda j: (0, j))],
        out_specs=pl.BlockSpec((bsz, tn), lambda j: (0, j)),
        compiler_params=_cparams(("arbitrary",)),
        name="modvec",
    )(c, w, b.reshape(1, n))


def _mod_spec(chunk, tiles_per_batch, d):
    return pl.BlockSpec((None, None, 1, d), lambda i, *_: (i // tiles_per_batch, chunk, 0, 0))


def _mla_proj_kernel(x_ref, pos_ref, sh_ref, sc_ref, g_ref, wd_ref, gq_ref, gkv_ref,
                     wqn_ref, wqr_ref, wkn_ref, wv_ref, inv_ref,
                     q_ref, k_ref, v_ref, *, q_lora, kv_lora):
    x = x_ref[...]
    h = _modulate(_rms(x, g_ref[...]), sh_ref[...], sc_ref[...]).astype(BF16)
    lat = _dot(h, wd_ref[...])
    c_q = _rms(lat[:, :q_lora], gq_ref[...]).astype(BF16)
    c_kv = _rms(lat[:, q_lora:q_lora + kv_lora], gkv_ref[...]).astype(BF16)
    k_rot = lat[:, q_lora + kv_lora:]

    half = QK_ROPE // 2
    ang = pos_ref[...].astype(F32) * inv_ref[...]
    cos = jnp.cos(ang)
    sin = jnp.sin(ang)
    lane = lax.broadcasted_iota(jnp.int32, ang.shape, 1)
    s_a = jnp.where(lane < half, -sin, 0.0)
    s_b = jnp.where((lane >= half) & (lane < 2 * half), sin, 0.0)

    def rope(r):
        return (r * cos + pltpu.roll(r, LANES - half, axis=1) * s_a
                + pltpu.roll(r, half, axis=1) * s_b)

    k_rot = rope(k_rot).astype(BF16)
    q_nope = _dot(c_q, wqn_ref[...])
    q_rope = _dot(c_q, wqr_ref[...])
    k_nope = _dot(c_kv, wkn_ref[...])
    v_ref[...] = _dot(c_kv, wv_ref[...]).astype(BF16)
    for hd in range(MLA_HEADS):
        a, b = hd * LANES, (hd + 1) * LANES
        q_ref[:, hd * QK_PAD:hd * QK_PAD + LANES] = q_nope[:, a:b].astype(BF16)
        q_ref[:, hd * QK_PAD + LANES:(hd + 1) * QK_PAD] = rope(q_rope[:, a:b]).astype(BF16)
        k_ref[:, hd * QK_PAD:hd * QK_PAD + LANES] = k_nope[:, a:b].astype(BF16)
        k_ref[:, hd * QK_PAD + LANES:(hd + 1) * QK_PAD] = k_rot


def _mla_proj(x, pos, mod, g_mix, wd, gq, gkv, wqn, wqr, wkn, wv, inv, *, seq, tm):
    n, d = x.shape
    tpb = seq // tm
    q_lora, kv_lora = gq.shape[1], gkv.shape[1]
    hq = MLA_HEADS * QK_PAD
    hv = MLA_HEADS * V_DIM
    full = lambda a: pl.BlockSpec(a.shape, lambda i: (0,) * a.ndim)
    row = lambda w: pl.BlockSpec((tm, w), lambda i: (i, 0))
    return pl.pallas_call(
        functools.partial(_mla_proj_kernel, q_lora=q_lora, kv_lora=kv_lora),
        out_shape=(jax.ShapeDtypeStruct((n, hq), BF16), jax.ShapeDtypeStruct((n, hq), BF16),
                   jax.ShapeDtypeStruct((n, hv), BF16)),
        grid=(n // tm,),
        in_specs=[row(d), row(1), _mod_spec(0, tpb, d), _mod_spec(1, tpb, d), full(g_mix), full(wd),
                  full(gq), full(gkv), full(wqn), full(wqr), full(wkn), full(wv), full(inv)],
        out_specs=(row(hq), row(hq), row(hv)),
        compiler_params=_cparams(("parallel",)),
        name="mla_proj",
    )(x, pos, mod, mod, g_mix, wd, gq, gkv, wqn, wqr, wkn, wv, inv)


def _lane_chunks(a):
    return [a[:, c * LANES:(c + 1) * LANES] for c in range(a.shape[1] // LANES)]


def _mla_attn_kernel(q_ref, k_ref, v_ref, o_ref, s_ref, m_ref, acc_ref, *, t, scale, heads):
    i = pl.program_id(2)

    def lane_max(s):
        m = None
        for sc in _lane_chunks(s):
            m = sc if m is None else jnp.maximum(m, sc)
        return m

    def scores(h, j):
        start = pl.multiple_of(j * t, t)
        hs = slice(h * QK_PAD, (h + 1) * QK_PAD)
        return _dot_nt(q_ref[0, :, hs], k_ref[0, pl.ds(start, t), hs])

    r = lax.broadcasted_iota(jnp.int32, (t, t), 0)
    c = lax.broadcasted_iota(jnp.int32, (t, t), 1)
    for h in range(heads):
        s = jnp.where(c <= r, scores(h, i), -jnp.inf)
        s_ref[h, i] = s
        m_ref[h] = lane_max(s)

    def pass1(j, carry):
        for h in range(heads):
            s = scores(h, j)
            s_ref[h, j] = s
            m_ref[h] = jnp.maximum(m_ref[h], lane_max(s))
        return carry

    lax.fori_loop(0, i, pass1, 0)
    for h in range(heads):
        m_ref[h] = jnp.broadcast_to(jnp.max(m_ref[h], axis=1, keepdims=True), (t, LANES))
    acc_ref[...] = jnp.zeros(acc_ref.shape, F32)
    ones = jnp.ones((t, LANES), BF16)
    cst = scale * math.log2(math.e)

    def pass2(j, carry):
        start = pl.multiple_of(j * t, t)
        for h in range(heads):
            m = m_ref[h]
            p = jnp.concatenate([jnp.exp2((sc - m) * cst).astype(BF16) for sc in _lane_chunks(s_ref[h, j])],
                                axis=1)
            v_ext = jnp.concatenate([v_ref[0, pl.ds(start, t), h * V_DIM:(h + 1) * V_DIM], ones], axis=1)
            acc_ref[h] += _dot(p, v_ext)
        return carry

    lax.fori_loop(0, i + 1, pass2, 0)
    for h in range(heads):
        acc = acc_ref[h]
        o_ref[0, :, h * V_DIM:(h + 1) * V_DIM] = (acc[:, :V_DIM] / acc[:, V_DIM:]).astype(o_ref.dtype)


def _mla_attn(q, k, v, *, t, heads=2):
    bsz, seq, _ = q.shape
    scale = 1.0 / math.sqrt(QK_NOPE + QK_ROPE)
    wq, wv = heads * QK_PAD, heads * V_DIM
    return pl.pallas_call(
        functools.partial(_mla_attn_kernel, t=t, scale=scale, heads=heads),
        out_shape=jax.ShapeDtypeStruct((bsz, seq, MLA_HEADS * V_DIM), BF16),
        grid=(bsz, MLA_HEADS // heads, seq // t),
        in_specs=[pl.BlockSpec((1, t, wq), lambda b, g, i: (b, i, g)),
                  pl.BlockSpec((1, seq, wq), lambda b, g, i: (b, 0, g)),
                  pl.BlockSpec((1, seq, wv), lambda b, g, i: (b, 0, g))],
        out_specs=pl.BlockSpec((1, t, wv), lambda b, g, i: (b, i, g)),
        scratch_shapes=[pltpu.VMEM((heads, seq // t, t, t), F32), pltpu.VMEM((heads, t, LANES), F32),
                        pltpu.VMEM((heads, t, V_DIM + LANES), F32)],
        compiler_params=_cparams(("parallel", "parallel", "arbitrary")),
        name="mla_attn",
    )(q, k, v)


def _sb_attn_kernel(q_ref, k_ref, v_ref, w_ref, o_ref, c_ref, acc_ref, *, t, scale, heads):
    i = pl.program_id(2)
    nl = t // LANES
    d = SB_HEAD_DIM
    c_ref[...] = jnp.zeros(c_ref.shape, F32)
    acc_ref[...] = jnp.zeros(acc_ref.shape, F32)
    r = lax.broadcasted_iota(jnp.int32, (t, t), 0)
    c = lax.broadcasted_iota(jnp.int32, (t, t), 1)
    strict = c < r
    cst = scale * math.log2(math.e)

    def step(j, masked):
        start = pl.multiple_of(j * t, t)
        for h in range(heads):
            hs = slice(h * d, (h + 1) * d)
            zz = _dot_nt(q_ref[0, :, hs], k_ref[0, pl.ds(start, t), hs]) * cst
            sp = jnp.log2(1.0 + jnp.exp2(-jnp.abs(zz)))
            log_beta = jnp.minimum(zz, 0.0) - sp
            log_1m = log_beta - zz
            if masked:
                log_1m = jnp.where(strict, log_1m, 0.0)
            hi, lo = _split_bf16(log_1m)
            carry = c_ref[h]
            a_chunks = [None] * nl
            for cc in reversed(range(nl)):
                cs = slice(cc * LANES, (cc + 1) * LANES)
                y = _dot(jnp.concatenate([hi[:, cs], lo[:, cs]], axis=1), w_ref[...])
                a = jnp.exp2(log_beta[:, cs] + y[:, :LANES] + carry)
                carry = carry + y[:, LANES:]
                if masked:
                    a = jnp.where(strict[:, cs], a, 0.0)
                a_chunks[cc] = a.astype(BF16)
            c_ref[h] = carry
            acc_ref[h] += _dot(jnp.concatenate(a_chunks, axis=1), v_ref[0, pl.ds(start, t), hs])

    step(i, True)

    def body(n, carry):
        step(i - 1 - n, False)
        return carry

    lax.fori_loop(0, i, body, 0)
    for h in range(heads):
        o_ref[0, :, h * d:(h + 1) * d] = acc_ref[h].astype(o_ref.dtype)


def _sb_attn(q, kv, *, t, heads=2):
    bsz, seq, _ = q.shape
    scale = 1.0 / math.sqrt(SB_HEAD_DIM)
    w = heads * SB_HEAD_DIM
    groups = SB_HEADS // heads
    tri = (jnp.arange(LANES)[:, None] > jnp.arange(LANES)[None, :]).astype(BF16)
    half = jnp.concatenate([tri, jnp.ones((LANES, LANES), BF16)], axis=1)
    w_sum = jnp.concatenate([half, half], axis=0)
    return pl.pallas_call(
        functools.partial(_sb_attn_kernel, t=t, scale=scale, heads=heads),
        out_shape=jax.ShapeDtypeStruct((bsz, seq, SB_HEADS * SB_HEAD_DIM), BF16),
        grid=(bsz, groups, seq // t),
        in_specs=[pl.BlockSpec((1, t, w), lambda b, g, i: (b, i, g)),
                  pl.BlockSpec((1, seq, w), lambda b, g, i: (b, 0, g)),
                  pl.BlockSpec((1, seq, w), lambda b, g, i: (b, 0, groups + g)),
                  pl.BlockSpec(w_sum.shape, lambda b, g, i: (0, 0))],
        out_specs=pl.BlockSpec((1, t, w), lambda b, g, i: (b, i, g)),
        scratch_shapes=[pltpu.VMEM((heads, t, LANES), F32), pltpu.VMEM((heads, t, SB_HEAD_DIM), F32)],
        compiler_params=_cparams(("parallel", "parallel", "arbitrary")),
        name="sb_attn",
    )(q, kv, kv, w_sum)


def _out_proj_kernel(*refs, n_experts):
    if n_experts:
        (o_ref, w_ref, x_ref, gt_ref, g_ref, sh_ref, sc_ref, wr_hi_ref, wr_lo_ref, br_ref, tri_ref,
         xo_ref, h_ref, rt_ref, cnt_ref) = refs
    else:
        o_ref, w_ref, x_ref, gt_ref, g_ref, sh_ref, sc_ref, xo_ref, h_ref = refs
    x = x_ref[...] + gt_ref[...] * _dot(o_ref[...], w_ref[...])
    xo_ref[...] = x
    h = _modulate(_rms(x, g_ref[...]), sh_ref[...], sc_ref[...])
    h_hi = h.astype(BF16)
    if not n_experts:
        h_ref[...] = h_hi
        return
    h_ref[...] = h
    h_lo = (h - h_hi.astype(F32)).astype(BF16)
    logits = (_dot(h_hi, wr_hi_ref[...]) + _dot(h_lo, wr_hi_ref[...]) + _dot(h_hi, wr_lo_ref[...])
              + br_ref[...])
    lane = lax.broadcasted_iota(jnp.int32, logits.shape, 1).astype(F32)
    lg = jnp.where(lane < n_experts, logits, -jnp.inf)
    m1 = jnp.max(lg, axis=1, keepdims=True)
    i1 = jnp.min(jnp.where(lg == m1, lane, float(LANES)), axis=1, keepdims=True)
    lg2 = jnp.where(lane == i1, -jnp.inf, lg)
    m2 = jnp.max(lg2, axis=1, keepdims=True)
    i2 = jnp.min(jnp.where(lg2 == m2, lane, float(LANES)), axis=1, keepdims=True)
    e2 = jnp.exp(m2 - m1)
    den = 1.0 + e2
    sel = jnp.where((lane == i1) | (lane == i2), 1.0, 0.0)
    prefix = _dot(tri_ref[...], sel.astype(BF16))
    r1 = jnp.sum(jnp.where(lane == i1, prefix, 0.0), axis=1, keepdims=True)
    r2 = jnp.sum(jnp.where(lane == i2, prefix, 0.0), axis=1, keepdims=True)
    cnt_ref[...] = jnp.sum(sel, axis=0, keepdims=True)
    rt = jnp.zeros_like(logits)
    for k, val in enumerate((i1, i2, 1.0 / den, e2 / den, r1, r2)):
        rt = jnp.where(lane == k, val, rt)
    rt_ref[...] = rt


def _out_proj(o, w, x, mod, g, router=None, *, seq, tm):
    n, d = x.shape
    tpb = seq // tm
    full = lambda a: pl.BlockSpec(a.shape, lambda i: (0,) * a.ndim)
    row = lambda wd: pl.BlockSpec((tm, wd), lambda i: (i, 0))
    in_specs = [row(o.shape[1]), full(w), row(d), _mod_spec(2, tpb, d), full(g),
                _mod_spec(3, tpb, d), _mod_spec(4, tpb, d)]
    args = [o, w, x, mod, g, mod, mod]
    out_shape = [jax.ShapeDtypeStruct((n, d), F32), jax.ShapeDtypeStruct((n, d), BF16)]
    out_specs = [row(d), row(d)]
    n_experts = 0
    if router is not None:
        wr_hi, wr_lo, br, n_experts = router
        tri = (jnp.arange(tm)[:, None] > jnp.arange(tm)[None, :]).astype(BF16)
        in_specs += [full(wr_hi), full(wr_lo), full(br), full(tri)]
        args += [wr_hi, wr_lo, br, tri]
        out_shape[1] = jax.ShapeDtypeStruct((n, d), F32)
        out_shape += [jax.ShapeDtypeStruct((n, LANES), F32), jax.ShapeDtypeStruct((n // tm, 1, LANES), F32)]
        out_specs += [row(LANES), pl.BlockSpec((None, 1, LANES), lambda i: (i, 0, 0))]
    return pl.pallas_call(
        functools.partial(_out_proj_kernel, n_experts=n_experts),
        out_shape=tuple(out_shape),
        grid=(n // tm,),
        in_specs=in_specs,
        out_specs=tuple(out_specs),
        compiler_params=_cparams(("parallel",)),
        name="out_proj_router" if n_experts else "out_proj",
    )(*args)


def _silu(g):
    return g * (1.0 / (1.0 + jnp.exp(-g)))


def _ffn_kernel(h_ref, wg_ref, wu_ref, wd_ref, x_ref, gt_ref, gkv_ref, shk_ref, sck_ref,
                gm_ref, shm_ref, scm_ref, xo_ref, hk_ref, hm_ref, acc_ref):
    f = pl.program_id(1)

    @pl.when(f == 0)
    def _():
        acc_ref[...] = jnp.zeros(acc_ref.shape, F32)

    h = h_ref[...]
    a = _silu(_dot(h, wg_ref[...])) * _dot(h, wu_ref[...])
    acc_ref[...] += _dot(a.astype(BF16), wd_ref[...])

    @pl.when(f == pl.num_programs(1) - 1)
    def _():
        x = x_ref[...] + gt_ref[...] * acc_ref[...]
        xo_ref[...] = x
        hk_ref[...] = _modulate(_rms(x, gkv_ref[...]), shk_ref[...], sck_ref[...]).astype(BF16)
        hm_ref[...] = _modulate(_rms(x, gm_ref[...]), shm_ref[...], scm_ref[...]).astype(BF16)


def _ffn(h, w_gu, w_down, x, mod0, g_kv, mod_kv, g_mix1, mod1, *, seq, tm, tf):
    n, d = x.shape
    d_ff = w_down.shape[0]
    nf = d_ff // tf
    tpb = seq // tm
    full = lambda a: pl.BlockSpec(a.shape, lambda i, f: (0,) * a.ndim)
    row = lambda: pl.BlockSpec((tm, d), lambda i, f: (i, 0))
    return pl.pallas_call(
        _ffn_kernel,
        out_shape=(jax.ShapeDtypeStruct((n, d), F32), jax.ShapeDtypeStruct((n, d), BF16),
                   jax.ShapeDtypeStruct((n, d), BF16)),
        grid=(n // tm, nf),
        in_specs=[row(),
                  pl.BlockSpec((d, tf), lambda i, f: (0, f)),
                  pl.BlockSpec((d, tf), lambda i, f: (0, nf + f)),
                  pl.BlockSpec((tf, d), lambda i, f: (f, 0)),
                  row(), _mod_spec(5, tpb, d),
                  full(g_kv), _mod_spec(0, tpb, d), _mod_spec(1, tpb, d),
                  full(g_mix1), _mod_spec(0, tpb, d), _mod_spec(1, tpb, d)],
        out_specs=(row(), row(), row()),
        scratch_shapes=[pltpu.VMEM((tm, d), F32)],
        compiler_params=_cparams(("parallel", "arbitrary")),
        name="ffn_dense",
    )(h, w_gu, w_gu, w_down, x, mod0, g_kv, mod_kv, mod_kv, g_mix1, mod1, mod1)


def _linear_kernel(x_ref, w_ref, o_ref):
    o_ref[...] = _dot(x_ref[...], w_ref[...]).astype(o_ref.dtype)


def _linear(x, w, *, tm, tn, name):
    n, k = x.shape
    m = w.shape[1]
    return pl.pallas_call(
        _linear_kernel,
        out_shape=jax.ShapeDtypeStruct((n, m), BF16),
        grid=(m // tn, n // tm),
        in_specs=[pl.BlockSpec((tm, k), lambda j, i: (i, 0)), pl.BlockSpec((k, tn), lambda j, i: (0, j))],
        out_specs=pl.BlockSpec((tm, tn), lambda j, i: (i, j)),
        compiler_params=_cparams(("parallel", "parallel")),
        name=name,
    )(x, w)


def _route_plan(rt, counts, n_exp, tm, n_tiles):
    cnt = counts[:, 0, :n_exp].astype(jnp.int32)
    sizes = jnp.sum(cnt, axis=0)
    padded = (sizes + tm - 1) // tm * tm
    ends = jnp.cumsum(padded)
    tile_base = (ends - padded)[None, :] + jnp.cumsum(cnt, axis=0) - cnt
    base = jnp.repeat(tile_base, tm, axis=0)
    experts = jnp.arange(n_exp, dtype=jnp.int32)[None, :]
    dest = []
    for k in range(TOP_K):
        e_k = rt[:, k].astype(jnp.int32)
        r_k = rt[:, 2 * TOP_K + k].astype(jnp.int32)
        dest.append(jnp.sum(jnp.where(e_k[:, None] == experts, base, 0), axis=1) + r_k)
    dest = jnp.stack(dest, axis=1).reshape(-1)
    tile_start = jnp.arange(n_tiles, dtype=jnp.int32) * tm
    tile_expert = jnp.minimum(jnp.sum(tile_start[:, None] >= ends[None, :], axis=1), n_exp - 1)
    n_used = (ends[-1] // tm).reshape(1)
    return dest, tile_expert.astype(jnp.int32), n_used.astype(jnp.int32)


def _dispatch_kernel(dest_ref, h_ref, xg_in_ref, xg_ref, sem, *, tm):
    del xg_in_ref
    base = pl.program_id(0) * (TOP_K * tm)

    def issue(r, carry):
        for k in range(TOP_K):
            d = dest_ref[base + TOP_K * r + k]
            pltpu.make_async_copy(h_ref.at[pl.ds(r, 1)], xg_ref.at[pl.ds(d, 1)], sem).start()
        return carry

    lax.fori_loop(0, tm, issue, 0)
    rows = xg_ref.at[pl.ds(0, TOP_K * tm)]
    pltpu.make_async_copy(rows, rows, sem).wait()


def _dispatch(dest, h, n_rows, *, tm):
    n, d = h.shape
    return pl.pallas_call(
        functools.partial(_dispatch_kernel, tm=tm),
        out_shape=jax.ShapeDtypeStruct((n_rows, d), F32),
        grid_spec=pltpu.PrefetchScalarGridSpec(
            num_scalar_prefetch=1,
            grid=(n // tm,),
            in_specs=[pl.BlockSpec((tm, d), lambda i, dest: (i, 0)),
                      pl.BlockSpec(memory_space=pl.ANY)],
            out_specs=pl.BlockSpec(memory_space=pl.ANY),
            scratch_shapes=[pltpu.SemaphoreType.DMA(())]),
        input_output_aliases={2: 0},
        compiler_params=_cparams(("arbitrary",)),
        name="moe_dispatch",
    )(dest, h, jnp.zeros((n_rows, d), F32))


def _grouped_ffn_kernel(te_ref, nu_ref, x_ref, wg_ref, wu_ref, wd_ref, y_ref, hb_ref, acc_ref):
    del te_ref
    i = pl.program_id(0)
    f = pl.program_id(1)
    used = i < nu_ref[0]
    last_f = f == pl.num_programs(1) - 1

    @pl.when(used & (f == 0))
    def _():
        hb_ref[...] = x_ref[...].astype(BF16)
        acc_ref[...] = jnp.zeros(acc_ref.shape, F32)

    @pl.when(used)
    def _():
        h = hb_ref[...]
        a = _silu(_dot(h, wg_ref[0])) * _dot(h, wu_ref[0])
        acc_ref[...] += _dot(a.astype(BF16), wd_ref[0])

    @pl.when(used & last_f)
    def _():
        y_ref[...] = acc_ref[...]

    @pl.when(jnp.logical_not(used) & last_f)
    def _():
        y_ref[...] = jnp.zeros(y_ref.shape, F32)


def _grouped_ffn(tile_expert, n_used, xg, w_gu, w_down, *, tm, tf):
    n_rows, _ = xg.shape
    n_exp, d_ff, d = w_down.shape
    nf = d_ff // tf

    def wspec(shape, index):
        def index_map(i, f, te, nu):
            return index(te[i], jnp.where(i < nu[0], f, nf - 1))
        return pl.BlockSpec(shape, index_map)

    return pl.pallas_call(
        _grouped_ffn_kernel,
        out_shape=jax.ShapeDtypeStruct((n_rows, d), F32),
        grid_spec=pltpu.PrefetchScalarGridSpec(
            num_scalar_prefetch=2,
            grid=(n_rows // tm, nf),
            in_specs=[pl.BlockSpec((tm, d), lambda i, f, te, nu: (i, 0)),
                      wspec((1, d, tf), lambda e, f: (e, 0, f)),
                      wspec((1, d, tf), lambda e, f: (e, 0, nf + f)),
                      wspec((1, tf, d), lambda e, f: (e, f, 0))],
            out_specs=pl.BlockSpec((tm, d), lambda i, f, te, nu: (i, 0)),
            scratch_shapes=[pltpu.VMEM((tm, d), BF16), pltpu.VMEM((tm, d), F32)]),
        compiler_params=_cparams(("arbitrary", "arbitrary")),
        name="moe_grouped_ffn",
    )(tile_expert, n_used, xg, w_gu, w_gu, w_down)


def _combine_kernel(dest_ref, x_ref, rt_ref, gt_ref, gf_ref, yg_ref, o_ref, y_ref, sem, *, tm):
    i = pl.program_id(0)
    slot = i % 2

    def issue(step, into):
        base = step * (TOP_K * tm)

        def body(r, carry):
            for k in range(TOP_K):
                d = dest_ref[base + TOP_K * r + k]
                pltpu.make_async_copy(yg_ref.at[pl.ds(d, 1)], y_ref.at[into, k, pl.ds(r, 1)],
                                      sem.at[into, k]).start()
            return carry

        lax.fori_loop(0, tm, body, 0)

    @pl.when(i == 0)
    def _():
        issue(0, 0)

    @pl.when(i + 1 < pl.num_programs(0))
    def _():
        issue(i + 1, 1 - slot)

    rt = rt_ref[...]
    tot = None
    for k in range(TOP_K):
        rows = y_ref.at[slot, k]
        pltpu.make_async_copy(rows, rows, sem.at[slot, k]).wait()
        term = rt[:, TOP_K + k:TOP_K + k + 1] * y_ref[slot, k]
        tot = term if tot is None else tot + term
    x = x_ref[...] + gt_ref[...] * tot
    o_ref[...] = _rms(x, gf_ref[...])


def _combine(dest, x, rt, mod1, g_final, yg, *, seq, tm):
    n, d = x.shape
    tpb = seq // tm
    row = lambda w: pl.BlockSpec((tm, w), lambda i, dest: (i, 0))
    return pl.pallas_call(
        functools.partial(_combine_kernel, tm=tm),
        out_shape=jax.ShapeDtypeStruct((n, d), F32),
        grid_spec=pltpu.PrefetchScalarGridSpec(
            num_scalar_prefetch=1,
            grid=(n // tm,),
            in_specs=[row(d), row(LANES), _mod_spec(5, tpb, d),
                      pl.BlockSpec(g_final.shape, lambda i, dest: (0, 0)),
                      pl.BlockSpec(memory_space=pl.ANY)],
            out_specs=row(d),
            scratch_shapes=[pltpu.VMEM((2, TOP_K, tm, d), F32), pltpu.SemaphoreType.DMA((2, TOP_K))]),
        compiler_params=_cparams(("arbitrary",)),
        name="moe_combine",
    )(dest, x, rt, mod1, g_final, yg)


def _pad_last(a, width):
    return jnp.pad(a, [(0, 0)] * (a.ndim - 1) + [(0, width - a.shape[-1])])


def kernel(x, c, positions, w_mod, b_mod, g_mix, g_ffn, w_a_down, g_q_lat, g_kv_lat, w_uq, w_ukv, w_oa,
           w_mod_kv, b_mod_kv, g_kv, w_kv_sb, w_q_sb, w_o_sb, w_ffn_gu, w_ffn_down, w_router, b_router,
           w_exp_gu, w_exp_down, g_final):
    bsz, seq, d = x.shape
    n = bsz * seq
    q_lora, kv_lora = g_q_lat.shape[1], g_kv_lat.shape[1]
    n_exp = w_router.shape[-1]
    d_ff = w_ffn_down.shape[1]
    tm = min(TOKEN_TILE, seq)
    t_mla = min(MLA_TILE, seq)
    t_sb = min(SB_TILE, seq)
    tf = next((t for t in (FFN_CHUNK, 512) if d_ff % t == 0), d_ff)

    mod0 = _modvec(c, w_mod[0], b_mod[0]).reshape(bsz, 6, 1, d)
    mod1 = _modvec(c, w_mod[1], b_mod[1]).reshape(bsz, 6, 1, d)
    mod_kv = _modvec(c, w_mod_kv, b_mod_kv).reshape(bsz, 2, 1, d)

    lat_w = q_lora + kv_lora + LANES
    wd = _pad_last(w_a_down[0], lat_w).astype(BF16)
    wq = w_uq[0].reshape(q_lora, MLA_HEADS, QK_NOPE + QK_ROPE)
    wqn = wq[:, :, :QK_NOPE].reshape(q_lora, MLA_HEADS * QK_NOPE).astype(BF16)
    wqr = _pad_last(wq[:, :, QK_NOPE:], LANES).reshape(q_lora, MLA_HEADS * LANES).astype(BF16)
    wkv = w_ukv[0].reshape(kv_lora, MLA_HEADS, QK_NOPE + V_DIM)
    wkn = wkv[:, :, :QK_NOPE].reshape(kv_lora, MLA_HEADS * QK_NOPE).astype(BF16)
    wv = wkv[:, :, QK_NOPE:].reshape(kv_lora, MLA_HEADS * V_DIM).astype(BF16)
    half = QK_ROPE // 2
    inv = ROPE_THETA ** (-jnp.arange(half, dtype=F32) / half)
    inv = _pad_last(jnp.concatenate([inv, inv]), LANES).reshape(1, LANES)
    wr = _pad_last(w_router[0], LANES)
    wr_hi = wr.astype(BF16)
    wr_lo = (wr - wr_hi.astype(F32)).astype(BF16)
    br = _pad_last(b_router[0], LANES).reshape(1, LANES)

    xf = x.reshape(n, d)
    pos = positions.reshape(n, 1)
    row1 = lambda a: a.reshape(1, -1)

    q, k, v = _mla_proj(xf, pos, mod0, row1(g_mix[0]), wd, row1(g_q_lat[0]), row1(g_kv_lat[0]),
                        wqn, wqr, wkn, wv, inv, seq=seq, tm=tm)
    o = _mla_attn(q.reshape(bsz, seq, -1), k.reshape(bsz, seq, -1), v.reshape(bsz, seq, -1), t=t_mla)
    x1, h = _out_proj(o.reshape(n, -1), w_oa[0].astype(BF16), xf, mod0, row1(g_ffn[0]), seq=seq, tm=tm)
    x2, hk, hm = _ffn(h, w_ffn_gu[0].astype(BF16), w_ffn_down[0].astype(BF16), x1, mod0,
                      row1(g_kv), mod_kv, row1(g_mix[1]), mod1, seq=seq, tm=tm, tf=tf)
    kv = _linear(hk, w_kv_sb.astype(BF16), tm=tm, tn=1024, name="kv_proj")
    qs = _linear(hm, w_q_sb[0].astype(BF16), tm=tm, tn=1024, name="q_proj")
    o = _sb_attn(qs.reshape(bsz, seq, -1), kv.reshape(bsz, seq, -1), t=t_sb)
    x3, h, rt, counts = _out_proj(o.reshape(n, -1), w_o_sb[0].astype(BF16), x2, mod1, row1(g_ffn[1]),
                                  router=(wr_hi, wr_lo, br, n_exp), seq=seq, tm=tm)
    n_tiles = TOP_K * n // tm + n_exp
    dest, tile_expert, n_used = _route_plan(rt, counts, n_exp, tm, n_tiles)
    xg = _dispatch(dest, h, n_tiles * tm, tm=tm)
    yg = _grouped_ffn(tile_expert, n_used, xg, w_exp_gu[0].astype(BF16), w_exp_down[0].astype(BF16),
                      tm=tm, tf=tf)
    out = _combine(dest, x3, rt, mod1, row1(g_final), yg, seq=seq, tm=tm)
    return out.reshape(bsz, seq, d)
```

```python
import functools
import math

import jax
import jax.numpy as jnp
from jax import lax
from jax.experimental import pallas as pl
from jax.experimental.pallas import tpu as pltpu

F32 = jnp.float32
BF16 = jnp.bfloat16

EPS = 1e-6
MLA_HEADS = 8
QK_NOPE = 128
QK_ROPE = 64
V_DIM = 128
ROPE_THETA = 10000.0
SB_HEADS = 8
SB_HEAD_DIM = 128
TOP_K = 2

LANES = 128
QK_PAD = 256
VMEM_LIMIT = 48 * 1024 * 1024
TOKEN_TILE = 512
MLA_TILE = 512
SB_TILE = 512
FFN_CHUNK = 1792


def _cparams(sem):
    return pltpu.CompilerParams(dimension_semantics=sem, vmem_limit_bytes=VMEM_LIMIT)


def _rms(x, g):
    return x * lax.rsqrt(jnp.mean(x * x, axis=-1, keepdims=True) + EPS) * g


def _modulate(h, shift, scale):
    return h * (1.0 + scale) + shift


def _split_bf16(a):
    hi = a.astype(BF16)
    lo = (a - hi.astype(F32)).astype(BF16)
    return hi, lo


def _dot(a, b):
    return jnp.dot(a, b, preferred_element_type=F32)


def _dot_nt(a, b):
    return lax.dot_general(a, b, (((1,), (1,)), ((), ())), preferred_element_type=F32)


def _modvec_kernel(c_ref, w_ref, b_ref, o_ref):
    c = c_ref[...]
    sc = c * (1.0 / (1.0 + jnp.exp(-c)))
    a_hi, a_lo = _split_bf16(sc)
    w_hi, w_lo = _split_bf16(w_ref[...])
    o_ref[...] = _dot(a_hi, w_hi) + _dot(a_lo, w_hi) + _dot(a_hi, w_lo) + b_ref[...]


def _modvec(c, w, b, tn=512):
    bsz, d = c.shape
    n = w.shape[1]
    return pl.pallas_call(
        _modvec_kernel,
        out_shape=jax.ShapeDtypeStruct((bsz, n), F32),
        grid=(n // tn,),
        in_specs=[pl.BlockSpec((bsz, d), lambda j: (0, 0)),
                  pl.BlockSpec((d, tn), lambda j: (0, j)),
                  pl.BlockSpec((1, tn), lambda j: (0, j))],
        out_specs=pl.BlockSpec((bsz, tn), lambda j: (0, j)),
        compiler_params=_cparams(("arbitrary",)),
        name="modvec",
    )(c, w, b.reshape(1, n))


def _mod_spec(chunk, tiles_per_batch, d):
    return pl.BlockSpec((None, None, 1, d), lambda i, *_: (i // tiles_per_batch, chunk, 0, 0))


def _mla_proj_kernel(x_ref, pos_ref, sh_ref, sc_ref, g_ref, wd_ref, gq_ref, gkv_ref,
                     wqn_ref, wqr_ref, wkn_ref, wv_ref, inv_ref,
                     q_ref, k_ref, v_ref, *, q_lora, kv_lora):
    x = x_ref[...]
    h = _modulate(_rms(x, g_ref[...]), sh_ref[...], sc_ref[...]).astype(BF16)
    lat = _dot(h, wd_ref[...])
    c_q = _rms(lat[:, :q_lora], gq_ref[...]).astype(BF16)
    c_kv = _rms(lat[:, q_lora:q_lora + kv_lora], gkv_ref[...]).astype(BF16)
    k_rot = lat[:, q_lora + kv_lora:]

    half = QK_ROPE // 2
    ang = pos_ref[...].astype(F32) * inv_ref[...]
    cos = jnp.cos(ang)
    sin = jnp.sin(ang)
    lane = lax.broadcasted_iota(jnp.int32, ang.shape, 1)
    s_a = jnp.where(lane < half, -sin, 0.0)
    s_b = jnp.where((lane >= half) & (lane < 2 * half), sin, 0.0)

    def rope(r):
        return (r * cos + pltpu.roll(r, LANES - half, axis=1) * s_a
                + pltpu.roll(r, half, axis=1) * s_b)

    k_rot = rope(k_rot).astype(BF16)
    q_nope = _dot(c_q, wqn_ref[...])
    q_rope = _dot(c_q, wqr_ref[...])
    k_nope = _dot(c_kv, wkn_ref[...])
    v_ref[...] = _dot(c_kv, wv_ref[...]).astype(BF16)
    for hd in range(MLA_HEADS):
        a, b = hd * LANES, (hd + 1) * LANES
        q_ref[:, hd * QK_PAD:hd * QK_PAD + LANES] = q_nope[:, a:b].astype(BF16)
        q_ref[:, hd * QK_PAD + LANES:(hd + 1) * QK_PAD] = rope(q_rope[:, a:b]).astype(BF16)
        k_ref[:, hd * QK_PAD:hd * QK_PAD + LANES] = k_nope[:, a:b].astype(BF16)
        k_ref[:, hd * QK_PAD + LANES:(hd + 1) * QK_PAD] = k_rot


def _mla_proj(x, pos, mod, g_mix, wd, gq, gkv, wqn, wqr, wkn, wv, inv, *, seq, tm):
    n, d = x.shape
    tpb = seq // tm
    q_lora, kv_lora = gq.shape[1], gkv.shape[1]
    hq = MLA_HEADS * QK_PAD
    hv = MLA_HEADS * V_DIM
    full = lambda a: pl.BlockSpec(a.shape, lambda i: (0,) * a.ndim)
    row = lambda w: pl.BlockSpec((tm, w), lambda i: (i, 0))
    return pl.pallas_call(
        functools.partial(_mla_proj_kernel, q_lora=q_lora, kv_lora=kv_lora),
        out_shape=(jax.ShapeDtypeStruct((n, hq), BF16), jax.ShapeDtypeStruct((n, hq), BF16),
                   jax.ShapeDtypeStruct((n, hv), BF16)),
        grid=(n // tm,),
        in_specs=[row(d), row(1), _mod_spec(0, tpb, d), _mod_spec(1, tpb, d), full(g_mix), full(wd),
                  full(gq), full(gkv), full(wqn), full(wqr), full(wkn), full(wv), full(inv)],
        out_specs=(row(hq), row(hq), row(hv)),
        compiler_params=_cparams(("parallel",)),
        name="mla_proj",
    )(x, pos, mod, mod, g_mix, wd, gq, gkv, wqn, wqr, wkn, wv, inv)


def _lane_chunks(a):
    return [a[:, c * LANES:(c + 1) * LANES] for c in range(a.shape[1] // LANES)]


def _for_blocks(n, step):
    def pair(p, carry):
        step(2 * p, 2)
        return carry

    lax.fori_loop(0, n // 2, pair, 0)

    @pl.when(n % 2 == 1)
    def _():
        step(n - 1, 1)


def _mla_attn_kernel(q_ref, k_ref, v_ref, o_ref, s_ref, m_ref, acc_ref, *, t, scale, heads):
    i = pl.program_id(2)

    def lane_max(s):
        m = None
        for sc in _lane_chunks(s):
            m = sc if m is None else jnp.maximum(m, sc)
        return m

    def scores(h, j):
        start = pl.multiple_of(j * t, t)
        hs = slice(h * QK_PAD, (h + 1) * QK_PAD)
        return _dot_nt(q_ref[0, :, hs], k_ref[0, pl.ds(start, t), hs])

    r = lax.broadcasted_iota(jnp.int32, (t, t), 0)
    c = lax.broadcasted_iota(jnp.int32, (t, t), 1)
    for h in range(heads):
        s = jnp.where(c <= r, scores(h, i), -jnp.inf)
        s_ref[h, i] = s
        m_ref[h] = lane_max(s)

    def pass1(j, nb):
        for h in range(heads):
            m = m_ref[h]
            for jj in range(nb):
                s = scores(h, j + jj)
                s_ref[h, j + jj] = s
                m = jnp.maximum(m, lane_max(s))
            m_ref[h] = m

    _for_blocks(i, pass1)
    for h in range(heads):
        m_ref[h] = jnp.broadcast_to(jnp.max(m_ref[h], axis=1, keepdims=True), (t, LANES))
    acc_ref[...] = jnp.zeros(acc_ref.shape, F32)
    cst = scale * math.log2(math.e)

    def pass2(j, nb):
        start = pl.multiple_of(j * t, t)
        ones = jnp.ones((nb * t, LANES), BF16)
        for h in range(heads):
            m = m_ref[h]
            p = jnp.concatenate([jnp.exp2((sc - m) * cst).astype(BF16)
                                 for jj in range(nb) for sc in _lane_chunks(s_ref[h, j + jj])], axis=1)
            v_ext = jnp.concatenate([v_ref[0, pl.ds(start, nb * t), h * V_DIM:(h + 1) * V_DIM], ones], axis=1)
            acc_ref[h] += _dot(p, v_ext)

    _for_blocks(i + 1, pass2)
    for h in range(heads):
        acc = acc_ref[h]
        o_ref[0, :, h * V_DIM:(h + 1) * V_DIM] = (acc[:, :V_DIM] / acc[:, V_DIM:]).astype(o_ref.dtype)


def _mla_attn(q, k, v, *, t, heads=2):
    bsz, seq, _ = q.shape
    scale = 1.0 / math.sqrt(QK_NOPE + QK_ROPE)
    wq, wv = heads * QK_PAD, heads * V_DIM
    return pl.pallas_call(
        functools.partial(_mla_attn_kernel, t=t, scale=scale, heads=heads),
        out_shape=jax.ShapeDtypeStruct((bsz, seq, MLA_HEADS * V_DIM), BF16),
        grid=(bsz, MLA_HEADS // heads, seq // t),
        in_specs=[pl.BlockSpec((1, t, wq), lambda b, g, i: (b, i, g)),
                  pl.BlockSpec((1, seq, wq), lambda b, g, i: (b, 0, g)),
                  pl.BlockSpec((1, seq, wv), lambda b, g, i: (b, 0, g))],
        out_specs=pl.BlockSpec((1, t, wv), lambda b, g, i: (b, i, g)),
        scratch_shapes=[pltpu.VMEM((heads, seq // t, t, t), F32), pltpu.VMEM((heads, t, LANES), F32),
                        pltpu.VMEM((heads, t, V_DIM + LANES), F32)],
        compiler_params=_cparams(("parallel", "parallel", "arbitrary")),
        name="mla_attn",
    )(q, k, v)


def _sb_attn_kernel(q_ref, k_ref, v_ref, w_ref, o_ref, c_ref, acc_ref, *, t, heads):
    i = pl.program_id(2)
    nl = t // LANES
    d = SB_HEAD_DIM
    c_ref[...] = jnp.zeros(c_ref.shape, F32)
    acc_ref[...] = jnp.zeros(acc_ref.shape, F32)
    r = lax.broadcasted_iota(jnp.int32, (t, t), 0)
    c = lax.broadcasted_iota(jnp.int32, (t, t), 1)
    strict = c < r

    def step(j, masked):
        start = pl.multiple_of(j * t, t)
        for h in range(heads):
            hs = slice(h * d, (h + 1) * d)
            zz = _dot_nt(q_ref[0, :, hs], k_ref[0, pl.ds(start, t), hs])
            zneg = -zz
            sp = jnp.log2(1.0 + jnp.exp2(jnp.minimum(zz, zneg)))
            log_1m = jnp.minimum(zneg, 0.0) - sp
            if masked:
                log_1m = jnp.where(strict, log_1m, 0.0)
            hi, lo = _split_bf16(log_1m)
            carry = c_ref[h]
            a_chunks = [None] * nl
            for cc in reversed(range(nl)):
                cs = slice(cc * LANES, (cc + 1) * LANES)
                y = _dot(jnp.concatenate([hi[:, cs], lo[:, cs]], axis=1), w_ref[...])
                a = jnp.exp2(zz[:, cs] + y[:, :LANES] + carry)
                carry = carry + y[:, LANES:]
                if masked:
                    a = jnp.where(strict[:, cs], a, 0.0)
                a_chunks[cc] = a.astype(BF16)
            c_ref[h] = carry
            acc_ref[h] += _dot(jnp.concatenate(a_chunks, axis=1), v_ref[0, pl.ds(start, t), hs])

    step(i, True)

    def body(n, carry):
        step(i - 1 - n, False)
        return carry

    lax.fori_loop(0, i, body, 0)
    for h in range(heads):
        o_ref[0, :, h * d:(h + 1) * d] = acc_ref[h].astype(o_ref.dtype)


def _sb_attn(q, kv, *, t, heads=4):
    bsz, seq, _ = q.shape
    w = heads * SB_HEAD_DIM
    groups = SB_HEADS // heads
    tri = (jnp.arange(LANES)[:, None] >= jnp.arange(LANES)[None, :]).astype(BF16)
    half = jnp.concatenate([tri, jnp.ones((LANES, LANES), BF16)], axis=1)
    w_sum = jnp.concatenate([half, half], axis=0)
    return pl.pallas_call(
        functools.partial(_sb_attn_kernel, t=t, heads=heads),
        out_shape=jax.ShapeDtypeStruct((bsz, seq, SB_HEADS * SB_HEAD_DIM), BF16),
        grid=(bsz, groups, seq // t),
        in_specs=[pl.BlockSpec((1, t, w), lambda b, g, i: (b, i, g)),
                  pl.BlockSpec((1, seq, w), lambda b, g, i: (b, 0, g)),
                  pl.BlockSpec((1, seq, w), lambda b, g, i: (b, 0, groups + g)),
                  pl.BlockSpec(w_sum.shape, lambda b, g, i: (0, 0))],
        out_specs=pl.BlockSpec((1, t, w), lambda b, g, i: (b, i, g)),
        scratch_shapes=[pltpu.VMEM((heads, t, LANES), F32), pltpu.VMEM((heads, t, SB_HEAD_DIM), F32)],
        compiler_params=_cparams(("parallel", "parallel", "arbitrary")),
        name="sb_attn",
    )(q, kv, kv, w_sum)


def _out_proj_kernel(*refs, n_experts):
    if n_experts:
        (o_ref, w_ref, x_ref, gt_ref, g_ref, sh_ref, sc_ref, wr_hi_ref, wr_lo_ref, br_ref, tri_ref,
         xo_ref, h_ref, rt_ref, cnt_ref) = refs
    else:
        o_ref, w_ref, x_ref, gt_ref, g_ref, sh_ref, sc_ref, xo_ref, h_ref = refs
    x = x_ref[...] + gt_ref[...] * _dot(o_ref[...], w_ref[...])
    xo_ref[...] = x
    h = _modulate(_rms(x, g_ref[...]), sh_ref[...], sc_ref[...])
    h_hi = h.astype(BF16)
    if not n_experts:
        h_ref[...] = h_hi
        return
    h_ref[...] = h
    h_lo = (h - h_hi.astype(F32)).astype(BF16)
    logits = (_dot(h_hi, wr_hi_ref[...]) + _dot(h_lo, wr_hi_ref[...]) + _dot(h_hi, wr_lo_ref[...])
              + br_ref[...])
    lane = lax.broadcasted_iota(jnp.int32, logits.shape, 1).astype(F32)
    lg = jnp.where(lane < n_experts, logits, -jnp.inf)
    m1 = jnp.max(lg, axis=1, keepdims=True)
    i1 = jnp.min(jnp.where(lg == m1, lane, float(LANES)), axis=1, keepdims=True)
    lg2 = jnp.where(lane == i1, -jnp.inf, lg)
    m2 = jnp.max(lg2, axis=1, keepdims=True)
    i2 = jnp.min(jnp.where(lg2 == m2, lane, float(LANES)), axis=1, keepdims=True)
    e2 = jnp.exp(m2 - m1)
    den = 1.0 + e2
    sel = jnp.where((lane == i1) | (lane == i2), 1.0, 0.0)
    prefix = _dot(tri_ref[...], sel.astype(BF16))
    r1 = jnp.sum(jnp.where(lane == i1, prefix, 0.0), axis=1, keepdims=True)
    r2 = jnp.sum(jnp.where(lane == i2, prefix, 0.0), axis=1, keepdims=True)
    cnt_ref[...] = jnp.sum(sel, axis=0, keepdims=True)
    rt = jnp.zeros_like(logits)
    for k, val in enumerate((i1, i2, 1.0 / den, e2 / den, r1, r2)):
        rt = jnp.where(lane == k, val, rt)
    rt_ref[...] = rt


def _out_proj(o, w, x, mod, g, router=None, *, seq, tm):
    n, d = x.shape
    tpb = seq // tm
    full = lambda a: pl.BlockSpec(a.shape, lambda i: (0,) * a.ndim)
    row = lambda wd: pl.BlockSpec((tm, wd), lambda i: (i, 0))
    in_specs = [row(o.shape[1]), full(w), row(d), _mod_spec(2, tpb, d), full(g),
                _mod_spec(3, tpb, d), _mod_spec(4, tpb, d)]
    args = [o, w, x, mod, g, mod, mod]
    out_shape = [jax.ShapeDtypeStruct((n, d), F32), jax.ShapeDtypeStruct((n, d), BF16)]
    out_specs = [row(d), row(d)]
    n_experts = 0
    if router is not None:
        wr_hi, wr_lo, br, n_experts = router
        tri = (jnp.arange(tm)[:, None] > jnp.arange(tm)[None, :]).astype(BF16)
        in_specs += [full(wr_hi), full(wr_lo), full(br), full(tri)]
        args += [wr_hi, wr_lo, br, tri]
        out_shape[1] = jax.ShapeDtypeStruct((n, d), F32)
        out_shape += [jax.ShapeDtypeStruct((n, LANES), F32), jax.ShapeDtypeStruct((n // tm, 1, LANES), F32)]
        out_specs += [row(LANES), pl.BlockSpec((None, 1, LANES), lambda i: (i, 0, 0))]
    return pl.pallas_call(
        functools.partial(_out_proj_kernel, n_experts=n_experts),
        out_shape=tuple(out_shape),
        grid=(n // tm,),
        in_specs=in_specs,
        out_specs=tuple(out_specs),
        compiler_params=_cparams(("parallel",)),
        name="out_proj_router" if n_experts else "out_proj",
    )(*args)


def _silu(g):
    return g * (1.0 / (1.0 + jnp.exp(-g)))


def _ffn_kernel(h_ref, wg_ref, wu_ref, wd_ref, x_ref, gt_ref, gkv_ref, shk_ref, sck_ref,
                gm_ref, shm_ref, scm_ref, xo_ref, hk_ref, hm_ref, acc_ref):
    f = pl.program_id(1)

    @pl.when(f == 0)
    def _():
        acc_ref[...] = jnp.zeros(acc_ref.shape, F32)

    h = h_ref[...]
    a = _silu(_dot(h, wg_ref[...])) * _dot(h, wu_ref[...])
    acc_ref[...] += _dot(a.astype(BF16), wd_ref[...])

    @pl.when(f == pl.num_programs(1) - 1)
    def _():
        x = x_ref[...] + gt_ref[...] * acc_ref[...]
        xo_ref[...] = x
        hk_ref[...] = _modulate(_rms(x, gkv_ref[...]), shk_ref[...], sck_ref[...]).astype(BF16)
        hm_ref[...] = _modulate(_rms(x, gm_ref[...]), shm_ref[...], scm_ref[...]).astype(BF16)


def _ffn(h, w_gu, w_down, x, mod0, g_kv, mod_kv, g_mix1, mod1, *, seq, tm, tf):
    n, d = x.shape
    d_ff = w_down.shape[0]
    nf = d_ff // tf
    tpb = seq // tm
    full = lambda a: pl.BlockSpec(a.shape, lambda i, f: (0,) * a.ndim)
    row = lambda: pl.BlockSpec((tm, d), lambda i, f: (i, 0))
    return pl.pallas_call(
        _ffn_kernel,
        out_shape=(jax.ShapeDtypeStruct((n, d), F32), jax.ShapeDtypeStruct((n, d), BF16),
                   jax.ShapeDtypeStruct((n, d), BF16)),
        grid=(n // tm, nf),
        in_specs=[row(),
                  pl.BlockSpec((d, tf), lambda i, f: (0, f)),
                  pl.BlockSpec((d, tf), lambda i, f: (0, nf + f)),
                  pl.BlockSpec((tf, d), lambda i, f: (f, 0)),
                  row(), _mod_spec(5, tpb, d),
                  full(g_kv), _mod_spec(0, tpb, d), _mod_spec(1, tpb, d),
                  full(g_mix1), _mod_spec(0, tpb, d), _mod_spec(1, tpb, d)],
        out_specs=(row(), row(), row()),
        scratch_shapes=[pltpu.VMEM((tm, d), F32)],
        compiler_params=_cparams(("parallel", "arbitrary")),
        name="ffn_dense",
    )(h, w_gu, w_gu, w_down, x, mod0, g_kv, mod_kv, mod_kv, g_mix1, mod1, mod1)


def _linear_kernel(x_ref, w_ref, o_ref):
    o_ref[...] = _dot(x_ref[...], w_ref[...]).astype(o_ref.dtype)


def _linear(x, w, *, tm, tn, name):
    n, k = x.shape
    m = w.shape[1]
    return pl.pallas_call(
        _linear_kernel,
        out_shape=jax.ShapeDtypeStruct((n, m), BF16),
        grid=(m // tn, n // tm),
        in_specs=[pl.BlockSpec((tm, k), lambda j, i: (i, 0)), pl.BlockSpec((k, tn), lambda j, i: (0, j))],
        out_specs=pl.BlockSpec((tm, tn), lambda j, i: (i, j)),
        compiler_params=_cparams(("parallel", "parallel")),
        name=name,
    )(x, w)


def _route_plan(rt, counts, n_exp, tm, n_tiles):
    cnt = counts[:, 0, :n_exp].astype(jnp.int32)
    sizes = jnp.sum(cnt, axis=0)
    padded = (sizes + tm - 1) // tm * tm
    ends = jnp.cumsum(padded)
    tile_base = (ends - padded)[None, :] + jnp.cumsum(cnt, axis=0) - cnt
    base = jnp.repeat(tile_base, tm, axis=0)
    experts = jnp.arange(n_exp, dtype=jnp.int32)[None, :]
    dest = []
    for k in range(TOP_K):
        e_k = rt[:, k].astype(jnp.int32)
        r_k = rt[:, 2 * TOP_K + k].astype(jnp.int32)
        dest.append(jnp.sum(jnp.where(e_k[:, None] == experts, base, 0), axis=1) + r_k)
    dest = jnp.stack(dest, axis=1).reshape(-1)
    tile_start = jnp.arange(n_tiles, dtype=jnp.int32) * tm
    tile_expert = jnp.minimum(jnp.sum(tile_start[:, None] >= ends[None, :], axis=1), n_exp - 1)
    n_used = (ends[-1] // tm).reshape(1)
    return dest, tile_expert.astype(jnp.int32), n_used.astype(jnp.int32)


def _dispatch_kernel(dest_ref, h_ref, xg_in_ref, xg_ref, sem, *, tm):
    del xg_in_ref
    base = pl.program_id(0) * (TOP_K * tm)

    def issue(r, carry):
        for k in range(TOP_K):
            d = dest_ref[base + TOP_K * r + k]
            pltpu.make_async_copy(h_ref.at[pl.ds(r, 1)], xg_ref.at[pl.ds(d, 1)], sem).start()
        return carry

    lax.fori_loop(0, tm, issue, 0)
    rows = xg_ref.at[pl.ds(0, TOP_K * tm)]
    pltpu.make_async_copy(rows, rows, sem).wait()


def _dispatch(dest, h, n_rows, *, tm):
    n, d = h.shape
    return pl.pallas_call(
        functools.partial(_dispatch_kernel, tm=tm),
        out_shape=jax.ShapeDtypeStruct((n_rows, d), F32),
        grid_spec=pltpu.PrefetchScalarGridSpec(
            num_scalar_prefetch=1,
            grid=(n // tm,),
            in_specs=[pl.BlockSpec((tm, d), lambda i, dest: (i, 0)),
                      pl.BlockSpec(memory_space=pl.ANY)],
            out_specs=pl.BlockSpec(memory_space=pl.ANY),
            scratch_shapes=[pltpu.SemaphoreType.DMA(())]),
        input_output_aliases={2: 0},
        compiler_params=_cparams(("arbitrary",)),
        name="moe_dispatch",
    )(dest, h, jnp.zeros((n_rows, d), F32))


def _grouped_ffn_kernel(te_ref, nu_ref, x_ref, wg_ref, wu_ref, wd_ref, y_ref, hb_ref, acc_ref):
    del te_ref
    i = pl.program_id(0)
    f = pl.program_id(1)
    used = i < nu_ref[0]
    last_f = f == pl.num_programs(1) - 1

    @pl.when(used & (f == 0))
    def _():
        hb_ref[...] = x_ref[...].astype(BF16)
        acc_ref[...] = jnp.zeros(acc_ref.shape, F32)

    @pl.when(used)
    def _():
        h = hb_ref[...]
        a = _silu(_dot(h, wg_ref[0])) * _dot(h, wu_ref[0])
        acc_ref[...] += _dot(a.astype(BF16), wd_ref[0])

    @pl.when(used & last_f)
    def _():
        y_ref[...] = acc_ref[...]

    @pl.when(jnp.logical_not(used) & last_f)
    def _():
        y_ref[...] = jnp.zeros(y_ref.shape, F32)


def _grouped_ffn(tile_expert, n_used, xg, w_gu, w_down, *, tm, tf):
    n_rows, _ = xg.shape
    n_exp, d_ff, d = w_down.shape
    nf = d_ff // tf

    def wspec(shape, index):
        def index_map(i, f, te, nu):
            return index(te[i], jnp.where(i < nu[0], f, nf - 1))
        return pl.BlockSpec(shape, index_map)

    return pl.pallas_call(
        _grouped_ffn_kernel,
        out_shape=jax.ShapeDtypeStruct((n_rows, d), F32),
        grid_spec=pltpu.PrefetchScalarGridSpec(
            num_scalar_prefetch=2,
            grid=(n_rows // tm, nf),
            in_specs=[pl.BlockSpec((tm, d), lambda i, f, te, nu: (i, 0)),
                      wspec((1, d, tf), lambda e, f: (e, 0, f)),
                      wspec((1, d, tf), lambda e, f: (e, 0, nf + f)),
                      wspec((1, tf, d), lambda e, f: (e, f, 0))],
            out_specs=pl.BlockSpec((tm, d), lambda i, f, te, nu: (i, 0)),
            scratch_shapes=[pltpu.VMEM((tm, d), BF16), pltpu.VMEM((tm, d), F32)]),
        compiler_params=_cparams(("arbitrary", "arbitrary")),
        name="moe_grouped_ffn",
    )(tile_expert, n_used, xg, w_gu, w_gu, w_down)


def _combine_kernel(dest_ref, x_ref, rt_ref, gt_ref, gf_ref, yg_ref, o_ref, y_ref, sem, *, tm):
    i = pl.program_id(0)
    slot = i % 2

    def issue(step, into):
        base = step * (TOP_K * tm)

        def body(r, carry):
            for k in range(TOP_K):
                d = dest_ref[base + TOP_K * r + k]
                pltpu.make_async_copy(yg_ref.at[pl.ds(d, 1)], y_ref.at[into, k, pl.ds(r, 1)],
                                      sem.at[into, k]).start()
            return carry

        lax.fori_loop(0, tm, body, 0)

    @pl.when(i == 0)
    def _():
        issue(0, 0)

    @pl.when(i + 1 < pl.num_programs(0))
    def _():
        issue(i + 1, 1 - slot)

    rt = rt_ref[...]
    tot = None
    for k in range(TOP_K):
        rows = y_ref.at[slot, k]
        pltpu.make_async_copy(rows, rows, sem.at[slot, k]).wait()
        term = rt[:, TOP_K + k:TOP_K + k + 1] * y_ref[slot, k]
        tot = term if tot is None else tot + term
    x = x_ref[...] + gt_ref[...] * tot
    o_ref[...] = _rms(x, gf_ref[...])


def _combine(dest, x, rt, mod1, g_final, yg, *, seq, tm):
    n, d = x.shape
    tpb = seq // tm
    row = lambda w: pl.BlockSpec((tm, w), lambda i, dest: (i, 0))
    return pl.pallas_call(
        functools.partial(_combine_kernel, tm=tm),
        out_shape=jax.ShapeDtypeStruct((n, d), F32),
        grid_spec=pltpu.PrefetchScalarGridSpec(
            num_scalar_prefetch=1,
            grid=(n // tm,),
            in_specs=[row(d), row(LANES), _mod_spec(5, tpb, d),
                      pl.BlockSpec(g_final.shape, lambda i, dest: (0, 0)),
                      pl.BlockSpec(memory_space=pl.ANY)],
            out_specs=row(d),
            scratch_shapes=[pltpu.VMEM((2, TOP_K, tm, d), F32), pltpu.SemaphoreType.DMA((2, TOP_K))]),
        compiler_params=_cparams(("arbitrary",)),
        name="moe_combine",
    )(dest, x, rt, mod1, g_final, yg)


def _pad_last(a, width):
    return jnp.pad(a, [(0, 0)] * (a.ndim - 1) + [(0, width - a.shape[-1])])


def kernel(x, c, positions, w_mod, b_mod, g_mix, g_ffn, w_a_down, g_q_lat, g_kv_lat, w_uq, w_ukv, w_oa,
           w_mod_kv, b_mod_kv, g_kv, w_kv_sb, w_q_sb, w_o_sb, w_ffn_gu, w_ffn_down, w_router, b_router,
           w_exp_gu, w_exp_down, g_final):
    bsz, seq, d = x.shape
    n = bsz * seq
    q_lora, kv_lora = g_q_lat.shape[1], g_kv_lat.shape[1]
    n_exp = w_router.shape[-1]
    d_ff = w_ffn_down.shape[1]
    tm = min(TOKEN_TILE, seq)
    t_mla = min(MLA_TILE, seq)
    t_sb = min(SB_TILE, seq)
    tf = next((t for t in (FFN_CHUNK, 512) if d_ff % t == 0), d_ff)

    mod0 = _modvec(c, w_mod[0], b_mod[0]).reshape(bsz, 6, 1, d)
    mod1 = _modvec(c, w_mod[1], b_mod[1]).reshape(bsz, 6, 1, d)
    mod_kv = _modvec(c, w_mod_kv, b_mod_kv).reshape(bsz, 2, 1, d)

    lat_w = q_lora + kv_lora + LANES
    wd = _pad_last(w_a_down[0], lat_w).astype(BF16)
    wq = w_uq[0].reshape(q_lora, MLA_HEADS, QK_NOPE + QK_ROPE)
    wqn = wq[:, :, :QK_NOPE].reshape(q_lora, MLA_HEADS * QK_NOPE).astype(BF16)
    wqr = _pad_last(wq[:, :, QK_NOPE:], LANES).reshape(q_lora, MLA_HEADS * LANES).astype(BF16)
    wkv = w_ukv[0].reshape(kv_lora, MLA_HEADS, QK_NOPE + V_DIM)
    wkn = wkv[:, :, :QK_NOPE].reshape(kv_lora, MLA_HEADS * QK_NOPE).astype(BF16)
    wv = wkv[:, :, QK_NOPE:].reshape(kv_lora, MLA_HEADS * V_DIM).astype(BF16)
    half = QK_ROPE // 2
    inv = ROPE_THETA ** (-jnp.arange(half, dtype=F32) / half)
    inv = _pad_last(jnp.concatenate([inv, inv]), LANES).reshape(1, LANES)
    wr = _pad_last(w_router[0], LANES)
    wr_hi = wr.astype(BF16)
    wr_lo = (wr - wr_hi.astype(F32)).astype(BF16)
    br = _pad_last(b_router[0], LANES).reshape(1, LANES)

    xf = x.reshape(n, d)
    pos = positions.reshape(n, 1)
    row1 = lambda a: a.reshape(1, -1)

    q, k, v = _mla_proj(xf, pos, mod0, row1(g_mix[0]), wd, row1(g_q_lat[0]), row1(g_kv_lat[0]),
                        wqn, wqr, wkn, wv, inv, seq=seq, tm=tm)
    o = _mla_attn(q.reshape(bsz, seq, -1), k.reshape(bsz, seq, -1), v.reshape(bsz, seq, -1), t=t_mla)
    x1, h = _out_proj(o.reshape(n, -1), w_oa[0].astype(BF16), xf, mod0, row1(g_ffn[0]), seq=seq, tm=tm)
    x2, hk, hm = _ffn(h, w_ffn_gu[0].astype(BF16), w_ffn_down[0].astype(BF16), x1, mod0,
                      row1(g_kv), mod_kv, row1(g_mix[1]), mod1, seq=seq, tm=tm, tf=tf)
    kv = _linear(hk, w_kv_sb.astype(BF16), tm=tm, tn=w_kv_sb.shape[1], name="kv_proj")
    q_scale = math.log2(math.e) / math.sqrt(SB_HEAD_DIM)
    qs = _linear(hm, (w_q_sb[0] * q_scale).astype(BF16), tm=tm, tn=1024, name="q_proj")
    o = _sb_attn(qs.reshape(bsz, seq, -1), kv.reshape(bsz, seq, -1), t=t_sb)
    x3, h, rt, counts = _out_proj(o.reshape(n, -1), w_o_sb[0].astype(BF16), x2, mod1, row1(g_ffn[1]),
                                  router=(wr_hi, wr_lo, br, n_exp), seq=seq, tm=tm)
    n_tiles = TOP_K * n // tm + n_exp
    dest, tile_expert, n_used = _route_plan(rt, counts, n_exp, tm, n_tiles)
    xg = _dispatch(dest, h, n_tiles * tm, tm=tm)
    yg = _grouped_ffn(tile_expert, n_used, xg, w_exp_gu[0].astype(BF16), w_exp_down[0].astype(BF16),
                      tm=tm, tf=tf)
    out = _combine(dest, x3, rt, mod1, row1(g_final), yg, seq=seq, tm=tm)
    return out.reshape(bsz, seq, d)
```

```python
import functools
import math

import jax
import jax.numpy as jnp
from jax import lax
from jax.experimental import pallas as pl
from jax.experimental.pallas import tpu as pltpu

F32 = jnp.float32
BF16 = jnp.bfloat16

EPS = 1e-6
MLA_HEADS = 8
QK_NOPE = 128
QK_ROPE = 64
V_DIM = 128
ROPE_THETA = 10000.0
SB_HEADS = 8
SB_HEAD_DIM = 128
TOP_K = 2

LANES = 128
QK_PAD = 256
VMEM_LIMIT = 48 * 1024 * 1024
TOKEN_TILE = 512
MLA_TILE = 512
SB_TILE = 512
FFN_CHUNK = 1792


def _cparams(sem):
    return pltpu.CompilerParams(dimension_semantics=sem, vmem_limit_bytes=VMEM_LIMIT)


def _rms(x, g):
    return x * lax.rsqrt(jnp.mean(x * x, axis=-1, keepdims=True) + EPS) * g


def _modulate(h, shift, scale):
    return h * (1.0 + scale) + shift


def _split_bf16(a):
    hi = a.astype(BF16)
    lo = (a - hi.astype(F32)).astype(BF16)
    return hi, lo


U32 = jnp.uint32
HIGH_HALF = 0xFFFF0000


def _pack_bf16_pairs(a):
    half = a.shape[1] // 2
    bits = lambda v: lax.bitcast_convert_type(v.astype(BF16).astype(F32), U32)
    return (bits(a[:, :half]) >> 16) | (bits(a[:, half:]) & U32(HIGH_HALF))


def _unpack_bf16_pairs(w):
    lo = lax.bitcast_convert_type(w << 16, F32)
    hi = lax.bitcast_convert_type(w & U32(HIGH_HALF), F32)
    return jnp.concatenate([lo, hi], axis=1)


def _dot(a, b):
    return jnp.dot(a, b, preferred_element_type=F32)


def _dot_nt(a, b):
    return lax.dot_general(a, b, (((1,), (1,)), ((), ())), preferred_element_type=F32)


def _modvec_kernel(c_ref, w_ref, b_ref, o_ref):
    c = c_ref[...]
    sc = c * (1.0 / (1.0 + jnp.exp(-c)))
    a_hi, a_lo = _split_bf16(sc)
    w_hi, w_lo = _split_bf16(w_ref[...])
    o_ref[...] = _dot(a_hi, w_hi) + _dot(a_lo, w_hi) + _dot(a_hi, w_lo) + b_ref[...]


def _modvec(c, w, b, tn=512):
    bsz, d = c.shape
    n = w.shape[1]
    return pl.pallas_call(
        _modvec_kernel,
        out_shape=jax.ShapeDtypeStruct((bsz, n), F32),
        grid=(n // tn,),
        in_specs=[pl.BlockSpec((bsz, d), lambda j: (0, 0)),
                  pl.BlockSpec((d, tn), lambda j: (0, j)),
                  pl.BlockSpec((1, tn), lambda j: (0, j))],
        out_specs=pl.BlockSpec((bsz, tn), lambda j: (0, j)),
        compiler_params=_cparams(("arbitrary",)),
        name="modvec",
    )(c, w, b.reshape(1, n))


def _mod_spec(chunk, tiles_per_batch, d):
    return pl.BlockSpec((None, None, 1, d), lambda i, *_: (i // tiles_per_batch, chunk, 0, 0))


def _mla_proj_kernel(x_ref, pos_ref, sh_ref, sc_ref, g_ref, wd_ref, gq_ref, gkv_ref,
                     wqn_ref, wqr_ref, wkn_ref, wv_ref, inv_ref,
                     q_ref, k_ref, v_ref, *, q_lora, kv_lora):
    x = x_ref[...]
    h = _modulate(_rms(x, g_ref[...]), sh_ref[...], sc_ref[...]).astype(BF16)
    lat = _dot(h, wd_ref[...])
    c_q = _rms(lat[:, :q_lora], gq_ref[...]).astype(BF16)
    c_kv = _rms(lat[:, q_lora:q_lora + kv_lora], gkv_ref[...]).astype(BF16)
    k_rot = lat[:, q_lora + kv_lora:]

    half = QK_ROPE // 2
    ang = pos_ref[...].astype(F32) * inv_ref[...]
    cos = jnp.cos(ang)
    sin = jnp.sin(ang)
    lane = lax.broadcasted_iota(jnp.int32, ang.shape, 1)
    s_a = jnp.where(lane < half, -sin, 0.0)
    s_b = jnp.where((lane >= half) & (lane < 2 * half), sin, 0.0)

    def rope(r):
        return (r * cos + pltpu.roll(r, LANES - half, axis=1) * s_a
                + pltpu.roll(r, half, axis=1) * s_b)

    k_rot = rope(k_rot).astype(BF16)
    q_nope = _dot(c_q, wqn_ref[...])
    q_rope = _dot(c_q, wqr_ref[...])
    k_nope = _dot(c_kv, wkn_ref[...])
    v_ref[...] = _dot(c_kv, wv_ref[...]).astype(BF16)
    for hd in range(MLA_HEADS):
        a, b = hd * LANES, (hd + 1) * LANES
        q_ref[:, hd * QK_PAD:hd * QK_PAD + LANES] = q_nope[:, a:b].astype(BF16)
        q_ref[:, hd * QK_PAD + LANES:(hd + 1) * QK_PAD] = rope(q_rope[:, a:b]).astype(BF16)
        k_ref[:, hd * QK_PAD:hd * QK_PAD + LANES] = k_nope[:, a:b].astype(BF16)
        k_ref[:, hd * QK_PAD + LANES:(hd + 1) * QK_PAD] = k_rot


def _mla_proj(x, pos, mod, g_mix, wd, gq, gkv, wqn, wqr, wkn, wv, inv, *, seq, tm):
    n, d = x.shape
    tpb = seq // tm
    q_lora, kv_lora = gq.shape[1], gkv.shape[1]
    hq = MLA_HEADS * QK_PAD
    hv = MLA_HEADS * V_DIM
    full = lambda a: pl.BlockSpec(a.shape, lambda i: (0,) * a.ndim)
    row = lambda w: pl.BlockSpec((tm, w), lambda i: (i, 0))
    return pl.pallas_call(
        functools.partial(_mla_proj_kernel, q_lora=q_lora, kv_lora=kv_lora),
        out_shape=(jax.ShapeDtypeStruct((n, hq), BF16), jax.ShapeDtypeStruct((n, hq), BF16),
                   jax.ShapeDtypeStruct((n, hv), BF16)),
        grid=(n // tm,),
        in_specs=[row(d), row(1), _mod_spec(0, tpb, d), _mod_spec(1, tpb, d), full(g_mix), full(wd),
                  full(gq), full(gkv), full(wqn), full(wqr), full(wkn), full(wv), full(inv)],
        out_specs=(row(hq), row(hq), row(hv)),
        compiler_params=_cparams(("parallel",)),
        name="mla_proj",
    )(x, pos, mod, mod, g_mix, wd, gq, gkv, wqn, wqr, wkn, wv, inv)


def _lane_chunks(a):
    return [a[:, c * LANES:(c + 1) * LANES] for c in range(a.shape[1] // LANES)]


def _for_blocks(n, step):
    def pair(p, carry):
        step(2 * p, 2)
        return carry

    lax.fori_loop(0, n // 2, pair, 0)

    @pl.when(n % 2 == 1)
    def _():
        step(n - 1, 1)


def _mla_attn_kernel(q_ref, k_ref, v_ref, o_ref, s_ref, m_ref, acc_ref, *, t, scale, heads):
    i = pl.program_id(2)

    def lane_max(s):
        m = None
        for sc in _lane_chunks(s):
            m = sc if m is None else jnp.maximum(m, sc)
        return m

    def scores(h, j):
        start = pl.multiple_of(j * t, t)
        hs = slice(h * QK_PAD, (h + 1) * QK_PAD)
        return _dot_nt(q_ref[0, :, hs], k_ref[0, pl.ds(start, t), hs])

    r = lax.broadcasted_iota(jnp.int32, (t, t), 0)
    c = lax.broadcasted_iota(jnp.int32, (t, t), 1)
    for h in range(heads):
        s = jnp.where(c <= r, scores(h, i), -jnp.inf)
        s_ref[h, i] = s
        m_ref[h] = lane_max(s)

    def pass1(j, nb):
        for h in range(heads):
            m = m_ref[h]
            for jj in range(nb):
                s = scores(h, j + jj)
                s_ref[h, j + jj] = s
                m = jnp.maximum(m, lane_max(s))
            m_ref[h] = m

    _for_blocks(i, pass1)
    for h in range(heads):
        m_ref[h] = jnp.broadcast_to(jnp.max(m_ref[h], axis=1, keepdims=True), (t, LANES))
    acc_ref[...] = jnp.zeros(acc_ref.shape, F32)
    cst = scale * math.log2(math.e)

    def pass2(j, nb):
        start = pl.multiple_of(j * t, t)
        ones = jnp.ones((nb * t, LANES), BF16)
        for h in range(heads):
            m = m_ref[h]
            p = jnp.concatenate([jnp.exp2((sc - m) * cst).astype(BF16)
                                 for jj in range(nb) for sc in _lane_chunks(s_ref[h, j + jj])], axis=1)
            v_ext = jnp.concatenate([v_ref[0, pl.ds(start, nb * t), h * V_DIM:(h + 1) * V_DIM], ones], axis=1)
            acc_ref[h] += _dot(p, v_ext)

    _for_blocks(i + 1, pass2)
    for h in range(heads):
        acc = acc_ref[h]
        o_ref[0, :, h * V_DIM:(h + 1) * V_DIM] = (acc[:, :V_DIM] / acc[:, V_DIM:]).astype(o_ref.dtype)


def _mla_attn(q, k, v, *, t, heads=2):
    bsz, seq, _ = q.shape
    scale = 1.0 / math.sqrt(QK_NOPE + QK_ROPE)
    wq, wv = heads * QK_PAD, heads * V_DIM
    return pl.pallas_call(
        functools.partial(_mla_attn_kernel, t=t, scale=scale, heads=heads),
        out_shape=jax.ShapeDtypeStruct((bsz, seq, MLA_HEADS * V_DIM), BF16),
        grid=(bsz, MLA_HEADS // heads, seq // t),
        in_specs=[pl.BlockSpec((1, t, wq), lambda b, g, i: (b, i, g)),
                  pl.BlockSpec((1, seq, wq), lambda b, g, i: (b, 0, g)),
                  pl.BlockSpec((1, seq, wv), lambda b, g, i: (b, 0, g))],
        out_specs=pl.BlockSpec((1, t, wv), lambda b, g, i: (b, i, g)),
        scratch_shapes=[pltpu.VMEM((heads, seq // t, t, t), F32), pltpu.VMEM((heads, t, LANES), F32),
                        pltpu.VMEM((heads, t, V_DIM + LANES), F32)],
        compiler_params=_cparams(("parallel", "parallel", "arbitrary")),
        name="mla_attn",
    )(q, k, v)


def _sb_attn_kernel(q_ref, k_ref, v_ref, w_ref, o_ref, c_ref, acc_ref, *, t, heads):
    i = pl.program_id(2)
    nl = t // LANES
    d = SB_HEAD_DIM
    c_ref[...] = jnp.zeros(c_ref.shape, F32)
    acc_ref[...] = jnp.zeros(acc_ref.shape, F32)
    r = lax.broadcasted_iota(jnp.int32, (t, t), 0)
    c = lax.broadcasted_iota(jnp.int32, (t, t), 1)
    strict = c < r

    def step(j, masked):
        start = pl.multiple_of(j * t, t)
        for h in range(heads):
            hs = slice(h * d, (h + 1) * d)
            zz = _dot_nt(q_ref[0, :, hs], k_ref[0, pl.ds(start, t), hs])
            zneg = -zz
            sp = jnp.log2(1.0 + jnp.exp2(jnp.minimum(zz, zneg)))
            log_1m = jnp.minimum(zneg, 0.0) - sp
            if masked:
                log_1m = jnp.where(strict, log_1m, 0.0)
            hi, lo = _split_bf16(log_1m)
            carry = c_ref[h]
            a_chunks = [None] * nl
            for cc in reversed(range(nl)):
                cs = slice(cc * LANES, (cc + 1) * LANES)
                y = _dot(jnp.concatenate([hi[:, cs], lo[:, cs]], axis=1), w_ref[...])
                a = jnp.exp2(zz[:, cs] + y[:, :LANES] + carry)
                carry = carry + y[:, LANES:]
                if masked:
                    a = jnp.where(strict[:, cs], a, 0.0)
                a_chunks[cc] = a.astype(BF16)
            c_ref[h] = carry
            acc_ref[h] += _dot(jnp.concatenate(a_chunks, axis=1), v_ref[0, pl.ds(start, t), hs])

    step(i, True)

    def body(n, carry):
        step(i - 1 - n, False)
        return carry

    lax.fori_loop(0, i, body, 0)
    for h in range(heads):
        o_ref[0, :, h * d:(h + 1) * d] = acc_ref[h].astype(o_ref.dtype)


def _sb_attn(q, kv, *, t, heads=4):
    bsz, seq, _ = q.shape
    w = heads * SB_HEAD_DIM
    groups = SB_HEADS // heads
    tri = (jnp.arange(LANES)[:, None] >= jnp.arange(LANES)[None, :]).astype(BF16)
    half = jnp.concatenate([tri, jnp.ones((LANES, LANES), BF16)], axis=1)
    w_sum = jnp.concatenate([half, half], axis=0)
    return pl.pallas_call(
        functools.partial(_sb_attn_kernel, t=t, heads=heads),
        out_shape=jax.ShapeDtypeStruct((bsz, seq, SB_HEADS * SB_HEAD_DIM), BF16),
        grid=(bsz, groups, seq // t),
        in_specs=[pl.BlockSpec((1, t, w), lambda b, g, i: (b, i, g)),
                  pl.BlockSpec((1, seq, w), lambda b, g, i: (b, 0, g)),
                  pl.BlockSpec((1, seq, w), lambda b, g, i: (b, 0, groups + g)),
                  pl.BlockSpec(w_sum.shape, lambda b, g, i: (0, 0))],
        out_specs=pl.BlockSpec((1, t, w), lambda b, g, i: (b, i, g)),
        scratch_shapes=[pltpu.VMEM((heads, t, LANES), F32), pltpu.VMEM((heads, t, SB_HEAD_DIM), F32)],
        compiler_params=_cparams(("parallel", "parallel", "arbitrary")),
        name="sb_attn",
    )(q, kv, kv, w_sum)


def _out_proj_kernel(*refs, n_experts):
    if n_experts:
        (o_ref, w_ref, x_ref, gt_ref, g_ref, sh_ref, sc_ref, wr_hi_ref, wr_lo_ref, br_ref, tri_ref,
         xo_ref, h_ref, rt_ref, cnt_ref) = refs
    else:
        o_ref, w_ref, x_ref, gt_ref, g_ref, sh_ref, sc_ref, xo_ref, h_ref = refs
    x = x_ref[...] + gt_ref[...] * _dot(o_ref[...], w_ref[...])
    xo_ref[...] = x
    h = _modulate(_rms(x, g_ref[...]), sh_ref[...], sc_ref[...])
    h_hi = h.astype(BF16)
    if not n_experts:
        h_ref[...] = h_hi
        return
    h_ref[...] = _pack_bf16_pairs(h)
    h_lo = (h - h_hi.astype(F32)).astype(BF16)
    logits = (_dot(h_hi, wr_hi_ref[...]) + _dot(h_lo, wr_hi_ref[...]) + _dot(h_hi, wr_lo_ref[...])
              + br_ref[...])
    lane = lax.broadcasted_iota(jnp.int32, logits.shape, 1).astype(F32)
    lg = jnp.where(lane < n_experts, logits, -jnp.inf)
    m1 = jnp.max(lg, axis=1, keepdims=True)
    i1 = jnp.min(jnp.where(lg == m1, lane, float(LANES)), axis=1, keepdims=True)
    lg2 = jnp.where(lane == i1, -jnp.inf, lg)
    m2 = jnp.max(lg2, axis=1, keepdims=True)
    i2 = jnp.min(jnp.where(lg2 == m2, lane, float(LANES)), axis=1, keepdims=True)
    e2 = jnp.exp(m2 - m1)
    den = 1.0 + e2
    sel = jnp.where((lane == i1) | (lane == i2), 1.0, 0.0)
    prefix = _dot(tri_ref[...], sel.astype(BF16))
    r1 = jnp.sum(jnp.where(lane == i1, prefix, 0.0), axis=1, keepdims=True)
    r2 = jnp.sum(jnp.where(lane == i2, prefix, 0.0), axis=1, keepdims=True)
    cnt_ref[...] = jnp.sum(sel, axis=0, keepdims=True)
    rt = jnp.zeros_like(logits)
    for k, val in enumerate((i1, i2, 1.0 / den, e2 / den, r1, r2)):
        rt = jnp.where(lane == k, val, rt)
    rt_ref[...] = rt


def _out_proj(o, w, x, mod, g, router=None, *, seq, tm):
    n, d = x.shape
    tpb = seq // tm
    full = lambda a: pl.BlockSpec(a.shape, lambda i: (0,) * a.ndim)
    row = lambda wd: pl.BlockSpec((tm, wd), lambda i: (i, 0))
    in_specs = [row(o.shape[1]), full(w), row(d), _mod_spec(2, tpb, d), full(g),
                _mod_spec(3, tpb, d), _mod_spec(4, tpb, d)]
    args = [o, w, x, mod, g, mod, mod]
    out_shape = [jax.ShapeDtypeStruct((n, d), F32), jax.ShapeDtypeStruct((n, d), BF16)]
    out_specs = [row(d), row(d)]
    n_experts = 0
    if router is not None:
        wr_hi, wr_lo, br, n_experts = router
        tri = (jnp.arange(tm)[:, None] > jnp.arange(tm)[None, :]).astype(BF16)
        in_specs += [full(wr_hi), full(wr_lo), full(br), full(tri)]
        args += [wr_hi, wr_lo, br, tri]
        out_shape[1] = jax.ShapeDtypeStruct((n, d // 2), U32)
        out_specs[1] = row(d // 2)
        out_shape += [jax.ShapeDtypeStruct((n, LANES), F32), jax.ShapeDtypeStruct((n // tm, 1, LANES), F32)]
        out_specs += [row(LANES), pl.BlockSpec((None, 1, LANES), lambda i: (i, 0, 0))]
    return pl.pallas_call(
        functools.partial(_out_proj_kernel, n_experts=n_experts),
        out_shape=tuple(out_shape),
        grid=(n // tm,),
        in_specs=in_specs,
        out_specs=tuple(out_specs),
        compiler_params=_cparams(("parallel",)),
        name="out_proj_router" if n_experts else "out_proj",
    )(*args)


def _silu(g):
    return g * (1.0 / (1.0 + jnp.exp(-g)))


def _ffn_kernel(h_ref, wg_ref, wu_ref, wd_ref, x_ref, gt_ref, gkv_ref, shk_ref, sck_ref,
                gm_ref, shm_ref, scm_ref, xo_ref, hk_ref, hm_ref, acc_ref):
    f = pl.program_id(1)

    @pl.when(f == 0)
    def _():
        acc_ref[...] = jnp.zeros(acc_ref.shape, F32)

    h = h_ref[...]
    a = _silu(_dot(h, wg_ref[...])) * _dot(h, wu_ref[...])
    acc_ref[...] += _dot(a.astype(BF16), wd_ref[...])

    @pl.when(f == pl.num_programs(1) - 1)
    def _():
        x = x_ref[...] + gt_ref[...] * acc_ref[...]
        xo_ref[...] = x
        hk_ref[...] = _modulate(_rms(x, gkv_ref[...]), shk_ref[...], sck_ref[...]).astype(BF16)
        hm_ref[...] = _modulate(_rms(x, gm_ref[...]), shm_ref[...], scm_ref[...]).astype(BF16)


def _ffn(h, w_gu, w_down, x, mod0, g_kv, mod_kv, g_mix1, mod1, *, seq, tm, tf):
    n, d = x.shape
    d_ff = w_down.shape[0]
    nf = d_ff // tf
    tpb = seq // tm
    full = lambda a: pl.BlockSpec(a.shape, lambda i, f: (0,) * a.ndim)
    row = lambda: pl.BlockSpec((tm, d), lambda i, f: (i, 0))
    return pl.pallas_call(
        _ffn_kernel,
        out_shape=(jax.ShapeDtypeStruct((n, d), F32), jax.ShapeDtypeStruct((n, d), BF16),
                   jax.ShapeDtypeStruct((n, d), BF16)),
        grid=(n // tm, nf),
        in_specs=[row(),
                  pl.BlockSpec((d, tf), lambda i, f: (0, f)),
                  pl.BlockSpec((d, tf), lambda i, f: (0, nf + f)),
                  pl.BlockSpec((tf, d), lambda i, f: (f, 0)),
                  row(), _mod_spec(5, tpb, d),
                  full(g_kv), _mod_spec(0, tpb, d), _mod_spec(1, tpb, d),
                  full(g_mix1), _mod_spec(0, tpb, d), _mod_spec(1, tpb, d)],
        out_specs=(row(), row(), row()),
        scratch_shapes=[pltpu.VMEM((tm, d), F32)],
        compiler_params=_cparams(("parallel", "arbitrary")),
        name="ffn_dense",
    )(h, w_gu, w_gu, w_down, x, mod0, g_kv, mod_kv, mod_kv, g_mix1, mod1, mod1)


def _linear_kernel(x_ref, w_ref, o_ref):
    o_ref[...] = _dot(x_ref[...], w_ref[...]).astype(o_ref.dtype)


def _linear(x, w, *, tm, tn, name):
    n, k = x.shape
    m = w.shape[1]
    return pl.pallas_call(
        _linear_kernel,
        out_shape=jax.ShapeDtypeStruct((n, m), BF16),
        grid=(m // tn, n // tm),
        in_specs=[pl.BlockSpec((tm, k), lambda j, i: (i, 0)), pl.BlockSpec((k, tn), lambda j, i: (0, j))],
        out_specs=pl.BlockSpec((tm, tn), lambda j, i: (i, j)),
        compiler_params=_cparams(("parallel", "parallel")),
        name=name,
    )(x, w)


def _route_plan(rt, counts, n_exp, tm, n_tiles):
    cnt = counts[:, 0, :n_exp].astype(jnp.int32)
    sizes = jnp.sum(cnt, axis=0)
    padded = (sizes + tm - 1) // tm * tm
    ends = jnp.cumsum(padded)
    tile_base = (ends - padded)[None, :] + jnp.cumsum(cnt, axis=0) - cnt
    base = jnp.repeat(tile_base, tm, axis=0)
    experts = jnp.arange(n_exp, dtype=jnp.int32)[None, :]
    dest = []
    for k in range(TOP_K):
        e_k = rt[:, k].astype(jnp.int32)
        r_k = rt[:, 2 * TOP_K + k].astype(jnp.int32)
        dest.append(jnp.sum(jnp.where(e_k[:, None] == experts, base, 0), axis=1) + r_k)
    dest = jnp.stack(dest, axis=1).reshape(-1)
    tile_start = jnp.arange(n_tiles, dtype=jnp.int32) * tm
    tile_expert = jnp.minimum(jnp.sum(tile_start[:, None] >= ends[None, :], axis=1), n_exp - 1)
    n_used = (ends[-1] // tm).reshape(1)
    return dest, tile_expert.astype(jnp.int32), n_used.astype(jnp.int32)


def _dispatch_kernel(dest_ref, h_ref, xg_in_ref, xg_ref, sem, *, tm):
    del xg_in_ref
    base = pl.program_id(0) * (TOP_K * tm)

    def issue(r, carry):
        for k in range(TOP_K):
            d = dest_ref[base + TOP_K * r + k]
            pltpu.make_async_copy(h_ref.at[pl.ds(r, 1)], xg_ref.at[pl.ds(d, 1)], sem).start()
        return carry

    lax.fori_loop(0, tm, issue, 0)
    rows = xg_ref.at[pl.ds(0, TOP_K * tm)]
    pltpu.make_async_copy(rows, rows, sem).wait()


def _dispatch(dest, h, n_rows, *, tm):
    n, d = h.shape
    return pl.pallas_call(
        functools.partial(_dispatch_kernel, tm=tm),
        out_shape=jax.ShapeDtypeStruct((n_rows, d), h.dtype),
        grid_spec=pltpu.PrefetchScalarGridSpec(
            num_scalar_prefetch=1,
            grid=(n // tm,),
            in_specs=[pl.BlockSpec((tm, d), lambda i, dest: (i, 0)),
                      pl.BlockSpec(memory_space=pl.ANY)],
            out_specs=pl.BlockSpec(memory_space=pl.ANY),
            scratch_shapes=[pltpu.SemaphoreType.DMA(())]),
        input_output_aliases={2: 0},
        compiler_params=_cparams(("arbitrary",)),
        name="moe_dispatch",
    )(dest, h, jnp.zeros((n_rows, d), h.dtype))


def _grouped_ffn_kernel(te_ref, nu_ref, x_ref, wg_ref, wu_ref, wd_ref, y_ref, hb_ref, acc_ref):
    del te_ref
    i = pl.program_id(0)
    f = pl.program_id(1)
    used = i < nu_ref[0]
    last_f = f == pl.num_programs(1) - 1

    @pl.when(used & (f == 0))
    def _():
        hb_ref[...] = _unpack_bf16_pairs(x_ref[...]).astype(BF16)
        acc_ref[...] = jnp.zeros(acc_ref.shape, F32)

    @pl.when(used)
    def _():
        h = hb_ref[...]
        a = _silu(_dot(h, wg_ref[0])) * _dot(h, wu_ref[0])
        acc_ref[...] += _dot(a.astype(BF16), wd_ref[0])

    @pl.when(used & last_f)
    def _():
        y_ref[...] = _pack_bf16_pairs(acc_ref[...])

    @pl.when(jnp.logical_not(used) & last_f)
    def _():
        y_ref[...] = jnp.zeros(y_ref.shape, y_ref.dtype)


def _grouped_ffn(tile_expert, n_used, xg, w_gu, w_down, *, tm, tf):
    n_rows, dp = xg.shape
    n_exp, d_ff, d = w_down.shape
    nf = d_ff // tf

    def wspec(shape, index):
        def index_map(i, f, te, nu):
            return index(te[i], jnp.where(i < nu[0], f, nf - 1))
        return pl.BlockSpec(shape, index_map)

    return pl.pallas_call(
        _grouped_ffn_kernel,
        out_shape=jax.ShapeDtypeStruct((n_rows, dp), xg.dtype),
        grid_spec=pltpu.PrefetchScalarGridSpec(
            num_scalar_prefetch=2,
            grid=(n_rows // tm, nf),
            in_specs=[pl.BlockSpec((tm, dp), lambda i, f, te, nu: (i, 0)),
                      wspec((1, d, tf), lambda e, f: (e, 0, f)),
                      wspec((1, d, tf), lambda e, f: (e, 0, nf + f)),
                      wspec((1, tf, d), lambda e, f: (e, f, 0))],
            out_specs=pl.BlockSpec((tm, dp), lambda i, f, te, nu: (i, 0)),
            scratch_shapes=[pltpu.VMEM((tm, d), BF16), pltpu.VMEM((tm, d), F32)]),
        compiler_params=_cparams(("arbitrary", "arbitrary")),
        name="moe_grouped_ffn",
    )(tile_expert, n_used, xg, w_gu, w_gu, w_down)


def _combine_kernel(dest_ref, x_ref, rt_ref, gt_ref, gf_ref, yg_ref, o_ref, y_ref, sem, *, tm):
    i = pl.program_id(0)
    slot = i % 2

    def issue(step, into):
        base = step * (TOP_K * tm)

        def body(r, carry):
            for k in range(TOP_K):
                d = dest_ref[base + TOP_K * r + k]
                pltpu.make_async_copy(yg_ref.at[pl.ds(d, 1)], y_ref.at[into, k, pl.ds(r, 1)],
                                      sem.at[into, k]).start()
            return carry

        lax.fori_loop(0, tm, body, 0)

    @pl.when(i == 0)
    def _():
        issue(0, 0)

    @pl.when(i + 1 < pl.num_programs(0))
    def _():
        issue(i + 1, 1 - slot)

    rt = rt_ref[...]
    tot = None
    for k in range(TOP_K):
        rows = y_ref.at[slot, k]
        pltpu.make_async_copy(rows, rows, sem.at[slot, k]).wait()
        term = rt[:, TOP_K + k:TOP_K + k + 1] * _unpack_bf16_pairs(y_ref[slot, k])
        tot = term if tot is None else tot + term
    x = x_ref[...] + gt_ref[...] * tot
    o_ref[...] = _rms(x, gf_ref[...])


def _combine(dest, x, rt, mod1, g_final, yg, *, seq, tm):
    n, d = x.shape
    tpb = seq // tm
    row = lambda w: pl.BlockSpec((tm, w), lambda i, dest: (i, 0))
    return pl.pallas_call(
        functools.partial(_combine_kernel, tm=tm),
        out_shape=jax.ShapeDtypeStruct((n, d), F32),
        grid_spec=pltpu.PrefetchScalarGridSpec(
            num_scalar_prefetch=1,
            grid=(n // tm,),
            in_specs=[row(d), row(LANES), _mod_spec(5, tpb, d),
                      pl.BlockSpec(g_final.shape, lambda i, dest: (0, 0)),
                      pl.BlockSpec(memory_space=pl.ANY)],
            out_specs=row(d),
            scratch_shapes=[pltpu.VMEM((2, TOP_K, tm, yg.shape[1]), yg.dtype),
                            pltpu.SemaphoreType.DMA((2, TOP_K))]),
        compiler_params=_cparams(("arbitrary",)),
        name="moe_combine",
    )(dest, x, rt, mod1, g_final, yg)


def _pad_last(a, width):
    return jnp.pad(a, [(0, 0)] * (a.ndim - 1) + [(0, width - a.shape[-1])])


def kernel(x, c, positions, w_mod, b_mod, g_mix, g_ffn, w_a_down, g_q_lat, g_kv_lat, w_uq, w_ukv, w_oa,
           w_mod_kv, b_mod_kv, g_kv, w_kv_sb, w_q_sb, w_o_sb, w_ffn_gu, w_ffn_down, w_router, b_router,
           w_exp_gu, w_exp_down, g_final):
    bsz, seq, d = x.shape
    n = bsz * seq
    q_lora, kv_lora = g_q_lat.shape[1], g_kv_lat.shape[1]
    n_exp = w_router.shape[-1]
    d_ff = w_ffn_down.shape[1]
    tm = min(TOKEN_TILE, seq)
    t_mla = min(MLA_TILE, seq)
    t_sb = min(SB_TILE, seq)
    tf = next((t for t in (FFN_CHUNK, 512) if d_ff % t == 0), d_ff)

    mod0 = _modvec(c, w_mod[0], b_mod[0]).reshape(bsz, 6, 1, d)
    mod1 = _modvec(c, w_mod[1], b_mod[1]).reshape(bsz, 6, 1, d)
    mod_kv = _modvec(c, w_mod_kv, b_mod_kv).reshape(bsz, 2, 1, d)

    lat_w = q_lora + kv_lora + LANES
    wd = _pad_last(w_a_down[0], lat_w).astype(BF16)
    wq = w_uq[0].reshape(q_lora, MLA_HEADS, QK_NOPE + QK_ROPE)
    wqn = wq[:, :, :QK_NOPE].reshape(q_lora, MLA_HEADS * QK_NOPE).astype(BF16)
    wqr = _pad_last(wq[:, :, QK_NOPE:], LANES).reshape(q_lora, MLA_HEADS * LANES).astype(BF16)
    wkv = w_ukv[0].reshape(kv_lora, MLA_HEADS, QK_NOPE + V_DIM)
    wkn = wkv[:, :, :QK_NOPE].reshape(kv_lora, MLA_HEADS * QK_NOPE).astype(BF16)
    wv = wkv[:, :, QK_NOPE:].reshape(kv_lora, MLA_HEADS * V_DIM).astype(BF16)
    half = QK_ROPE // 2
    inv = ROPE_THETA ** (-jnp.arange(half, dtype=F32) / half)
    inv = _pad_last(jnp.concatenate([inv, inv]), LANES).reshape(1, LANES)
    wr = _pad_last(w_router[0], LANES)
    wr_hi = wr.astype(BF16)
    wr_lo = (wr - wr_hi.astype(F32)).astype(BF16)
    br = _pad_last(b_router[0], LANES).reshape(1, LANES)

    xf = x.reshape(n, d)
    pos = positions.reshape(n, 1)
    row1 = lambda a: a.reshape(1, -1)

    q, k, v = _mla_proj(xf, pos, mod0, row1(g_mix[0]), wd, row1(g_q_lat[0]), row1(g_kv_lat[0]),
                        wqn, wqr, wkn, wv, inv, seq=seq, tm=tm)
    o = _mla_attn(q.reshape(bsz, seq, -1), k.reshape(bsz, seq, -1), v.reshape(bsz, seq, -1), t=t_mla)
    x1, h = _out_proj(o.reshape(n, -1), w_oa[0].astype(BF16), xf, mod0, row1(g_ffn[0]), seq=seq, tm=tm)
    x2, hk, hm = _ffn(h, w_ffn_gu[0].astype(BF16), w_ffn_down[0].astype(BF16), x1, mod0,
                      row1(g_kv), mod_kv, row1(g_mix[1]), mod1, seq=seq, tm=tm, tf=tf)
    kv = _linear(hk, w_kv_sb.astype(BF16), tm=tm, tn=w_kv_sb.shape[1], name="kv_proj")
    q_scale = math.log2(math.e) / math.sqrt(SB_HEAD_DIM)
    qs = _linear(hm, (w_q_sb[0] * q_scale).astype(BF16), tm=tm, tn=1024, name="q_proj")
    o = _sb_attn(qs.reshape(bsz, seq, -1), kv.reshape(bsz, seq, -1), t=t_sb)
    x3, h, rt, counts = _out_proj(o.reshape(n, -1), w_o_sb[0].astype(BF16), x2, mod1, row1(g_ffn[1]),
                                  router=(wr_hi, wr_lo, br, n_exp), seq=seq, tm=tm)
    n_tiles = TOP_K * n // tm + n_exp
    dest, tile_expert, n_used = _route_plan(rt, counts, n_exp, tm, n_tiles)
    xg = _dispatch(dest, h, n_tiles * tm, tm=tm)
    yg = _grouped_ffn(tile_expert, n_used, xg, w_exp_gu[0].astype(BF16), w_exp_down[0].astype(BF16),
                      tm=tm, tf=tf)
    out = _combine(dest, x3, rt, mod1, row1(g_final), yg, seq=seq, tm=tm)
    return out.reshape(bsz, seq, d)
```

```python
import functools
import math

import jax
import jax.numpy as jnp
from jax import lax
from jax.experimental import pallas as pl
from jax.experimental.pallas import tpu as pltpu
from jax.experimental.pallas import tpu_sc as plsc

F32 = jnp.float32
BF16 = jnp.bfloat16

EPS = 1e-6
MLA_HEADS = 8
QK_NOPE = 128
QK_ROPE = 64
V_DIM = 128
ROPE_THETA = 10000.0
SB_HEADS = 8
SB_HEAD_DIM = 128
TOP_K = 2

LANES = 128
QK_PAD = 256
VMEM_LIMIT = 48 * 1024 * 1024
TOKEN_TILE = 512
MLA_TILE = 512
SB_TILE = 512
FFN_CHUNK = 1792
SC_CORES = 2
SC_SUBCORES = 16
SC_WINDOW = 128
SC_ROW_PARTS = 2


def _cparams(sem):
    return pltpu.CompilerParams(dimension_semantics=sem, vmem_limit_bytes=VMEM_LIMIT)


def _rms(x, g):
    return x * lax.rsqrt(jnp.mean(x * x, axis=-1, keepdims=True) + EPS) * g


def _modulate(h, shift, scale):
    return h * (1.0 + scale) + shift


def _split_bf16(a):
    hi = a.astype(BF16)
    lo = (a - hi.astype(F32)).astype(BF16)
    return hi, lo


U32 = jnp.uint32
HIGH_HALF = 0xFFFF0000


def _pack_bf16_pairs(a):
    half = a.shape[1] // 2
    bits = lambda v: lax.bitcast_convert_type(v.astype(BF16).astype(F32), U32)
    return (bits(a[:, :half]) >> 16) | (bits(a[:, half:]) & U32(HIGH_HALF))


def _unpack_bf16_pairs(w):
    lo = lax.bitcast_convert_type(w << 16, F32)
    hi = lax.bitcast_convert_type(w & U32(HIGH_HALF), F32)
    return jnp.concatenate([lo, hi], axis=1)


def _dot(a, b):
    return jnp.dot(a, b, preferred_element_type=F32)


def _dot_nt(a, b):
    return lax.dot_general(a, b, (((1,), (1,)), ((), ())), preferred_element_type=F32)


def _modvec_kernel(c_ref, w_ref, b_ref, o_ref):
    c = c_ref[...]
    sc = c * (1.0 / (1.0 + jnp.exp(-c)))
    a_hi, a_lo = _split_bf16(sc)
    w_hi, w_lo = _split_bf16(w_ref[...])
    o_ref[...] = _dot(a_hi, w_hi) + _dot(a_lo, w_hi) + _dot(a_hi, w_lo) + b_ref[...]


def _modvec(c, w, b, tn=512):
    bsz, d = c.shape
    n = w.shape[1]
    return pl.pallas_call(
        _modvec_kernel,
        out_shape=jax.ShapeDtypeStruct((bsz, n), F32),
        grid=(n // tn,),
        in_specs=[pl.BlockSpec((bsz, d), lambda j: (0, 0)),
                  pl.BlockSpec((d, tn), lambda j: (0, j)),
                  pl.BlockSpec((1, tn), lambda j: (0, j))],
        out_specs=pl.BlockSpec((bsz, tn), lambda j: (0, j)),
        compiler_params=_cparams(("arbitrary",)),
        name="modvec",
    )(c, w, b.reshape(1, n))


def _mod_spec(chunk, tiles_per_batch, d):
    return pl.BlockSpec((None, None, 1, d), lambda i, *_: (i // tiles_per_batch, chunk, 0, 0))


def _mla_proj_kernel(x_ref, pos_ref, sh_ref, sc_ref, g_ref, wd_ref, gq_ref, gkv_ref,
                     wqn_ref, wqr_ref, wkn_ref, wv_ref, inv_ref,
                     q_ref, k_ref, v_ref, *, q_lora, kv_lora):
    x = x_ref[...]
    h = _modulate(_rms(x, g_ref[...]), sh_ref[...], sc_ref[...]).astype(BF16)
    lat = _dot(h, wd_ref[...])
    c_q = _rms(lat[:, :q_lora], gq_ref[...]).astype(BF16)
    c_kv = _rms(lat[:, q_lora:q_lora + kv_lora], gkv_ref[...]).astype(BF16)
    k_rot = lat[:, q_lora + kv_lora:]

    half = QK_ROPE // 2
    ang = pos_ref[...].astype(F32) * inv_ref[...]
    cos = jnp.cos(ang)
    sin = jnp.sin(ang)
    lane = lax.broadcasted_iota(jnp.int32, ang.shape, 1)
    s_a = jnp.where(lane < half, -sin, 0.0)
    s_b = jnp.where((lane >= half) & (lane < 2 * half), sin, 0.0)

    def rope(r):
        return (r * cos + pltpu.roll(r, LANES - half, axis=1) * s_a
                + pltpu.roll(r, half, axis=1) * s_b)

    k_rot = rope(k_rot).astype(BF16)
    q_nope = _dot(c_q, wqn_ref[...])
    q_rope = _dot(c_q, wqr_ref[...])
    k_nope = _dot(c_kv, wkn_ref[...])
    v_ref[...] = _dot(c_kv, wv_ref[...]).astype(BF16)
    for hd in range(MLA_HEADS):
        a, b = hd * LANES, (hd + 1) * LANES
        q_ref[:, hd * QK_PAD:hd * QK_PAD + LANES] = q_nope[:, a:b].astype(BF16)
        q_ref[:, hd * QK_PAD + LANES:(hd + 1) * QK_PAD] = rope(q_rope[:, a:b]).astype(BF16)
        k_ref[:, hd * QK_PAD:hd * QK_PAD + LANES] = k_nope[:, a:b].astype(BF16)
        k_ref[:, hd * QK_PAD + LANES:(hd + 1) * QK_PAD] = k_rot


def _mla_proj(x, pos, mod, g_mix, wd, gq, gkv, wqn, wqr, wkn, wv, inv, *, seq, tm):
    n, d = x.shape
    tpb = seq // tm
    q_lora, kv_lora = gq.shape[1], gkv.shape[1]
    hq = MLA_HEADS * QK_PAD
    hv = MLA_HEADS * V_DIM
    full = lambda a: pl.BlockSpec(a.shape, lambda i: (0,) * a.ndim)
    row = lambda w: pl.BlockSpec((tm, w), lambda i: (i, 0))
    return pl.pallas_call(
        functools.partial(_mla_proj_kernel, q_lora=q_lora, kv_lora=kv_lora),
        out_shape=(jax.ShapeDtypeStruct((n, hq), BF16), jax.ShapeDtypeStruct((n, hq), BF16),
                   jax.ShapeDtypeStruct((n, hv), BF16)),
        grid=(n // tm,),
        in_specs=[row(d), row(1), _mod_spec(0, tpb, d), _mod_spec(1, tpb, d), full(g_mix), full(wd),
                  full(gq), full(gkv), full(wqn), full(wqr), full(wkn), full(wv), full(inv)],
        out_specs=(row(hq), row(hq), row(hv)),
        compiler_params=_cparams(("parallel",)),
        name="mla_proj",
    )(x, pos, mod, mod, g_mix, wd, gq, gkv, wqn, wqr, wkn, wv, inv)


def _lane_chunks(a):
    return [a[:, c * LANES:(c + 1) * LANES] for c in range(a.shape[1] // LANES)]


def _for_blocks(n, step):
    def pair(p, carry):
        step(2 * p, 2)
        return carry

    lax.fori_loop(0, n // 2, pair, 0)

    @pl.when(n % 2 == 1)
    def _():
        step(n - 1, 1)


def _mla_attn_kernel(q_ref, k_ref, v_ref, o_ref, s_ref, m_ref, acc_ref, *, t, scale, heads):
    i = pl.program_id(2)

    def lane_max(s):
        m = None
        for sc in _lane_chunks(s):
            m = sc if m is None else jnp.maximum(m, sc)
        return m

    def scores(h, j):
        start = pl.multiple_of(j * t, t)
        hs = slice(h * QK_PAD, (h + 1) * QK_PAD)
        return _dot_nt(q_ref[0, :, hs], k_ref[0, pl.ds(start, t), hs])

    r = lax.broadcasted_iota(jnp.int32, (t, t), 0)
    c = lax.broadcasted_iota(jnp.int32, (t, t), 1)
    for h in range(heads):
        s = jnp.where(c <= r, scores(h, i), -jnp.inf)
        s_ref[h, i] = s
        m_ref[h] = lane_max(s)

    def pass1(j, nb):
        for h in range(heads):
            m = m_ref[h]
            for jj in range(nb):
                s = scores(h, j + jj)
                s_ref[h, j + jj] = s
                m = jnp.maximum(m, lane_max(s))
            m_ref[h] = m

    _for_blocks(i, pass1)
    for h in range(heads):
        m_ref[h] = jnp.broadcast_to(jnp.max(m_ref[h], axis=1, keepdims=True), (t, LANES))
    acc_ref[...] = jnp.zeros(acc_ref.shape, F32)
    cst = scale * math.log2(math.e)

    def pass2(j, nb):
        start = pl.multiple_of(j * t, t)
        ones = jnp.ones((nb * t, LANES), BF16)
        for h in range(heads):
            m = m_ref[h]
            p = jnp.concatenate([jnp.exp2((sc - m) * cst).astype(BF16)
                                 for jj in range(nb) for sc in _lane_chunks(s_ref[h, j + jj])], axis=1)
            v_ext = jnp.concatenate([v_ref[0, pl.ds(start, nb * t), h * V_DIM:(h + 1) * V_DIM], ones], axis=1)
            acc_ref[h] += _dot(p, v_ext)

    _for_blocks(i + 1, pass2)
    for h in range(heads):
        acc = acc_ref[h]
        o_ref[0, :, h * V_DIM:(h + 1) * V_DIM] = (acc[:, :V_DIM] / acc[:, V_DIM:]).astype(o_ref.dtype)


def _mla_attn(q, k, v, *, t, heads=2):
    bsz, seq, _ = q.shape
    scale = 1.0 / math.sqrt(QK_NOPE + QK_ROPE)
    wq, wv = heads * QK_PAD, heads * V_DIM
    return pl.pallas_call(
        functools.partial(_mla_attn_kernel, t=t, scale=scale, heads=heads),
        out_shape=jax.ShapeDtypeStruct((bsz, seq, MLA_HEADS * V_DIM), BF16),
        grid=(bsz, MLA_HEADS // heads, seq // t),
        in_specs=[pl.BlockSpec((1, t, wq), lambda b, g, i: (b, i, g)),
                  pl.BlockSpec((1, seq, wq), lambda b, g, i: (b, 0, g)),
                  pl.BlockSpec((1, seq, wv), lambda b, g, i: (b, 0, g))],
        out_specs=pl.BlockSpec((1, t, wv), lambda b, g, i: (b, i, g)),
        scratch_shapes=[pltpu.VMEM((heads, seq // t, t, t), F32), pltpu.VMEM((heads, t, LANES), F32),
                        pltpu.VMEM((heads, t, V_DIM + LANES), F32)],
        compiler_params=_cparams(("parallel", "parallel", "arbitrary")),
        name="mla_attn",
    )(q, k, v)


def _sb_attn_kernel(q_ref, k_ref, v_ref, w_ref, o_ref, c_ref, acc_ref, *, t, heads):
    i = pl.program_id(2)
    nl = t // LANES
    d = SB_HEAD_DIM
    c_ref[...] = jnp.zeros(c_ref.shape, F32)
    acc_ref[...] = jnp.zeros(acc_ref.shape, F32)
    r = lax.broadcasted_iota(jnp.int32, (t, t), 0)
    c = lax.broadcasted_iota(jnp.int32, (t, t), 1)
    strict = c < r

    def step(j, masked):
        start = pl.multiple_of(j * t, t)
        for h in range(heads):
            hs = slice(h * d, (h + 1) * d)
            zz = _dot_nt(q_ref[0, :, hs], k_ref[0, pl.ds(start, t), hs])
            zneg = -zz
            sp = jnp.log2(1.0 + jnp.exp2(jnp.minimum(zz, zneg)))
            log_1m = jnp.minimum(zneg, 0.0) - sp
            if masked:
                log_1m = jnp.where(strict, log_1m, 0.0)
            hi, lo = _split_bf16(log_1m)
            carry = c_ref[h]
            a_chunks = [None] * nl
            for cc in reversed(range(nl)):
                cs = slice(cc * LANES, (cc + 1) * LANES)
                y = _dot(jnp.concatenate([hi[:, cs], lo[:, cs]], axis=1), w_ref[...])
                a = jnp.exp2(zz[:, cs] + y[:, :LANES] + carry)
                carry = carry + y[:, LANES:]
                if masked:
                    a = jnp.where(strict[:, cs], a, 0.0)
                a_chunks[cc] = a.astype(BF16)
            c_ref[h] = carry
            acc_ref[h] += _dot(jnp.concatenate(a_chunks, axis=1), v_ref[0, pl.ds(start, t), hs])

    step(i, True)

    def body(n, carry):
        step(i - 1 - n, False)
        return carry

    lax.fori_loop(0, i, body, 0)
    for h in range(heads):
        o_ref[0, :, h * d:(h + 1) * d] = acc_ref[h].astype(o_ref.dtype)


def _sb_attn(q, kv, *, t, heads=4):
    bsz, seq, _ = q.shape
    w = heads * SB_HEAD_DIM
    groups = SB_HEADS // heads
    tri = (jnp.arange(LANES)[:, None] >= jnp.arange(LANES)[None, :]).astype(BF16)
    half = jnp.concatenate([tri, jnp.ones((LANES, LANES), BF16)], axis=1)
    w_sum = jnp.concatenate([half, half], axis=0)
    return pl.pallas_call(
        functools.partial(_sb_attn_kernel, t=t, heads=heads),
        out_shape=jax.ShapeDtypeStruct((bsz, seq, SB_HEADS * SB_HEAD_DIM), BF16),
        grid=(bsz, groups, seq // t),
        in_specs=[pl.BlockSpec((1, t, w), lambda b, g, i: (b, i, g)),
                  pl.BlockSpec((1, seq, w), lambda b, g, i: (b, 0, g)),
                  pl.BlockSpec((1, seq, w), lambda b, g, i: (b, 0, groups + g)),
                  pl.BlockSpec(w_sum.shape, lambda b, g, i: (0, 0))],
        out_specs=pl.BlockSpec((1, t, w), lambda b, g, i: (b, i, g)),
        scratch_shapes=[pltpu.VMEM((heads, t, LANES), F32), pltpu.VMEM((heads, t, SB_HEAD_DIM), F32)],
        compiler_params=_cparams(("parallel", "parallel", "arbitrary")),
        name="sb_attn",
    )(q, kv, kv, w_sum)


def _out_proj_kernel(*refs, n_experts):
    if n_experts:
        (o_ref, w_ref, x_ref, gt_ref, g_ref, sh_ref, sc_ref, wr_hi_ref, wr_lo_ref, br_ref, tri_ref,
         xo_ref, h_ref, rt_ref, cnt_ref) = refs
    else:
        o_ref, w_ref, x_ref, gt_ref, g_ref, sh_ref, sc_ref, xo_ref, h_ref = refs
    x = x_ref[...] + gt_ref[...] * _dot(o_ref[...], w_ref[...])
    xo_ref[...] = x
    h = _modulate(_rms(x, g_ref[...]), sh_ref[...], sc_ref[...])
    h_hi = h.astype(BF16)
    if not n_experts:
        h_ref[...] = h_hi
        return
    h_ref[...] = _pack_bf16_pairs(h)
    h_lo = (h - h_hi.astype(F32)).astype(BF16)
    logits = (_dot(h_hi, wr_hi_ref[...]) + _dot(h_lo, wr_hi_ref[...]) + _dot(h_hi, wr_lo_ref[...])
              + br_ref[...])
    lane = lax.broadcasted_iota(jnp.int32, logits.shape, 1).astype(F32)
    lg = jnp.where(lane < n_experts, logits, -jnp.inf)
    m1 = jnp.max(lg, axis=1, keepdims=True)
    i1 = jnp.min(jnp.where(lg == m1, lane, float(LANES)), axis=1, keepdims=True)
    lg2 = jnp.where(lane == i1, -jnp.inf, lg)
    m2 = jnp.max(lg2, axis=1, keepdims=True)
    i2 = jnp.min(jnp.where(lg2 == m2, lane, float(LANES)), axis=1, keepdims=True)
    e2 = jnp.exp(m2 - m1)
    den = 1.0 + e2
    sel = jnp.where((lane == i1) | (lane == i2), 1.0, 0.0)
    prefix = _dot(tri_ref[...], sel.astype(BF16))
    r1 = jnp.sum(jnp.where(lane == i1, prefix, 0.0), axis=1, keepdims=True)
    r2 = jnp.sum(jnp.where(lane == i2, prefix, 0.0), axis=1, keepdims=True)
    cnt_ref[...] = jnp.sum(sel, axis=0, keepdims=True)
    rt = jnp.zeros_like(logits)
    for k, val in enumerate((i1, i2, 1.0 / den, e2 / den, r1, r2)):
        rt = jnp.where(lane == k, val, rt)
    rt_ref[...] = rt


def _out_proj(o, w, x, mod, g, router=None, *, seq, tm):
    n, d = x.shape
    tpb = seq // tm
    full = lambda a: pl.BlockSpec(a.shape, lambda i: (0,) * a.ndim)
    row = lambda wd: pl.BlockSpec((tm, wd), lambda i: (i, 0))
    in_specs = [row(o.shape[1]), full(w), row(d), _mod_spec(2, tpb, d), full(g),
                _mod_spec(3, tpb, d), _mod_spec(4, tpb, d)]
    args = [o, w, x, mod, g, mod, mod]
    out_shape = [jax.ShapeDtypeStruct((n, d), F32), jax.ShapeDtypeStruct((n, d), BF16)]
    out_specs = [row(d), row(d)]
    n_experts = 0
    if router is not None:
        wr_hi, wr_lo, br, n_experts = router
        tri = (jnp.arange(tm)[:, None] > jnp.arange(tm)[None, :]).astype(BF16)
        in_specs += [full(wr_hi), full(wr_lo), full(br), full(tri)]
        args += [wr_hi, wr_lo, br, tri]
        out_shape[1] = jax.ShapeDtypeStruct((n, d // 2), U32)
        out_specs[1] = row(d // 2)
        out_shape += [jax.ShapeDtypeStruct((n, LANES), F32), jax.ShapeDtypeStruct((n // tm, 1, LANES), F32)]
        out_specs += [row(LANES), pl.BlockSpec((None, 1, LANES), lambda i: (i, 0, 0))]
    return pl.pallas_call(
        functools.partial(_out_proj_kernel, n_experts=n_experts),
        out_shape=tuple(out_shape),
        grid=(n // tm,),
        in_specs=in_specs,
        out_specs=tuple(out_specs),
        compiler_params=_cparams(("parallel",)),
        name="out_proj_router" if n_experts else "out_proj",
    )(*args)


def _silu(g):
    return g * (1.0 / (1.0 + jnp.exp(-g)))


def _ffn_kernel(h_ref, wg_ref, wu_ref, wd_ref, x_ref, gt_ref, gkv_ref, shk_ref, sck_ref,
                gm_ref, shm_ref, scm_ref, xo_ref, hk_ref, hm_ref, acc_ref):
    f = pl.program_id(1)

    @pl.when(f == 0)
    def _():
        acc_ref[...] = jnp.zeros(acc_ref.shape, F32)

    h = h_ref[...]
    a = _silu(_dot(h, wg_ref[...])) * _dot(h, wu_ref[...])
    acc_ref[...] += _dot(a.astype(BF16), wd_ref[...])

    @pl.when(f == pl.num_programs(1) - 1)
    def _():
        x = x_ref[...] + gt_ref[...] * acc_ref[...]
        xo_ref[...] = x
        hk_ref[...] = _modulate(_rms(x, gkv_ref[...]), shk_ref[...], sck_ref[...]).astype(BF16)
        hm_ref[...] = _modulate(_rms(x, gm_ref[...]), shm_ref[...], scm_ref[...]).astype(BF16)


def _ffn(h, w_gu, w_down, x, mod0, g_kv, mod_kv, g_mix1, mod1, *, seq, tm, tf):
    n, d = x.shape
    d_ff = w_down.shape[0]
    nf = d_ff // tf
    tpb = seq // tm
    full = lambda a: pl.BlockSpec(a.shape, lambda i, f: (0,) * a.ndim)
    row = lambda: pl.BlockSpec((tm, d), lambda i, f: (i, 0))
    return pl.pallas_call(
        _ffn_kernel,
        out_shape=(jax.ShapeDtypeStruct((n, d), F32), jax.ShapeDtypeStruct((n, d), BF16),
                   jax.ShapeDtypeStruct((n, d), BF16)),
        grid=(n // tm, nf),
        in_specs=[row(),
                  pl.BlockSpec((d, tf), lambda i, f: (0, f)),
                  pl.BlockSpec((d, tf), lambda i, f: (0, nf + f)),
                  pl.BlockSpec((tf, d), lambda i, f: (f, 0)),
                  row(), _mod_spec(5, tpb, d),
                  full(g_kv), _mod_spec(0, tpb, d), _mod_spec(1, tpb, d),
                  full(g_mix1), _mod_spec(0, tpb, d), _mod_spec(1, tpb, d)],
        out_specs=(row(), row(), row()),
        scratch_shapes=[pltpu.VMEM((tm, d), F32)],
        compiler_params=_cparams(("parallel", "arbitrary")),
        name="ffn_dense",
    )(h, w_gu, w_gu, w_down, x, mod0, g_kv, mod_kv, mod_kv, g_mix1, mod1, mod1)


def _linear_kernel(x_ref, w_ref, o_ref):
    o_ref[...] = _dot(x_ref[...], w_ref[...]).astype(o_ref.dtype)


def _linear(x, w, *, tm, tn, name):
    n, k = x.shape
    m = w.shape[1]
    return pl.pallas_call(
        _linear_kernel,
        out_shape=jax.ShapeDtypeStruct((n, m), BF16),
        grid=(m // tn, n // tm),
        in_specs=[pl.BlockSpec((tm, k), lambda j, i: (i, 0)), pl.BlockSpec((k, tn), lambda j, i: (0, j))],
        out_specs=pl.BlockSpec((tm, tn), lambda j, i: (i, j)),
        compiler_params=_cparams(("parallel", "parallel")),
        name=name,
    )(x, w)


def _route_plan(rt, counts, n_exp, tm, n_tiles):
    cnt = counts[:, 0, :n_exp].astype(jnp.int32)
    sizes = jnp.sum(cnt, axis=0)
    padded = (sizes + tm - 1) // tm * tm
    ends = jnp.cumsum(padded)
    tile_base = (ends - padded)[None, :] + jnp.cumsum(cnt, axis=0) - cnt
    base = jnp.repeat(tile_base, tm, axis=0)
    experts = jnp.arange(n_exp, dtype=jnp.int32)[None, :]
    dest = []
    for k in range(TOP_K):
        e_k = rt[:, k].astype(jnp.int32)
        r_k = rt[:, 2 * TOP_K + k].astype(jnp.int32)
        dest.append(jnp.sum(jnp.where(e_k[:, None] == experts, base, 0), axis=1) + r_k)
    dest = jnp.stack(dest, axis=1).reshape(-1)
    tile_start = jnp.arange(n_tiles, dtype=jnp.int32) * tm
    tile_expert = jnp.minimum(jnp.sum(tile_start[:, None] >= ends[None, :], axis=1), n_exp - 1)
    n_used = (ends[-1] // tm).reshape(1)
    return dest, tile_expert.astype(jnp.int32), n_used.astype(jnp.int32)


def _dispatch_kernel(dest_ref, h_ref, xg_in_ref, xg_ref, sem, *, tm):
    del xg_in_ref
    base = pl.program_id(0) * (TOP_K * tm)

    def issue(r, carry):
        for k in range(TOP_K):
            d = dest_ref[base + TOP_K * r + k]
            pltpu.make_async_copy(h_ref.at[pl.ds(r, 1)], xg_ref.at[pl.ds(d, 1)], sem).start()
        return carry

    lax.fori_loop(0, tm, issue, 0)
    rows = xg_ref.at[pl.ds(0, TOP_K * tm)]
    pltpu.make_async_copy(rows, rows, sem).wait()


def _dispatch(dest, h, n_rows, *, tm):
    n, d = h.shape
    return pl.pallas_call(
        functools.partial(_dispatch_kernel, tm=tm),
        out_shape=jax.ShapeDtypeStruct((n_rows, d), h.dtype),
        grid_spec=pltpu.PrefetchScalarGridSpec(
            num_scalar_prefetch=1,
            grid=(n // tm,),
            in_specs=[pl.BlockSpec((tm, d), lambda i, dest: (i, 0)),
                      pl.BlockSpec(memory_space=pl.ANY)],
            out_specs=pl.BlockSpec(memory_space=pl.ANY),
            scratch_shapes=[pltpu.SemaphoreType.DMA(())]),
        input_output_aliases={2: 0},
        compiler_params=_cparams(("arbitrary",)),
        name="moe_dispatch",
    )(dest, h, jnp.zeros((n_rows, d), h.dtype))


def _grouped_ffn_kernel(te_ref, nu_ref, x_ref, wg_ref, wu_ref, wd_ref, y_ref, hb_ref, acc_ref):
    del te_ref
    i = pl.program_id(0)
    f = pl.program_id(1)
    used = i < nu_ref[0]
    last_f = f == pl.num_programs(1) - 1

    @pl.when(used & (f == 0))
    def _():
        hb_ref[...] = _unpack_bf16_pairs(x_ref[...]).astype(BF16)
        acc_ref[...] = jnp.zeros(acc_ref.shape, F32)

    @pl.when(used)
    def _():
        h = hb_ref[...]
        a = _silu(_dot(h, wg_ref[0])) * _dot(h, wu_ref[0])
        acc_ref[...] += _dot(a.astype(BF16), wd_ref[0])

    @pl.when(used & last_f)
    def _():
        y = _pack_bf16_pairs(acc_ref[...])
        for part in range(SC_ROW_PARTS):
            y_ref[part] = y[:, part * y_ref.shape[2]:(part + 1) * y_ref.shape[2]]

    @pl.when(jnp.logical_not(used) & last_f)
    def _():
        y_ref[...] = jnp.zeros(y_ref.shape, y_ref.dtype)


def _grouped_ffn(tile_expert, n_used, xg, w_gu, w_down, *, tm, tf):
    n_rows, dp = xg.shape
    n_exp, d_ff, d = w_down.shape
    nf = d_ff // tf

    def wspec(shape, index):
        def index_map(i, f, te, nu):
            return index(te[i], jnp.where(i < nu[0], f, nf - 1))
        return pl.BlockSpec(shape, index_map)

    return pl.pallas_call(
        _grouped_ffn_kernel,
        out_shape=jax.ShapeDtypeStruct((SC_ROW_PARTS, n_rows, dp // SC_ROW_PARTS), xg.dtype),
        grid_spec=pltpu.PrefetchScalarGridSpec(
            num_scalar_prefetch=2,
            grid=(n_rows // tm, nf),
            in_specs=[pl.BlockSpec((tm, dp), lambda i, f, te, nu: (i, 0)),
                      wspec((1, d, tf), lambda e, f: (e, 0, f)),
                      wspec((1, d, tf), lambda e, f: (e, 0, nf + f)),
                      wspec((1, tf, d), lambda e, f: (e, f, 0))],
            out_specs=pl.BlockSpec((SC_ROW_PARTS, tm, dp // SC_ROW_PARTS), lambda i, f, te, nu: (0, i, 0)),
            scratch_shapes=[pltpu.VMEM((tm, d), BF16), pltpu.VMEM((tm, d), F32)]),
        compiler_params=_cparams(("arbitrary", "arbitrary")),
        name="moe_grouped_ffn",
    )(tile_expert, n_used, xg, w_gu, w_gu, w_down)


def _sc_gather_rows(table, idx):
    n_idx = idx.shape[0]
    d = table.shape[1]
    mesh = plsc.VectorSubcoreMesh(core_axis_name="core", subcore_axis_name="subcore",
                                  num_cores=SC_CORES, num_subcores=SC_SUBCORES)

    @functools.partial(pl.kernel, out_type=jax.ShapeDtypeStruct((n_idx, d), table.dtype), mesh=mesh,
                       name="sc_gather_rows")
    def gather(t_hbm, i_hbm, o_hbm):
        def body(i_vmem, o_vmem):
            pltpu.sync_copy(t_hbm.at[i_vmem.at[0]], o_vmem)

        pltpu.emit_pipeline(
            body,
            grid=(n_idx // SC_WINDOW,),
            in_specs=[pl.BlockSpec((1, SC_WINDOW), index_map=lambda i: (0, i))],
            out_specs=[pl.BlockSpec((SC_WINDOW, d), index_map=lambda i: (i, 0))],
            core_axis_name="subcore",
            dimension_semantics=(pltpu.PARALLEL,),
        )(i_hbm, o_hbm)

    return gather(table, idx.reshape(1, n_idx))


def _combine_kernel(x_ref, rt_ref, gt_ref, gf_ref, *refs):
    y_refs, o_ref = refs[:-1], refs[-1]
    rt = rt_ref[...]
    tot = None
    for k in range(TOP_K):
        words = jnp.concatenate([y_refs[p * TOP_K + k][...] for p in range(SC_ROW_PARTS)], axis=1)
        term = rt[:, TOP_K + k:TOP_K + k + 1] * _unpack_bf16_pairs(words)
        tot = term if tot is None else tot + term
    x = x_ref[...] + gt_ref[...] * tot
    o_ref[...] = _rms(x, gf_ref[...])


def _combine(x, rt, mod1, g_final, ysel, *, seq, tm):
    n, d = x.shape
    tpb = seq // tm
    nt = n // tm
    row = lambda w: pl.BlockSpec((tm, w), lambda i: (i, 0))
    piece = lambda j: pl.BlockSpec((tm, ysel.shape[1]), lambda i: (j * nt + i, 0))
    n_pieces = SC_ROW_PARTS * TOP_K
    return pl.pallas_call(
        _combine_kernel,
        out_shape=jax.ShapeDtypeStruct((n, d), F32),
        grid=(nt,),
        in_specs=[row(d), row(LANES), _mod_spec(5, tpb, d), pl.BlockSpec(g_final.shape, lambda i: (0, 0))]
                 + [piece(j) for j in range(n_pieces)],
        out_specs=row(d),
        compiler_params=_cparams(("parallel",)),
        name="moe_combine",
    )(x, rt, mod1, g_final, *([ysel] * n_pieces))


def _pad_last(a, width):
    return jnp.pad(a, [(0, 0)] * (a.ndim - 1) + [(0, width - a.shape[-1])])


def kernel(x, c, positions, w_mod, b_mod, g_mix, g_ffn, w_a_down, g_q_lat, g_kv_lat, w_uq, w_ukv, w_oa,
           w_mod_kv, b_mod_kv, g_kv, w_kv_sb, w_q_sb, w_o_sb, w_ffn_gu, w_ffn_down, w_router, b_router,
           w_exp_gu, w_exp_down, g_final):
    bsz, seq, d = x.shape
    n = bsz * seq
    q_lora, kv_lora = g_q_lat.shape[1], g_kv_lat.shape[1]
    n_exp = w_router.shape[-1]
    d_ff = w_ffn_down.shape[1]
    tm = min(TOKEN_TILE, seq)
    t_mla = min(MLA_TILE, seq)
    t_sb = min(SB_TILE, seq)
    tf = next((t for t in (FFN_CHUNK, 512) if d_ff % t == 0), d_ff)

    mod0 = _modvec(c, w_mod[0], b_mod[0]).reshape(bsz, 6, 1, d)
    mod1 = _modvec(c, w_mod[1], b_mod[1]).reshape(bsz, 6, 1, d)
    mod_kv = _modvec(c, w_mod_kv, b_mod_kv).reshape(bsz, 2, 1, d)

    lat_w = q_lora + kv_lora + LANES
    wd = _pad_last(w_a_down[0], lat_w).astype(BF16)
    wq = w_uq[0].reshape(q_lora, MLA_HEADS, QK_NOPE + QK_ROPE)
    wqn = wq[:, :, :QK_NOPE].reshape(q_lora, MLA_HEADS * QK_NOPE).astype(BF16)
    wqr = _pad_last(wq[:, :, QK_NOPE:], LANES).reshape(q_lora, MLA_HEADS * LANES).astype(BF16)
    wkv = w_ukv[0].reshape(kv_lora, MLA_HEADS, QK_NOPE + V_DIM)
    wkn = wkv[:, :, :QK_NOPE].reshape(kv_lora, MLA_HEADS * QK_NOPE).astype(BF16)
    wv = wkv[:, :, QK_NOPE:].reshape(kv_lora, MLA_HEADS * V_DIM).astype(BF16)
    half = QK_ROPE // 2
    inv = ROPE_THETA ** (-jnp.arange(half, dtype=F32) / half)
    inv = _pad_last(jnp.concatenate([inv, inv]), LANES).reshape(1, LANES)
    wr = _pad_last(w_router[0], LANES)
    wr_hi = wr.astype(BF16)
    wr_lo = (wr - wr_hi.astype(F32)).astype(BF16)
    br = _pad_last(b_router[0], LANES).reshape(1, LANES)

    xf = x.reshape(n, d)
    pos = positions.reshape(n, 1)
    row1 = lambda a: a.reshape(1, -1)

    q, k, v = _mla_proj(xf, pos, mod0, row1(g_mix[0]), wd, row1(g_q_lat[0]), row1(g_kv_lat[0]),
                        wqn, wqr, wkn, wv, inv, seq=seq, tm=tm)
    o = _mla_attn(q.reshape(bsz, seq, -1), k.reshape(bsz, seq, -1), v.reshape(bsz, seq, -1), t=t_mla)
    x1, h = _out_proj(o.reshape(n, -1), w_oa[0].astype(BF16), xf, mod0, row1(g_ffn[0]), seq=seq, tm=tm)
    x2, hk, hm = _ffn(h, w_ffn_gu[0].astype(BF16), w_ffn_down[0].astype(BF16), x1, mod0,
                      row1(g_kv), mod_kv, row1(g_mix[1]), mod1, seq=seq, tm=tm, tf=tf)
    kv = _linear(hk, w_kv_sb.astype(BF16), tm=tm, tn=w_kv_sb.shape[1], name="kv_proj")
    q_scale = math.log2(math.e) / math.sqrt(SB_HEAD_DIM)
    qs = _linear(hm, (w_q_sb[0] * q_scale).astype(BF16), tm=tm, tn=1024, name="q_proj")
    o = _sb_attn(qs.reshape(bsz, seq, -1), kv.reshape(bsz, seq, -1), t=t_sb)
    x3, h, rt, counts = _out_proj(o.reshape(n, -1), w_o_sb[0].astype(BF16), x2, mod1, row1(g_ffn[1]),
                                  router=(wr_hi, wr_lo, br, n_exp), seq=seq, tm=tm)
    n_tiles = TOP_K * n // tm + n_exp
    dest, tile_expert, n_used = _route_plan(rt, counts, n_exp, tm, n_tiles)
    xg = _dispatch(dest, h, n_tiles * tm, tm=tm)
    yg = _grouped_ffn(tile_expert, n_used, xg, w_exp_gu[0].astype(BF16), w_exp_down[0].astype(BF16),
                      tm=tm, tf=tf)
    n_rows = n_tiles * tm
    picks = dest.reshape(n, TOP_K).T.reshape(1, -1) + n_rows * jnp.arange(SC_ROW_PARTS, dtype=jnp.int32)[:, None]
    ysel = _sc_gather_rows(yg.reshape(SC_ROW_PARTS * n_rows, -1), picks.reshape(-1))
    out = _combine(x3, rt, mod1, row1(g_final), ysel, seq=seq, tm=tm)
    return out.reshape(bsz, seq, d)
```

```python
import functools
import math

import jax
import jax.numpy as jnp
from jax import lax
from jax.experimental import pallas as pl
from jax.experimental.pallas import tpu as pltpu
from jax.experimental.pallas import tpu_sc as plsc

F32 = jnp.float32
BF16 = jnp.bfloat16

EPS = 1e-6
MLA_HEADS = 8
QK_NOPE = 128
QK_ROPE = 64
V_DIM = 128
ROPE_THETA = 10000.0
SB_HEADS = 8
SB_HEAD_DIM = 128
TOP_K = 2

LANES = 128
QK_PAD = 256
VMEM_LIMIT = 48 * 1024 * 1024
TOKEN_TILE = 512
MLA_TILE = 512
SB_TILE = 512
FFN_CHUNK = 1792
SC_CORES = 2
SC_SUBCORES = 16
SC_WINDOW = 128
SC_ROW_PARTS = 2


def _cparams(sem):
    return pltpu.CompilerParams(dimension_semantics=sem, vmem_limit_bytes=VMEM_LIMIT)


def _rms(x, g):
    return x * lax.rsqrt(jnp.mean(x * x, axis=-1, keepdims=True) + EPS) * g


def _modulate(h, shift, scale):
    return h * (1.0 + scale) + shift


def _split_bf16(a):
    hi = a.astype(BF16)
    lo = (a - hi.astype(F32)).astype(BF16)
    return hi, lo


U32 = jnp.uint32
HIGH_HALF = 0xFFFF0000


def _pack_bf16_pairs(a):
    half = a.shape[1] // 2
    bits = lambda v: lax.bitcast_convert_type(v.astype(BF16).astype(F32), U32)
    return (bits(a[:, :half]) >> 16) | (bits(a[:, half:]) & U32(HIGH_HALF))


def _unpack_bf16_pairs(w):
    lo = lax.bitcast_convert_type(w << 16, F32)
    hi = lax.bitcast_convert_type(w & U32(HIGH_HALF), F32)
    return jnp.concatenate([lo, hi], axis=1)


def _dot(a, b):
    return jnp.dot(a, b, preferred_element_type=F32)


def _dot_nt(a, b):
    return lax.dot_general(a, b, (((1,), (1,)), ((), ())), preferred_element_type=F32)


def _modvec_kernel(c_ref, w_ref, b_ref, o_ref):
    c = c_ref[...]
    sc = c * (1.0 / (1.0 + jnp.exp(-c)))
    a_hi, a_lo = _split_bf16(sc)
    w_hi, w_lo = _split_bf16(w_ref[...])
    o_ref[...] = _dot(a_hi, w_hi) + _dot(a_lo, w_hi) + _dot(a_hi, w_lo) + b_ref[...]


def _modvec(c, w, b, tn=512):
    bsz, d = c.shape
    n = w.shape[1]
    return pl.pallas_call(
        _modvec_kernel,
        out_shape=jax.ShapeDtypeStruct((bsz, n), F32),
        grid=(n // tn,),
        in_specs=[pl.BlockSpec((bsz, d), lambda j: (0, 0)),
                  pl.BlockSpec((d, tn), lambda j: (0, j)),
                  pl.BlockSpec((1, tn), lambda j: (0, j))],
        out_specs=pl.BlockSpec((bsz, tn), lambda j: (0, j)),
        compiler_params=_cparams(("arbitrary",)),
        name="modvec",
    )(c, w, b.reshape(1, n))


def _mod_spec(chunk, tiles_per_batch, d):
    return pl.BlockSpec((None, None, 1, d), lambda i, *_: (i // tiles_per_batch, chunk, 0, 0))


def _mla_proj_kernel(x_ref, pos_ref, sh_ref, sc_ref, g_ref, wd_ref, gq_ref, gkv_ref,
                     wqn_ref, wqr_ref, wkn_ref, wv_ref, inv_ref,
                     q_ref, k_ref, v_ref, *, q_lora, kv_lora):
    x = x_ref[...]
    h = _modulate(_rms(x, g_ref[...]), sh_ref[...], sc_ref[...]).astype(BF16)
    lat = _dot(h, wd_ref[...])
    c_q = _rms(lat[:, :q_lora], gq_ref[...]).astype(BF16)
    c_kv = _rms(lat[:, q_lora:q_lora + kv_lora], gkv_ref[...]).astype(BF16)
    k_rot = lat[:, q_lora + kv_lora:]

    half = QK_ROPE // 2
    ang = pos_ref[...].astype(F32) * inv_ref[...]
    cos = jnp.cos(ang)
    sin = jnp.sin(ang)
    lane = lax.broadcasted_iota(jnp.int32, ang.shape, 1)
    s_a = jnp.where(lane < half, -sin, 0.0)
    s_b = jnp.where((lane >= half) & (lane < 2 * half), sin, 0.0)

    def rope(r):
        return (r * cos + pltpu.roll(r, LANES - half, axis=1) * s_a
                + pltpu.roll(r, half, axis=1) * s_b)

    k_rot = rope(k_rot).astype(BF16)
    q_nope = _dot(c_q, wqn_ref[...])
    q_rope = _dot(c_q, wqr_ref[...])
    k_nope = _dot(c_kv, wkn_ref[...])
    v_ref[...] = _dot(c_kv, wv_ref[...]).astype(BF16)
    for hd in range(MLA_HEADS):
        a, b = hd * LANES, (hd + 1) * LANES
        q_ref[:, hd * QK_PAD:hd * QK_PAD + LANES] = q_nope[:, a:b].astype(BF16)
        q_ref[:, hd * QK_PAD + LANES:(hd + 1) * QK_PAD] = rope(q_rope[:, a:b]).astype(BF16)
        k_ref[:, hd * QK_PAD:hd * QK_PAD + LANES] = k_nope[:, a:b].astype(BF16)
        k_ref[:, hd * QK_PAD + LANES:(hd + 1) * QK_PAD] = k_rot


def _mla_proj(x, pos, mod, g_mix, wd, gq, gkv, wqn, wqr, wkn, wv, inv, *, seq, tm):
    n, d = x.shape
    tpb = seq // tm
    q_lora, kv_lora = gq.shape[1], gkv.shape[1]
    hq = MLA_HEADS * QK_PAD
    hv = MLA_HEADS * V_DIM
    full = lambda a: pl.BlockSpec(a.shape, lambda i: (0,) * a.ndim)
    row = lambda w: pl.BlockSpec((tm, w), lambda i: (i, 0))
    return pl.pallas_call(
        functools.partial(_mla_proj_kernel, q_lora=q_lora, kv_lora=kv_lora),
        out_shape=(jax.ShapeDtypeStruct((n, hq), BF16), jax.ShapeDtypeStruct((n, hq), BF16),
                   jax.ShapeDtypeStruct((n, hv), BF16)),
        grid=(n // tm,),
        in_specs=[row(d), row(1), _mod_spec(0, tpb, d), _mod_spec(1, tpb, d), full(g_mix), full(wd),
                  full(gq), full(gkv), full(wqn), full(wqr), full(wkn), full(wv), full(inv)],
        out_specs=(row(hq), row(hq), row(hv)),
        compiler_params=_cparams(("parallel",)),
        name="mla_proj",
    )(x, pos, mod, mod, g_mix, wd, gq, gkv, wqn, wqr, wkn, wv, inv)


def _lane_chunks(a):
    return [a[:, c * LANES:(c + 1) * LANES] for c in range(a.shape[1] // LANES)]


def _for_blocks(n, step):
    def pair(p, carry):
        step(2 * p, 2)
        return carry

    lax.fori_loop(0, n // 2, pair, 0)

    @pl.when(n % 2 == 1)
    def _():
        step(n - 1, 1)


def _mla_attn_kernel(q_ref, k_ref, v_ref, o_ref, s_ref, m_ref, acc_ref, *, t, scale, heads):
    i = pl.program_id(2)

    def lane_max(s):
        m = None
        for sc in _lane_chunks(s):
            m = sc if m is None else jnp.maximum(m, sc)
        return m

    def scores(h, j):
        start = pl.multiple_of(j * t, t)
        hs = slice(h * QK_PAD, (h + 1) * QK_PAD)
        return _dot_nt(q_ref[0, :, hs], k_ref[0, pl.ds(start, t), hs])

    r = lax.broadcasted_iota(jnp.int32, (t, t), 0)
    c = lax.broadcasted_iota(jnp.int32, (t, t), 1)
    for h in range(heads):
        s = jnp.where(c <= r, scores(h, i), -jnp.inf)
        s_ref[h, i] = s
        m_ref[h] = lane_max(s)

    def pass1(j, nb):
        for h in range(heads):
            m = m_ref[h]
            for jj in range(nb):
                s = scores(h, j + jj)
                s_ref[h, j + jj] = s
                m = jnp.maximum(m, lane_max(s))
            m_ref[h] = m

    _for_blocks(i, pass1)
    for h in range(heads):
        m_ref[h] = jnp.broadcast_to(jnp.max(m_ref[h], axis=1, keepdims=True), (t, LANES))
    acc_ref[...] = jnp.zeros(acc_ref.shape, F32)
    cst = scale * math.log2(math.e)

    def pass2(j, nb):
        start = pl.multiple_of(j * t, t)
        ones = jnp.ones((nb * t, LANES), BF16)
        for h in range(heads):
            m = m_ref[h]
            p = jnp.concatenate([jnp.exp2((sc - m) * cst).astype(BF16)
                                 for jj in range(nb) for sc in _lane_chunks(s_ref[h, j + jj])], axis=1)
            v_ext = jnp.concatenate([v_ref[0, pl.ds(start, nb * t), h * V_DIM:(h + 1) * V_DIM], ones], axis=1)
            acc_ref[h] += _dot(p, v_ext)

    _for_blocks(i + 1, pass2)
    for h in range(heads):
        acc = acc_ref[h]
        o_ref[0, :, h * V_DIM:(h + 1) * V_DIM] = (acc[:, :V_DIM] / acc[:, V_DIM:]).astype(o_ref.dtype)


def _mla_attn(q, k, v, *, t, heads=2):
    bsz, seq, _ = q.shape
    scale = 1.0 / math.sqrt(QK_NOPE + QK_ROPE)
    wq, wv = heads * QK_PAD, heads * V_DIM
    return pl.pallas_call(
        functools.partial(_mla_attn_kernel, t=t, scale=scale, heads=heads),
        out_shape=jax.ShapeDtypeStruct((bsz, seq, MLA_HEADS * V_DIM), BF16),
        grid=(bsz, MLA_HEADS // heads, seq // t),
        in_specs=[pl.BlockSpec((1, t, wq), lambda b, g, i: (b, i, g)),
                  pl.BlockSpec((1, seq, wq), lambda b, g, i: (b, 0, g)),
                  pl.BlockSpec((1, seq, wv), lambda b, g, i: (b, 0, g))],
        out_specs=pl.BlockSpec((1, t, wv), lambda b, g, i: (b, i, g)),
        scratch_shapes=[pltpu.VMEM((heads, seq // t, t, t), F32), pltpu.VMEM((heads, t, LANES), F32),
                        pltpu.VMEM((heads, t, V_DIM + LANES), F32)],
        compiler_params=_cparams(("parallel", "parallel", "arbitrary")),
        name="mla_attn",
    )(q, k, v)


def _sb_attn_kernel(q_ref, k_ref, v_ref, w_ref, o_ref, c_ref, acc_ref, *, t, heads):
    i = pl.program_id(2)
    nl = t // LANES
    d = SB_HEAD_DIM
    c_ref[...] = jnp.zeros(c_ref.shape, F32)
    acc_ref[...] = jnp.zeros(acc_ref.shape, F32)
    r = lax.broadcasted_iota(jnp.int32, (t, t), 0)
    c = lax.broadcasted_iota(jnp.int32, (t, t), 1)
    strict = c < r

    def step(j, masked):
        start = pl.multiple_of(j * t, t)
        for h in range(heads):
            hs = slice(h * d, (h + 1) * d)
            zz = _dot_nt(q_ref[0, :, hs], k_ref[0, pl.ds(start, t), hs])
            zneg = -zz
            sp = jnp.log2(1.0 + jnp.exp2(jnp.minimum(zz, zneg)))
            log_1m = jnp.minimum(zneg, 0.0) - sp
            if masked:
                log_1m = jnp.where(strict, log_1m, 0.0)
            hi, lo = _split_bf16(log_1m)
            carry = c_ref[h]
            a_chunks = [None] * nl
            for cc in reversed(range(nl)):
                cs = slice(cc * LANES, (cc + 1) * LANES)
                y = _dot(jnp.concatenate([hi[:, cs], lo[:, cs]], axis=1), w_ref[...])
                a = jnp.exp2(zz[:, cs] + y[:, :LANES] + carry)
                carry = carry + y[:, LANES:]
                if masked:
                    a = jnp.where(strict[:, cs], a, 0.0)
                a_chunks[cc] = a.astype(BF16)
            c_ref[h] = carry
            acc_ref[h] += _dot(jnp.concatenate(a_chunks, axis=1), v_ref[0, pl.ds(start, t), hs])

    step(i, True)

    def body(n, carry):
        step(i - 1 - n, False)
        return carry

    lax.fori_loop(0, i, body, 0)
    for h in range(heads):
        o_ref[0, :, h * d:(h + 1) * d] = acc_ref[h].astype(o_ref.dtype)


def _sb_attn(q, kv, *, t, heads=4):
    bsz, seq, _ = q.shape
    w = heads * SB_HEAD_DIM
    groups = SB_HEADS // heads
    tri = (jnp.arange(LANES)[:, None] >= jnp.arange(LANES)[None, :]).astype(BF16)
    half = jnp.concatenate([tri, jnp.ones((LANES, LANES), BF16)], axis=1)
    w_sum = jnp.concatenate([half, half], axis=0)
    return pl.pallas_call(
        functools.partial(_sb_attn_kernel, t=t, heads=heads),
        out_shape=jax.ShapeDtypeStruct((bsz, seq, SB_HEADS * SB_HEAD_DIM), BF16),
        grid=(bsz, groups, seq // t),
        in_specs=[pl.BlockSpec((1, t, w), lambda b, g, i: (b, i, g)),
                  pl.BlockSpec((1, seq, w), lambda b, g, i: (b, 0, g)),
                  pl.BlockSpec((1, seq, w), lambda b, g, i: (b, 0, groups + g)),
                  pl.BlockSpec(w_sum.shape, lambda b, g, i: (0, 0))],
        out_specs=pl.BlockSpec((1, t, w), lambda b, g, i: (b, i, g)),
        scratch_shapes=[pltpu.VMEM((heads, t, LANES), F32), pltpu.VMEM((heads, t, SB_HEAD_DIM), F32)],
        compiler_params=_cparams(("parallel", "parallel", "arbitrary")),
        name="sb_attn",
    )(q, kv, kv, w_sum)


def _out_proj_kernel(*refs, n_experts):
    if n_experts:
        (o_ref, w_ref, x_ref, gt_ref, g_ref, sh_ref, sc_ref, wr_hi_ref, wr_lo_ref, br_ref, tri_ref,
         xo_ref, h_ref, rt_ref, cnt_ref) = refs
    else:
        o_ref, w_ref, x_ref, gt_ref, g_ref, sh_ref, sc_ref, xo_ref, h_ref = refs
    x = x_ref[...] + gt_ref[...] * _dot(o_ref[...], w_ref[...])
    xo_ref[...] = x
    h = _modulate(_rms(x, g_ref[...]), sh_ref[...], sc_ref[...])
    h_hi = h.astype(BF16)
    if not n_experts:
        h_ref[...] = h_hi
        return
    words = _pack_bf16_pairs(h)
    for part in range(SC_ROW_PARTS):
        h_ref[part] = words[:, part * h_ref.shape[2]:(part + 1) * h_ref.shape[2]]
    h_lo = (h - h_hi.astype(F32)).astype(BF16)
    logits = (_dot(h_hi, wr_hi_ref[...]) + _dot(h_lo, wr_hi_ref[...]) + _dot(h_hi, wr_lo_ref[...])
              + br_ref[...])
    lane = lax.broadcasted_iota(jnp.int32, logits.shape, 1).astype(F32)
    lg = jnp.where(lane < n_experts, logits, -jnp.inf)
    m1 = jnp.max(lg, axis=1, keepdims=True)
    i1 = jnp.min(jnp.where(lg == m1, lane, float(LANES)), axis=1, keepdims=True)
    lg2 = jnp.where(lane == i1, -jnp.inf, lg)
    m2 = jnp.max(lg2, axis=1, keepdims=True)
    i2 = jnp.min(jnp.where(lg2 == m2, lane, float(LANES)), axis=1, keepdims=True)
    e2 = jnp.exp(m2 - m1)
    den = 1.0 + e2
    sel = jnp.where((lane == i1) | (lane == i2), 1.0, 0.0)
    prefix = _dot(tri_ref[...], sel.astype(BF16))
    r1 = jnp.sum(jnp.where(lane == i1, prefix, 0.0), axis=1, keepdims=True)
    r2 = jnp.sum(jnp.where(lane == i2, prefix, 0.0), axis=1, keepdims=True)
    cnt_ref[...] = jnp.sum(sel, axis=0, keepdims=True)
    rt = jnp.zeros_like(logits)
    for k, val in enumerate((i1, i2, 1.0 / den, e2 / den, r1, r2)):
        rt = jnp.where(lane == k, val, rt)
    rt_ref[...] = rt


def _out_proj(o, w, x, mod, g, router=None, *, seq, tm):
    n, d = x.shape
    tpb = seq // tm
    full = lambda a: pl.BlockSpec(a.shape, lambda i: (0,) * a.ndim)
    row = lambda wd: pl.BlockSpec((tm, wd), lambda i: (i, 0))
    in_specs = [row(o.shape[1]), full(w), row(d), _mod_spec(2, tpb, d), full(g),
                _mod_spec(3, tpb, d), _mod_spec(4, tpb, d)]
    args = [o, w, x, mod, g, mod, mod]
    out_shape = [jax.ShapeDtypeStruct((n, d), F32), jax.ShapeDtypeStruct((n, d), BF16)]
    out_specs = [row(d), row(d)]
    n_experts = 0
    if router is not None:
        wr_hi, wr_lo, br, n_experts = router
        tri = (jnp.arange(tm)[:, None] > jnp.arange(tm)[None, :]).astype(BF16)
        in_specs += [full(wr_hi), full(wr_lo), full(br), full(tri)]
        args += [wr_hi, wr_lo, br, tri]
        dpp = d // 2 // SC_ROW_PARTS
        out_shape[1] = jax.ShapeDtypeStruct((SC_ROW_PARTS, n, dpp), U32)
        out_specs[1] = pl.BlockSpec((SC_ROW_PARTS, tm, dpp), lambda i: (0, i, 0))
        out_shape += [jax.ShapeDtypeStruct((n, LANES), F32), jax.ShapeDtypeStruct((n // tm, 1, LANES), F32)]
        out_specs += [row(LANES), pl.BlockSpec((None, 1, LANES), lambda i: (i, 0, 0))]
    return pl.pallas_call(
        functools.partial(_out_proj_kernel, n_experts=n_experts),
        out_shape=tuple(out_shape),
        grid=(n // tm,),
        in_specs=in_specs,
        out_specs=tuple(out_specs),
        compiler_params=_cparams(("parallel",)),
        name="out_proj_router" if n_experts else "out_proj",
    )(*args)


def _silu(g):
    return g * (1.0 / (1.0 + jnp.exp(-g)))


def _ffn_kernel(h_ref, wg_ref, wu_ref, wd_ref, x_ref, gt_ref, gkv_ref, shk_ref, sck_ref,
                gm_ref, shm_ref, scm_ref, xo_ref, hk_ref, hm_ref, acc_ref):
    f = pl.program_id(1)

    @pl.when(f == 0)
    def _():
        acc_ref[...] = jnp.zeros(acc_ref.shape, F32)

    h = h_ref[...]
    a = _silu(_dot(h, wg_ref[...])) * _dot(h, wu_ref[...])
    acc_ref[...] += _dot(a.astype(BF16), wd_ref[...])

    @pl.when(f == pl.num_programs(1) - 1)
    def _():
        x = x_ref[...] + gt_ref[...] * acc_ref[...]
        xo_ref[...] = x
        hk_ref[...] = _modulate(_rms(x, gkv_ref[...]), shk_ref[...], sck_ref[...]).astype(BF16)
        hm_ref[...] = _modulate(_rms(x, gm_ref[...]), shm_ref[...], scm_ref[...]).astype(BF16)


def _ffn(h, w_gu, w_down, x, mod0, g_kv, mod_kv, g_mix1, mod1, *, seq, tm, tf):
    n, d = x.shape
    d_ff = w_down.shape[0]
    nf = d_ff // tf
    tpb = seq // tm
    full = lambda a: pl.BlockSpec(a.shape, lambda i, f: (0,) * a.ndim)
    row = lambda: pl.BlockSpec((tm, d), lambda i, f: (i, 0))
    return pl.pallas_call(
        _ffn_kernel,
        out_shape=(jax.ShapeDtypeStruct((n, d), F32), jax.ShapeDtypeStruct((n, d), BF16),
                   jax.ShapeDtypeStruct((n, d), BF16)),
        grid=(n // tm, nf),
        in_specs=[row(),
                  pl.BlockSpec((d, tf), lambda i, f: (0, f)),
                  pl.BlockSpec((d, tf), lambda i, f: (0, nf + f)),
                  pl.BlockSpec((tf, d), lambda i, f: (f, 0)),
                  row(), _mod_spec(5, tpb, d),
                  full(g_kv), _mod_spec(0, tpb, d), _mod_spec(1, tpb, d),
                  full(g_mix1), _mod_spec(0, tpb, d), _mod_spec(1, tpb, d)],
        out_specs=(row(), row(), row()),
        scratch_shapes=[pltpu.VMEM((tm, d), F32)],
        compiler_params=_cparams(("parallel", "arbitrary")),
        name="ffn_dense",
    )(h, w_gu, w_gu, w_down, x, mod0, g_kv, mod_kv, mod_kv, g_mix1, mod1, mod1)


def _linear_kernel(x_ref, w_ref, o_ref):
    o_ref[...] = _dot(x_ref[...], w_ref[...]).astype(o_ref.dtype)


def _linear(x, w, *, tm, tn, name):
    n, k = x.shape
    m = w.shape[1]
    return pl.pallas_call(
        _linear_kernel,
        out_shape=jax.ShapeDtypeStruct((n, m), BF16),
        grid=(m // tn, n // tm),
        in_specs=[pl.BlockSpec((tm, k), lambda j, i: (i, 0)), pl.BlockSpec((k, tn), lambda j, i: (0, j))],
        out_specs=pl.BlockSpec((tm, tn), lambda j, i: (i, j)),
        compiler_params=_cparams(("parallel", "parallel")),
        name=name,
    )(x, w)


def _route_plan(rt, counts, n_exp, tm, n_tiles):
    cnt = counts[:, 0, :n_exp].astype(jnp.int32)
    sizes = jnp.sum(cnt, axis=0)
    padded = (sizes + tm - 1) // tm * tm
    ends = jnp.cumsum(padded)
    tile_base = (ends - padded)[None, :] + jnp.cumsum(cnt, axis=0) - cnt
    base = jnp.repeat(tile_base, tm, axis=0)
    experts = jnp.arange(n_exp, dtype=jnp.int32)[None, :]
    dest = []
    for k in range(TOP_K):
        e_k = rt[:, k].astype(jnp.int32)
        r_k = rt[:, 2 * TOP_K + k].astype(jnp.int32)
        dest.append(jnp.sum(jnp.where(e_k[:, None] == experts, base, 0), axis=1) + r_k)
    dest = jnp.stack(dest, axis=0)
    tile_start = jnp.arange(n_tiles, dtype=jnp.int32) * tm
    tile_expert = jnp.minimum(jnp.sum(tile_start[:, None] >= ends[None, :], axis=1), n_exp - 1)
    n_used = (ends[-1] // tm).reshape(1)
    n_valid = jnp.clip((ends - padded + sizes)[tile_expert] - tile_start, 0, tm)
    return dest, tile_expert.astype(jnp.int32), n_used.astype(jnp.int32), n_valid.astype(jnp.int32)


def _sc_mesh():
    return plsc.VectorSubcoreMesh(core_axis_name="core", subcore_axis_name="subcore",
                                  num_cores=SC_CORES, num_subcores=SC_SUBCORES)


def _sc_scatter_rows(x, idx_list, n_rows):
    m, d = x.shape

    @functools.partial(pl.kernel, out_type=jax.ShapeDtypeStruct((n_rows, d), x.dtype), mesh=_sc_mesh(),
                       scratch_types=[], name="sc_scatter_rows")
    def scatter(x_hbm, *refs):
        i_hbms, o_hbm = refs[:-1], refs[-1]

        def body(x_vmem, *i_vmems):
            for i_vmem in i_vmems:
                pltpu.sync_copy(x_vmem, o_hbm.at[i_vmem.at[0]])

        pltpu.emit_pipeline(
            body,
            grid=(m // SC_WINDOW,),
            in_specs=[pl.BlockSpec((SC_WINDOW, d), index_map=lambda i: (i, 0))]
                     + [pl.BlockSpec((1, SC_WINDOW), index_map=lambda i: (0, i))] * len(idx_list),
            out_specs=[],
            core_axis_name="subcore",
            dimension_semantics=(pltpu.PARALLEL,),
        )(x_hbm, *i_hbms)

    return scatter(x, *[idx.reshape(1, m) for idx in idx_list])


def _grouped_ffn_kernel(te_ref, nu_ref, nv_ref, x_ref, wg_ref, wu_ref, wd_ref, y_ref, hb_ref, acc_ref):
    del te_ref
    i = pl.program_id(0)
    f = pl.program_id(1)
    used = i < nu_ref[0]
    last_f = f == pl.num_programs(1) - 1

    @pl.when(used & (f == 0))
    def _():
        words = jnp.concatenate([x_ref[p] for p in range(SC_ROW_PARTS)], axis=1)
        row = lax.broadcasted_iota(jnp.int32, words.shape, 0)
        words = jnp.where(row < nv_ref[i], words, U32(0))
        hb_ref[...] = _unpack_bf16_pairs(words).astype(BF16)
        acc_ref[...] = jnp.zeros(acc_ref.shape, F32)

    @pl.when(used)
    def _():
        h = hb_ref[...]
        a = _silu(_dot(h, wg_ref[0])) * _dot(h, wu_ref[0])
        acc_ref[...] += _dot(a.astype(BF16), wd_ref[0])

    @pl.when(used & last_f)
    def _():
        y = _pack_bf16_pairs(acc_ref[...])
        for part in range(SC_ROW_PARTS):
            y_ref[part] = y[:, part * y_ref.shape[2]:(part + 1) * y_ref.shape[2]]

    @pl.when(jnp.logical_not(used) & last_f)
    def _():
        y_ref[...] = jnp.zeros(y_ref.shape, y_ref.dtype)


def _grouped_ffn(tile_expert, n_used, n_valid, xg, w_gu, w_down, *, tm, tf):
    parts, n_rows, dpp = xg.shape
    n_exp, d_ff, d = w_down.shape
    nf = d_ff // tf

    def wspec(shape, index):
        def index_map(i, f, te, nu, nv):
            return index(te[i], jnp.where(i < nu[0], f, nf - 1))
        return pl.BlockSpec(shape, index_map)

    rows = pl.BlockSpec((parts, tm, dpp), lambda i, f, te, nu, nv: (0, i, 0))
    return pl.pallas_call(
        _grouped_ffn_kernel,
        out_shape=jax.ShapeDtypeStruct(xg.shape, xg.dtype),
        grid_spec=pltpu.PrefetchScalarGridSpec(
            num_scalar_prefetch=3,
            grid=(n_rows // tm, nf),
            in_specs=[rows,
                      wspec((1, d, tf), lambda e, f: (e, 0, f)),
                      wspec((1, d, tf), lambda e, f: (e, 0, nf + f)),
                      wspec((1, tf, d), lambda e, f: (e, f, 0))],
            out_specs=rows,
            scratch_shapes=[pltpu.VMEM((tm, d), BF16), pltpu.VMEM((tm, d), F32)]),
        compiler_params=_cparams(("arbitrary", "arbitrary")),
        name="moe_grouped_ffn",
    )(tile_expert, n_used, n_valid, xg, w_gu, w_gu, w_down)


def _sc_gather_rows(table, idx):
    n_idx = idx.shape[0]
    d = table.shape[1]

    @functools.partial(pl.kernel, out_type=jax.ShapeDtypeStruct((n_idx, d), table.dtype), mesh=_sc_mesh(),
                       name="sc_gather_rows")
    def gather(t_hbm, i_hbm, o_hbm):
        def body(i_vmem, o_vmem):
            pltpu.sync_copy(t_hbm.at[i_vmem.at[0]], o_vmem)

        pltpu.emit_pipeline(
            body,
            grid=(n_idx // SC_WINDOW,),
            in_specs=[pl.BlockSpec((1, SC_WINDOW), index_map=lambda i: (0, i))],
            out_specs=[pl.BlockSpec((SC_WINDOW, d), index_map=lambda i: (i, 0))],
            core_axis_name="subcore",
            dimension_semantics=(pltpu.PARALLEL,),
        )(i_hbm, o_hbm)

    return gather(table, idx.reshape(1, n_idx))


def _combine_kernel(x_ref, rt_ref, gt_ref, gf_ref, *refs):
    y_refs, o_ref = refs[:-1], refs[-1]
    rt = rt_ref[...]
    tot = None
    for k in range(TOP_K):
        words = jnp.concatenate([y_refs[p * TOP_K + k][...] for p in range(SC_ROW_PARTS)], axis=1)
        term = rt[:, TOP_K + k:TOP_K + k + 1] * _unpack_bf16_pairs(words)
        tot = term if tot is None else tot + term
    x = x_ref[...] + gt_ref[...] * tot
    o_ref[...] = _rms(x, gf_ref[...])


def _combine(x, rt, mod1, g_final, ysel, *, seq, tm):
    n, d = x.shape
    tpb = seq // tm
    nt = n // tm
    row = lambda w: pl.BlockSpec((tm, w), lambda i: (i, 0))
    piece = lambda j: pl.BlockSpec((tm, ysel.shape[1]), lambda i: (j * nt + i, 0))
    n_pieces = SC_ROW_PARTS * TOP_K
    return pl.pallas_call(
        _combine_kernel,
        out_shape=jax.ShapeDtypeStruct((n, d), F32),
        grid=(nt,),
        in_specs=[row(d), row(LANES), _mod_spec(5, tpb, d), pl.BlockSpec(g_final.shape, lambda i: (0, 0))]
                 + [piece(j) for j in range(n_pieces)],
        out_specs=row(d),
        compiler_params=_cparams(("parallel",)),
        name="moe_combine",
    )(x, rt, mod1, g_final, *([ysel] * n_pieces))


def _pad_last(a, width):
    return jnp.pad(a, [(0, 0)] * (a.ndim - 1) + [(0, width - a.shape[-1])])


def kernel(x, c, positions, w_mod, b_mod, g_mix, g_ffn, w_a_down, g_q_lat, g_kv_lat, w_uq, w_ukv, w_oa,
           w_mod_kv, b_mod_kv, g_kv, w_kv_sb, w_q_sb, w_o_sb, w_ffn_gu, w_ffn_down, w_router, b_router,
           w_exp_gu, w_exp_down, g_final):
    bsz, seq, d = x.shape
    n = bsz * seq
    q_lora, kv_lora = g_q_lat.shape[1], g_kv_lat.shape[1]
    n_exp = w_router.shape[-1]
    d_ff = w_ffn_down.shape[1]
    tm = min(TOKEN_TILE, seq)
    t_mla = min(MLA_TILE, seq)
    t_sb = min(SB_TILE, seq)
    tf = next((t for t in (FFN_CHUNK, 512) if d_ff % t == 0), d_ff)

    mod0 = _modvec(c, w_mod[0], b_mod[0]).reshape(bsz, 6, 1, d)
    mod1 = _modvec(c, w_mod[1], b_mod[1]).reshape(bsz, 6, 1, d)
    mod_kv = _modvec(c, w_mod_kv, b_mod_kv).reshape(bsz, 2, 1, d)

    lat_w = q_lora + kv_lora + LANES
    wd = _pad_last(w_a_down[0], lat_w).astype(BF16)
    wq = w_uq[0].reshape(q_lora, MLA_HEADS, QK_NOPE + QK_ROPE)
    wqn = wq[:, :, :QK_NOPE].reshape(q_lora, MLA_HEADS * QK_NOPE).astype(BF16)
    wqr = _pad_last(wq[:, :, QK_NOPE:], LANES).reshape(q_lora, MLA_HEADS * LANES).astype(BF16)
    wkv = w_ukv[0].reshape(kv_lora, MLA_HEADS, QK_NOPE + V_DIM)
    wkn = wkv[:, :, :QK_NOPE].reshape(kv_lora, MLA_HEADS * QK_NOPE).astype(BF16)
    wv = wkv[:, :, QK_NOPE:].reshape(kv_lora, MLA_HEADS * V_DIM).astype(BF16)
    half = QK_ROPE // 2
    inv = ROPE_THETA ** (-jnp.arange(half, dtype=F32) / half)
    inv = _pad_last(jnp.concatenate([inv, inv]), LANES).reshape(1, LANES)
    wr = _pad_last(w_router[0], LANES)
    wr_hi = wr.astype(BF16)
    wr_lo = (wr - wr_hi.astype(F32)).astype(BF16)
    br = _pad_last(b_router[0], LANES).reshape(1, LANES)

    xf = x.reshape(n, d)
    pos = positions.reshape(n, 1)
    row1 = lambda a: a.reshape(1, -1)

    q, k, v = _mla_proj(xf, pos, mod0, row1(g_mix[0]), wd, row1(g_q_lat[0]), row1(g_kv_lat[0]),
                        wqn, wqr, wkn, wv, inv, seq=seq, tm=tm)
    o = _mla_attn(q.reshape(bsz, seq, -1), k.reshape(bsz, seq, -1), v.reshape(bsz, seq, -1), t=t_mla)
    x1, h = _out_proj(o.reshape(n, -1), w_oa[0].astype(BF16), xf, mod0, row1(g_ffn[0]), seq=seq, tm=tm)
    x2, hk, hm = _ffn(h, w_ffn_gu[0].astype(BF16), w_ffn_down[0].astype(BF16), x1, mod0,
                      row1(g_kv), mod_kv, row1(g_mix[1]), mod1, seq=seq, tm=tm, tf=tf)
    kv = _linear(hk, w_kv_sb.astype(BF16), tm=tm, tn=w_kv_sb.shape[1], name="kv_proj")
    q_scale = math.log2(math.e) / math.sqrt(SB_HEAD_DIM)
    qs = _linear(hm, (w_q_sb[0] * q_scale).astype(BF16), tm=tm, tn=1024, name="q_proj")
    o = _sb_attn(qs.reshape(bsz, seq, -1), kv.reshape(bsz, seq, -1), t=t_sb)
    x3, h, rt, counts = _out_proj(o.reshape(n, -1), w_o_sb[0].astype(BF16), x2, mod1, row1(g_ffn[1]),
                                  router=(wr_hi, wr_lo, br, n_exp), seq=seq, tm=tm)
    n_tiles = TOP_K * n // tm + n_exp
    n_rows = n_tiles * tm
    dest, tile_expert, n_used, n_valid = _route_plan(rt, counts, n_exp, tm, n_tiles)
    picks = dest[None] + n_rows * jnp.arange(SC_ROW_PARTS, dtype=jnp.int32)[:, None, None]
    xg = _sc_scatter_rows(h.reshape(SC_ROW_PARTS * n, -1), [picks[:, k].reshape(-1) for k in range(TOP_K)],
                          SC_ROW_PARTS * n_rows)
    yg = _grouped_ffn(tile_expert, n_used, n_valid, xg.reshape(SC_ROW_PARTS, n_rows, -1),
                      w_exp_gu[0].astype(BF16), w_exp_down[0].astype(BF16), tm=tm, tf=tf)
    ysel = _sc_gather_rows(yg.reshape(SC_ROW_PARTS * n_rows, -1), picks.reshape(-1))
    out = _combine(x3, rt, mod1, row1(g_final), ysel, seq=seq, tm=tm)
    return out.reshape(bsz, seq, d)
```

```python
import functools
import math

import jax
import jax.numpy as jnp
from jax import lax
from jax.experimental import pallas as pl
from jax.experimental.pallas import tpu as pltpu
from jax.experimental.pallas import tpu_sc as plsc

F32 = jnp.float32
BF16 = jnp.bfloat16

EPS = 1e-6
MLA_HEADS = 8
QK_NOPE = 128
QK_ROPE = 64
V_DIM = 128
ROPE_THETA = 10000.0
SB_HEADS = 8
SB_HEAD_DIM = 128
TOP_K = 2

LANES = 128
QK_PAD = 256
ROPE_PACK = LANES // (QK_ROPE // 2)
VMEM_LIMIT = 48 * 1024 * 1024
TOKEN_TILE = 512
MLA_TILE = 512
SB_TILE = 512
FFN_CHUNK = 1792
SC_CORES = 2
SC_SUBCORES = 16
SC_WINDOW = 128
SC_ROW_PARTS = 2


def _cparams(sem):
    return pltpu.CompilerParams(dimension_semantics=sem, vmem_limit_bytes=VMEM_LIMIT)


def _rms(x, g):
    return x * lax.rsqrt(jnp.mean(x * x, axis=-1, keepdims=True) + EPS) * g


def _modulate(h, shift, scale):
    return h * (1.0 + scale) + shift


def _split_bf16(a):
    hi = a.astype(BF16)
    lo = (a - hi.astype(F32)).astype(BF16)
    return hi, lo


U32 = jnp.uint32
HIGH_HALF = 0xFFFF0000


def _pack_bf16_pairs(a):
    half = a.shape[1] // 2
    bits = lambda v: lax.bitcast_convert_type(v.astype(BF16).astype(F32), U32)
    return (bits(a[:, :half]) >> 16) | (bits(a[:, half:]) & U32(HIGH_HALF))


def _unpack_bf16_pairs(w):
    lo = lax.bitcast_convert_type(w << 16, F32)
    hi = lax.bitcast_convert_type(w & U32(HIGH_HALF), F32)
    return jnp.concatenate([lo, hi], axis=1)


def _dot(a, b):
    return jnp.dot(a, b, preferred_element_type=F32)


def _dot_nt(a, b):
    return lax.dot_general(a, b, (((1,), (1,)), ((), ())), preferred_element_type=F32)


def _modvec_kernel(c_ref, w_ref, b_ref, o_ref):
    c = c_ref[...]
    sc = c * (1.0 / (1.0 + jnp.exp(-c)))
    a_hi, a_lo = _split_bf16(sc)
    w_hi, w_lo = _split_bf16(w_ref[...])
    o_ref[...] = _dot(a_hi, w_hi) + _dot(a_lo, w_hi) + _dot(a_hi, w_lo) + b_ref[...]


def _modvec(c, w, b, tn=512):
    bsz, d = c.shape
    n = w.shape[1]
    return pl.pallas_call(
        _modvec_kernel,
        out_shape=jax.ShapeDtypeStruct((bsz, n), F32),
        grid=(n // tn,),
        in_specs=[pl.BlockSpec((bsz, d), lambda j: (0, 0)),
                  pl.BlockSpec((d, tn), lambda j: (0, j)),
                  pl.BlockSpec((1, tn), lambda j: (0, j))],
        out_specs=pl.BlockSpec((bsz, tn), lambda j: (0, j)),
        compiler_params=_cparams(("arbitrary",)),
        name="modvec",
    )(c, w, b.reshape(1, n))


def _mod_spec(chunk, tiles_per_batch, d):
    return pl.BlockSpec((None, None, 1, d), lambda i, *_: (i // tiles_per_batch, chunk, 0, 0))


def _mla_proj_kernel(x_ref, pos_ref, sh_ref, sc_ref, g_ref, wd_ref, gq_ref, gkv_ref,
                     wqn_ref, wqr_ref, wkn_ref, wv_ref, inv_ref,
                     q_ref, k_ref, v_ref, *, q_lora, kv_lora):
    x = x_ref[...]
    h = _modulate(_rms(x, g_ref[...]), sh_ref[...], sc_ref[...]).astype(BF16)
    lat = _dot(h, wd_ref[...])
    c_q = _rms(lat[:, :q_lora], gq_ref[...]).astype(BF16)
    c_kv = _rms(lat[:, q_lora:q_lora + kv_lora], gkv_ref[...]).astype(BF16)
    k_rot = lat[:, q_lora + kv_lora:]

    half = QK_ROPE // 2
    ang = pos_ref[...].astype(F32) * inv_ref[...]
    cos_p = jnp.cos(ang)
    sin_p = jnp.sin(ang)
    lane = lax.broadcasted_iota(jnp.int32, ang.shape, 1)
    cos, s_a, s_b = [], [], []
    for g in range(ROPE_PACK):
        shift = (LANES - g * half) % LANES
        cg = pltpu.roll(cos_p, shift, axis=1) if shift else cos_p
        sg = pltpu.roll(sin_p, shift, axis=1) if shift else sin_p
        cos.append(jnp.where(lane < half, cg, pltpu.roll(cg, half, axis=1)))
        s_a.append(jnp.where(lane < half, -sg, 0.0))
        s_b.append(jnp.where((lane >= half) & (lane < 2 * half), pltpu.roll(sg, half, axis=1), 0.0))
    cos, s_a, s_b = (jnp.concatenate(v, axis=0) for v in (cos, s_a, s_b))

    def rope(r):
        return (r * cos + pltpu.roll(r, LANES - half, axis=1) * s_a
                + pltpu.roll(r, half, axis=1) * s_b)

    k_rot = rope(k_rot).astype(BF16)
    q_nope = _dot(c_q, wqn_ref[...])
    q_rope = _dot(c_q, wqr_ref[...])
    k_nope = _dot(c_kv, wkn_ref[...])
    v_ref[...] = _dot(c_kv, wv_ref[...]).astype(BF16)
    for hd in range(MLA_HEADS):
        a, b = hd * LANES, (hd + 1) * LANES
        q_ref[:, hd * QK_PAD:hd * QK_PAD + LANES] = q_nope[:, a:b].astype(BF16)
        q_ref[:, hd * QK_PAD + LANES:(hd + 1) * QK_PAD] = rope(q_rope[:, a:b]).astype(BF16)
        k_ref[:, hd * QK_PAD:hd * QK_PAD + LANES] = k_nope[:, a:b].astype(BF16)
        k_ref[:, hd * QK_PAD + LANES:(hd + 1) * QK_PAD] = k_rot


def _mla_proj(x, pos, mod, g_mix, wd, gq, gkv, wqn, wqr, wkn, wv, inv, *, seq, tm):
    n, d = x.shape
    tpb = seq // tm
    q_lora, kv_lora = gq.shape[1], gkv.shape[1]
    hq = MLA_HEADS * QK_PAD
    hv = MLA_HEADS * V_DIM
    full = lambda a: pl.BlockSpec(a.shape, lambda i: (0,) * a.ndim)
    row = lambda w: pl.BlockSpec((tm, w), lambda i: (i, 0))
    return pl.pallas_call(
        functools.partial(_mla_proj_kernel, q_lora=q_lora, kv_lora=kv_lora),
        out_shape=(jax.ShapeDtypeStruct((n, hq), BF16), jax.ShapeDtypeStruct((n, hq), BF16),
                   jax.ShapeDtypeStruct((n, hv), BF16)),
        grid=(n // tm,),
        in_specs=[row(d), pl.BlockSpec((tm // ROPE_PACK, LANES), lambda i: (i, 0)), _mod_spec(0, tpb, d), _mod_spec(1, tpb, d), full(g_mix), full(wd),
                  full(gq), full(gkv), full(wqn), full(wqr), full(wkn), full(wv), full(inv)],
        out_specs=(row(hq), row(hq), row(hv)),
        compiler_params=_cparams(("parallel",)),
        name="mla_proj",
    )(x, pos, mod, mod, g_mix, wd, gq, gkv, wqn, wqr, wkn, wv, inv)


def _lane_chunks(a):
    return [a[:, c * LANES:(c + 1) * LANES] for c in range(a.shape[1] // LANES)]


def _for_blocks(n, step):
    def pair(p, carry):
        step(2 * p, 2)
        return carry

    lax.fori_loop(0, n // 2, pair, 0)

    @pl.when(n % 2 == 1)
    def _():
        step(n - 1, 1)


def _mla_attn_kernel(q_ref, k_ref, v_ref, o_ref, s_ref, m_ref, acc_ref, *, t, scale, heads):
    i = pl.program_id(2)

    def lane_max(s):
        m = None
        for sc in _lane_chunks(s):
            m = sc if m is None else jnp.maximum(m, sc)
        return m

    def scores(h, j):
        start = pl.multiple_of(j * t, t)
        hs = slice(h * QK_PAD, (h + 1) * QK_PAD)
        return _dot_nt(q_ref[0, :, hs], k_ref[0, pl.ds(start, t), hs])

    r = lax.broadcasted_iota(jnp.int32, (t, t), 0)
    c = lax.broadcasted_iota(jnp.int32, (t, t), 1)
    for h in range(heads):
        s = jnp.where(c <= r, scores(h, i), -jnp.inf)
        s_ref[h, i] = s
        m_ref[h] = lane_max(s)

    def pass1(j, nb):
        for h in range(heads):
            m = m_ref[h]
            for jj in range(nb):
                s = scores(h, j + jj)
                s_ref[h, j + jj] = s
                m = jnp.maximum(m, lane_max(s))
            m_ref[h] = m

    _for_blocks(i, pass1)
    for h in range(heads):
        m_ref[h] = jnp.broadcast_to(jnp.max(m_ref[h], axis=1, keepdims=True), (t, LANES))
    acc_ref[...] = jnp.zeros(acc_ref.shape, F32)
    cst = scale * math.log2(math.e)

    def pass2(j, nb):
        start = pl.multiple_of(j * t, t)
        ones = jnp.ones((nb * t, LANES), BF16)
        for h in range(heads):
            m = m_ref[h]
            p = jnp.concatenate([jnp.exp2((sc - m) * cst).astype(BF16)
                                 for jj in range(nb) for sc in _lane_chunks(s_ref[h, j + jj])], axis=1)
            v_ext = jnp.concatenate([v_ref[0, pl.ds(start, nb * t), h * V_DIM:(h + 1) * V_DIM], ones], axis=1)
            acc_ref[h] += _dot(p, v_ext)

    _for_blocks(i + 1, pass2)
    for h in range(heads):
        acc = acc_ref[h]
        o_ref[0, :, h * V_DIM:(h + 1) * V_DIM] = (acc[:, :V_DIM] / acc[:, V_DIM:]).astype(o_ref.dtype)


def _mla_attn(q, k, v, *, t, heads=2):
    bsz, seq, _ = q.shape
    scale = 1.0 / math.sqrt(QK_NOPE + QK_ROPE)
    wq, wv = heads * QK_PAD, heads * V_DIM
    return pl.pallas_call(
        functools.partial(_mla_attn_kernel, t=t, scale=scale, heads=heads),
        out_shape=jax.ShapeDtypeStruct((bsz, seq, MLA_HEADS * V_DIM), BF16),
        grid=(bsz, MLA_HEADS // heads, seq // t),
        in_specs=[pl.BlockSpec((1, t, wq), lambda b, g, i: (b, i, g)),
                  pl.BlockSpec((1, seq, wq), lambda b, g, i: (b, 0, g)),
                  pl.BlockSpec((1, seq, wv), lambda b, g, i: (b, 0, g))],
        out_specs=pl.BlockSpec((1, t, wv), lambda b, g, i: (b, i, g)),
        scratch_shapes=[pltpu.VMEM((heads, seq // t, t, t), F32), pltpu.VMEM((heads, t, LANES), F32),
                        pltpu.VMEM((heads, t, V_DIM + LANES), F32)],
        compiler_params=_cparams(("parallel", "parallel", "arbitrary")),
        name="mla_attn",
    )(q, k, v)


def _sb_attn_kernel(q_ref, k_ref, v_ref, w_ref, o_ref, c_ref, acc_ref, *, t, heads):
    i = pl.program_id(2)
    nl = t // LANES
    d = SB_HEAD_DIM
    c_ref[...] = jnp.zeros(c_ref.shape, F32)
    acc_ref[...] = jnp.zeros(acc_ref.shape, F32)
    sub = t // 2 if t % (2 * LANES) == 0 else t
    r = lax.broadcasted_iota(jnp.int32, (sub, sub), 0)
    c = lax.broadcasted_iota(jnp.int32, (sub, sub), 1)
    strict = c < r

    def block(r0, nr, start, nk, masked):
        rows = slice(r0, r0 + nr)
        for h in range(heads):
            hs = slice(h * d, (h + 1) * d)
            zz = _dot_nt(q_ref[0, rows, hs], k_ref[0, pl.ds(start, nk), hs])
            zneg = -zz
            sp = jnp.log2(1.0 + jnp.exp2(jnp.minimum(zz, zneg)))
            log_1m = jnp.minimum(zneg, 0.0) - sp
            if masked:
                log_1m = jnp.where(strict, log_1m, 0.0)
            hi, lo = _split_bf16(log_1m)
            carry = c_ref[h, rows]
            a_chunks = [None] * (nk // LANES)
            for cc in reversed(range(nk // LANES)):
                cs = slice(cc * LANES, (cc + 1) * LANES)
                y = _dot(jnp.concatenate([hi[:, cs], lo[:, cs]], axis=1), w_ref[...])
                a = jnp.exp2(zz[:, cs] + y[:, :LANES] + carry)
                carry = carry + y[:, LANES:]
                if masked:
                    a = jnp.where(strict[:, cs], a, 0.0)
                a_chunks[cc] = a.astype(BF16)
            c_ref[h, rows] = carry
            acc_ref[h, rows] += _dot(jnp.concatenate(a_chunks, axis=1), v_ref[0, pl.ds(start, nk), hs])

    for band in range(t // sub):
        for piece in reversed(range(band + 1)):
            block(band * sub, sub, pl.multiple_of(i * t + piece * sub, sub), sub, piece == band)

    def body(n, carry):
        block(0, t, pl.multiple_of((i - 1 - n) * t, t), t, False)
        return carry

    lax.fori_loop(0, i, body, 0)
    for h in range(heads):
        o_ref[0, :, h * d:(h + 1) * d] = acc_ref[h].astype(o_ref.dtype)


def _sb_attn(q, kv, *, t, heads=4):
    bsz, seq, _ = q.shape
    w = heads * SB_HEAD_DIM
    groups = SB_HEADS // heads
    tri = (jnp.arange(LANES)[:, None] >= jnp.arange(LANES)[None, :]).astype(BF16)
    half = jnp.concatenate([tri, jnp.ones((LANES, LANES), BF16)], axis=1)
    w_sum = jnp.concatenate([half, half], axis=0)
    return pl.pallas_call(
        functools.partial(_sb_attn_kernel, t=t, heads=heads),
        out_shape=jax.ShapeDtypeStruct((bsz, seq, SB_HEADS * SB_HEAD_DIM), BF16),
        grid=(bsz, groups, seq // t),
        in_specs=[pl.BlockSpec((1, t, w), lambda b, g, i: (b, i, g)),
                  pl.BlockSpec((1, seq, w), lambda b, g, i: (b, 0, g)),
                  pl.BlockSpec((1, seq, w), lambda b, g, i: (b, 0, groups + g)),
                  pl.BlockSpec(w_sum.shape, lambda b, g, i: (0, 0))],
        out_specs=pl.BlockSpec((1, t, w), lambda b, g, i: (b, i, g)),
        scratch_shapes=[pltpu.VMEM((heads, t, LANES), F32), pltpu.VMEM((heads, t, SB_HEAD_DIM), F32)],
        compiler_params=_cparams(("parallel", "parallel", "arbitrary")),
        name="sb_attn",
    )(q, kv, kv, w_sum)


def _out_proj_kernel(*refs, n_experts):
    if n_experts:
        (o_ref, w_ref, x_ref, gt_ref, g_ref, sh_ref, sc_ref, wr_hi_ref, wr_lo_ref, br_ref, tri_ref,
         xo_ref, h_ref, rt_ref, cnt_ref) = refs
    else:
        o_ref, w_ref, x_ref, gt_ref, g_ref, sh_ref, sc_ref, xo_ref, h_ref = refs
    x = x_ref[...] + gt_ref[...] * _dot(o_ref[...], w_ref[...])
    xo_ref[...] = x
    h = _modulate(_rms(x, g_ref[...]), sh_ref[...], sc_ref[...])
    h_hi = h.astype(BF16)
    if not n_experts:
        h_ref[...] = h_hi
        return
    words = _pack_bf16_pairs(h)
    for part in range(SC_ROW_PARTS):
        h_ref[part] = words[:, part * h_ref.shape[2]:(part + 1) * h_ref.shape[2]]
    h_lo = (h - h_hi.astype(F32)).astype(BF16)
    logits = (_dot(h_hi, wr_hi_ref[...]) + _dot(h_lo, wr_hi_ref[...]) + _dot(h_hi, wr_lo_ref[...])
              + br_ref[...])
    lane = lax.broadcasted_iota(jnp.int32, logits.shape, 1).astype(F32)
    lg = jnp.where(lane < n_experts, logits, -jnp.inf)
    m1 = jnp.max(lg, axis=1, keepdims=True)
    i1 = jnp.min(jnp.where(lg == m1, lane, float(LANES)), axis=1, keepdims=True)
    lg2 = jnp.where(lane == i1, -jnp.inf, lg)
    m2 = jnp.max(lg2, axis=1, keepdims=True)
    i2 = jnp.min(jnp.where(lg2 == m2, lane, float(LANES)), axis=1, keepdims=True)
    e2 = jnp.exp(m2 - m1)
    den = 1.0 + e2
    sel = jnp.where((lane == i1) | (lane == i2), 1.0, 0.0)
    prefix = _dot(tri_ref[...], sel.astype(BF16))
    r1 = jnp.sum(jnp.where(lane == i1, prefix, 0.0), axis=1, keepdims=True)
    r2 = jnp.sum(jnp.where(lane == i2, prefix, 0.0), axis=1, keepdims=True)
    cnt_ref[...] = jnp.sum(sel, axis=0, keepdims=True)
    rt = jnp.zeros_like(logits)
    for k, val in enumerate((i1, i2, 1.0 / den, e2 / den, r1, r2)):
        rt = jnp.where(lane == k, val, rt)
    rt_ref[...] = rt


def _out_proj(o, w, x, mod, g, router=None, *, seq, tm):
    n, d = x.shape
    tpb = seq // tm
    full = lambda a: pl.BlockSpec(a.shape, lambda i: (0,) * a.ndim)
    row = lambda wd: pl.BlockSpec((tm, wd), lambda i: (i, 0))
    in_specs = [row(o.shape[1]), full(w), row(d), _mod_spec(2, tpb, d), full(g),
                _mod_spec(3, tpb, d), _mod_spec(4, tpb, d)]
    args = [o, w, x, mod, g, mod, mod]
    out_shape = [jax.ShapeDtypeStruct((n, d), F32), jax.ShapeDtypeStruct((n, d), BF16)]
    out_specs = [row(d), row(d)]
    n_experts = 0
    if router is not None:
        wr_hi, wr_lo, br, n_experts = router
        tri = (jnp.arange(tm)[:, None] > jnp.arange(tm)[None, :]).astype(BF16)
        in_specs += [full(wr_hi), full(wr_lo), full(br), full(tri)]
        args += [wr_hi, wr_lo, br, tri]
        dpp = d // 2 // SC_ROW_PARTS
        out_shape[1] = jax.ShapeDtypeStruct((SC_ROW_PARTS, n, dpp), U32)
        out_specs[1] = pl.BlockSpec((SC_ROW_PARTS, tm, dpp), lambda i: (0, i, 0))
        out_shape += [jax.ShapeDtypeStruct((n, LANES), F32), jax.ShapeDtypeStruct((n // tm, 1, LANES), F32)]
        out_specs += [row(LANES), pl.BlockSpec((None, 1, LANES), lambda i: (i, 0, 0))]
    return pl.pallas_call(
        functools.partial(_out_proj_kernel, n_experts=n_experts),
        out_shape=tuple(out_shape),
        grid=(n // tm,),
        in_specs=in_specs,
        out_specs=tuple(out_specs),
        compiler_params=_cparams(("parallel",)),
        name="out_proj_router" if n_experts else "out_proj",
    )(*args)


def _silu(g):
    return g * (1.0 / (1.0 + jnp.exp(-g)))


def _ffn_kernel(h_ref, wg_ref, wu_ref, wd_ref, x_ref, gt_ref, gkv_ref, shk_ref, sck_ref,
                gm_ref, shm_ref, scm_ref, xo_ref, hk_ref, hm_ref, acc_ref):
    f = pl.program_id(1)

    @pl.when(f == 0)
    def _():
        acc_ref[...] = jnp.zeros(acc_ref.shape, F32)

    h = h_ref[...]
    a = _silu(_dot(h, wg_ref[...])) * _dot(h, wu_ref[...])
    acc_ref[...] += _dot(a.astype(BF16), wd_ref[...])

    @pl.when(f == pl.num_programs(1) - 1)
    def _():
        x = x_ref[...] + gt_ref[...] * acc_ref[...]
        xo_ref[...] = x
        hk_ref[...] = _modulate(_rms(x, gkv_ref[...]), shk_ref[...], sck_ref[...]).astype(BF16)
        hm_ref[...] = _modulate(_rms(x, gm_ref[...]), shm_ref[...], scm_ref[...]).astype(BF16)


def _ffn(h, w_gu, w_down, x, mod0, g_kv, mod_kv, g_mix1, mod1, *, seq, tm, tf):
    n, d = x.shape
    d_ff = w_down.shape[0]
    nf = d_ff // tf
    tpb = seq // tm
    full = lambda a: pl.BlockSpec(a.shape, lambda i, f: (0,) * a.ndim)
    row = lambda: pl.BlockSpec((tm, d), lambda i, f: (i, 0))
    return pl.pallas_call(
        _ffn_kernel,
        out_shape=(jax.ShapeDtypeStruct((n, d), F32), jax.ShapeDtypeStruct((n, d), BF16),
                   jax.ShapeDtypeStruct((n, d), BF16)),
        grid=(n // tm, nf),
        in_specs=[row(),
                  pl.BlockSpec((d, tf), lambda i, f: (0, f)),
                  pl.BlockSpec((d, tf), lambda i, f: (0, nf + f)),
                  pl.BlockSpec((tf, d), lambda i, f: (f, 0)),
                  row(), _mod_spec(5, tpb, d),
                  full(g_kv), _mod_spec(0, tpb, d), _mod_spec(1, tpb, d),
                  full(g_mix1), _mod_spec(0, tpb, d), _mod_spec(1, tpb, d)],
        out_specs=(row(), row(), row()),
        scratch_shapes=[pltpu.VMEM((tm, d), F32)],
        compiler_params=_cparams(("parallel", "arbitrary")),
        name="ffn_dense",
    )(h, w_gu, w_gu, w_down, x, mod0, g_kv, mod_kv, mod_kv, g_mix1, mod1, mod1)


def _linear_kernel(x_ref, w_ref, o_ref):
    o_ref[...] = _dot(x_ref[...], w_ref[...]).astype(o_ref.dtype)


def _linear(x, w, *, tm, tn, name):
    n, k = x.shape
    m = w.shape[1]
    return pl.pallas_call(
        _linear_kernel,
        out_shape=jax.ShapeDtypeStruct((n, m), BF16),
        grid=(m // tn, n // tm),
        in_specs=[pl.BlockSpec((tm, k), lambda j, i: (i, 0)), pl.BlockSpec((k, tn), lambda j, i: (0, j))],
        out_specs=pl.BlockSpec((tm, tn), lambda j, i: (i, j)),
        compiler_params=_cparams(("parallel", "parallel")),
        name=name,
    )(x, w)


def _route_plan(rt, counts, n_exp, tm, n_tiles):
    cnt = counts[:, 0, :n_exp].astype(jnp.int32)
    sizes = jnp.sum(cnt, axis=0)
    padded = (sizes + tm - 1) // tm * tm
    ends = jnp.cumsum(padded)
    tile_base = (ends - padded)[None, :] + jnp.cumsum(cnt, axis=0) - cnt
    base = jnp.repeat(tile_base, tm, axis=0)
    experts = jnp.arange(n_exp, dtype=jnp.int32)[None, :]
    dest = []
    for k in range(TOP_K):
        e_k = rt[:, k].astype(jnp.int32)
        r_k = rt[:, 2 * TOP_K + k].astype(jnp.int32)
        dest.append(jnp.sum(jnp.where(e_k[:, None] == experts, base, 0), axis=1) + r_k)
    dest = jnp.stack(dest, axis=0)
    tile_start = jnp.arange(n_tiles, dtype=jnp.int32) * tm
    tile_expert = jnp.minimum(jnp.sum(tile_start[:, None] >= ends[None, :], axis=1), n_exp - 1)
    n_used = (ends[-1] // tm).reshape(1)
    n_valid = jnp.clip((ends - padded + sizes)[tile_expert] - tile_start, 0, tm)
    return dest, tile_expert.astype(jnp.int32), n_used.astype(jnp.int32), n_valid.astype(jnp.int32)


def _sc_mesh():
    return plsc.VectorSubcoreMesh(core_axis_name="core", subcore_axis_name="subcore",
                                  num_cores=SC_CORES, num_subcores=SC_SUBCORES)


def _sc_scatter_rows(x, idx_list, n_rows):
    m, d = x.shape

    @functools.partial(pl.kernel, out_type=jax.ShapeDtypeStruct((n_rows, d), x.dtype), mesh=_sc_mesh(),
                       scratch_types=[], name="sc_scatter_rows")
    def scatter(x_hbm, *refs):
        i_hbms, o_hbm = refs[:-1], refs[-1]

        def body(x_vmem, *i_vmems):
            for i_vmem in i_vmems:
                pltpu.sync_copy(x_vmem, o_hbm.at[i_vmem.at[0]])

        pltpu.emit_pipeline(
            body,
            grid=(m // SC_WINDOW,),
            in_specs=[pl.BlockSpec((SC_WINDOW, d), index_map=lambda i: (i, 0))]
                     + [pl.BlockSpec((1, SC_WINDOW), index_map=lambda i: (0, i))] * len(idx_list),
            out_specs=[],
            core_axis_name="subcore",
            dimension_semantics=(pltpu.PARALLEL,),
        )(x_hbm, *i_hbms)

    return scatter(x, *[idx.reshape(1, m) for idx in idx_list])


def _grouped_ffn_kernel(te_ref, nu_ref, nv_ref, x_ref, wg_ref, wu_ref, wd_ref, y_ref, hb_ref, acc_ref):
    del te_ref
    i = pl.program_id(0)
    f = pl.program_id(1)
    used = i < nu_ref[0]
    last_f = f == pl.num_programs(1) - 1

    @pl.when(used & (f == 0))
    def _():
        words = jnp.concatenate([x_ref[p] for p in range(SC_ROW_PARTS)], axis=1)
        row = lax.broadcasted_iota(jnp.int32, words.shape, 0)
        words = jnp.where(row < nv_ref[i], words, U32(0))
        hb_ref[...] = _unpack_bf16_pairs(words).astype(BF16)
        acc_ref[...] = jnp.zeros(acc_ref.shape, F32)

    @pl.when(used)
    def _():
        h = hb_ref[...]
        a = _silu(_dot(h, wg_ref[0])) * _dot(h, wu_ref[0])
        acc_ref[...] += _dot(a.astype(BF16), wd_ref[0])

    @pl.when(used & last_f)
    def _():
        y = _pack_bf16_pairs(acc_ref[...])
        for part in range(SC_ROW_PARTS):
            y_ref[part] = y[:, part * y_ref.shape[2]:(part + 1) * y_ref.shape[2]]

    @pl.when(jnp.logical_not(used) & last_f)
    def _():
        y_ref[...] = jnp.zeros(y_ref.shape, y_ref.dtype)


def _grouped_ffn(tile_expert, n_used, n_valid, xg, w_gu, w_down, *, tm, tf):
    parts, n_rows, dpp = xg.shape
    n_exp, d_ff, d = w_down.shape
    nf = d_ff // tf

    def wspec(shape, index):
        def index_map(i, f, te, nu, nv):
            return index(te[i], jnp.where(i < nu[0], f, nf - 1))
        return pl.BlockSpec(shape, index_map)

    rows = pl.BlockSpec((parts, tm, dpp), lambda i, f, te, nu, nv: (0, i, 0))
    return pl.pallas_call(
        _grouped_ffn_kernel,
        out_shape=jax.ShapeDtypeStruct(xg.shape, xg.dtype),
        grid_spec=pltpu.PrefetchScalarGridSpec(
            num_scalar_prefetch=3,
            grid=(n_rows // tm, nf),
            in_specs=[rows,
                      wspec((1, d, tf), lambda e, f: (e, 0, f)),
                      wspec((1, d, tf), lambda e, f: (e, 0, nf + f)),
                      wspec((1, tf, d), lambda e, f: (e, f, 0))],
            out_specs=rows,
            scratch_shapes=[pltpu.VMEM((tm, d), BF16), pltpu.VMEM((tm, d), F32)]),
        compiler_params=_cparams(("arbitrary", "arbitrary")),
        name="moe_grouped_ffn",
    )(tile_expert, n_used, n_valid, xg, w_gu, w_gu, w_down)


def _sc_gather_rows(table, idx):
    n_idx = idx.shape[0]
    d = table.shape[1]

    @functools.partial(pl.kernel, out_type=jax.ShapeDtypeStruct((n_idx, d), table.dtype), mesh=_sc_mesh(),
                       name="sc_gather_rows")
    def gather(t_hbm, i_hbm, o_hbm):
        def body(i_vmem, o_vmem):
            pltpu.sync_copy(t_hbm.at[i_vmem.at[0]], o_vmem)

        pltpu.emit_pipeline(
            body,
            grid=(n_idx // SC_WINDOW,),
            in_specs=[pl.BlockSpec((1, SC_WINDOW), index_map=lambda i: (0, i))],
            out_specs=[pl.BlockSpec((SC_WINDOW, d), index_map=lambda i: (i, 0))],
            core_axis_name="subcore",
            dimension_semantics=(pltpu.PARALLEL,),
        )(i_hbm, o_hbm)

    return gather(table, idx.reshape(1, n_idx))


def _combine_kernel(x_ref, rt_ref, gt_ref, gf_ref, *refs):
    y_refs, o_ref = refs[:-1], refs[-1]
    rt = rt_ref[...]
    tot = None
    for k in range(TOP_K):
        words = jnp.concatenate([y_refs[p * TOP_K + k][...] for p in range(SC_ROW_PARTS)], axis=1)
        term = rt[:, TOP_K + k:TOP_K + k + 1] * _unpack_bf16_pairs(words)
        tot = term if tot is None else tot + term
    x = x_ref[...] + gt_ref[...] * tot
    o_ref[...] = _rms(x, gf_ref[...])


def _combine(x, rt, mod1, g_final, ysel, *, seq, tm):
    n, d = x.shape
    tpb = seq // tm
    nt = n // tm
    row = lambda w: pl.BlockSpec((tm, w), lambda i: (i, 0))
    piece = lambda j: pl.BlockSpec((tm, ysel.shape[1]), lambda i: (j * nt + i, 0))
    n_pieces = SC_ROW_PARTS * TOP_K
    return pl.pallas_call(
        _combine_kernel,
        out_shape=jax.ShapeDtypeStruct((n, d), F32),
        grid=(nt,),
        in_specs=[row(d), row(LANES), _mod_spec(5, tpb, d), pl.BlockSpec(g_final.shape, lambda i: (0, 0))]
                 + [piece(j) for j in range(n_pieces)],
        out_specs=row(d),
        compiler_params=_cparams(("parallel",)),
        name="moe_combine",
    )(x, rt, mod1, g_final, *([ysel] * n_pieces))


def _pad_last(a, width):
    return jnp.pad(a, [(0, 0)] * (a.ndim - 1) + [(0, width - a.shape[-1])])


def kernel(x, c, positions, w_mod, b_mod, g_mix, g_ffn, w_a_down, g_q_lat, g_kv_lat, w_uq, w_ukv, w_oa,
           w_mod_kv, b_mod_kv, g_kv, w_kv_sb, w_q_sb, w_o_sb, w_ffn_gu, w_ffn_down, w_router, b_router,
           w_exp_gu, w_exp_down, g_final):
    bsz, seq, d = x.shape
    n = bsz * seq
    q_lora, kv_lora = g_q_lat.shape[1], g_kv_lat.shape[1]
    n_exp = w_router.shape[-1]
    d_ff = w_ffn_down.shape[1]
    tm = min(TOKEN_TILE, seq)
    t_mla = min(MLA_TILE, seq)
    t_sb = min(SB_TILE, seq)
    tf = next((t for t in (FFN_CHUNK, 512) if d_ff % t == 0), d_ff)

    mod0 = _modvec(c, w_mod[0], b_mod[0]).reshape(bsz, 6, 1, d)
    mod1 = _modvec(c, w_mod[1], b_mod[1]).reshape(bsz, 6, 1, d)
    mod_kv = _modvec(c, w_mod_kv, b_mod_kv).reshape(bsz, 2, 1, d)

    lat_w = q_lora + kv_lora + LANES
    wd = _pad_last(w_a_down[0], lat_w).astype(BF16)
    wq = w_uq[0].reshape(q_lora, MLA_HEADS, QK_NOPE + QK_ROPE)
    wqn = wq[:, :, :QK_NOPE].reshape(q_lora, MLA_HEADS * QK_NOPE).astype(BF16)
    wqr = _pad_last(wq[:, :, QK_NOPE:], LANES).reshape(q_lora, MLA_HEADS * LANES).astype(BF16)
    wkv = w_ukv[0].reshape(kv_lora, MLA_HEADS, QK_NOPE + V_DIM)
    wkn = wkv[:, :, :QK_NOPE].reshape(kv_lora, MLA_HEADS * QK_NOPE).astype(BF16)
    wv = wkv[:, :, QK_NOPE:].reshape(kv_lora, MLA_HEADS * V_DIM).astype(BF16)
    half = QK_ROPE // 2
    inv = ROPE_THETA ** (-jnp.arange(half, dtype=F32) / half)
    inv = jnp.tile(inv, ROPE_PACK).reshape(1, LANES)
    wr = _pad_last(w_router[0], LANES)
    wr_hi = wr.astype(BF16)
    wr_lo = (wr - wr_hi.astype(F32)).astype(BF16)
    br = _pad_last(b_router[0], LANES).reshape(1, LANES)

    xf = x.reshape(n, d)
    pos = positions.reshape(n // tm, ROPE_PACK, tm // ROPE_PACK).swapaxes(1, 2)
    pos = jnp.repeat(pos.reshape(n // ROPE_PACK, ROPE_PACK), QK_ROPE // 2, axis=1)
    row1 = lambda a: a.reshape(1, -1)

    q, k, v = _mla_proj(xf, pos, mod0, row1(g_mix[0]), wd, row1(g_q_lat[0]), row1(g_kv_lat[0]),
                        wqn, wqr, wkn, wv, inv, seq=seq, tm=tm)
    o = _mla_attn(q.reshape(bsz, seq, -1), k.reshape(bsz, seq, -1), v.reshape(bsz, seq, -1), t=t_mla)
    x1, h = _out_proj(o.reshape(n, -1), w_oa[0].astype(BF16), xf, mod0, row1(g_ffn[0]), seq=seq, tm=tm)
    x2, hk, hm = _ffn(h, w_ffn_gu[0].astype(BF16), w_ffn_down[0].astype(BF16), x1, mod0,
                      row1(g_kv), mod_kv, row1(g_mix[1]), mod1, seq=seq, tm=tm, tf=tf)
    kv = _linear(hk, w_kv_sb.astype(BF16), tm=tm, tn=w_kv_sb.shape[1], name="kv_proj")
    q_scale = math.log2(math.e) / math.sqrt(SB_HEAD_DIM)
    qs = _linear(hm, (w_q_sb[0] * q_scale).astype(BF16), tm=tm, tn=1024, name="q_proj")
    o = _sb_attn(qs.reshape(bsz, seq, -1), kv.reshape(bsz, seq, -1), t=t_sb)
    x3, h, rt, counts = _out_proj(o.reshape(n, -1), w_o_sb[0].astype(BF16), x2, mod1, row1(g_ffn[1]),
                                  router=(wr_hi, wr_lo, br, n_exp), seq=seq, tm=tm)
    n_tiles = TOP_K * n // tm + n_exp
    n_rows = n_tiles * tm
    dest, tile_expert, n_used, n_valid = _route_plan(rt, counts, n_exp, tm, n_tiles)
    picks = dest[None] + n_rows * jnp.arange(SC_ROW_PARTS, dtype=jnp.int32)[:, None, None]
    xg = _sc_scatter_rows(h.reshape(SC_ROW_PARTS * n, -1), [picks[:, k].reshape(-1) for k in range(TOP_K)],
                          SC_ROW_PARTS * n_rows)
    yg = _grouped_ffn(tile_expert, n_used, n_valid, xg.reshape(SC_ROW_PARTS, n_rows, -1),
                      w_exp_gu[0].astype(BF16), w_exp_down[0].astype(BF16), tm=tm, tf=tf)
    ysel = _sc_gather_rows(yg.reshape(SC_ROW_PARTS * n_rows, -1), picks.reshape(-1))
    out = _combine(x3, rt, mod1, row1(g_final), ysel, seq=seq, tm=tm)
    return out.reshape(bsz, seq, d)
```

```python
import functools
import math

import jax
import jax.numpy as jnp
from jax import lax
from jax.experimental import pallas as pl
from jax.experimental.pallas import tpu as pltpu
from jax.experimental.pallas import tpu_sc as plsc

F32 = jnp.float32
BF16 = jnp.bfloat16

EPS = 1e-6
MLA_HEADS = 8
QK_NOPE = 128
QK_ROPE = 64
V_DIM = 128
ROPE_THETA = 10000.0
SB_HEADS = 8
SB_HEAD_DIM = 128
TOP_K = 2

LANES = 128
QK_PAD = 256
ROPE_PACK = LANES // (QK_ROPE // 2)
VMEM_LIMIT = 48 * 1024 * 1024
VMEM_LIMIT_LARGE = 58 * 1024 * 1024
TOKEN_TILE = 512
MLA_TILE = 512
SB_TILE = 512
FFN_CHUNK = 1792
SC_CORES = 2
SC_SUBCORES = 16
SC_WINDOW = 128
SC_ROW_PARTS = 2


def _cparams(sem, vmem_limit=VMEM_LIMIT):
    return pltpu.CompilerParams(dimension_semantics=sem, vmem_limit_bytes=vmem_limit)


def _rms(x, g):
    return x * lax.rsqrt(jnp.mean(x * x, axis=-1, keepdims=True) + EPS) * g


def _modulate(h, shift, scale):
    return h * (1.0 + scale) + shift


def _split_bf16(a):
    hi = a.astype(BF16)
    lo = (a - hi.astype(F32)).astype(BF16)
    return hi, lo


U32 = jnp.uint32
HIGH_HALF = 0xFFFF0000


def _pack_bf16_pairs(a):
    half = a.shape[1] // 2
    bits = lambda v: lax.bitcast_convert_type(v.astype(BF16).astype(F32), U32)
    return (bits(a[:, :half]) >> 16) | (bits(a[:, half:]) & U32(HIGH_HALF))


def _unpack_bf16_pairs(w):
    lo = lax.bitcast_convert_type(w << 16, F32)
    hi = lax.bitcast_convert_type(w & U32(HIGH_HALF), F32)
    return jnp.concatenate([lo, hi], axis=1)


def _dot(a, b):
    return jnp.dot(a, b, preferred_element_type=F32)


def _dot_nt(a, b):
    return lax.dot_general(a, b, (((1,), (1,)), ((), ())), preferred_element_type=F32)


def _modvec_kernel(c_ref, w_ref, b_ref, o_ref):
    c = c_ref[...]
    sc = c * (1.0 / (1.0 + jnp.exp(-c)))
    a_hi, a_lo = _split_bf16(sc)
    w_hi, w_lo = _split_bf16(w_ref[...])
    o_ref[...] = _dot(a_hi, w_hi) + _dot(a_lo, w_hi) + _dot(a_hi, w_lo) + b_ref[...]


def _modvec(c, w, b, tn=512):
    bsz, d = c.shape
    n = w.shape[1]
    return pl.pallas_call(
        _modvec_kernel,
        out_shape=jax.ShapeDtypeStruct((bsz, n), F32),
        grid=(n // tn,),
        in_specs=[pl.BlockSpec((bsz, d), lambda j: (0, 0)),
                  pl.BlockSpec((d, tn), lambda j: (0, j)),
                  pl.BlockSpec((1, tn), lambda j: (0, j))],
        out_specs=pl.BlockSpec((bsz, tn), lambda j: (0, j)),
        compiler_params=_cparams(("arbitrary",)),
        name="modvec",
    )(c, w, b.reshape(1, n))


def _mod_spec(chunk, tiles_per_batch, d):
    return pl.BlockSpec((None, None, 1, d), lambda i, *_: (i // tiles_per_batch, chunk, 0, 0))


def _mla_proj_kernel(x_ref, pos_ref, sh_ref, sc_ref, g_ref, wd_ref, gq_ref, gkv_ref,
                     wqn_ref, wqr_ref, wkn_ref, wv_ref, inv_ref,
                     q_ref, k_ref, v_ref, *, q_lora, kv_lora):
    x = x_ref[...]
    h = _modulate(_rms(x, g_ref[...]), sh_ref[...], sc_ref[...]).astype(BF16)
    lat = _dot(h, wd_ref[...])
    c_q = _rms(lat[:, :q_lora], gq_ref[...]).astype(BF16)
    c_kv = _rms(lat[:, q_lora:q_lora + kv_lora], gkv_ref[...]).astype(BF16)
    k_rot = lat[:, q_lora + kv_lora:]

    half = QK_ROPE // 2
    ang = pos_ref[...].astype(F32) * inv_ref[...]
    cos_p = jnp.cos(ang)
    sin_p = jnp.sin(ang)
    lane = lax.broadcasted_iota(jnp.int32, ang.shape, 1)
    cos, s_a, s_b = [], [], []
    for g in range(ROPE_PACK):
        shift = (LANES - g * half) % LANES
        cg = pltpu.roll(cos_p, shift, axis=1) if shift else cos_p
        sg = pltpu.roll(sin_p, shift, axis=1) if shift else sin_p
        cos.append(jnp.where(lane < half, cg, pltpu.roll(cg, half, axis=1)))
        s_a.append(jnp.where(lane < half, -sg, 0.0))
        s_b.append(jnp.where((lane >= half) & (lane < 2 * half), pltpu.roll(sg, half, axis=1), 0.0))
    cos, s_a, s_b = (jnp.concatenate(v, axis=0) for v in (cos, s_a, s_b))

    def rope(r):
        return (r * cos + pltpu.roll(r, LANES - half, axis=1) * s_a
                + pltpu.roll(r, half, axis=1) * s_b)

    k_rot = rope(k_rot).astype(BF16)
    q_nope = _dot(c_q, wqn_ref[...])
    q_rope = _dot(c_q, wqr_ref[...])
    k_nope = _dot(c_kv, wkn_ref[...])
    v_ref[...] = _dot(c_kv, wv_ref[...]).astype(BF16)
    for hd in range(MLA_HEADS):
        a, b = hd * LANES, (hd + 1) * LANES
        q_ref[:, hd * QK_PAD:hd * QK_PAD + LANES] = q_nope[:, a:b].astype(BF16)
        q_ref[:, hd * QK_PAD + LANES:(hd + 1) * QK_PAD] = rope(q_rope[:, a:b]).astype(BF16)
        k_ref[:, hd * QK_PAD:hd * QK_PAD + LANES] = k_nope[:, a:b].astype(BF16)
        k_ref[:, hd * QK_PAD + LANES:(hd + 1) * QK_PAD] = k_rot


def _mla_proj(x, pos, mod, g_mix, wd, gq, gkv, wqn, wqr, wkn, wv, inv, *, seq, tm):
    n, d = x.shape
    tpb = seq // tm
    q_lora, kv_lora = gq.shape[1], gkv.shape[1]
    hq = MLA_HEADS * QK_PAD
    hv = MLA_HEADS * V_DIM
    full = lambda a: pl.BlockSpec(a.shape, lambda i: (0,) * a.ndim)
    row = lambda w: pl.BlockSpec((tm, w), lambda i: (i, 0))
    return pl.pallas_call(
        functools.partial(_mla_proj_kernel, q_lora=q_lora, kv_lora=kv_lora),
        out_shape=(jax.ShapeDtypeStruct((n, hq), BF16), jax.ShapeDtypeStruct((n, hq), BF16),
                   jax.ShapeDtypeStruct((n, hv), BF16)),
        grid=(n // tm,),
        in_specs=[row(d), pl.BlockSpec((tm // ROPE_PACK, LANES), lambda i: (i, 0)), _mod_spec(0, tpb, d), _mod_spec(1, tpb, d), full(g_mix), full(wd),
                  full(gq), full(gkv), full(wqn), full(wqr), full(wkn), full(wv), full(inv)],
        out_specs=(row(hq), row(hq), row(hv)),
        compiler_params=_cparams(("parallel",)),
        name="mla_proj",
    )(x, pos, mod, mod, g_mix, wd, gq, gkv, wqn, wqr, wkn, wv, inv)


def _lane_chunks(a):
    return [a[:, c * LANES:(c + 1) * LANES] for c in range(a.shape[1] // LANES)]


def _for_blocks(n, step):
    def pair(p, carry):
        step(2 * p, 2)
        return carry

    lax.fori_loop(0, n // 2, pair, 0)

    @pl.when(n % 2 == 1)
    def _():
        step(n - 1, 1)


def _mla_attn_kernel(q_ref, k_ref, v_ref, o_ref, s_ref, m_ref, acc_ref, *, t, scale, heads):
    i = pl.program_id(2)

    def lane_max(s):
        m = None
        for sc in _lane_chunks(s):
            m = sc if m is None else jnp.maximum(m, sc)
        return m

    def scores(h, j):
        start = pl.multiple_of(j * t, t)
        hs = slice(h * QK_PAD, (h + 1) * QK_PAD)
        return _dot_nt(q_ref[0, :, hs], k_ref[0, pl.ds(start, t), hs])

    r = lax.broadcasted_iota(jnp.int32, (t, t), 0)
    c = lax.broadcasted_iota(jnp.int32, (t, t), 1)
    for h in range(heads):
        s = jnp.where(c <= r, scores(h, i), -jnp.inf)
        s_ref[h, i] = s
        m_ref[h] = lane_max(s)

    def pass1(j, nb):
        for h in range(heads):
            m = m_ref[h]
            for jj in range(nb):
                s = scores(h, j + jj)
                s_ref[h, j + jj] = s
                m = jnp.maximum(m, lane_max(s))
            m_ref[h] = m

    _for_blocks(i, pass1)
    for h in range(heads):
        m_ref[h] = jnp.broadcast_to(jnp.max(m_ref[h], axis=1, keepdims=True), (t, LANES))
    acc_ref[...] = jnp.zeros(acc_ref.shape, F32)
    cst = scale * math.log2(math.e)

    def pass2(j, nb):
        start = pl.multiple_of(j * t, t)
        ones = jnp.ones((nb * t, LANES), BF16)
        for h in range(heads):
            m = m_ref[h]
            p = jnp.concatenate([jnp.exp2((sc - m) * cst).astype(BF16)
                                 for jj in range(nb) for sc in _lane_chunks(s_ref[h, j + jj])], axis=1)
            v_ext = jnp.concatenate([v_ref[0, pl.ds(start, nb * t), h * V_DIM:(h + 1) * V_DIM], ones], axis=1)
            acc_ref[h] += _dot(p, v_ext)

    _for_blocks(i + 1, pass2)
    for h in range(heads):
        acc = acc_ref[h]
        o_ref[0, :, h * V_DIM:(h + 1) * V_DIM] = (acc[:, :V_DIM] / acc[:, V_DIM:]).astype(o_ref.dtype)


def _mla_attn(q, k, v, *, t, heads=2):
    bsz, seq, _ = q.shape
    scale = 1.0 / math.sqrt(QK_NOPE + QK_ROPE)
    wq, wv = heads * QK_PAD, heads * V_DIM
    return pl.pallas_call(
        functools.partial(_mla_attn_kernel, t=t, scale=scale, heads=heads),
        out_shape=jax.ShapeDtypeStruct((bsz, seq, MLA_HEADS * V_DIM), BF16),
        grid=(bsz, MLA_HEADS // heads, seq // t),
        in_specs=[pl.BlockSpec((1, t, wq), lambda b, g, i: (b, i, g)),
                  pl.BlockSpec((1, seq, wq), lambda b, g, i: (b, 0, g)),
                  pl.BlockSpec((1, seq, wv), lambda b, g, i: (b, 0, g))],
        out_specs=pl.BlockSpec((1, t, wv), lambda b, g, i: (b, i, g)),
        scratch_shapes=[pltpu.VMEM((heads, seq // t, t, t), F32), pltpu.VMEM((heads, t, LANES), F32),
                        pltpu.VMEM((heads, t, V_DIM + LANES), F32)],
        compiler_params=_cparams(("parallel", "parallel", "arbitrary")),
        name="mla_attn",
    )(q, k, v)


def _sb_attn_kernel(q_ref, k_ref, v_ref, w_ref, o_ref, c_ref, acc_ref, *, t, heads):
    i = pl.program_id(2)
    nl = t // LANES
    d = SB_HEAD_DIM
    c_ref[...] = jnp.zeros(c_ref.shape, F32)
    acc_ref[...] = jnp.zeros(acc_ref.shape, F32)
    sub = t // 2 if t % (2 * LANES) == 0 else t
    r = lax.broadcasted_iota(jnp.int32, (sub, sub), 0)
    c = lax.broadcasted_iota(jnp.int32, (sub, sub), 1)
    strict = c < r

    def block(r0, nr, start, nk, masked):
        rows = slice(r0, r0 + nr)
        for h in range(heads):
            hs = slice(h * d, (h + 1) * d)
            zz = _dot_nt(q_ref[0, rows, hs], k_ref[0, pl.ds(start, nk), hs])
            zneg = -zz
            sp = jnp.log2(1.0 + jnp.exp2(jnp.minimum(zz, zneg)))
            log_1m = jnp.minimum(zneg, 0.0) - sp
            if masked:
                log_1m = jnp.where(strict, log_1m, 0.0)
            hi, lo = _split_bf16(log_1m)
            carry = c_ref[h, rows]
            a_chunks = [None] * (nk // LANES)
            for cc in reversed(range(nk // LANES)):
                cs = slice(cc * LANES, (cc + 1) * LANES)
                y = _dot(jnp.concatenate([hi[:, cs], lo[:, cs]], axis=1), w_ref[...])
                a = jnp.exp2(zz[:, cs] + y[:, :LANES] + carry)
                carry = carry + y[:, LANES:]
                if masked:
                    a = jnp.where(strict[:, cs], a, 0.0)
                a_chunks[cc] = a.astype(BF16)
            c_ref[h, rows] = carry
            acc_ref[h, rows] += _dot(jnp.concatenate(a_chunks, axis=1), v_ref[0, pl.ds(start, nk), hs])

    for band in range(t // sub):
        for piece in reversed(range(band + 1)):
            block(band * sub, sub, pl.multiple_of(i * t + piece * sub, sub), sub, piece == band)

    def body(n, carry):
        block(0, t, pl.multiple_of((i - 1 - n) * t, t), t, False)
        return carry

    lax.fori_loop(0, i, body, 0)
    for h in range(heads):
        o_ref[0, :, h * d:(h + 1) * d] = acc_ref[h].astype(o_ref.dtype)


def _sb_attn(q, kv, *, t, heads=4):
    bsz, seq, _ = q.shape
    w = heads * SB_HEAD_DIM
    groups = SB_HEADS // heads
    tri = (jnp.arange(LANES)[:, None] >= jnp.arange(LANES)[None, :]).astype(BF16)
    half = jnp.concatenate([tri, jnp.ones((LANES, LANES), BF16)], axis=1)
    w_sum = jnp.concatenate([half, half], axis=0)
    return pl.pallas_call(
        functools.partial(_sb_attn_kernel, t=t, heads=heads),
        out_shape=jax.ShapeDtypeStruct((bsz, seq, SB_HEADS * SB_HEAD_DIM), BF16),
        grid=(bsz, groups, seq // t),
        in_specs=[pl.BlockSpec((1, t, w), lambda b, g, i: (b, i, g)),
                  pl.BlockSpec((1, seq, w), lambda b, g, i: (b, 0, g)),
                  pl.BlockSpec((1, seq, w), lambda b, g, i: (b, 0, groups + g)),
                  pl.BlockSpec(w_sum.shape, lambda b, g, i: (0, 0))],
        out_specs=pl.BlockSpec((1, t, w), lambda b, g, i: (b, i, g)),
        scratch_shapes=[pltpu.VMEM((heads, t, LANES), F32), pltpu.VMEM((heads, t, SB_HEAD_DIM), F32)],
        compiler_params=_cparams(("parallel", "parallel", "arbitrary")),
        name="sb_attn",
    )(q, kv, kv, w_sum)


def _out_proj_kernel(o_ref, w_ref, x_ref, gt_ref, g_ref, sh_ref, sc_ref, wr_hi_ref, wr_lo_ref, br_ref, tri_ref,
                     xo_ref, h_ref, rt_ref, cnt_ref, *, n_experts):
    x = x_ref[...] + gt_ref[...] * _dot(o_ref[...], w_ref[...])
    xo_ref[...] = x
    h = _modulate(_rms(x, g_ref[...]), sh_ref[...], sc_ref[...])
    h_hi = h.astype(BF16)
    words = _pack_bf16_pairs(h)
    for part in range(SC_ROW_PARTS):
        h_ref[part] = words[:, part * h_ref.shape[2]:(part + 1) * h_ref.shape[2]]
    h_lo = (h - h_hi.astype(F32)).astype(BF16)
    logits = (_dot(h_hi, wr_hi_ref[...]) + _dot(h_lo, wr_hi_ref[...]) + _dot(h_hi, wr_lo_ref[...])
              + br_ref[...])
    lane = lax.broadcasted_iota(jnp.int32, logits.shape, 1).astype(F32)
    lg = jnp.where(lane < n_experts, logits, -jnp.inf)
    m1 = jnp.max(lg, axis=1, keepdims=True)
    i1 = jnp.min(jnp.where(lg == m1, lane, float(LANES)), axis=1, keepdims=True)
    lg2 = jnp.where(lane == i1, -jnp.inf, lg)
    m2 = jnp.max(lg2, axis=1, keepdims=True)
    i2 = jnp.min(jnp.where(lg2 == m2, lane, float(LANES)), axis=1, keepdims=True)
    e2 = jnp.exp(m2 - m1)
    den = 1.0 + e2
    sel = jnp.where((lane == i1) | (lane == i2), 1.0, 0.0)
    prefix = _dot(tri_ref[...], sel.astype(BF16))
    r1 = jnp.sum(jnp.where(lane == i1, prefix, 0.0), axis=1, keepdims=True)
    r2 = jnp.sum(jnp.where(lane == i2, prefix, 0.0), axis=1, keepdims=True)
    cnt_ref[...] = jnp.sum(sel, axis=0, keepdims=True)
    rt = jnp.zeros_like(logits)
    for k, val in enumerate((i1, i2, 1.0 / den, e2 / den, r1, r2)):
        rt = jnp.where(lane == k, val, rt)
    rt_ref[...] = rt


def _out_proj_router(o, w, x, mod, g, wr_hi, wr_lo, br, n_experts, *, seq, tm):
    n, d = x.shape
    tpb = seq // tm
    dpp = d // 2 // SC_ROW_PARTS
    tri = (jnp.arange(tm)[:, None] > jnp.arange(tm)[None, :]).astype(BF16)
    full = lambda a: pl.BlockSpec(a.shape, lambda i: (0,) * a.ndim)
    row = lambda wd: pl.BlockSpec((tm, wd), lambda i: (i, 0))
    return pl.pallas_call(
        functools.partial(_out_proj_kernel, n_experts=n_experts),
        out_shape=(jax.ShapeDtypeStruct((n, d), F32), jax.ShapeDtypeStruct((SC_ROW_PARTS, n, dpp), U32),
                   jax.ShapeDtypeStruct((n, LANES), F32), jax.ShapeDtypeStruct((n // tm, 1, LANES), F32)),
        grid=(n // tm,),
        in_specs=[row(o.shape[1]), full(w), row(d), _mod_spec(2, tpb, d), full(g),
                  _mod_spec(3, tpb, d), _mod_spec(4, tpb, d), full(wr_hi), full(wr_lo), full(br), full(tri)],
        out_specs=(row(d), pl.BlockSpec((SC_ROW_PARTS, tm, dpp), lambda i: (0, i, 0)), row(LANES),
                   pl.BlockSpec((None, 1, LANES), lambda i: (i, 0, 0))),
        compiler_params=_cparams(("parallel",)),
        name="out_proj_router",
    )(o, w, x, mod, g, mod, mod, wr_hi, wr_lo, br, tri)


def _silu(g):
    return g * (1.0 / (1.0 + jnp.exp(-g)))


def _ffn_kernel(o_ref, wo_ref, x_ref, gt1_ref, gf_ref, sh2_ref, sc2_ref, wg_ref, wu_ref, wd_ref, gt2_ref,
                gkv_ref, shk_ref, sck_ref, gm_ref, shm_ref, scm_ref,
                xo_ref, hk_ref, hm_ref, x1_ref, h_ref, acc_ref):
    f = pl.program_id(1)

    @pl.when(f == 0)
    def _():
        x1 = x_ref[...] + gt1_ref[...] * _dot(o_ref[...], wo_ref[...])
        x1_ref[...] = x1
        h_ref[...] = _modulate(_rms(x1, gf_ref[...]), sh2_ref[...], sc2_ref[...]).astype(BF16)
        acc_ref[...] = jnp.zeros(acc_ref.shape, F32)

    h = h_ref[...]
    a = _silu(_dot(h, wg_ref[...])) * _dot(h, wu_ref[...])
    acc_ref[...] += _dot(a.astype(BF16), wd_ref[...])

    @pl.when(f == pl.num_programs(1) - 1)
    def _():
        x = x1_ref[...] + gt2_ref[...] * acc_ref[...]
        xo_ref[...] = x
        hk_ref[...] = _modulate(_rms(x, gkv_ref[...]), shk_ref[...], sck_ref[...]).astype(BF16)
        hm_ref[...] = _modulate(_rms(x, gm_ref[...]), shm_ref[...], scm_ref[...]).astype(BF16)


def _ffn(o, w_o, x, mod0, g_ffn, w_gu, w_down, g_kv, mod_kv, g_mix1, mod1, *, seq, tm, tf):
    n, d = x.shape
    d_ff = w_down.shape[0]
    nf = d_ff // tf
    tpb = seq // tm
    full = lambda a: pl.BlockSpec(a.shape, lambda i, f: (0,) * a.ndim)
    row = lambda w=d: pl.BlockSpec((tm, w), lambda i, f: (i, 0))
    return pl.pallas_call(
        _ffn_kernel,
        out_shape=(jax.ShapeDtypeStruct((n, d), F32), jax.ShapeDtypeStruct((n, d), BF16),
                   jax.ShapeDtypeStruct((n, d), BF16)),
        grid=(n // tm, nf),
        in_specs=[row(o.shape[1]), full(w_o), row(), _mod_spec(2, tpb, d), full(g_ffn),
                  _mod_spec(3, tpb, d), _mod_spec(4, tpb, d),
                  pl.BlockSpec((d, tf), lambda i, f: (0, f)),
                  pl.BlockSpec((d, tf), lambda i, f: (0, nf + f)),
                  pl.BlockSpec((tf, d), lambda i, f: (f, 0)),
                  _mod_spec(5, tpb, d),
                  full(g_kv), _mod_spec(0, tpb, d), _mod_spec(1, tpb, d),
                  full(g_mix1), _mod_spec(0, tpb, d), _mod_spec(1, tpb, d)],
        out_specs=(row(), row(), row()),
        scratch_shapes=[pltpu.VMEM((tm, d), F32), pltpu.VMEM((tm, d), BF16), pltpu.VMEM((tm, d), F32)],
        compiler_params=_cparams(("parallel", "arbitrary"), vmem_limit=VMEM_LIMIT_LARGE),
        name="ffn_dense",
    )(o, w_o, x, mod0, g_ffn, mod0, mod0, w_gu, w_gu, w_down, mod0,
      g_kv, mod_kv, mod_kv, g_mix1, mod1, mod1)


def _linear_kernel(x_ref, w_ref, o_ref):
    o_ref[...] = _dot(x_ref[...], w_ref[...]).astype(o_ref.dtype)


def _linear(x, w, *, tm, tn, name):
    n, k = x.shape
    m = w.shape[1]
    return pl.pallas_call(
        _linear_kernel,
        out_shape=jax.ShapeDtypeStruct((n, m), BF16),
        grid=(m // tn, n // tm),
        in_specs=[pl.BlockSpec((tm, k), lambda j, i: (i, 0)), pl.BlockSpec((k, tn), lambda j, i: (0, j))],
        out_specs=pl.BlockSpec((tm, tn), lambda j, i: (i, j)),
        compiler_params=_cparams(("parallel", "parallel")),
        name=name,
    )(x, w)


def _route_plan(rt, counts, n_exp, tm, n_tiles):
    cnt = counts[:, 0, :n_exp].astype(jnp.int32)
    sizes = jnp.sum(cnt, axis=0)
    padded = (sizes + tm - 1) // tm * tm
    ends = jnp.cumsum(padded)
    tile_base = (ends - padded)[None, :] + jnp.cumsum(cnt, axis=0) - cnt
    base = jnp.repeat(tile_base, tm, axis=0)
    experts = jnp.arange(n_exp, dtype=jnp.int32)[None, :]
    dest = []
    for k in range(TOP_K):
        e_k = rt[:, k].astype(jnp.int32)
        r_k = rt[:, 2 * TOP_K + k].astype(jnp.int32)
        dest.append(jnp.sum(jnp.where(e_k[:, None] == experts, base, 0), axis=1) + r_k)
    dest = jnp.stack(dest, axis=0)
    tile_start = jnp.arange(n_tiles, dtype=jnp.int32) * tm
    tile_expert = jnp.minimum(jnp.sum(tile_start[:, None] >= ends[None, :], axis=1), n_exp - 1)
    n_used = (ends[-1] // tm).reshape(1)
    n_valid = jnp.clip((ends - padded + sizes)[tile_expert] - tile_start, 0, tm)
    return dest, tile_expert.astype(jnp.int32), n_used.astype(jnp.int32), n_valid.astype(jnp.int32)


def _sc_mesh():
    return plsc.VectorSubcoreMesh(core_axis_name="core", subcore_axis_name="subcore",
                                  num_cores=SC_CORES, num_subcores=SC_SUBCORES)


def _sc_scatter_rows(x, idx_list, n_rows):
    m, d = x.shape

    @functools.partial(pl.kernel, out_type=jax.ShapeDtypeStruct((n_rows, d), x.dtype), mesh=_sc_mesh(),
                       scratch_types=[], name="sc_scatter_rows")
    def scatter(x_hbm, *refs):
        i_hbms, o_hbm = refs[:-1], refs[-1]

        def body(x_vmem, *i_vmems):
            for i_vmem in i_vmems:
                pltpu.sync_copy(x_vmem, o_hbm.at[i_vmem.at[0]])

        pltpu.emit_pipeline(
            body,
            grid=(m // SC_WINDOW,),
            in_specs=[pl.BlockSpec((SC_WINDOW, d), index_map=lambda i: (i, 0))]
                     + [pl.BlockSpec((1, SC_WINDOW), index_map=lambda i: (0, i))] * len(idx_list),
            out_specs=[],
            core_axis_name="subcore",
            dimension_semantics=(pltpu.PARALLEL,),
        )(x_hbm, *i_hbms)

    return scatter(x, *[idx.reshape(1, m) for idx in idx_list])


def _grouped_ffn_kernel(te_ref, nu_ref, nv_ref, x_ref, wg_ref, wu_ref, wd_ref, y_ref, hb_ref, acc_ref):
    del te_ref
    i = pl.program_id(0)
    f = pl.program_id(1)
    used = i < nu_ref[0]
    last_f = f == pl.num_programs(1) - 1

    @pl.when(used & (f == 0))
    def _():
        words = jnp.concatenate([x_ref[p] for p in range(SC_ROW_PARTS)], axis=1)
        row = lax.broadcasted_iota(jnp.int32, words.shape, 0)
        words = jnp.where(row < nv_ref[i], words, U32(0))
        hb_ref[...] = _unpack_bf16_pairs(words).astype(BF16)
        acc_ref[...] = jnp.zeros(acc_ref.shape, F32)

    @pl.when(used)
    def _():
        h = hb_ref[...]
        a = _silu(_dot(h, wg_ref[0])) * _dot(h, wu_ref[0])
        acc_ref[...] += _dot(a.astype(BF16), wd_ref[0])

    @pl.when(used & last_f)
    def _():
        y = _pack_bf16_pairs(acc_ref[...])
        for part in range(SC_ROW_PARTS):
            y_ref[part] = y[:, part * y_ref.shape[2]:(part + 1) * y_ref.shape[2]]

    @pl.when(jnp.logical_not(used) & last_f)
    def _():
        y_ref[...] = jnp.zeros(y_ref.shape, y_ref.dtype)


def _grouped_ffn(tile_expert, n_used, n_valid, xg, w_gu, w_down, *, tm, tf):
    parts, n_rows, dpp = xg.shape
    n_exp, d_ff, d = w_down.shape
    nf = d_ff // tf

    def wspec(shape, index):
        def index_map(i, f, te, nu, nv):
            return index(te[i], jnp.where(i < nu[0], f, nf - 1))
        return pl.BlockSpec(shape, index_map)

    rows = pl.BlockSpec((parts, tm, dpp), lambda i, f, te, nu, nv: (0, i, 0))
    return pl.pallas_call(
        _grouped_ffn_kernel,
        out_shape=jax.ShapeDtypeStruct(xg.shape, xg.dtype),
        grid_spec=pltpu.PrefetchScalarGridSpec(
            num_scalar_prefetch=3,
            grid=(n_rows // tm, nf),
            in_specs=[rows,
                      wspec((1, d, tf), lambda e, f: (e, 0, f)),
                      wspec((1, d, tf), lambda e, f: (e, 0, nf + f)),
                      wspec((1, tf, d), lambda e, f: (e, f, 0))],
            out_specs=rows,
            scratch_shapes=[pltpu.VMEM((tm, d), BF16), pltpu.VMEM((tm, d), F32)]),
        compiler_params=_cparams(("arbitrary", "arbitrary")),
        name="moe_grouped_ffn",
    )(tile_expert, n_used, n_valid, xg, w_gu, w_gu, w_down)


def _sc_gather_rows(table, idx):
    n_idx = idx.shape[0]
    d = table.shape[1]

    @functools.partial(pl.kernel, out_type=jax.ShapeDtypeStruct((n_idx, d), table.dtype), mesh=_sc_mesh(),
                       name="sc_gather_rows")
    def gather(t_hbm, i_hbm, o_hbm):
        def body(i_vmem, o_vmem):
            pltpu.sync_copy(t_hbm.at[i_vmem.at[0]], o_vmem)

        pltpu.emit_pipeline(
            body,
            grid=(n_idx // SC_WINDOW,),
            in_specs=[pl.BlockSpec((1, SC_WINDOW), index_map=lambda i: (0, i))],
            out_specs=[pl.BlockSpec((SC_WINDOW, d), index_map=lambda i: (i, 0))],
            core_axis_name="subcore",
            dimension_semantics=(pltpu.PARALLEL,),
        )(i_hbm, o_hbm)

    return gather(table, idx.reshape(1, n_idx))


def _combine_kernel(x_ref, rt_ref, gt_ref, gf_ref, *refs):
    y_refs, o_ref = refs[:-1], refs[-1]
    rt = rt_ref[...]
    tot = None
    for k in range(TOP_K):
        words = jnp.concatenate([y_refs[p * TOP_K + k][...] for p in range(SC_ROW_PARTS)], axis=1)
        term = rt[:, TOP_K + k:TOP_K + k + 1] * _unpack_bf16_pairs(words)
        tot = term if tot is None else tot + term
    x = x_ref[...] + gt_ref[...] * tot
    o_ref[...] = _rms(x, gf_ref[...])


def _combine(x, rt, mod1, g_final, ysel, *, seq, tm):
    n, d = x.shape
    tpb = seq // tm
    nt = n // tm
    row = lambda w: pl.BlockSpec((tm, w), lambda i: (i, 0))
    piece = lambda j: pl.BlockSpec((tm, ysel.shape[1]), lambda i: (j * nt + i, 0))
    n_pieces = SC_ROW_PARTS * TOP_K
    return pl.pallas_call(
        _combine_kernel,
        out_shape=jax.ShapeDtypeStruct((n, d), F32),
        grid=(nt,),
        in_specs=[row(d), row(LANES), _mod_spec(5, tpb, d), pl.BlockSpec(g_final.shape, lambda i: (0, 0))]
                 + [piece(j) for j in range(n_pieces)],
        out_specs=row(d),
        compiler_params=_cparams(("parallel",)),
        name="moe_combine",
    )(x, rt, mod1, g_final, *([ysel] * n_pieces))


def _pad_last(a, width):
    return jnp.pad(a, [(0, 0)] * (a.ndim - 1) + [(0, width - a.shape[-1])])


def kernel(x, c, positions, w_mod, b_mod, g_mix, g_ffn, w_a_down, g_q_lat, g_kv_lat, w_uq, w_ukv, w_oa,
           w_mod_kv, b_mod_kv, g_kv, w_kv_sb, w_q_sb, w_o_sb, w_ffn_gu, w_ffn_down, w_router, b_router,
           w_exp_gu, w_exp_down, g_final):
    bsz, seq, d = x.shape
    n = bsz * seq
    q_lora, kv_lora = g_q_lat.shape[1], g_kv_lat.shape[1]
    n_exp = w_router.shape[-1]
    d_ff = w_ffn_down.shape[1]
    tm = min(TOKEN_TILE, seq)
    t_mla = min(MLA_TILE, seq)
    t_sb = min(SB_TILE, seq)
    tf = next((t for t in (FFN_CHUNK, 512) if d_ff % t == 0), d_ff)

    mod0 = _modvec(c, w_mod[0], b_mod[0]).reshape(bsz, 6, 1, d)
    mod1 = _modvec(c, w_mod[1], b_mod[1]).reshape(bsz, 6, 1, d)
    mod_kv = _modvec(c, w_mod_kv, b_mod_kv).reshape(bsz, 2, 1, d)

    lat_w = q_lora + kv_lora + LANES
    wd = _pad_last(w_a_down[0], lat_w).astype(BF16)
    wq = w_uq[0].reshape(q_lora, MLA_HEADS, QK_NOPE + QK_ROPE)
    wqn = wq[:, :, :QK_NOPE].reshape(q_lora, MLA_HEADS * QK_NOPE).astype(BF16)
    wqr = _pad_last(wq[:, :, QK_NOPE:], LANES).reshape(q_lora, MLA_HEADS * LANES).astype(BF16)
    wkv = w_ukv[0].reshape(kv_lora, MLA_HEADS, QK_NOPE + V_DIM)
    wkn = wkv[:, :, :QK_NOPE].reshape(kv_lora, MLA_HEADS * QK_NOPE).astype(BF16)
    wv = wkv[:, :, QK_NOPE:].reshape(kv_lora, MLA_HEADS * V_DIM).astype(BF16)
    half = QK_ROPE // 2
    inv = ROPE_THETA ** (-jnp.arange(half, dtype=F32) / half)
    inv = jnp.tile(inv, ROPE_PACK).reshape(1, LANES)
    wr = _pad_last(w_router[0], LANES)
    wr_hi = wr.astype(BF16)
    wr_lo = (wr - wr_hi.astype(F32)).astype(BF16)
    br = _pad_last(b_router[0], LANES).reshape(1, LANES)

    xf = x.reshape(n, d)
    pos = positions.reshape(n // tm, ROPE_PACK, tm // ROPE_PACK).swapaxes(1, 2)
    pos = jnp.repeat(pos.reshape(n // ROPE_PACK, ROPE_PACK), QK_ROPE // 2, axis=1)
    row1 = lambda a: a.reshape(1, -1)

    q, k, v = _mla_proj(xf, pos, mod0, row1(g_mix[0]), wd, row1(g_q_lat[0]), row1(g_kv_lat[0]),
                        wqn, wqr, wkn, wv, inv, seq=seq, tm=tm)
    o = _mla_attn(q.reshape(bsz, seq, -1), k.reshape(bsz, seq, -1), v.reshape(bsz, seq, -1), t=t_mla)
    x2, hk, hm = _ffn(o.reshape(n, -1), w_oa[0].astype(BF16), xf, mod0, row1(g_ffn[0]),
                      w_ffn_gu[0].astype(BF16), w_ffn_down[0].astype(BF16),
                      row1(g_kv), mod_kv, row1(g_mix[1]), mod1, seq=seq, tm=tm, tf=tf)
    kv = _linear(hk, w_kv_sb.astype(BF16), tm=tm, tn=w_kv_sb.shape[1], name="kv_proj")
    q_scale = math.log2(math.e) / math.sqrt(SB_HEAD_DIM)
    qs = _linear(hm, (w_q_sb[0] * q_scale).astype(BF16), tm=tm, tn=1024, name="q_proj")
    o = _sb_attn(qs.reshape(bsz, seq, -1), kv.reshape(bsz, seq, -1), t=t_sb)
    x3, h, rt, counts = _out_proj_router(o.reshape(n, -1), w_o_sb[0].astype(BF16), x2, mod1, row1(g_ffn[1]),
                                         wr_hi, wr_lo, br, n_exp, seq=seq, tm=tm)
    n_tiles = TOP_K * n // tm + n_exp
    n_rows = n_tiles * tm
    dest, tile_expert, n_used, n_valid = _route_plan(rt, counts, n_exp, tm, n_tiles)
    picks = dest[None] + n_rows * jnp.arange(SC_ROW_PARTS, dtype=jnp.int32)[:, None, None]
    xg = _sc_scatter_rows(h.reshape(SC_ROW_PARTS * n, -1), [picks[:, k].reshape(-1) for k in range(TOP_K)],
                          SC_ROW_PARTS * n_rows)
    yg = _grouped_ffn(tile_expert, n_used, n_valid, xg.reshape(SC_ROW_PARTS, n_rows, -1),
                      w_exp_gu[0].astype(BF16), w_exp_down[0].astype(BF16), tm=tm, tf=tf)
    ysel = _sc_gather_rows(yg.reshape(SC_ROW_PARTS * n_rows, -1), picks.reshape(-1))
    out = _combine(x3, rt, mod1, row1(g_final), ysel, seq=seq, tm=tm)
    return out.reshape(bsz, seq, d)
```

```python
import functools
import math

import jax
import jax.numpy as jnp
from jax import lax
from jax.experimental import pallas as pl
from jax.experimental.pallas import tpu as pltpu
from jax.experimental.pallas import tpu_sc as plsc

F32 = jnp.float32
BF16 = jnp.bfloat16

EPS = 1e-6
MLA_HEADS = 8
QK_NOPE = 128
QK_ROPE = 64
V_DIM = 128
ROPE_THETA = 10000.0
SB_HEADS = 8
SB_HEAD_DIM = 128
TOP_K = 2

LANES = 128
QK_PAD = 256
ROPE_PACK = LANES // (QK_ROPE // 2)
VMEM_LIMIT = 48 * 1024 * 1024
VMEM_LIMIT_LARGE = 58 * 1024 * 1024
TOKEN_TILE = 512
MLA_TILE = 512
MLA_BLOCKS_PER_TRIP = 4
SB_TILE = 512
FFN_CHUNK = 1792
SC_CORES = 2
SC_SUBCORES = 16
SC_WINDOW = 128
SC_ROW_PARTS = 2


def _cparams(sem, vmem_limit=VMEM_LIMIT):
    return pltpu.CompilerParams(dimension_semantics=sem, vmem_limit_bytes=vmem_limit)


def _rms(x, g):
    return x * lax.rsqrt(jnp.mean(x * x, axis=-1, keepdims=True) + EPS) * g


def _modulate(h, shift, scale):
    return h * (1.0 + scale) + shift


def _split_bf16(a):
    hi = a.astype(BF16)
    lo = (a - hi.astype(F32)).astype(BF16)
    return hi, lo


U32 = jnp.uint32
HIGH_HALF = 0xFFFF0000


def _pack_bf16_pairs(a):
    half = a.shape[1] // 2
    bits = lambda v: lax.bitcast_convert_type(v.astype(BF16).astype(F32), U32)
    return (bits(a[:, :half]) >> 16) | (bits(a[:, half:]) & U32(HIGH_HALF))


def _unpack_bf16_pairs(w):
    lo = lax.bitcast_convert_type(w << 16, F32)
    hi = lax.bitcast_convert_type(w & U32(HIGH_HALF), F32)
    return jnp.concatenate([lo, hi], axis=1)


def _dot(a, b):
    return jnp.dot(a, b, preferred_element_type=F32)


def _dot_nt(a, b):
    return lax.dot_general(a, b, (((1,), (1,)), ((), ())), preferred_element_type=F32)


def _modvec_kernel(c_ref, w_ref, b_ref, o_ref):
    c = c_ref[...]
    sc = c * (1.0 / (1.0 + jnp.exp(-c)))
    a_hi, a_lo = _split_bf16(sc)
    w_hi, w_lo = _split_bf16(w_ref[...])
    o_ref[...] = _dot(a_hi, w_hi) + _dot(a_lo, w_hi) + _dot(a_hi, w_lo) + b_ref[...]


def _modvec(c, w, b, tn=512):
    bsz, d = c.shape
    n = w.shape[1]
    return pl.pallas_call(
        _modvec_kernel,
        out_shape=jax.ShapeDtypeStruct((bsz, n), F32),
        grid=(n // tn,),
        in_specs=[pl.BlockSpec((bsz, d), lambda j: (0, 0)),
                  pl.BlockSpec((d, tn), lambda j: (0, j)),
                  pl.BlockSpec((1, tn), lambda j: (0, j))],
        out_specs=pl.BlockSpec((bsz, tn), lambda j: (0, j)),
        compiler_params=_cparams(("arbitrary",)),
        name="modvec",
    )(c, w, b.reshape(1, n))


def _mod_spec(chunk, tiles_per_batch, d):
    return pl.BlockSpec((None, None, 1, d), lambda i, *_: (i // tiles_per_batch, chunk, 0, 0))


def _mla_proj_kernel(x_ref, pos_ref, sh_ref, sc_ref, g_ref, wd_ref, gq_ref, gkv_ref,
                     wqn_ref, wqr_ref, wkn_ref, wv_ref, inv_ref,
                     q_ref, k_ref, v_ref, *, q_lora, kv_lora):
    x = x_ref[...]
    h = _modulate(_rms(x, g_ref[...]), sh_ref[...], sc_ref[...]).astype(BF16)
    lat = _dot(h, wd_ref[...])
    c_q = _rms(lat[:, :q_lora], gq_ref[...]).astype(BF16)
    c_kv = _rms(lat[:, q_lora:q_lora + kv_lora], gkv_ref[...]).astype(BF16)
    k_rot = lat[:, q_lora + kv_lora:]

    half = QK_ROPE // 2
    ang = pos_ref[...].astype(F32) * inv_ref[...]
    cos_p = jnp.cos(ang)
    sin_p = jnp.sin(ang)
    lane = lax.broadcasted_iota(jnp.int32, ang.shape, 1)
    cos, s_a, s_b = [], [], []
    for g in range(ROPE_PACK):
        shift = (LANES - g * half) % LANES
        cg = pltpu.roll(cos_p, shift, axis=1) if shift else cos_p
        sg = pltpu.roll(sin_p, shift, axis=1) if shift else sin_p
        cos.append(jnp.where(lane < half, cg, pltpu.roll(cg, half, axis=1)))
        s_a.append(jnp.where(lane < half, -sg, 0.0))
        s_b.append(jnp.where((lane >= half) & (lane < 2 * half), pltpu.roll(sg, half, axis=1), 0.0))
    cos, s_a, s_b = (jnp.concatenate(v, axis=0) for v in (cos, s_a, s_b))

    def rope(r):
        return (r * cos + pltpu.roll(r, LANES - half, axis=1) * s_a
                + pltpu.roll(r, half, axis=1) * s_b)

    k_rot = rope(k_rot).astype(BF16)
    q_nope = _dot(c_q, wqn_ref[...])
    q_rope = _dot(c_q, wqr_ref[...])
    k_nope = _dot(c_kv, wkn_ref[...])
    v_ref[...] = _dot(c_kv, wv_ref[...]).astype(BF16)
    for hd in range(MLA_HEADS):
        a, b = hd * LANES, (hd + 1) * LANES
        q_ref[:, hd * QK_PAD:hd * QK_PAD + LANES] = q_nope[:, a:b].astype(BF16)
        q_ref[:, hd * QK_PAD + LANES:(hd + 1) * QK_PAD] = rope(q_rope[:, a:b]).astype(BF16)
        k_ref[:, hd * QK_PAD:hd * QK_PAD + LANES] = k_nope[:, a:b].astype(BF16)
        k_ref[:, hd * QK_PAD + LANES:(hd + 1) * QK_PAD] = k_rot


def _mla_proj(x, pos, mod, g_mix, wd, gq, gkv, wqn, wqr, wkn, wv, inv, *, seq, tm):
    n, d = x.shape
    tpb = seq // tm
    q_lora, kv_lora = gq.shape[1], gkv.shape[1]
    hq = MLA_HEADS * QK_PAD
    hv = MLA_HEADS * V_DIM
    full = lambda a: pl.BlockSpec(a.shape, lambda i: (0,) * a.ndim)
    row = lambda w: pl.BlockSpec((tm, w), lambda i: (i, 0))
    return pl.pallas_call(
        functools.partial(_mla_proj_kernel, q_lora=q_lora, kv_lora=kv_lora),
        out_shape=(jax.ShapeDtypeStruct((n, hq), BF16), jax.ShapeDtypeStruct((n, hq), BF16),
                   jax.ShapeDtypeStruct((n, hv), BF16)),
        grid=(n // tm,),
        in_specs=[row(d), pl.BlockSpec((tm // ROPE_PACK, LANES), lambda i: (i, 0)), _mod_spec(0, tpb, d), _mod_spec(1, tpb, d), full(g_mix), full(wd),
                  full(gq), full(gkv), full(wqn), full(wqr), full(wkn), full(wv), full(inv)],
        out_specs=(row(hq), row(hq), row(hv)),
        compiler_params=_cparams(("parallel",)),
        name="mla_proj",
    )(x, pos, mod, mod, g_mix, wd, gq, gkv, wqn, wqr, wkn, wv, inv)


def _lane_chunks(a):
    return [a[:, c * LANES:(c + 1) * LANES] for c in range(a.shape[1] // LANES)]


def _for_blocks(n, step, group):
    def trip(p, carry):
        step(group * p, group)
        return carry

    lax.fori_loop(0, n // group, trip, 0)
    size = group // 2
    while size:
        @pl.when((n // size) % 2 == 1)
        def _(size=size):
            step(n // (2 * size) * (2 * size), size)
        size //= 2


def _mla_attn_kernel(q_ref, k_ref, v_ref, o_ref, s_ref, m_ref, acc_ref, *, t, scale, heads):
    i = pl.program_id(2)

    def lane_max(s):
        m = None
        for sc in _lane_chunks(s):
            m = sc if m is None else jnp.maximum(m, sc)
        return m

    def scores(h, j):
        start = pl.multiple_of(j * t, t)
        hs = slice(h * QK_PAD, (h + 1) * QK_PAD)
        return _dot_nt(q_ref[0, :, hs], k_ref[0, pl.ds(start, t), hs])

    r_minus_c = (lax.broadcasted_iota(jnp.int32, (t, t), 0) - lax.broadcasted_iota(jnp.int32, (t, t), 1))
    m_ref[...] = jnp.full(m_ref.shape, -jnp.inf, F32)

    def pass1(j, nb):
        for h in range(heads):
            m = m_ref[h]
            for jj in range(nb):
                s = jnp.where(r_minus_c >= (j + jj - i) * t, scores(h, j + jj), -jnp.inf)
                s_ref[h, j + jj] = s
                m = jnp.maximum(m, lane_max(s))
            m_ref[h] = m

    _for_blocks(i + 1, pass1, MLA_BLOCKS_PER_TRIP)
    for h in range(heads):
        m_ref[h] = jnp.broadcast_to(jnp.max(m_ref[h], axis=1, keepdims=True), (t, LANES))
    acc_ref[...] = jnp.zeros(acc_ref.shape, F32)
    cst = scale * math.log2(math.e)

    def pass2(j, nb):
        start = pl.multiple_of(j * t, t)
        ones = jnp.ones((nb * t, LANES), BF16)
        for h in range(heads):
            m = m_ref[h]
            p = jnp.concatenate([jnp.exp2((sc - m) * cst).astype(BF16)
                                 for jj in range(nb) for sc in _lane_chunks(s_ref[h, j + jj])], axis=1)
            v_ext = jnp.concatenate([v_ref[0, pl.ds(start, nb * t), h * V_DIM:(h + 1) * V_DIM], ones], axis=1)
            acc_ref[h] += _dot(p, v_ext)

    _for_blocks(i + 1, pass2, MLA_BLOCKS_PER_TRIP)
    for h in range(heads):
        acc = acc_ref[h]
        o_ref[0, :, h * V_DIM:(h + 1) * V_DIM] = (acc[:, :V_DIM] / acc[:, V_DIM:]).astype(o_ref.dtype)


def _mla_attn(q, k, v, *, t, heads=2):
    bsz, seq, _ = q.shape
    scale = 1.0 / math.sqrt(QK_NOPE + QK_ROPE)
    wq, wv = heads * QK_PAD, heads * V_DIM
    return pl.pallas_call(
        functools.partial(_mla_attn_kernel, t=t, scale=scale, heads=heads),
        out_shape=jax.ShapeDtypeStruct((bsz, seq, MLA_HEADS * V_DIM), BF16),
        grid=(bsz, MLA_HEADS // heads, seq // t),
        in_specs=[pl.BlockSpec((1, t, wq), lambda b, g, i: (b, i, g)),
                  pl.BlockSpec((1, seq, wq), lambda b, g, i: (b, 0, g)),
                  pl.BlockSpec((1, seq, wv), lambda b, g, i: (b, 0, g))],
        out_specs=pl.BlockSpec((1, t, wv), lambda b, g, i: (b, i, g)),
        scratch_shapes=[pltpu.VMEM((heads, seq // t, t, t), F32), pltpu.VMEM((heads, t, LANES), F32),
                        pltpu.VMEM((heads, t, V_DIM + LANES), F32)],
        compiler_params=_cparams(("parallel", "parallel", "arbitrary")),
        name="mla_attn",
    )(q, k, v)


def _sb_attn_kernel(q_ref, k_ref, v_ref, w_ref, o_ref, c_ref, acc_ref, *, t, heads):
    i = pl.program_id(2)
    nl = t // LANES
    d = SB_HEAD_DIM
    c_ref[...] = jnp.zeros(c_ref.shape, F32)
    acc_ref[...] = jnp.zeros(acc_ref.shape, F32)
    sub = t // 2 if t % (2 * LANES) == 0 else t
    r = lax.broadcasted_iota(jnp.int32, (sub, sub), 0)
    c = lax.broadcasted_iota(jnp.int32, (sub, sub), 1)
    strict = c < r

    def block(r0, nr, start, nk, masked):
        rows = slice(r0, r0 + nr)
        for h in range(heads):
            hs = slice(h * d, (h + 1) * d)
            zz = _dot_nt(q_ref[0, rows, hs], k_ref[0, pl.ds(start, nk), hs])
            zneg = -zz
            sp = jnp.log2(1.0 + jnp.exp2(jnp.minimum(zz, zneg)))
            log_1m = jnp.minimum(zneg, 0.0) - sp
            if masked:
                log_1m = jnp.where(strict, log_1m, 0.0)
            hi, lo = _split_bf16(log_1m)
            carry = c_ref[h, rows]
            a_chunks = [None] * (nk // LANES)
            for cc in reversed(range(nk // LANES)):
                cs = slice(cc * LANES, (cc + 1) * LANES)
                y = _dot(jnp.concatenate([hi[:, cs], lo[:, cs]], axis=1), w_ref[...])
                a = jnp.exp2(zz[:, cs] + y[:, :LANES] + carry)
                carry = carry + y[:, LANES:]
                if masked:
                    a = jnp.where(strict[:, cs], a, 0.0)
                a_chunks[cc] = a.astype(BF16)
            c_ref[h, rows] = carry
            acc_ref[h, rows] += _dot(jnp.concatenate(a_chunks, axis=1), v_ref[0, pl.ds(start, nk), hs])

    for band in range(t // sub):
        for piece in reversed(range(band + 1)):
            block(band * sub, sub, pl.multiple_of(i * t + piece * sub, sub), sub, piece == band)

    def body(n, carry):
        block(0, t, pl.multiple_of((i - 1 - n) * t, t), t, False)
        return carry

    lax.fori_loop(0, i, body, 0)
    for h in range(heads):
        o_ref[0, :, h * d:(h + 1) * d] = acc_ref[h].astype(o_ref.dtype)


def _sb_attn(q, kv, *, t, heads=4):
    bsz, seq, _ = q.shape
    w = heads * SB_HEAD_DIM
    groups = SB_HEADS // heads
    tri = (jnp.arange(LANES)[:, None] >= jnp.arange(LANES)[None, :]).astype(BF16)
    half = jnp.concatenate([tri, jnp.ones((LANES, LANES), BF16)], axis=1)
    w_sum = jnp.concatenate([half, half], axis=0)
    return pl.pallas_call(
        functools.partial(_sb_attn_kernel, t=t, heads=heads),
        out_shape=jax.ShapeDtypeStruct((bsz, seq, SB_HEADS * SB_HEAD_DIM), BF16),
        grid=(bsz, groups, seq // t),
        in_specs=[pl.BlockSpec((1, t, w), lambda b, g, i: (b, i, g)),
                  pl.BlockSpec((1, seq, w), lambda b, g, i: (b, 0, g)),
                  pl.BlockSpec((1, seq, w), lambda b, g, i: (b, 0, groups + g)),
                  pl.BlockSpec(w_sum.shape, lambda b, g, i: (0, 0))],
        out_specs=pl.BlockSpec((1, t, w), lambda b, g, i: (b, i, g)),
        scratch_shapes=[pltpu.VMEM((heads, t, LANES), F32), pltpu.VMEM((heads, t, SB_HEAD_DIM), F32)],
        compiler_params=_cparams(("parallel", "parallel", "arbitrary")),
        name="sb_attn",
    )(q, kv, kv, w_sum)


def _out_proj_kernel(o_ref, w_ref, x_ref, gt_ref, g_ref, sh_ref, sc_ref, wr_hi_ref, wr_lo_ref, br_ref, tri_ref,
                     xo_ref, h_ref, rt_ref, cnt_ref, *, n_experts):
    x = x_ref[...] + gt_ref[...] * _dot(o_ref[...], w_ref[...])
    xo_ref[...] = x
    h = _modulate(_rms(x, g_ref[...]), sh_ref[...], sc_ref[...])
    h_hi = h.astype(BF16)
    words = _pack_bf16_pairs(h)
    for part in range(SC_ROW_PARTS):
        h_ref[part] = words[:, part * h_ref.shape[2]:(part + 1) * h_ref.shape[2]]
    h_lo = (h - h_hi.astype(F32)).astype(BF16)
    logits = (_dot(h_hi, wr_hi_ref[...]) + _dot(h_lo, wr_hi_ref[...]) + _dot(h_hi, wr_lo_ref[...])
              + br_ref[...])
    lane = lax.broadcasted_iota(jnp.int32, logits.shape, 1).astype(F32)
    lg = jnp.where(lane < n_experts, logits, -jnp.inf)
    m1 = jnp.max(lg, axis=1, keepdims=True)
    i1 = jnp.min(jnp.where(lg == m1, lane, float(LANES)), axis=1, keepdims=True)
    lg2 = jnp.where(lane == i1, -jnp.inf, lg)
    m2 = jnp.max(lg2, axis=1, keepdims=True)
    i2 = jnp.min(jnp.where(lg2 == m2, lane, float(LANES)), axis=1, keepdims=True)
    e2 = jnp.exp(m2 - m1)
    den = 1.0 + e2
    sel = jnp.where((lane == i1) | (lane == i2), 1.0, 0.0)
    prefix = _dot(tri_ref[...], sel.astype(BF16))
    r1 = jnp.sum(jnp.where(lane == i1, prefix, 0.0), axis=1, keepdims=True)
    r2 = jnp.sum(jnp.where(lane == i2, prefix, 0.0), axis=1, keepdims=True)
    cnt_ref[...] = jnp.sum(sel, axis=0, keepdims=True)
    rt = jnp.zeros_like(logits)
    for k, val in enumerate((i1, i2, 1.0 / den, e2 / den, r1, r2)):
        rt = jnp.where(lane == k, val, rt)
    rt_ref[...] = rt


def _out_proj_router(o, w, x, mod, g, wr_hi, wr_lo, br, n_experts, *, seq, tm):
    n, d = x.shape
    tpb = seq // tm
    dpp = d // 2 // SC_ROW_PARTS
    tri = (jnp.arange(tm)[:, None] > jnp.arange(tm)[None, :]).astype(BF16)
    full = lambda a: pl.BlockSpec(a.shape, lambda i: (0,) * a.ndim)
    row = lambda wd: pl.BlockSpec((tm, wd), lambda i: (i, 0))
    return pl.pallas_call(
        functools.partial(_out_proj_kernel, n_experts=n_experts),
        out_shape=(jax.ShapeDtypeStruct((n, d), F32), jax.ShapeDtypeStruct((SC_ROW_PARTS, n, dpp), U32),
                   jax.ShapeDtypeStruct((n, LANES), F32), jax.ShapeDtypeStruct((n // tm, 1, LANES), F32)),
        grid=(n // tm,),
        in_specs=[row(o.shape[1]), full(w), row(d), _mod_spec(2, tpb, d), full(g),
                  _mod_spec(3, tpb, d), _mod_spec(4, tpb, d), full(wr_hi), full(wr_lo), full(br), full(tri)],
        out_specs=(row(d), pl.BlockSpec((SC_ROW_PARTS, tm, dpp), lambda i: (0, i, 0)), row(LANES),
                   pl.BlockSpec((None, 1, LANES), lambda i: (i, 0, 0))),
        compiler_params=_cparams(("parallel",)),
        name="out_proj_router",
    )(o, w, x, mod, g, mod, mod, wr_hi, wr_lo, br, tri)


def _silu(g):
    return g * (1.0 / (1.0 + jnp.exp(-g)))


def _ffn_kernel(o_ref, wo_ref, x_ref, gt1_ref, gf_ref, sh2_ref, sc2_ref, wg_ref, wu_ref, wd_ref, gt2_ref,
                gkv_ref, shk_ref, sck_ref, gm_ref, shm_ref, scm_ref,
                xo_ref, hk_ref, hm_ref, x1_ref, h_ref, acc_ref):
    f = pl.program_id(1)

    @pl.when(f == 0)
    def _():
        x1 = x_ref[...] + gt1_ref[...] * _dot(o_ref[...], wo_ref[...])
        x1_ref[...] = x1
        h_ref[...] = _modulate(_rms(x1, gf_ref[...]), sh2_ref[...], sc2_ref[...]).astype(BF16)
        acc_ref[...] = jnp.zeros(acc_ref.shape, F32)

    h = h_ref[...]
    a = _silu(_dot(h, wg_ref[...])) * _dot(h, wu_ref[...])
    acc_ref[...] += _dot(a.astype(BF16), wd_ref[...])

    @pl.when(f == pl.num_programs(1) - 1)
    def _():
        x = x1_ref[...] + gt2_ref[...] * acc_ref[...]
        xo_ref[...] = x
        hk_ref[...] = _modulate(_rms(x, gkv_ref[...]), shk_ref[...], sck_ref[...]).astype(BF16)
        hm_ref[...] = _modulate(_rms(x, gm_ref[...]), shm_ref[...], scm_ref[...]).astype(BF16)


def _ffn(o, w_o, x, mod0, g_ffn, w_gu, w_down, g_kv, mod_kv, g_mix1, mod1, *, seq, tm, tf):
    n, d = x.shape
    d_ff = w_down.shape[0]
    nf = d_ff // tf
    tpb = seq // tm
    full = lambda a: pl.BlockSpec(a.shape, lambda i, f: (0,) * a.ndim)
    row = lambda w=d: pl.BlockSpec((tm, w), lambda i, f: (i, 0))
    return pl.pallas_call(
        _ffn_kernel,
        out_shape=(jax.ShapeDtypeStruct((n, d), F32), jax.ShapeDtypeStruct((n, d), BF16),
                   jax.ShapeDtypeStruct((n, d), BF16)),
        grid=(n // tm, nf),
        in_specs=[row(o.shape[1]), full(w_o), row(), _mod_spec(2, tpb, d), full(g_ffn),
                  _mod_spec(3, tpb, d), _mod_spec(4, tpb, d),
                  pl.BlockSpec((d, tf), lambda i, f: (0, f)),
                  pl.BlockSpec((d, tf), lambda i, f: (0, nf + f)),
                  pl.BlockSpec((tf, d), lambda i, f: (f, 0)),
                  _mod_spec(5, tpb, d),
                  full(g_kv), _mod_spec(0, tpb, d), _mod_spec(1, tpb, d),
                  full(g_mix1), _mod_spec(0, tpb, d), _mod_spec(1, tpb, d)],
        out_specs=(row(), row(), row()),
        scratch_shapes=[pltpu.VMEM((tm, d), F32), pltpu.VMEM((tm, d), BF16), pltpu.VMEM((tm, d), F32)],
        compiler_params=_cparams(("parallel", "arbitrary"), vmem_limit=VMEM_LIMIT_LARGE),
        name="ffn_dense",
    )(o, w_o, x, mod0, g_ffn, mod0, mod0, w_gu, w_gu, w_down, mod0,
      g_kv, mod_kv, mod_kv, g_mix1, mod1, mod1)


def _linear_kernel(x_ref, w_ref, o_ref):
    o_ref[...] = _dot(x_ref[...], w_ref[...]).astype(o_ref.dtype)


def _linear(x, w, *, tm, tn, name):
    n, k = x.shape
    m = w.shape[1]
    return pl.pallas_call(
        _linear_kernel,
        out_shape=jax.ShapeDtypeStruct((n, m), BF16),
        grid=(m // tn, n // tm),
        in_specs=[pl.BlockSpec((tm, k), lambda j, i: (i, 0)), pl.BlockSpec((k, tn), lambda j, i: (0, j))],
        out_specs=pl.BlockSpec((tm, tn), lambda j, i: (i, j)),
        compiler_params=_cparams(("parallel", "parallel")),
        name=name,
    )(x, w)


def _route_plan(rt, counts, n_exp, tm, n_tiles):
    cnt = counts[:, 0, :n_exp].astype(jnp.int32)
    sizes = jnp.sum(cnt, axis=0)
    padded = (sizes + tm - 1) // tm * tm
    ends = jnp.cumsum(padded)
    tile_base = (ends - padded)[None, :] + jnp.cumsum(cnt, axis=0) - cnt
    base = jnp.repeat(tile_base, tm, axis=0)
    experts = jnp.arange(n_exp, dtype=jnp.int32)[None, :]
    dest = []
    for k in range(TOP_K):
        e_k = rt[:, k].astype(jnp.int32)
        r_k = rt[:, 2 * TOP_K + k].astype(jnp.int32)
        dest.append(jnp.sum(jnp.where(e_k[:, None] == experts, base, 0), axis=1) + r_k)
    dest = jnp.stack(dest, axis=0)
    tile_start = jnp.arange(n_tiles, dtype=jnp.int32) * tm
    tile_expert = jnp.minimum(jnp.sum(tile_start[:, None] >= ends[None, :], axis=1), n_exp - 1)
    n_used = (ends[-1] // tm).reshape(1)
    n_valid = jnp.clip((ends - padded + sizes)[tile_expert] - tile_start, 0, tm)
    return dest, tile_expert.astype(jnp.int32), n_used.astype(jnp.int32), n_valid.astype(jnp.int32)


def _sc_mesh():
    return plsc.VectorSubcoreMesh(core_axis_name="core", subcore_axis_name="subcore",
                                  num_cores=SC_CORES, num_subcores=SC_SUBCORES)


def _sc_scatter_rows(x, idx_list, n_rows):
    m, d = x.shape

    @functools.partial(pl.kernel, out_type=jax.ShapeDtypeStruct((n_rows, d), x.dtype), mesh=_sc_mesh(),
                       scratch_types=[], name="sc_scatter_rows")
    def scatter(x_hbm, *refs):
        i_hbms, o_hbm = refs[:-1], refs[-1]

        def body(x_vmem, *i_vmems):
            for i_vmem in i_vmems:
                pltpu.sync_copy(x_vmem, o_hbm.at[i_vmem.at[0]])

        pltpu.emit_pipeline(
            body,
            grid=(m // SC_WINDOW,),
            in_specs=[pl.BlockSpec((SC_WINDOW, d), index_map=lambda i: (i, 0))]
                     + [pl.BlockSpec((1, SC_WINDOW), index_map=lambda i: (0, i))] * len(idx_list),
            out_specs=[],
            core_axis_name="subcore",
            dimension_semantics=(pltpu.PARALLEL,),
        )(x_hbm, *i_hbms)

    return scatter(x, *[idx.reshape(1, m) for idx in idx_list])


def _grouped_ffn_kernel(te_ref, nu_ref, nv_ref, x_ref, wg_ref, wu_ref, wd_ref, y_ref, hb_ref, acc_ref):
    del te_ref
    i = pl.program_id(0)
    f = pl.program_id(1)
    used = i < nu_ref[0]
    last_f = f == pl.num_programs(1) - 1

    @pl.when(used & (f == 0))
    def _():
        words = jnp.concatenate([x_ref[p] for p in range(SC_ROW_PARTS)], axis=1)
        row = lax.broadcasted_iota(jnp.int32, words.shape, 0)
        words = jnp.where(row < nv_ref[i], words, U32(0))
        hb_ref[...] = _unpack_bf16_pairs(words).astype(BF16)
        acc_ref[...] = jnp.zeros(acc_ref.shape, F32)

    @pl.when(used)
    def _():
        h = hb_ref[...]
        a = _silu(_dot(h, wg_ref[0])) * _dot(h, wu_ref[0])
        acc_ref[...] += _dot(a.astype(BF16), wd_ref[0])

    @pl.when(used & last_f)
    def _():
        y = _pack_bf16_pairs(acc_ref[...])
        for part in range(SC_ROW_PARTS):
            y_ref[part] = y[:, part * y_ref.shape[2]:(part + 1) * y_ref.shape[2]]

    @pl.when(jnp.logical_not(used) & last_f)
    def _():
        y_ref[...] = jnp.zeros(y_ref.shape, y_ref.dtype)


def _grouped_ffn(tile_expert, n_used, n_valid, xg, w_gu, w_down, *, tm, tf):
    parts, n_rows, dpp = xg.shape
    n_exp, d_ff, d = w_down.shape
    nf = d_ff // tf

    def wspec(shape, index):
        def index_map(i, f, te, nu, nv):
            return index(te[i], jnp.where(i < nu[0], f, nf - 1))
        return pl.BlockSpec(shape, index_map)

    rows = pl.BlockSpec((parts, tm, dpp), lambda i, f, te, nu, nv: (0, i, 0))
    return pl.pallas_call(
        _grouped_ffn_kernel,
        out_shape=jax.ShapeDtypeStruct(xg.shape, xg.dtype),
        grid_spec=pltpu.PrefetchScalarGridSpec(
            num_scalar_prefetch=3,
            grid=(n_rows // tm, nf),
            in_specs=[rows,
                      wspec((1, d, tf), lambda e, f: (e, 0, f)),
                      wspec((1, d, tf), lambda e, f: (e, 0, nf + f)),
                      wspec((1, tf, d), lambda e, f: (e, f, 0))],
            out_specs=rows,
            scratch_shapes=[pltpu.VMEM((tm, d), BF16), pltpu.VMEM((tm, d), F32)]),
        compiler_params=_cparams(("arbitrary", "arbitrary")),
        name="moe_grouped_ffn",
    )(tile_expert, n_used, n_valid, xg, w_gu, w_gu, w_down)


def _sc_gather_rows(table, idx):
    n_idx = idx.shape[0]
    d = table.shape[1]

    @functools.partial(pl.kernel, out_type=jax.ShapeDtypeStruct((n_idx, d), table.dtype), mesh=_sc_mesh(),
                       name="sc_gather_rows")
    def gather(t_hbm, i_hbm, o_hbm):
        def body(i_vmem, o_vmem):
            pltpu.sync_copy(t_hbm.at[i_vmem.at[0]], o_vmem)

        pltpu.emit_pipeline(
            body,
            grid=(n_idx // SC_WINDOW,),
            in_specs=[pl.BlockSpec((1, SC_WINDOW), index_map=lambda i: (0, i))],
            out_specs=[pl.BlockSpec((SC_WINDOW, d), index_map=lambda i: (i, 0))],
            core_axis_name="subcore",
            dimension_semantics=(pltpu.PARALLEL,),
        )(i_hbm, o_hbm)

    return gather(table, idx.reshape(1, n_idx))


def _combine_kernel(x_ref, rt_ref, gt_ref, gf_ref, *refs):
    y_refs, o_ref = refs[:-1], refs[-1]
    rt = rt_ref[...]
    tot = None
    for k in range(TOP_K):
        words = jnp.concatenate([y_refs[p * TOP_K + k][...] for p in range(SC_ROW_PARTS)], axis=1)
        term = rt[:, TOP_K + k:TOP_K + k + 1] * _unpack_bf16_pairs(words)
        tot = term if tot is None else tot + term
    x = x_ref[...] + gt_ref[...] * tot
    o_ref[...] = _rms(x, gf_ref[...])


def _combine(x, rt, mod1, g_final, ysel, *, seq, tm):
    n, d = x.shape
    tpb = seq // tm
    nt = n // tm
    row = lambda w: pl.BlockSpec((tm, w), lambda i: (i, 0))
    piece = lambda j: pl.BlockSpec((tm, ysel.shape[1]), lambda i: (j * nt + i, 0))
    n_pieces = SC_ROW_PARTS * TOP_K
    return pl.pallas_call(
        _combine_kernel,
        out_shape=jax.ShapeDtypeStruct((n, d), F32),
        grid=(nt,),
        in_specs=[row(d), row(LANES), _mod_spec(5, tpb, d), pl.BlockSpec(g_final.shape, lambda i: (0, 0))]
                 + [piece(j) for j in range(n_pieces)],
        out_specs=row(d),
        compiler_params=_cparams(("parallel",)),
        name="moe_combine",
    )(x, rt, mod1, g_final, *([ysel] * n_pieces))


def _pad_last(a, width):
    return jnp.pad(a, [(0, 0)] * (a.ndim - 1) + [(0, width - a.shape[-1])])


def kernel(x, c, positions, w_mod, b_mod, g_mix, g_ffn, w_a_down, g_q_lat, g_kv_lat, w_uq, w_ukv, w_oa,
           w_mod_kv, b_mod_kv, g_kv, w_kv_sb, w_q_sb, w_o_sb, w_ffn_gu, w_ffn_down, w_router, b_router,
           w_exp_gu, w_exp_down, g_final):
    bsz, seq, d = x.shape
    n = bsz * seq
    q_lora, kv_lora = g_q_lat.shape[1], g_kv_lat.shape[1]
    n_exp = w_router.shape[-1]
    d_ff = w_ffn_down.shape[1]
    tm = min(TOKEN_TILE, seq)
    t_mla = min(MLA_TILE, seq)
    t_sb = min(SB_TILE, seq)
    tf = next((t for t in (FFN_CHUNK, 512) if d_ff % t == 0), d_ff)

    mod0 = _modvec(c, w_mod[0], b_mod[0]).reshape(bsz, 6, 1, d)
    mod1 = _modvec(c, w_mod[1], b_mod[1]).reshape(bsz, 6, 1, d)
    mod_kv = _modvec(c, w_mod_kv, b_mod_kv).reshape(bsz, 2, 1, d)

    lat_w = q_lora + kv_lora + LANES
    wd = _pad_last(w_a_down[0], lat_w).astype(BF16)
    wq = w_uq[0].reshape(q_lora, MLA_HEADS, QK_NOPE + QK_ROPE)
    wqn = wq[:, :, :QK_NOPE].reshape(q_lora, MLA_HEADS * QK_NOPE).astype(BF16)
    wqr = _pad_last(wq[:, :, QK_NOPE:], LANES).reshape(q_lora, MLA_HEADS * LANES).astype(BF16)
    wkv = w_ukv[0].reshape(kv_lora, MLA_HEADS, QK_NOPE + V_DIM)
    wkn = wkv[:, :, :QK_NOPE].reshape(kv_lora, MLA_HEADS * QK_NOPE).astype(BF16)
    wv = wkv[:, :, QK_NOPE:].reshape(kv_lora, MLA_HEADS * V_DIM).astype(BF16)
    half = QK_ROPE // 2
    inv = ROPE_THETA ** (-jnp.arange(half, dtype=F32) / half)
    inv = jnp.tile(inv, ROPE_PACK).reshape(1, LANES)
    wr = _pad_last(w_router[0], LANES)
    wr_hi = wr.astype(BF16)
    wr_lo = (wr - wr_hi.astype(F32)).astype(BF16)
    br = _pad_last(b_router[0], LANES).reshape(1, LANES)

    xf = x.reshape(n, d)
    pos = positions.reshape(n // tm, ROPE_PACK, tm // ROPE_PACK).swapaxes(1, 2)
    pos = jnp.repeat(pos.reshape(n // ROPE_PACK, ROPE_PACK), QK_ROPE // 2, axis=1)
    row1 = lambda a: a.reshape(1, -1)

    q, k, v = _mla_proj(xf, pos, mod0, row1(g_mix[0]), wd, row1(g_q_lat[0]), row1(g_kv_lat[0]),
                        wqn, wqr, wkn, wv, inv, seq=seq, tm=tm)
    o = _mla_attn(q.reshape(bsz, seq, -1), k.reshape(bsz, seq, -1), v.reshape(bsz, seq, -1), t=t_mla)
    x2, hk, hm = _ffn(o.reshape(n, -1), w_oa[0].astype(BF16), xf, mod0, row1(g_ffn[0]),
                      w_ffn_gu[0].astype(BF16), w_ffn_down[0].astype(BF16),
                      row1(g_kv), mod_kv, row1(g_mix[1]), mod1, seq=seq, tm=tm, tf=tf)
    kv = _linear(hk, w_kv_sb.astype(BF16), tm=tm, tn=w_kv_sb.shape[1], name="kv_proj")
    q_scale = math.log2(math.e) / math.sqrt(SB_HEAD_DIM)
    qs = _linear(hm, (w_q_sb[0] * q_scale).astype(BF16), tm=tm, tn=1024, name="q_proj")
    o = _sb_attn(qs.reshape(bsz, seq, -1), kv.reshape(bsz, seq, -1), t=t_sb)
    x3, h, rt, counts = _out_proj_router(o.reshape(n, -1), w_o_sb[0].astype(BF16), x2, mod1, row1(g_ffn[1]),
                                         wr_hi, wr_lo, br, n_exp, seq=seq, tm=tm)
    n_tiles = TOP_K * n // tm + n_exp
    n_rows = n_tiles * tm
    dest, tile_expert, n_used, n_valid = _route_plan(rt, counts, n_exp, tm, n_tiles)
    picks = dest[None] + n_rows * jnp.arange(SC_ROW_PARTS, dtype=jnp.int32)[:, None, None]
    xg = _sc_scatter_rows(h.reshape(SC_ROW_PARTS * n, -1), [picks[:, k].reshape(-1) for k in range(TOP_K)],
                          SC_ROW_PARTS * n_rows)
    yg = _grouped_ffn(tile_expert, n_used, n_valid, xg.reshape(SC_ROW_PARTS, n_rows, -1),
                      w_exp_gu[0].astype(BF16), w_exp_down[0].astype(BF16), tm=tm, tf=tf)
    ysel = _sc_gather_rows(yg.reshape(SC_ROW_PARTS * n_rows, -1), picks.reshape(-1))
    out = _combine(x3, rt, mod1, row1(g_final), ysel, seq=seq, tm=tm)
    return out.reshape(bsz, seq, d)
```

```python
import functools
import math

import jax
import jax.numpy as jnp
from jax import lax
from jax.experimental import pallas as pl
from jax.experimental.pallas import tpu as pltpu
from jax.experimental.pallas import tpu_sc as plsc

F32 = jnp.float32
BF16 = jnp.bfloat16

EPS = 1e-6
MLA_HEADS = 8
QK_NOPE = 128
QK_ROPE = 64
V_DIM = 128
ROPE_THETA = 10000.0
SB_HEADS = 8
SB_HEAD_DIM = 128
TOP_K = 2

LANES = 128
QK_PAD = 256
ROPE_PACK = LANES // (QK_ROPE // 2)
VMEM_LIMIT = 48 * 1024 * 1024
VMEM_LIMIT_LARGE = 58 * 1024 * 1024
TOKEN_TILE = 512
MLA_TILE = 512
MLA_BLOCKS_PER_TRIP = 4
SB_TILE = 512
SB_BLOCKS_PER_TRIP = 2
FFN_CHUNK = 1792
SC_CORES = 2
SC_SUBCORES = 16
SC_WINDOW = 128
SC_ROW_PARTS = 2


def _cparams(sem, vmem_limit=VMEM_LIMIT):
    return pltpu.CompilerParams(dimension_semantics=sem, vmem_limit_bytes=vmem_limit)


def _rms(x, g):
    return x * lax.rsqrt(jnp.mean(x * x, axis=-1, keepdims=True) + EPS) * g


def _modulate(h, shift, scale):
    return h * (1.0 + scale) + shift


def _split_bf16(a):
    hi = a.astype(BF16)
    lo = (a - hi.astype(F32)).astype(BF16)
    return hi, lo


U32 = jnp.uint32
HIGH_HALF = 0xFFFF0000


def _pack_bf16_pairs(a):
    half = a.shape[1] // 2
    bits = lambda v: lax.bitcast_convert_type(v.astype(BF16).astype(F32), U32)
    return (bits(a[:, :half]) >> 16) | (bits(a[:, half:]) & U32(HIGH_HALF))


def _unpack_bf16_pairs(w):
    lo = lax.bitcast_convert_type(w << 16, F32)
    hi = lax.bitcast_convert_type(w & U32(HIGH_HALF), F32)
    return jnp.concatenate([lo, hi], axis=1)


def _dot(a, b):
    return jnp.dot(a, b, preferred_element_type=F32)


def _dot_nt(a, b):
    return lax.dot_general(a, b, (((1,), (1,)), ((), ())), preferred_element_type=F32)


def _modvec_kernel(c_ref, w_ref, b_ref, o_ref):
    c = c_ref[...]
    sc = c * (1.0 / (1.0 + jnp.exp(-c)))
    a_hi, a_lo = _split_bf16(sc)
    w_hi, w_lo = _split_bf16(w_ref[...])
    o_ref[...] = _dot(a_hi, w_hi) + _dot(a_lo, w_hi) + _dot(a_hi, w_lo) + b_ref[...]


def _modvec(c, w, b, tn=512):
    bsz, d = c.shape
    n = w.shape[1]
    return pl.pallas_call(
        _modvec_kernel,
        out_shape=jax.ShapeDtypeStruct((bsz, n), F32),
        grid=(n // tn,),
        in_specs=[pl.BlockSpec((bsz, d), lambda j: (0, 0)),
                  pl.BlockSpec((d, tn), lambda j: (0, j)),
                  pl.BlockSpec((1, tn), lambda j: (0, j))],
        out_specs=pl.BlockSpec((bsz, tn), lambda j: (0, j)),
        compiler_params=_cparams(("arbitrary",)),
        name="modvec",
    )(c, w, b.reshape(1, n))


def _mod_spec(chunk, tiles_per_batch, d):
    return pl.BlockSpec((None, None, 1, d), lambda i, *_: (i // tiles_per_batch, chunk, 0, 0))


def _mla_proj_kernel(x_ref, pos_ref, sh_ref, sc_ref, g_ref, wd_ref, gq_ref, gkv_ref,
                     wqn_ref, wqr_ref, wkn_ref, wv_ref, inv_ref,
                     q_ref, k_ref, v_ref, *, q_lora, kv_lora):
    x = x_ref[...]
    h = _modulate(_rms(x, g_ref[...]), sh_ref[...], sc_ref[...]).astype(BF16)
    lat = _dot(h, wd_ref[...])
    c_q = _rms(lat[:, :q_lora], gq_ref[...]).astype(BF16)
    c_kv = _rms(lat[:, q_lora:q_lora + kv_lora], gkv_ref[...]).astype(BF16)
    k_rot = lat[:, q_lora + kv_lora:]

    half = QK_ROPE // 2
    ang = pos_ref[...].astype(F32) * inv_ref[...]
    cos_p = jnp.cos(ang)
    sin_p = jnp.sin(ang)
    lane = lax.broadcasted_iota(jnp.int32, ang.shape, 1)
    cos, s_a, s_b = [], [], []
    for g in range(ROPE_PACK):
        shift = (LANES - g * half) % LANES
        cg = pltpu.roll(cos_p, shift, axis=1) if shift else cos_p
        sg = pltpu.roll(sin_p, shift, axis=1) if shift else sin_p
        cos.append(jnp.where(lane < half, cg, pltpu.roll(cg, half, axis=1)))
        s_a.append(jnp.where(lane < half, -sg, 0.0))
        s_b.append(jnp.where((lane >= half) & (lane < 2 * half), pltpu.roll(sg, half, axis=1), 0.0))
    cos, s_a, s_b = (jnp.concatenate(v, axis=0) for v in (cos, s_a, s_b))

    def rope(r):
        return (r * cos + pltpu.roll(r, LANES - half, axis=1) * s_a
                + pltpu.roll(r, half, axis=1) * s_b)

    k_rot = rope(k_rot).astype(BF16)
    q_nope = _dot(c_q, wqn_ref[...])
    q_rope = _dot(c_q, wqr_ref[...])
    k_nope = _dot(c_kv, wkn_ref[...])
    v_ref[...] = _dot(c_kv, wv_ref[...]).astype(BF16)
    for hd in range(MLA_HEADS):
        a, b = hd * LANES, (hd + 1) * LANES
        q_ref[:, hd * QK_PAD:hd * QK_PAD + LANES] = q_nope[:, a:b].astype(BF16)
        q_ref[:, hd * QK_PAD + LANES:(hd + 1) * QK_PAD] = rope(q_rope[:, a:b]).astype(BF16)
        k_ref[:, hd * QK_PAD:hd * QK_PAD + LANES] = k_nope[:, a:b].astype(BF16)
        k_ref[:, hd * QK_PAD + LANES:(hd + 1) * QK_PAD] = k_rot


def _mla_proj(x, pos, mod, g_mix, wd, gq, gkv, wqn, wqr, wkn, wv, inv, *, seq, tm):
    n, d = x.shape
    tpb = seq // tm
    q_lora, kv_lora = gq.shape[1], gkv.shape[1]
    hq = MLA_HEADS * QK_PAD
    hv = MLA_HEADS * V_DIM
    full = lambda a: pl.BlockSpec(a.shape, lambda i: (0,) * a.ndim)
    row = lambda w: pl.BlockSpec((tm, w), lambda i: (i, 0))
    return pl.pallas_call(
        functools.partial(_mla_proj_kernel, q_lora=q_lora, kv_lora=kv_lora),
        out_shape=(jax.ShapeDtypeStruct((n, hq), BF16), jax.ShapeDtypeStruct((n, hq), BF16),
                   jax.ShapeDtypeStruct((n, hv), BF16)),
        grid=(n // tm,),
        in_specs=[row(d), pl.BlockSpec((tm // ROPE_PACK, LANES), lambda i: (i, 0)), _mod_spec(0, tpb, d), _mod_spec(1, tpb, d), full(g_mix), full(wd),
                  full(gq), full(gkv), full(wqn), full(wqr), full(wkn), full(wv), full(inv)],
        out_specs=(row(hq), row(hq), row(hv)),
        compiler_params=_cparams(("parallel",)),
        name="mla_proj",
    )(x, pos, mod, mod, g_mix, wd, gq, gkv, wqn, wqr, wkn, wv, inv)


def _lane_chunks(a):
    return [a[:, c * LANES:(c + 1) * LANES] for c in range(a.shape[1] // LANES)]


def _for_blocks(n, step, group):
    def trip(p, carry):
        step(group * p, group)
        return carry

    lax.fori_loop(0, n // group, trip, 0)
    size = group // 2
    while size:
        @pl.when((n // size) % 2 == 1)
        def _(size=size):
            step(n // (2 * size) * (2 * size), size)
        size //= 2


def _mla_attn_kernel(q_ref, k_ref, v_ref, o_ref, s_ref, m_ref, acc_ref, *, t, scale, heads):
    i = pl.program_id(2)

    def lane_max(s):
        m = None
        for sc in _lane_chunks(s):
            m = sc if m is None else jnp.maximum(m, sc)
        return m

    def scores(h, j):
        start = pl.multiple_of(j * t, t)
        hs = slice(h * QK_PAD, (h + 1) * QK_PAD)
        return _dot_nt(q_ref[0, :, hs], k_ref[0, pl.ds(start, t), hs])

    r_minus_c = (lax.broadcasted_iota(jnp.int32, (t, t), 0) - lax.broadcasted_iota(jnp.int32, (t, t), 1))
    m_ref[...] = jnp.full(m_ref.shape, -jnp.inf, F32)

    def pass1(j, nb):
        for h in range(heads):
            m = m_ref[h]
            for jj in range(nb):
                s = jnp.where(r_minus_c >= (j + jj - i) * t, scores(h, j + jj), -jnp.inf)
                s_ref[h, j + jj] = s
                m = jnp.maximum(m, lane_max(s))
            m_ref[h] = m

    _for_blocks(i + 1, pass1, MLA_BLOCKS_PER_TRIP)
    for h in range(heads):
        m_ref[h] = jnp.broadcast_to(jnp.max(m_ref[h], axis=1, keepdims=True), (t, LANES))
    acc_ref[...] = jnp.zeros(acc_ref.shape, F32)
    cst = scale * math.log2(math.e)

    def pass2(j, nb):
        start = pl.multiple_of(j * t, t)
        ones = jnp.ones((nb * t, LANES), BF16)
        for h in range(heads):
            m = m_ref[h]
            p = jnp.concatenate([jnp.exp2((sc - m) * cst).astype(BF16)
                                 for jj in range(nb) for sc in _lane_chunks(s_ref[h, j + jj])], axis=1)
            v_ext = jnp.concatenate([v_ref[0, pl.ds(start, nb * t), h * V_DIM:(h + 1) * V_DIM], ones], axis=1)
            acc_ref[h] += _dot(p, v_ext)

    _for_blocks(i + 1, pass2, MLA_BLOCKS_PER_TRIP)
    for h in range(heads):
        acc = acc_ref[h]
        o_ref[0, :, h * V_DIM:(h + 1) * V_DIM] = (acc[:, :V_DIM] / acc[:, V_DIM:]).astype(o_ref.dtype)


def _mla_attn(q, k, v, *, t, heads=2):
    bsz, seq, _ = q.shape
    scale = 1.0 / math.sqrt(QK_NOPE + QK_ROPE)
    wq, wv = heads * QK_PAD, heads * V_DIM
    return pl.pallas_call(
        functools.partial(_mla_attn_kernel, t=t, scale=scale, heads=heads),
        out_shape=jax.ShapeDtypeStruct((bsz, seq, MLA_HEADS * V_DIM), BF16),
        grid=(bsz, MLA_HEADS // heads, seq // t),
        in_specs=[pl.BlockSpec((1, t, wq), lambda b, g, i: (b, i, g)),
                  pl.BlockSpec((1, seq, wq), lambda b, g, i: (b, 0, g)),
                  pl.BlockSpec((1, seq, wv), lambda b, g, i: (b, 0, g))],
        out_specs=pl.BlockSpec((1, t, wv), lambda b, g, i: (b, i, g)),
        scratch_shapes=[pltpu.VMEM((heads, seq // t, t, t), F32), pltpu.VMEM((heads, t, LANES), F32),
                        pltpu.VMEM((heads, t, V_DIM + LANES), F32)],
        compiler_params=_cparams(("parallel", "parallel", "arbitrary")),
        name="mla_attn",
    )(q, k, v)


def _sb_attn_kernel(q_ref, k_ref, v_ref, w_ref, o_ref, c_ref, acc_ref, *, t, heads):
    i = pl.program_id(2)
    nl = t // LANES
    d = SB_HEAD_DIM
    c_ref[...] = jnp.zeros(c_ref.shape, F32)
    acc_ref[...] = jnp.zeros(acc_ref.shape, F32)
    sub = t // 2 if t % (2 * LANES) == 0 else t
    r = lax.broadcasted_iota(jnp.int32, (sub, sub), 0)
    c = lax.broadcasted_iota(jnp.int32, (sub, sub), 1)
    strict = c < r

    def block(r0, nr, start, nk, masked):
        rows = slice(r0, r0 + nr)
        for h in range(heads):
            hs = slice(h * d, (h + 1) * d)
            zz = _dot_nt(q_ref[0, rows, hs], k_ref[0, pl.ds(start, nk), hs])
            zneg = -zz
            sp = jnp.log2(1.0 + jnp.exp2(jnp.minimum(zz, zneg)))
            log_1m = jnp.minimum(zneg, 0.0) - sp
            if masked:
                log_1m = jnp.where(strict, log_1m, 0.0)
            hi, lo = _split_bf16(log_1m)
            carry = c_ref[h, rows]
            a_chunks = [None] * (nk // LANES)
            for cc in reversed(range(nk // LANES)):
                cs = slice(cc * LANES, (cc + 1) * LANES)
                y = _dot(jnp.concatenate([hi[:, cs], lo[:, cs]], axis=1), w_ref[...])
                a = jnp.exp2(zz[:, cs] + y[:, :LANES] + carry)
                carry = carry + y[:, LANES:]
                if masked:
                    a = jnp.where(strict[:, cs], a, 0.0)
                a_chunks[cc] = a.astype(BF16)
            c_ref[h, rows] = carry
            acc_ref[h, rows] += _dot(jnp.concatenate(a_chunks, axis=1), v_ref[0, pl.ds(start, nk), hs])

    for band in range(t // sub):
        for piece in reversed(range(band + 1)):
            block(band * sub, sub, pl.multiple_of(i * t + piece * sub, sub), sub, piece == band)

    def left_blocks(n, nb):
        for jj in range(nb):
            block(0, t, pl.multiple_of((i - 1 - n - jj) * t, t), t, False)

    _for_blocks(i, left_blocks, SB_BLOCKS_PER_TRIP)
    for h in range(heads):
        o_ref[0, :, h * d:(h + 1) * d] = acc_ref[h].astype(o_ref.dtype)


def _sb_attn(q, kv, *, t, heads=4):
    bsz, seq, _ = q.shape
    w = heads * SB_HEAD_DIM
    groups = SB_HEADS // heads
    tri = (jnp.arange(LANES)[:, None] >= jnp.arange(LANES)[None, :]).astype(BF16)
    half = jnp.concatenate([tri, jnp.ones((LANES, LANES), BF16)], axis=1)
    w_sum = jnp.concatenate([half, half], axis=0)
    return pl.pallas_call(
        functools.partial(_sb_attn_kernel, t=t, heads=heads),
        out_shape=jax.ShapeDtypeStruct((bsz, seq, SB_HEADS * SB_HEAD_DIM), BF16),
        grid=(bsz, groups, seq // t),
        in_specs=[pl.BlockSpec((1, t, w), lambda b, g, i: (b, i, g)),
                  pl.BlockSpec((1, seq, w), lambda b, g, i: (b, 0, g)),
                  pl.BlockSpec((1, seq, w), lambda b, g, i: (b, 0, groups + g)),
                  pl.BlockSpec(w_sum.shape, lambda b, g, i: (0, 0))],
        out_specs=pl.BlockSpec((1, t, w), lambda b, g, i: (b, i, g)),
        scratch_shapes=[pltpu.VMEM((heads, t, LANES), F32), pltpu.VMEM((heads, t, SB_HEAD_DIM), F32)],
        compiler_params=_cparams(("parallel", "parallel", "arbitrary")),
        name="sb_attn",
    )(q, kv, kv, w_sum)


def _out_proj_kernel(o_ref, w_ref, x_ref, gt_ref, g_ref, sh_ref, sc_ref, wr_hi_ref, wr_lo_ref, br_ref, tri_ref,
                     xo_ref, h_ref, rt_ref, cnt_ref, *, n_experts):
    x = x_ref[...] + gt_ref[...] * _dot(o_ref[...], w_ref[...])
    xo_ref[...] = x
    h = _modulate(_rms(x, g_ref[...]), sh_ref[...], sc_ref[...])
    h_hi = h.astype(BF16)
    words = _pack_bf16_pairs(h)
    for part in range(SC_ROW_PARTS):
        h_ref[part] = words[:, part * h_ref.shape[2]:(part + 1) * h_ref.shape[2]]
    h_lo = (h - h_hi.astype(F32)).astype(BF16)
    logits = (_dot(h_hi, wr_hi_ref[...]) + _dot(h_lo, wr_hi_ref[...]) + _dot(h_hi, wr_lo_ref[...])
              + br_ref[...])
    lane = lax.broadcasted_iota(jnp.int32, logits.shape, 1).astype(F32)
    lg = jnp.where(lane < n_experts, logits, -jnp.inf)
    m1 = jnp.max(lg, axis=1, keepdims=True)
    i1 = jnp.min(jnp.where(lg == m1, lane, float(LANES)), axis=1, keepdims=True)
    lg2 = jnp.where(lane == i1, -jnp.inf, lg)
    m2 = jnp.max(lg2, axis=1, keepdims=True)
    i2 = jnp.min(jnp.where(lg2 == m2, lane, float(LANES)), axis=1, keepdims=True)
    e2 = jnp.exp(m2 - m1)
    den = 1.0 + e2
    sel = jnp.where((lane == i1) | (lane == i2), 1.0, 0.0)
    prefix = _dot(tri_ref[...], sel.astype(BF16))
    r1 = jnp.sum(jnp.where(lane == i1, prefix, 0.0), axis=1, keepdims=True)
    r2 = jnp.sum(jnp.where(lane == i2, prefix, 0.0), axis=1, keepdims=True)
    cnt_ref[...] = jnp.sum(sel, axis=0, keepdims=True)
    rt = jnp.zeros_like(logits)
    for k, val in enumerate((i1, i2, 1.0 / den, e2 / den, r1, r2)):
        rt = jnp.where(lane == k, val, rt)
    rt_ref[...] = rt


def _out_proj_router(o, w, x, mod, g, wr_hi, wr_lo, br, n_experts, *, seq, tm):
    n, d = x.shape
    tpb = seq // tm
    dpp = d // 2 // SC_ROW_PARTS
    tri = (jnp.arange(tm)[:, None] > jnp.arange(tm)[None, :]).astype(BF16)
    full = lambda a: pl.BlockSpec(a.shape, lambda i: (0,) * a.ndim)
    row = lambda wd: pl.BlockSpec((tm, wd), lambda i: (i, 0))
    return pl.pallas_call(
        functools.partial(_out_proj_kernel, n_experts=n_experts),
        out_shape=(jax.ShapeDtypeStruct((n, d), F32), jax.ShapeDtypeStruct((SC_ROW_PARTS, n, dpp), U32),
                   jax.ShapeDtypeStruct((n, LANES), F32), jax.ShapeDtypeStruct((n // tm, 1, LANES), F32)),
        grid=(n // tm,),
        in_specs=[row(o.shape[1]), full(w), row(d), _mod_spec(2, tpb, d), full(g),
                  _mod_spec(3, tpb, d), _mod_spec(4, tpb, d), full(wr_hi), full(wr_lo), full(br), full(tri)],
        out_specs=(row(d), pl.BlockSpec((SC_ROW_PARTS, tm, dpp), lambda i: (0, i, 0)), row(LANES),
                   pl.BlockSpec((None, 1, LANES), lambda i: (i, 0, 0))),
        compiler_params=_cparams(("parallel",)),
        name="out_proj_router",
    )(o, w, x, mod, g, mod, mod, wr_hi, wr_lo, br, tri)


def _silu(g):
    return g * (1.0 / (1.0 + jnp.exp(-g)))


def _ffn_kernel(o_ref, wo_ref, x_ref, gt1_ref, gf_ref, sh2_ref, sc2_ref, wg_ref, wu_ref, wd_ref, gt2_ref,
                gkv_ref, shk_ref, sck_ref, gm_ref, shm_ref, scm_ref,
                xo_ref, hk_ref, hm_ref, x1_ref, h_ref, acc_ref):
    f = pl.program_id(1)

    @pl.when(f == 0)
    def _():
        x1 = x_ref[...] + gt1_ref[...] * _dot(o_ref[...], wo_ref[...])
        x1_ref[...] = x1
        h_ref[...] = _modulate(_rms(x1, gf_ref[...]), sh2_ref[...], sc2_ref[...]).astype(BF16)
        acc_ref[...] = jnp.zeros(acc_ref.shape, F32)

    h = h_ref[...]
    a = _silu(_dot(h, wg_ref[...])) * _dot(h, wu_ref[...])
    acc_ref[...] += _dot(a.astype(BF16), wd_ref[...])

    @pl.when(f == pl.num_programs(1) - 1)
    def _():
        x = x1_ref[...] + gt2_ref[...] * acc_ref[...]
        xo_ref[...] = x
        hk_ref[...] = _modulate(_rms(x, gkv_ref[...]), shk_ref[...], sck_ref[...]).astype(BF16)
        hm_ref[...] = _modulate(_rms(x, gm_ref[...]), shm_ref[...], scm_ref[...]).astype(BF16)


def _ffn(o, w_o, x, mod0, g_ffn, w_gu, w_down, g_kv, mod_kv, g_mix1, mod1, *, seq, tm, tf):
    n, d = x.shape
    d_ff = w_down.shape[0]
    nf = d_ff // tf
    tpb = seq // tm
    full = lambda a: pl.BlockSpec(a.shape, lambda i, f: (0,) * a.ndim)
    row = lambda w=d: pl.BlockSpec((tm, w), lambda i, f: (i, 0))
    return pl.pallas_call(
        _ffn_kernel,
        out_shape=(jax.ShapeDtypeStruct((n, d), F32), jax.ShapeDtypeStruct((n, d), BF16),
                   jax.ShapeDtypeStruct((n, d), BF16)),
        grid=(n // tm, nf),
        in_specs=[row(o.shape[1]), full(w_o), row(), _mod_spec(2, tpb, d), full(g_ffn),
                  _mod_spec(3, tpb, d), _mod_spec(4, tpb, d),
                  pl.BlockSpec((d, tf), lambda i, f: (0, f)),
                  pl.BlockSpec((d, tf), lambda i, f: (0, nf + f)),
                  pl.BlockSpec((tf, d), lambda i, f: (f, 0)),
                  _mod_spec(5, tpb, d),
                  full(g_kv), _mod_spec(0, tpb, d), _mod_spec(1, tpb, d),
                  full(g_mix1), _mod_spec(0, tpb, d), _mod_spec(1, tpb, d)],
        out_specs=(row(), row(), row()),
        scratch_shapes=[pltpu.VMEM((tm, d), F32), pltpu.VMEM((tm, d), BF16), pltpu.VMEM((tm, d), F32)],
        compiler_params=_cparams(("parallel", "arbitrary"), vmem_limit=VMEM_LIMIT_LARGE),
        name="ffn_dense",
    )(o, w_o, x, mod0, g_ffn, mod0, mod0, w_gu, w_gu, w_down, mod0,
      g_kv, mod_kv, mod_kv, g_mix1, mod1, mod1)


def _linear_kernel(x_ref, w_ref, o_ref):
    o_ref[...] = _dot(x_ref[...], w_ref[...]).astype(o_ref.dtype)


def _linear(x, w, *, tm, tn, name):
    n, k = x.shape
    m = w.shape[1]
    return pl.pallas_call(
        _linear_kernel,
        out_shape=jax.ShapeDtypeStruct((n, m), BF16),
        grid=(m // tn, n // tm),
        in_specs=[pl.BlockSpec((tm, k), lambda j, i: (i, 0)), pl.BlockSpec((k, tn), lambda j, i: (0, j))],
        out_specs=pl.BlockSpec((tm, tn), lambda j, i: (i, j)),
        compiler_params=_cparams(("parallel", "parallel")),
        name=name,
    )(x, w)


def _route_plan(rt, counts, n_exp, tm, n_tiles):
    cnt = counts[:, 0, :n_exp].astype(jnp.int32)
    sizes = jnp.sum(cnt, axis=0)
    padded = (sizes + tm - 1) // tm * tm
    ends = jnp.cumsum(padded)
    tile_base = (ends - padded)[None, :] + jnp.cumsum(cnt, axis=0) - cnt
    base = jnp.repeat(tile_base, tm, axis=0)
    experts = jnp.arange(n_exp, dtype=jnp.int32)[None, :]
    dest = []
    for k in range(TOP_K):
        e_k = rt[:, k].astype(jnp.int32)
        r_k = rt[:, 2 * TOP_K + k].astype(jnp.int32)
        dest.append(jnp.sum(jnp.where(e_k[:, None] == experts, base, 0), axis=1) + r_k)
    dest = jnp.stack(dest, axis=0)
    tile_start = jnp.arange(n_tiles, dtype=jnp.int32) * tm
    tile_expert = jnp.minimum(jnp.sum(tile_start[:, None] >= ends[None, :], axis=1), n_exp - 1)
    n_used = (ends[-1] // tm).reshape(1)
    n_valid = jnp.clip((ends - padded + sizes)[tile_expert] - tile_start, 0, tm)
    return dest, tile_expert.astype(jnp.int32), n_used.astype(jnp.int32), n_valid.astype(jnp.int32)


def _sc_mesh():
    return plsc.VectorSubcoreMesh(core_axis_name="core", subcore_axis_name="subcore",
                                  num_cores=SC_CORES, num_subcores=SC_SUBCORES)


def _sc_scatter_rows(x, idx_list, n_rows):
    m, d = x.shape

    @functools.partial(pl.kernel, out_type=jax.ShapeDtypeStruct((n_rows, d), x.dtype), mesh=_sc_mesh(),
                       scratch_types=[], name="sc_scatter_rows")
    def scatter(x_hbm, *refs):
        i_hbms, o_hbm = refs[:-1], refs[-1]

        def body(x_vmem, *i_vmems):
            for i_vmem in i_vmems:
                pltpu.sync_copy(x_vmem, o_hbm.at[i_vmem.at[0]])

        pltpu.emit_pipeline(
            body,
            grid=(m // SC_WINDOW,),
            in_specs=[pl.BlockSpec((SC_WINDOW, d), index_map=lambda i: (i, 0))]
                     + [pl.BlockSpec((1, SC_WINDOW), index_map=lambda i: (0, i))] * len(idx_list),
            out_specs=[],
            core_axis_name="subcore",
            dimension_semantics=(pltpu.PARALLEL,),
        )(x_hbm, *i_hbms)

    return scatter(x, *[idx.reshape(1, m) for idx in idx_list])


def _grouped_ffn_kernel(te_ref, nu_ref, nv_ref, x_ref, wg_ref, wu_ref, wd_ref, y_ref, hb_ref, acc_ref):
    del te_ref
    i = pl.program_id(0)
    f = pl.program_id(1)
    used = i < nu_ref[0]
    last_f = f == pl.num_programs(1) - 1

    @pl.when(used & (f == 0))
    def _():
        words = jnp.concatenate([x_ref[p] for p in range(SC_ROW_PARTS)], axis=1)
        row = lax.broadcasted_iota(jnp.int32, words.shape, 0)
        words = jnp.where(row < nv_ref[i], words, U32(0))
        hb_ref[...] = _unpack_bf16_pairs(words).astype(BF16)
        acc_ref[...] = jnp.zeros(acc_ref.shape, F32)

    @pl.when(used)
    def _():
        h = hb_ref[...]
        a = _silu(_dot(h, wg_ref[0])) * _dot(h, wu_ref[0])
        acc_ref[...] += _dot(a.astype(BF16), wd_ref[0])

    @pl.when(used & last_f)
    def _():
        y = _pack_bf16_pairs(acc_ref[...])
        for part in range(SC_ROW_PARTS):
            y_ref[part] = y[:, part * y_ref.shape[2]:(part + 1) * y_ref.shape[2]]

    @pl.when(jnp.logical_not(used) & last_f)
    def _():
        y_ref[...] = jnp.zeros(y_ref.shape, y_ref.dtype)


def _grouped_ffn(tile_expert, n_used, n_valid, xg, w_gu, w_down, *, tm, tf):
    parts, n_rows, dpp = xg.shape
    n_exp, d_ff, d = w_down.shape
    nf = d_ff // tf

    def wspec(shape, index):
        def index_map(i, f, te, nu, nv):
            return index(te[i], jnp.where(i < nu[0], f, nf - 1))
        return pl.BlockSpec(shape, index_map)

    rows = pl.BlockSpec((parts, tm, dpp), lambda i, f, te, nu, nv: (0, i, 0))
    return pl.pallas_call(
        _grouped_ffn_kernel,
        out_shape=jax.ShapeDtypeStruct(xg.shape, xg.dtype),
        grid_spec=pltpu.PrefetchScalarGridSpec(
            num_scalar_prefetch=3,
            grid=(n_rows // tm, nf),
            in_specs=[rows,
                      wspec((1, d, tf), lambda e, f: (e, 0, f)),
                      wspec((1, d, tf), lambda e, f: (e, 0, nf + f)),
                      wspec((1, tf, d), lambda e, f: (e, f, 0))],
            out_specs=rows,
            scratch_shapes=[pltpu.VMEM((tm, d), BF16), pltpu.VMEM((tm, d), F32)]),
        compiler_params=_cparams(("arbitrary", "arbitrary")),
        name="moe_grouped_ffn",
    )(tile_expert, n_used, n_valid, xg, w_gu, w_gu, w_down)


def _sc_gather_rows(table, idx):
    n_idx = idx.shape[0]
    d = table.shape[1]

    @functools.partial(pl.kernel, out_type=jax.ShapeDtypeStruct((n_idx, d), table.dtype), mesh=_sc_mesh(),
                       name="sc_gather_rows")
    def gather(t_hbm, i_hbm, o_hbm):
        def body(i_vmem, o_vmem):
            pltpu.sync_copy(t_hbm.at[i_vmem.at[0]], o_vmem)

        pltpu.emit_pipeline(
            body,
            grid=(n_idx // SC_WINDOW,),
            in_specs=[pl.BlockSpec((1, SC_WINDOW), index_map=lambda i: (0, i))],
            out_specs=[pl.BlockSpec((SC_WINDOW, d), index_map=lambda i: (i, 0))],
            core_axis_name="subcore",
            dimension_semantics=(pltpu.PARALLEL,),
        )(i_hbm, o_hbm)

    return gather(table, idx.reshape(1, n_idx))


def _combine_kernel(x_ref, rt_ref, gt_ref, gf_ref, *refs):
    y_refs, o_ref = refs[:-1], refs[-1]
    rt = rt_ref[...]
    tot = None
    for k in range(TOP_K):
        words = jnp.concatenate([y_refs[p * TOP_K + k][...] for p in range(SC_ROW_PARTS)], axis=1)
        term = rt[:, TOP_K + k:TOP_K + k + 1] * _unpack_bf16_pairs(words)
        tot = term if tot is None else tot + term
    x = x_ref[...] + gt_ref[...] * tot
    o_ref[...] = _rms(x, gf_ref[...])


def _combine(x, rt, mod1, g_final, ysel, *, seq, tm):
    n, d = x.shape
    tpb = seq // tm
    nt = n // tm
    row = lambda w: pl.BlockSpec((tm, w), lambda i: (i, 0))
    piece = lambda j: pl.BlockSpec((tm, ysel.shape[1]), lambda i: (j * nt + i, 0))
    n_pieces = SC_ROW_PARTS * TOP_K
    return pl.pallas_call(
        _combine_kernel,
        out_shape=jax.ShapeDtypeStruct((n, d), F32),
        grid=(nt,),
        in_specs=[row(d), row(LANES), _mod_spec(5, tpb, d), pl.BlockSpec(g_final.shape, lambda i: (0, 0))]
                 + [piece(j) for j in range(n_pieces)],
        out_specs=row(d),
        compiler_params=_cparams(("parallel",)),
        name="moe_combine",
    )(x, rt, mod1, g_final, *([ysel] * n_pieces))


def _pad_last(a, width):
    return jnp.pad(a, [(0, 0)] * (a.ndim - 1) + [(0, width - a.shape[-1])])


def kernel(x, c, positions, w_mod, b_mod, g_mix, g_ffn, w_a_down, g_q_lat, g_kv_lat, w_uq, w_ukv, w_oa,
           w_mod_kv, b_mod_kv, g_kv, w_kv_sb, w_q_sb, w_o_sb, w_ffn_gu, w_ffn_down, w_router, b_router,
           w_exp_gu, w_exp_down, g_final):
    bsz, seq, d = x.shape
    n = bsz * seq
    q_lora, kv_lora = g_q_lat.shape[1], g_kv_lat.shape[1]
    n_exp = w_router.shape[-1]
    d_ff = w_ffn_down.shape[1]
    tm = min(TOKEN_TILE, seq)
    t_mla = min(MLA_TILE, seq)
    t_sb = min(SB_TILE, seq)
    tf = next((t for t in (FFN_CHUNK, 512) if d_ff % t == 0), d_ff)

    mod0 = _modvec(c, w_mod[0], b_mod[0]).reshape(bsz, 6, 1, d)
    mod1 = _modvec(c, w_mod[1], b_mod[1]).reshape(bsz, 6, 1, d)
    mod_kv = _modvec(c, w_mod_kv, b_mod_kv).reshape(bsz, 2, 1, d)

    lat_w = q_lora + kv_lora + LANES
    wd = _pad_last(w_a_down[0], lat_w).astype(BF16)
    wq = w_uq[0].reshape(q_lora, MLA_HEADS, QK_NOPE + QK_ROPE)
    wqn = wq[:, :, :QK_NOPE].reshape(q_lora, MLA_HEADS * QK_NOPE).astype(BF16)
    wqr = _pad_last(wq[:, :, QK_NOPE:], LANES).reshape(q_lora, MLA_HEADS * LANES).astype(BF16)
    wkv = w_ukv[0].reshape(kv_lora, MLA_HEADS, QK_NOPE + V_DIM)
    wkn = wkv[:, :, :QK_NOPE].reshape(kv_lora, MLA_HEADS * QK_NOPE).astype(BF16)
    wv = wkv[:, :, QK_NOPE:].reshape(kv_lora, MLA_HEADS * V_DIM).astype(BF16)
    half = QK_ROPE // 2
    inv = ROPE_THETA ** (-jnp.arange(half, dtype=F32) / half)
    inv = jnp.tile(inv, ROPE_PACK).reshape(1, LANES)
    wr = _pad_last(w_router[0], LANES)
    wr_hi = wr.astype(BF16)
    wr_lo = (wr - wr_hi.astype(F32)).astype(BF16)
    br = _pad_last(b_router[0], LANES).reshape(1, LANES)

    xf = x.reshape(n, d)
    pos = positions.reshape(n // tm, ROPE_PACK, tm // ROPE_PACK).swapaxes(1, 2)
    pos = jnp.repeat(pos.reshape(n // ROPE_PACK, ROPE_PACK), QK_ROPE // 2, axis=1)
    row1 = lambda a: a.reshape(1, -1)

    q, k, v = _mla_proj(xf, pos, mod0, row1(g_mix[0]), wd, row1(g_q_lat[0]), row1(g_kv_lat[0]),
                        wqn, wqr, wkn, wv, inv, seq=seq, tm=tm)
    o = _mla_attn(q.reshape(bsz, seq, -1), k.reshape(bsz, seq, -1), v.reshape(bsz, seq, -1), t=t_mla)
    x2, hk, hm = _ffn(o.reshape(n, -1), w_oa[0].astype(BF16), xf, mod0, row1(g_ffn[0]),
                      w_ffn_gu[0].astype(BF16), w_ffn_down[0].astype(BF16),
                      row1(g_kv), mod_kv, row1(g_mix[1]), mod1, seq=seq, tm=tm, tf=tf)
    kv = _linear(hk, w_kv_sb.astype(BF16), tm=tm, tn=w_kv_sb.shape[1], name="kv_proj")
    q_scale = math.log2(math.e) / math.sqrt(SB_HEAD_DIM)
    qs = _linear(hm, (w_q_sb[0] * q_scale).astype(BF16), tm=tm, tn=1024, name="q_proj")
    o = _sb_attn(qs.reshape(bsz, seq, -1), kv.reshape(bsz, seq, -1), t=t_sb)
    x3, h, rt, counts = _out_proj_router(o.reshape(n, -1), w_o_sb[0].astype(BF16), x2, mod1, row1(g_ffn[1]),
                                         wr_hi, wr_lo, br, n_exp, seq=seq, tm=tm)
    n_tiles = TOP_K * n // tm + n_exp
    n_rows = n_tiles * tm
    dest, tile_expert, n_used, n_valid = _route_plan(rt, counts, n_exp, tm, n_tiles)
    picks = dest[None] + n_rows * jnp.arange(SC_ROW_PARTS, dtype=jnp.int32)[:, None, None]
    xg = _sc_scatter_rows(h.reshape(SC_ROW_PARTS * n, -1), [picks[:, k].reshape(-1) for k in range(TOP_K)],
                          SC_ROW_PARTS * n_rows)
    yg = _grouped_ffn(tile_expert, n_used, n_valid, xg.reshape(SC_ROW_PARTS, n_rows, -1),
                      w_exp_gu[0].astype(BF16), w_exp_down[0].astype(BF16), tm=tm, tf=tf)
    ysel = _sc_gather_rows(yg.reshape(SC_ROW_PARTS * n_rows, -1), picks.reshape(-1))
    out = _combine(x3, rt, mod1, row1(g_final), ysel, seq=seq, tm=tm)
    return out.reshape(bsz, seq, d)
```

```python
import functools
import math

import jax
import jax.numpy as jnp
from jax import lax
from jax.experimental import pallas as pl
from jax.experimental.pallas import tpu as pltpu
from jax.experimental.pallas import tpu_sc as plsc

F32 = jnp.float32
BF16 = jnp.bfloat16

EPS = 1e-6
MLA_HEADS = 8
QK_NOPE = 128
QK_ROPE = 64
V_DIM = 128
ROPE_THETA = 10000.0
SB_HEADS = 8
SB_HEAD_DIM = 128
TOP_K = 2

LANES = 128
QK_PAD = 256
ROPE_PACK = LANES // (QK_ROPE // 2)
VMEM_LIMIT = 48 * 1024 * 1024
VMEM_LIMIT_LARGE = 58 * 1024 * 1024
TOKEN_TILE = 512
MLA_TILE = 512
MLA_BLOCKS_PER_TRIP = 4
SB_TILE = 512
SB_BLOCKS_PER_TRIP = 2
SB_ZERO_LOG2 = -140.0
FFN_CHUNK = 1792
SC_CORES = 2
SC_SUBCORES = 16
SC_WINDOW = 128
SC_ROW_PARTS = 2


def _cparams(sem, vmem_limit=VMEM_LIMIT):
    return pltpu.CompilerParams(dimension_semantics=sem, vmem_limit_bytes=vmem_limit)


def _rms(x, g):
    return x * lax.rsqrt(jnp.mean(x * x, axis=-1, keepdims=True) + EPS) * g


def _modulate(h, shift, scale):
    return h * (1.0 + scale) + shift


def _split_bf16(a):
    hi = a.astype(BF16)
    lo = (a - hi.astype(F32)).astype(BF16)
    return hi, lo


U32 = jnp.uint32
HIGH_HALF = 0xFFFF0000


def _pack_bf16_pairs(a):
    half = a.shape[1] // 2
    bits = lambda v: lax.bitcast_convert_type(v.astype(BF16).astype(F32), U32)
    return (bits(a[:, :half]) >> 16) | (bits(a[:, half:]) & U32(HIGH_HALF))


def _unpack_bf16_pairs(w):
    lo = lax.bitcast_convert_type(w << 16, F32)
    hi = lax.bitcast_convert_type(w & U32(HIGH_HALF), F32)
    return jnp.concatenate([lo, hi], axis=1)


def _dot(a, b):
    return jnp.dot(a, b, preferred_element_type=F32)


def _dot_nt(a, b):
    return lax.dot_general(a, b, (((1,), (1,)), ((), ())), preferred_element_type=F32)


def _modvec_kernel(c_ref, w_ref, b_ref, o_ref):
    c = c_ref[...]
    sc = c * (1.0 / (1.0 + jnp.exp(-c)))
    a_hi, a_lo = _split_bf16(sc)
    w_hi, w_lo = _split_bf16(w_ref[...])
    o_ref[...] = _dot(a_hi, w_hi) + _dot(a_lo, w_hi) + _dot(a_hi, w_lo) + b_ref[...]


def _modvec(c, w, b, tn=512):
    bsz, d = c.shape
    n = w.shape[1]
    return pl.pallas_call(
        _modvec_kernel,
        out_shape=jax.ShapeDtypeStruct((bsz, n), F32),
        grid=(n // tn,),
        in_specs=[pl.BlockSpec((bsz, d), lambda j: (0, 0)),
                  pl.BlockSpec((d, tn), lambda j: (0, j)),
                  pl.BlockSpec((1, tn), lambda j: (0, j))],
        out_specs=pl.BlockSpec((bsz, tn), lambda j: (0, j)),
        compiler_params=_cparams(("arbitrary",)),
        name="modvec",
    )(c, w, b.reshape(1, n))


def _mod_spec(chunk, tiles_per_batch, d):
    return pl.BlockSpec((None, None, 1, d), lambda i, *_: (i // tiles_per_batch, chunk, 0, 0))


def _mla_proj_kernel(x_ref, pos_ref, sh_ref, sc_ref, g_ref, wd_ref, gq_ref, gkv_ref,
                     wqn_ref, wqr_ref, wkn_ref, wv_ref, inv_ref,
                     q_ref, k_ref, v_ref, *, q_lora, kv_lora):
    x = x_ref[...]
    h = _modulate(_rms(x, g_ref[...]), sh_ref[...], sc_ref[...]).astype(BF16)
    lat = _dot(h, wd_ref[...])
    c_q = _rms(lat[:, :q_lora], gq_ref[...]).astype(BF16)
    c_kv = _rms(lat[:, q_lora:q_lora + kv_lora], gkv_ref[...]).astype(BF16)
    k_rot = lat[:, q_lora + kv_lora:]

    half = QK_ROPE // 2
    ang = pos_ref[...].astype(F32) * inv_ref[...]
    cos_p = jnp.cos(ang)
    sin_p = jnp.sin(ang)
    lane = lax.broadcasted_iota(jnp.int32, ang.shape, 1)
    cos, s_a, s_b = [], [], []
    for g in range(ROPE_PACK):
        shift = (LANES - g * half) % LANES
        cg = pltpu.roll(cos_p, shift, axis=1) if shift else cos_p
        sg = pltpu.roll(sin_p, shift, axis=1) if shift else sin_p
        cos.append(jnp.where(lane < half, cg, pltpu.roll(cg, half, axis=1)))
        s_a.append(jnp.where(lane < half, -sg, 0.0))
        s_b.append(jnp.where((lane >= half) & (lane < 2 * half), pltpu.roll(sg, half, axis=1), 0.0))
    cos, s_a, s_b = (jnp.concatenate(v, axis=0) for v in (cos, s_a, s_b))

    def rope(r):
        return (r * cos + pltpu.roll(r, LANES - half, axis=1) * s_a
                + pltpu.roll(r, half, axis=1) * s_b)

    k_rot = rope(k_rot).astype(BF16)
    q_nope = _dot(c_q, wqn_ref[...])
    q_rope = _dot(c_q, wqr_ref[...])
    k_nope = _dot(c_kv, wkn_ref[...])
    v_ref[...] = _dot(c_kv, wv_ref[...]).astype(BF16)
    for hd in range(MLA_HEADS):
        a, b = hd * LANES, (hd + 1) * LANES
        q_ref[:, hd * QK_PAD:hd * QK_PAD + LANES] = q_nope[:, a:b].astype(BF16)
        q_ref[:, hd * QK_PAD + LANES:(hd + 1) * QK_PAD] = rope(q_rope[:, a:b]).astype(BF16)
        k_ref[:, hd * QK_PAD:hd * QK_PAD + LANES] = k_nope[:, a:b].astype(BF16)
        k_ref[:, hd * QK_PAD + LANES:(hd + 1) * QK_PAD] = k_rot


def _mla_proj(x, pos, mod, g_mix, wd, gq, gkv, wqn, wqr, wkn, wv, inv, *, seq, tm):
    n, d = x.shape
    tpb = seq // tm
    q_lora, kv_lora = gq.shape[1], gkv.shape[1]
    hq = MLA_HEADS * QK_PAD
    hv = MLA_HEADS * V_DIM
    full = lambda a: pl.BlockSpec(a.shape, lambda i: (0,) * a.ndim)
    row = lambda w: pl.BlockSpec((tm, w), lambda i: (i, 0))
    return pl.pallas_call(
        functools.partial(_mla_proj_kernel, q_lora=q_lora, kv_lora=kv_lora),
        out_shape=(jax.ShapeDtypeStruct((n, hq), BF16), jax.ShapeDtypeStruct((n, hq), BF16),
                   jax.ShapeDtypeStruct((n, hv), BF16)),
        grid=(n // tm,),
        in_specs=[row(d), pl.BlockSpec((tm // ROPE_PACK, LANES), lambda i: (i, 0)), _mod_spec(0, tpb, d), _mod_spec(1, tpb, d), full(g_mix), full(wd),
                  full(gq), full(gkv), full(wqn), full(wqr), full(wkn), full(wv), full(inv)],
        out_specs=(row(hq), row(hq), row(hv)),
        compiler_params=_cparams(("parallel",)),
        name="mla_proj",
    )(x, pos, mod, mod, g_mix, wd, gq, gkv, wqn, wqr, wkn, wv, inv)


def _lane_chunks(a):
    return [a[:, c * LANES:(c + 1) * LANES] for c in range(a.shape[1] // LANES)]


def _for_blocks(n, step, group):
    def trip(p, carry):
        step(group * p, group)
        return carry

    lax.fori_loop(0, n // group, trip, 0)
    size = group // 2
    while size:
        @pl.when((n // size) % 2 == 1)
        def _(size=size):
            step(n // (2 * size) * (2 * size), size)
        size //= 2


def _mla_attn_kernel(q_ref, k_ref, v_ref, o_ref, s_ref, m_ref, acc_ref, *, t, scale, heads):
    i = pl.program_id(2)

    def lane_max(s):
        m = None
        for sc in _lane_chunks(s):
            m = sc if m is None else jnp.maximum(m, sc)
        return m

    def scores(h, j):
        start = pl.multiple_of(j * t, t)
        hs = slice(h * QK_PAD, (h + 1) * QK_PAD)
        return _dot_nt(q_ref[0, :, hs], k_ref[0, pl.ds(start, t), hs])

    r_minus_c = (lax.broadcasted_iota(jnp.int32, (t, t), 0) - lax.broadcasted_iota(jnp.int32, (t, t), 1))
    m_ref[...] = jnp.full(m_ref.shape, -jnp.inf, F32)

    def pass1(j, nb):
        for h in range(heads):
            m = m_ref[h]
            for jj in range(nb):
                s = jnp.where(r_minus_c >= (j + jj - i) * t, scores(h, j + jj), -jnp.inf)
                s_ref[h, j + jj] = s
                m = jnp.maximum(m, lane_max(s))
            m_ref[h] = m

    _for_blocks(i + 1, pass1, MLA_BLOCKS_PER_TRIP)
    for h in range(heads):
        m_ref[h] = jnp.broadcast_to(jnp.max(m_ref[h], axis=1, keepdims=True), (t, LANES))
    acc_ref[...] = jnp.zeros(acc_ref.shape, F32)
    cst = scale * math.log2(math.e)

    def pass2(j, nb):
        start = pl.multiple_of(j * t, t)
        ones = jnp.ones((nb * t, LANES), BF16)
        for h in range(heads):
            m = m_ref[h]
            p = jnp.concatenate([jnp.exp2((sc - m) * cst).astype(BF16)
                                 for jj in range(nb) for sc in _lane_chunks(s_ref[h, j + jj])], axis=1)
            v_ext = jnp.concatenate([v_ref[0, pl.ds(start, nb * t), h * V_DIM:(h + 1) * V_DIM], ones], axis=1)
            acc_ref[h] += _dot(p, v_ext)

    _for_blocks(i + 1, pass2, MLA_BLOCKS_PER_TRIP)
    for h in range(heads):
        acc = acc_ref[h]
        o_ref[0, :, h * V_DIM:(h + 1) * V_DIM] = (acc[:, :V_DIM] / acc[:, V_DIM:]).astype(o_ref.dtype)


def _mla_attn(q, k, v, *, t, heads=2):
    bsz, seq, _ = q.shape
    scale = 1.0 / math.sqrt(QK_NOPE + QK_ROPE)
    wq, wv = heads * QK_PAD, heads * V_DIM
    return pl.pallas_call(
        functools.partial(_mla_attn_kernel, t=t, scale=scale, heads=heads),
        out_shape=jax.ShapeDtypeStruct((bsz, seq, MLA_HEADS * V_DIM), BF16),
        grid=(bsz, MLA_HEADS // heads, seq // t),
        in_specs=[pl.BlockSpec((1, t, wq), lambda b, g, i: (b, i, g)),
                  pl.BlockSpec((1, seq, wq), lambda b, g, i: (b, 0, g)),
                  pl.BlockSpec((1, seq, wv), lambda b, g, i: (b, 0, g))],
        out_specs=pl.BlockSpec((1, t, wv), lambda b, g, i: (b, i, g)),
        scratch_shapes=[pltpu.VMEM((heads, seq // t, t, t), F32), pltpu.VMEM((heads, t, LANES), F32),
                        pltpu.VMEM((heads, t, V_DIM + LANES), F32)],
        compiler_params=_cparams(("parallel", "parallel", "arbitrary")),
        name="mla_attn",
    )(q, k, v)


def _sb_attn_kernel(q_ref, k_ref, v_ref, w_ref, o_ref, c_ref, acc_ref, kmax_ref, bound_ref, *, t, heads):
    i = pl.program_id(2)
    nl = t // LANES
    d = SB_HEAD_DIM
    c_ref[...] = jnp.zeros(c_ref.shape, F32)
    acc_ref[...] = jnp.zeros(acc_ref.shape, F32)
    sub = t // 2 if t % (2 * LANES) == 0 else t
    r_iota = lax.broadcasted_iota(jnp.int32, (sub, LANES), 0)
    c_iota = lax.broadcasted_iota(jnp.int32, (sub, LANES), 1)

    def block(r0, nr, start, nk, key0):
        rows = slice(r0, r0 + nr)
        masks = []
        for cc in range(nk // LANES):
            off = None if key0 is None else key0 + cc * LANES - r0
            masks.append(None if off is None or off <= -LANES else c_iota + off < r_iota)
        for h in range(heads):
            hs = slice(h * d, (h + 1) * d)
            zz = _dot_nt(q_ref[0, rows, hs], k_ref[0, pl.ds(start, nk), hs])
            zneg = -zz
            sp = jnp.log2(1.0 + jnp.exp2(jnp.minimum(zz, zneg)))
            log_1m = jnp.minimum(zneg, 0.0) - sp
            carry = c_ref[h, rows]
            a_chunks = [None] * (nk // LANES)
            for cc in reversed(range(nk // LANES)):
                cs = slice(cc * LANES, (cc + 1) * LANES)
                l1m = log_1m[:, cs] if masks[cc] is None else jnp.where(masks[cc], log_1m[:, cs], 0.0)
                hi, lo = _split_bf16(l1m)
                y = _dot(jnp.concatenate([hi, lo], axis=1), w_ref[...])
                a = jnp.exp2(zz[:, cs] + y[:, :LANES] + carry)
                carry = carry + y[:, LANES:]
                if masks[cc] is not None:
                    a = jnp.where(masks[cc], a, 0.0)
                a_chunks[cc] = a.astype(BF16)
            c_ref[h, rows] = carry
            acc_ref[h, rows] += _dot(jnp.concatenate(a_chunks, axis=1), v_ref[0, pl.ds(start, nk), hs])

    for band in range(t // sub):
        block(band * sub, sub, pl.multiple_of(i * t, t), (band + 1) * sub, 0)

    def left_blocks(n, nb):
        for jj in range(nb):
            block(0, t, pl.multiple_of((i - 1 - n - jj) * t, t), t, None)

    @pl.when(i == 0)
    def _():
        for h in range(heads):
            kmax = jnp.max(jnp.abs(k_ref[0, :, h * d:(h + 1) * d].astype(F32)))
            kmax_ref[h] = jnp.full(kmax_ref.shape[1:], kmax, F32)

    for h in range(heads):
        q_l1 = jnp.sum(jnp.abs(q_ref[0, :, h * d:(h + 1) * d].astype(F32)), axis=1, keepdims=True)
        bound_ref[h] = q_l1 * kmax_ref[h, 0:1, :]

    def more_to_add():
        worst = None
        for h in range(heads):
            top = jnp.max(c_ref[h] + bound_ref[h])
            worst = top if worst is None else jnp.maximum(worst, top)
        return worst >= SB_ZERO_LOG2

    go = (i > 0) & more_to_add()

    @pl.when(go)
    def _():
        left_blocks(0, 1)

    def trip(state):
        n, _ = state
        left_blocks(n, SB_BLOCKS_PER_TRIP)
        return n + SB_BLOCKS_PER_TRIP, more_to_add()

    n, go = lax.while_loop(lambda s: s[1] & (s[0] + SB_BLOCKS_PER_TRIP <= i), trip,
                           (jnp.int32(1), go & more_to_add()))
    size = SB_BLOCKS_PER_TRIP // 2
    while size:
        take = go & (n + size <= i)

        @pl.when(take)
        def _(n=n, size=size):
            left_blocks(n, size)

        n = n + jnp.where(take, size, 0)
        size //= 2
    for h in range(heads):
        o_ref[0, :, h * d:(h + 1) * d] = acc_ref[h].astype(o_ref.dtype)


def _sb_attn(q, kv, *, t, heads=4):
    bsz, seq, _ = q.shape
    w = heads * SB_HEAD_DIM
    groups = SB_HEADS // heads
    tri = (jnp.arange(LANES)[:, None] >= jnp.arange(LANES)[None, :]).astype(BF16)
    half = jnp.concatenate([tri, jnp.ones((LANES, LANES), BF16)], axis=1)
    w_sum = jnp.concatenate([half, half], axis=0)
    return pl.pallas_call(
        functools.partial(_sb_attn_kernel, t=t, heads=heads),
        out_shape=jax.ShapeDtypeStruct((bsz, seq, SB_HEADS * SB_HEAD_DIM), BF16),
        grid=(bsz, groups, seq // t),
        in_specs=[pl.BlockSpec((1, t, w), lambda b, g, i: (b, i, g)),
                  pl.BlockSpec((1, seq, w), lambda b, g, i: (b, 0, g)),
                  pl.BlockSpec((1, seq, w), lambda b, g, i: (b, 0, groups + g)),
                  pl.BlockSpec(w_sum.shape, lambda b, g, i: (0, 0))],
        out_specs=pl.BlockSpec((1, t, w), lambda b, g, i: (b, i, g)),
        scratch_shapes=[pltpu.VMEM((heads, t, LANES), F32), pltpu.VMEM((heads, t, SB_HEAD_DIM), F32),
                        pltpu.VMEM((heads, 8, LANES), F32), pltpu.VMEM((heads, t, LANES), F32)],
        compiler_params=_cparams(("parallel", "parallel", "arbitrary")),
        name="sb_attn",
    )(q, kv, kv, w_sum)


def _out_proj_kernel(o_ref, w_ref, x_ref, gt_ref, g_ref, sh_ref, sc_ref, wr_hi_ref, wr_lo_ref, br_ref, tri_ref,
                     xo_ref, h_ref, rt_ref, cnt_ref, *, n_experts):
    x = x_ref[...] + gt_ref[...] * _dot(o_ref[...], w_ref[...])
    xo_ref[...] = x
    h = _modulate(_rms(x, g_ref[...]), sh_ref[...], sc_ref[...])
    h_hi = h.astype(BF16)
    words = _pack_bf16_pairs(h)
    for part in range(SC_ROW_PARTS):
        h_ref[part] = words[:, part * h_ref.shape[2]:(part + 1) * h_ref.shape[2]]
    h_lo = (h - h_hi.astype(F32)).astype(BF16)
    logits = (_dot(h_hi, wr_hi_ref[...]) + _dot(h_lo, wr_hi_ref[...]) + _dot(h_hi, wr_lo_ref[...])
              + br_ref[...])
    lane = lax.broadcasted_iota(jnp.int32, logits.shape, 1).astype(F32)
    lg = jnp.where(lane < n_experts, logits, -jnp.inf)
    m1 = jnp.max(lg, axis=1, keepdims=True)
    i1 = jnp.min(jnp.where(lg == m1, lane, float(LANES)), axis=1, keepdims=True)
    lg2 = jnp.where(lane == i1, -jnp.inf, lg)
    m2 = jnp.max(lg2, axis=1, keepdims=True)
    i2 = jnp.min(jnp.where(lg2 == m2, lane, float(LANES)), axis=1, keepdims=True)
    e2 = jnp.exp(m2 - m1)
    den = 1.0 + e2
    sel = jnp.where((lane == i1) | (lane == i2), 1.0, 0.0)
    prefix = _dot(tri_ref[...], sel.astype(BF16))
    r1 = jnp.sum(jnp.where(lane == i1, prefix, 0.0), axis=1, keepdims=True)
    r2 = jnp.sum(jnp.where(lane == i2, prefix, 0.0), axis=1, keepdims=True)
    cnt_ref[...] = jnp.sum(sel, axis=0, keepdims=True)
    rt = jnp.zeros_like(logits)
    for k, val in enumerate((i1, i2, 1.0 / den, e2 / den, r1, r2)):
        rt = jnp.where(lane == k, val, rt)
    rt_ref[...] = rt


def _out_proj_router(o, w, x, mod, g, wr_hi, wr_lo, br, n_experts, *, seq, tm):
    n, d = x.shape
    tpb = seq // tm
    dpp = d // 2 // SC_ROW_PARTS
    tri = (jnp.arange(tm)[:, None] > jnp.arange(tm)[None, :]).astype(BF16)
    full = lambda a: pl.BlockSpec(a.shape, lambda i: (0,) * a.ndim)
    row = lambda wd: pl.BlockSpec((tm, wd), lambda i: (i, 0))
    return pl.pallas_call(
        functools.partial(_out_proj_kernel, n_experts=n_experts),
        out_shape=(jax.ShapeDtypeStruct((n, d), F32), jax.ShapeDtypeStruct((SC_ROW_PARTS, n, dpp), U32),
                   jax.ShapeDtypeStruct((n, LANES), F32), jax.ShapeDtypeStruct((n // tm, 1, LANES), F32)),
        grid=(n // tm,),
        in_specs=[row(o.shape[1]), full(w), row(d), _mod_spec(2, tpb, d), full(g),
                  _mod_spec(3, tpb, d), _mod_spec(4, tpb, d), full(wr_hi), full(wr_lo), full(br), full(tri)],
        out_specs=(row(d), pl.BlockSpec((SC_ROW_PARTS, tm, dpp), lambda i: (0, i, 0)), row(LANES),
                   pl.BlockSpec((None, 1, LANES), lambda i: (i, 0, 0))),
        compiler_params=_cparams(("parallel",)),
        name="out_proj_router",
    )(o, w, x, mod, g, mod, mod, wr_hi, wr_lo, br, tri)


def _silu(g):
    return g * (1.0 / (1.0 + jnp.exp(-g)))


def _ffn_kernel(o_ref, wo_ref, x_ref, gt1_ref, gf_ref, sh2_ref, sc2_ref, wg_ref, wu_ref, wd_ref, gt2_ref,
                gkv_ref, shk_ref, sck_ref, gm_ref, shm_ref, scm_ref,
                xo_ref, hk_ref, hm_ref, x1_ref, h_ref, acc_ref):
    f = pl.program_id(1)

    @pl.when(f == 0)
    def _():
        x1 = x_ref[...] + gt1_ref[...] * _dot(o_ref[...], wo_ref[...])
        x1_ref[...] = x1
        h_ref[...] = _modulate(_rms(x1, gf_ref[...]), sh2_ref[...], sc2_ref[...]).astype(BF16)
        acc_ref[...] = jnp.zeros(acc_ref.shape, F32)

    h = h_ref[...]
    a = _silu(_dot(h, wg_ref[...])) * _dot(h, wu_ref[...])
    acc_ref[...] += _dot(a.astype(BF16), wd_ref[...])

    @pl.when(f == pl.num_programs(1) - 1)
    def _():
        x = x1_ref[...] + gt2_ref[...] * acc_ref[...]
        xo_ref[...] = x
        hk_ref[...] = _modulate(_rms(x, gkv_ref[...]), shk_ref[...], sck_ref[...]).astype(BF16)
        hm_ref[...] = _modulate(_rms(x, gm_ref[...]), shm_ref[...], scm_ref[...]).astype(BF16)


def _ffn(o, w_o, x, mod0, g_ffn, w_gu, w_down, g_kv, mod_kv, g_mix1, mod1, *, seq, tm, tf):
    n, d = x.shape
    d_ff = w_down.shape[0]
    nf = d_ff // tf
    tpb = seq // tm
    full = lambda a: pl.BlockSpec(a.shape, lambda i, f: (0,) * a.ndim)
    row = lambda w=d: pl.BlockSpec((tm, w), lambda i, f: (i, 0))
    return pl.pallas_call(
        _ffn_kernel,
        out_shape=(jax.ShapeDtypeStruct((n, d), F32), jax.ShapeDtypeStruct((n, d), BF16),
                   jax.ShapeDtypeStruct((n, d), BF16)),
        grid=(n // tm, nf),
        in_specs=[row(o.shape[1]), full(w_o), row(), _mod_spec(2, tpb, d), full(g_ffn),
                  _mod_spec(3, tpb, d), _mod_spec(4, tpb, d),
                  pl.BlockSpec((d, tf), lambda i, f: (0, f)),
                  pl.BlockSpec((d, tf), lambda i, f: (0, nf + f)),
                  pl.BlockSpec((tf, d), lambda i, f: (f, 0)),
                  _mod_spec(5, tpb, d),
                  full(g_kv), _mod_spec(0, tpb, d), _mod_spec(1, tpb, d),
                  full(g_mix1), _mod_spec(0, tpb, d), _mod_spec(1, tpb, d)],
        out_specs=(row(), row(), row()),
        scratch_shapes=[pltpu.VMEM((tm, d), F32), pltpu.VMEM((tm, d), BF16), pltpu.VMEM((tm, d), F32)],
        compiler_params=_cparams(("parallel", "arbitrary"), vmem_limit=VMEM_LIMIT_LARGE),
        name="ffn_dense",
    )(o, w_o, x, mod0, g_ffn, mod0, mod0, w_gu, w_gu, w_down, mod0,
      g_kv, mod_kv, mod_kv, g_mix1, mod1, mod1)


def _linear_kernel(x_ref, w_ref, o_ref):
    o_ref[...] = _dot(x_ref[...], w_ref[...]).astype(o_ref.dtype)


def _linear(x, w, *, tm, tn, name):
    n, k = x.shape
    m = w.shape[1]
    return pl.pallas_call(
        _linear_kernel,
        out_shape=jax.ShapeDtypeStruct((n, m), BF16),
        grid=(m // tn, n // tm),
        in_specs=[pl.BlockSpec((tm, k), lambda j, i: (i, 0)), pl.BlockSpec((k, tn), lambda j, i: (0, j))],
        out_specs=pl.BlockSpec((tm, tn), lambda j, i: (i, j)),
        compiler_params=_cparams(("parallel", "parallel")),
        name=name,
    )(x, w)


def _route_plan(counts, n_exp, tm, n_tiles):
    cnt = counts[:, 0, :n_exp].astype(jnp.int32)
    sizes = jnp.sum(cnt, axis=0)
    padded = (sizes + tm - 1) // tm * tm
    ends = jnp.cumsum(padded)
    tile_base = (ends - padded)[None, :] + jnp.cumsum(cnt, axis=0) - cnt
    tile_start = jnp.arange(n_tiles, dtype=jnp.int32) * tm
    tile_expert = jnp.minimum(jnp.sum(tile_start[:, None] >= ends[None, :], axis=1), n_exp - 1)
    n_used = (ends[-1] // tm).reshape(1)
    n_valid = jnp.clip((ends - padded + sizes)[tile_expert] - tile_start, 0, tm)
    return tile_base, tile_expert.astype(jnp.int32), n_used.astype(jnp.int32), n_valid.astype(jnp.int32)


PICK_ROWS = 8


def _picks_kernel(rt_ref, base_ref, o_ref, *, n_rows):
    rt = rt_ref[...]
    lane = lax.broadcasted_iota(jnp.int32, rt.shape, 1).astype(F32)
    cols = jnp.zeros_like(rt)
    for k in range(TOP_K):
        e_k = rt[:, k:k + 1]
        r_k = rt[:, 2 * TOP_K + k:2 * TOP_K + k + 1]
        d_k = jnp.sum(jnp.where(lane == e_k, base_ref[...], 0.0), axis=1, keepdims=True) + r_k
        for p in range(SC_ROW_PARTS):
            cols = jnp.where(lane == p * TOP_K + k, d_k + float(p * n_rows), cols)
    o_ref[...] = cols.T[:PICK_ROWS].astype(jnp.int32)


def _picks(rt, tile_base, n_rows, *, tm):
    n = rt.shape[0]
    base = _pad_last(tile_base.astype(F32), LANES).reshape(n // tm, 1, LANES)
    out = pl.pallas_call(
        functools.partial(_picks_kernel, n_rows=n_rows),
        out_shape=jax.ShapeDtypeStruct((PICK_ROWS, n), jnp.int32),
        grid=(n // tm,),
        in_specs=[pl.BlockSpec((tm, LANES), lambda i: (i, 0)),
                  pl.BlockSpec((None, 1, LANES), lambda i: (i, 0, 0))],
        out_specs=pl.BlockSpec((PICK_ROWS, tm), lambda i: (0, i)),
        compiler_params=_cparams(("parallel",)),
        name="moe_picks",
    )(rt, base)
    return out[:SC_ROW_PARTS * TOP_K].reshape(SC_ROW_PARTS, TOP_K, n)


def _sc_mesh():
    return plsc.VectorSubcoreMesh(core_axis_name="core", subcore_axis_name="subcore",
                                  num_cores=SC_CORES, num_subcores=SC_SUBCORES)


def _sc_scatter_rows(x, idx_list, n_rows):
    m, d = x.shape

    @functools.partial(pl.kernel, out_type=jax.ShapeDtypeStruct((n_rows, d), x.dtype), mesh=_sc_mesh(),
                       scratch_types=[], name="sc_scatter_rows")
    def scatter(x_hbm, *refs):
        i_hbms, o_hbm = refs[:-1], refs[-1]

        def body(x_vmem, *i_vmems):
            for i_vmem in i_vmems:
                pltpu.sync_copy(x_vmem, o_hbm.at[i_vmem.at[0]])

        pltpu.emit_pipeline(
            body,
            grid=(m // SC_WINDOW,),
            in_specs=[pl.BlockSpec((SC_WINDOW, d), index_map=lambda i: (i, 0))]
                     + [pl.BlockSpec((1, SC_WINDOW), index_map=lambda i: (0, i))] * len(idx_list),
            out_specs=[],
            core_axis_name=("core", "subcore"),
            dimension_semantics=(pltpu.PARALLEL,),
        )(x_hbm, *i_hbms)

    return scatter(x, *[idx.reshape(1, m) for idx in idx_list])


def _grouped_ffn_kernel(te_ref, nu_ref, nv_ref, x_ref, wg_ref, wu_ref, wd_ref, y_ref, hb_ref, acc_ref):
    del te_ref
    i = pl.program_id(0)
    f = pl.program_id(1)
    used = i < nu_ref[0]
    last_f = f == pl.num_programs(1) - 1

    @pl.when(used & (f == 0))
    def _():
        words = jnp.concatenate([x_ref[p] for p in range(SC_ROW_PARTS)], axis=1)
        row = lax.broadcasted_iota(jnp.int32, words.shape, 0)
        words = jnp.where(row < nv_ref[i], words, U32(0))
        hb_ref[...] = _unpack_bf16_pairs(words).astype(BF16)
        acc_ref[...] = jnp.zeros(acc_ref.shape, F32)

    @pl.when(used)
    def _():
        h = hb_ref[...]
        a = _silu(_dot(h, wg_ref[0])) * _dot(h, wu_ref[0])
        acc_ref[...] += _dot(a.astype(BF16), wd_ref[0])

    @pl.when(used & last_f)
    def _():
        y = _pack_bf16_pairs(acc_ref[...])
        for part in range(SC_ROW_PARTS):
            y_ref[part] = y[:, part * y_ref.shape[2]:(part + 1) * y_ref.shape[2]]

    @pl.when(jnp.logical_not(used) & last_f)
    def _():
        y_ref[...] = jnp.zeros(y_ref.shape, y_ref.dtype)


def _grouped_ffn(tile_expert, n_used, n_valid, xg, w_gu, w_down, *, tm, tf):
    parts, n_rows, dpp = xg.shape
    n_exp, d_ff, d = w_down.shape
    nf = d_ff // tf

    def wspec(shape, index):
        def index_map(i, f, te, nu, nv):
            return index(te[i], jnp.where(i < nu[0], f, nf - 1))
        return pl.BlockSpec(shape, index_map)

    rows = pl.BlockSpec((parts, tm, dpp), lambda i, f, te, nu, nv: (0, i, 0))
    return pl.pallas_call(
        _grouped_ffn_kernel,
        out_shape=jax.ShapeDtypeStruct(xg.shape, xg.dtype),
        grid_spec=pltpu.PrefetchScalarGridSpec(
            num_scalar_prefetch=3,
            grid=(n_rows // tm, nf),
            in_specs=[rows,
                      wspec((1, d, tf), lambda e, f: (e, 0, f)),
                      wspec((1, d, tf), lambda e, f: (e, 0, nf + f)),
                      wspec((1, tf, d), lambda e, f: (e, f, 0))],
            out_specs=rows,
            scratch_shapes=[pltpu.VMEM((tm, d), BF16), pltpu.VMEM((tm, d), F32)]),
        compiler_params=_cparams(("arbitrary", "arbitrary")),
        name="moe_grouped_ffn",
    )(tile_expert, n_used, n_valid, xg, w_gu, w_gu, w_down)


def _sc_gather_rows(table, idx):
    n_idx = idx.shape[0]
    d = table.shape[1]

    @functools.partial(pl.kernel, out_type=jax.ShapeDtypeStruct((n_idx, d), table.dtype), mesh=_sc_mesh(),
                       name="sc_gather_rows")
    def gather(t_hbm, i_hbm, o_hbm):
        def body(i_vmem, o_vmem):
            pltpu.sync_copy(t_hbm.at[i_vmem.at[0]], o_vmem)

        pltpu.emit_pipeline(
            body,
            grid=(n_idx // SC_WINDOW,),
            in_specs=[pl.BlockSpec((1, SC_WINDOW), index_map=lambda i: (0, i))],
            out_specs=[pl.BlockSpec((SC_WINDOW, d), index_map=lambda i: (i, 0))],
            core_axis_name=("core", "subcore"),
            dimension_semantics=(pltpu.PARALLEL,),
        )(i_hbm, o_hbm)

    return gather(table, idx.reshape(1, n_idx))


def _combine_kernel(x_ref, rt_ref, gt_ref, gf_ref, *refs):
    y_refs, o_ref = refs[:-1], refs[-1]
    rt = rt_ref[...]
    tot = None
    for k in range(TOP_K):
        words = jnp.concatenate([y_refs[p * TOP_K + k][...] for p in range(SC_ROW_PARTS)], axis=1)
        term = rt[:, TOP_K + k:TOP_K + k + 1] * _unpack_bf16_pairs(words)
        tot = term if tot is None else tot + term
    x = x_ref[...] + gt_ref[...] * tot
    o_ref[...] = _rms(x, gf_ref[...])


def _combine(x, rt, mod1, g_final, ysel, *, seq, tm):
    n, d = x.shape
    tpb = seq // tm
    nt = n // tm
    row = lambda w: pl.BlockSpec((tm, w), lambda i: (i, 0))
    piece = lambda j: pl.BlockSpec((tm, ysel.shape[1]), lambda i: (j * nt + i, 0))
    n_pieces = SC_ROW_PARTS * TOP_K
    return pl.pallas_call(
        _combine_kernel,
        out_shape=jax.ShapeDtypeStruct((n, d), F32),
        grid=(nt,),
        in_specs=[row(d), row(LANES), _mod_spec(5, tpb, d), pl.BlockSpec(g_final.shape, lambda i: (0, 0))]
                 + [piece(j) for j in range(n_pieces)],
        out_specs=row(d),
        compiler_params=_cparams(("parallel",)),
        name="moe_combine",
    )(x, rt, mod1, g_final, *([ysel] * n_pieces))


def _pad_last(a, width):
    return jnp.pad(a, [(0, 0)] * (a.ndim - 1) + [(0, width - a.shape[-1])])


def kernel(x, c, positions, w_mod, b_mod, g_mix, g_ffn, w_a_down, g_q_lat, g_kv_lat, w_uq, w_ukv, w_oa,
           w_mod_kv, b_mod_kv, g_kv, w_kv_sb, w_q_sb, w_o_sb, w_ffn_gu, w_ffn_down, w_router, b_router,
           w_exp_gu, w_exp_down, g_final):
    bsz, seq, d = x.shape
    n = bsz * seq
    q_lora, kv_lora = g_q_lat.shape[1], g_kv_lat.shape[1]
    n_exp = w_router.shape[-1]
    d_ff = w_ffn_down.shape[1]
    tm = min(TOKEN_TILE, seq)
    t_mla = min(MLA_TILE, seq)
    t_sb = min(SB_TILE, seq)
    tf = next((t for t in (FFN_CHUNK, 512) if d_ff % t == 0), d_ff)

    mod0 = _modvec(c, w_mod[0], b_mod[0]).reshape(bsz, 6, 1, d)
    mod1 = _modvec(c, w_mod[1], b_mod[1]).reshape(bsz, 6, 1, d)
    mod_kv = _modvec(c, w_mod_kv, b_mod_kv).reshape(bsz, 2, 1, d)

    lat_w = q_lora + kv_lora + LANES
    wd = _pad_last(w_a_down[0], lat_w).astype(BF16)
    wq = w_uq[0].reshape(q_lora, MLA_HEADS, QK_NOPE + QK_ROPE)
    wqn = wq[:, :, :QK_NOPE].reshape(q_lora, MLA_HEADS * QK_NOPE).astype(BF16)
    wqr = _pad_last(wq[:, :, QK_NOPE:], LANES).reshape(q_lora, MLA_HEADS * LANES).astype(BF16)
    wkv = w_ukv[0].reshape(kv_lora, MLA_HEADS, QK_NOPE + V_DIM)
    wkn = wkv[:, :, :QK_NOPE].reshape(kv_lora, MLA_HEADS * QK_NOPE).astype(BF16)
    wv = wkv[:, :, QK_NOPE:].reshape(kv_lora, MLA_HEADS * V_DIM).astype(BF16)
    half = QK_ROPE // 2
    inv = ROPE_THETA ** (-jnp.arange(half, dtype=F32) / half)
    inv = jnp.tile(inv, ROPE_PACK).reshape(1, LANES)
    wr = _pad_last(w_router[0], LANES)
    wr_hi = wr.astype(BF16)
    wr_lo = (wr - wr_hi.astype(F32)).astype(BF16)
    br = _pad_last(b_router[0], LANES).reshape(1, LANES)

    xf = x.reshape(n, d)
    pos = positions.reshape(n // tm, ROPE_PACK, tm // ROPE_PACK).swapaxes(1, 2)
    pos = jnp.repeat(pos.reshape(n // ROPE_PACK, ROPE_PACK), QK_ROPE // 2, axis=1)
    row1 = lambda a: a.reshape(1, -1)

    q, k, v = _mla_proj(xf, pos, mod0, row1(g_mix[0]), wd, row1(g_q_lat[0]), row1(g_kv_lat[0]),
                        wqn, wqr, wkn, wv, inv, seq=seq, tm=tm)
    o = _mla_attn(q.reshape(bsz, seq, -1), k.reshape(bsz, seq, -1), v.reshape(bsz, seq, -1), t=t_mla)
    x2, hk, hm = _ffn(o.reshape(n, -1), w_oa[0].astype(BF16), xf, mod0, row1(g_ffn[0]),
                      w_ffn_gu[0].astype(BF16), w_ffn_down[0].astype(BF16),
                      row1(g_kv), mod_kv, row1(g_mix[1]), mod1, seq=seq, tm=tm, tf=tf)
    kv = _linear(hk, w_kv_sb.astype(BF16), tm=tm, tn=w_kv_sb.shape[1], name="kv_proj")
    q_scale = math.log2(math.e) / math.sqrt(SB_HEAD_DIM)
    qs = _linear(hm, (w_q_sb[0] * q_scale).astype(BF16), tm=tm, tn=1024, name="q_proj")
    o = _sb_attn(qs.reshape(bsz, seq, -1), kv.reshape(bsz, seq, -1), t=t_sb)
    x3, h, rt, counts = _out_proj_router(o.reshape(n, -1), w_o_sb[0].astype(BF16), x2, mod1, row1(g_ffn[1]),
                                         wr_hi, wr_lo, br, n_exp, seq=seq, tm=tm)
    n_tiles = TOP_K * n // tm + n_exp
    n_rows = n_tiles * tm
    tile_base, tile_expert, n_used, n_valid = _route_plan(counts, n_exp, tm, n_tiles)
    picks = _picks(rt, tile_base, n_rows, tm=tm)
    xg = _sc_scatter_rows(h.reshape(SC_ROW_PARTS * n, -1), [picks[:, k].reshape(-1) for k in range(TOP_K)],
                          SC_ROW_PARTS * n_rows)
    yg = _grouped_ffn(tile_expert, n_used, n_valid, xg.reshape(SC_ROW_PARTS, n_rows, -1),
                      w_exp_gu[0].astype(BF16), w_exp_down[0].astype(BF16), tm=tm, tf=tf)
    ysel = _sc_gather_rows(yg.reshape(SC_ROW_PARTS * n_rows, -1), picks.reshape(-1))
    out = _combine(x3, rt, mod1, row1(g_final), ysel, seq=seq, tm=tm)
    return out.reshape(bsz, seq, d)
```

```python
import functools
import math

import jax
import jax.numpy as jnp
from jax import lax
from jax.experimental import pallas as pl
from jax.experimental.pallas import tpu as pltpu
from jax.experimental.pallas import tpu_sc as plsc

F32 = jnp.float32
BF16 = jnp.bfloat16

EPS = 1e-6
MLA_HEADS = 8
QK_NOPE = 128
QK_ROPE = 64
V_DIM = 128
ROPE_THETA = 10000.0
SB_HEADS = 8
SB_HEAD_DIM = 128
TOP_K = 2

LANES = 128
QK_PAD = 256
ROPE_PACK = LANES // (QK_ROPE // 2)
VMEM_LIMIT = 48 * 1024 * 1024
VMEM_LIMIT_LARGE = 58 * 1024 * 1024
TOKEN_TILE = 512
MLA_TILE = 512
MLA_BLOCKS_PER_TRIP = 4
SB_TILE = 512
SB_BLOCKS_PER_TRIP = 2
SB_ZERO_LOG2 = -140.0
FFN_CHUNK = 1792
SC_CORES = 2
SC_SUBCORES = 16
SC_WINDOW = 128
SC_ROW_PARTS = 2


def _cparams(sem, vmem_limit=VMEM_LIMIT):
    return pltpu.CompilerParams(dimension_semantics=sem, vmem_limit_bytes=vmem_limit)


def _rms(x, g):
    return x * lax.rsqrt(jnp.mean(x * x, axis=-1, keepdims=True) + EPS) * g


def _modulate(h, shift, scale):
    return h * (1.0 + scale) + shift


def _split_bf16(a):
    hi = a.astype(BF16)
    lo = (a - hi.astype(F32)).astype(BF16)
    return hi, lo


U32 = jnp.uint32
HIGH_HALF = 0xFFFF0000


def _pack_bf16_pairs(a):
    half = a.shape[1] // 2
    bits = lambda v: lax.bitcast_convert_type(v.astype(BF16).astype(F32), U32)
    return (bits(a[:, :half]) >> 16) | (bits(a[:, half:]) & U32(HIGH_HALF))


def _unpack_bf16_pairs(w):
    lo = lax.bitcast_convert_type(w << 16, F32)
    hi = lax.bitcast_convert_type(w & U32(HIGH_HALF), F32)
    return jnp.concatenate([lo, hi], axis=1)


def _dot(a, b):
    return jnp.dot(a, b, preferred_element_type=F32)


def _dot_nt(a, b):
    return lax.dot_general(a, b, (((1,), (1,)), ((), ())), preferred_element_type=F32)


def _modvec_kernel(c_ref, w_ref, b_ref, o_ref):
    c = c_ref[...]
    sc = c * (1.0 / (1.0 + jnp.exp(-c)))
    a_hi, a_lo = _split_bf16(sc)
    w_hi, w_lo = _split_bf16(w_ref[...])
    o_ref[...] = _dot(a_hi, w_hi) + _dot(a_lo, w_hi) + _dot(a_hi, w_lo) + b_ref[...]


def _modvec(c, w, b, tn=512):
    bsz, d = c.shape
    n = w.shape[1]
    return pl.pallas_call(
        _modvec_kernel,
        out_shape=jax.ShapeDtypeStruct((bsz, n), F32),
        grid=(n // tn,),
        in_specs=[pl.BlockSpec((bsz, d), lambda j: (0, 0)),
                  pl.BlockSpec((d, tn), lambda j: (0, j)),
                  pl.BlockSpec((1, tn), lambda j: (0, j))],
        out_specs=pl.BlockSpec((bsz, tn), lambda j: (0, j)),
        compiler_params=_cparams(("arbitrary",)),
        name="modvec",
    )(c, w, b.reshape(1, n))


def _mod_spec(chunk, tiles_per_batch, d):
    return pl.BlockSpec((None, None, 1, d), lambda i, *_: (i // tiles_per_batch, chunk, 0, 0))


def _mla_proj_kernel(x_ref, pos_ref, sh_ref, sc_ref, g_ref, wd_ref, gq_ref, gkv_ref,
                     wqn_ref, wqr_ref, wkn_ref, wv_ref, inv_ref,
                     q_ref, k_ref, v_ref, *, q_lora, kv_lora):
    x = x_ref[...]
    h = _modulate(_rms(x, g_ref[...]), sh_ref[...], sc_ref[...]).astype(BF16)
    lat = _dot(h, wd_ref[...])
    c_q = _rms(lat[:, :q_lora], gq_ref[...]).astype(BF16)
    c_kv = _rms(lat[:, q_lora:q_lora + kv_lora], gkv_ref[...]).astype(BF16)
    k_rot = lat[:, q_lora + kv_lora:]

    half = QK_ROPE // 2
    ang = pos_ref[...].astype(F32) * inv_ref[...]
    cos_p = jnp.cos(ang)
    sin_p = jnp.sin(ang)
    lane = lax.broadcasted_iota(jnp.int32, ang.shape, 1)
    cos, s_a, s_b = [], [], []
    for g in range(ROPE_PACK):
        shift = (LANES - g * half) % LANES
        cg = pltpu.roll(cos_p, shift, axis=1) if shift else cos_p
        sg = pltpu.roll(sin_p, shift, axis=1) if shift else sin_p
        cos.append(jnp.where(lane < half, cg, pltpu.roll(cg, half, axis=1)))
        s_a.append(jnp.where(lane < half, -sg, 0.0))
        s_b.append(jnp.where((lane >= half) & (lane < 2 * half), pltpu.roll(sg, half, axis=1), 0.0))
    cos, s_a, s_b = (jnp.concatenate(v, axis=0) for v in (cos, s_a, s_b))

    def rope(r):
        return (r * cos + pltpu.roll(r, LANES - half, axis=1) * s_a
                + pltpu.roll(r, half, axis=1) * s_b)

    k_rot = rope(k_rot).astype(BF16)
    q_nope = _dot(c_q, wqn_ref[...])
    q_rope = _dot(c_q, wqr_ref[...])
    k_nope = _dot(c_kv, wkn_ref[...])
    v_ref[...] = _dot(c_kv, wv_ref[...]).astype(BF16)
    for hd in range(MLA_HEADS):
        a, b = hd * LANES, (hd + 1) * LANES
        q_ref[:, hd * QK_PAD:hd * QK_PAD + LANES] = q_nope[:, a:b].astype(BF16)
        q_ref[:, hd * QK_PAD + LANES:(hd + 1) * QK_PAD] = rope(q_rope[:, a:b]).astype(BF16)
        k_ref[:, hd * QK_PAD:hd * QK_PAD + LANES] = k_nope[:, a:b].astype(BF16)
        k_ref[:, hd * QK_PAD + LANES:(hd + 1) * QK_PAD] = k_rot


def _mla_proj(x, pos, mod, g_mix, wd, gq, gkv, wqn, wqr, wkn, wv, inv, *, seq, tm):
    n, d = x.shape
    tpb = seq // tm
    q_lora, kv_lora = gq.shape[1], gkv.shape[1]
    hq = MLA_HEADS * QK_PAD
    hv = MLA_HEADS * V_DIM
    full = lambda a: pl.BlockSpec(a.shape, lambda i: (0,) * a.ndim)
    row = lambda w: pl.BlockSpec((tm, w), lambda i: (i, 0))
    return pl.pallas_call(
        functools.partial(_mla_proj_kernel, q_lora=q_lora, kv_lora=kv_lora),
        out_shape=(jax.ShapeDtypeStruct((n, hq), BF16), jax.ShapeDtypeStruct((n, hq), BF16),
                   jax.ShapeDtypeStruct((n, hv), BF16)),
        grid=(n // tm,),
        in_specs=[row(d), pl.BlockSpec((tm // ROPE_PACK, LANES), lambda i: (i, 0)), _mod_spec(0, tpb, d), _mod_spec(1, tpb, d), full(g_mix), full(wd),
                  full(gq), full(gkv), full(wqn), full(wqr), full(wkn), full(wv), full(inv)],
        out_specs=(row(hq), row(hq), row(hv)),
        compiler_params=_cparams(("parallel",)),
        name="mla_proj",
    )(x, pos, mod, mod, g_mix, wd, gq, gkv, wqn, wqr, wkn, wv, inv)


def _lane_chunks(a):
    return [a[:, c * LANES:(c + 1) * LANES] for c in range(a.shape[1] // LANES)]


def _for_blocks(n, step, group):
    def trip(p, carry):
        step(group * p, group)
        return carry

    lax.fori_loop(0, n // group, trip, 0)
    size = group // 2
    while size:
        @pl.when((n // size) % 2 == 1)
        def _(size=size):
            step(n // (2 * size) * (2 * size), size)
        size //= 2


def _mla_attn_kernel(q_ref, k_ref, v_ref, o_ref, s_ref, m_ref, acc_ref, *, t, scale, heads):
    i = pl.program_id(2)

    def lane_max(s):
        m = None
        for sc in _lane_chunks(s):
            m = sc if m is None else jnp.maximum(m, sc)
        return m

    def scores(h, j):
        start = pl.multiple_of(j * t, t)
        hs = slice(h * QK_PAD, (h + 1) * QK_PAD)
        return _dot_nt(q_ref[0, :, hs], k_ref[0, pl.ds(start, t), hs])

    r_minus_c = (lax.broadcasted_iota(jnp.int32, (t, t), 0) - lax.broadcasted_iota(jnp.int32, (t, t), 1))
    m_ref[...] = jnp.full(m_ref.shape, -jnp.inf, F32)

    def pass1(j, nb):
        for h in range(heads):
            m = m_ref[h]
            for jj in range(nb):
                s = jnp.where(r_minus_c >= (j + jj - i) * t, scores(h, j + jj), -jnp.inf)
                s_ref[h, j + jj] = s
                m = jnp.maximum(m, lane_max(s))
            m_ref[h] = m

    _for_blocks(i + 1, pass1, MLA_BLOCKS_PER_TRIP)
    for h in range(heads):
        m_ref[h] = jnp.broadcast_to(jnp.max(m_ref[h], axis=1, keepdims=True), (t, LANES))
    acc_ref[...] = jnp.zeros(acc_ref.shape, F32)
    cst = scale * math.log2(math.e)

    def pass2(j, nb):
        start = pl.multiple_of(j * t, t)
        ones = jnp.ones((nb * t, LANES), BF16)
        for h in range(heads):
            m = m_ref[h]
            p = jnp.concatenate([jnp.exp2((sc - m) * cst).astype(BF16)
                                 for jj in range(nb) for sc in _lane_chunks(s_ref[h, j + jj])], axis=1)
            v_ext = jnp.concatenate([v_ref[0, pl.ds(start, nb * t), h * V_DIM:(h + 1) * V_DIM], ones], axis=1)
            acc_ref[h] += _dot(p, v_ext)

    _for_blocks(i + 1, pass2, MLA_BLOCKS_PER_TRIP)
    for h in range(heads):
        acc = acc_ref[h]
        o_ref[0, :, h * V_DIM:(h + 1) * V_DIM] = (acc[:, :V_DIM] / acc[:, V_DIM:]).astype(o_ref.dtype)


def _mla_attn(q, k, v, *, t, heads=2):
    bsz, seq, _ = q.shape
    scale = 1.0 / math.sqrt(QK_NOPE + QK_ROPE)
    wq, wv = heads * QK_PAD, heads * V_DIM
    return pl.pallas_call(
        functools.partial(_mla_attn_kernel, t=t, scale=scale, heads=heads),
        out_shape=jax.ShapeDtypeStruct((bsz, seq, MLA_HEADS * V_DIM), BF16),
        grid=(bsz, MLA_HEADS // heads, seq // t),
        in_specs=[pl.BlockSpec((1, t, wq), lambda b, g, i: (b, i, g)),
                  pl.BlockSpec((1, seq, wq), lambda b, g, i: (b, 0, g)),
                  pl.BlockSpec((1, seq, wv), lambda b, g, i: (b, 0, g))],
        out_specs=pl.BlockSpec((1, t, wv), lambda b, g, i: (b, i, g)),
        scratch_shapes=[pltpu.VMEM((heads, seq // t, t, t), F32), pltpu.VMEM((heads, t, LANES), F32),
                        pltpu.VMEM((heads, t, V_DIM + LANES), F32)],
        compiler_params=_cparams(("parallel", "parallel", "arbitrary")),
        name="mla_attn",
    )(q, k, v)


def _sb_attn_kernel(q_ref, k_ref, v_ref, w_ref, o_ref, c_ref, acc_ref, kmax_ref, bound_ref, *, t, heads):
    i = pl.program_id(2)
    nl = t // LANES
    d = SB_HEAD_DIM
    c_ref[...] = jnp.zeros(c_ref.shape, F32)
    acc_ref[...] = jnp.zeros(acc_ref.shape, F32)
    sub = t // 2 if t % (2 * LANES) == 0 else t
    r_iota = lax.broadcasted_iota(jnp.int32, (sub, LANES), 0)
    c_iota = lax.broadcasted_iota(jnp.int32, (sub, LANES), 1)

    def block(r0, nr, start, nk, key0):
        rows = slice(r0, r0 + nr)
        masks = []
        for cc in range(nk // LANES):
            off = None if key0 is None else key0 + cc * LANES - r0
            masks.append(None if off is None or off <= -LANES else c_iota + off < r_iota)
        for h in range(heads):
            hs = slice(h * d, (h + 1) * d)
            zz = _dot_nt(q_ref[0, rows, hs], k_ref[0, pl.ds(start, nk), hs])
            zneg = -zz
            sp = jnp.log2(1.0 + jnp.exp2(jnp.minimum(zz, zneg)))
            log_1m = jnp.minimum(zneg, 0.0) - sp
            carry = c_ref[h, rows]
            a_chunks = [None] * (nk // LANES)
            for cc in reversed(range(nk // LANES)):
                cs = slice(cc * LANES, (cc + 1) * LANES)
                l1m = log_1m[:, cs] if masks[cc] is None else jnp.where(masks[cc], log_1m[:, cs], 0.0)
                hi, lo = _split_bf16(l1m)
                y = _dot(jnp.concatenate([hi, lo], axis=1), w_ref[...])
                a = jnp.exp2(zz[:, cs] + y[:, :LANES] + carry)
                carry = carry + y[:, LANES:]
                if masks[cc] is not None:
                    a = jnp.where(masks[cc], a, 0.0)
                a_chunks[cc] = a.astype(BF16)
            c_ref[h, rows] = carry
            acc_ref[h, rows] += _dot(jnp.concatenate(a_chunks, axis=1), v_ref[0, pl.ds(start, nk), hs])

    for band in range(t // sub):
        block(band * sub, sub, pl.multiple_of(i * t, t), (band + 1) * sub, 0)

    def left_blocks(n, nb):
        for jj in range(nb):
            block(0, t, pl.multiple_of((i - 1 - n - jj) * t, t), t, None)

    @pl.when(i == 0)
    def _():
        for h in range(heads):
            kmax = jnp.max(jnp.abs(k_ref[0, :, h * d:(h + 1) * d].astype(F32)))
            kmax_ref[h] = jnp.full(kmax_ref.shape[1:], kmax, F32)

    for h in range(heads):
        q_l1 = jnp.sum(jnp.abs(q_ref[0, :, h * d:(h + 1) * d].astype(F32)), axis=1, keepdims=True)
        bound_ref[h] = q_l1 * kmax_ref[h, 0:1, :]

    def more_to_add():
        worst = None
        for h in range(heads):
            top = jnp.max(c_ref[h] + bound_ref[h])
            worst = top if worst is None else jnp.maximum(worst, top)
        return worst >= SB_ZERO_LOG2

    go = (i > 0) & more_to_add()
    for piece in reversed(range(t // sub)):
        @pl.when(go)
        def _(piece=piece):
            block(0, t, pl.multiple_of((i - 1) * t + piece * sub, sub), sub, None)

        go = go & more_to_add()

    def trip(state):
        n, _ = state
        left_blocks(n, SB_BLOCKS_PER_TRIP)
        return n + SB_BLOCKS_PER_TRIP, more_to_add()

    n, go = lax.while_loop(lambda s: s[1] & (s[0] + SB_BLOCKS_PER_TRIP <= i), trip,
                           (jnp.int32(1), go))
    size = SB_BLOCKS_PER_TRIP // 2
    while size:
        take = go & (n + size <= i)

        @pl.when(take)
        def _(n=n, size=size):
            left_blocks(n, size)

        n = n + jnp.where(take, size, 0)
        size //= 2
    for h in range(heads):
        o_ref[0, :, h * d:(h + 1) * d] = acc_ref[h].astype(o_ref.dtype)


def _sb_attn(q, kv, *, t, heads=4):
    bsz, seq, _ = q.shape
    w = heads * SB_HEAD_DIM
    groups = SB_HEADS // heads
    tri = (jnp.arange(LANES)[:, None] >= jnp.arange(LANES)[None, :]).astype(BF16)
    half = jnp.concatenate([tri, jnp.ones((LANES, LANES), BF16)], axis=1)
    w_sum = jnp.concatenate([half, half], axis=0)
    return pl.pallas_call(
        functools.partial(_sb_attn_kernel, t=t, heads=heads),
        out_shape=jax.ShapeDtypeStruct((bsz, seq, SB_HEADS * SB_HEAD_DIM), BF16),
        grid=(bsz, groups, seq // t),
        in_specs=[pl.BlockSpec((1, t, w), lambda b, g, i: (b, i, g)),
                  pl.BlockSpec((1, seq, w), lambda b, g, i: (b, 0, g)),
                  pl.BlockSpec((1, seq, w), lambda b, g, i: (b, 0, groups + g)),
                  pl.BlockSpec(w_sum.shape, lambda b, g, i: (0, 0))],
        out_specs=pl.BlockSpec((1, t, w), lambda b, g, i: (b, i, g)),
        scratch_shapes=[pltpu.VMEM((heads, t, LANES), F32), pltpu.VMEM((heads, t, SB_HEAD_DIM), F32),
                        pltpu.VMEM((heads, 8, LANES), F32), pltpu.VMEM((heads, t, LANES), F32)],
        compiler_params=_cparams(("parallel", "parallel", "arbitrary")),
        name="sb_attn",
    )(q, kv, kv, w_sum)


def _out_proj_kernel(o_ref, w_ref, x_ref, gt_ref, g_ref, sh_ref, sc_ref, wr_hi_ref, wr_lo_ref, br_ref, tri_ref,
                     xo_ref, h_ref, rt_ref, cnt_ref, *, n_experts):
    x = x_ref[...] + gt_ref[...] * _dot(o_ref[...], w_ref[...])
    xo_ref[...] = x
    h = _modulate(_rms(x, g_ref[...]), sh_ref[...], sc_ref[...])
    h_hi = h.astype(BF16)
    words = _pack_bf16_pairs(h)
    for part in range(SC_ROW_PARTS):
        h_ref[part] = words[:, part * h_ref.shape[2]:(part + 1) * h_ref.shape[2]]
    h_lo = (h - h_hi.astype(F32)).astype(BF16)
    logits = (_dot(h_hi, wr_hi_ref[...]) + _dot(h_lo, wr_hi_ref[...]) + _dot(h_hi, wr_lo_ref[...])
              + br_ref[...])
    lane = lax.broadcasted_iota(jnp.int32, logits.shape, 1).astype(F32)
    lg = jnp.where(lane < n_experts, logits, -jnp.inf)
    m1 = jnp.max(lg, axis=1, keepdims=True)
    i1 = jnp.min(jnp.where(lg == m1, lane, float(LANES)), axis=1, keepdims=True)
    lg2 = jnp.where(lane == i1, -jnp.inf, lg)
    m2 = jnp.max(lg2, axis=1, keepdims=True)
    i2 = jnp.min(jnp.where(lg2 == m2, lane, float(LANES)), axis=1, keepdims=True)
    e2 = jnp.exp(m2 - m1)
    den = 1.0 + e2
    sel = jnp.where((lane == i1) | (lane == i2), 1.0, 0.0)
    prefix = _dot(tri_ref[...], sel.astype(BF16))
    r1 = jnp.sum(jnp.where(lane == i1, prefix, 0.0), axis=1, keepdims=True)
    r2 = jnp.sum(jnp.where(lane == i2, prefix, 0.0), axis=1, keepdims=True)
    cnt_ref[...] = jnp.sum(sel, axis=0, keepdims=True)
    rt = jnp.zeros_like(logits)
    for k, val in enumerate((i1, i2, 1.0 / den, e2 / den, r1, r2)):
        rt = jnp.where(lane == k, val, rt)
    rt_ref[...] = rt


def _out_proj_router(o, w, x, mod, g, wr_hi, wr_lo, br, n_experts, *, seq, tm):
    n, d = x.shape
    tpb = seq // tm
    dpp = d // 2 // SC_ROW_PARTS
    tri = (jnp.arange(tm)[:, None] > jnp.arange(tm)[None, :]).astype(BF16)
    full = lambda a: pl.BlockSpec(a.shape, lambda i: (0,) * a.ndim)
    row = lambda wd: pl.BlockSpec((tm, wd), lambda i: (i, 0))
    return pl.pallas_call(
        functools.partial(_out_proj_kernel, n_experts=n_experts),
        out_shape=(jax.ShapeDtypeStruct((n, d), F32), jax.ShapeDtypeStruct((SC_ROW_PARTS, n, dpp), U32),
                   jax.ShapeDtypeStruct((n, LANES), F32), jax.ShapeDtypeStruct((n // tm, 1, LANES), F32)),
        grid=(n // tm,),
        in_specs=[row(o.shape[1]), full(w), row(d), _mod_spec(2, tpb, d), full(g),
                  _mod_spec(3, tpb, d), _mod_spec(4, tpb, d), full(wr_hi), full(wr_lo), full(br), full(tri)],
        out_specs=(row(d), pl.BlockSpec((SC_ROW_PARTS, tm, dpp), lambda i: (0, i, 0)), row(LANES),
                   pl.BlockSpec((None, 1, LANES), lambda i: (i, 0, 0))),
        compiler_params=_cparams(("parallel",)),
        name="out_proj_router",
    )(o, w, x, mod, g, mod, mod, wr_hi, wr_lo, br, tri)


def _silu(g):
    return g * (1.0 / (1.0 + jnp.exp(-g)))


def _ffn_kernel(o_ref, wo_ref, x_ref, gt1_ref, gf_ref, sh2_ref, sc2_ref, wg_ref, wu_ref, wd_ref, gt2_ref,
                gkv_ref, shk_ref, sck_ref, gm_ref, shm_ref, scm_ref,
                xo_ref, hk_ref, hm_ref, x1_ref, h_ref, acc_ref):
    f = pl.program_id(1)

    @pl.when(f == 0)
    def _():
        x1 = x_ref[...] + gt1_ref[...] * _dot(o_ref[...], wo_ref[...])
        x1_ref[...] = x1
        h_ref[...] = _modulate(_rms(x1, gf_ref[...]), sh2_ref[...], sc2_ref[...]).astype(BF16)
        acc_ref[...] = jnp.zeros(acc_ref.shape, F32)

    h = h_ref[...]
    a = _silu(_dot(h, wg_ref[...])) * _dot(h, wu_ref[...])
    acc_ref[...] += _dot(a.astype(BF16), wd_ref[...])

    @pl.when(f == pl.num_programs(1) - 1)
    def _():
        x = x1_ref[...] + gt2_ref[...] * acc_ref[...]
        xo_ref[...] = x
        hk_ref[...] = _modulate(_rms(x, gkv_ref[...]), shk_ref[...], sck_ref[...]).astype(BF16)
        hm_ref[...] = _modulate(_rms(x, gm_ref[...]), shm_ref[...], scm_ref[...]).astype(BF16)


def _ffn(o, w_o, x, mod0, g_ffn, w_gu, w_down, g_kv, mod_kv, g_mix1, mod1, *, seq, tm, tf):
    n, d = x.shape
    d_ff = w_down.shape[0]
    nf = d_ff // tf
    tpb = seq // tm
    full = lambda a: pl.BlockSpec(a.shape, lambda i, f: (0,) * a.ndim)
    row = lambda w=d: pl.BlockSpec((tm, w), lambda i, f: (i, 0))
    return pl.pallas_call(
        _ffn_kernel,
        out_shape=(jax.ShapeDtypeStruct((n, d), F32), jax.ShapeDtypeStruct((n, d), BF16),
                   jax.ShapeDtypeStruct((n, d), BF16)),
        grid=(n // tm, nf),
        in_specs=[row(o.shape[1]), full(w_o), row(), _mod_spec(2, tpb, d), full(g_ffn),
                  _mod_spec(3, tpb, d), _mod_spec(4, tpb, d),
                  pl.BlockSpec((d, tf), lambda i, f: (0, f)),
                  pl.BlockSpec((d, tf), lambda i, f: (0, nf + f)),
                  pl.BlockSpec((tf, d), lambda i, f: (f, 0)),
                  _mod_spec(5, tpb, d),
                  full(g_kv), _mod_spec(0, tpb, d), _mod_spec(1, tpb, d),
                  full(g_mix1), _mod_spec(0, tpb, d), _mod_spec(1, tpb, d)],
        out_specs=(row(), row(), row()),
        scratch_shapes=[pltpu.VMEM((tm, d), F32), pltpu.VMEM((tm, d), BF16), pltpu.VMEM((tm, d), F32)],
        compiler_params=_cparams(("parallel", "arbitrary"), vmem_limit=VMEM_LIMIT_LARGE),
        name="ffn_dense",
    )(o, w_o, x, mod0, g_ffn, mod0, mod0, w_gu, w_gu, w_down, mod0,
      g_kv, mod_kv, mod_kv, g_mix1, mod1, mod1)


def _linear_kernel(x_ref, w_ref, o_ref):
    o_ref[...] = _dot(x_ref[...], w_ref[...]).astype(o_ref.dtype)


def _linear(x, w, *, tm, tn, name):
    n, k = x.shape
    m = w.shape[1]
    return pl.pallas_call(
        _linear_kernel,
        out_shape=jax.ShapeDtypeStruct((n, m), BF16),
        grid=(m // tn, n // tm),
        in_specs=[pl.BlockSpec((tm, k), lambda j, i: (i, 0)), pl.BlockSpec((k, tn), lambda j, i: (0, j))],
        out_specs=pl.BlockSpec((tm, tn), lambda j, i: (i, j)),
        compiler_params=_cparams(("parallel", "parallel")),
        name=name,
    )(x, w)


def _route_plan(counts, n_exp, tm, n_tiles):
    cnt = counts[:, 0, :n_exp].astype(jnp.int32)
    sizes = jnp.sum(cnt, axis=0)
    padded = (sizes + tm - 1) // tm * tm
    ends = jnp.cumsum(padded)
    tile_base = (ends - padded)[None, :] + jnp.cumsum(cnt, axis=0) - cnt
    tile_start = jnp.arange(n_tiles, dtype=jnp.int32) * tm
    tile_expert = jnp.minimum(jnp.sum(tile_start[:, None] >= ends[None, :], axis=1), n_exp - 1)
    n_used = (ends[-1] // tm).reshape(1)
    n_valid = jnp.clip((ends - padded + sizes)[tile_expert] - tile_start, 0, tm)
    return tile_base, tile_expert.astype(jnp.int32), n_used.astype(jnp.int32), n_valid.astype(jnp.int32)


PICK_ROWS = 8


def _picks_kernel(rt_ref, base_ref, o_ref, *, n_rows):
    rt = rt_ref[...]
    lane = lax.broadcasted_iota(jnp.int32, rt.shape, 1).astype(F32)
    cols = jnp.zeros_like(rt)
    for k in range(TOP_K):
        e_k = rt[:, k:k + 1]
        r_k = rt[:, 2 * TOP_K + k:2 * TOP_K + k + 1]
        d_k = jnp.sum(jnp.where(lane == e_k, base_ref[...], 0.0), axis=1, keepdims=True) + r_k
        for p in range(SC_ROW_PARTS):
            cols = jnp.where(lane == p * TOP_K + k, d_k + float(p * n_rows), cols)
    o_ref[...] = cols.T[:PICK_ROWS].astype(jnp.int32)


def _picks(rt, tile_base, n_rows, *, tm):
    n = rt.shape[0]
    base = _pad_last(tile_base.astype(F32), LANES).reshape(n // tm, 1, LANES)
    out = pl.pallas_call(
        functools.partial(_picks_kernel, n_rows=n_rows),
        out_shape=jax.ShapeDtypeStruct((PICK_ROWS, n), jnp.int32),
        grid=(n // tm,),
        in_specs=[pl.BlockSpec((tm, LANES), lambda i: (i, 0)),
                  pl.BlockSpec((None, 1, LANES), lambda i: (i, 0, 0))],
        out_specs=pl.BlockSpec((PICK_ROWS, tm), lambda i: (0, i)),
        compiler_params=_cparams(("parallel",)),
        name="moe_picks",
    )(rt, base)
    return out[:SC_ROW_PARTS * TOP_K].reshape(SC_ROW_PARTS, TOP_K, n)


def _sc_mesh():
    return plsc.VectorSubcoreMesh(core_axis_name="core", subcore_axis_name="subcore",
                                  num_cores=SC_CORES, num_subcores=SC_SUBCORES)


def _sc_scatter_rows(x, idx_list, n_rows):
    m, d = x.shape

    @functools.partial(pl.kernel, out_type=jax.ShapeDtypeStruct((n_rows, d), x.dtype), mesh=_sc_mesh(),
                       scratch_types=[], name="sc_scatter_rows")
    def scatter(x_hbm, *refs):
        i_hbms, o_hbm = refs[:-1], refs[-1]

        def body(x_vmem, *i_vmems):
            for i_vmem in i_vmems:
                pltpu.sync_copy(x_vmem, o_hbm.at[i_vmem.at[0]])

        pltpu.emit_pipeline(
            body,
            grid=(m // SC_WINDOW,),
            in_specs=[pl.BlockSpec((SC_WINDOW, d), index_map=lambda i: (i, 0))]
                     + [pl.BlockSpec((1, SC_WINDOW), index_map=lambda i: (0, i))] * len(idx_list),
            out_specs=[],
            core_axis_name=("core", "subcore"),
            dimension_semantics=(pltpu.PARALLEL,),
        )(x_hbm, *i_hbms)

    return scatter(x, *[idx.reshape(1, m) for idx in idx_list])


def _grouped_ffn_kernel(te_ref, nu_ref, nv_ref, x_ref, wg_ref, wu_ref, wd_ref, y_ref, hb_ref, acc_ref):
    del te_ref
    i = pl.program_id(0)
    f = pl.program_id(1)
    used = i < nu_ref[0]
    last_f = f == pl.num_programs(1) - 1

    @pl.when(used & (f == 0))
    def _():
        words = jnp.concatenate([x_ref[p] for p in range(SC_ROW_PARTS)], axis=1)
        row = lax.broadcasted_iota(jnp.int32, words.shape, 0)
        words = jnp.where(row < nv_ref[i], words, U32(0))
        hb_ref[...] = _unpack_bf16_pairs(words).astype(BF16)
        acc_ref[...] = jnp.zeros(acc_ref.shape, F32)

    @pl.when(used)
    def _():
        h = hb_ref[...]
        a = _silu(_dot(h, wg_ref[0])) * _dot(h, wu_ref[0])
        acc_ref[...] += _dot(a.astype(BF16), wd_ref[0])

    @pl.when(used & last_f)
    def _():
        y = _pack_bf16_pairs(acc_ref[...])
        for part in range(SC_ROW_PARTS):
            y_ref[part] = y[:, part * y_ref.shape[2]:(part + 1) * y_ref.shape[2]]

    @pl.when(jnp.logical_not(used) & last_f)
    def _():
        y_ref[...] = jnp.zeros(y_ref.shape, y_ref.dtype)


def _grouped_ffn(tile_expert, n_used, n_valid, xg, w_gu, w_down, *, tm, tf):
    parts, n_rows, dpp = xg.shape
    n_exp, d_ff, d = w_down.shape
    nf = d_ff // tf

    def wspec(shape, index):
        def index_map(i, f, te, nu, nv):
            return index(te[i], jnp.where(i < nu[0], f, nf - 1))
        return pl.BlockSpec(shape, index_map)

    rows = pl.BlockSpec((parts, tm, dpp), lambda i, f, te, nu, nv: (0, i, 0))
    return pl.pallas_call(
        _grouped_ffn_kernel,
        out_shape=jax.ShapeDtypeStruct(xg.shape, xg.dtype),
        grid_spec=pltpu.PrefetchScalarGridSpec(
            num_scalar_prefetch=3,
            grid=(n_rows // tm, nf),
            in_specs=[rows,
                      wspec((1, d, tf), lambda e, f: (e, 0, f)),
                      wspec((1, d, tf), lambda e, f: (e, 0, nf + f)),
                      wspec((1, tf, d), lambda e, f: (e, f, 0))],
            out_specs=rows,
            scratch_shapes=[pltpu.VMEM((tm, d), BF16), pltpu.VMEM((tm, d), F32)]),
        compiler_params=_cparams(("arbitrary", "arbitrary")),
        name="moe_grouped_ffn",
    )(tile_expert, n_used, n_valid, xg, w_gu, w_gu, w_down)


def _sc_gather_rows(table, idx):
    n_idx = idx.shape[0]
    d = table.shape[1]

    @functools.partial(pl.kernel, out_type=jax.ShapeDtypeStruct((n_idx, d), table.dtype), mesh=_sc_mesh(),
                       name="sc_gather_rows")
    def gather(t_hbm, i_hbm, o_hbm):
        def body(i_vmem, o_vmem):
            pltpu.sync_copy(t_hbm.at[i_vmem.at[0]], o_vmem)

        pltpu.emit_pipeline(
            body,
            grid=(n_idx // SC_WINDOW,),
            in_specs=[pl.BlockSpec((1, SC_WINDOW), index_map=lambda i: (0, i))],
            out_specs=[pl.BlockSpec((SC_WINDOW, d), index_map=lambda i: (i, 0))],
            core_axis_name=("core", "subcore"),
            dimension_semantics=(pltpu.PARALLEL,),
        )(i_hbm, o_hbm)

    return gather(table, idx.reshape(1, n_idx))


def _combine_kernel(x_ref, rt_ref, gt_ref, gf_ref, *refs):
    y_refs, o_ref = refs[:-1], refs[-1]
    rt = rt_ref[...]
    tot = None
    for k in range(TOP_K):
        words = jnp.concatenate([y_refs[p * TOP_K + k][...] for p in range(SC_ROW_PARTS)], axis=1)
        term = rt[:, TOP_K + k:TOP_K + k + 1] * _unpack_bf16_pairs(words)
        tot = term if tot is None else tot + term
    x = x_ref[...] + gt_ref[...] * tot
    o_ref[...] = _rms(x, gf_ref[...])


def _combine(x, rt, mod1, g_final, ysel, *, seq, tm):
    n, d = x.shape
    tpb = seq // tm
    nt = n // tm
    row = lambda w: pl.BlockSpec((tm, w), lambda i: (i, 0))
    piece = lambda j: pl.BlockSpec((tm, ysel.shape[1]), lambda i: (j * nt + i, 0))
    n_pieces = SC_ROW_PARTS * TOP_K
    return pl.pallas_call(
        _combine_kernel,
        out_shape=jax.ShapeDtypeStruct((n, d), F32),
        grid=(nt,),
        in_specs=[row(d), row(LANES), _mod_spec(5, tpb, d), pl.BlockSpec(g_final.shape, lambda i: (0, 0))]
                 + [piece(j) for j in range(n_pieces)],
        out_specs=row(d),
        compiler_params=_cparams(("parallel",)),
        name="moe_combine",
    )(x, rt, mod1, g_final, *([ysel] * n_pieces))


def _pad_last(a, width):
    return jnp.pad(a, [(0, 0)] * (a.ndim - 1) + [(0, width - a.shape[-1])])


def kernel(x, c, positions, w_mod, b_mod, g_mix, g_ffn, w_a_down, g_q_lat, g_kv_lat, w_uq, w_ukv, w_oa,
           w_mod_kv, b_mod_kv, g_kv, w_kv_sb, w_q_sb, w_o_sb, w_ffn_gu, w_ffn_down, w_router, b_router,
           w_exp_gu, w_exp_down, g_final):
    bsz, seq, d = x.shape
    n = bsz * seq
    q_lora, kv_lora = g_q_lat.shape[1], g_kv_lat.shape[1]
    n_exp = w_router.shape[-1]
    d_ff = w_ffn_down.shape[1]
    tm = min(TOKEN_TILE, seq)
    t_mla = min(MLA_TILE, seq)
    t_sb = min(SB_TILE, seq)
    tf = next((t for t in (FFN_CHUNK, 512) if d_ff % t == 0), d_ff)

    mod0 = _modvec(c, w_mod[0], b_mod[0]).reshape(bsz, 6, 1, d)
    mod1 = _modvec(c, w_mod[1], b_mod[1]).reshape(bsz, 6, 1, d)
    mod_kv = _modvec(c, w_mod_kv, b_mod_kv).reshape(bsz, 2, 1, d)

    lat_w = q_lora + kv_lora + LANES
    wd = _pad_last(w_a_down[0], lat_w).astype(BF16)
    wq = w_uq[0].reshape(q_lora, MLA_HEADS, QK_NOPE + QK_ROPE)
    wqn = wq[:, :, :QK_NOPE].reshape(q_lora, MLA_HEADS * QK_NOPE).astype(BF16)
    wqr = _pad_last(wq[:, :, QK_NOPE:], LANES).reshape(q_lora, MLA_HEADS * LANES).astype(BF16)
    wkv = w_ukv[0].reshape(kv_lora, MLA_HEADS, QK_NOPE + V_DIM)
    wkn = wkv[:, :, :QK_NOPE].reshape(kv_lora, MLA_HEADS * QK_NOPE).astype(BF16)
    wv = wkv[:, :, QK_NOPE:].reshape(kv_lora, MLA_HEADS * V_DIM).astype(BF16)
    half = QK_ROPE // 2
    inv = ROPE_THETA ** (-jnp.arange(half, dtype=F32) / half)
    inv = jnp.tile(inv, ROPE_PACK).reshape(1, LANES)
    wr = _pad_last(w_router[0], LANES)
    wr_hi = wr.astype(BF16)
    wr_lo = (wr - wr_hi.astype(F32)).astype(BF16)
    br = _pad_last(b_router[0], LANES).reshape(1, LANES)

    xf = x.reshape(n, d)
    pos = positions.reshape(n // tm, ROPE_PACK, tm // ROPE_PACK).swapaxes(1, 2)
    pos = jnp.repeat(pos.reshape(n // ROPE_PACK, ROPE_PACK), QK_ROPE // 2, axis=1)
    row1 = lambda a: a.reshape(1, -1)

    q, k, v = _mla_proj(xf, pos, mod0, row1(g_mix[0]), wd, row1(g_q_lat[0]), row1(g_kv_lat[0]),
                        wqn, wqr, wkn, wv, inv, seq=seq, tm=tm)
    o = _mla_attn(q.reshape(bsz, seq, -1), k.reshape(bsz, seq, -1), v.reshape(bsz, seq, -1), t=t_mla)
    x2, hk, hm = _ffn(o.reshape(n, -1), w_oa[0].astype(BF16), xf, mod0, row1(g_ffn[0]),
                      w_ffn_gu[0].astype(BF16), w_ffn_down[0].astype(BF16),
                      row1(g_kv), mod_kv, row1(g_mix[1]), mod1, seq=seq, tm=tm, tf=tf)
    kv = _linear(hk, w_kv_sb.astype(BF16), tm=tm, tn=w_kv_sb.shape[1], name="kv_proj")
    q_scale = math.log2(math.e) / math.sqrt(SB_HEAD_DIM)
    qs = _linear(hm, (w_q_sb[0] * q_scale).astype(BF16), tm=tm, tn=1024, name="q_proj")
    o = _sb_attn(qs.reshape(bsz, seq, -1), kv.reshape(bsz, seq, -1), t=t_sb)
    x3, h, rt, counts = _out_proj_router(o.reshape(n, -1), w_o_sb[0].astype(BF16), x2, mod1, row1(g_ffn[1]),
                                         wr_hi, wr_lo, br, n_exp, seq=seq, tm=tm)
    n_tiles = TOP_K * n // tm + n_exp
    n_rows = n_tiles * tm
    tile_base, tile_expert, n_used, n_valid = _route_plan(counts, n_exp, tm, n_tiles)
    picks = _picks(rt, tile_base, n_rows, tm=tm)
    xg = _sc_scatter_rows(h.reshape(SC_ROW_PARTS * n, -1), [picks[:, k].reshape(-1) for k in range(TOP_K)],
                          SC_ROW_PARTS * n_rows)
    yg = _grouped_ffn(tile_expert, n_used, n_valid, xg.reshape(SC_ROW_PARTS, n_rows, -1),
                      w_exp_gu[0].astype(BF16), w_exp_down[0].astype(BF16), tm=tm, tf=tf)
    ysel = _sc_gather_rows(yg.reshape(SC_ROW_PARTS * n_rows, -1), picks.reshape(-1))
    out = _combine(x3, rt, mod1, row1(g_final), ysel, seq=seq, tm=tm)
    return out.reshape(bsz, seq, d)
```

```python
import functools
import math

import jax
import jax.numpy as jnp
from jax import lax
from jax.experimental import pallas as pl
from jax.experimental.pallas import tpu as pltpu
from jax.experimental.pallas import tpu_sc as plsc

F32 = jnp.float32
BF16 = jnp.bfloat16

EPS = 1e-6
MLA_HEADS = 8
QK_NOPE = 128
QK_ROPE = 64
V_DIM = 128
ROPE_THETA = 10000.0
SB_HEADS = 8
SB_HEAD_DIM = 128
TOP_K = 2

LANES = 128
QK_PAD = 256
ROPE_PACK = LANES // (QK_ROPE // 2)
VMEM_LIMIT = 48 * 1024 * 1024
VMEM_LIMIT_LARGE = 58 * 1024 * 1024
TOKEN_TILE = 512
MLA_TILE = 512
MLA_BLOCKS_PER_TRIP = 4
SB_TILE = 512
SB_BLOCKS_PER_TRIP = 2
SB_ZERO_LOG2 = -140.0
FFN_CHUNK = 1792
SC_CORES = 2
SC_SUBCORES = 16
SC_WINDOW = 128
SC_ROW_PARTS = 2


def _cparams(sem, vmem_limit=VMEM_LIMIT):
    return pltpu.CompilerParams(dimension_semantics=sem, vmem_limit_bytes=vmem_limit)


def _rms(x, g):
    return x * lax.rsqrt(jnp.mean(x * x, axis=-1, keepdims=True) + EPS) * g


def _modulate(h, shift, scale):
    return h * (1.0 + scale) + shift


def _split_bf16(a):
    hi = a.astype(BF16)
    lo = (a - hi.astype(F32)).astype(BF16)
    return hi, lo


U32 = jnp.uint32
HIGH_HALF = 0xFFFF0000


def _pack_bf16_pairs(a):
    half = a.shape[1] // 2
    bits = lambda v: lax.bitcast_convert_type(v.astype(BF16).astype(F32), U32)
    return (bits(a[:, :half]) >> 16) | (bits(a[:, half:]) & U32(HIGH_HALF))


def _unpack_bf16_pairs(w):
    lo = lax.bitcast_convert_type(w << 16, F32)
    hi = lax.bitcast_convert_type(w & U32(HIGH_HALF), F32)
    return jnp.concatenate([lo, hi], axis=1)


def _dot(a, b):
    return jnp.dot(a, b, preferred_element_type=F32)


def _dot_nt(a, b):
    return lax.dot_general(a, b, (((1,), (1,)), ((), ())), preferred_element_type=F32)


def _modvec_kernel(c_ref, w_ref, b_ref, o_ref):
    c = c_ref[...]
    sc = c * (1.0 / (1.0 + jnp.exp(-c)))
    a_hi, a_lo = _split_bf16(sc)
    w_hi, w_lo = _split_bf16(w_ref[...])
    o_ref[...] = _dot(a_hi, w_hi) + _dot(a_lo, w_hi) + _dot(a_hi, w_lo) + b_ref[...]


def _modvec(c, w, b, tn=512):
    bsz, d = c.shape
    n = w.shape[1]
    return pl.pallas_call(
        _modvec_kernel,
        out_shape=jax.ShapeDtypeStruct((bsz, n), F32),
        grid=(n // tn,),
        in_specs=[pl.BlockSpec((bsz, d), lambda j: (0, 0)),
                  pl.BlockSpec((d, tn), lambda j: (0, j)),
                  pl.BlockSpec((1, tn), lambda j: (0, j))],
        out_specs=pl.BlockSpec((bsz, tn), lambda j: (0, j)),
        compiler_params=_cparams(("arbitrary",)),
        name="modvec",
    )(c, w, b.reshape(1, n))


def _mod_spec(chunk, tiles_per_batch, d):
    return pl.BlockSpec((None, None, 1, d), lambda i, *_: (i // tiles_per_batch, chunk, 0, 0))


def _mla_proj_kernel(x_ref, pos_ref, sh_ref, sc_ref, g_ref, wd_ref, gq_ref, gkv_ref,
                     wqn_ref, wqr_ref, wkn_ref, wv_ref, inv_ref,
                     q_ref, k_ref, v_ref, *, q_lora, kv_lora):
    x = x_ref[...]
    h = _modulate(_rms(x, g_ref[...]), sh_ref[...], sc_ref[...]).astype(BF16)
    lat = _dot(h, wd_ref[...])
    c_q = _rms(lat[:, :q_lora], gq_ref[...]).astype(BF16)
    c_kv = _rms(lat[:, q_lora:q_lora + kv_lora], gkv_ref[...]).astype(BF16)
    k_rot = lat[:, q_lora + kv_lora:]

    half = QK_ROPE // 2
    ang = pos_ref[...].astype(F32) * inv_ref[...]
    cos_p = jnp.cos(ang)
    sin_p = jnp.sin(ang)
    lane = lax.broadcasted_iota(jnp.int32, ang.shape, 1)
    cos, s_a, s_b = [], [], []
    for g in range(ROPE_PACK):
        shift = (LANES - g * half) % LANES
        cg = pltpu.roll(cos_p, shift, axis=1) if shift else cos_p
        sg = pltpu.roll(sin_p, shift, axis=1) if shift else sin_p
        cos.append(jnp.where(lane < half, cg, pltpu.roll(cg, half, axis=1)))
        s_a.append(jnp.where(lane < half, -sg, 0.0))
        s_b.append(jnp.where((lane >= half) & (lane < 2 * half), pltpu.roll(sg, half, axis=1), 0.0))
    cos, s_a, s_b = (jnp.concatenate(v, axis=0) for v in (cos, s_a, s_b))

    def rope(r):
        return (r * cos + pltpu.roll(r, LANES - half, axis=1) * s_a
                + pltpu.roll(r, half, axis=1) * s_b)

    k_rot = rope(k_rot).astype(BF16)
    q_nope = _dot(c_q, wqn_ref[...])
    q_rope = _dot(c_q, wqr_ref[...])
    k_nope = _dot(c_kv, wkn_ref[...])
    v_ref[...] = _dot(c_kv, wv_ref[...]).astype(BF16)
    for hd in range(MLA_HEADS):
        a, b = hd * LANES, (hd + 1) * LANES
        q_ref[:, hd * QK_PAD:hd * QK_PAD + LANES] = q_nope[:, a:b].astype(BF16)
        q_ref[:, hd * QK_PAD + LANES:(hd + 1) * QK_PAD] = rope(q_rope[:, a:b]).astype(BF16)
        k_ref[:, hd * QK_PAD:hd * QK_PAD + LANES] = k_nope[:, a:b].astype(BF16)
        k_ref[:, hd * QK_PAD + LANES:(hd + 1) * QK_PAD] = k_rot


def _mla_proj(x, pos, mod, g_mix, wd, gq, gkv, wqn, wqr, wkn, wv, inv, *, seq, tm):
    n, d = x.shape
    tpb = seq // tm
    q_lora, kv_lora = gq.shape[1], gkv.shape[1]
    hq = MLA_HEADS * QK_PAD
    hv = MLA_HEADS * V_DIM
    full = lambda a: pl.BlockSpec(a.shape, lambda i: (0,) * a.ndim)
    row = lambda w: pl.BlockSpec((tm, w), lambda i: (i, 0))
    return pl.pallas_call(
        functools.partial(_mla_proj_kernel, q_lora=q_lora, kv_lora=kv_lora),
        out_shape=(jax.ShapeDtypeStruct((n, hq), BF16), jax.ShapeDtypeStruct((n, hq), BF16),
                   jax.ShapeDtypeStruct((n, hv), BF16)),
        grid=(n // tm,),
        in_specs=[row(d), pl.BlockSpec((tm // ROPE_PACK, LANES), lambda i: (i, 0)), _mod_spec(0, tpb, d), _mod_spec(1, tpb, d), full(g_mix), full(wd),
                  full(gq), full(gkv), full(wqn), full(wqr), full(wkn), full(wv), full(inv)],
        out_specs=(row(hq), row(hq), row(hv)),
        compiler_params=_cparams(("parallel",)),
        name="mla_proj",
    )(x, pos, mod, mod, g_mix, wd, gq, gkv, wqn, wqr, wkn, wv, inv)


def _lane_chunks(a):
    return [a[:, c * LANES:(c + 1) * LANES] for c in range(a.shape[1] // LANES)]


def _for_blocks(n, step, group):
    def trip(p, carry):
        step(group * p, group)
        return carry

    lax.fori_loop(0, n // group, trip, 0)
    size = group // 2
    while size:
        @pl.when((n // size) % 2 == 1)
        def _(size=size):
            step(n // (2 * size) * (2 * size), size)
        size //= 2


def _mla_attn_kernel(q_ref, k_ref, v_ref, o_ref, s_ref, m_ref, acc_ref, *, t, scale, heads):
    i = pl.program_id(2)

    def lane_max(s):
        m = None
        for sc in _lane_chunks(s):
            m = sc if m is None else jnp.maximum(m, sc)
        return m

    def scores(h, j):
        start = pl.multiple_of(j * t, t)
        hs = slice(h * QK_PAD, (h + 1) * QK_PAD)
        return _dot_nt(q_ref[0, :, hs], k_ref[0, pl.ds(start, t), hs])

    r_minus_c = (lax.broadcasted_iota(jnp.int32, (t, t), 0) - lax.broadcasted_iota(jnp.int32, (t, t), 1))
    m_ref[...] = jnp.full(m_ref.shape, -jnp.inf, F32)

    def pass1(j, nb):
        for h in range(heads):
            m = m_ref[h]
            for jj in range(nb):
                s = jnp.where(r_minus_c >= (j + jj - i) * t, scores(h, j + jj), -jnp.inf)
                s_ref[h, j + jj] = s
                m = jnp.maximum(m, lane_max(s))
            m_ref[h] = m

    _for_blocks(i + 1, pass1, MLA_BLOCKS_PER_TRIP)
    for h in range(heads):
        m_ref[h] = jnp.broadcast_to(jnp.max(m_ref[h], axis=1, keepdims=True), (t, LANES))
    acc_ref[...] = jnp.zeros(acc_ref.shape, F32)
    cst = scale * math.log2(math.e)

    def pass2(j, nb):
        start = pl.multiple_of(j * t, t)
        ones = jnp.ones((nb * t, LANES), BF16)
        for h in range(heads):
            m = m_ref[h]
            p = jnp.concatenate([jnp.exp2((sc - m) * cst).astype(BF16)
                                 for jj in range(nb) for sc in _lane_chunks(s_ref[h, j + jj])], axis=1)
            v_ext = jnp.concatenate([v_ref[0, pl.ds(start, nb * t), h * V_DIM:(h + 1) * V_DIM], ones], axis=1)
            acc_ref[h] += _dot(p, v_ext)

    _for_blocks(i + 1, pass2, MLA_BLOCKS_PER_TRIP)
    for h in range(heads):
        acc = acc_ref[h]
        o_ref[0, :, h * V_DIM:(h + 1) * V_DIM] = (acc[:, :V_DIM] / acc[:, V_DIM:]).astype(o_ref.dtype)


def _mla_attn(q, k, v, *, t, heads=2):
    bsz, seq, _ = q.shape
    scale = 1.0 / math.sqrt(QK_NOPE + QK_ROPE)
    wq, wv = heads * QK_PAD, heads * V_DIM
    return pl.pallas_call(
        functools.partial(_mla_attn_kernel, t=t, scale=scale, heads=heads),
        out_shape=jax.ShapeDtypeStruct((bsz, seq, MLA_HEADS * V_DIM), BF16),
        grid=(bsz, MLA_HEADS // heads, seq // t),
        in_specs=[pl.BlockSpec((1, t, wq), lambda b, g, i: (b, i, g)),
                  pl.BlockSpec((1, seq, wq), lambda b, g, i: (b, 0, g)),
                  pl.BlockSpec((1, seq, wv), lambda b, g, i: (b, 0, g))],
        out_specs=pl.BlockSpec((1, t, wv), lambda b, g, i: (b, i, g)),
        scratch_shapes=[pltpu.VMEM((heads, seq // t, t, t), F32), pltpu.VMEM((heads, t, LANES), F32),
                        pltpu.VMEM((heads, t, V_DIM + LANES), F32)],
        compiler_params=_cparams(("parallel", "parallel", "arbitrary")),
        name="mla_attn",
    )(q, k, v)


def _sb_attn_kernel(q_ref, k_ref, v_ref, w_ref, o_ref, c_ref, acc_ref, kmax_ref, bound_ref, *, t, heads):
    i = pl.program_id(2)
    d = SB_HEAD_DIM
    c_ref[...] = jnp.zeros(c_ref.shape, F32)
    acc_ref[...] = jnp.zeros(acc_ref.shape, F32)
    sub = t // 2 if t % (2 * LANES) == 0 else t
    r_iota = lax.broadcasted_iota(jnp.int32, (sub, LANES), 0)
    c_iota = lax.broadcasted_iota(jnp.int32, (sub, LANES), 1)

    def block(r0, nr, start, nk, key0):
        rows = slice(r0, r0 + nr)
        masks = []
        for cc in range(nk // LANES):
            off = None if key0 is None else key0 + cc * LANES - r0
            masks.append(None if off is None or off <= -LANES else c_iota + off < r_iota)
        for h in range(heads):
            hs = slice(h * d, (h + 1) * d)
            zz = _dot_nt(q_ref[0, rows, hs], k_ref[0, pl.ds(start, nk), hs])
            zneg = -zz
            sp = jnp.log2(1.0 + jnp.exp2(jnp.minimum(zz, zneg)))
            log_1m = jnp.minimum(zneg, 0.0) - sp
            carry = c_ref[h, rows]
            a_chunks = [None] * (nk // LANES)
            for cc in reversed(range(nk // LANES)):
                cs = slice(cc * LANES, (cc + 1) * LANES)
                l1m = log_1m[:, cs] if masks[cc] is None else jnp.where(masks[cc], log_1m[:, cs], 0.0)
                hi, lo = _split_bf16(l1m)
                y = _dot(jnp.concatenate([hi, lo], axis=1), w_ref[...])
                a = jnp.exp2(zz[:, cs] + y[:, :LANES] + carry)
                carry = carry + y[:, LANES:]
                if masks[cc] is not None:
                    a = jnp.where(masks[cc], a, 0.0)
                a_chunks[cc] = a.astype(BF16)
            c_ref[h, rows] = carry
            acc_ref[h, rows] += _dot(jnp.concatenate(a_chunks, axis=1), v_ref[0, pl.ds(start, nk), hs])

    def diagonal_tile():
        for band in range(t // sub):
            block(band * sub, sub, pl.multiple_of(i * t, t), (band + 1) * sub, 0)

    def left_blocks(n, nb):
        for jj in range(nb):
            block(0, t, pl.multiple_of((i - 1 - n - jj) * t, t), t, None)

    @pl.when(i == 0)
    def _():
        diagonal_tile()

    @pl.when(i > 0)
    def _():
        diagonal_tile()
        left_blocks(0, 1)

    @pl.when(i == 0)
    def _():
        for h in range(heads):
            kmax = jnp.max(jnp.abs(k_ref[0, :, h * d:(h + 1) * d].astype(F32)))
            kmax_ref[h] = jnp.full(kmax_ref.shape[1:], kmax, F32)

    for h in range(heads):
        q_l1 = jnp.sum(jnp.abs(q_ref[0, :, h * d:(h + 1) * d].astype(F32)), axis=1, keepdims=True)
        bound_ref[h] = q_l1 * kmax_ref[h, 0:1, :]

    def more_to_add():
        worst = None
        for h in range(heads):
            top = jnp.max(c_ref[h] + bound_ref[h])
            worst = top if worst is None else jnp.maximum(worst, top)
        return worst >= SB_ZERO_LOG2

    def trip(state):
        n, _ = state
        left_blocks(n, SB_BLOCKS_PER_TRIP)
        return n + SB_BLOCKS_PER_TRIP, more_to_add()

    n, go = lax.while_loop(lambda s: s[1] & (s[0] + SB_BLOCKS_PER_TRIP <= i), trip,
                           (jnp.int32(1), (i > 0) & more_to_add()))
    size = SB_BLOCKS_PER_TRIP // 2
    while size:
        take = go & (n + size <= i)

        @pl.when(take)
        def _(n=n, size=size):
            left_blocks(n, size)

        n = n + jnp.where(take, size, 0)
        size //= 2
    for h in range(heads):
        o_ref[0, :, h * d:(h + 1) * d] = acc_ref[h].astype(o_ref.dtype)


def _sb_attn(q, kv, *, t, heads=4):
    bsz, seq, _ = q.shape
    w = heads * SB_HEAD_DIM
    groups = SB_HEADS // heads
    tri = (jnp.arange(LANES)[:, None] >= jnp.arange(LANES)[None, :]).astype(BF16)
    half = jnp.concatenate([tri, jnp.ones((LANES, LANES), BF16)], axis=1)
    w_sum = jnp.concatenate([half, half], axis=0)
    return pl.pallas_call(
        functools.partial(_sb_attn_kernel, t=t, heads=heads),
        out_shape=jax.ShapeDtypeStruct((bsz, seq, SB_HEADS * SB_HEAD_DIM), BF16),
        grid=(bsz, groups, seq // t),
        in_specs=[pl.BlockSpec((1, t, w), lambda b, g, i: (b, i, g)),
                  pl.BlockSpec((1, seq, w), lambda b, g, i: (b, 0, g)),
                  pl.BlockSpec((1, seq, w), lambda b, g, i: (b, 0, groups + g)),
                  pl.BlockSpec(w_sum.shape, lambda b, g, i: (0, 0))],
        out_specs=pl.BlockSpec((1, t, w), lambda b, g, i: (b, i, g)),
        scratch_shapes=[pltpu.VMEM((heads, t, LANES), F32), pltpu.VMEM((heads, t, SB_HEAD_DIM), F32),
                        pltpu.VMEM((heads, 8, LANES), F32), pltpu.VMEM((heads, t, LANES), F32)],
        compiler_params=_cparams(("parallel", "parallel", "arbitrary")),
        name="sb_attn",
    )(q, kv, kv, w_sum)


def _out_proj_kernel(o_ref, w_ref, x_ref, gt_ref, g_ref, sh_ref, sc_ref, wr_hi_ref, wr_lo_ref, br_ref, tri_ref,
                     xo_ref, h_ref, rt_ref, cnt_ref, *, n_experts):
    x = x_ref[...] + gt_ref[...] * _dot(o_ref[...], w_ref[...])
    xo_ref[...] = x
    h = _modulate(_rms(x, g_ref[...]), sh_ref[...], sc_ref[...])
    h_hi = h.astype(BF16)
    words = _pack_bf16_pairs(h)
    for part in range(SC_ROW_PARTS):
        h_ref[part] = words[:, part * h_ref.shape[2]:(part + 1) * h_ref.shape[2]]
    h_lo = (h - h_hi.astype(F32)).astype(BF16)
    logits = (_dot(h_hi, wr_hi_ref[...]) + _dot(h_lo, wr_hi_ref[...]) + _dot(h_hi, wr_lo_ref[...])
              + br_ref[...])
    lane = lax.broadcasted_iota(jnp.int32, logits.shape, 1).astype(F32)
    lg = jnp.where(lane < n_experts, logits, -jnp.inf)
    m1 = jnp.max(lg, axis=1, keepdims=True)
    i1 = jnp.min(jnp.where(lg == m1, lane, float(LANES)), axis=1, keepdims=True)
    lg2 = jnp.where(lane == i1, -jnp.inf, lg)
    m2 = jnp.max(lg2, axis=1, keepdims=True)
    i2 = jnp.min(jnp.where(lg2 == m2, lane, float(LANES)), axis=1, keepdims=True)
    e2 = jnp.exp(m2 - m1)
    den = 1.0 + e2
    sel = jnp.where((lane == i1) | (lane == i2), 1.0, 0.0)
    prefix = _dot(tri_ref[...], sel.astype(BF16))
    r1 = jnp.sum(jnp.where(lane == i1, prefix, 0.0), axis=1, keepdims=True)
    r2 = jnp.sum(jnp.where(lane == i2, prefix, 0.0), axis=1, keepdims=True)
    cnt_ref[...] = jnp.sum(sel, axis=0, keepdims=True)
    rt = jnp.zeros_like(logits)
    for k, val in enumerate((i1, i2, 1.0 / den, e2 / den, r1, r2)):
        rt = jnp.where(lane == k, val, rt)
    rt_ref[...] = rt


def _out_proj_router(o, w, x, mod, g, wr_hi, wr_lo, br, n_experts, *, seq, tm):
    n, d = x.shape
    tpb = seq // tm
    dpp = d // 2 // SC_ROW_PARTS
    tri = (jnp.arange(tm)[:, None] > jnp.arange(tm)[None, :]).astype(BF16)
    full = lambda a: pl.BlockSpec(a.shape, lambda i: (0,) * a.ndim)
    row = lambda wd: pl.BlockSpec((tm, wd), lambda i: (i, 0))
    return pl.pallas_call(
        functools.partial(_out_proj_kernel, n_experts=n_experts),
        out_shape=(jax.ShapeDtypeStruct((n, d), F32), jax.ShapeDtypeStruct((SC_ROW_PARTS, n, dpp), U32),
                   jax.ShapeDtypeStruct((n, LANES), F32), jax.ShapeDtypeStruct((n // tm, 1, LANES), F32)),
        grid=(n // tm,),
        in_specs=[row(o.shape[1]), full(w), row(d), _mod_spec(2, tpb, d), full(g),
                  _mod_spec(3, tpb, d), _mod_spec(4, tpb, d), full(wr_hi), full(wr_lo), full(br), full(tri)],
        out_specs=(row(d), pl.BlockSpec((SC_ROW_PARTS, tm, dpp), lambda i: (0, i, 0)), row(LANES),
                   pl.BlockSpec((None, 1, LANES), lambda i: (i, 0, 0))),
        compiler_params=_cparams(("parallel",)),
        name="out_proj_router",
    )(o, w, x, mod, g, mod, mod, wr_hi, wr_lo, br, tri)


def _silu(g):
    return g * (1.0 / (1.0 + jnp.exp(-g)))


def _ffn_kernel(o_ref, wo_ref, x_ref, gt1_ref, gf_ref, sh2_ref, sc2_ref, wg_ref, wu_ref, wd_ref, gt2_ref,
                gkv_ref, shk_ref, sck_ref, gm_ref, shm_ref, scm_ref,
                xo_ref, hk_ref, hm_ref, x1_ref, h_ref, acc_ref):
    f = pl.program_id(1)

    @pl.when(f == 0)
    def _():
        x1 = x_ref[...] + gt1_ref[...] * _dot(o_ref[...], wo_ref[...])
        x1_ref[...] = x1
        h_ref[...] = _modulate(_rms(x1, gf_ref[...]), sh2_ref[...], sc2_ref[...]).astype(BF16)
        acc_ref[...] = jnp.zeros(acc_ref.shape, F32)

    h = h_ref[...]
    a = _silu(_dot(h, wg_ref[...])) * _dot(h, wu_ref[...])
    acc_ref[...] += _dot(a.astype(BF16), wd_ref[...])

    @pl.when(f == pl.num_programs(1) - 1)
    def _():
        x = x1_ref[...] + gt2_ref[...] * acc_ref[...]
        xo_ref[...] = x
        y = x * lax.rsqrt(jnp.mean(x * x, axis=-1, keepdims=True) + EPS)
        hk_ref[...] = _modulate(y * gkv_ref[...], shk_ref[...], sck_ref[...]).astype(BF16)
        hm_ref[...] = _modulate(y * gm_ref[...], shm_ref[...], scm_ref[...]).astype(BF16)


def _ffn(o, w_o, x, mod0, g_ffn, w_gu, w_down, g_kv, mod_kv, g_mix1, mod1, *, seq, tm, tf):
    n, d = x.shape
    d_ff = w_down.shape[0]
    nf = d_ff // tf
    tpb = seq // tm
    full = lambda a: pl.BlockSpec(a.shape, lambda i, f: (0,) * a.ndim)
    row = lambda w=d: pl.BlockSpec((tm, w), lambda i, f: (i, 0))
    return pl.pallas_call(
        _ffn_kernel,
        out_shape=(jax.ShapeDtypeStruct((n, d), F32), jax.ShapeDtypeStruct((n, d), BF16),
                   jax.ShapeDtypeStruct((n, d), BF16)),
        grid=(n // tm, nf),
        in_specs=[row(o.shape[1]), full(w_o), row(), _mod_spec(2, tpb, d), full(g_ffn),
                  _mod_spec(3, tpb, d), _mod_spec(4, tpb, d),
                  pl.BlockSpec((d, tf), lambda i, f: (0, f)),
                  pl.BlockSpec((d, tf), lambda i, f: (0, nf + f)),
                  pl.BlockSpec((tf, d), lambda i, f: (f, 0)),
                  _mod_spec(5, tpb, d),
                  full(g_kv), _mod_spec(0, tpb, d), _mod_spec(1, tpb, d),
                  full(g_mix1), _mod_spec(0, tpb, d), _mod_spec(1, tpb, d)],
        out_specs=(row(), row(), row()),
        scratch_shapes=[pltpu.VMEM((tm, d), F32), pltpu.VMEM((tm, d), BF16), pltpu.VMEM((tm, d), F32)],
        compiler_params=_cparams(("parallel", "arbitrary"), vmem_limit=VMEM_LIMIT_LARGE),
        name="ffn_dense",
    )(o, w_o, x, mod0, g_ffn, mod0, mod0, w_gu, w_gu, w_down, mod0,
      g_kv, mod_kv, mod_kv, g_mix1, mod1, mod1)


def _linear_kernel(x_ref, w_ref, o_ref):
    o_ref[...] = _dot(x_ref[...], w_ref[...]).astype(o_ref.dtype)


def _linear(x, w, *, tm, tn, name):
    n, k = x.shape
    m = w.shape[1]
    return pl.pallas_call(
        _linear_kernel,
        out_shape=jax.ShapeDtypeStruct((n, m), BF16),
        grid=(m // tn, n // tm),
        in_specs=[pl.BlockSpec((tm, k), lambda j, i: (i, 0)), pl.BlockSpec((k, tn), lambda j, i: (0, j))],
        out_specs=pl.BlockSpec((tm, tn), lambda j, i: (i, j)),
        compiler_params=_cparams(("parallel", "parallel")),
        name=name,
    )(x, w)


def _route_plan(counts, n_exp, tm, n_tiles):
    cnt = counts[:, 0, :n_exp].astype(jnp.int32)
    sizes = jnp.sum(cnt, axis=0)
    padded = (sizes + tm - 1) // tm * tm
    ends = jnp.cumsum(padded)
    tile_base = (ends - padded)[None, :] + jnp.cumsum(cnt, axis=0) - cnt
    tile_start = jnp.arange(n_tiles, dtype=jnp.int32) * tm
    tile_expert = jnp.minimum(jnp.sum(tile_start[:, None] >= ends[None, :], axis=1), n_exp - 1)
    n_used = (ends[-1] // tm).reshape(1)
    n_valid = jnp.clip((ends - padded + sizes)[tile_expert] - tile_start, 0, tm)
    return tile_base, tile_expert.astype(jnp.int32), n_used.astype(jnp.int32), n_valid.astype(jnp.int32)


PICK_ROWS = 8


def _picks_kernel(rt_ref, base_ref, o_ref, *, n_rows):
    rt = rt_ref[...]
    lane = lax.broadcasted_iota(jnp.int32, rt.shape, 1).astype(F32)
    cols = jnp.zeros_like(rt)
    for k in range(TOP_K):
        e_k = rt[:, k:k + 1]
        r_k = rt[:, 2 * TOP_K + k:2 * TOP_K + k + 1]
        d_k = jnp.sum(jnp.where(lane == e_k, base_ref[...], 0.0), axis=1, keepdims=True) + r_k
        for p in range(SC_ROW_PARTS):
            cols = jnp.where(lane == p * TOP_K + k, d_k + float(p * n_rows), cols)
    o_ref[...] = cols.T[:PICK_ROWS].astype(jnp.int32)


def _picks(rt, tile_base, n_rows, *, tm):
    n = rt.shape[0]
    base = _pad_last(tile_base.astype(F32), LANES).reshape(n // tm, 1, LANES)
    out = pl.pallas_call(
        functools.partial(_picks_kernel, n_rows=n_rows),
        out_shape=jax.ShapeDtypeStruct((PICK_ROWS, n), jnp.int32),
        grid=(n // tm,),
        in_specs=[pl.BlockSpec((tm, LANES), lambda i: (i, 0)),
                  pl.BlockSpec((None, 1, LANES), lambda i: (i, 0, 0))],
        out_specs=pl.BlockSpec((PICK_ROWS, tm), lambda i: (0, i)),
        compiler_params=_cparams(("parallel",)),
        name="moe_picks",
    )(rt, base)
    return out[:SC_ROW_PARTS * TOP_K].reshape(SC_ROW_PARTS, TOP_K, n)


def _sc_mesh():
    return plsc.VectorSubcoreMesh(core_axis_name="core", subcore_axis_name="subcore",
                                  num_cores=SC_CORES, num_subcores=SC_SUBCORES)


def _sc_scatter_rows(x, idx_list, n_rows):
    m, d = x.shape

    @functools.partial(pl.kernel, out_type=jax.ShapeDtypeStruct((n_rows, d), x.dtype), mesh=_sc_mesh(),
                       scratch_types=[], name="sc_scatter_rows")
    def scatter(x_hbm, *refs):
        i_hbms, o_hbm = refs[:-1], refs[-1]

        def body(x_vmem, *i_vmems):
            for i_vmem in i_vmems:
                pltpu.sync_copy(x_vmem, o_hbm.at[i_vmem.at[0]])

        pltpu.emit_pipeline(
            body,
            grid=(m // SC_WINDOW,),
            in_specs=[pl.BlockSpec((SC_WINDOW, d), index_map=lambda i: (i, 0))]
                     + [pl.BlockSpec((1, SC_WINDOW), index_map=lambda i: (0, i))] * len(idx_list),
            out_specs=[],
            core_axis_name=("core", "subcore"),
            dimension_semantics=(pltpu.PARALLEL,),
        )(x_hbm, *i_hbms)

    return scatter(x, *[idx.reshape(1, m) for idx in idx_list])


def _grouped_ffn_kernel(te_ref, nu_ref, nv_ref, x_ref, wg_ref, wu_ref, wd_ref, y_ref, hb_ref, acc_ref):
    del te_ref
    i = pl.program_id(0)
    f = pl.program_id(1)
    used = i < nu_ref[0]
    last_f = f == pl.num_programs(1) - 1

    @pl.when(used & (f == 0))
    def _():
        words = jnp.concatenate([x_ref[p] for p in range(SC_ROW_PARTS)], axis=1)
        row = lax.broadcasted_iota(jnp.int32, words.shape, 0)
        words = jnp.where(row < nv_ref[i], words, U32(0))
        hb_ref[...] = _unpack_bf16_pairs(words).astype(BF16)
        acc_ref[...] = jnp.zeros(acc_ref.shape, F32)

    @pl.when(used)
    def _():
        h = hb_ref[...]
        a = _silu(_dot(h, wg_ref[0])) * _dot(h, wu_ref[0])
        acc_ref[...] += _dot(a.astype(BF16), wd_ref[0])

    @pl.when(used & last_f)
    def _():
        y = _pack_bf16_pairs(acc_ref[...])
        for part in range(SC_ROW_PARTS):
            y_ref[part] = y[:, part * y_ref.shape[2]:(part + 1) * y_ref.shape[2]]

    @pl.when(jnp.logical_not(used) & last_f)
    def _():
        y_ref[...] = jnp.zeros(y_ref.shape, y_ref.dtype)


def _grouped_ffn(tile_expert, n_used, n_valid, xg, w_gu, w_down, *, tm, tf):
    parts, n_rows, dpp = xg.shape
    n_exp, d_ff, d = w_down.shape
    nf = d_ff // tf

    def wspec(shape, index):
        def index_map(i, f, te, nu, nv):
            return index(te[i], jnp.where(i < nu[0], f, nf - 1))
        return pl.BlockSpec(shape, index_map)

    rows = pl.BlockSpec((parts, tm, dpp), lambda i, f, te, nu, nv: (0, i, 0))
    return pl.pallas_call(
        _grouped_ffn_kernel,
        out_shape=jax.ShapeDtypeStruct(xg.shape, xg.dtype),
        grid_spec=pltpu.PrefetchScalarGridSpec(
            num_scalar_prefetch=3,
            grid=(n_rows // tm, nf),
            in_specs=[rows,
                      wspec((1, d, tf), lambda e, f: (e, 0, f)),
                      wspec((1, d, tf), lambda e, f: (e, 0, nf + f)),
                      wspec((1, tf, d), lambda e, f: (e, f, 0))],
            out_specs=rows,
            scratch_shapes=[pltpu.VMEM((tm, d), BF16), pltpu.VMEM((tm, d), F32)]),
        compiler_params=_cparams(("arbitrary", "arbitrary")),
        name="moe_grouped_ffn",
    )(tile_expert, n_used, n_valid, xg, w_gu, w_gu, w_down)


def _sc_gather_rows(table, idx):
    n_idx = idx.shape[0]
    d = table.shape[1]

    @functools.partial(pl.kernel, out_type=jax.ShapeDtypeStruct((n_idx, d), table.dtype), mesh=_sc_mesh(),
                       name="sc_gather_rows")
    def gather(t_hbm, i_hbm, o_hbm):
        def body(i_vmem, o_vmem):
            pltpu.sync_copy(t_hbm.at[i_vmem.at[0]], o_vmem)

        pltpu.emit_pipeline(
            body,
            grid=(n_idx // SC_WINDOW,),
            in_specs=[pl.BlockSpec((1, SC_WINDOW), index_map=lambda i: (0, i))],
            out_specs=[pl.BlockSpec((SC_WINDOW, d), index_map=lambda i: (i, 0))],
            core_axis_name=("core", "subcore"),
            dimension_semantics=(pltpu.PARALLEL,),
        )(i_hbm, o_hbm)

    return gather(table, idx.reshape(1, n_idx))


def _combine_kernel(x_ref, rt_ref, gt_ref, gf_ref, *refs):
    y_refs, o_ref = refs[:-1], refs[-1]
    rt = rt_ref[...]
    tot = None
    for k in range(TOP_K):
        words = jnp.concatenate([y_refs[p * TOP_K + k][...] for p in range(SC_ROW_PARTS)], axis=1)
        term = rt[:, TOP_K + k:TOP_K + k + 1] * _unpack_bf16_pairs(words)
        tot = term if tot is None else tot + term
    x = x_ref[...] + gt_ref[...] * tot
    o_ref[...] = _rms(x, gf_ref[...])


def _combine(x, rt, mod1, g_final, ysel, *, seq, tm):
    n, d = x.shape
    tpb = seq // tm
    nt = n // tm
    row = lambda w: pl.BlockSpec((tm, w), lambda i: (i, 0))
    piece = lambda j: pl.BlockSpec((tm, ysel.shape[1]), lambda i: (j * nt + i, 0))
    n_pieces = SC_ROW_PARTS * TOP_K
    return pl.pallas_call(
        _combine_kernel,
        out_shape=jax.ShapeDtypeStruct((n, d), F32),
        grid=(nt,),
        in_specs=[row(d), row(LANES), _mod_spec(5, tpb, d), pl.BlockSpec(g_final.shape, lambda i: (0, 0))]
                 + [piece(j) for j in range(n_pieces)],
        out_specs=row(d),
        compiler_params=_cparams(("parallel",)),
        name="moe_combine",
    )(x, rt, mod1, g_final, *([ysel] * n_pieces))


def _pad_last(a, width):
    return jnp.pad(a, [(0, 0)] * (a.ndim - 1) + [(0, width - a.shape[-1])])


def kernel(x, c, positions, w_mod, b_mod, g_mix, g_ffn, w_a_down, g_q_lat, g_kv_lat, w_uq, w_ukv, w_oa,
           w_mod_kv, b_mod_kv, g_kv, w_kv_sb, w_q_sb, w_o_sb, w_ffn_gu, w_ffn_down, w_router, b_router,
           w_exp_gu, w_exp_down, g_final):
    bsz, seq, d = x.shape
    n = bsz * seq
    q_lora, kv_lora = g_q_lat.shape[1], g_kv_lat.shape[1]
    n_exp = w_router.shape[-1]
    d_ff = w_ffn_down.shape[1]
    tm = min(TOKEN_TILE, seq)
    t_mla = min(MLA_TILE, seq)
    t_sb = min(SB_TILE, seq)
    tf = next((t for t in (FFN_CHUNK, 512) if d_ff % t == 0), d_ff)

    mod0 = _modvec(c, w_mod[0], b_mod[0]).reshape(bsz, 6, 1, d)
    mod1 = _modvec(c, w_mod[1], b_mod[1]).reshape(bsz, 6, 1, d)
    mod_kv = _modvec(c, w_mod_kv, b_mod_kv).reshape(bsz, 2, 1, d)

    lat_w = q_lora + kv_lora + LANES
    wd = _pad_last(w_a_down[0], lat_w).astype(BF16)
    wq = w_uq[0].reshape(q_lora, MLA_HEADS, QK_NOPE + QK_ROPE)
    wqn = wq[:, :, :QK_NOPE].reshape(q_lora, MLA_HEADS * QK_NOPE).astype(BF16)
    wqr = _pad_last(wq[:, :, QK_NOPE:], LANES).reshape(q_lora, MLA_HEADS * LANES).astype(BF16)
    wkv = w_ukv[0].reshape(kv_lora, MLA_HEADS, QK_NOPE + V_DIM)
    wkn = wkv[:, :, :QK_NOPE].reshape(kv_lora, MLA_HEADS * QK_NOPE).astype(BF16)
    wv = wkv[:, :, QK_NOPE:].reshape(kv_lora, MLA_HEADS * V_DIM).astype(BF16)
    half = QK_ROPE // 2
    inv = ROPE_THETA ** (-jnp.arange(half, dtype=F32) / half)
    inv = jnp.tile(inv, ROPE_PACK).reshape(1, LANES)
    wr = _pad_last(w_router[0], LANES)
    wr_hi = wr.astype(BF16)
    wr_lo = (wr - wr_hi.astype(F32)).astype(BF16)
    br = _pad_last(b_router[0], LANES).reshape(1, LANES)

    xf = x.reshape(n, d)
    pos = positions.reshape(n // tm, ROPE_PACK, tm // ROPE_PACK).swapaxes(1, 2)
    pos = jnp.repeat(pos.reshape(n // ROPE_PACK, ROPE_PACK), QK_ROPE // 2, axis=1)
    row1 = lambda a: a.reshape(1, -1)

    q, k, v = _mla_proj(xf, pos, mod0, row1(g_mix[0]), wd, row1(g_q_lat[0]), row1(g_kv_lat[0]),
                        wqn, wqr, wkn, wv, inv, seq=seq, tm=tm)
    o = _mla_attn(q.reshape(bsz, seq, -1), k.reshape(bsz, seq, -1), v.reshape(bsz, seq, -1), t=t_mla)
    x2, hk, hm = _ffn(o.reshape(n, -1), w_oa[0].astype(BF16), xf, mod0, row1(g_ffn[0]),
                      w_ffn_gu[0].astype(BF16), w_ffn_down[0].astype(BF16),
                      row1(g_kv), mod_kv, row1(g_mix[1]), mod1, seq=seq, tm=tm, tf=tf)
    kv = _linear(hk, w_kv_sb.astype(BF16), tm=tm, tn=w_kv_sb.shape[1], name="kv_proj")
    q_scale = math.log2(math.e) / math.sqrt(SB_HEAD_DIM)
    qs = _linear(hm, (w_q_sb[0] * q_scale).astype(BF16), tm=tm, tn=1024, name="q_proj")
    o = _sb_attn(qs.reshape(bsz, seq, -1), kv.reshape(bsz, seq, -1), t=t_sb)
    x3, h, rt, counts = _out_proj_router(o.reshape(n, -1), w_o_sb[0].astype(BF16), x2, mod1, row1(g_ffn[1]),
                                         wr_hi, wr_lo, br, n_exp, seq=seq, tm=tm)
    n_tiles = TOP_K * n // tm + n_exp
    n_rows = n_tiles * tm
    tile_base, tile_expert, n_used, n_valid = _route_plan(counts, n_exp, tm, n_tiles)
    picks = _picks(rt, tile_base, n_rows, tm=tm)
    xg = _sc_scatter_rows(h.reshape(SC_ROW_PARTS * n, -1), [picks[:, k].reshape(-1) for k in range(TOP_K)],
                          SC_ROW_PARTS * n_rows)
    yg = _grouped_ffn(tile_expert, n_used, n_valid, xg.reshape(SC_ROW_PARTS, n_rows, -1),
                      w_exp_gu[0].astype(BF16), w_exp_down[0].astype(BF16), tm=tm, tf=tf)
    ysel = _sc_gather_rows(yg.reshape(SC_ROW_PARTS * n_rows, -1), picks.reshape(-1))
    out = _combine(x3, rt, mod1, row1(g_final), ysel, seq=seq, tm=tm)
    return out.reshape(bsz, seq, d)
```

```python
import functools
import math

import jax
import jax.numpy as jnp
from jax import lax
from jax.experimental import pallas as pl
from jax.experimental.pallas import tpu as pltpu
from jax.experimental.pallas import tpu_sc as plsc

F32 = jnp.float32
BF16 = jnp.bfloat16

EPS = 1e-6
MLA_HEADS = 8
QK_NOPE = 128
QK_ROPE = 64
V_DIM = 128
ROPE_THETA = 10000.0
SB_HEADS = 8
SB_HEAD_DIM = 128
TOP_K = 2

LANES = 128
QK_PAD = 256
ROPE_PACK = LANES // (QK_ROPE // 2)
VMEM_LIMIT = 48 * 1024 * 1024
VMEM_LIMIT_LARGE = 58 * 1024 * 1024
TOKEN_TILE = 512
MLA_TILE = 512
MLA_BLOCKS_PER_TRIP = 4
SB_TILE = 512
SB_BLOCKS_PER_TRIP = 2
SB_ZERO_LOG2 = -140.0
FFN_CHUNK = 1792
SC_CORES = 2
SC_SUBCORES = 16
SC_WINDOW = 128
SC_ROW_PARTS = 2


def _cparams(sem, vmem_limit=VMEM_LIMIT):
    return pltpu.CompilerParams(dimension_semantics=sem, vmem_limit_bytes=vmem_limit)


def _rms(x, g):
    return x * lax.rsqrt(jnp.mean(x * x, axis=-1, keepdims=True) + EPS) * g


def _modulate(h, shift, scale):
    return h * (1.0 + scale) + shift


def _split_bf16(a):
    hi = a.astype(BF16)
    lo = (a - hi.astype(F32)).astype(BF16)
    return hi, lo


U32 = jnp.uint32
HIGH_HALF = 0xFFFF0000


def _pack_bf16_pairs(a):
    half = a.shape[1] // 2
    bits = lambda v: lax.bitcast_convert_type(v.astype(BF16).astype(F32), U32)
    return (bits(a[:, :half]) >> 16) | (bits(a[:, half:]) & U32(HIGH_HALF))


def _unpack_bf16_pairs(w):
    lo = lax.bitcast_convert_type(w << 16, F32)
    hi = lax.bitcast_convert_type(w & U32(HIGH_HALF), F32)
    return jnp.concatenate([lo, hi], axis=1)


def _dot(a, b):
    return jnp.dot(a, b, preferred_element_type=F32)


def _dot_nt(a, b):
    return lax.dot_general(a, b, (((1,), (1,)), ((), ())), preferred_element_type=F32)


def _modvec_kernel(c_ref, w_ref, b_ref, o_ref):
    c = c_ref[...]
    sc = c * (1.0 / (1.0 + jnp.exp(-c)))
    a_hi, a_lo = _split_bf16(sc)
    w_hi, w_lo = _split_bf16(w_ref[...])
    o_ref[...] = _dot(a_hi, w_hi) + _dot(a_lo, w_hi) + _dot(a_hi, w_lo) + b_ref[...]


def _modvec(c, w, layer, b, tn=512):
    bsz, d = c.shape
    n = w.shape[2]
    return pl.pallas_call(
        _modvec_kernel,
        out_shape=jax.ShapeDtypeStruct((bsz, n), F32),
        grid=(n // tn,),
        in_specs=[pl.BlockSpec((bsz, d), lambda j: (0, 0)),
                  pl.BlockSpec((None, d, tn), lambda j: (layer, 0, j)),
                  pl.BlockSpec((1, tn), lambda j: (0, j))],
        out_specs=pl.BlockSpec((bsz, tn), lambda j: (0, j)),
        compiler_params=_cparams(("arbitrary",)),
        name="modvec",
    )(c, w, b.reshape(1, n))


def _mod_spec(chunk, tiles_per_batch, d):
    return pl.BlockSpec((None, None, 1, d), lambda i, *_: (i // tiles_per_batch, chunk, 0, 0))


def _mla_proj_kernel(x_ref, pos_ref, sh_ref, sc_ref, g_ref, wd_ref, gq_ref, gkv_ref,
                     wqn_ref, wqr_ref, wkn_ref, wv_ref, inv_ref,
                     q_ref, k_ref, v_ref, *, q_lora, kv_lora):
    x = x_ref[...]
    h = _modulate(_rms(x, g_ref[...]), sh_ref[...], sc_ref[...]).astype(BF16)
    lat = _dot(h, wd_ref[...])
    c_q = _rms(lat[:, :q_lora], gq_ref[...]).astype(BF16)
    c_kv = _rms(lat[:, q_lora:q_lora + kv_lora], gkv_ref[...]).astype(BF16)
    k_rot = lat[:, q_lora + kv_lora:]

    half = QK_ROPE // 2
    ang = pos_ref[...].astype(F32) * inv_ref[...]
    cos_p = jnp.cos(ang)
    sin_p = jnp.sin(ang)
    lane = lax.broadcasted_iota(jnp.int32, ang.shape, 1)
    cos, s_a, s_b = [], [], []
    for g in range(ROPE_PACK):
        shift = (LANES - g * half) % LANES
        cg = pltpu.roll(cos_p, shift, axis=1) if shift else cos_p
        sg = pltpu.roll(sin_p, shift, axis=1) if shift else sin_p
        cos.append(jnp.where(lane < half, cg, pltpu.roll(cg, half, axis=1)))
        s_a.append(jnp.where(lane < half, -sg, 0.0))
        s_b.append(jnp.where((lane >= half) & (lane < 2 * half), pltpu.roll(sg, half, axis=1), 0.0))
    cos, s_a, s_b = (jnp.concatenate(v, axis=0) for v in (cos, s_a, s_b))

    def rope(r):
        return (r * cos + pltpu.roll(r, LANES - half, axis=1) * s_a
                + pltpu.roll(r, half, axis=1) * s_b)

    k_rot = rope(k_rot).astype(BF16)
    q_nope = _dot(c_q, wqn_ref[...])
    q_rope = _dot(c_q, wqr_ref[...])
    k_nope = _dot(c_kv, wkn_ref[...])
    v_ref[...] = _dot(c_kv, wv_ref[...]).astype(BF16)
    for hd in range(MLA_HEADS):
        a, b = hd * LANES, (hd + 1) * LANES
        q_ref[:, hd * QK_PAD:hd * QK_PAD + LANES] = q_nope[:, a:b].astype(BF16)
        q_ref[:, hd * QK_PAD + LANES:(hd + 1) * QK_PAD] = rope(q_rope[:, a:b]).astype(BF16)
        k_ref[:, hd * QK_PAD:hd * QK_PAD + LANES] = k_nope[:, a:b].astype(BF16)
        k_ref[:, hd * QK_PAD + LANES:(hd + 1) * QK_PAD] = k_rot


def _mla_proj(x, pos, mod, g_mix, wd, gq, gkv, wqn, wqr, wkn, wv, inv, *, seq, tm):
    n, d = x.shape
    tpb = seq // tm
    q_lora, kv_lora = gq.shape[1], gkv.shape[1]
    hq = MLA_HEADS * QK_PAD
    hv = MLA_HEADS * V_DIM
    full = lambda a: pl.BlockSpec(a.shape, lambda i: (0,) * a.ndim)
    row = lambda w: pl.BlockSpec((tm, w), lambda i: (i, 0))
    return pl.pallas_call(
        functools.partial(_mla_proj_kernel, q_lora=q_lora, kv_lora=kv_lora),
        out_shape=(jax.ShapeDtypeStruct((n, hq), BF16), jax.ShapeDtypeStruct((n, hq), BF16),
                   jax.ShapeDtypeStruct((n, hv), BF16)),
        grid=(n // tm,),
        in_specs=[row(d), pl.BlockSpec((tm // ROPE_PACK, LANES), lambda i: (i, 0)), _mod_spec(0, tpb, d), _mod_spec(1, tpb, d), full(g_mix), full(wd),
                  full(gq), full(gkv), full(wqn), full(wqr), full(wkn), full(wv), full(inv)],
        out_specs=(row(hq), row(hq), row(hv)),
        compiler_params=_cparams(("parallel",)),
        name="mla_proj",
    )(x, pos, mod, mod, g_mix, wd, gq, gkv, wqn, wqr, wkn, wv, inv)


def _lane_chunks(a):
    return [a[:, c * LANES:(c + 1) * LANES] for c in range(a.shape[1] // LANES)]


def _for_blocks(n, step, group):
    def trip(p, carry):
        step(group * p, group)
        return carry

    lax.fori_loop(0, n // group, trip, 0)
    size = group // 2
    while size:
        @pl.when((n // size) % 2 == 1)
        def _(size=size):
            step(n // (2 * size) * (2 * size), size)
        size //= 2


def _mla_attn_kernel(q_ref, k_ref, v_ref, o_ref, s_ref, m_ref, acc_ref, *, t, scale, heads):
    i = pl.program_id(2)

    def lane_max(s):
        m = None
        for sc in _lane_chunks(s):
            m = sc if m is None else jnp.maximum(m, sc)
        return m

    def scores(h, j):
        start = pl.multiple_of(j * t, t)
        hs = slice(h * QK_PAD, (h + 1) * QK_PAD)
        return _dot_nt(q_ref[0, :, hs], k_ref[0, pl.ds(start, t), hs])

    r_minus_c = (lax.broadcasted_iota(jnp.int32, (t, t), 0) - lax.broadcasted_iota(jnp.int32, (t, t), 1))
    m_ref[...] = jnp.full(m_ref.shape, -jnp.inf, F32)

    def pass1(j, nb):
        for h in range(heads):
            m = m_ref[h]
            for jj in range(nb):
                s = jnp.where(r_minus_c >= (j + jj - i) * t, scores(h, j + jj), -jnp.inf)
                s_ref[h, j + jj] = s
                m = jnp.maximum(m, lane_max(s))
            m_ref[h] = m

    _for_blocks(i + 1, pass1, MLA_BLOCKS_PER_TRIP)
    for h in range(heads):
        m_ref[h] = jnp.broadcast_to(jnp.max(m_ref[h], axis=1, keepdims=True), (t, LANES))
    acc_ref[...] = jnp.zeros(acc_ref.shape, F32)
    cst = scale * math.log2(math.e)

    def pass2(j, nb):
        start = pl.multiple_of(j * t, t)
        ones = jnp.ones((nb * t, LANES), BF16)
        for h in range(heads):
            m = m_ref[h]
            p = jnp.concatenate([jnp.exp2((sc - m) * cst).astype(BF16)
                                 for jj in range(nb) for sc in _lane_chunks(s_ref[h, j + jj])], axis=1)
            v_ext = jnp.concatenate([v_ref[0, pl.ds(start, nb * t), h * V_DIM:(h + 1) * V_DIM], ones], axis=1)
            acc_ref[h] += _dot(p, v_ext)

    _for_blocks(i + 1, pass2, MLA_BLOCKS_PER_TRIP)
    for h in range(heads):
        acc = acc_ref[h]
        o_ref[0, :, h * V_DIM:(h + 1) * V_DIM] = (acc[:, :V_DIM] / acc[:, V_DIM:]).astype(o_ref.dtype)


def _mla_attn(q, k, v, *, t, heads=2):
    bsz, seq, _ = q.shape
    scale = 1.0 / math.sqrt(QK_NOPE + QK_ROPE)
    wq, wv = heads * QK_PAD, heads * V_DIM
    return pl.pallas_call(
        functools.partial(_mla_attn_kernel, t=t, scale=scale, heads=heads),
        out_shape=jax.ShapeDtypeStruct((bsz, seq, MLA_HEADS * V_DIM), BF16),
        grid=(bsz, MLA_HEADS // heads, seq // t),
        in_specs=[pl.BlockSpec((1, t, wq), lambda b, g, i: (b, i, g)),
                  pl.BlockSpec((1, seq, wq), lambda b, g, i: (b, 0, g)),
                  pl.BlockSpec((1, seq, wv), lambda b, g, i: (b, 0, g))],
        out_specs=pl.BlockSpec((1, t, wv), lambda b, g, i: (b, i, g)),
        scratch_shapes=[pltpu.VMEM((heads, seq // t, t, t), F32), pltpu.VMEM((heads, t, LANES), F32),
                        pltpu.VMEM((heads, t, V_DIM + LANES), F32)],
        compiler_params=_cparams(("parallel", "parallel", "arbitrary")),
        name="mla_attn",
    )(q, k, v)


def _sb_attn_kernel(q_ref, k_ref, v_ref, w_ref, o_ref, c_ref, acc_ref, kmax_ref, bound_ref, *, t, heads):
    i = pl.program_id(2)
    d = SB_HEAD_DIM
    c_ref[...] = jnp.zeros(c_ref.shape, F32)
    acc_ref[...] = jnp.zeros(acc_ref.shape, F32)
    sub = t // 2 if t % (2 * LANES) == 0 else t
    r_iota = lax.broadcasted_iota(jnp.int32, (sub, LANES), 0)
    c_iota = lax.broadcasted_iota(jnp.int32, (sub, LANES), 1)

    def block(r0, nr, start, nk, key0):
        rows = slice(r0, r0 + nr)
        masks = []
        for cc in range(nk // LANES):
            off = None if key0 is None else key0 + cc * LANES - r0
            masks.append(None if off is None or off <= -LANES else c_iota + off < r_iota)
        for h in range(heads):
            hs = slice(h * d, (h + 1) * d)
            zz = _dot_nt(q_ref[0, rows, hs], k_ref[0, pl.ds(start, nk), hs])
            zneg = -zz
            sp = jnp.log2(1.0 + jnp.exp2(jnp.minimum(zz, zneg)))
            log_1m = jnp.minimum(zneg, 0.0) - sp
            carry = c_ref[h, rows]
            a_chunks = [None] * (nk // LANES)
            for cc in reversed(range(nk // LANES)):
                cs = slice(cc * LANES, (cc + 1) * LANES)
                l1m = log_1m[:, cs] if masks[cc] is None else jnp.where(masks[cc], log_1m[:, cs], 0.0)
                hi, lo = _split_bf16(l1m)
                y = _dot(jnp.concatenate([hi, lo], axis=1), w_ref[...])
                a = jnp.exp2(zz[:, cs] + y[:, :LANES] + carry)
                carry = carry + y[:, LANES:]
                if masks[cc] is not None:
                    a = jnp.where(masks[cc], a, 0.0)
                a_chunks[cc] = a.astype(BF16)
            c_ref[h, rows] = carry
            acc_ref[h, rows] += _dot(jnp.concatenate(a_chunks, axis=1), v_ref[0, pl.ds(start, nk), hs])

    def diagonal_tile():
        for band in range(t // sub):
            block(band * sub, sub, pl.multiple_of(i * t, t), (band + 1) * sub, 0)

    def left_blocks(n, nb):
        for jj in range(nb):
            block(0, t, pl.multiple_of((i - 1 - n - jj) * t, t), t, None)

    @pl.when(i == 0)
    def _():
        diagonal_tile()

    @pl.when(i > 0)
    def _():
        diagonal_tile()
        left_blocks(0, 1)

    @pl.when(i == 0)
    def _():
        for h in range(heads):
            kmax = jnp.max(jnp.abs(k_ref[0, :, h * d:(h + 1) * d].astype(F32)))
            kmax_ref[h] = jnp.full(kmax_ref.shape[1:], kmax, F32)

    for h in range(heads):
        q_l1 = jnp.sum(jnp.abs(q_ref[0, :, h * d:(h + 1) * d].astype(F32)), axis=1, keepdims=True)
        bound_ref[h] = q_l1 * kmax_ref[h, 0:1, :]

    def more_to_add():
        worst = None
        for h in range(heads):
            top = jnp.max(c_ref[h] + bound_ref[h])
            worst = top if worst is None else jnp.maximum(worst, top)
        return worst >= SB_ZERO_LOG2

    def trip(state):
        n, _ = state
        left_blocks(n, SB_BLOCKS_PER_TRIP)
        return n + SB_BLOCKS_PER_TRIP, more_to_add()

    n, go = lax.while_loop(lambda s: s[1] & (s[0] + SB_BLOCKS_PER_TRIP <= i), trip,
                           (jnp.int32(1), (i > 0) & more_to_add()))
    size = SB_BLOCKS_PER_TRIP // 2
    while size:
        take = go & (n + size <= i)

        @pl.when(take)
        def _(n=n, size=size):
            left_blocks(n, size)

        n = n + jnp.where(take, size, 0)
        size //= 2
    for h in range(heads):
        o_ref[0, :, h * d:(h + 1) * d] = acc_ref[h].astype(o_ref.dtype)


def _sb_attn(q, kv, *, t, heads=4):
    bsz, seq, _ = q.shape
    w = heads * SB_HEAD_DIM
    groups = SB_HEADS // heads
    tri = (jnp.arange(LANES)[:, None] >= jnp.arange(LANES)[None, :]).astype(BF16)
    half = jnp.concatenate([tri, jnp.ones((LANES, LANES), BF16)], axis=1)
    w_sum = jnp.concatenate([half, half], axis=0)
    return pl.pallas_call(
        functools.partial(_sb_attn_kernel, t=t, heads=heads),
        out_shape=jax.ShapeDtypeStruct((bsz, seq, SB_HEADS * SB_HEAD_DIM), BF16),
        grid=(bsz, groups, seq // t),
        in_specs=[pl.BlockSpec((1, t, w), lambda b, g, i: (b, i, g)),
                  pl.BlockSpec((1, seq, w), lambda b, g, i: (b, 0, g)),
                  pl.BlockSpec((1, seq, w), lambda b, g, i: (b, 0, groups + g)),
                  pl.BlockSpec(w_sum.shape, lambda b, g, i: (0, 0))],
        out_specs=pl.BlockSpec((1, t, w), lambda b, g, i: (b, i, g)),
        scratch_shapes=[pltpu.VMEM((heads, t, LANES), F32), pltpu.VMEM((heads, t, SB_HEAD_DIM), F32),
                        pltpu.VMEM((heads, 8, LANES), F32), pltpu.VMEM((heads, t, LANES), F32)],
        compiler_params=_cparams(("parallel", "parallel", "arbitrary")),
        name="sb_attn",
    )(q, kv, kv, w_sum)


def _out_proj_kernel(o_ref, w_ref, x_ref, gt_ref, g_ref, sh_ref, sc_ref, wr_hi_ref, wr_lo_ref, br_ref, tri_ref,
                     xo_ref, h_ref, rt_ref, cnt_ref, *, n_experts):
    x = x_ref[...] + gt_ref[...] * _dot(o_ref[...], w_ref[...])
    xo_ref[...] = x
    h = _modulate(_rms(x, g_ref[...]), sh_ref[...], sc_ref[...])
    h_hi = h.astype(BF16)
    words = _pack_bf16_pairs(h)
    for part in range(SC_ROW_PARTS):
        h_ref[part] = words[:, part * h_ref.shape[2]:(part + 1) * h_ref.shape[2]]
    h_lo = (h - h_hi.astype(F32)).astype(BF16)
    both = _dot(h_hi, jnp.concatenate([wr_hi_ref[...], wr_lo_ref[...]], axis=1))
    logits = both[:, :LANES] + both[:, LANES:] + _dot(h_lo, wr_hi_ref[...]) + br_ref[...]
    lane = lax.broadcasted_iota(jnp.int32, logits.shape, 1).astype(F32)
    lg = jnp.where(lane < n_experts, logits, -jnp.inf)
    m1 = jnp.max(lg, axis=1, keepdims=True)
    i1 = jnp.min(jnp.where(lg == m1, lane, float(LANES)), axis=1, keepdims=True)
    lg2 = jnp.where(lane == i1, -jnp.inf, lg)
    m2 = jnp.max(lg2, axis=1, keepdims=True)
    i2 = jnp.min(jnp.where(lg2 == m2, lane, float(LANES)), axis=1, keepdims=True)
    e2 = jnp.exp(m2 - m1)
    den = 1.0 + e2
    sel = jnp.where((lane == i1) | (lane == i2), 1.0, 0.0)
    prefix = _dot(tri_ref[...], sel.astype(BF16))
    r1 = jnp.sum(jnp.where(lane == i1, prefix, 0.0), axis=1, keepdims=True)
    r2 = jnp.sum(jnp.where(lane == i2, prefix, 0.0), axis=1, keepdims=True)
    cnt_ref[...] = jnp.sum(sel, axis=0, keepdims=True)
    rt = jnp.zeros_like(logits)
    for k, val in enumerate((i1, i2, 1.0 / den, e2 / den, r1, r2)):
        rt = jnp.where(lane == k, val, rt)
    rt_ref[...] = rt


def _out_proj_router(o, w, x, mod, g, wr_hi, wr_lo, br, n_experts, *, seq, tm):
    n, d = x.shape
    tpb = seq // tm
    dpp = d // 2 // SC_ROW_PARTS
    tri = (jnp.arange(tm)[:, None] > jnp.arange(tm)[None, :]).astype(BF16)
    full = lambda a: pl.BlockSpec(a.shape, lambda i: (0,) * a.ndim)
    row = lambda wd: pl.BlockSpec((tm, wd), lambda i: (i, 0))
    return pl.pallas_call(
        functools.partial(_out_proj_kernel, n_experts=n_experts),
        out_shape=(jax.ShapeDtypeStruct((n, d), F32), jax.ShapeDtypeStruct((SC_ROW_PARTS, n, dpp), U32),
                   jax.ShapeDtypeStruct((n, LANES), F32), jax.ShapeDtypeStruct((n // tm, 1, LANES), F32)),
        grid=(n // tm,),
        in_specs=[row(o.shape[1]), full(w), row(d), _mod_spec(2, tpb, d), full(g),
                  _mod_spec(3, tpb, d), _mod_spec(4, tpb, d), full(wr_hi), full(wr_lo), full(br), full(tri)],
        out_specs=(row(d), pl.BlockSpec((SC_ROW_PARTS, tm, dpp), lambda i: (0, i, 0)), row(LANES),
                   pl.BlockSpec((None, 1, LANES), lambda i: (i, 0, 0))),
        compiler_params=_cparams(("parallel",)),
        name="out_proj_router",
    )(o, w, x, mod, g, mod, mod, wr_hi, wr_lo, br, tri)


def _silu(g):
    return g * (1.0 / (1.0 + jnp.exp(-g)))


def _ffn_kernel(o_ref, wo_ref, x_ref, gt1_ref, gf_ref, sh2_ref, sc2_ref, wg_ref, wu_ref, wd_ref, gt2_ref,
                gkv_ref, shk_ref, sck_ref, gm_ref, shm_ref, scm_ref,
                xo_ref, hk_ref, hm_ref, x1_ref, h_ref, acc_ref):
    f = pl.program_id(1)

    @pl.when(f == 0)
    def _():
        x1 = x_ref[...] + gt1_ref[...] * _dot(o_ref[...], wo_ref[...])
        x1_ref[...] = x1
        h_ref[...] = _modulate(_rms(x1, gf_ref[...]), sh2_ref[...], sc2_ref[...]).astype(BF16)
        acc_ref[...] = jnp.zeros(acc_ref.shape, F32)

    h = h_ref[...]
    a = _silu(_dot(h, wg_ref[...])) * _dot(h, wu_ref[...])
    acc_ref[...] += _dot(a.astype(BF16), wd_ref[...])

    @pl.when(f == pl.num_programs(1) - 1)
    def _():
        x = x1_ref[...] + gt2_ref[...] * acc_ref[...]
        xo_ref[...] = x
        y = x * lax.rsqrt(jnp.mean(x * x, axis=-1, keepdims=True) + EPS)
        hk_ref[...] = _modulate(y * gkv_ref[...], shk_ref[...], sck_ref[...]).astype(BF16)
        hm_ref[...] = _modulate(y * gm_ref[...], shm_ref[...], scm_ref[...]).astype(BF16)


def _ffn(o, w_o, x, mod0, g_ffn, w_gu, w_down, g_kv, mod_kv, g_mix1, mod1, *, seq, tm, tf):
    n, d = x.shape
    d_ff = w_down.shape[0]
    nf = d_ff // tf
    tpb = seq // tm
    full = lambda a: pl.BlockSpec(a.shape, lambda i, f: (0,) * a.ndim)
    row = lambda w=d: pl.BlockSpec((tm, w), lambda i, f: (i, 0))
    return pl.pallas_call(
        _ffn_kernel,
        out_shape=(jax.ShapeDtypeStruct((n, d), F32), jax.ShapeDtypeStruct((n, d), BF16),
                   jax.ShapeDtypeStruct((n, d), BF16)),
        grid=(n // tm, nf),
        in_specs=[row(o.shape[1]), full(w_o), row(), _mod_spec(2, tpb, d), full(g_ffn),
                  _mod_spec(3, tpb, d), _mod_spec(4, tpb, d),
                  pl.BlockSpec((d, tf), lambda i, f: (0, f)),
                  pl.BlockSpec((d, tf), lambda i, f: (0, nf + f)),
                  pl.BlockSpec((tf, d), lambda i, f: (f, 0)),
                  _mod_spec(5, tpb, d),
                  full(g_kv), _mod_spec(0, tpb, d), _mod_spec(1, tpb, d),
                  full(g_mix1), _mod_spec(0, tpb, d), _mod_spec(1, tpb, d)],
        out_specs=(row(), row(), row()),
        scratch_shapes=[pltpu.VMEM((tm, d), F32), pltpu.VMEM((tm, d), BF16), pltpu.VMEM((tm, d), F32)],
        compiler_params=_cparams(("parallel", "arbitrary"), vmem_limit=VMEM_LIMIT_LARGE),
        name="ffn_dense",
    )(o, w_o, x, mod0, g_ffn, mod0, mod0, w_gu, w_gu, w_down, mod0,
      g_kv, mod_kv, mod_kv, g_mix1, mod1, mod1)


def _linear_kernel(x_ref, w_ref, o_ref):
    o_ref[...] = _dot(x_ref[...], w_ref[...]).astype(o_ref.dtype)


def _linear(x, w, *, tm, tn, name):
    n, k = x.shape
    m = w.shape[1]
    return pl.pallas_call(
        _linear_kernel,
        out_shape=jax.ShapeDtypeStruct((n, m), BF16),
        grid=(m // tn, n // tm),
        in_specs=[pl.BlockSpec((tm, k), lambda j, i: (i, 0)), pl.BlockSpec((k, tn), lambda j, i: (0, j))],
        out_specs=pl.BlockSpec((tm, tn), lambda j, i: (i, j)),
        compiler_params=_cparams(("parallel", "parallel")),
        name=name,
    )(x, w)


def _route_plan(counts, n_exp, tm, n_tiles):
    cnt = counts[:, 0, :n_exp].astype(jnp.int32)
    sizes = jnp.sum(cnt, axis=0)
    padded = (sizes + tm - 1) // tm * tm
    ends = jnp.cumsum(padded)
    tile_base = (ends - padded)[None, :] + jnp.cumsum(cnt, axis=0) - cnt
    tile_start = jnp.arange(n_tiles, dtype=jnp.int32) * tm
    tile_expert = jnp.minimum(jnp.sum(tile_start[:, None] >= ends[None, :], axis=1), n_exp - 1)
    n_used = (ends[-1] // tm).reshape(1)
    n_valid = jnp.clip((ends - padded + sizes)[tile_expert] - tile_start, 0, tm)
    return tile_base, tile_expert.astype(jnp.int32), n_used.astype(jnp.int32), n_valid.astype(jnp.int32)


PICK_ROWS = 8


def _picks_kernel(rt_ref, base_ref, o_ref, *, n_rows):
    rt = rt_ref[...]
    lane = lax.broadcasted_iota(jnp.int32, rt.shape, 1).astype(F32)
    cols = jnp.zeros_like(rt)
    for k in range(TOP_K):
        e_k = rt[:, k:k + 1]
        r_k = rt[:, 2 * TOP_K + k:2 * TOP_K + k + 1]
        d_k = jnp.sum(jnp.where(lane == e_k, base_ref[...], 0.0), axis=1, keepdims=True) + r_k
        for p in range(SC_ROW_PARTS):
            cols = jnp.where(lane == p * TOP_K + k, d_k + float(p * n_rows), cols)
    o_ref[...] = cols.T[:PICK_ROWS].astype(jnp.int32)


def _picks(rt, tile_base, n_rows, *, tm):
    n = rt.shape[0]
    base = _pad_last(tile_base.astype(F32), LANES).reshape(n // tm, 1, LANES)
    out = pl.pallas_call(
        functools.partial(_picks_kernel, n_rows=n_rows),
        out_shape=jax.ShapeDtypeStruct((PICK_ROWS, n), jnp.int32),
        grid=(n // tm,),
        in_specs=[pl.BlockSpec((tm, LANES), lambda i: (i, 0)),
                  pl.BlockSpec((None, 1, LANES), lambda i: (i, 0, 0))],
        out_specs=pl.BlockSpec((PICK_ROWS, tm), lambda i: (0, i)),
        compiler_params=_cparams(("parallel",)),
        name="moe_picks",
    )(rt, base)
    return out[:SC_ROW_PARTS * TOP_K].reshape(SC_ROW_PARTS, TOP_K, n)


def _sc_mesh():
    return plsc.VectorSubcoreMesh(core_axis_name="core", subcore_axis_name="subcore",
                                  num_cores=SC_CORES, num_subcores=SC_SUBCORES)


def _sc_scatter_rows(x, idx_list, n_rows):
    m, d = x.shape

    @functools.partial(pl.kernel, out_type=jax.ShapeDtypeStruct((n_rows, d), x.dtype), mesh=_sc_mesh(),
                       scratch_types=[], name="sc_scatter_rows")
    def scatter(x_hbm, *refs):
        i_hbms, o_hbm = refs[:-1], refs[-1]

        def body(x_vmem, *i_vmems):
            for i_vmem in i_vmems:
                pltpu.sync_copy(x_vmem, o_hbm.at[i_vmem.at[0]])

        pltpu.emit_pipeline(
            body,
            grid=(m // SC_WINDOW,),
            in_specs=[pl.BlockSpec((SC_WINDOW, d), index_map=lambda i: (i, 0))]
                     + [pl.BlockSpec((1, SC_WINDOW), index_map=lambda i: (0, i))] * len(idx_list),
            out_specs=[],
            core_axis_name=("core", "subcore"),
            dimension_semantics=(pltpu.PARALLEL,),
        )(x_hbm, *i_hbms)

    return scatter(x, *[idx.reshape(1, m) for idx in idx_list])


def _grouped_ffn_kernel(te_ref, nu_ref, nv_ref, x_ref, wg_ref, wu_ref, wd_ref, y_ref, hb_ref, acc_ref):
    del te_ref
    i = pl.program_id(0)
    f = pl.program_id(1)
    used = i < nu_ref[0]
    last_f = f == pl.num_programs(1) - 1

    @pl.when(used & (f == 0))
    def _():
        words = jnp.concatenate([x_ref[p] for p in range(SC_ROW_PARTS)], axis=1)
        row = lax.broadcasted_iota(jnp.int32, words.shape, 0)
        words = jnp.where(row < nv_ref[i], words, U32(0))
        hb_ref[...] = _unpack_bf16_pairs(words).astype(BF16)
        acc_ref[...] = jnp.zeros(acc_ref.shape, F32)

    @pl.when(used)
    def _():
        h = hb_ref[...]
        a = _silu(_dot(h, wg_ref[0])) * _dot(h, wu_ref[0])
        acc_ref[...] += _dot(a.astype(BF16), wd_ref[0])

    @pl.when(used & last_f)
    def _():
        y = _pack_bf16_pairs(acc_ref[...])
        for part in range(SC_ROW_PARTS):
            y_ref[part] = y[:, part * y_ref.shape[2]:(part + 1) * y_ref.shape[2]]

    @pl.when(jnp.logical_not(used) & last_f)
    def _():
        y_ref[...] = jnp.zeros(y_ref.shape, y_ref.dtype)


def _grouped_ffn(tile_expert, n_used, n_valid, xg, w_gu, w_down, *, tm, tf):
    parts, n_rows, dpp = xg.shape
    n_exp, d_ff, d = w_down.shape
    nf = d_ff // tf

    def wspec(shape, index):
        def index_map(i, f, te, nu, nv):
            return index(te[i], jnp.where(i < nu[0], f, nf - 1))
        return pl.BlockSpec(shape, index_map)

    rows = pl.BlockSpec((parts, tm, dpp), lambda i, f, te, nu, nv: (0, i, 0))
    return pl.pallas_call(
        _grouped_ffn_kernel,
        out_shape=jax.ShapeDtypeStruct(xg.shape, xg.dtype),
        grid_spec=pltpu.PrefetchScalarGridSpec(
            num_scalar_prefetch=3,
            grid=(n_rows // tm, nf),
            in_specs=[rows,
                      wspec((1, d, tf), lambda e, f: (e, 0, f)),
                      wspec((1, d, tf), lambda e, f: (e, 0, nf + f)),
                      wspec((1, tf, d), lambda e, f: (e, f, 0))],
            out_specs=rows,
            scratch_shapes=[pltpu.VMEM((tm, d), BF16), pltpu.VMEM((tm, d), F32)]),
        compiler_params=_cparams(("arbitrary", "arbitrary")),
        name="moe_grouped_ffn",
    )(tile_expert, n_used, n_valid, xg, w_gu, w_gu, w_down)


def _sc_gather_rows(table, idx):
    n_idx = idx.shape[0]
    d = table.shape[1]

    @functools.partial(pl.kernel, out_type=jax.ShapeDtypeStruct((n_idx, d), table.dtype), mesh=_sc_mesh(),
                       name="sc_gather_rows")
    def gather(t_hbm, i_hbm, o_hbm):
        def body(i_vmem, o_vmem):
            pltpu.sync_copy(t_hbm.at[i_vmem.at[0]], o_vmem)

        pltpu.emit_pipeline(
            body,
            grid=(n_idx // SC_WINDOW,),
            in_specs=[pl.BlockSpec((1, SC_WINDOW), index_map=lambda i: (0, i))],
            out_specs=[pl.BlockSpec((SC_WINDOW, d), index_map=lambda i: (i, 0))],
            core_axis_name=("core", "subcore"),
            dimension_semantics=(pltpu.PARALLEL,),
        )(i_hbm, o_hbm)

    return gather(table, idx.reshape(1, n_idx))


def _combine_kernel(x_ref, rt_ref, gt_ref, gf_ref, *refs):
    y_refs, o_ref = refs[:-1], refs[-1]
    rt = rt_ref[...]
    tot = None
    for k in range(TOP_K):
        words = jnp.concatenate([y_refs[p * TOP_K + k][...] for p in range(SC_ROW_PARTS)], axis=1)
        term = rt[:, TOP_K + k:TOP_K + k + 1] * _unpack_bf16_pairs(words)
        tot = term if tot is None else tot + term
    x = x_ref[...] + gt_ref[...] * tot
    o_ref[...] = _rms(x, gf_ref[...])


def _combine(x, rt, mod1, g_final, ysel, *, seq, tm):
    n, d = x.shape
    tpb = seq // tm
    nt = n // tm
    row = lambda w: pl.BlockSpec((tm, w), lambda i: (i, 0))
    piece = lambda j: pl.BlockSpec((tm, ysel.shape[1]), lambda i: (j * nt + i, 0))
    n_pieces = SC_ROW_PARTS * TOP_K
    return pl.pallas_call(
        _combine_kernel,
        out_shape=jax.ShapeDtypeStruct((n, d), F32),
        grid=(nt,),
        in_specs=[row(d), row(LANES), _mod_spec(5, tpb, d), pl.BlockSpec(g_final.shape, lambda i: (0, 0))]
                 + [piece(j) for j in range(n_pieces)],
        out_specs=row(d),
        compiler_params=_cparams(("parallel",)),
        name="moe_combine",
    )(x, rt, mod1, g_final, *([ysel] * n_pieces))


def _pad_last(a, width):
    return jnp.pad(a, [(0, 0)] * (a.ndim - 1) + [(0, width - a.shape[-1])])


def kernel(x, c, positions, w_mod, b_mod, g_mix, g_ffn, w_a_down, g_q_lat, g_kv_lat, w_uq, w_ukv, w_oa,
           w_mod_kv, b_mod_kv, g_kv, w_kv_sb, w_q_sb, w_o_sb, w_ffn_gu, w_ffn_down, w_router, b_router,
           w_exp_gu, w_exp_down, g_final):
    bsz, seq, d = x.shape
    n = bsz * seq
    q_lora, kv_lora = g_q_lat.shape[1], g_kv_lat.shape[1]
    n_exp = w_router.shape[-1]
    d_ff = w_ffn_down.shape[1]
    tm = min(TOKEN_TILE, seq)
    t_mla = min(MLA_TILE, seq)
    t_sb = min(SB_TILE, seq)
    tf = next((t for t in (FFN_CHUNK, 512) if d_ff % t == 0), d_ff)

    mod0 = _modvec(c, w_mod, 0, b_mod[0]).reshape(bsz, 6, 1, d)
    mod1 = _modvec(c, w_mod, 1, b_mod[1]).reshape(bsz, 6, 1, d)
    mod_kv = _modvec(c, w_mod_kv[None], 0, b_mod_kv).reshape(bsz, 2, 1, d)

    lat_w = q_lora + kv_lora + LANES
    wd = _pad_last(w_a_down[0], lat_w).astype(BF16)
    wq = w_uq[0].reshape(q_lora, MLA_HEADS, QK_NOPE + QK_ROPE)
    wqn = wq[:, :, :QK_NOPE].reshape(q_lora, MLA_HEADS * QK_NOPE).astype(BF16)
    wqr = _pad_last(wq[:, :, QK_NOPE:], LANES).reshape(q_lora, MLA_HEADS * LANES).astype(BF16)
    wkv = w_ukv[0].reshape(kv_lora, MLA_HEADS, QK_NOPE + V_DIM)
    wkn = wkv[:, :, :QK_NOPE].reshape(kv_lora, MLA_HEADS * QK_NOPE).astype(BF16)
    wv = wkv[:, :, QK_NOPE:].reshape(kv_lora, MLA_HEADS * V_DIM).astype(BF16)
    half = QK_ROPE // 2
    inv = ROPE_THETA ** (-jnp.arange(half, dtype=F32) / half)
    inv = jnp.tile(inv, ROPE_PACK).reshape(1, LANES)
    wr = _pad_last(w_router[0], LANES)
    wr_hi = wr.astype(BF16)
    wr_lo = (wr - wr_hi.astype(F32)).astype(BF16)
    br = _pad_last(b_router[0], LANES).reshape(1, LANES)

    xf = x.reshape(n, d)
    pos = positions.reshape(n // tm, ROPE_PACK, tm // ROPE_PACK).swapaxes(1, 2)
    pos = jnp.repeat(pos.reshape(n // ROPE_PACK, ROPE_PACK), QK_ROPE // 2, axis=1)
    row1 = lambda a: a.reshape(1, -1)

    q, k, v = _mla_proj(xf, pos, mod0, row1(g_mix[0]), wd, row1(g_q_lat[0]), row1(g_kv_lat[0]),
                        wqn, wqr, wkn, wv, inv, seq=seq, tm=tm)
    o = _mla_attn(q.reshape(bsz, seq, -1), k.reshape(bsz, seq, -1), v.reshape(bsz, seq, -1), t=t_mla)
    x2, hk, hm = _ffn(o.reshape(n, -1), w_oa[0].astype(BF16), xf, mod0, row1(g_ffn[0]),
                      w_ffn_gu[0].astype(BF16), w_ffn_down[0].astype(BF16),
                      row1(g_kv), mod_kv, row1(g_mix[1]), mod1, seq=seq, tm=tm, tf=tf)
    kv = _linear(hk, w_kv_sb.astype(BF16), tm=tm, tn=w_kv_sb.shape[1], name="kv_proj")
    q_scale = math.log2(math.e) / math.sqrt(SB_HEAD_DIM)
    qs = _linear(hm, (w_q_sb[0] * q_scale).astype(BF16), tm=tm, tn=1024, name="q_proj")
    o = _sb_attn(qs.reshape(bsz, seq, -1), kv.reshape(bsz, seq, -1), t=t_sb)
    x3, h, rt, counts = _out_proj_router(o.reshape(n, -1), w_o_sb[0].astype(BF16), x2, mod1, row1(g_ffn[1]),
                                         wr_hi, wr_lo, br, n_exp, seq=seq, tm=tm)
    n_tiles = TOP_K * n // tm + n_exp
    n_rows = n_tiles * tm
    tile_base, tile_expert, n_used, n_valid = _route_plan(counts, n_exp, tm, n_tiles)
    picks = _picks(rt, tile_base, n_rows, tm=tm)
    xg = _sc_scatter_rows(h.reshape(SC_ROW_PARTS * n, -1), [picks[:, k].reshape(-1) for k in range(TOP_K)],
                          SC_ROW_PARTS * n_rows)
    yg = _grouped_ffn(tile_expert, n_used, n_valid, xg.reshape(SC_ROW_PARTS, n_rows, -1),
                      w_exp_gu[0].astype(BF16), w_exp_down[0].astype(BF16), tm=tm, tf=tf)
    ysel = _sc_gather_rows(yg.reshape(SC_ROW_PARTS * n_rows, -1), picks.reshape(-1))
    out = _combine(x3, rt, mod1, row1(g_final), ysel, seq=seq, tm=tm)
    return out.reshape(bsz, seq, d)
```

```python
import functools
import math

import jax
import jax.numpy as jnp
from jax import lax
from jax.experimental import pallas as pl
from jax.experimental.pallas import tpu as pltpu
from jax.experimental.pallas import tpu_sc as plsc

F32 = jnp.float32
BF16 = jnp.bfloat16

EPS = 1e-6
MLA_HEADS = 8
QK_NOPE = 128
QK_ROPE = 64
V_DIM = 128
ROPE_THETA = 10000.0
SB_HEADS = 8
SB_HEAD_DIM = 128
TOP_K = 2

LANES = 128
QK_PAD = 256
ROPE_PACK = LANES // (QK_ROPE // 2)
VMEM_LIMIT = 48 * 1024 * 1024
VMEM_LIMIT_LARGE = 58 * 1024 * 1024
TOKEN_TILE = 512
MLA_TILE = 512
MLA_BLOCKS_PER_TRIP = 4
SB_TILE = 512
SB_BLOCKS_PER_TRIP = 2
SB_ZERO_LOG2 = -140.0
FFN_CHUNK = 1792
SC_CORES = 2
SC_SUBCORES = 16
SC_WINDOW = 128
COMBINE_SLICES = 4
SC_ROW_PARTS = 2


def _cparams(sem, vmem_limit=VMEM_LIMIT):
    return pltpu.CompilerParams(dimension_semantics=sem, vmem_limit_bytes=vmem_limit)


def _rms(x, g):
    return x * lax.rsqrt(jnp.mean(x * x, axis=-1, keepdims=True) + EPS) * g


def _modulate(h, shift, scale):
    return h * (1.0 + scale) + shift


def _split_bf16(a):
    hi = a.astype(BF16)
    lo = (a - hi.astype(F32)).astype(BF16)
    return hi, lo


U32 = jnp.uint32
HIGH_HALF = 0xFFFF0000


def _pack_bf16_pairs(a):
    half = a.shape[1] // 2
    bits = lambda v: lax.bitcast_convert_type(v.astype(BF16).astype(F32), U32)
    return (bits(a[:, :half]) >> 16) | (bits(a[:, half:]) & U32(HIGH_HALF))


def _unpack_bf16_pairs(w):
    lo = lax.bitcast_convert_type(w << 16, F32)
    hi = lax.bitcast_convert_type(w & U32(HIGH_HALF), F32)
    return jnp.concatenate([lo, hi], axis=1)


def _dot(a, b):
    return jnp.dot(a, b, preferred_element_type=F32)


def _dot_nt(a, b):
    return lax.dot_general(a, b, (((1,), (1,)), ((), ())), preferred_element_type=F32)


def _modvec_kernel(c_ref, w_ref, b_ref, o_ref):
    c = c_ref[...]
    sc = c * (1.0 / (1.0 + jnp.exp(-c)))
    a_hi, a_lo = _split_bf16(sc)
    w_hi, w_lo = _split_bf16(w_ref[...])
    o_ref[...] = _dot(a_hi, w_hi) + _dot(a_lo, w_hi) + _dot(a_hi, w_lo) + b_ref[...]


def _modvec(c, w, layer, b, tn=512):
    bsz, d = c.shape
    n = w.shape[2]
    return pl.pallas_call(
        _modvec_kernel,
        out_shape=jax.ShapeDtypeStruct((bsz, n), F32),
        grid=(n // tn,),
        in_specs=[pl.BlockSpec((bsz, d), lambda j: (0, 0)),
                  pl.BlockSpec((None, d, tn), lambda j: (layer, 0, j)),
                  pl.BlockSpec((1, tn), lambda j: (0, j))],
        out_specs=pl.BlockSpec((bsz, tn), lambda j: (0, j)),
        compiler_params=_cparams(("arbitrary",)),
        name="modvec",
    )(c, w, b.reshape(1, n))


def _mod_spec(chunk, tiles_per_batch, d):
    return pl.BlockSpec((None, None, 1, d), lambda i, *_: (i // tiles_per_batch, chunk, 0, 0))


def _mla_proj_kernel(x_ref, pos_ref, sh_ref, sc_ref, g_ref, wd_ref, gq_ref, gkv_ref,
                     wqn_ref, wqr_ref, wkn_ref, wv_ref, inv_ref,
                     q_ref, k_ref, v_ref, *, q_lora, kv_lora):
    x = x_ref[...]
    h = _modulate(_rms(x, g_ref[...]), sh_ref[...], sc_ref[...]).astype(BF16)
    lat = _dot(h, wd_ref[...])
    c_q = _rms(lat[:, :q_lora], gq_ref[...]).astype(BF16)
    c_kv = _rms(lat[:, q_lora:q_lora + kv_lora], gkv_ref[...]).astype(BF16)
    k_rot = lat[:, q_lora + kv_lora:]

    half = QK_ROPE // 2
    ang = pos_ref[...].astype(F32) * inv_ref[...]
    cos_p = jnp.cos(ang)
    sin_p = jnp.sin(ang)
    lane = lax.broadcasted_iota(jnp.int32, ang.shape, 1)
    cos, s_a, s_b = [], [], []
    for g in range(ROPE_PACK):
        shift = (LANES - g * half) % LANES
        cg = pltpu.roll(cos_p, shift, axis=1) if shift else cos_p
        sg = pltpu.roll(sin_p, shift, axis=1) if shift else sin_p
        cos.append(jnp.where(lane < half, cg, pltpu.roll(cg, half, axis=1)))
        s_a.append(jnp.where(lane < half, -sg, 0.0))
        s_b.append(jnp.where((lane >= half) & (lane < 2 * half), pltpu.roll(sg, half, axis=1), 0.0))
    cos, s_a, s_b = (jnp.concatenate(v, axis=0) for v in (cos, s_a, s_b))

    def rope(r):
        return (r * cos + pltpu.roll(r, LANES - half, axis=1) * s_a
                + pltpu.roll(r, half, axis=1) * s_b)

    k_rot = rope(k_rot).astype(BF16)
    q_nope = _dot(c_q, wqn_ref[...])
    q_rope = _dot(c_q, wqr_ref[...])
    k_nope = _dot(c_kv, wkn_ref[...])
    v_ref[...] = _dot(c_kv, wv_ref[...]).astype(BF16)
    for hd in range(MLA_HEADS):
        a, b = hd * LANES, (hd + 1) * LANES
        q_ref[:, hd * QK_PAD:hd * QK_PAD + LANES] = q_nope[:, a:b].astype(BF16)
        q_ref[:, hd * QK_PAD + LANES:(hd + 1) * QK_PAD] = rope(q_rope[:, a:b]).astype(BF16)
        k_ref[:, hd * QK_PAD:hd * QK_PAD + LANES] = k_nope[:, a:b].astype(BF16)
        k_ref[:, hd * QK_PAD + LANES:(hd + 1) * QK_PAD] = k_rot


def _mla_proj(x, pos, mod, g_mix, wd, gq, gkv, wqn, wqr, wkn, wv, inv, *, seq, tm):
    n, d = x.shape
    tpb = seq // tm
    q_lora, kv_lora = gq.shape[1], gkv.shape[1]
    hq = MLA_HEADS * QK_PAD
    hv = MLA_HEADS * V_DIM
    full = lambda a: pl.BlockSpec(a.shape, lambda i: (0,) * a.ndim)
    row = lambda w: pl.BlockSpec((tm, w), lambda i: (i, 0))
    return pl.pallas_call(
        functools.partial(_mla_proj_kernel, q_lora=q_lora, kv_lora=kv_lora),
        out_shape=(jax.ShapeDtypeStruct((n, hq), BF16), jax.ShapeDtypeStruct((n, hq), BF16),
                   jax.ShapeDtypeStruct((n, hv), BF16)),
        grid=(n // tm,),
        in_specs=[row(d), pl.BlockSpec((tm // ROPE_PACK, LANES), lambda i: (i, 0)), _mod_spec(0, tpb, d), _mod_spec(1, tpb, d), full(g_mix), full(wd),
                  full(gq), full(gkv), full(wqn), full(wqr), full(wkn), full(wv), full(inv)],
        out_specs=(row(hq), row(hq), row(hv)),
        compiler_params=_cparams(("parallel",)),
        name="mla_proj",
    )(x, pos, mod, mod, g_mix, wd, gq, gkv, wqn, wqr, wkn, wv, inv)


def _lane_chunks(a):
    return [a[:, c * LANES:(c + 1) * LANES] for c in range(a.shape[1] // LANES)]


def _for_blocks(n, step, group):
    def trip(p, carry):
        step(group * p, group)
        return carry

    lax.fori_loop(0, n // group, trip, 0)
    size = group // 2
    while size:
        @pl.when((n // size) % 2 == 1)
        def _(size=size):
            step(n // (2 * size) * (2 * size), size)
        size //= 2


def _mla_attn_kernel(q_ref, k_ref, v_ref, o_ref, s_ref, m_ref, acc_ref, *, t, scale, heads):
    i = pl.program_id(2)

    def lane_max(s):
        m = None
        for sc in _lane_chunks(s):
            m = sc if m is None else jnp.maximum(m, sc)
        return m

    def scores(h, j):
        start = pl.multiple_of(j * t, t)
        hs = slice(h * QK_PAD, (h + 1) * QK_PAD)
        return _dot_nt(q_ref[0, :, hs], k_ref[0, pl.ds(start, t), hs])

    r_minus_c = (lax.broadcasted_iota(jnp.int32, (t, t), 0) - lax.broadcasted_iota(jnp.int32, (t, t), 1))
    m_ref[...] = jnp.full(m_ref.shape, -jnp.inf, F32)

    def pass1(j, nb):
        for h in range(heads):
            m = m_ref[h]
            for jj in range(nb):
                s = jnp.where(r_minus_c >= (j + jj - i) * t, scores(h, j + jj), -jnp.inf)
                s_ref[h, j + jj] = s
                m = jnp.maximum(m, lane_max(s))
            m_ref[h] = m

    _for_blocks(i + 1, pass1, MLA_BLOCKS_PER_TRIP)
    for h in range(heads):
        m_ref[h] = jnp.broadcast_to(jnp.max(m_ref[h], axis=1, keepdims=True), (t, LANES))
    acc_ref[...] = jnp.zeros(acc_ref.shape, F32)
    cst = scale * math.log2(math.e)

    def pass2(j, nb):
        start = pl.multiple_of(j * t, t)
        ones = jnp.ones((nb * t, LANES), BF16)
        for h in range(heads):
            m = m_ref[h]
            p = jnp.concatenate([jnp.exp2((sc - m) * cst).astype(BF16)
                                 for jj in range(nb) for sc in _lane_chunks(s_ref[h, j + jj])], axis=1)
            v_ext = jnp.concatenate([v_ref[0, pl.ds(start, nb * t), h * V_DIM:(h + 1) * V_DIM], ones], axis=1)
            acc_ref[h] += _dot(p, v_ext)

    _for_blocks(i + 1, pass2, MLA_BLOCKS_PER_TRIP)
    for h in range(heads):
        acc = acc_ref[h]
        o_ref[0, :, h * V_DIM:(h + 1) * V_DIM] = (acc[:, :V_DIM] / acc[:, V_DIM:]).astype(o_ref.dtype)


def _mla_attn(q, k, v, *, t, heads=2):
    bsz, seq, _ = q.shape
    scale = 1.0 / math.sqrt(QK_NOPE + QK_ROPE)
    wq, wv = heads * QK_PAD, heads * V_DIM
    return pl.pallas_call(
        functools.partial(_mla_attn_kernel, t=t, scale=scale, heads=heads),
        out_shape=jax.ShapeDtypeStruct((bsz, seq, MLA_HEADS * V_DIM), BF16),
        grid=(bsz, MLA_HEADS // heads, seq // t),
        in_specs=[pl.BlockSpec((1, t, wq), lambda b, g, i: (b, i, g)),
                  pl.BlockSpec((1, seq, wq), lambda b, g, i: (b, 0, g)),
                  pl.BlockSpec((1, seq, wv), lambda b, g, i: (b, 0, g))],
        out_specs=pl.BlockSpec((1, t, wv), lambda b, g, i: (b, i, g)),
        scratch_shapes=[pltpu.VMEM((heads, seq // t, t, t), F32), pltpu.VMEM((heads, t, LANES), F32),
                        pltpu.VMEM((heads, t, V_DIM + LANES), F32)],
        compiler_params=_cparams(("parallel", "parallel", "arbitrary")),
        name="mla_attn",
    )(q, k, v)


def _sb_attn_kernel(q_ref, k_ref, v_ref, w_ref, o_ref, c_ref, acc_ref, kmax_ref, bound_ref, *, t, heads):
    i = pl.program_id(2)
    d = SB_HEAD_DIM
    c_ref[...] = jnp.zeros(c_ref.shape, F32)
    acc_ref[...] = jnp.zeros(acc_ref.shape, F32)
    sub = t // 2 if t % (2 * LANES) == 0 else t
    r_iota = lax.broadcasted_iota(jnp.int32, (sub, LANES), 0)
    c_iota = lax.broadcasted_iota(jnp.int32, (sub, LANES), 1)

    def block(r0, nr, start, nk, key0):
        rows = slice(r0, r0 + nr)
        masks = []
        for cc in range(nk // LANES):
            off = None if key0 is None else key0 + cc * LANES - r0
            masks.append(None if off is None or off <= -LANES else c_iota + off < r_iota)
        for h in range(heads):
            hs = slice(h * d, (h + 1) * d)
            zz = _dot_nt(q_ref[0, rows, hs], k_ref[0, pl.ds(start, nk), hs])
            zneg = -zz
            sp = jnp.log2(1.0 + jnp.exp2(jnp.minimum(zz, zneg)))
            log_1m = jnp.minimum(zneg, 0.0) - sp
            carry = c_ref[h, rows]
            a_chunks = [None] * (nk // LANES)
            for cc in reversed(range(nk // LANES)):
                cs = slice(cc * LANES, (cc + 1) * LANES)
                l1m = log_1m[:, cs] if masks[cc] is None else jnp.where(masks[cc], log_1m[:, cs], 0.0)
                hi, lo = _split_bf16(l1m)
                y = _dot(jnp.concatenate([hi, lo], axis=1), w_ref[...])
                a = jnp.exp2(zz[:, cs] + y[:, :LANES] + carry)
                carry = carry + y[:, LANES:]
                if masks[cc] is not None:
                    a = jnp.where(masks[cc], a, 0.0)
                a_chunks[cc] = a.astype(BF16)
            c_ref[h, rows] = carry
            acc_ref[h, rows] += _dot(jnp.concatenate(a_chunks, axis=1), v_ref[0, pl.ds(start, nk), hs])

    def diagonal_tile():
        for band in range(t // sub):
            block(band * sub, sub, pl.multiple_of(i * t, t), (band + 1) * sub, 0)

    def left_blocks(n, nb):
        for jj in range(nb):
            block(0, t, pl.multiple_of((i - 1 - n - jj) * t, t), t, None)

    @pl.when(i == 0)
    def _():
        diagonal_tile()

    @pl.when(i > 0)
    def _():
        diagonal_tile()
        left_blocks(0, 1)

    @pl.when(i == 0)
    def _():
        for h in range(heads):
            kmax = jnp.max(jnp.abs(k_ref[0, :, h * d:(h + 1) * d].astype(F32)))
            kmax_ref[h] = jnp.full(kmax_ref.shape[1:], kmax, F32)

    for h in range(heads):
        q_l1 = jnp.sum(jnp.abs(q_ref[0, :, h * d:(h + 1) * d].astype(F32)), axis=1, keepdims=True)
        bound_ref[h] = q_l1 * kmax_ref[h, 0:1, :]

    def more_to_add():
        worst = None
        for h in range(heads):
            top = jnp.max(c_ref[h] + bound_ref[h])
            worst = top if worst is None else jnp.maximum(worst, top)
        return worst >= SB_ZERO_LOG2

    def trip(state):
        n, _ = state
        left_blocks(n, SB_BLOCKS_PER_TRIP)
        return n + SB_BLOCKS_PER_TRIP, more_to_add()

    n, go = lax.while_loop(lambda s: s[1] & (s[0] + SB_BLOCKS_PER_TRIP <= i), trip,
                           (jnp.int32(1), (i > 0) & more_to_add()))
    size = SB_BLOCKS_PER_TRIP // 2
    while size:
        take = go & (n + size <= i)

        @pl.when(take)
        def _(n=n, size=size):
            left_blocks(n, size)

        n = n + jnp.where(take, size, 0)
        size //= 2
    for h in range(heads):
        o_ref[0, :, h * d:(h + 1) * d] = acc_ref[h].astype(o_ref.dtype)


def _sb_attn(q, kv, *, t, heads=4):
    bsz, seq, _ = q.shape
    w = heads * SB_HEAD_DIM
    groups = SB_HEADS // heads
    tri = (jnp.arange(LANES)[:, None] >= jnp.arange(LANES)[None, :]).astype(BF16)
    half = jnp.concatenate([tri, jnp.ones((LANES, LANES), BF16)], axis=1)
    w_sum = jnp.concatenate([half, half], axis=0)
    return pl.pallas_call(
        functools.partial(_sb_attn_kernel, t=t, heads=heads),
        out_shape=jax.ShapeDtypeStruct((bsz, seq, SB_HEADS * SB_HEAD_DIM), BF16),
        grid=(bsz, groups, seq // t),
        in_specs=[pl.BlockSpec((1, t, w), lambda b, g, i: (b, i, g)),
                  pl.BlockSpec((1, seq, w), lambda b, g, i: (b, 0, g)),
                  pl.BlockSpec((1, seq, w), lambda b, g, i: (b, 0, groups + g)),
                  pl.BlockSpec(w_sum.shape, lambda b, g, i: (0, 0))],
        out_specs=pl.BlockSpec((1, t, w), lambda b, g, i: (b, i, g)),
        scratch_shapes=[pltpu.VMEM((heads, t, LANES), F32), pltpu.VMEM((heads, t, SB_HEAD_DIM), F32),
                        pltpu.VMEM((heads, 8, LANES), F32), pltpu.VMEM((heads, t, LANES), F32)],
        compiler_params=_cparams(("parallel", "parallel", "arbitrary")),
        name="sb_attn",
    )(q, kv, kv, w_sum)


def _out_proj_kernel(o_ref, w_ref, x_ref, gt_ref, g_ref, sh_ref, sc_ref, wr_hi_ref, wr_lo_ref, br_ref, tri_ref,
                     xo_ref, h_ref, rt_ref, cnt_ref, *, n_experts):
    x = x_ref[...] + gt_ref[...] * _dot(o_ref[...], w_ref[...])
    xo_ref[...] = x
    h = _modulate(_rms(x, g_ref[...]), sh_ref[...], sc_ref[...])
    h_hi = h.astype(BF16)
    words = _pack_bf16_pairs(h)
    for part in range(SC_ROW_PARTS):
        h_ref[part] = words[:, part * h_ref.shape[2]:(part + 1) * h_ref.shape[2]]
    h_lo = (h - h_hi.astype(F32)).astype(BF16)
    both = _dot(h_hi, jnp.concatenate([wr_hi_ref[...], wr_lo_ref[...]], axis=1))
    logits = both[:, :LANES] + both[:, LANES:] + _dot(h_lo, wr_hi_ref[...]) + br_ref[...]
    lane = lax.broadcasted_iota(jnp.int32, logits.shape, 1).astype(F32)
    lg = jnp.where(lane < n_experts, logits, -jnp.inf)
    m1 = jnp.max(lg, axis=1, keepdims=True)
    i1 = jnp.min(jnp.where(lg == m1, lane, float(LANES)), axis=1, keepdims=True)
    lg2 = jnp.where(lane == i1, -jnp.inf, lg)
    m2 = jnp.max(lg2, axis=1, keepdims=True)
    i2 = jnp.min(jnp.where(lg2 == m2, lane, float(LANES)), axis=1, keepdims=True)
    e2 = jnp.exp(m2 - m1)
    den = 1.0 + e2
    sel = jnp.where((lane == i1) | (lane == i2), 1.0, 0.0)
    prefix = _dot(tri_ref[...], sel.astype(BF16))
    r1 = jnp.sum(jnp.where(lane == i1, prefix, 0.0), axis=1, keepdims=True)
    r2 = jnp.sum(jnp.where(lane == i2, prefix, 0.0), axis=1, keepdims=True)
    cnt_ref[...] = jnp.sum(sel, axis=0, keepdims=True)
    rt = jnp.zeros_like(logits)
    for k, val in enumerate((i1, i2, 1.0 / den, e2 / den, r1, r2)):
        rt = jnp.where(lane == k, val, rt)
    rt_ref[...] = rt


def _out_proj_router(o, w, x, mod, g, wr_hi, wr_lo, br, n_experts, *, seq, tm):
    n, d = x.shape
    tpb = seq // tm
    dpp = d // 2 // SC_ROW_PARTS
    tri = (jnp.arange(tm)[:, None] > jnp.arange(tm)[None, :]).astype(BF16)
    full = lambda a: pl.BlockSpec(a.shape, lambda i: (0,) * a.ndim)
    row = lambda wd: pl.BlockSpec((tm, wd), lambda i: (i, 0))
    return pl.pallas_call(
        functools.partial(_out_proj_kernel, n_experts=n_experts),
        out_shape=(jax.ShapeDtypeStruct((n, d), F32), jax.ShapeDtypeStruct((SC_ROW_PARTS, n, dpp), U32),
                   jax.ShapeDtypeStruct((n, LANES), F32), jax.ShapeDtypeStruct((n // tm, 1, LANES), F32)),
        grid=(n // tm,),
        in_specs=[row(o.shape[1]), full(w), row(d), _mod_spec(2, tpb, d), full(g),
                  _mod_spec(3, tpb, d), _mod_spec(4, tpb, d), full(wr_hi), full(wr_lo), full(br), full(tri)],
        out_specs=(row(d), pl.BlockSpec((SC_ROW_PARTS, tm, dpp), lambda i: (0, i, 0)), row(LANES),
                   pl.BlockSpec((None, 1, LANES), lambda i: (i, 0, 0))),
        compiler_params=_cparams(("parallel",)),
        name="out_proj_router",
    )(o, w, x, mod, g, mod, mod, wr_hi, wr_lo, br, tri)


def _silu(g):
    return g * (1.0 / (1.0 + jnp.exp(-g)))


def _ffn_kernel(o_ref, wo_ref, x_ref, gt1_ref, gf_ref, sh2_ref, sc2_ref, wg_ref, wu_ref, wd_ref, gt2_ref,
                gkv_ref, shk_ref, sck_ref, gm_ref, shm_ref, scm_ref,
                xo_ref, hk_ref, hm_ref, x1_ref, h_ref, acc_ref):
    f = pl.program_id(1)

    @pl.when(f == 0)
    def _():
        x1 = x_ref[...] + gt1_ref[...] * _dot(o_ref[...], wo_ref[...])
        x1_ref[...] = x1
        h_ref[...] = _modulate(_rms(x1, gf_ref[...]), sh2_ref[...], sc2_ref[...]).astype(BF16)
        acc_ref[...] = jnp.zeros(acc_ref.shape, F32)

    h = h_ref[...]
    a = _silu(_dot(h, wg_ref[...])) * _dot(h, wu_ref[...])
    acc_ref[...] += _dot(a.astype(BF16), wd_ref[...])

    @pl.when(f == pl.num_programs(1) - 1)
    def _():
        x = x1_ref[...] + gt2_ref[...] * acc_ref[...]
        xo_ref[...] = x
        y = x * lax.rsqrt(jnp.mean(x * x, axis=-1, keepdims=True) + EPS)
        hk_ref[...] = _modulate(y * gkv_ref[...], shk_ref[...], sck_ref[...]).astype(BF16)
        hm_ref[...] = _modulate(y * gm_ref[...], shm_ref[...], scm_ref[...]).astype(BF16)


def _ffn(o, w_o, x, mod0, g_ffn, w_gu, w_down, g_kv, mod_kv, g_mix1, mod1, *, seq, tm, tf):
    n, d = x.shape
    d_ff = w_down.shape[0]
    nf = d_ff // tf
    tpb = seq // tm
    full = lambda a: pl.BlockSpec(a.shape, lambda i, f: (0,) * a.ndim)
    row = lambda w=d: pl.BlockSpec((tm, w), lambda i, f: (i, 0))
    return pl.pallas_call(
        _ffn_kernel,
        out_shape=(jax.ShapeDtypeStruct((n, d), F32), jax.ShapeDtypeStruct((n, d), BF16),
                   jax.ShapeDtypeStruct((n, d), BF16)),
        grid=(n // tm, nf),
        in_specs=[row(o.shape[1]), full(w_o), row(), _mod_spec(2, tpb, d), full(g_ffn),
                  _mod_spec(3, tpb, d), _mod_spec(4, tpb, d),
                  pl.BlockSpec((d, tf), lambda i, f: (0, f)),
                  pl.BlockSpec((d, tf), lambda i, f: (0, nf + f)),
                  pl.BlockSpec((tf, d), lambda i, f: (f, 0)),
                  _mod_spec(5, tpb, d),
                  full(g_kv), _mod_spec(0, tpb, d), _mod_spec(1, tpb, d),
                  full(g_mix1), _mod_spec(0, tpb, d), _mod_spec(1, tpb, d)],
        out_specs=(row(), row(), row()),
        scratch_shapes=[pltpu.VMEM((tm, d), F32), pltpu.VMEM((tm, d), BF16), pltpu.VMEM((tm, d), F32)],
        compiler_params=_cparams(("parallel", "arbitrary"), vmem_limit=VMEM_LIMIT_LARGE),
        name="ffn_dense",
    )(o, w_o, x, mod0, g_ffn, mod0, mod0, w_gu, w_gu, w_down, mod0,
      g_kv, mod_kv, mod_kv, g_mix1, mod1, mod1)


def _linear_kernel(x_ref, w_ref, o_ref):
    o_ref[...] = _dot(x_ref[...], w_ref[...]).astype(o_ref.dtype)


def _linear(x, w, *, tm, tn, name):
    n, k = x.shape
    m = w.shape[1]
    return pl.pallas_call(
        _linear_kernel,
        out_shape=jax.ShapeDtypeStruct((n, m), BF16),
        grid=(m // tn, n // tm),
        in_specs=[pl.BlockSpec((tm, k), lambda j, i: (i, 0)), pl.BlockSpec((k, tn), lambda j, i: (0, j))],
        out_specs=pl.BlockSpec((tm, tn), lambda j, i: (i, j)),
        compiler_params=_cparams(("parallel", "parallel")),
        name=name,
    )(x, w)


def _route_plan(counts, n_exp, tm, n_tiles):
    cnt = counts[:, 0, :n_exp].astype(jnp.int32)
    sizes = jnp.sum(cnt, axis=0)
    padded = (sizes + tm - 1) // tm * tm
    ends = jnp.cumsum(padded)
    tile_base = (ends - padded)[None, :] + jnp.cumsum(cnt, axis=0) - cnt
    tile_start = jnp.arange(n_tiles, dtype=jnp.int32) * tm
    tile_expert = jnp.minimum(jnp.sum(tile_start[:, None] >= ends[None, :], axis=1), n_exp - 1)
    n_used = (ends[-1] // tm).reshape(1)
    n_valid = jnp.clip((ends - padded + sizes)[tile_expert] - tile_start, 0, tm)
    return tile_base, tile_expert.astype(jnp.int32), n_used.astype(jnp.int32), n_valid.astype(jnp.int32)


PICK_ROWS = 8


def _picks_kernel(rt_ref, base_ref, o_ref, *, n_rows):
    rt = rt_ref[...]
    lane = lax.broadcasted_iota(jnp.int32, rt.shape, 1).astype(F32)
    cols = jnp.zeros_like(rt)
    for k in range(TOP_K):
        e_k = rt[:, k:k + 1]
        r_k = rt[:, 2 * TOP_K + k:2 * TOP_K + k + 1]
        d_k = jnp.sum(jnp.where(lane == e_k, base_ref[...], 0.0), axis=1, keepdims=True) + r_k
        for p in range(SC_ROW_PARTS):
            cols = jnp.where(lane == p * TOP_K + k, d_k + float(p * n_rows), cols)
    o_ref[...] = cols.T[:PICK_ROWS].astype(jnp.int32)


def _picks(rt, tile_base, n_rows, *, tm):
    n = rt.shape[0]
    base = _pad_last(tile_base.astype(F32), LANES).reshape(n // tm, 1, LANES)
    out = pl.pallas_call(
        functools.partial(_picks_kernel, n_rows=n_rows),
        out_shape=jax.ShapeDtypeStruct((PICK_ROWS, n), jnp.int32),
        grid=(n // tm,),
        in_specs=[pl.BlockSpec((tm, LANES), lambda i: (i, 0)),
                  pl.BlockSpec((None, 1, LANES), lambda i: (i, 0, 0))],
        out_specs=pl.BlockSpec((PICK_ROWS, tm), lambda i: (0, i)),
        compiler_params=_cparams(("parallel",)),
        name="moe_picks",
    )(rt, base)
    return out[:SC_ROW_PARTS * TOP_K].reshape(SC_ROW_PARTS, TOP_K, n)


def _sc_mesh():
    return plsc.VectorSubcoreMesh(core_axis_name="core", subcore_axis_name="subcore",
                                  num_cores=SC_CORES, num_subcores=SC_SUBCORES)


def _sc_scatter_rows(x, idx_list, n_rows):
    m, d = x.shape

    @functools.partial(pl.kernel, out_type=jax.ShapeDtypeStruct((n_rows, d), x.dtype), mesh=_sc_mesh(),
                       scratch_types=[], name="sc_scatter_rows")
    def scatter(x_hbm, *refs):
        i_hbms, o_hbm = refs[:-1], refs[-1]

        def body(x_vmem, *i_vmems):
            for i_vmem in i_vmems:
                pltpu.sync_copy(x_vmem, o_hbm.at[i_vmem.at[0]])

        pltpu.emit_pipeline(
            body,
            grid=(m // SC_WINDOW,),
            in_specs=[pl.BlockSpec((SC_WINDOW, d), index_map=lambda i: (i, 0))]
                     + [pl.BlockSpec((1, SC_WINDOW), index_map=lambda i: (0, i))] * len(idx_list),
            out_specs=[],
            core_axis_name=("core", "subcore"),
            dimension_semantics=(pltpu.PARALLEL,),
        )(x_hbm, *i_hbms)

    return scatter(x, *[idx.reshape(1, m) for idx in idx_list])


def _grouped_ffn_kernel(te_ref, nu_ref, nv_ref, x_ref, wg_ref, wu_ref, wd_ref, y_ref, hb_ref, acc_ref):
    del te_ref
    i = pl.program_id(0)
    f = pl.program_id(1)
    used = i < nu_ref[0]
    last_f = f == pl.num_programs(1) - 1

    @pl.when(used & (f == 0))
    def _():
        words = jnp.concatenate([x_ref[p] for p in range(SC_ROW_PARTS)], axis=1)
        row = lax.broadcasted_iota(jnp.int32, words.shape, 0)
        words = jnp.where(row < nv_ref[i], words, U32(0))
        hb_ref[...] = _unpack_bf16_pairs(words).astype(BF16)
        acc_ref[...] = jnp.zeros(acc_ref.shape, F32)

    @pl.when(used)
    def _():
        h = hb_ref[...]
        a = _silu(_dot(h, wg_ref[0])) * _dot(h, wu_ref[0])
        acc_ref[...] += _dot(a.astype(BF16), wd_ref[0])

    @pl.when(used & last_f)
    def _():
        y = _pack_bf16_pairs(acc_ref[...])
        for part in range(SC_ROW_PARTS):
            y_ref[part] = y[:, part * y_ref.shape[2]:(part + 1) * y_ref.shape[2]]

    @pl.when(jnp.logical_not(used) & last_f)
    def _():
        y_ref[...] = jnp.zeros(y_ref.shape, y_ref.dtype)


def _grouped_ffn(tile_expert, n_used, n_valid, xg, w_gu, w_down, *, tm, tf):
    parts, n_rows, dpp = xg.shape
    n_exp, d_ff, d = w_down.shape
    nf = d_ff // tf

    def wspec(shape, index):
        def index_map(i, f, te, nu, nv):
            return index(te[i], jnp.where(i < nu[0], f, nf - 1))
        return pl.BlockSpec(shape, index_map)

    rows = pl.BlockSpec((parts, tm, dpp), lambda i, f, te, nu, nv: (0, i, 0))
    return pl.pallas_call(
        _grouped_ffn_kernel,
        out_shape=jax.ShapeDtypeStruct(xg.shape, xg.dtype),
        grid_spec=pltpu.PrefetchScalarGridSpec(
            num_scalar_prefetch=3,
            grid=(n_rows // tm, nf),
            in_specs=[rows,
                      wspec((1, d, tf), lambda e, f: (e, 0, f)),
                      wspec((1, d, tf), lambda e, f: (e, 0, nf + f)),
                      wspec((1, tf, d), lambda e, f: (e, f, 0))],
            out_specs=rows,
            scratch_shapes=[pltpu.VMEM((tm, d), BF16), pltpu.VMEM((tm, d), F32)]),
        compiler_params=_cparams(("arbitrary", "arbitrary")),
        name="moe_grouped_ffn",
    )(tile_expert, n_used, n_valid, xg, w_gu, w_gu, w_down)


def _sc_gather_rows(table, idx):
    n_idx = idx.shape[0]
    d = table.shape[1]

    @functools.partial(pl.kernel, out_type=jax.ShapeDtypeStruct((n_idx, d), table.dtype), mesh=_sc_mesh(),
                       name="sc_gather_rows")
    def gather(t_hbm, i_hbm, o_hbm):
        def body(i_vmem, o_vmem):
            pltpu.sync_copy(t_hbm.at[i_vmem.at[0]], o_vmem)

        pltpu.emit_pipeline(
            body,
            grid=(n_idx // SC_WINDOW,),
            in_specs=[pl.BlockSpec((1, SC_WINDOW), index_map=lambda i: (0, i))],
            out_specs=[pl.BlockSpec((SC_WINDOW, d), index_map=lambda i: (i, 0))],
            core_axis_name=("core", "subcore"),
            dimension_semantics=(pltpu.PARALLEL,),
        )(i_hbm, o_hbm)

    return gather(table, idx.reshape(1, n_idx))


def _combine_kernel(x_ref, rt_ref, gt_ref, gf_ref, *refs):
    y_refs, o_ref = refs[:SC_ROW_PARTS * TOP_K], refs[-1]
    rt = rt_ref[...]
    tot = None
    for k in range(TOP_K):
        words = jnp.concatenate([y_refs[p * TOP_K + k][...] for p in range(SC_ROW_PARTS)], axis=1)
        term = rt[:, TOP_K + k:TOP_K + k + 1] * _unpack_bf16_pairs(words)
        tot = term if tot is None else tot + term
    x = x_ref[...] + gt_ref[...] * tot
    o_ref[...] = _rms(x, gf_ref[...])


def _combine(x, rt, mod1, g_final, ysel, prev, *, seq, tm, tile0):
    n, d = x.shape
    tpb = seq // tm
    n_pieces = SC_ROW_PARTS * TOP_K
    nt = ysel.shape[0] // n_pieces // tm
    row = lambda w: pl.BlockSpec((tm, w), lambda i: (tile0 + i, 0))
    piece = lambda j: pl.BlockSpec((tm, ysel.shape[1]), lambda i: (j * nt + i, 0))
    gate = pl.BlockSpec((None, None, 1, d), lambda i: ((tile0 + i) // tpb, 5, 0, 0))
    in_specs = ([row(d), row(LANES), gate, pl.BlockSpec(g_final.shape, lambda i: (0, 0))]
                + [piece(j) for j in range(n_pieces)])
    args = [x, rt, mod1, g_final] + [ysel] * n_pieces
    aliases = {}
    if prev is not None:
        in_specs.append(pl.BlockSpec(memory_space=pl.ANY))
        aliases = {len(args): 0}
        args.append(prev)
    return pl.pallas_call(
        _combine_kernel,
        out_shape=jax.ShapeDtypeStruct((n, d), F32),
        grid=(nt,),
        in_specs=in_specs,
        out_specs=row(d),
        input_output_aliases=aliases,
        compiler_params=_cparams(("parallel",)),
        name="moe_combine",
    )(*args)


def _pad_last(a, width):
    return jnp.pad(a, [(0, 0)] * (a.ndim - 1) + [(0, width - a.shape[-1])])


def kernel(x, c, positions, w_mod, b_mod, g_mix, g_ffn, w_a_down, g_q_lat, g_kv_lat, w_uq, w_ukv, w_oa,
           w_mod_kv, b_mod_kv, g_kv, w_kv_sb, w_q_sb, w_o_sb, w_ffn_gu, w_ffn_down, w_router, b_router,
           w_exp_gu, w_exp_down, g_final):
    bsz, seq, d = x.shape
    n = bsz * seq
    q_lora, kv_lora = g_q_lat.shape[1], g_kv_lat.shape[1]
    n_exp = w_router.shape[-1]
    d_ff = w_ffn_down.shape[1]
    tm = min(TOKEN_TILE, seq)
    t_mla = min(MLA_TILE, seq)
    t_sb = min(SB_TILE, seq)
    tf = next((t for t in (FFN_CHUNK, 512) if d_ff % t == 0), d_ff)

    mod0 = _modvec(c, w_mod, 0, b_mod[0]).reshape(bsz, 6, 1, d)
    mod1 = _modvec(c, w_mod, 1, b_mod[1]).reshape(bsz, 6, 1, d)
    mod_kv = _modvec(c, w_mod_kv[None], 0, b_mod_kv).reshape(bsz, 2, 1, d)

    lat_w = q_lora + kv_lora + LANES
    wd = _pad_last(w_a_down[0], lat_w).astype(BF16)
    wq = w_uq[0].reshape(q_lora, MLA_HEADS, QK_NOPE + QK_ROPE)
    wqn = wq[:, :, :QK_NOPE].reshape(q_lora, MLA_HEADS * QK_NOPE).astype(BF16)
    wqr = _pad_last(wq[:, :, QK_NOPE:], LANES).reshape(q_lora, MLA_HEADS * LANES).astype(BF16)
    wkv = w_ukv[0].reshape(kv_lora, MLA_HEADS, QK_NOPE + V_DIM)
    wkn = wkv[:, :, :QK_NOPE].reshape(kv_lora, MLA_HEADS * QK_NOPE).astype(BF16)
    wv = wkv[:, :, QK_NOPE:].reshape(kv_lora, MLA_HEADS * V_DIM).astype(BF16)
    half = QK_ROPE // 2
    inv = ROPE_THETA ** (-jnp.arange(half, dtype=F32) / half)
    inv = jnp.tile(inv, ROPE_PACK).reshape(1, LANES)
    wr = _pad_last(w_router[0], LANES)
    wr_hi = wr.astype(BF16)
    wr_lo = (wr - wr_hi.astype(F32)).astype(BF16)
    br = _pad_last(b_router[0], LANES).reshape(1, LANES)

    xf = x.reshape(n, d)
    pos = positions.reshape(n // tm, ROPE_PACK, tm // ROPE_PACK).swapaxes(1, 2)
    pos = jnp.repeat(pos.reshape(n // ROPE_PACK, ROPE_PACK), QK_ROPE // 2, axis=1)
    row1 = lambda a: a.reshape(1, -1)

    q, k, v = _mla_proj(xf, pos, mod0, row1(g_mix[0]), wd, row1(g_q_lat[0]), row1(g_kv_lat[0]),
                        wqn, wqr, wkn, wv, inv, seq=seq, tm=tm)
    o = _mla_attn(q.reshape(bsz, seq, -1), k.reshape(bsz, seq, -1), v.reshape(bsz, seq, -1), t=t_mla)
    x2, hk, hm = _ffn(o.reshape(n, -1), w_oa[0].astype(BF16), xf, mod0, row1(g_ffn[0]),
                      w_ffn_gu[0].astype(BF16), w_ffn_down[0].astype(BF16),
                      row1(g_kv), mod_kv, row1(g_mix[1]), mod1, seq=seq, tm=tm, tf=tf)
    kv = _linear(hk, w_kv_sb.astype(BF16), tm=tm, tn=w_kv_sb.shape[1], name="kv_proj")
    q_scale = math.log2(math.e) / math.sqrt(SB_HEAD_DIM)
    qs = _linear(hm, (w_q_sb[0] * q_scale).astype(BF16), tm=tm, tn=1024, name="q_proj")
    o = _sb_attn(qs.reshape(bsz, seq, -1), kv.reshape(bsz, seq, -1), t=t_sb)
    x3, h, rt, counts = _out_proj_router(o.reshape(n, -1), w_o_sb[0].astype(BF16), x2, mod1, row1(g_ffn[1]),
                                         wr_hi, wr_lo, br, n_exp, seq=seq, tm=tm)
    n_tiles = TOP_K * n // tm + n_exp
    n_rows = n_tiles * tm
    tile_base, tile_expert, n_used, n_valid = _route_plan(counts, n_exp, tm, n_tiles)
    picks = _picks(rt, tile_base, n_rows, tm=tm)
    xg = _sc_scatter_rows(h.reshape(SC_ROW_PARTS * n, -1), [picks[:, k].reshape(-1) for k in range(TOP_K)],
                          SC_ROW_PARTS * n_rows)
    yg = _grouped_ffn(tile_expert, n_used, n_valid, xg.reshape(SC_ROW_PARTS, n_rows, -1),
                      w_exp_gu[0].astype(BF16), w_exp_down[0].astype(BF16), tm=tm, tf=tf)
    slices = COMBINE_SLICES if (n // tm) % COMBINE_SLICES == 0 else 1
    per = n // tm // slices
    out = None
    for sl in range(slices):
        tok = slice(sl * per * tm, (sl + 1) * per * tm)
        ysel = _sc_gather_rows(yg.reshape(SC_ROW_PARTS * n_rows, -1), picks[:, :, tok].reshape(-1))
        out = _combine(x3, rt, mod1, row1(g_final), ysel, out, seq=seq, tm=tm, tile0=sl * per)
    return out.reshape(bsz, seq, d)
```

```python
import functools
import math

import jax
import jax.numpy as jnp
from jax import lax
from jax.experimental import pallas as pl
from jax.experimental.pallas import tpu as pltpu
from jax.experimental.pallas import tpu_sc as plsc

F32 = jnp.float32
BF16 = jnp.bfloat16

EPS = 1e-6
MLA_HEADS = 8
QK_NOPE = 128
QK_ROPE = 64
V_DIM = 128
ROPE_THETA = 10000.0
SB_HEADS = 8
SB_HEAD_DIM = 128
TOP_K = 2

LANES = 128
QK_PAD = 256
ROPE_PACK = LANES // (QK_ROPE // 2)
VMEM_LIMIT = 48 * 1024 * 1024
VMEM_LIMIT_LARGE = 58 * 1024 * 1024
TOKEN_TILE = 512
MLA_TILE = 512
MLA_BLOCKS_PER_TRIP = 4
SB_TILE = 512
SB_BLOCKS_PER_TRIP = 2
SB_ZERO_LOG2 = -140.0
FFN_CHUNK = 1792
SC_CORES = 2
SC_SUBCORES = 16
SC_WINDOW = 128
SC_ROW_PARTS = 2


def _cparams(sem, vmem_limit=VMEM_LIMIT):
    return pltpu.CompilerParams(dimension_semantics=sem, vmem_limit_bytes=vmem_limit)


def _rms(x, g):
    return x * lax.rsqrt(jnp.mean(x * x, axis=-1, keepdims=True) + EPS) * g


def _modulate(h, shift, scale):
    return h * (1.0 + scale) + shift


def _split_bf16(a):
    hi = a.astype(BF16)
    lo = (a - hi.astype(F32)).astype(BF16)
    return hi, lo


U32 = jnp.uint32
HIGH_HALF = 0xFFFF0000


def _pack_bf16_pairs(a):
    half = a.shape[1] // 2
    bits = lambda v: lax.bitcast_convert_type(v.astype(BF16).astype(F32), U32)
    return (bits(a[:, :half]) >> 16) | (bits(a[:, half:]) & U32(HIGH_HALF))


def _unpack_bf16_pairs(w):
    lo = lax.bitcast_convert_type(w << 16, F32)
    hi = lax.bitcast_convert_type(w & U32(HIGH_HALF), F32)
    return jnp.concatenate([lo, hi], axis=1)


def _dot(a, b):
    return jnp.dot(a, b, preferred_element_type=F32)


def _dot_nt(a, b):
    return lax.dot_general(a, b, (((1,), (1,)), ((), ())), preferred_element_type=F32)


def _modvec_kernel(c_ref, w_ref, b_ref, o_ref):
    c = c_ref[...]
    sc = c * (1.0 / (1.0 + jnp.exp(-c)))
    a_hi, a_lo = _split_bf16(sc)
    w_hi, w_lo = _split_bf16(w_ref[...])
    o_ref[...] = _dot(a_hi, w_hi) + _dot(a_lo, w_hi) + _dot(a_hi, w_lo) + b_ref[...]


def _modvec(c, w, layer, b, tn=512):
    bsz, d = c.shape
    n = w.shape[2]
    return pl.pallas_call(
        _modvec_kernel,
        out_shape=jax.ShapeDtypeStruct((bsz, n), F32),
        grid=(n // tn,),
        in_specs=[pl.BlockSpec((bsz, d), lambda j: (0, 0)),
                  pl.BlockSpec((None, d, tn), lambda j: (layer, 0, j)),
                  pl.BlockSpec((1, tn), lambda j: (0, j))],
        out_specs=pl.BlockSpec((bsz, tn), lambda j: (0, j)),
        compiler_params=_cparams(("arbitrary",)),
        name="modvec",
    )(c, w, b.reshape(1, n))


def _mod_spec(chunk, tiles_per_batch, d):
    return pl.BlockSpec((None, None, 1, d), lambda i, *_: (i // tiles_per_batch, chunk, 0, 0))


def _mla_proj_kernel(x_ref, pos_ref, sh_ref, sc_ref, g_ref, wd_ref, gq_ref, gkv_ref,
                     wqn_ref, wqr_ref, wkn_ref, wv_ref, inv_ref,
                     q_ref, k_ref, v_ref, *, q_lora, kv_lora):
    x = x_ref[...]
    h = _modulate(_rms(x, g_ref[...]), sh_ref[...], sc_ref[...]).astype(BF16)
    lat = _dot(h, wd_ref[...])
    c_q = _rms(lat[:, :q_lora], gq_ref[...]).astype(BF16)
    c_kv = _rms(lat[:, q_lora:q_lora + kv_lora], gkv_ref[...]).astype(BF16)
    k_rot = lat[:, q_lora + kv_lora:]

    half = QK_ROPE // 2
    ang = pos_ref[...].astype(F32) * inv_ref[...]
    cos_p = jnp.cos(ang)
    sin_p = jnp.sin(ang)
    lane = lax.broadcasted_iota(jnp.int32, ang.shape, 1)
    cos, s_a, s_b = [], [], []
    for g in range(ROPE_PACK):
        shift = (LANES - g * half) % LANES
        cg = pltpu.roll(cos_p, shift, axis=1) if shift else cos_p
        sg = pltpu.roll(sin_p, shift, axis=1) if shift else sin_p
        cos.append(jnp.where(lane < half, cg, pltpu.roll(cg, half, axis=1)))
        s_a.append(jnp.where(lane < half, -sg, 0.0))
        s_b.append(jnp.where((lane >= half) & (lane < 2 * half), pltpu.roll(sg, half, axis=1), 0.0))
    cos, s_a, s_b = (jnp.concatenate(v, axis=0) for v in (cos, s_a, s_b))

    def rope(r):
        return (r * cos + pltpu.roll(r, LANES - half, axis=1) * s_a
                + pltpu.roll(r, half, axis=1) * s_b)

    k_rot = rope(k_rot).astype(BF16)
    q_nope = _dot(c_q, wqn_ref[...])
    q_rope = _dot(c_q, wqr_ref[...])
    k_nope = _dot(c_kv, wkn_ref[...])
    v_ref[...] = _dot(c_kv, wv_ref[...]).astype(BF16)
    for hd in range(MLA_HEADS):
        a, b = hd * LANES, (hd + 1) * LANES
        q_ref[:, hd * QK_PAD:hd * QK_PAD + LANES] = q_nope[:, a:b].astype(BF16)
        q_ref[:, hd * QK_PAD + LANES:(hd + 1) * QK_PAD] = rope(q_rope[:, a:b]).astype(BF16)
        k_ref[:, hd * QK_PAD:hd * QK_PAD + LANES] = k_nope[:, a:b].astype(BF16)
        k_ref[:, hd * QK_PAD + LANES:(hd + 1) * QK_PAD] = k_rot


def _mla_proj(x, pos, mod, g_mix, wd, gq, gkv, wqn, wqr, wkn, wv, inv, *, seq, tm):
    n, d = x.shape
    tpb = seq // tm
    q_lora, kv_lora = gq.shape[1], gkv.shape[1]
    hq = MLA_HEADS * QK_PAD
    hv = MLA_HEADS * V_DIM
    full = lambda a: pl.BlockSpec(a.shape, lambda i: (0,) * a.ndim)
    row = lambda w: pl.BlockSpec((tm, w), lambda i: (i, 0))
    return pl.pallas_call(
        functools.partial(_mla_proj_kernel, q_lora=q_lora, kv_lora=kv_lora),
        out_shape=(jax.ShapeDtypeStruct((n, hq), BF16), jax.ShapeDtypeStruct((n, hq), BF16),
                   jax.ShapeDtypeStruct((n, hv), BF16)),
        grid=(n // tm,),
        in_specs=[row(d), pl.BlockSpec((tm // ROPE_PACK, LANES), lambda i: (i, 0)), _mod_spec(0, tpb, d), _mod_spec(1, tpb, d), full(g_mix), full(wd),
                  full(gq), full(gkv), full(wqn), full(wqr), full(wkn), full(wv), full(inv)],
        out_specs=(row(hq), row(hq), row(hv)),
        compiler_params=_cparams(("parallel",)),
        name="mla_proj",
    )(x, pos, mod, mod, g_mix, wd, gq, gkv, wqn, wqr, wkn, wv, inv)


def _lane_chunks(a):
    return [a[:, c * LANES:(c + 1) * LANES] for c in range(a.shape[1] // LANES)]


def _for_blocks(n, step, group):
    def trip(p, carry):
        step(group * p, group)
        return carry

    lax.fori_loop(0, n // group, trip, 0)
    size = group // 2
    while size:
        @pl.when((n // size) % 2 == 1)
        def _(size=size):
            step(n // (2 * size) * (2 * size), size)
        size //= 2


def _mla_attn_kernel(q_ref, k_ref, v_ref, o_ref, s_ref, m_ref, acc_ref, *, t, scale, heads):
    i = pl.program_id(2)

    def lane_max(s):
        m = None
        for sc in _lane_chunks(s):
            m = sc if m is None else jnp.maximum(m, sc)
        return m

    def scores(h, j):
        start = pl.multiple_of(j * t, t)
        hs = slice(h * QK_PAD, (h + 1) * QK_PAD)
        return _dot_nt(q_ref[0, :, hs], k_ref[0, pl.ds(start, t), hs])

    r_minus_c = (lax.broadcasted_iota(jnp.int32, (t, t), 0) - lax.broadcasted_iota(jnp.int32, (t, t), 1))
    m_ref[...] = jnp.full(m_ref.shape, -jnp.inf, F32)

    def pass1(j, nb):
        for h in range(heads):
            m = m_ref[h]
            for jj in range(nb):
                s = jnp.where(r_minus_c >= (j + jj - i) * t, scores(h, j + jj), -jnp.inf)
                s_ref[h, j + jj] = s
                m = jnp.maximum(m, lane_max(s))
            m_ref[h] = m

    _for_blocks(i + 1, pass1, MLA_BLOCKS_PER_TRIP)
    for h in range(heads):
        m_ref[h] = jnp.broadcast_to(jnp.max(m_ref[h], axis=1, keepdims=True), (t, LANES))
    acc_ref[...] = jnp.zeros(acc_ref.shape, F32)
    cst = scale * math.log2(math.e)

    def pass2(j, nb):
        start = pl.multiple_of(j * t, t)
        ones = jnp.ones((nb * t, LANES), BF16)
        for h in range(heads):
            m = m_ref[h]
            p = jnp.concatenate([jnp.exp2((sc - m) * cst).astype(BF16)
                                 for jj in range(nb) for sc in _lane_chunks(s_ref[h, j + jj])], axis=1)
            v_ext = jnp.concatenate([v_ref[0, pl.ds(start, nb * t), h * V_DIM:(h + 1) * V_DIM], ones], axis=1)
            acc_ref[h] += _dot(p, v_ext)

    _for_blocks(i + 1, pass2, MLA_BLOCKS_PER_TRIP)
    for h in range(heads):
        acc = acc_ref[h]
        o_ref[0, :, h * V_DIM:(h + 1) * V_DIM] = (acc[:, :V_DIM] / acc[:, V_DIM:]).astype(o_ref.dtype)


def _mla_attn(q, k, v, *, t, heads=2):
    bsz, seq, _ = q.shape
    scale = 1.0 / math.sqrt(QK_NOPE + QK_ROPE)
    wq, wv = heads * QK_PAD, heads * V_DIM
    return pl.pallas_call(
        functools.partial(_mla_attn_kernel, t=t, scale=scale, heads=heads),
        out_shape=jax.ShapeDtypeStruct((bsz, seq, MLA_HEADS * V_DIM), BF16),
        grid=(bsz, MLA_HEADS // heads, seq // t),
        in_specs=[pl.BlockSpec((1, t, wq), lambda b, g, i: (b, i, g)),
                  pl.BlockSpec((1, seq, wq), lambda b, g, i: (b, 0, g)),
                  pl.BlockSpec((1, seq, wv), lambda b, g, i: (b, 0, g))],
        out_specs=pl.BlockSpec((1, t, wv), lambda b, g, i: (b, i, g)),
        scratch_shapes=[pltpu.VMEM((heads, seq // t, t, t), F32), pltpu.VMEM((heads, t, LANES), F32),
                        pltpu.VMEM((heads, t, V_DIM + LANES), F32)],
        compiler_params=_cparams(("parallel", "parallel", "arbitrary")),
        name="mla_attn",
    )(q, k, v)


def _sb_attn_kernel(q_ref, k_ref, v_ref, w_ref, o_ref, c_ref, acc_ref, kmax_ref, bound_ref, *, t, heads):
    i = pl.program_id(2)
    d = SB_HEAD_DIM
    c_ref[...] = jnp.zeros(c_ref.shape, F32)
    acc_ref[...] = jnp.zeros(acc_ref.shape, F32)
    sub = t // 2 if t % (2 * LANES) == 0 else t
    r_iota = lax.broadcasted_iota(jnp.int32, (sub, LANES), 0)
    c_iota = lax.broadcasted_iota(jnp.int32, (sub, LANES), 1)

    def block(r0, nr, start, nk, key0):
        rows = slice(r0, r0 + nr)
        masks = []
        for cc in range(nk // LANES):
            off = None if key0 is None else key0 + cc * LANES - r0
            masks.append(None if off is None or off <= -LANES else c_iota + off < r_iota)
        for h in range(heads):
            hs = slice(h * d, (h + 1) * d)
            zz = _dot_nt(q_ref[0, rows, hs], k_ref[0, pl.ds(start, nk), hs])
            zneg = -zz
            sp = jnp.log2(1.0 + jnp.exp2(jnp.minimum(zz, zneg)))
            log_1m = jnp.minimum(zneg, 0.0) - sp
            carry = c_ref[h, rows]
            a_chunks = [None] * (nk // LANES)
            for cc in reversed(range(nk // LANES)):
                cs = slice(cc * LANES, (cc + 1) * LANES)
                l1m = log_1m[:, cs] if masks[cc] is None else jnp.where(masks[cc], log_1m[:, cs], 0.0)
                hi, lo = _split_bf16(l1m)
                y = _dot(jnp.concatenate([hi, lo], axis=1), w_ref[...])
                a = jnp.exp2(zz[:, cs] + y[:, :LANES] + carry)
                carry = carry + y[:, LANES:]
                if masks[cc] is not None:
                    a = jnp.where(masks[cc], a, 0.0)
                a_chunks[cc] = a.astype(BF16)
            c_ref[h, rows] = carry
            acc_ref[h, rows] += _dot(jnp.concatenate(a_chunks, axis=1), v_ref[0, pl.ds(start, nk), hs])

    def diagonal_tile():
        for band in range(t // sub):
            block(band * sub, sub, pl.multiple_of(i * t, t), (band + 1) * sub, 0)

    def left_blocks(n, nb):
        for jj in range(nb):
            block(0, t, pl.multiple_of((i - 1 - n - jj) * t, t), t, None)

    @pl.when(i == 0)
    def _():
        diagonal_tile()

    @pl.when(i > 0)
    def _():
        diagonal_tile()
        left_blocks(0, 1)

    @pl.when(i == 0)
    def _():
        for h in range(heads):
            kmax = jnp.max(jnp.abs(k_ref[0, :, h * d:(h + 1) * d].astype(F32)))
            kmax_ref[h] = jnp.full(kmax_ref.shape[1:], kmax, F32)

    for h in range(heads):
        q_l1 = jnp.sum(jnp.abs(q_ref[0, :, h * d:(h + 1) * d].astype(F32)), axis=1, keepdims=True)
        bound_ref[h] = q_l1 * kmax_ref[h, 0:1, :]

    def more_to_add():
        worst = None
        for h in range(heads):
            top = jnp.max(c_ref[h] + bound_ref[h])
            worst = top if worst is None else jnp.maximum(worst, top)
        return worst >= SB_ZERO_LOG2

    def trip(state):
        n, _ = state
        left_blocks(n, SB_BLOCKS_PER_TRIP)
        return n + SB_BLOCKS_PER_TRIP, more_to_add()

    n, go = lax.while_loop(lambda s: s[1] & (s[0] + SB_BLOCKS_PER_TRIP <= i), trip,
                           (jnp.int32(1), (i > 0) & more_to_add()))
    size = SB_BLOCKS_PER_TRIP // 2
    while size:
        take = go & (n + size <= i)

        @pl.when(take)
        def _(n=n, size=size):
            left_blocks(n, size)

        n = n + jnp.where(take, size, 0)
        size //= 2
    for h in range(heads):
        o_ref[0, :, h * d:(h + 1) * d] = acc_ref[h].astype(o_ref.dtype)


def _sb_attn(q, kv, *, t, heads=4):
    bsz, seq, _ = q.shape
    w = heads * SB_HEAD_DIM
    groups = SB_HEADS // heads
    tri = (jnp.arange(LANES)[:, None] >= jnp.arange(LANES)[None, :]).astype(BF16)
    half = jnp.concatenate([tri, jnp.ones((LANES, LANES), BF16)], axis=1)
    w_sum = jnp.concatenate([half, half], axis=0)
    return pl.pallas_call(
        functools.partial(_sb_attn_kernel, t=t, heads=heads),
        out_shape=jax.ShapeDtypeStruct((bsz, seq, SB_HEADS * SB_HEAD_DIM), BF16),
        grid=(bsz, groups, seq // t),
        in_specs=[pl.BlockSpec((1, t, w), lambda b, g, i: (b, i, g)),
                  pl.BlockSpec((1, seq, w), lambda b, g, i: (b, 0, g)),
                  pl.BlockSpec((1, seq, w), lambda b, g, i: (b, 0, groups + g)),
                  pl.BlockSpec(w_sum.shape, lambda b, g, i: (0, 0))],
        out_specs=pl.BlockSpec((1, t, w), lambda b, g, i: (b, i, g)),
        scratch_shapes=[pltpu.VMEM((heads, t, LANES), F32), pltpu.VMEM((heads, t, SB_HEAD_DIM), F32),
                        pltpu.VMEM((heads, 8, LANES), F32), pltpu.VMEM((heads, t, LANES), F32)],
        compiler_params=_cparams(("parallel", "parallel", "arbitrary")),
        name="sb_attn",
    )(q, kv, kv, w_sum)


def _out_proj_kernel(o_ref, w_ref, x_ref, gt_ref, g_ref, sh_ref, sc_ref, wr_hi_ref, wr_lo_ref, br_ref, tri_ref,
                     xo_ref, h_ref, rt_ref, rtt_ref, cnt_ref, *, n_experts):
    x = x_ref[...] + gt_ref[...] * _dot(o_ref[...], w_ref[...])
    xo_ref[...] = x
    h = _modulate(_rms(x, g_ref[...]), sh_ref[...], sc_ref[...])
    h_hi = h.astype(BF16)
    words = _pack_bf16_pairs(h)
    for part in range(SC_ROW_PARTS):
        h_ref[part] = words[:, part * h_ref.shape[2]:(part + 1) * h_ref.shape[2]]
    h_lo = (h - h_hi.astype(F32)).astype(BF16)
    both = _dot(h_hi, jnp.concatenate([wr_hi_ref[...], wr_lo_ref[...]], axis=1))
    logits = both[:, :LANES] + both[:, LANES:] + _dot(h_lo, wr_hi_ref[...]) + br_ref[...]
    lane = lax.broadcasted_iota(jnp.int32, logits.shape, 1).astype(F32)
    lg = jnp.where(lane < n_experts, logits, -jnp.inf)
    m1 = jnp.max(lg, axis=1, keepdims=True)
    i1 = jnp.min(jnp.where(lg == m1, lane, float(LANES)), axis=1, keepdims=True)
    lg2 = jnp.where(lane == i1, -jnp.inf, lg)
    m2 = jnp.max(lg2, axis=1, keepdims=True)
    i2 = jnp.min(jnp.where(lg2 == m2, lane, float(LANES)), axis=1, keepdims=True)
    e2 = jnp.exp(m2 - m1)
    den = 1.0 + e2
    sel = jnp.where((lane == i1) | (lane == i2), 1.0, 0.0)
    prefix = _dot(tri_ref[...], sel.astype(BF16))
    r1 = jnp.sum(jnp.where(lane == i1, prefix, 0.0), axis=1, keepdims=True)
    r2 = jnp.sum(jnp.where(lane == i2, prefix, 0.0), axis=1, keepdims=True)
    cnt_ref[...] = jnp.sum(sel, axis=0, keepdims=True)
    rt = jnp.zeros_like(logits)
    for k, val in enumerate((i1, i2, 1.0 / den, e2 / den, r1, r2)):
        rt = jnp.where(lane == k, val, rt)
    rt_ref[...] = rt
    rtt_ref[...] = rt.T[:PICK_ROWS]


def _out_proj_router(o, w, x, mod, g, wr_hi, wr_lo, br, n_experts, *, seq, tm):
    n, d = x.shape
    tpb = seq // tm
    dpp = d // 2 // SC_ROW_PARTS
    tri = (jnp.arange(tm)[:, None] > jnp.arange(tm)[None, :]).astype(BF16)
    full = lambda a: pl.BlockSpec(a.shape, lambda i: (0,) * a.ndim)
    row = lambda wd: pl.BlockSpec((tm, wd), lambda i: (i, 0))
    return pl.pallas_call(
        functools.partial(_out_proj_kernel, n_experts=n_experts),
        out_shape=(jax.ShapeDtypeStruct((n, d), F32), jax.ShapeDtypeStruct((SC_ROW_PARTS, n, dpp), U32),
                   jax.ShapeDtypeStruct((n, LANES), F32), jax.ShapeDtypeStruct((PICK_ROWS, n), F32),
                   jax.ShapeDtypeStruct((n // tm, 1, LANES), F32)),
        grid=(n // tm,),
        in_specs=[row(o.shape[1]), full(w), row(d), _mod_spec(2, tpb, d), full(g),
                  _mod_spec(3, tpb, d), _mod_spec(4, tpb, d), full(wr_hi), full(wr_lo), full(br), full(tri)],
        out_specs=(row(d), pl.BlockSpec((SC_ROW_PARTS, tm, dpp), lambda i: (0, i, 0)), row(LANES),
                   pl.BlockSpec((PICK_ROWS, tm), lambda i: (0, i)),
                   pl.BlockSpec((None, 1, LANES), lambda i: (i, 0, 0))),
        compiler_params=_cparams(("parallel",)),
        name="out_proj_router",
    )(o, w, x, mod, g, mod, mod, wr_hi, wr_lo, br, tri)


def _silu(g):
    return g * (1.0 / (1.0 + jnp.exp(-g)))


def _ffn_kernel(o_ref, wo_ref, x_ref, gt1_ref, gf_ref, sh2_ref, sc2_ref, wg_ref, wu_ref, wd_ref, gt2_ref,
                gkv_ref, shk_ref, sck_ref, gm_ref, shm_ref, scm_ref,
                xo_ref, hk_ref, hm_ref, x1_ref, h_ref, acc_ref):
    f = pl.program_id(1)

    @pl.when(f == 0)
    def _():
        x1 = x_ref[...] + gt1_ref[...] * _dot(o_ref[...], wo_ref[...])
        x1_ref[...] = x1
        h_ref[...] = _modulate(_rms(x1, gf_ref[...]), sh2_ref[...], sc2_ref[...]).astype(BF16)
        acc_ref[...] = jnp.zeros(acc_ref.shape, F32)

    h = h_ref[...]
    a = _silu(_dot(h, wg_ref[...])) * _dot(h, wu_ref[...])
    acc_ref[...] += _dot(a.astype(BF16), wd_ref[...])

    @pl.when(f == pl.num_programs(1) - 1)
    def _():
        x = x1_ref[...] + gt2_ref[...] * acc_ref[...]
        xo_ref[...] = x
        y = x * lax.rsqrt(jnp.mean(x * x, axis=-1, keepdims=True) + EPS)
        hk_ref[...] = _modulate(y * gkv_ref[...], shk_ref[...], sck_ref[...]).astype(BF16)
        hm_ref[...] = _modulate(y * gm_ref[...], shm_ref[...], scm_ref[...]).astype(BF16)


def _ffn(o, w_o, x, mod0, g_ffn, w_gu, w_down, g_kv, mod_kv, g_mix1, mod1, *, seq, tm, tf):
    n, d = x.shape
    d_ff = w_down.shape[0]
    nf = d_ff // tf
    tpb = seq // tm
    full = lambda a: pl.BlockSpec(a.shape, lambda i, f: (0,) * a.ndim)
    row = lambda w=d: pl.BlockSpec((tm, w), lambda i, f: (i, 0))
    return pl.pallas_call(
        _ffn_kernel,
        out_shape=(jax.ShapeDtypeStruct((n, d), F32), jax.ShapeDtypeStruct((n, d), BF16),
                   jax.ShapeDtypeStruct((n, d), BF16)),
        grid=(n // tm, nf),
        in_specs=[row(o.shape[1]), full(w_o), row(), _mod_spec(2, tpb, d), full(g_ffn),
                  _mod_spec(3, tpb, d), _mod_spec(4, tpb, d),
                  pl.BlockSpec((d, tf), lambda i, f: (0, f)),
                  pl.BlockSpec((d, tf), lambda i, f: (0, nf + f)),
                  pl.BlockSpec((tf, d), lambda i, f: (f, 0)),
                  _mod_spec(5, tpb, d),
                  full(g_kv), _mod_spec(0, tpb, d), _mod_spec(1, tpb, d),
                  full(g_mix1), _mod_spec(0, tpb, d), _mod_spec(1, tpb, d)],
        out_specs=(row(), row(), row()),
        scratch_shapes=[pltpu.VMEM((tm, d), F32), pltpu.VMEM((tm, d), BF16), pltpu.VMEM((tm, d), F32)],
        compiler_params=_cparams(("parallel", "arbitrary"), vmem_limit=VMEM_LIMIT_LARGE),
        name="ffn_dense",
    )(o, w_o, x, mod0, g_ffn, mod0, mod0, w_gu, w_gu, w_down, mod0,
      g_kv, mod_kv, mod_kv, g_mix1, mod1, mod1)


def _linear_kernel(x_ref, w_ref, o_ref):
    o_ref[...] = _dot(x_ref[...], w_ref[...]).astype(o_ref.dtype)


def _linear(x, w, *, tm, tn, name):
    n, k = x.shape
    m = w.shape[1]
    return pl.pallas_call(
        _linear_kernel,
        out_shape=jax.ShapeDtypeStruct((n, m), BF16),
        grid=(m // tn, n // tm),
        in_specs=[pl.BlockSpec((tm, k), lambda j, i: (i, 0)), pl.BlockSpec((k, tn), lambda j, i: (0, j))],
        out_specs=pl.BlockSpec((tm, tn), lambda j, i: (i, j)),
        compiler_params=_cparams(("parallel", "parallel")),
        name=name,
    )(x, w)


def _route_plan(counts, n_exp, tm, n_tiles):
    cnt = counts[:, 0, :n_exp].astype(jnp.int32)
    sizes = jnp.sum(cnt, axis=0)
    padded = (sizes + tm - 1) // tm * tm
    ends = jnp.cumsum(padded)
    tile_base = (ends - padded)[None, :] + jnp.cumsum(cnt, axis=0) - cnt
    tile_start = jnp.arange(n_tiles, dtype=jnp.int32) * tm
    tile_expert = jnp.minimum(jnp.sum(tile_start[:, None] >= ends[None, :], axis=1), n_exp - 1)
    n_used = (ends[-1] // tm).reshape(1)
    n_valid = jnp.clip((ends - padded + sizes)[tile_expert] - tile_start, 0, tm)
    return tile_base, tile_expert.astype(jnp.int32), n_used.astype(jnp.int32), n_valid.astype(jnp.int32)


PICK_ROWS = 8


def _picks_kernel(rtt_ref, base_ref, o_ref, *, n_rows):
    rtt = rtt_ref[...]
    sub = lax.broadcasted_iota(jnp.int32, rtt.shape, 0)
    expert = sub.astype(F32)
    out = jnp.zeros_like(rtt)
    for k in range(TOP_K):
        chosen = expert == rtt[k:k + 1, :]
        d_k = (jnp.sum(jnp.where(chosen, base_ref[...], 0.0), axis=0, keepdims=True)
               + rtt[2 * TOP_K + k:2 * TOP_K + k + 1, :])
        for p in range(SC_ROW_PARTS):
            out = jnp.where(sub == p * TOP_K + k, d_k + float(p * n_rows), out)
    o_ref[...] = out.astype(jnp.int32)


def _picks(rtt, tile_base, n_rows, *, tm):
    n = rtt.shape[1]
    n_tok_tiles, n_exp = tile_base.shape
    assert n_exp <= PICK_ROWS
    base = jnp.pad(tile_base.astype(F32), ((0, 0), (0, PICK_ROWS - n_exp))).reshape(n_tok_tiles, PICK_ROWS, 1)
    out = pl.pallas_call(
        functools.partial(_picks_kernel, n_rows=n_rows),
        out_shape=jax.ShapeDtypeStruct((PICK_ROWS, n), jnp.int32),
        grid=(n // tm,),
        in_specs=[pl.BlockSpec((PICK_ROWS, tm), lambda i: (0, i)),
                  pl.BlockSpec((None, PICK_ROWS, 1), lambda i: (i, 0, 0))],
        out_specs=pl.BlockSpec((PICK_ROWS, tm), lambda i: (0, i)),
        compiler_params=_cparams(("parallel",)),
        name="moe_picks",
    )(rtt, base)
    return out[:SC_ROW_PARTS * TOP_K].reshape(SC_ROW_PARTS, TOP_K, n)


def _sc_mesh():
    return plsc.VectorSubcoreMesh(core_axis_name="core", subcore_axis_name="subcore",
                                  num_cores=SC_CORES, num_subcores=SC_SUBCORES)


def _sc_scatter_rows(x, idx_list, n_rows):
    m, d = x.shape

    @functools.partial(pl.kernel, out_type=jax.ShapeDtypeStruct((n_rows, d), x.dtype), mesh=_sc_mesh(),
                       scratch_types=[], name="sc_scatter_rows")
    def scatter(x_hbm, *refs):
        i_hbms, o_hbm = refs[:-1], refs[-1]

        def body(x_vmem, *i_vmems):
            for i_vmem in i_vmems:
                pltpu.sync_copy(x_vmem, o_hbm.at[i_vmem.at[0]])

        pltpu.emit_pipeline(
            body,
            grid=(m // SC_WINDOW,),
            in_specs=[pl.BlockSpec((SC_WINDOW, d), index_map=lambda i: (i, 0))]
                     + [pl.BlockSpec((1, SC_WINDOW), index_map=lambda i: (0, i))] * len(idx_list),
            out_specs=[],
            core_axis_name=("core", "subcore"),
            dimension_semantics=(pltpu.PARALLEL,),
        )(x_hbm, *i_hbms)

    return scatter(x, *[idx.reshape(1, m) for idx in idx_list])


def _grouped_ffn_kernel(te_ref, nu_ref, nv_ref, x_ref, wg_ref, wu_ref, wd_ref, y_ref, hb_ref, acc_ref):
    del te_ref
    i = pl.program_id(0)
    f = pl.program_id(1)
    used = i < nu_ref[0]
    last_f = f == pl.num_programs(1) - 1

    @pl.when(used & (f == 0))
    def _():
        words = jnp.concatenate([x_ref[p] for p in range(SC_ROW_PARTS)], axis=1)
        row = lax.broadcasted_iota(jnp.int32, words.shape, 0)
        words = jnp.where(row < nv_ref[i], words, U32(0))
        hb_ref[...] = _unpack_bf16_pairs(words).astype(BF16)
        acc_ref[...] = jnp.zeros(acc_ref.shape, F32)

    @pl.when(used)
    def _():
        h = hb_ref[...]
        a = _silu(_dot(h, wg_ref[0])) * _dot(h, wu_ref[0])
        acc_ref[...] += _dot(a.astype(BF16), wd_ref[0])

    @pl.when(used & last_f)
    def _():
        y = _pack_bf16_pairs(acc_ref[...])
        for part in range(SC_ROW_PARTS):
            y_ref[part] = y[:, part * y_ref.shape[2]:(part + 1) * y_ref.shape[2]]

    @pl.when(jnp.logical_not(used) & last_f)
    def _():
        y_ref[...] = jnp.zeros(y_ref.shape, y_ref.dtype)


def _grouped_ffn(tile_expert, n_used, n_valid, xg, w_gu, w_down, *, tm, tf):
    parts, n_rows, dpp = xg.shape
    n_exp, d_ff, d = w_down.shape
    nf = d_ff // tf

    def wspec(shape, index):
        def index_map(i, f, te, nu, nv):
            return index(te[i], jnp.where(i < nu[0], f, nf - 1))
        return pl.BlockSpec(shape, index_map)

    rows = pl.BlockSpec((parts, tm, dpp), lambda i, f, te, nu, nv: (0, i, 0))
    return pl.pallas_call(
        _grouped_ffn_kernel,
        out_shape=jax.ShapeDtypeStruct(xg.shape, xg.dtype),
        grid_spec=pltpu.PrefetchScalarGridSpec(
            num_scalar_prefetch=3,
            grid=(n_rows // tm, nf),
            in_specs=[rows,
                      wspec((1, d, tf), lambda e, f: (e, 0, f)),
                      wspec((1, d, tf), lambda e, f: (e, 0, nf + f)),
                      wspec((1, tf, d), lambda e, f: (e, f, 0))],
            out_specs=rows,
            scratch_shapes=[pltpu.VMEM((tm, d), BF16), pltpu.VMEM((tm, d), F32)]),
        compiler_params=_cparams(("arbitrary", "arbitrary")),
        name="moe_grouped_ffn",
    )(tile_expert, n_used, n_valid, xg, w_gu, w_gu, w_down)


def _sc_gather_rows(table, idx):
    n_idx = idx.shape[0]
    d = table.shape[1]

    @functools.partial(pl.kernel, out_type=jax.ShapeDtypeStruct((n_idx, d), table.dtype), mesh=_sc_mesh(),
                       name="sc_gather_rows")
    def gather(t_hbm, i_hbm, o_hbm):
        def body(i_vmem, o_vmem):
            pltpu.sync_copy(t_hbm.at[i_vmem.at[0]], o_vmem)

        pltpu.emit_pipeline(
            body,
            grid=(n_idx // SC_WINDOW,),
            in_specs=[pl.BlockSpec((1, SC_WINDOW), index_map=lambda i: (0, i))],
            out_specs=[pl.BlockSpec((SC_WINDOW, d), index_map=lambda i: (i, 0))],
            core_axis_name=("core", "subcore"),
            dimension_semantics=(pltpu.PARALLEL,),
        )(i_hbm, o_hbm)

    return gather(table, idx.reshape(1, n_idx))


def _combine_kernel(x_ref, rt_ref, gt_ref, gf_ref, *refs):
    y_refs, o_ref = refs[:-1], refs[-1]
    rt = rt_ref[...]
    tot = None
    for k in range(TOP_K):
        words = jnp.concatenate([y_refs[p * TOP_K + k][...] for p in range(SC_ROW_PARTS)], axis=1)
        term = rt[:, TOP_K + k:TOP_K + k + 1] * _unpack_bf16_pairs(words)
        tot = term if tot is None else tot + term
    x = x_ref[...] + gt_ref[...] * tot
    o_ref[...] = _rms(x, gf_ref[...])


def _combine(x, rt, mod1, g_final, ysel, *, seq, tm):
    n, d = x.shape
    tpb = seq // tm
    nt = n // tm
    row = lambda w: pl.BlockSpec((tm, w), lambda i: (i, 0))
    piece = lambda j: pl.BlockSpec((tm, ysel.shape[1]), lambda i: (j * nt + i, 0))
    n_pieces = SC_ROW_PARTS * TOP_K
    return pl.pallas_call(
        _combine_kernel,
        out_shape=jax.ShapeDtypeStruct((n, d), F32),
        grid=(nt,),
        in_specs=[row(d), row(LANES), _mod_spec(5, tpb, d), pl.BlockSpec(g_final.shape, lambda i: (0, 0))]
                 + [piece(j) for j in range(n_pieces)],
        out_specs=row(d),
        compiler_params=_cparams(("parallel",)),
        name="moe_combine",
    )(x, rt, mod1, g_final, *([ysel] * n_pieces))


def _pad_last(a, width):
    return jnp.pad(a, [(0, 0)] * (a.ndim - 1) + [(0, width - a.shape[-1])])


def kernel(x, c, positions, w_mod, b_mod, g_mix, g_ffn, w_a_down, g_q_lat, g_kv_lat, w_uq, w_ukv, w_oa,
           w_mod_kv, b_mod_kv, g_kv, w_kv_sb, w_q_sb, w_o_sb, w_ffn_gu, w_ffn_down, w_router, b_router,
           w_exp_gu, w_exp_down, g_final):
    bsz, seq, d = x.shape
    n = bsz * seq
    q_lora, kv_lora = g_q_lat.shape[1], g_kv_lat.shape[1]
    n_exp = w_router.shape[-1]
    d_ff = w_ffn_down.shape[1]
    tm = min(TOKEN_TILE, seq)
    t_mla = min(MLA_TILE, seq)
    t_sb = min(SB_TILE, seq)
    tf = next((t for t in (FFN_CHUNK, 512) if d_ff % t == 0), d_ff)

    mod0 = _modvec(c, w_mod, 0, b_mod[0]).reshape(bsz, 6, 1, d)
    mod1 = _modvec(c, w_mod, 1, b_mod[1]).reshape(bsz, 6, 1, d)
    mod_kv = _modvec(c, w_mod_kv[None], 0, b_mod_kv).reshape(bsz, 2, 1, d)

    lat_w = q_lora + kv_lora + LANES
    wd = _pad_last(w_a_down[0], lat_w).astype(BF16)
    wq = w_uq[0].reshape(q_lora, MLA_HEADS, QK_NOPE + QK_ROPE)
    wqn = wq[:, :, :QK_NOPE].reshape(q_lora, MLA_HEADS * QK_NOPE).astype(BF16)
    wqr = _pad_last(wq[:, :, QK_NOPE:], LANES).reshape(q_lora, MLA_HEADS * LANES).astype(BF16)
    wkv = w_ukv[0].reshape(kv_lora, MLA_HEADS, QK_NOPE + V_DIM)
    wkn = wkv[:, :, :QK_NOPE].reshape(kv_lora, MLA_HEADS * QK_NOPE).astype(BF16)
    wv = wkv[:, :, QK_NOPE:].reshape(kv_lora, MLA_HEADS * V_DIM).astype(BF16)
    half = QK_ROPE // 2
    inv = ROPE_THETA ** (-jnp.arange(half, dtype=F32) / half)
    inv = jnp.tile(inv, ROPE_PACK).reshape(1, LANES)
    wr = _pad_last(w_router[0], LANES)
    wr_hi = wr.astype(BF16)
    wr_lo = (wr - wr_hi.astype(F32)).astype(BF16)
    br = _pad_last(b_router[0], LANES).reshape(1, LANES)

    xf = x.reshape(n, d)
    pos = positions.reshape(n // tm, ROPE_PACK, tm // ROPE_PACK).swapaxes(1, 2)
    pos = jnp.repeat(pos.reshape(n // ROPE_PACK, ROPE_PACK), QK_ROPE // 2, axis=1)
    row1 = lambda a: a.reshape(1, -1)

    q, k, v = _mla_proj(xf, pos, mod0, row1(g_mix[0]), wd, row1(g_q_lat[0]), row1(g_kv_lat[0]),
                        wqn, wqr, wkn, wv, inv, seq=seq, tm=tm)
    o = _mla_attn(q.reshape(bsz, seq, -1), k.reshape(bsz, seq, -1), v.reshape(bsz, seq, -1), t=t_mla)
    x2, hk, hm = _ffn(o.reshape(n, -1), w_oa[0].astype(BF16), xf, mod0, row1(g_ffn[0]),
                      w_ffn_gu[0].astype(BF16), w_ffn_down[0].astype(BF16),
                      row1(g_kv), mod_kv, row1(g_mix[1]), mod1, seq=seq, tm=tm, tf=tf)
    kv = _linear(hk, w_kv_sb.astype(BF16), tm=tm, tn=w_kv_sb.shape[1], name="kv_proj")
    q_scale = math.log2(math.e) / math.sqrt(SB_HEAD_DIM)
    qs = _linear(hm, (w_q_sb[0] * q_scale).astype(BF16), tm=tm, tn=1024, name="q_proj")
    o = _sb_attn(qs.reshape(bsz, seq, -1), kv.reshape(bsz, seq, -1), t=t_sb)
    x3, h, rt, rtt, counts = _out_proj_router(o.reshape(n, -1), w_o_sb[0].astype(BF16), x2, mod1,
                                              row1(g_ffn[1]), wr_hi, wr_lo, br, n_exp, seq=seq, tm=tm)
    n_tiles = TOP_K * n // tm + n_exp
    n_rows = n_tiles * tm
    tile_base, tile_expert, n_used, n_valid = _route_plan(counts, n_exp, tm, n_tiles)
    picks = _picks(rtt, tile_base, n_rows, tm=tm)
    xg = _sc_scatter_rows(h.reshape(SC_ROW_PARTS * n, -1), [picks[:, k].reshape(-1) for k in range(TOP_K)],
                          SC_ROW_PARTS * n_rows)
    yg = _grouped_ffn(tile_expert, n_used, n_valid, xg.reshape(SC_ROW_PARTS, n_rows, -1),
                      w_exp_gu[0].astype(BF16), w_exp_down[0].astype(BF16), tm=tm, tf=tf)
    ysel = _sc_gather_rows(yg.reshape(SC_ROW_PARTS * n_rows, -1), picks.reshape(-1))
    out = _combine(x3, rt, mod1, row1(g_final), ysel, seq=seq, tm=tm)
    return out.reshape(bsz, seq, d)
```

```python
import functools
import math

import jax
import jax.numpy as jnp
from jax import lax
from jax.experimental import pallas as pl
from jax.experimental.pallas import tpu as pltpu
from jax.experimental.pallas import tpu_sc as plsc

F32 = jnp.float32
BF16 = jnp.bfloat16

EPS = 1e-6
MLA_HEADS = 8
QK_NOPE = 128
QK_ROPE = 64
V_DIM = 128
ROPE_THETA = 10000.0
SB_HEADS = 8
SB_HEAD_DIM = 128
TOP_K = 2

LANES = 128
QK_PAD = 256
ROPE_PACK = LANES // (QK_ROPE // 2)
VMEM_LIMIT = 48 * 1024 * 1024
VMEM_LIMIT_LARGE = 58 * 1024 * 1024
TOKEN_TILE = 512
MLA_TILE = 512
MLA_BLOCKS_PER_TRIP = 4
MLA_TILES_PER_STEP = 2
SB_TILE = 512
SB_BLOCKS_PER_TRIP = 2
SB_ZERO_LOG2 = -140.0
FFN_CHUNK = 1792
SC_CORES = 2
SC_SUBCORES = 16
SC_WINDOW = 128
SC_ROW_PARTS = 2


def _cparams(sem, vmem_limit=VMEM_LIMIT):
    return pltpu.CompilerParams(dimension_semantics=sem, vmem_limit_bytes=vmem_limit)


def _rms(x, g):
    return x * lax.rsqrt(jnp.mean(x * x, axis=-1, keepdims=True) + EPS) * g


def _modulate(h, shift, scale):
    return h * (1.0 + scale) + shift


def _split_bf16(a):
    hi = a.astype(BF16)
    lo = (a - hi.astype(F32)).astype(BF16)
    return hi, lo


U32 = jnp.uint32
HIGH_HALF = 0xFFFF0000


def _pack_bf16_pairs(a):
    half = a.shape[1] // 2
    bits = lambda v: lax.bitcast_convert_type(v.astype(BF16).astype(F32), U32)
    return (bits(a[:, :half]) >> 16) | (bits(a[:, half:]) & U32(HIGH_HALF))


def _unpack_bf16_pairs(w):
    lo = lax.bitcast_convert_type(w << 16, F32)
    hi = lax.bitcast_convert_type(w & U32(HIGH_HALF), F32)
    return jnp.concatenate([lo, hi], axis=1)


def _dot(a, b):
    return jnp.dot(a, b, preferred_element_type=F32)


def _dot_nt(a, b):
    return lax.dot_general(a, b, (((1,), (1,)), ((), ())), preferred_element_type=F32)


def _modvec_kernel(c_ref, w_ref, b_ref, o_ref):
    c = c_ref[...]
    sc = c * (1.0 / (1.0 + jnp.exp(-c)))
    a_hi, a_lo = _split_bf16(sc)
    w_hi, w_lo = _split_bf16(w_ref[...])
    o_ref[...] = _dot(a_hi, w_hi) + _dot(a_lo, w_hi) + _dot(a_hi, w_lo) + b_ref[...]


def _modvec(c, w, layer, b, tn=512):
    bsz, d = c.shape
    n = w.shape[2]
    return pl.pallas_call(
        _modvec_kernel,
        out_shape=jax.ShapeDtypeStruct((bsz, n), F32),
        grid=(n // tn,),
        in_specs=[pl.BlockSpec((bsz, d), lambda j: (0, 0)),
                  pl.BlockSpec((None, d, tn), lambda j: (layer, 0, j)),
                  pl.BlockSpec((1, tn), lambda j: (0, j))],
        out_specs=pl.BlockSpec((bsz, tn), lambda j: (0, j)),
        compiler_params=_cparams(("arbitrary",)),
        name="modvec",
    )(c, w, b.reshape(1, n))


def _mod_spec(chunk, tiles_per_batch, d):
    return pl.BlockSpec((None, None, 1, d), lambda i, *_: (i // tiles_per_batch, chunk, 0, 0))


def _mla_proj_kernel(x_ref, pos_ref, sh_ref, sc_ref, g_ref, wd_ref, gq_ref, gkv_ref,
                     wqn_ref, wqr_ref, wkn_ref, wv_ref, inv_ref,
                     q_ref, k_ref, v_ref, *, q_lora, kv_lora):
    x = x_ref[...]
    h = _modulate(_rms(x, g_ref[...]), sh_ref[...], sc_ref[...]).astype(BF16)
    lat = _dot(h, wd_ref[...])
    c_q = _rms(lat[:, :q_lora], gq_ref[...]).astype(BF16)
    c_kv = _rms(lat[:, q_lora:q_lora + kv_lora], gkv_ref[...]).astype(BF16)
    k_rot = lat[:, q_lora + kv_lora:]

    half = QK_ROPE // 2
    ang = pos_ref[...].astype(F32) * inv_ref[...]
    cos_p = jnp.cos(ang)
    sin_p = jnp.sin(ang)
    lane = lax.broadcasted_iota(jnp.int32, ang.shape, 1)
    cos, s_a, s_b = [], [], []
    for g in range(ROPE_PACK):
        shift = (LANES - g * half) % LANES
        cg = pltpu.roll(cos_p, shift, axis=1) if shift else cos_p
        sg = pltpu.roll(sin_p, shift, axis=1) if shift else sin_p
        cos.append(jnp.where(lane < half, cg, pltpu.roll(cg, half, axis=1)))
        s_a.append(jnp.where(lane < half, -sg, 0.0))
        s_b.append(jnp.where((lane >= half) & (lane < 2 * half), pltpu.roll(sg, half, axis=1), 0.0))
    cos, s_a, s_b = (jnp.concatenate(v, axis=0) for v in (cos, s_a, s_b))

    def rope(r):
        return (r * cos + pltpu.roll(r, LANES - half, axis=1) * s_a
                + pltpu.roll(r, half, axis=1) * s_b)

    k_rot = rope(k_rot).astype(BF16)
    q_nope = _dot(c_q, wqn_ref[...])
    q_rope = _dot(c_q, wqr_ref[...])
    k_nope = _dot(c_kv, wkn_ref[...])
    v_ref[...] = _dot(c_kv, wv_ref[...]).astype(BF16)
    for hd in range(MLA_HEADS):
        a, b = hd * LANES, (hd + 1) * LANES
        q_ref[:, hd * QK_PAD:hd * QK_PAD + LANES] = q_nope[:, a:b].astype(BF16)
        q_ref[:, hd * QK_PAD + LANES:(hd + 1) * QK_PAD] = rope(q_rope[:, a:b]).astype(BF16)
        k_ref[:, hd * QK_PAD:hd * QK_PAD + LANES] = k_nope[:, a:b].astype(BF16)
        k_ref[:, hd * QK_PAD + LANES:(hd + 1) * QK_PAD] = k_rot


def _mla_proj(x, pos, mod, g_mix, wd, gq, gkv, wqn, wqr, wkn, wv, inv, *, seq, tm):
    n, d = x.shape
    tpb = seq // tm
    q_lora, kv_lora = gq.shape[1], gkv.shape[1]
    hq = MLA_HEADS * QK_PAD
    hv = MLA_HEADS * V_DIM
    full = lambda a: pl.BlockSpec(a.shape, lambda i: (0,) * a.ndim)
    row = lambda w: pl.BlockSpec((tm, w), lambda i: (i, 0))
    return pl.pallas_call(
        functools.partial(_mla_proj_kernel, q_lora=q_lora, kv_lora=kv_lora),
        out_shape=(jax.ShapeDtypeStruct((n, hq), BF16), jax.ShapeDtypeStruct((n, hq), BF16),
                   jax.ShapeDtypeStruct((n, hv), BF16)),
        grid=(n // tm,),
        in_specs=[row(d), pl.BlockSpec((tm // ROPE_PACK, LANES), lambda i: (i, 0)), _mod_spec(0, tpb, d), _mod_spec(1, tpb, d), full(g_mix), full(wd),
                  full(gq), full(gkv), full(wqn), full(wqr), full(wkn), full(wv), full(inv)],
        out_specs=(row(hq), row(hq), row(hv)),
        compiler_params=_cparams(("parallel",)),
        name="mla_proj",
    )(x, pos, mod, mod, g_mix, wd, gq, gkv, wqn, wqr, wkn, wv, inv)


def _lane_chunks(a):
    return [a[:, c * LANES:(c + 1) * LANES] for c in range(a.shape[1] // LANES)]


def _for_blocks(n, step, group):
    def trip(p, carry):
        step(group * p, group)
        return carry

    lax.fori_loop(0, n // group, trip, 0)
    size = group // 2
    while size:
        @pl.when((n // size) % 2 == 1)
        def _(size=size):
            step(n // (2 * size) * (2 * size), size)
        size //= 2


def _mla_attn_kernel(q_ref, k_ref, v_ref, o_ref, s_ref, m_ref, acc_ref, *, t, scale, heads, tiles):
    def lane_max(s):
        m = None
        for sc in _lane_chunks(s):
            m = sc if m is None else jnp.maximum(m, sc)
        return m

    r_minus_c = (lax.broadcasted_iota(jnp.int32, (t, t), 0) - lax.broadcasted_iota(jnp.int32, (t, t), 1))
    cst = scale * math.log2(math.e)

    def query_tile(i, rows):
        def scores(h, j):
            start = pl.multiple_of(j * t, t)
            hs = slice(h * QK_PAD, (h + 1) * QK_PAD)
            return _dot_nt(q_ref[0, rows, hs], k_ref[0, pl.ds(start, t), hs])

        m_ref[...] = jnp.full(m_ref.shape, -jnp.inf, F32)

        def pass1(j, nb):
            for h in range(heads):
                m = m_ref[h]
                for jj in range(nb):
                    s = jnp.where(r_minus_c >= (j + jj - i) * t, scores(h, j + jj), -jnp.inf)
                    s_ref[h, j + jj] = s
                    m = jnp.maximum(m, lane_max(s))
                m_ref[h] = m

        _for_blocks(i + 1, pass1, MLA_BLOCKS_PER_TRIP)
        for h in range(heads):
            m_ref[h] = jnp.broadcast_to(jnp.max(m_ref[h], axis=1, keepdims=True), (t, LANES))
        acc_ref[...] = jnp.zeros(acc_ref.shape, F32)

        def pass2(j, nb):
            start = pl.multiple_of(j * t, t)
            ones = jnp.ones((nb * t, LANES), BF16)
            for h in range(heads):
                m = m_ref[h]
                p = jnp.concatenate([jnp.exp2((sc - m) * cst).astype(BF16)
                                     for jj in range(nb) for sc in _lane_chunks(s_ref[h, j + jj])], axis=1)
                v_ext = jnp.concatenate([v_ref[0, pl.ds(start, nb * t), h * V_DIM:(h + 1) * V_DIM], ones],
                                        axis=1)
                acc_ref[h] += _dot(p, v_ext)

        _for_blocks(i + 1, pass2, MLA_BLOCKS_PER_TRIP)
        for h in range(heads):
            acc = acc_ref[h]
            o_ref[0, rows, h * V_DIM:(h + 1) * V_DIM] = (acc[:, :V_DIM] / acc[:, V_DIM:]).astype(o_ref.dtype)

    for sub in range(tiles):
        query_tile(pl.program_id(2) * tiles + sub, slice(sub * t, (sub + 1) * t))


def _mla_attn(q, k, v, *, t, heads=2):
    bsz, seq, _ = q.shape
    scale = 1.0 / math.sqrt(QK_NOPE + QK_ROPE)
    wq, wv = heads * QK_PAD, heads * V_DIM
    tiles = MLA_TILES_PER_STEP if (seq // t) % MLA_TILES_PER_STEP == 0 else 1
    return pl.pallas_call(
        functools.partial(_mla_attn_kernel, t=t, scale=scale, heads=heads, tiles=tiles),
        out_shape=jax.ShapeDtypeStruct((bsz, seq, MLA_HEADS * V_DIM), BF16),
        grid=(bsz, MLA_HEADS // heads, seq // (tiles * t)),
        in_specs=[pl.BlockSpec((1, tiles * t, wq), lambda b, g, i: (b, i, g)),
                  pl.BlockSpec((1, seq, wq), lambda b, g, i: (b, 0, g)),
                  pl.BlockSpec((1, seq, wv), lambda b, g, i: (b, 0, g))],
        out_specs=pl.BlockSpec((1, tiles * t, wv), lambda b, g, i: (b, i, g)),
        scratch_shapes=[pltpu.VMEM((heads, seq // t, t, t), F32), pltpu.VMEM((heads, t, LANES), F32),
                        pltpu.VMEM((heads, t, V_DIM + LANES), F32)],
        compiler_params=_cparams(("parallel", "parallel", "arbitrary")),
        name="mla_attn",
    )(q, k, v)


def _sb_attn_kernel(q_ref, k_ref, v_ref, w_ref, o_ref, c_ref, acc_ref, kmax_ref, bound_ref, *, t, heads):
    i = pl.program_id(2)
    d = SB_HEAD_DIM
    c_ref[...] = jnp.zeros(c_ref.shape, F32)
    acc_ref[...] = jnp.zeros(acc_ref.shape, F32)
    sub = t // 2 if t % (2 * LANES) == 0 else t
    r_iota = lax.broadcasted_iota(jnp.int32, (sub, LANES), 0)
    c_iota = lax.broadcasted_iota(jnp.int32, (sub, LANES), 1)

    def block(r0, nr, start, nk, key0):
        rows = slice(r0, r0 + nr)
        masks = []
        for cc in range(nk // LANES):
            off = None if key0 is None else key0 + cc * LANES - r0
            masks.append(None if off is None or off <= -LANES else c_iota + off < r_iota)
        for h in range(heads):
            hs = slice(h * d, (h + 1) * d)
            zz = _dot_nt(q_ref[0, rows, hs], k_ref[0, pl.ds(start, nk), hs])
            zneg = -zz
            sp = jnp.log2(1.0 + jnp.exp2(jnp.minimum(zz, zneg)))
            log_1m = jnp.minimum(zneg, 0.0) - sp
            carry = c_ref[h, rows]
            a_chunks = [None] * (nk // LANES)
            for cc in reversed(range(nk // LANES)):
                cs = slice(cc * LANES, (cc + 1) * LANES)
                l1m = log_1m[:, cs] if masks[cc] is None else jnp.where(masks[cc], log_1m[:, cs], 0.0)
                hi, lo = _split_bf16(l1m)
                y = _dot(jnp.concatenate([hi, lo], axis=1), w_ref[...])
                a = jnp.exp2(zz[:, cs] + y[:, :LANES] + carry)
                carry = carry + y[:, LANES:]
                if masks[cc] is not None:
                    a = jnp.where(masks[cc], a, 0.0)
                a_chunks[cc] = a.astype(BF16)
            c_ref[h, rows] = carry
            acc_ref[h, rows] += _dot(jnp.concatenate(a_chunks, axis=1), v_ref[0, pl.ds(start, nk), hs])

    def diagonal_tile():
        for band in range(t // sub):
            block(band * sub, sub, pl.multiple_of(i * t, t), (band + 1) * sub, 0)

    def left_blocks(n, nb):
        for jj in range(nb):
            block(0, t, pl.multiple_of((i - 1 - n - jj) * t, t), t, None)

    @pl.when(i == 0)
    def _():
        diagonal_tile()

    @pl.when(i > 0)
    def _():
        diagonal_tile()
        left_blocks(0, 1)

    @pl.when(i == 0)
    def _():
        for h in range(heads):
            kmax = jnp.max(jnp.abs(k_ref[0, :, h * d:(h + 1) * d].astype(F32)))
            kmax_ref[h] = jnp.full(kmax_ref.shape[1:], kmax, F32)

    for h in range(heads):
        q_l1 = jnp.sum(jnp.abs(q_ref[0, :, h * d:(h + 1) * d].astype(F32)), axis=1, keepdims=True)
        bound_ref[h] = q_l1 * kmax_ref[h, 0:1, :]

    def more_to_add():
        worst = None
        for h in range(heads):
            top = jnp.max(c_ref[h] + bound_ref[h])
            worst = top if worst is None else jnp.maximum(worst, top)
        return worst >= SB_ZERO_LOG2

    def trip(state):
        n, _ = state
        left_blocks(n, SB_BLOCKS_PER_TRIP)
        return n + SB_BLOCKS_PER_TRIP, more_to_add()

    n, go = lax.while_loop(lambda s: s[1] & (s[0] + SB_BLOCKS_PER_TRIP <= i), trip,
                           (jnp.int32(1), (i > 0) & more_to_add()))
    size = SB_BLOCKS_PER_TRIP // 2
    while size:
        take = go & (n + size <= i)

        @pl.when(take)
        def _(n=n, size=size):
            left_blocks(n, size)

        n = n + jnp.where(take, size, 0)
        size //= 2
    for h in range(heads):
        o_ref[0, :, h * d:(h + 1) * d] = acc_ref[h].astype(o_ref.dtype)


def _sb_attn(q, kv, *, t, heads=4):
    bsz, seq, _ = q.shape
    w = heads * SB_HEAD_DIM
    groups = SB_HEADS // heads
    tri = (jnp.arange(LANES)[:, None] >= jnp.arange(LANES)[None, :]).astype(BF16)
    half = jnp.concatenate([tri, jnp.ones((LANES, LANES), BF16)], axis=1)
    w_sum = jnp.concatenate([half, half], axis=0)
    return pl.pallas_call(
        functools.partial(_sb_attn_kernel, t=t, heads=heads),
        out_shape=jax.ShapeDtypeStruct((bsz, seq, SB_HEADS * SB_HEAD_DIM), BF16),
        grid=(bsz, groups, seq // t),
        in_specs=[pl.BlockSpec((1, t, w), lambda b, g, i: (b, i, g)),
                  pl.BlockSpec((1, seq, w), lambda b, g, i: (b, 0, g)),
                  pl.BlockSpec((1, seq, w), lambda b, g, i: (b, 0, groups + g)),
                  pl.BlockSpec(w_sum.shape, lambda b, g, i: (0, 0))],
        out_specs=pl.BlockSpec((1, t, w), lambda b, g, i: (b, i, g)),
        scratch_shapes=[pltpu.VMEM((heads, t, LANES), F32), pltpu.VMEM((heads, t, SB_HEAD_DIM), F32),
                        pltpu.VMEM((heads, 8, LANES), F32), pltpu.VMEM((heads, t, LANES), F32)],
        compiler_params=_cparams(("parallel", "parallel", "arbitrary")),
        name="sb_attn",
    )(q, kv, kv, w_sum)


def _out_proj_kernel(o_ref, w_ref, x_ref, gt_ref, g_ref, sh_ref, sc_ref, wr_hi_ref, wr_lo_ref, br_ref, tri_ref,
                     xo_ref, h_ref, rt_ref, rtt_ref, cnt_ref, *, n_experts):
    x = x_ref[...] + gt_ref[...] * _dot(o_ref[...], w_ref[...])
    xo_ref[...] = x
    h = _modulate(_rms(x, g_ref[...]), sh_ref[...], sc_ref[...])
    h_hi = h.astype(BF16)
    words = _pack_bf16_pairs(h)
    for part in range(SC_ROW_PARTS):
        h_ref[part] = words[:, part * h_ref.shape[2]:(part + 1) * h_ref.shape[2]]
    h_lo = (h - h_hi.astype(F32)).astype(BF16)
    both = _dot(h_hi, jnp.concatenate([wr_hi_ref[...], wr_lo_ref[...]], axis=1))
    logits = both[:, :LANES] + both[:, LANES:] + _dot(h_lo, wr_hi_ref[...]) + br_ref[...]
    lane = lax.broadcasted_iota(jnp.int32, logits.shape, 1).astype(F32)
    lg = jnp.where(lane < n_experts, logits, -jnp.inf)
    m1 = jnp.max(lg, axis=1, keepdims=True)
    i1 = jnp.min(jnp.where(lg == m1, lane, float(LANES)), axis=1, keepdims=True)
    lg2 = jnp.where(lane == i1, -jnp.inf, lg)
    m2 = jnp.max(lg2, axis=1, keepdims=True)
    i2 = jnp.min(jnp.where(lg2 == m2, lane, float(LANES)), axis=1, keepdims=True)
    e2 = jnp.exp(m2 - m1)
    den = 1.0 + e2
    sel = jnp.where((lane == i1) | (lane == i2), 1.0, 0.0)
    prefix = _dot(tri_ref[...], sel.astype(BF16))
    r1 = jnp.sum(jnp.where(lane == i1, prefix, 0.0), axis=1, keepdims=True)
    r2 = jnp.sum(jnp.where(lane == i2, prefix, 0.0), axis=1, keepdims=True)
    cnt_ref[...] = jnp.sum(sel, axis=0, keepdims=True)
    rt = jnp.zeros_like(logits)
    for k, val in enumerate((i1, i2, 1.0 / den, e2 / den, r1, r2)):
        rt = jnp.where(lane == k, val, rt)
    rt_ref[...] = rt
    rtt_ref[...] = rt.T[:PICK_ROWS]


def _out_proj_router(o, w, x, mod, g, wr_hi, wr_lo, br, n_experts, *, seq, tm):
    n, d = x.shape
    tpb = seq // tm
    dpp = d // 2 // SC_ROW_PARTS
    tri = (jnp.arange(tm)[:, None] > jnp.arange(tm)[None, :]).astype(BF16)
    full = lambda a: pl.BlockSpec(a.shape, lambda i: (0,) * a.ndim)
    row = lambda wd: pl.BlockSpec((tm, wd), lambda i: (i, 0))
    return pl.pallas_call(
        functools.partial(_out_proj_kernel, n_experts=n_experts),
        out_shape=(jax.ShapeDtypeStruct((n, d), F32), jax.ShapeDtypeStruct((SC_ROW_PARTS, n, dpp), U32),
                   jax.ShapeDtypeStruct((n, LANES), F32), jax.ShapeDtypeStruct((PICK_ROWS, n), F32),
                   jax.ShapeDtypeStruct((n // tm, 1, LANES), F32)),
        grid=(n // tm,),
        in_specs=[row(o.shape[1]), full(w), row(d), _mod_spec(2, tpb, d), full(g),
                  _mod_spec(3, tpb, d), _mod_spec(4, tpb, d), full(wr_hi), full(wr_lo), full(br), full(tri)],
        out_specs=(row(d), pl.BlockSpec((SC_ROW_PARTS, tm, dpp), lambda i: (0, i, 0)), row(LANES),
                   pl.BlockSpec((PICK_ROWS, tm), lambda i: (0, i)),
                   pl.BlockSpec((None, 1, LANES), lambda i: (i, 0, 0))),
        compiler_params=_cparams(("parallel",)),
        name="out_proj_router",
    )(o, w, x, mod, g, mod, mod, wr_hi, wr_lo, br, tri)


def _silu(g):
    return g * (1.0 / (1.0 + jnp.exp(-g)))


def _ffn_kernel(o_ref, wo_ref, x_ref, gt1_ref, gf_ref, sh2_ref, sc2_ref, wg_ref, wu_ref, wd_ref, gt2_ref,
                gkv_ref, shk_ref, sck_ref, gm_ref, shm_ref, scm_ref,
                xo_ref, hk_ref, hm_ref, x1_ref, h_ref, acc_ref):
    f = pl.program_id(1)

    @pl.when(f == 0)
    def _():
        x1 = x_ref[...] + gt1_ref[...] * _dot(o_ref[...], wo_ref[...])
        x1_ref[...] = x1
        h_ref[...] = _modulate(_rms(x1, gf_ref[...]), sh2_ref[...], sc2_ref[...]).astype(BF16)
        acc_ref[...] = jnp.zeros(acc_ref.shape, F32)

    h = h_ref[...]
    a = _silu(_dot(h, wg_ref[...])) * _dot(h, wu_ref[...])
    acc_ref[...] += _dot(a.astype(BF16), wd_ref[...])

    @pl.when(f == pl.num_programs(1) - 1)
    def _():
        x = x1_ref[...] + gt2_ref[...] * acc_ref[...]
        xo_ref[...] = x
        y = x * lax.rsqrt(jnp.mean(x * x, axis=-1, keepdims=True) + EPS)
        hk_ref[...] = _modulate(y * gkv_ref[...], shk_ref[...], sck_ref[...]).astype(BF16)
        hm_ref[...] = _modulate(y * gm_ref[...], shm_ref[...], scm_ref[...]).astype(BF16)


def _ffn(o, w_o, x, mod0, g_ffn, w_gu, w_down, g_kv, mod_kv, g_mix1, mod1, *, seq, tm, tf):
    n, d = x.shape
    d_ff = w_down.shape[0]
    nf = d_ff // tf
    tpb = seq // tm
    full = lambda a: pl.BlockSpec(a.shape, lambda i, f: (0,) * a.ndim)
    row = lambda w=d: pl.BlockSpec((tm, w), lambda i, f: (i, 0))
    return pl.pallas_call(
        _ffn_kernel,
        out_shape=(jax.ShapeDtypeStruct((n, d), F32), jax.ShapeDtypeStruct((n, d), BF16),
                   jax.ShapeDtypeStruct((n, d), BF16)),
        grid=(n // tm, nf),
        in_specs=[row(o.shape[1]), full(w_o), row(), _mod_spec(2, tpb, d), full(g_ffn),
                  _mod_spec(3, tpb, d), _mod_spec(4, tpb, d),
                  pl.BlockSpec((d, tf), lambda i, f: (0, f)),
                  pl.BlockSpec((d, tf), lambda i, f: (0, nf + f)),
                  pl.BlockSpec((tf, d), lambda i, f: (f, 0)),
                  _mod_spec(5, tpb, d),
                  full(g_kv), _mod_spec(0, tpb, d), _mod_spec(1, tpb, d),
                  full(g_mix1), _mod_spec(0, tpb, d), _mod_spec(1, tpb, d)],
        out_specs=(row(), row(), row()),
        scratch_shapes=[pltpu.VMEM((tm, d), F32), pltpu.VMEM((tm, d), BF16), pltpu.VMEM((tm, d), F32)],
        compiler_params=_cparams(("parallel", "arbitrary"), vmem_limit=VMEM_LIMIT_LARGE),
        name="ffn_dense",
    )(o, w_o, x, mod0, g_ffn, mod0, mod0, w_gu, w_gu, w_down, mod0,
      g_kv, mod_kv, mod_kv, g_mix1, mod1, mod1)


def _linear_kernel(x_ref, w_ref, o_ref):
    o_ref[...] = _dot(x_ref[...], w_ref[...]).astype(o_ref.dtype)


def _linear(x, w, *, tm, tn, name):
    n, k = x.shape
    m = w.shape[1]
    return pl.pallas_call(
        _linear_kernel,
        out_shape=jax.ShapeDtypeStruct((n, m), BF16),
        grid=(m // tn, n // tm),
        in_specs=[pl.BlockSpec((tm, k), lambda j, i: (i, 0)), pl.BlockSpec((k, tn), lambda j, i: (0, j))],
        out_specs=pl.BlockSpec((tm, tn), lambda j, i: (i, j)),
        compiler_params=_cparams(("parallel", "parallel")),
        name=name,
    )(x, w)


def _route_plan(counts, n_exp, tm, n_tiles):
    cnt = counts[:, 0, :n_exp].astype(jnp.int32)
    sizes = jnp.sum(cnt, axis=0)
    padded = (sizes + tm - 1) // tm * tm
    ends = jnp.cumsum(padded)
    tile_base = (ends - padded)[None, :] + jnp.cumsum(cnt, axis=0) - cnt
    tile_start = jnp.arange(n_tiles, dtype=jnp.int32) * tm
    tile_expert = jnp.minimum(jnp.sum(tile_start[:, None] >= ends[None, :], axis=1), n_exp - 1)
    n_used = (ends[-1] // tm).reshape(1)
    n_valid = jnp.clip((ends - padded + sizes)[tile_expert] - tile_start, 0, tm)
    return tile_base, tile_expert.astype(jnp.int32), n_used.astype(jnp.int32), n_valid.astype(jnp.int32)


PICK_ROWS = 8


def _picks_kernel(rtt_ref, base_ref, o_ref, *, n_rows):
    rtt = rtt_ref[...]
    sub = lax.broadcasted_iota(jnp.int32, rtt.shape, 0)
    expert = sub.astype(F32)
    out = jnp.zeros_like(rtt)
    for k in range(TOP_K):
        chosen = expert == rtt[k:k + 1, :]
        d_k = (jnp.sum(jnp.where(chosen, base_ref[...], 0.0), axis=0, keepdims=True)
               + rtt[2 * TOP_K + k:2 * TOP_K + k + 1, :])
        for p in range(SC_ROW_PARTS):
            out = jnp.where(sub == p * TOP_K + k, d_k + float(p * n_rows), out)
    o_ref[...] = out.astype(jnp.int32)


def _picks(rtt, tile_base, n_rows, *, tm):
    n = rtt.shape[1]
    n_tok_tiles, n_exp = tile_base.shape
    assert n_exp <= PICK_ROWS
    base = jnp.pad(tile_base.astype(F32), ((0, 0), (0, PICK_ROWS - n_exp))).reshape(n_tok_tiles, PICK_ROWS, 1)
    out = pl.pallas_call(
        functools.partial(_picks_kernel, n_rows=n_rows),
        out_shape=jax.ShapeDtypeStruct((PICK_ROWS, n), jnp.int32),
        grid=(n // tm,),
        in_specs=[pl.BlockSpec((PICK_ROWS, tm), lambda i: (0, i)),
                  pl.BlockSpec((None, PICK_ROWS, 1), lambda i: (i, 0, 0))],
        out_specs=pl.BlockSpec((PICK_ROWS, tm), lambda i: (0, i)),
        compiler_params=_cparams(("parallel",)),
        name="moe_picks",
    )(rtt, base)
    return out[:SC_ROW_PARTS * TOP_K].reshape(SC_ROW_PARTS, TOP_K, n)


def _sc_mesh():
    return plsc.VectorSubcoreMesh(core_axis_name="core", subcore_axis_name="subcore",
                                  num_cores=SC_CORES, num_subcores=SC_SUBCORES)


def _sc_scatter_rows(x, idx_list, n_rows):
    m, d = x.shape

    @functools.partial(pl.kernel, out_type=jax.ShapeDtypeStruct((n_rows, d), x.dtype), mesh=_sc_mesh(),
                       scratch_types=[], name="sc_scatter_rows")
    def scatter(x_hbm, *refs):
        i_hbms, o_hbm = refs[:-1], refs[-1]

        def body(x_vmem, *i_vmems):
            for i_vmem in i_vmems:
                pltpu.sync_copy(x_vmem, o_hbm.at[i_vmem.at[0]])

        pltpu.emit_pipeline(
            body,
            grid=(m // SC_WINDOW,),
            in_specs=[pl.BlockSpec((SC_WINDOW, d), index_map=lambda i: (i, 0))]
                     + [pl.BlockSpec((1, SC_WINDOW), index_map=lambda i: (0, i))] * len(idx_list),
            out_specs=[],
            core_axis_name=("core", "subcore"),
            dimension_semantics=(pltpu.PARALLEL,),
        )(x_hbm, *i_hbms)

    return scatter(x, *[idx.reshape(1, m) for idx in idx_list])


def _grouped_ffn_kernel(te_ref, nu_ref, nv_ref, x_ref, wg_ref, wu_ref, wd_ref, y_ref, hb_ref, acc_ref):
    del te_ref
    i = pl.program_id(0)
    f = pl.program_id(1)
    used = i < nu_ref[0]
    last_f = f == pl.num_programs(1) - 1

    @pl.when(used & (f == 0))
    def _():
        words = jnp.concatenate([x_ref[p] for p in range(SC_ROW_PARTS)], axis=1)
        row = lax.broadcasted_iota(jnp.int32, words.shape, 0)
        words = jnp.where(row < nv_ref[i], words, U32(0))
        hb_ref[...] = _unpack_bf16_pairs(words).astype(BF16)
        acc_ref[...] = jnp.zeros(acc_ref.shape, F32)

    @pl.when(used)
    def _():
        h = hb_ref[...]
        a = _silu(_dot(h, wg_ref[0])) * _dot(h, wu_ref[0])
        acc_ref[...] += _dot(a.astype(BF16), wd_ref[0])

    @pl.when(used & last_f)
    def _():
        y = _pack_bf16_pairs(acc_ref[...])
        for part in range(SC_ROW_PARTS):
            y_ref[part] = y[:, part * y_ref.shape[2]:(part + 1) * y_ref.shape[2]]

    @pl.when(jnp.logical_not(used) & last_f)
    def _():
        y_ref[...] = jnp.zeros(y_ref.shape, y_ref.dtype)


def _grouped_ffn(tile_expert, n_used, n_valid, xg, w_gu, w_down, *, tm, tf):
    parts, n_rows, dpp = xg.shape
    n_exp, d_ff, d = w_down.shape
    nf = d_ff // tf

    def wspec(shape, index):
        def index_map(i, f, te, nu, nv):
            return index(te[i], jnp.where(i < nu[0], f, nf - 1))
        return pl.BlockSpec(shape, index_map)

    rows = pl.BlockSpec((parts, tm, dpp), lambda i, f, te, nu, nv: (0, i, 0))
    return pl.pallas_call(
        _grouped_ffn_kernel,
        out_shape=jax.ShapeDtypeStruct(xg.shape, xg.dtype),
        grid_spec=pltpu.PrefetchScalarGridSpec(
            num_scalar_prefetch=3,
            grid=(n_rows // tm, nf),
            in_specs=[rows,
                      wspec((1, d, tf), lambda e, f: (e, 0, f)),
                      wspec((1, d, tf), lambda e, f: (e, 0, nf + f)),
                      wspec((1, tf, d), lambda e, f: (e, f, 0))],
            out_specs=rows,
            scratch_shapes=[pltpu.VMEM((tm, d), BF16), pltpu.VMEM((tm, d), F32)]),
        compiler_params=_cparams(("arbitrary", "arbitrary")),
        name="moe_grouped_ffn",
    )(tile_expert, n_used, n_valid, xg, w_gu, w_gu, w_down)


def _sc_gather_rows(table, idx):
    n_idx = idx.shape[0]
    d = table.shape[1]

    @functools.partial(pl.kernel, out_type=jax.ShapeDtypeStruct((n_idx, d), table.dtype), mesh=_sc_mesh(),
                       name="sc_gather_rows")
    def gather(t_hbm, i_hbm, o_hbm):
        def body(i_vmem, o_vmem):
            pltpu.sync_copy(t_hbm.at[i_vmem.at[0]], o_vmem)

        pltpu.emit_pipeline(
            body,
            grid=(n_idx // SC_WINDOW,),
            in_specs=[pl.BlockSpec((1, SC_WINDOW), index_map=lambda i: (0, i))],
            out_specs=[pl.BlockSpec((SC_WINDOW, d), index_map=lambda i: (i, 0))],
            core_axis_name=("core", "subcore"),
            dimension_semantics=(pltpu.PARALLEL,),
        )(i_hbm, o_hbm)

    return gather(table, idx.reshape(1, n_idx))


def _combine_kernel(x_ref, rt_ref, gt_ref, gf_ref, *refs):
    y_refs, o_ref = refs[:-1], refs[-1]
    rt = rt_ref[...]
    tot = None
    for k in range(TOP_K):
        words = jnp.concatenate([y_refs[p * TOP_K + k][...] for p in range(SC_ROW_PARTS)], axis=1)
        term = rt[:, TOP_K + k:TOP_K + k + 1] * _unpack_bf16_pairs(words)
        tot = term if tot is None else tot + term
    x = x_ref[...] + gt_ref[...] * tot
    o_ref[...] = _rms(x, gf_ref[...])


def _combine(x, rt, mod1, g_final, ysel, *, seq, tm):
    n, d = x.shape
    tpb = seq // tm
    nt = n // tm
    row = lambda w: pl.BlockSpec((tm, w), lambda i: (i, 0))
    piece = lambda j: pl.BlockSpec((tm, ysel.shape[1]), lambda i: (j * nt + i, 0))
    n_pieces = SC_ROW_PARTS * TOP_K
    return pl.pallas_call(
        _combine_kernel,
        out_shape=jax.ShapeDtypeStruct((n, d), F32),
        grid=(nt,),
        in_specs=[row(d), row(LANES), _mod_spec(5, tpb, d), pl.BlockSpec(g_final.shape, lambda i: (0, 0))]
                 + [piece(j) for j in range(n_pieces)],
        out_specs=row(d),
        compiler_params=_cparams(("parallel",)),
        name="moe_combine",
    )(x, rt, mod1, g_final, *([ysel] * n_pieces))


def _pad_last(a, width):
    return jnp.pad(a, [(0, 0)] * (a.ndim - 1) + [(0, width - a.shape[-1])])


def kernel(x, c, positions, w_mod, b_mod, g_mix, g_ffn, w_a_down, g_q_lat, g_kv_lat, w_uq, w_ukv, w_oa,
           w_mod_kv, b_mod_kv, g_kv, w_kv_sb, w_q_sb, w_o_sb, w_ffn_gu, w_ffn_down, w_router, b_router,
           w_exp_gu, w_exp_down, g_final):
    bsz, seq, d = x.shape
    n = bsz * seq
    q_lora, kv_lora = g_q_lat.shape[1], g_kv_lat.shape[1]
    n_exp = w_router.shape[-1]
    d_ff = w_ffn_down.shape[1]
    tm = min(TOKEN_TILE, seq)
    t_mla = min(MLA_TILE, seq)
    t_sb = min(SB_TILE, seq)
    tf = next((t for t in (FFN_CHUNK, 512) if d_ff % t == 0), d_ff)

    mod0 = _modvec(c, w_mod, 0, b_mod[0]).reshape(bsz, 6, 1, d)
    mod1 = _modvec(c, w_mod, 1, b_mod[1]).reshape(bsz, 6, 1, d)
    mod_kv = _modvec(c, w_mod_kv[None], 0, b_mod_kv).reshape(bsz, 2, 1, d)

    lat_w = q_lora + kv_lora + LANES
    wd = _pad_last(w_a_down[0], lat_w).astype(BF16)
    wq = w_uq[0].reshape(q_lora, MLA_HEADS, QK_NOPE + QK_ROPE)
    wqn = wq[:, :, :QK_NOPE].reshape(q_lora, MLA_HEADS * QK_NOPE).astype(BF16)
    wqr = _pad_last(wq[:, :, QK_NOPE:], LANES).reshape(q_lora, MLA_HEADS * LANES).astype(BF16)
    wkv = w_ukv[0].reshape(kv_lora, MLA_HEADS, QK_NOPE + V_DIM)
    wkn = wkv[:, :, :QK_NOPE].reshape(kv_lora, MLA_HEADS * QK_NOPE).astype(BF16)
    wv = wkv[:, :, QK_NOPE:].reshape(kv_lora, MLA_HEADS * V_DIM).astype(BF16)
    half = QK_ROPE // 2
    inv = ROPE_THETA ** (-jnp.arange(half, dtype=F32) / half)
    inv = jnp.tile(inv, ROPE_PACK).reshape(1, LANES)
    wr = _pad_last(w_router[0], LANES)
    wr_hi = wr.astype(BF16)
    wr_lo = (wr - wr_hi.astype(F32)).astype(BF16)
    br = _pad_last(b_router[0], LANES).reshape(1, LANES)

    xf = x.reshape(n, d)
    pos = positions.reshape(n // tm, ROPE_PACK, tm // ROPE_PACK).swapaxes(1, 2)
    pos = jnp.repeat(pos.reshape(n // ROPE_PACK, ROPE_PACK), QK_ROPE // 2, axis=1)
    row1 = lambda a: a.reshape(1, -1)

    q, k, v = _mla_proj(xf, pos, mod0, row1(g_mix[0]), wd, row1(g_q_lat[0]), row1(g_kv_lat[0]),
                        wqn, wqr, wkn, wv, inv, seq=seq, tm=tm)
    o = _mla_attn(q.reshape(bsz, seq, -1), k.reshape(bsz, seq, -1), v.reshape(bsz, seq, -1), t=t_mla)
    x2, hk, hm = _ffn(o.reshape(n, -1), w_oa[0].astype(BF16), xf, mod0, row1(g_ffn[0]),
                      w_ffn_gu[0].astype(BF16), w_ffn_down[0].astype(BF16),
                      row1(g_kv), mod_kv, row1(g_mix[1]), mod1, seq=seq, tm=tm, tf=tf)
    kv = _linear(hk, w_kv_sb.astype(BF16), tm=tm, tn=w_kv_sb.shape[1], name="kv_proj")
    q_scale = math.log2(math.e) / math.sqrt(SB_HEAD_DIM)
    qs = _linear(hm, (w_q_sb[0] * q_scale).astype(BF16), tm=tm, tn=1024, name="q_proj")
    o = _sb_attn(qs.reshape(bsz, seq, -1), kv.reshape(bsz, seq, -1), t=t_sb)
    x3, h, rt, rtt, counts = _out_proj_router(o.reshape(n, -1), w_o_sb[0].astype(BF16), x2, mod1,
                                              row1(g_ffn[1]), wr_hi, wr_lo, br, n_exp, seq=seq, tm=tm)
    n_tiles = TOP_K * n // tm + n_exp
    n_rows = n_tiles * tm
    tile_base, tile_expert, n_used, n_valid = _route_plan(counts, n_exp, tm, n_tiles)
    picks = _picks(rtt, tile_base, n_rows, tm=tm)
    xg = _sc_scatter_rows(h.reshape(SC_ROW_PARTS * n, -1), [picks[:, k].reshape(-1) for k in range(TOP_K)],
                          SC_ROW_PARTS * n_rows)
    yg = _grouped_ffn(tile_expert, n_used, n_valid, xg.reshape(SC_ROW_PARTS, n_rows, -1),
                      w_exp_gu[0].astype(BF16), w_exp_down[0].astype(BF16), tm=tm, tf=tf)
    ysel = _sc_gather_rows(yg.reshape(SC_ROW_PARTS * n_rows, -1), picks.reshape(-1))
    out = _combine(x3, rt, mod1, row1(g_final), ysel, seq=seq, tm=tm)
    return out.reshape(bsz, seq, d)
```

```python
import functools
import math

import jax
import jax.numpy as jnp
from jax import lax
from jax.experimental import pallas as pl
from jax.experimental.pallas import tpu as pltpu
from jax.experimental.pallas import tpu_sc as plsc

F32 = jnp.float32
BF16 = jnp.bfloat16

EPS = 1e-6
MLA_HEADS = 8
QK_NOPE = 128
QK_ROPE = 64
V_DIM = 128
ROPE_THETA = 10000.0
SB_HEADS = 8
SB_HEAD_DIM = 128
TOP_K = 2

LANES = 128
QK_PAD = 256
ROPE_PACK = LANES // (QK_ROPE // 2)
VMEM_LIMIT = 48 * 1024 * 1024
VMEM_LIMIT_LARGE = 58 * 1024 * 1024
TOKEN_TILE = 512
MLA_TILE = 512
MLA_BLOCKS_PER_TRIP = 4
SB_TILE = 512
SB_BLOCKS_PER_TRIP = 2
SB_ZERO_LOG2 = -140.0
FFN_CHUNK = 1792
SC_CORES = 2
SC_SUBCORES = 16
SC_WINDOW = 128
SC_ROW_PARTS = 2


def _cparams(sem, vmem_limit=VMEM_LIMIT):
    return pltpu.CompilerParams(dimension_semantics=sem, vmem_limit_bytes=vmem_limit)


def _rms(x, g):
    return x * lax.rsqrt(jnp.mean(x * x, axis=-1, keepdims=True) + EPS) * g


def _modulate(h, shift, scale):
    return h * (1.0 + scale) + shift


def _split_bf16(a):
    hi = a.astype(BF16)
    lo = (a - hi.astype(F32)).astype(BF16)
    return hi, lo


U32 = jnp.uint32
HIGH_HALF = 0xFFFF0000


def _pack_bf16_pairs(a):
    half = a.shape[1] // 2
    bits = lambda v: lax.bitcast_convert_type(v.astype(BF16).astype(F32), U32)
    return (bits(a[:, :half]) >> 16) | (bits(a[:, half:]) & U32(HIGH_HALF))


def _unpack_bf16_pairs(w):
    lo = lax.bitcast_convert_type(w << 16, F32)
    hi = lax.bitcast_convert_type(w & U32(HIGH_HALF), F32)
    return jnp.concatenate([lo, hi], axis=1)


def _dot(a, b):
    return jnp.dot(a, b, preferred_element_type=F32)


def _dot_nt(a, b):
    return lax.dot_general(a, b, (((1,), (1,)), ((), ())), preferred_element_type=F32)


def _modvec_kernel(c_ref, w_ref, b_ref, o_ref):
    c = c_ref[...]
    sc = c * (1.0 / (1.0 + jnp.exp(-c)))
    a_hi, a_lo = _split_bf16(sc)
    w_hi, w_lo = _split_bf16(w_ref[...])
    o_ref[...] = _dot(a_hi, w_hi) + _dot(a_lo, w_hi) + _dot(a_hi, w_lo) + b_ref[...]


def _modvec(c, w, layer, b, tn=512):
    bsz, d = c.shape
    n = w.shape[2]
    return pl.pallas_call(
        _modvec_kernel,
        out_shape=jax.ShapeDtypeStruct((bsz, n), F32),
        grid=(n // tn,),
        in_specs=[pl.BlockSpec((bsz, d), lambda j: (0, 0)),
                  pl.BlockSpec((None, d, tn), lambda j: (layer, 0, j)),
                  pl.BlockSpec((1, tn), lambda j: (0, j))],
        out_specs=pl.BlockSpec((bsz, tn), lambda j: (0, j)),
        compiler_params=_cparams(("arbitrary",)),
        name="modvec",
    )(c, w, b.reshape(1, n))


def _mod_spec(chunk, tiles_per_batch, d):
    return pl.BlockSpec((None, None, 1, d), lambda i, *_: (i // tiles_per_batch, chunk, 0, 0))


def _mla_proj_kernel(x_ref, pos_ref, sh_ref, sc_ref, g_ref, wd_ref, gq_ref, gkv_ref,
                     wqn_ref, wqr_ref, wkn_ref, wv_ref, inv_ref,
                     q_ref, k_ref, v_ref, *, q_lora, kv_lora):
    x = x_ref[...]
    h = _modulate(_rms(x, g_ref[...]), sh_ref[...], sc_ref[...]).astype(BF16)
    lat = _dot(h, wd_ref[...])
    c_q = _rms(lat[:, :q_lora], gq_ref[...]).astype(BF16)
    c_kv = _rms(lat[:, q_lora:q_lora + kv_lora], gkv_ref[...]).astype(BF16)
    k_rot = lat[:, q_lora + kv_lora:]

    half = QK_ROPE // 2
    ang = pos_ref[...].astype(F32) * inv_ref[...]
    cos_p = jnp.cos(ang)
    sin_p = jnp.sin(ang)
    lane = lax.broadcasted_iota(jnp.int32, ang.shape, 1)
    cos, s_a, s_b = [], [], []
    for g in range(ROPE_PACK):
        shift = (LANES - g * half) % LANES
        cg = pltpu.roll(cos_p, shift, axis=1) if shift else cos_p
        sg = pltpu.roll(sin_p, shift, axis=1) if shift else sin_p
        cos.append(jnp.where(lane < half, cg, pltpu.roll(cg, half, axis=1)))
        s_a.append(jnp.where(lane < half, -sg, 0.0))
        s_b.append(jnp.where((lane >= half) & (lane < 2 * half), pltpu.roll(sg, half, axis=1), 0.0))
    cos, s_a, s_b = (jnp.concatenate(v, axis=0) for v in (cos, s_a, s_b))

    def rope(r):
        return (r * cos + pltpu.roll(r, LANES - half, axis=1) * s_a
                + pltpu.roll(r, half, axis=1) * s_b)

    k_rot = rope(k_rot).astype(BF16)
    q_nope = _dot(c_q, wqn_ref[...])
    q_rope = _dot(c_q, wqr_ref[...])
    k_nope = _dot(c_kv, wkn_ref[...])
    v_ref[...] = _dot(c_kv, wv_ref[...]).astype(BF16)
    for hd in range(MLA_HEADS):
        a, b = hd * LANES, (hd + 1) * LANES
        q_ref[:, hd * QK_PAD:hd * QK_PAD + LANES] = q_nope[:, a:b].astype(BF16)
        q_ref[:, hd * QK_PAD + LANES:(hd + 1) * QK_PAD] = rope(q_rope[:, a:b]).astype(BF16)
        k_ref[:, hd * QK_PAD:hd * QK_PAD + LANES] = k_nope[:, a:b].astype(BF16)
        k_ref[:, hd * QK_PAD + LANES:(hd + 1) * QK_PAD] = k_rot


def _mla_proj(x, pos, mod, g_mix, wd, gq, gkv, wqn, wqr, wkn, wv, inv, *, seq, tm):
    n, d = x.shape
    tpb = seq // tm
    q_lora, kv_lora = gq.shape[1], gkv.shape[1]
    hq = MLA_HEADS * QK_PAD
    hv = MLA_HEADS * V_DIM
    full = lambda a: pl.BlockSpec(a.shape, lambda i: (0,) * a.ndim)
    row = lambda w: pl.BlockSpec((tm, w), lambda i: (i, 0))
    return pl.pallas_call(
        functools.partial(_mla_proj_kernel, q_lora=q_lora, kv_lora=kv_lora),
        out_shape=(jax.ShapeDtypeStruct((n, hq), BF16), jax.ShapeDtypeStruct((n, hq), BF16),
                   jax.ShapeDtypeStruct((n, hv), BF16)),
        grid=(n // tm,),
        in_specs=[row(d), pl.BlockSpec((tm // ROPE_PACK, LANES), lambda i: (i, 0)), _mod_spec(0, tpb, d), _mod_spec(1, tpb, d), full(g_mix), full(wd),
                  full(gq), full(gkv), full(wqn), full(wqr), full(wkn), full(wv), full(inv)],
        out_specs=(row(hq), row(hq), row(hv)),
        compiler_params=_cparams(("parallel",)),
        name="mla_proj",
    )(x, pos, mod, mod, g_mix, wd, gq, gkv, wqn, wqr, wkn, wv, inv)


def _lane_chunks(a):
    return [a[:, c * LANES:(c + 1) * LANES] for c in range(a.shape[1] // LANES)]


def _for_blocks(n, step, group):
    def trip(p, carry):
        step(group * p, group)
        return carry

    lax.fori_loop(0, n // group, trip, 0)
    size = group // 2
    while size:
        @pl.when((n // size) % 2 == 1)
        def _(size=size):
            step(n // (2 * size) * (2 * size), size)
        size //= 2


def _mla_attn_kernel(q_ref, k_ref, v_ref, o_ref, s_ref, m_ref, acc_ref, *, t, scale, heads):
    i = pl.program_id(2)

    def lane_max(s):
        m = None
        for sc in _lane_chunks(s):
            m = sc if m is None else jnp.maximum(m, sc)
        return m

    def scores(h, j):
        start = pl.multiple_of(j * t, t)
        hs = slice(h * QK_PAD, (h + 1) * QK_PAD)
        return _dot_nt(q_ref[0, :, hs], k_ref[0, pl.ds(start, t), hs])

    r_minus_c = (lax.broadcasted_iota(jnp.int32, (t, t), 0) - lax.broadcasted_iota(jnp.int32, (t, t), 1))
    m_ref[...] = jnp.full(m_ref.shape, -jnp.inf, F32)

    def pass1(j, nb):
        for h in range(heads):
            m = m_ref[h]
            for jj in range(nb):
                s = jnp.where(r_minus_c >= (j + jj - i) * t, scores(h, j + jj), -jnp.inf)
                s_ref[h, j + jj] = s
                m = jnp.maximum(m, lane_max(s))
            m_ref[h] = m

    _for_blocks(i + 1, pass1, MLA_BLOCKS_PER_TRIP)
    for h in range(heads):
        m_ref[h] = jnp.broadcast_to(jnp.max(m_ref[h], axis=1, keepdims=True), (t, LANES))
    acc_ref[...] = jnp.zeros(acc_ref.shape, F32)
    cst = scale * math.log2(math.e)

    def pass2(j, nb):
        start = pl.multiple_of(j * t, t)
        ones = jnp.ones((nb * t, LANES), BF16)
        for h in range(heads):
            m = m_ref[h]
            p = jnp.concatenate([jnp.exp2((sc - m) * cst).astype(BF16)
                                 for jj in range(nb) for sc in _lane_chunks(s_ref[h, j + jj])], axis=1)
            v_ext = jnp.concatenate([v_ref[0, pl.ds(start, nb * t), h * V_DIM:(h + 1) * V_DIM], ones], axis=1)
            acc_ref[h] += _dot(p, v_ext)

    _for_blocks(i + 1, pass2, MLA_BLOCKS_PER_TRIP)
    for h in range(heads):
        acc = acc_ref[h]
        o_ref[0, :, h * V_DIM:(h + 1) * V_DIM] = (acc[:, :V_DIM] / acc[:, V_DIM:]).astype(o_ref.dtype)


def _mla_attn(q, k, v, *, t, heads=2):
    bsz, seq, _ = q.shape
    scale = 1.0 / math.sqrt(QK_NOPE + QK_ROPE)
    wq, wv = heads * QK_PAD, heads * V_DIM
    return pl.pallas_call(
        functools.partial(_mla_attn_kernel, t=t, scale=scale, heads=heads),
        out_shape=jax.ShapeDtypeStruct((bsz, seq, MLA_HEADS * V_DIM), BF16),
        grid=(bsz, MLA_HEADS // heads, seq // t),
        in_specs=[pl.BlockSpec((1, t, wq), lambda b, g, i: (b, i, g)),
                  pl.BlockSpec((1, seq, wq), lambda b, g, i: (b, 0, g)),
                  pl.BlockSpec((1, seq, wv), lambda b, g, i: (b, 0, g))],
        out_specs=pl.BlockSpec((1, t, wv), lambda b, g, i: (b, i, g)),
        scratch_shapes=[pltpu.VMEM((heads, seq // t, t, t), F32), pltpu.VMEM((heads, t, LANES), F32),
                        pltpu.VMEM((heads, t, V_DIM + LANES), F32)],
        compiler_params=_cparams(("parallel", "parallel", "arbitrary")),
        name="mla_attn",
    )(q, k, v)


def _sb_attn_kernel(q_ref, k_ref, v_ref, w_ref, o_ref, c_ref, acc_ref, kmax_ref, bound_ref, *, t, heads):
    i = pl.program_id(2)
    d = SB_HEAD_DIM
    c_ref[...] = jnp.zeros(c_ref.shape, F32)
    acc_ref[...] = jnp.zeros(acc_ref.shape, F32)
    sub = t // 2 if t % (2 * LANES) == 0 else t
    r_iota = lax.broadcasted_iota(jnp.int32, (sub, LANES), 0)
    c_iota = lax.broadcasted_iota(jnp.int32, (sub, LANES), 1)

    def block(r0, nr, start, nk, key0):
        rows = slice(r0, r0 + nr)
        masks = []
        for cc in range(nk // LANES):
            off = None if key0 is None else key0 + cc * LANES - r0
            masks.append(None if off is None or off <= -LANES else c_iota + off < r_iota)
        for h in range(heads):
            hs = slice(h * d, (h + 1) * d)
            zz = _dot_nt(q_ref[0, rows, hs], k_ref[0, pl.ds(start, nk), hs])
            zneg = -zz
            sp = jnp.log2(1.0 + jnp.exp2(jnp.minimum(zz, zneg)))
            log_1m = jnp.minimum(zneg, 0.0) - sp
            carry = c_ref[h, rows]
            a_chunks = [None] * (nk // LANES)
            for cc in reversed(range(nk // LANES)):
                cs = slice(cc * LANES, (cc + 1) * LANES)
                l1m = log_1m[:, cs] if masks[cc] is None else jnp.where(masks[cc], log_1m[:, cs], 0.0)
                hi, lo = _split_bf16(l1m)
                y = _dot(jnp.concatenate([hi, lo], axis=1), w_ref[...])
                a = jnp.exp2(zz[:, cs] + y[:, :LANES] + carry)
                carry = carry + y[:, LANES:]
                if masks[cc] is not None:
                    a = jnp.where(masks[cc], a, 0.0)
                a_chunks[cc] = a.astype(BF16)
            c_ref[h, rows] = carry
            acc_ref[h, rows] += _dot(jnp.concatenate(a_chunks, axis=1), v_ref[0, pl.ds(start, nk), hs])

    def diagonal_tile():
        for band in range(t // sub):
            block(band * sub, sub, pl.multiple_of(i * t, t), (band + 1) * sub, 0)

    def left_blocks(n, nb):
        for jj in range(nb):
            block(0, t, pl.multiple_of((i - 1 - n - jj) * t, t), t, None)

    @pl.when(i == 0)
    def _():
        diagonal_tile()

    @pl.when(i > 0)
    def _():
        diagonal_tile()
        left_blocks(0, 1)

    @pl.when(i == 0)
    def _():
        for h in range(heads):
            kmax = jnp.max(jnp.abs(k_ref[0, :, h * d:(h + 1) * d].astype(F32)))
            kmax_ref[h] = jnp.full(kmax_ref.shape[1:], kmax, F32)

    for h in range(heads):
        q_l1 = jnp.sum(jnp.abs(q_ref[0, :, h * d:(h + 1) * d].astype(F32)), axis=1, keepdims=True)
        bound_ref[h] = q_l1 * kmax_ref[h, 0:1, :]

    def more_to_add():
        worst = None
        for h in range(heads):
            top = jnp.max(c_ref[h] + bound_ref[h])
            worst = top if worst is None else jnp.maximum(worst, top)
        return worst >= SB_ZERO_LOG2

    def trip(state):
        n, _ = state
        left_blocks(n, SB_BLOCKS_PER_TRIP)
        return n + SB_BLOCKS_PER_TRIP, more_to_add()

    n, go = lax.while_loop(lambda s: s[1] & (s[0] + SB_BLOCKS_PER_TRIP <= i), trip,
                           (jnp.int32(1), (i > 0) & more_to_add()))
    size = SB_BLOCKS_PER_TRIP // 2
    while size:
        take = go & (n + size <= i)

        @pl.when(take)
        def _(n=n, size=size):
            left_blocks(n, size)

        n = n + jnp.where(take, size, 0)
        size //= 2
    for h in range(heads):
        o_ref[0, :, h * d:(h + 1) * d] = acc_ref[h].astype(o_ref.dtype)


def _sb_attn(q, kv, *, t, heads=4):
    bsz, seq, _ = q.shape
    w = heads * SB_HEAD_DIM
    groups = SB_HEADS // heads
    tri = (jnp.arange(LANES)[:, None] >= jnp.arange(LANES)[None, :]).astype(BF16)
    half = jnp.concatenate([tri, jnp.ones((LANES, LANES), BF16)], axis=1)
    w_sum = jnp.concatenate([half, half], axis=0)
    return pl.pallas_call(
        functools.partial(_sb_attn_kernel, t=t, heads=heads),
        out_shape=jax.ShapeDtypeStruct((bsz, seq, SB_HEADS * SB_HEAD_DIM), BF16),
        grid=(bsz, groups, seq // t),
        in_specs=[pl.BlockSpec((1, t, w), lambda b, g, i: (b, i, g)),
                  pl.BlockSpec((1, seq, w), lambda b, g, i: (b, 0, g)),
                  pl.BlockSpec((1, seq, w), lambda b, g, i: (b, 0, groups + g)),
                  pl.BlockSpec(w_sum.shape, lambda b, g, i: (0, 0))],
        out_specs=pl.BlockSpec((1, t, w), lambda b, g, i: (b, i, g)),
        scratch_shapes=[pltpu.VMEM((heads, t, LANES), F32), pltpu.VMEM((heads, t, SB_HEAD_DIM), F32),
                        pltpu.VMEM((heads, 8, LANES), F32), pltpu.VMEM((heads, t, LANES), F32)],
        compiler_params=_cparams(("parallel", "parallel", "arbitrary")),
        name="sb_attn",
    )(q, kv, kv, w_sum)


def _out_proj_kernel(o_ref, w_ref, x_ref, gt_ref, g_ref, sh_ref, sc_ref, wr_hi_ref, wr_lo_ref, br_ref, tri_ref,
                     xo_ref, h_ref, rt_ref, rtt_ref, cnt_ref, *, n_experts):
    x = x_ref[...] + gt_ref[...] * _dot(o_ref[...], w_ref[...])
    xo_ref[...] = x
    h = _modulate(_rms(x, g_ref[...]), sh_ref[...], sc_ref[...])
    h_hi = h.astype(BF16)
    words = _pack_bf16_pairs(h)
    for part in range(SC_ROW_PARTS):
        h_ref[part] = words[:, part * h_ref.shape[2]:(part + 1) * h_ref.shape[2]]
    h_lo = (h - h_hi.astype(F32)).astype(BF16)
    both = _dot(h_hi, jnp.concatenate([wr_hi_ref[...], wr_lo_ref[...]], axis=1))
    logits = both[:, :LANES] + both[:, LANES:] + _dot(h_lo, wr_hi_ref[...]) + br_ref[...]
    lane = lax.broadcasted_iota(jnp.int32, logits.shape, 1).astype(F32)
    lg = jnp.where(lane < n_experts, logits, -jnp.inf)
    m1 = jnp.max(lg, axis=1, keepdims=True)
    i1 = jnp.min(jnp.where(lg == m1, lane, float(LANES)), axis=1, keepdims=True)
    lg2 = jnp.where(lane == i1, -jnp.inf, lg)
    m2 = jnp.max(lg2, axis=1, keepdims=True)
    i2 = jnp.min(jnp.where(lg2 == m2, lane, float(LANES)), axis=1, keepdims=True)
    e2 = jnp.exp(m2 - m1)
    den = 1.0 + e2
    sel = jnp.where((lane == i1) | (lane == i2), 1.0, 0.0)
    prefix = _dot(tri_ref[...], sel.astype(BF16))
    r1 = jnp.sum(jnp.where(lane == i1, prefix, 0.0), axis=1, keepdims=True)
    r2 = jnp.sum(jnp.where(lane == i2, prefix, 0.0), axis=1, keepdims=True)
    cnt_ref[...] = jnp.sum(sel, axis=0, keepdims=True)
    rt = jnp.zeros_like(logits)
    for k, val in enumerate((i1, i2, 1.0 / den, e2 / den, r1, r2)):
        rt = jnp.where(lane == k, val, rt)
    rt_ref[...] = rt
    rtt_ref[...] = rt.T[:PICK_ROWS]


def _out_proj_router(o, w, x, mod, g, wr_hi, wr_lo, br, n_experts, *, seq, tm):
    n, d = x.shape
    tpb = seq // tm
    dpp = d // 2 // SC_ROW_PARTS
    tri = (jnp.arange(tm)[:, None] > jnp.arange(tm)[None, :]).astype(BF16)
    full = lambda a: pl.BlockSpec(a.shape, lambda i: (0,) * a.ndim)
    row = lambda wd: pl.BlockSpec((tm, wd), lambda i: (i, 0))
    return pl.pallas_call(
        functools.partial(_out_proj_kernel, n_experts=n_experts),
        out_shape=(jax.ShapeDtypeStruct((n, d), F32), jax.ShapeDtypeStruct((SC_ROW_PARTS, n, dpp), U32),
                   jax.ShapeDtypeStruct((n, LANES), F32), jax.ShapeDtypeStruct((PICK_ROWS, n), F32),
                   jax.ShapeDtypeStruct((n // tm, 1, LANES), F32)),
        grid=(n // tm,),
        in_specs=[row(o.shape[1]), full(w), row(d), _mod_spec(2, tpb, d), full(g),
                  _mod_spec(3, tpb, d), _mod_spec(4, tpb, d), full(wr_hi), full(wr_lo), full(br), full(tri)],
        out_specs=(row(d), pl.BlockSpec((SC_ROW_PARTS, tm, dpp), lambda i: (0, i, 0)), row(LANES),
                   pl.BlockSpec((PICK_ROWS, tm), lambda i: (0, i)),
                   pl.BlockSpec((None, 1, LANES), lambda i: (i, 0, 0))),
        compiler_params=_cparams(("parallel",)),
        name="out_proj_router",
    )(o, w, x, mod, g, mod, mod, wr_hi, wr_lo, br, tri)


def _silu(g):
    return g * (1.0 / (1.0 + jnp.exp(-g)))


def _ffn_kernel(o_ref, wo_ref, x_ref, gt1_ref, gf_ref, sh2_ref, sc2_ref, wg_ref, wu_ref, wd_ref, gt2_ref,
                gkv_ref, shk_ref, sck_ref, gm_ref, shm_ref, scm_ref,
                xo_ref, hk_ref, hm_ref, x1_ref, h_ref, acc_ref):
    i = pl.program_id(0)
    f = pl.program_id(1)
    n_tiles = pl.num_programs(0) - 1

    def open_tile(slot):
        x1 = x_ref[...] + gt1_ref[...] * _dot(o_ref[...], wo_ref[...])
        x1_ref[slot] = x1
        h_ref[...] = _modulate(_rms(x1, gf_ref[...]), sh2_ref[...], sc2_ref[...]).astype(BF16)
        acc_ref[slot] = jnp.zeros(acc_ref.shape[1:], F32)

    def close_tile(slot):
        x = x1_ref[slot] + gt2_ref[...] * acc_ref[slot]
        xo_ref[...] = x
        y = x * lax.rsqrt(jnp.mean(x * x, axis=-1, keepdims=True) + EPS)
        hk_ref[...] = _modulate(y * gkv_ref[...], shk_ref[...], sck_ref[...]).astype(BF16)
        hm_ref[...] = _modulate(y * gm_ref[...], shm_ref[...], scm_ref[...]).astype(BF16)

    for parity in range(2):
        mine = (f == 0) & (i % 2 == parity)

        @pl.when(mine & (i == 0))
        def _(parity=parity):
            open_tile(parity)

        @pl.when(mine & (i > 0) & (i < n_tiles))
        def _(parity=parity):
            open_tile(parity)
            close_tile(1 - parity)

        @pl.when(mine & (i == n_tiles))
        def _(parity=parity):
            close_tile(1 - parity)

    @pl.when(i < n_tiles)
    def _():
        h = h_ref[...]
        a = _silu(_dot(h, wg_ref[...])) * _dot(h, wu_ref[...])
        acc_ref[i % 2] += _dot(a.astype(BF16), wd_ref[...])


def _ffn(o, w_o, x, mod0, g_ffn, w_gu, w_down, g_kv, mod_kv, g_mix1, mod1, *, seq, tm, tf):
    n, d = x.shape
    d_ff = w_down.shape[0]
    nf = d_ff // tf
    tpb = seq // tm
    nt = n // tm
    opened = lambda i: jnp.minimum(i, nt - 1)
    closed = lambda i: jnp.maximum(i - 1, 0)
    chunk = lambda i, f: jnp.where(i < nt, f, nf - 1)
    full = lambda a: pl.BlockSpec(a.shape, lambda i, f: (0,) * a.ndim)
    row_in = lambda w=d: pl.BlockSpec((tm, w), lambda i, f: (opened(i), 0))
    row_out = lambda: pl.BlockSpec((tm, d), lambda i, f: (closed(i), 0))
    mod = lambda c, tile: pl.BlockSpec((None, None, 1, d), lambda i, f: (tile(i) // tpb, c, 0, 0))
    return pl.pallas_call(
        _ffn_kernel,
        out_shape=(jax.ShapeDtypeStruct((n, d), F32), jax.ShapeDtypeStruct((n, d), BF16),
                   jax.ShapeDtypeStruct((n, d), BF16)),
        grid=(nt + 1, nf),
        in_specs=[row_in(o.shape[1]), full(w_o), row_in(), mod(2, opened), full(g_ffn),
                  mod(3, opened), mod(4, opened),
                  pl.BlockSpec((d, tf), lambda i, f: (0, chunk(i, f))),
                  pl.BlockSpec((d, tf), lambda i, f: (0, nf + chunk(i, f))),
                  pl.BlockSpec((tf, d), lambda i, f: (chunk(i, f), 0)),
                  mod(5, closed),
                  full(g_kv), mod(0, closed), mod(1, closed),
                  full(g_mix1), mod(0, closed), mod(1, closed)],
        out_specs=(row_out(), row_out(), row_out()),
        scratch_shapes=[pltpu.VMEM((2, tm, d), F32), pltpu.VMEM((tm, d), BF16), pltpu.VMEM((2, tm, d), F32)],
        compiler_params=_cparams(("arbitrary", "arbitrary"), vmem_limit=VMEM_LIMIT_LARGE),
        name="ffn_dense",
    )(o, w_o, x, mod0, g_ffn, mod0, mod0, w_gu, w_gu, w_down, mod0,
      g_kv, mod_kv, mod_kv, g_mix1, mod1, mod1)


def _linear_kernel(x_ref, w_ref, o_ref):
    o_ref[...] = _dot(x_ref[...], w_ref[...]).astype(o_ref.dtype)


def _linear(x, w, *, tm, tn, name):
    n, k = x.shape
    m = w.shape[1]
    return pl.pallas_call(
        _linear_kernel,
        out_shape=jax.ShapeDtypeStruct((n, m), BF16),
        grid=(m // tn, n // tm),
        in_specs=[pl.BlockSpec((tm, k), lambda j, i: (i, 0)), pl.BlockSpec((k, tn), lambda j, i: (0, j))],
        out_specs=pl.BlockSpec((tm, tn), lambda j, i: (i, j)),
        compiler_params=_cparams(("parallel", "parallel")),
        name=name,
    )(x, w)


def _route_plan(counts, n_exp, tm, n_tiles):
    cnt = counts[:, 0, :n_exp].astype(jnp.int32)
    sizes = jnp.sum(cnt, axis=0)
    padded = (sizes + tm - 1) // tm * tm
    ends = jnp.cumsum(padded)
    tile_base = (ends - padded)[None, :] + jnp.cumsum(cnt, axis=0) - cnt
    tile_start = jnp.arange(n_tiles, dtype=jnp.int32) * tm
    tile_expert = jnp.minimum(jnp.sum(tile_start[:, None] >= ends[None, :], axis=1), n_exp - 1)
    n_used = (ends[-1] // tm).reshape(1)
    n_valid = jnp.clip((ends - padded + sizes)[tile_expert] - tile_start, 0, tm)
    return tile_base, tile_expert.astype(jnp.int32), n_used.astype(jnp.int32), n_valid.astype(jnp.int32)


PICK_ROWS = 8


def _picks_kernel(rtt_ref, base_ref, o_ref, *, n_rows):
    rtt = rtt_ref[...]
    sub = lax.broadcasted_iota(jnp.int32, rtt.shape, 0)
    expert = sub.astype(F32)
    out = jnp.zeros_like(rtt)
    for k in range(TOP_K):
        chosen = expert == rtt[k:k + 1, :]
        d_k = (jnp.sum(jnp.where(chosen, base_ref[...], 0.0), axis=0, keepdims=True)
               + rtt[2 * TOP_K + k:2 * TOP_K + k + 1, :])
        for p in range(SC_ROW_PARTS):
            out = jnp.where(sub == p * TOP_K + k, d_k + float(p * n_rows), out)
    o_ref[...] = out.astype(jnp.int32)


def _picks(rtt, tile_base, n_rows, *, tm):
    n = rtt.shape[1]
    n_tok_tiles, n_exp = tile_base.shape
    assert n_exp <= PICK_ROWS
    base = jnp.pad(tile_base.astype(F32), ((0, 0), (0, PICK_ROWS - n_exp))).reshape(n_tok_tiles, PICK_ROWS, 1)
    out = pl.pallas_call(
        functools.partial(_picks_kernel, n_rows=n_rows),
        out_shape=jax.ShapeDtypeStruct((PICK_ROWS, n), jnp.int32),
        grid=(n // tm,),
        in_specs=[pl.BlockSpec((PICK_ROWS, tm), lambda i: (0, i)),
                  pl.BlockSpec((None, PICK_ROWS, 1), lambda i: (i, 0, 0))],
        out_specs=pl.BlockSpec((PICK_ROWS, tm), lambda i: (0, i)),
        compiler_params=_cparams(("parallel",)),
        name="moe_picks",
    )(rtt, base)
    return out[:SC_ROW_PARTS * TOP_K].reshape(SC_ROW_PARTS, TOP_K, n)


def _sc_mesh():
    return plsc.VectorSubcoreMesh(core_axis_name="core", subcore_axis_name="subcore",
                                  num_cores=SC_CORES, num_subcores=SC_SUBCORES)


def _sc_scatter_rows(x, idx_list, n_rows):
    m, d = x.shape

    @functools.partial(pl.kernel, out_type=jax.ShapeDtypeStruct((n_rows, d), x.dtype), mesh=_sc_mesh(),
                       scratch_types=[], name="sc_scatter_rows")
    def scatter(x_hbm, *refs):
        i_hbms, o_hbm = refs[:-1], refs[-1]

        def body(x_vmem, *i_vmems):
            for i_vmem in i_vmems:
                pltpu.sync_copy(x_vmem, o_hbm.at[i_vmem.at[0]])

        pltpu.emit_pipeline(
            body,
            grid=(m // SC_WINDOW,),
            in_specs=[pl.BlockSpec((SC_WINDOW, d), index_map=lambda i: (i, 0))]
                     + [pl.BlockSpec((1, SC_WINDOW), index_map=lambda i: (0, i))] * len(idx_list),
            out_specs=[],
            core_axis_name=("core", "subcore"),
            dimension_semantics=(pltpu.PARALLEL,),
        )(x_hbm, *i_hbms)

    return scatter(x, *[idx.reshape(1, m) for idx in idx_list])


def _grouped_ffn_kernel(te_ref, nu_ref, nv_ref, x_ref, wg_ref, wu_ref, wd_ref, y_ref, hb_ref, acc_ref):
    del te_ref
    i = pl.program_id(0)
    f = pl.program_id(1)
    used = i < nu_ref[0]
    last_f = f == pl.num_programs(1) - 1

    @pl.when(used & (f == 0))
    def _():
        words = jnp.concatenate([x_ref[p] for p in range(SC_ROW_PARTS)], axis=1)
        row = lax.broadcasted_iota(jnp.int32, words.shape, 0)
        words = jnp.where(row < nv_ref[i], words, U32(0))
        hb_ref[...] = _unpack_bf16_pairs(words).astype(BF16)
        acc_ref[...] = jnp.zeros(acc_ref.shape, F32)

    @pl.when(used)
    def _():
        h = hb_ref[...]
        a = _silu(_dot(h, wg_ref[0])) * _dot(h, wu_ref[0])
        acc_ref[...] += _dot(a.astype(BF16), wd_ref[0])

    @pl.when(used & last_f)
    def _():
        y = _pack_bf16_pairs(acc_ref[...])
        for part in range(SC_ROW_PARTS):
            y_ref[part] = y[:, part * y_ref.shape[2]:(part + 1) * y_ref.shape[2]]

    @pl.when(jnp.logical_not(used) & last_f)
    def _():
        y_ref[...] = jnp.zeros(y_ref.shape, y_ref.dtype)


def _grouped_ffn(tile_expert, n_used, n_valid, xg, w_gu, w_down, *, tm, tf):
    parts, n_rows, dpp = xg.shape
    n_exp, d_ff, d = w_down.shape
    nf = d_ff // tf

    def wspec(shape, index):
        def index_map(i, f, te, nu, nv):
            return index(te[i], jnp.where(i < nu[0], f, nf - 1))
        return pl.BlockSpec(shape, index_map)

    rows = pl.BlockSpec((parts, tm, dpp), lambda i, f, te, nu, nv: (0, i, 0))
    return pl.pallas_call(
        _grouped_ffn_kernel,
        out_shape=jax.ShapeDtypeStruct(xg.shape, xg.dtype),
        grid_spec=pltpu.PrefetchScalarGridSpec(
            num_scalar_prefetch=3,
            grid=(n_rows // tm, nf),
            in_specs=[rows,
                      wspec((1, d, tf), lambda e, f: (e, 0, f)),
                      wspec((1, d, tf), lambda e, f: (e, 0, nf + f)),
                      wspec((1, tf, d), lambda e, f: (e, f, 0))],
            out_specs=rows,
            scratch_shapes=[pltpu.VMEM((tm, d), BF16), pltpu.VMEM((tm, d), F32)]),
        compiler_params=_cparams(("arbitrary", "arbitrary")),
        name="moe_grouped_ffn",
    )(tile_expert, n_used, n_valid, xg, w_gu, w_gu, w_down)


def _sc_gather_rows(table, idx):
    n_idx = idx.shape[0]
    d = table.shape[1]

    @functools.partial(pl.kernel, out_type=jax.ShapeDtypeStruct((n_idx, d), table.dtype), mesh=_sc_mesh(),
                       name="sc_gather_rows")
    def gather(t_hbm, i_hbm, o_hbm):
        def body(i_vmem, o_vmem):
            pltpu.sync_copy(t_hbm.at[i_vmem.at[0]], o_vmem)

        pltpu.emit_pipeline(
            body,
            grid=(n_idx // SC_WINDOW,),
            in_specs=[pl.BlockSpec((1, SC_WINDOW), index_map=lambda i: (0, i))],
            out_specs=[pl.BlockSpec((SC_WINDOW, d), index_map=lambda i: (i, 0))],
            core_axis_name=("core", "subcore"),
            dimension_semantics=(pltpu.PARALLEL,),
        )(i_hbm, o_hbm)

    return gather(table, idx.reshape(1, n_idx))


def _combine_kernel(x_ref, rt_ref, gt_ref, gf_ref, *refs):
    y_refs, o_ref = refs[:-1], refs[-1]
    rt = rt_ref[...]
    tot = None
    for k in range(TOP_K):
        words = jnp.concatenate([y_refs[p * TOP_K + k][...] for p in range(SC_ROW_PARTS)], axis=1)
        term = rt[:, TOP_K + k:TOP_K + k + 1] * _unpack_bf16_pairs(words)
        tot = term if tot is None else tot + term
    x = x_ref[...] + gt_ref[...] * tot
    o_ref[...] = _rms(x, gf_ref[...])


def _combine(x, rt, mod1, g_final, ysel, *, seq, tm):
    n, d = x.shape
    tpb = seq // tm
    nt = n // tm
    row = lambda w: pl.BlockSpec((tm, w), lambda i: (i, 0))
    piece = lambda j: pl.BlockSpec((tm, ysel.shape[1]), lambda i: (j * nt + i, 0))
    n_pieces = SC_ROW_PARTS * TOP_K
    return pl.pallas_call(
        _combine_kernel,
        out_shape=jax.ShapeDtypeStruct((n, d), F32),
        grid=(nt,),
        in_specs=[row(d), row(LANES), _mod_spec(5, tpb, d), pl.BlockSpec(g_final.shape, lambda i: (0, 0))]
                 + [piece(j) for j in range(n_pieces)],
        out_specs=row(d),
        compiler_params=_cparams(("parallel",)),
        name="moe_combine",
    )(x, rt, mod1, g_final, *([ysel] * n_pieces))


def _pad_last(a, width):
    return jnp.pad(a, [(0, 0)] * (a.ndim - 1) + [(0, width - a.shape[-1])])


def kernel(x, c, positions, w_mod, b_mod, g_mix, g_ffn, w_a_down, g_q_lat, g_kv_lat, w_uq, w_ukv, w_oa,
           w_mod_kv, b_mod_kv, g_kv, w_kv_sb, w_q_sb, w_o_sb, w_ffn_gu, w_ffn_down, w_router, b_router,
           w_exp_gu, w_exp_down, g_final):
    bsz, seq, d = x.shape
    n = bsz * seq
    q_lora, kv_lora = g_q_lat.shape[1], g_kv_lat.shape[1]
    n_exp = w_router.shape[-1]
    d_ff = w_ffn_down.shape[1]
    tm = min(TOKEN_TILE, seq)
    t_mla = min(MLA_TILE, seq)
    t_sb = min(SB_TILE, seq)
    tf = next((t for t in (FFN_CHUNK, 512) if d_ff % t == 0), d_ff)

    mod0 = _modvec(c, w_mod, 0, b_mod[0]).reshape(bsz, 6, 1, d)
    mod1 = _modvec(c, w_mod, 1, b_mod[1]).reshape(bsz, 6, 1, d)
    mod_kv = _modvec(c, w_mod_kv[None], 0, b_mod_kv).reshape(bsz, 2, 1, d)

    lat_w = q_lora + kv_lora + LANES
    wd = _pad_last(w_a_down[0], lat_w).astype(BF16)
    wq = w_uq[0].reshape(q_lora, MLA_HEADS, QK_NOPE + QK_ROPE)
    wqn = wq[:, :, :QK_NOPE].reshape(q_lora, MLA_HEADS * QK_NOPE).astype(BF16)
    wqr = _pad_last(wq[:, :, QK_NOPE:], LANES).reshape(q_lora, MLA_HEADS * LANES).astype(BF16)
    wkv = w_ukv[0].reshape(kv_lora, MLA_HEADS, QK_NOPE + V_DIM)
    wkn = wkv[:, :, :QK_NOPE].reshape(kv_lora, MLA_HEADS * QK_NOPE).astype(BF16)
    wv = wkv[:, :, QK_NOPE:].reshape(kv_lora, MLA_HEADS * V_DIM).astype(BF16)
    half = QK_ROPE // 2
    inv = ROPE_THETA ** (-jnp.arange(half, dtype=F32) / half)
    inv = jnp.tile(inv, ROPE_PACK).reshape(1, LANES)
    wr = _pad_last(w_router[0], LANES)
    wr_hi = wr.astype(BF16)
    wr_lo = (wr - wr_hi.astype(F32)).astype(BF16)
    br = _pad_last(b_router[0], LANES).reshape(1, LANES)

    xf = x.reshape(n, d)
    pos = positions.reshape(n // tm, ROPE_PACK, tm // ROPE_PACK).swapaxes(1, 2)
    pos = jnp.repeat(pos.reshape(n // ROPE_PACK, ROPE_PACK), QK_ROPE // 2, axis=1)
    row1 = lambda a: a.reshape(1, -1)

    q, k, v = _mla_proj(xf, pos, mod0, row1(g_mix[0]), wd, row1(g_q_lat[0]), row1(g_kv_lat[0]),
                        wqn, wqr, wkn, wv, inv, seq=seq, tm=tm)
    o = _mla_attn(q.reshape(bsz, seq, -1), k.reshape(bsz, seq, -1), v.reshape(bsz, seq, -1), t=t_mla)
    x2, hk, hm = _ffn(o.reshape(n, -1), w_oa[0].astype(BF16), xf, mod0, row1(g_ffn[0]),
                      w_ffn_gu[0].astype(BF16), w_ffn_down[0].astype(BF16),
                      row1(g_kv), mod_kv, row1(g_mix[1]), mod1, seq=seq, tm=tm, tf=tf)
    kv = _linear(hk, w_kv_sb.astype(BF16), tm=tm, tn=w_kv_sb.shape[1], name="kv_proj")
    q_scale = math.log2(math.e) / math.sqrt(SB_HEAD_DIM)
    qs = _linear(hm, (w_q_sb[0] * q_scale).astype(BF16), tm=tm, tn=1024, name="q_proj")
    o = _sb_attn(qs.reshape(bsz, seq, -1), kv.reshape(bsz, seq, -1), t=t_sb)
    x3, h, rt, rtt, counts = _out_proj_router(o.reshape(n, -1), w_o_sb[0].astype(BF16), x2, mod1,
                                              row1(g_ffn[1]), wr_hi, wr_lo, br, n_exp, seq=seq, tm=tm)
    n_tiles = TOP_K * n // tm + n_exp
    n_rows = n_tiles * tm
    tile_base, tile_expert, n_used, n_valid = _route_plan(counts, n_exp, tm, n_tiles)
    picks = _picks(rtt, tile_base, n_rows, tm=tm)
    xg = _sc_scatter_rows(h.reshape(SC_ROW_PARTS * n, -1), [picks[:, k].reshape(-1) for k in range(TOP_K)],
                          SC_ROW_PARTS * n_rows)
    yg = _grouped_ffn(tile_expert, n_used, n_valid, xg.reshape(SC_ROW_PARTS, n_rows, -1),
                      w_exp_gu[0].astype(BF16), w_exp_down[0].astype(BF16), tm=tm, tf=tf)
    ysel = _sc_gather_rows(yg.reshape(SC_ROW_PARTS * n_rows, -1), picks.reshape(-1))
    out = _combine(x3, rt, mod1, row1(g_final), ysel, seq=seq, tm=tm)
    return out.reshape(bsz, seq, d)
```

```python
import functools
import math

import jax
import jax.numpy as jnp
from jax import lax
from jax.experimental import pallas as pl
from jax.experimental.pallas import tpu as pltpu
from jax.experimental.pallas import tpu_sc as plsc

F32 = jnp.float32
BF16 = jnp.bfloat16

EPS = 1e-6
MLA_HEADS = 8
QK_NOPE = 128
QK_ROPE = 64
V_DIM = 128
ROPE_THETA = 10000.0
SB_HEADS = 8
SB_HEAD_DIM = 128
TOP_K = 2

LANES = 128
QK_PAD = 256
ROPE_PACK = LANES // (QK_ROPE // 2)
VMEM_LIMIT = 48 * 1024 * 1024
VMEM_LIMIT_LARGE = 58 * 1024 * 1024
TOKEN_TILE = 512
LINEAR_TILE = 1024
MLA_TILE = 512
MLA_BLOCKS_PER_TRIP = 4
SB_TILE = 512
SB_BLOCKS_PER_TRIP = 2
SB_ZERO_LOG2 = -140.0
FFN_CHUNK = 1792
SC_CORES = 2
SC_SUBCORES = 16
SC_WINDOW = 128
SC_ROW_PARTS = 2


def _cparams(sem, vmem_limit=VMEM_LIMIT):
    return pltpu.CompilerParams(dimension_semantics=sem, vmem_limit_bytes=vmem_limit)


def _rms(x, g):
    return x * lax.rsqrt(jnp.mean(x * x, axis=-1, keepdims=True) + EPS) * g


def _modulate(h, shift, scale):
    return h * (1.0 + scale) + shift


def _split_bf16(a):
    hi = a.astype(BF16)
    lo = (a - hi.astype(F32)).astype(BF16)
    return hi, lo


U32 = jnp.uint32
HIGH_HALF = 0xFFFF0000


def _pack_bf16_pairs(a):
    half = a.shape[1] // 2
    bits = lambda v: lax.bitcast_convert_type(v.astype(BF16).astype(F32), U32)
    return (bits(a[:, :half]) >> 16) | (bits(a[:, half:]) & U32(HIGH_HALF))


def _unpack_bf16_pairs(w):
    lo = lax.bitcast_convert_type(w << 16, F32)
    hi = lax.bitcast_convert_type(w & U32(HIGH_HALF), F32)
    return jnp.concatenate([lo, hi], axis=1)


def _dot(a, b):
    return jnp.dot(a, b, preferred_element_type=F32)


def _dot_nt(a, b):
    return lax.dot_general(a, b, (((1,), (1,)), ((), ())), preferred_element_type=F32)


def _modvec_kernel(c_ref, w_ref, b_ref, o_ref):
    c = c_ref[...]
    sc = c * (1.0 / (1.0 + jnp.exp(-c)))
    a_hi, a_lo = _split_bf16(sc)
    w_hi, w_lo = _split_bf16(w_ref[...])
    o_ref[...] = _dot(a_hi, w_hi) + _dot(a_lo, w_hi) + _dot(a_hi, w_lo) + b_ref[...]


def _modvec(c, w, layer, b, tn=512):
    bsz, d = c.shape
    n = w.shape[2]
    return pl.pallas_call(
        _modvec_kernel,
        out_shape=jax.ShapeDtypeStruct((bsz, n), F32),
        grid=(n // tn,),
        in_specs=[pl.BlockSpec((bsz, d), lambda j: (0, 0)),
                  pl.BlockSpec((None, d, tn), lambda j: (layer, 0, j)),
                  pl.BlockSpec((1, tn), lambda j: (0, j))],
        out_specs=pl.BlockSpec((bsz, tn), lambda j: (0, j)),
        compiler_params=_cparams(("arbitrary",)),
        name="modvec",
    )(c, w, b.reshape(1, n))


def _mod_spec(chunk, tiles_per_batch, d):
    return pl.BlockSpec((None, None, 1, d), lambda i, *_: (i // tiles_per_batch, chunk, 0, 0))


def _mla_proj_kernel(x_ref, pos_ref, sh_ref, sc_ref, g_ref, wd_ref, gq_ref, gkv_ref,
                     wqn_ref, wqr_ref, wkn_ref, wv_ref, inv_ref,
                     q_ref, k_ref, v_ref, *, q_lora, kv_lora):
    x = x_ref[...]
    h = _modulate(_rms(x, g_ref[...]), sh_ref[...], sc_ref[...]).astype(BF16)
    lat = _dot(h, wd_ref[...])
    c_q = _rms(lat[:, :q_lora], gq_ref[...]).astype(BF16)
    c_kv = _rms(lat[:, q_lora:q_lora + kv_lora], gkv_ref[...]).astype(BF16)
    k_rot = lat[:, q_lora + kv_lora:]

    half = QK_ROPE // 2
    ang = pos_ref[...].astype(F32) * inv_ref[...]
    cos_p = jnp.cos(ang)
    sin_p = jnp.sin(ang)
    lane = lax.broadcasted_iota(jnp.int32, ang.shape, 1)
    cos, s_a, s_b = [], [], []
    for g in range(ROPE_PACK):
        shift = (LANES - g * half) % LANES
        cg = pltpu.roll(cos_p, shift, axis=1) if shift else cos_p
        sg = pltpu.roll(sin_p, shift, axis=1) if shift else sin_p
        cos.append(jnp.where(lane < half, cg, pltpu.roll(cg, half, axis=1)))
        s_a.append(jnp.where(lane < half, -sg, 0.0))
        s_b.append(jnp.where((lane >= half) & (lane < 2 * half), pltpu.roll(sg, half, axis=1), 0.0))
    cos, s_a, s_b = (jnp.concatenate(v, axis=0) for v in (cos, s_a, s_b))

    def rope(r):
        return (r * cos + pltpu.roll(r, LANES - half, axis=1) * s_a
                + pltpu.roll(r, half, axis=1) * s_b)

    k_rot = rope(k_rot).astype(BF16)
    q_nope = _dot(c_q, wqn_ref[...])
    q_rope = _dot(c_q, wqr_ref[...])
    k_nope = _dot(c_kv, wkn_ref[...])
    v_ref[...] = _dot(c_kv, wv_ref[...]).astype(BF16)
    for hd in range(MLA_HEADS):
        a, b = hd * LANES, (hd + 1) * LANES
        q_ref[:, hd * QK_PAD:hd * QK_PAD + LANES] = q_nope[:, a:b].astype(BF16)
        q_ref[:, hd * QK_PAD + LANES:(hd + 1) * QK_PAD] = rope(q_rope[:, a:b]).astype(BF16)
        k_ref[:, hd * QK_PAD:hd * QK_PAD + LANES] = k_nope[:, a:b].astype(BF16)
        k_ref[:, hd * QK_PAD + LANES:(hd + 1) * QK_PAD] = k_rot


def _mla_proj(x, pos, mod, g_mix, wd, gq, gkv, wqn, wqr, wkn, wv, inv, *, seq, tm):
    n, d = x.shape
    tpb = seq // tm
    q_lora, kv_lora = gq.shape[1], gkv.shape[1]
    hq = MLA_HEADS * QK_PAD
    hv = MLA_HEADS * V_DIM
    full = lambda a: pl.BlockSpec(a.shape, lambda i: (0,) * a.ndim)
    row = lambda w: pl.BlockSpec((tm, w), lambda i: (i, 0))
    return pl.pallas_call(
        functools.partial(_mla_proj_kernel, q_lora=q_lora, kv_lora=kv_lora),
        out_shape=(jax.ShapeDtypeStruct((n, hq), BF16), jax.ShapeDtypeStruct((n, hq), BF16),
                   jax.ShapeDtypeStruct((n, hv), BF16)),
        grid=(n // tm,),
        in_specs=[row(d), pl.BlockSpec((tm // ROPE_PACK, LANES), lambda i: (i, 0)), _mod_spec(0, tpb, d), _mod_spec(1, tpb, d), full(g_mix), full(wd),
                  full(gq), full(gkv), full(wqn), full(wqr), full(wkn), full(wv), full(inv)],
        out_specs=(row(hq), row(hq), row(hv)),
        compiler_params=_cparams(("parallel",)),
        name="mla_proj",
    )(x, pos, mod, mod, g_mix, wd, gq, gkv, wqn, wqr, wkn, wv, inv)


def _lane_chunks(a):
    return [a[:, c * LANES:(c + 1) * LANES] for c in range(a.shape[1] // LANES)]


def _for_blocks(n, step, group):
    def trip(p, carry):
        step(group * p, group)
        return carry

    lax.fori_loop(0, n // group, trip, 0)
    size = group // 2
    while size:
        @pl.when((n // size) % 2 == 1)
        def _(size=size):
            step(n // (2 * size) * (2 * size), size)
        size //= 2


def _mla_attn_kernel(q_ref, k_ref, v_ref, o_ref, s_ref, m_ref, acc_ref, *, t, scale, heads):
    i = pl.program_id(2)

    def lane_max(s):
        m = None
        for sc in _lane_chunks(s):
            m = sc if m is None else jnp.maximum(m, sc)
        return m

    def scores(h, j):
        start = pl.multiple_of(j * t, t)
        hs = slice(h * QK_PAD, (h + 1) * QK_PAD)
        return _dot_nt(q_ref[0, :, hs], k_ref[0, pl.ds(start, t), hs])

    r_minus_c = (lax.broadcasted_iota(jnp.int32, (t, t), 0) - lax.broadcasted_iota(jnp.int32, (t, t), 1))
    m_ref[...] = jnp.full(m_ref.shape, -jnp.inf, F32)

    def pass1(j, nb):
        for h in range(heads):
            m = m_ref[h]
            for jj in range(nb):
                s = jnp.where(r_minus_c >= (j + jj - i) * t, scores(h, j + jj), -jnp.inf)
                s_ref[h, j + jj] = s
                m = jnp.maximum(m, lane_max(s))
            m_ref[h] = m

    _for_blocks(i + 1, pass1, MLA_BLOCKS_PER_TRIP)
    for h in range(heads):
        m_ref[h] = jnp.broadcast_to(jnp.max(m_ref[h], axis=1, keepdims=True), (t, LANES))
    acc_ref[...] = jnp.zeros(acc_ref.shape, F32)
    cst = scale * math.log2(math.e)

    def pass2(j, nb):
        start = pl.multiple_of(j * t, t)
        ones = jnp.ones((nb * t, LANES), BF16)
        for h in range(heads):
            m = m_ref[h]
            p = jnp.concatenate([jnp.exp2((sc - m) * cst).astype(BF16)
                                 for jj in range(nb) for sc in _lane_chunks(s_ref[h, j + jj])], axis=1)
            v_ext = jnp.concatenate([v_ref[0, pl.ds(start, nb * t), h * V_DIM:(h + 1) * V_DIM], ones], axis=1)
            acc_ref[h] += _dot(p, v_ext)

    _for_blocks(i + 1, pass2, MLA_BLOCKS_PER_TRIP)
    for h in range(heads):
        acc = acc_ref[h]
        o_ref[0, :, h * V_DIM:(h + 1) * V_DIM] = (acc[:, :V_DIM] / acc[:, V_DIM:]).astype(o_ref.dtype)


def _mla_attn(q, k, v, *, t, heads=2):
    bsz, seq, _ = q.shape
    scale = 1.0 / math.sqrt(QK_NOPE + QK_ROPE)
    wq, wv = heads * QK_PAD, heads * V_DIM
    return pl.pallas_call(
        functools.partial(_mla_attn_kernel, t=t, scale=scale, heads=heads),
        out_shape=jax.ShapeDtypeStruct((bsz, seq, MLA_HEADS * V_DIM), BF16),
        grid=(bsz, MLA_HEADS // heads, seq // t),
        in_specs=[pl.BlockSpec((1, t, wq), lambda b, g, i: (b, i, g)),
                  pl.BlockSpec((1, seq, wq), lambda b, g, i: (b, 0, g)),
                  pl.BlockSpec((1, seq, wv), lambda b, g, i: (b, 0, g))],
        out_specs=pl.BlockSpec((1, t, wv), lambda b, g, i: (b, i, g)),
        scratch_shapes=[pltpu.VMEM((heads, seq // t, t, t), F32), pltpu.VMEM((heads, t, LANES), F32),
                        pltpu.VMEM((heads, t, V_DIM + LANES), F32)],
        compiler_params=_cparams(("parallel", "parallel", "arbitrary")),
        name="mla_attn",
    )(q, k, v)


def _sb_attn_kernel(q_ref, k_ref, v_ref, w_ref, o_ref, c_ref, acc_ref, kmax_ref, bound_ref, *, t, heads):
    i = pl.program_id(2)
    d = SB_HEAD_DIM
    c_ref[...] = jnp.zeros(c_ref.shape, F32)
    acc_ref[...] = jnp.zeros(acc_ref.shape, F32)
    sub = t // 2 if t % (2 * LANES) == 0 else t
    r_iota = lax.broadcasted_iota(jnp.int32, (sub, LANES), 0)
    c_iota = lax.broadcasted_iota(jnp.int32, (sub, LANES), 1)

    def block(r0, nr, start, nk, key0):
        rows = slice(r0, r0 + nr)
        masks = []
        for cc in range(nk // LANES):
            off = None if key0 is None else key0 + cc * LANES - r0
            masks.append(None if off is None or off <= -LANES else c_iota + off < r_iota)
        for h in range(heads):
            hs = slice(h * d, (h + 1) * d)
            zz = _dot_nt(q_ref[0, rows, hs], k_ref[0, pl.ds(start, nk), hs])
            zneg = -zz
            sp = jnp.log2(1.0 + jnp.exp2(jnp.minimum(zz, zneg)))
            log_1m = jnp.minimum(zneg, 0.0) - sp
            carry = c_ref[h, rows]
            a_chunks = [None] * (nk // LANES)
            for cc in reversed(range(nk // LANES)):
                cs = slice(cc * LANES, (cc + 1) * LANES)
                l1m = log_1m[:, cs] if masks[cc] is None else jnp.where(masks[cc], log_1m[:, cs], 0.0)
                hi, lo = _split_bf16(l1m)
                y = _dot(jnp.concatenate([hi, lo], axis=1), w_ref[...])
                a = jnp.exp2(zz[:, cs] + y[:, :LANES] + carry)
                carry = carry + y[:, LANES:]
                if masks[cc] is not None:
                    a = jnp.where(masks[cc], a, 0.0)
                a_chunks[cc] = a.astype(BF16)
            c_ref[h, rows] = carry
            acc_ref[h, rows] += _dot(jnp.concatenate(a_chunks, axis=1), v_ref[0, pl.ds(start, nk), hs])

    def diagonal_tile():
        for band in range(t // sub):
            block(band * sub, sub, pl.multiple_of(i * t, t), (band + 1) * sub, 0)

    def left_blocks(n, nb):
        for jj in range(nb):
            block(0, t, pl.multiple_of((i - 1 - n - jj) * t, t), t, None)

    @pl.when(i == 0)
    def _():
        diagonal_tile()

    @pl.when(i > 0)
    def _():
        diagonal_tile()
        left_blocks(0, 1)

    @pl.when(i == 0)
    def _():
        for h in range(heads):
            kmax = jnp.max(jnp.abs(k_ref[0, :, h * d:(h + 1) * d].astype(F32)))
            kmax_ref[h] = jnp.full(kmax_ref.shape[1:], kmax, F32)

    for h in range(heads):
        q_l1 = jnp.sum(jnp.abs(q_ref[0, :, h * d:(h + 1) * d].astype(F32)), axis=1, keepdims=True)
        bound_ref[h] = q_l1 * kmax_ref[h, 0:1, :]

    def more_to_add():
        worst = None
        for h in range(heads):
            top = jnp.max(c_ref[h] + bound_ref[h])
            worst = top if worst is None else jnp.maximum(worst, top)
        return worst >= SB_ZERO_LOG2

    def trip(state):
        n, _ = state
        left_blocks(n, SB_BLOCKS_PER_TRIP)
        return n + SB_BLOCKS_PER_TRIP, more_to_add()

    n, go = lax.while_loop(lambda s: s[1] & (s[0] + SB_BLOCKS_PER_TRIP <= i), trip,
                           (jnp.int32(1), (i > 0) & more_to_add()))
    size = SB_BLOCKS_PER_TRIP // 2
    while size:
        take = go & (n + size <= i)

        @pl.when(take)
        def _(n=n, size=size):
            left_blocks(n, size)

        n = n + jnp.where(take, size, 0)
        size //= 2
    for h in range(heads):
        o_ref[0, :, h * d:(h + 1) * d] = acc_ref[h].astype(o_ref.dtype)


def _sb_attn(q, kv, *, t, heads=4):
    bsz, seq, _ = q.shape
    w = heads * SB_HEAD_DIM
    groups = SB_HEADS // heads
    tri = (jnp.arange(LANES)[:, None] >= jnp.arange(LANES)[None, :]).astype(BF16)
    half = jnp.concatenate([tri, jnp.ones((LANES, LANES), BF16)], axis=1)
    w_sum = jnp.concatenate([half, half], axis=0)
    return pl.pallas_call(
        functools.partial(_sb_attn_kernel, t=t, heads=heads),
        out_shape=jax.ShapeDtypeStruct((bsz, seq, SB_HEADS * SB_HEAD_DIM), BF16),
        grid=(bsz, groups, seq // t),
        in_specs=[pl.BlockSpec((1, t, w), lambda b, g, i: (b, i, g)),
                  pl.BlockSpec((1, seq, w), lambda b, g, i: (b, 0, g)),
                  pl.BlockSpec((1, seq, w), lambda b, g, i: (b, 0, groups + g)),
                  pl.BlockSpec(w_sum.shape, lambda b, g, i: (0, 0))],
        out_specs=pl.BlockSpec((1, t, w), lambda b, g, i: (b, i, g)),
        scratch_shapes=[pltpu.VMEM((heads, t, LANES), F32), pltpu.VMEM((heads, t, SB_HEAD_DIM), F32),
                        pltpu.VMEM((heads, 8, LANES), F32), pltpu.VMEM((heads, t, LANES), F32)],
        compiler_params=_cparams(("parallel", "parallel", "arbitrary")),
        name="sb_attn",
    )(q, kv, kv, w_sum)


def _out_proj_kernel(o_ref, w_ref, x_ref, gt_ref, g_ref, sh_ref, sc_ref, wr_hi_ref, wr_lo_ref, br_ref, tri_ref,
                     xo_ref, h_ref, rt_ref, rtt_ref, cnt_ref, *, n_experts):
    x = x_ref[...] + gt_ref[...] * _dot(o_ref[...], w_ref[...])
    xo_ref[...] = x
    h = _modulate(_rms(x, g_ref[...]), sh_ref[...], sc_ref[...])
    h_hi = h.astype(BF16)
    words = _pack_bf16_pairs(h)
    for part in range(SC_ROW_PARTS):
        h_ref[part] = words[:, part * h_ref.shape[2]:(part + 1) * h_ref.shape[2]]
    h_lo = (h - h_hi.astype(F32)).astype(BF16)
    both = _dot(h_hi, jnp.concatenate([wr_hi_ref[...], wr_lo_ref[...]], axis=1))
    logits = both[:, :LANES] + both[:, LANES:] + _dot(h_lo, wr_hi_ref[...]) + br_ref[...]
    lane = lax.broadcasted_iota(jnp.int32, logits.shape, 1).astype(F32)
    lg = jnp.where(lane < n_experts, logits, -jnp.inf)
    m1 = jnp.max(lg, axis=1, keepdims=True)
    i1 = jnp.min(jnp.where(lg == m1, lane, float(LANES)), axis=1, keepdims=True)
    lg2 = jnp.where(lane == i1, -jnp.inf, lg)
    m2 = jnp.max(lg2, axis=1, keepdims=True)
    i2 = jnp.min(jnp.where(lg2 == m2, lane, float(LANES)), axis=1, keepdims=True)
    e2 = jnp.exp(m2 - m1)
    den = 1.0 + e2
    sel = jnp.where((lane == i1) | (lane == i2), 1.0, 0.0)
    prefix = _dot(tri_ref[...], sel.astype(BF16))
    r1 = jnp.sum(jnp.where(lane == i1, prefix, 0.0), axis=1, keepdims=True)
    r2 = jnp.sum(jnp.where(lane == i2, prefix, 0.0), axis=1, keepdims=True)
    cnt_ref[...] = jnp.sum(sel, axis=0, keepdims=True)
    rt = jnp.zeros_like(logits)
    for k, val in enumerate((i1, i2, 1.0 / den, e2 / den, r1, r2)):
        rt = jnp.where(lane == k, val, rt)
    rt_ref[...] = rt
    rtt_ref[...] = rt.T[:PICK_ROWS]


def _out_proj_router(o, w, x, mod, g, wr_hi, wr_lo, br, n_experts, *, seq, tm):
    n, d = x.shape
    tpb = seq // tm
    dpp = d // 2 // SC_ROW_PARTS
    tri = (jnp.arange(tm)[:, None] > jnp.arange(tm)[None, :]).astype(BF16)
    full = lambda a: pl.BlockSpec(a.shape, lambda i: (0,) * a.ndim)
    row = lambda wd: pl.BlockSpec((tm, wd), lambda i: (i, 0))
    return pl.pallas_call(
        functools.partial(_out_proj_kernel, n_experts=n_experts),
        out_shape=(jax.ShapeDtypeStruct((n, d), F32), jax.ShapeDtypeStruct((SC_ROW_PARTS, n, dpp), U32),
                   jax.ShapeDtypeStruct((n, LANES), F32), jax.ShapeDtypeStruct((PICK_ROWS, n), F32),
                   jax.ShapeDtypeStruct((n // tm, 1, LANES), F32)),
        grid=(n // tm,),
        in_specs=[row(o.shape[1]), full(w), row(d), _mod_spec(2, tpb, d), full(g),
                  _mod_spec(3, tpb, d), _mod_spec(4, tpb, d), full(wr_hi), full(wr_lo), full(br), full(tri)],
        out_specs=(row(d), pl.BlockSpec((SC_ROW_PARTS, tm, dpp), lambda i: (0, i, 0)), row(LANES),
                   pl.BlockSpec((PICK_ROWS, tm), lambda i: (0, i)),
                   pl.BlockSpec((None, 1, LANES), lambda i: (i, 0, 0))),
        compiler_params=_cparams(("parallel",)),
        name="out_proj_router",
    )(o, w, x, mod, g, mod, mod, wr_hi, wr_lo, br, tri)


def _silu(g):
    return g * (1.0 / (1.0 + jnp.exp(-g)))


def _ffn_kernel(o_ref, wo_ref, x_ref, gt1_ref, gf_ref, sh2_ref, sc2_ref, wg_ref, wu_ref, wd_ref, gt2_ref,
                gkv_ref, shk_ref, sck_ref, gm_ref, shm_ref, scm_ref,
                xo_ref, hk_ref, hm_ref, x1_ref, h_ref, acc_ref):
    i = pl.program_id(0)
    f = pl.program_id(1)
    n_tiles = pl.num_programs(0) - 1

    def open_tile(slot):
        x1 = x_ref[...] + gt1_ref[...] * _dot(o_ref[...], wo_ref[...])
        x1_ref[slot] = x1
        h_ref[...] = _modulate(_rms(x1, gf_ref[...]), sh2_ref[...], sc2_ref[...]).astype(BF16)
        acc_ref[slot] = jnp.zeros(acc_ref.shape[1:], F32)

    def close_tile(slot):
        x = x1_ref[slot] + gt2_ref[...] * acc_ref[slot]
        xo_ref[...] = x
        y = x * lax.rsqrt(jnp.mean(x * x, axis=-1, keepdims=True) + EPS)
        hk_ref[...] = _modulate(y * gkv_ref[...], shk_ref[...], sck_ref[...]).astype(BF16)
        hm_ref[...] = _modulate(y * gm_ref[...], shm_ref[...], scm_ref[...]).astype(BF16)

    for parity in range(2):
        mine = (f == 0) & (i % 2 == parity)

        @pl.when(mine & (i == 0))
        def _(parity=parity):
            open_tile(parity)

        @pl.when(mine & (i > 0) & (i < n_tiles))
        def _(parity=parity):
            open_tile(parity)
            close_tile(1 - parity)

        @pl.when(mine & (i == n_tiles))
        def _(parity=parity):
            close_tile(1 - parity)

    @pl.when(i < n_tiles)
    def _():
        h = h_ref[...]
        a = _silu(_dot(h, wg_ref[...])) * _dot(h, wu_ref[...])
        acc_ref[i % 2] += _dot(a.astype(BF16), wd_ref[...])


def _ffn(o, w_o, x, mod0, g_ffn, w_gu, w_down, g_kv, mod_kv, g_mix1, mod1, *, seq, tm, tf):
    n, d = x.shape
    d_ff = w_down.shape[0]
    nf = d_ff // tf
    tpb = seq // tm
    nt = n // tm
    opened = lambda i: jnp.minimum(i, nt - 1)
    closed = lambda i: jnp.maximum(i - 1, 0)
    chunk = lambda i, f: jnp.where(i < nt, f, nf - 1)
    full = lambda a: pl.BlockSpec(a.shape, lambda i, f: (0,) * a.ndim)
    row_in = lambda w=d: pl.BlockSpec((tm, w), lambda i, f: (opened(i), 0))
    row_out = lambda: pl.BlockSpec((tm, d), lambda i, f: (closed(i), 0))
    mod = lambda c, tile: pl.BlockSpec((None, None, 1, d), lambda i, f: (tile(i) // tpb, c, 0, 0))
    return pl.pallas_call(
        _ffn_kernel,
        out_shape=(jax.ShapeDtypeStruct((n, d), F32), jax.ShapeDtypeStruct((n, d), BF16),
                   jax.ShapeDtypeStruct((n, d), BF16)),
        grid=(nt + 1, nf),
        in_specs=[row_in(o.shape[1]), full(w_o), row_in(), mod(2, opened), full(g_ffn),
                  mod(3, opened), mod(4, opened),
                  pl.BlockSpec((d, tf), lambda i, f: (0, chunk(i, f))),
                  pl.BlockSpec((d, tf), lambda i, f: (0, nf + chunk(i, f))),
                  pl.BlockSpec((tf, d), lambda i, f: (chunk(i, f), 0)),
                  mod(5, closed),
                  full(g_kv), mod(0, closed), mod(1, closed),
                  full(g_mix1), mod(0, closed), mod(1, closed)],
        out_specs=(row_out(), row_out(), row_out()),
        scratch_shapes=[pltpu.VMEM((2, tm, d), F32), pltpu.VMEM((tm, d), BF16), pltpu.VMEM((2, tm, d), F32)],
        compiler_params=_cparams(("arbitrary", "arbitrary"), vmem_limit=VMEM_LIMIT_LARGE),
        name="ffn_dense",
    )(o, w_o, x, mod0, g_ffn, mod0, mod0, w_gu, w_gu, w_down, mod0,
      g_kv, mod_kv, mod_kv, g_mix1, mod1, mod1)


def _linear_kernel(x_ref, w_ref, o_ref):
    o_ref[...] = _dot(x_ref[...], w_ref[...]).astype(o_ref.dtype)


def _linear(x, w, *, tm, tn, name):
    n, k = x.shape
    m = w.shape[1]
    return pl.pallas_call(
        _linear_kernel,
        out_shape=jax.ShapeDtypeStruct((n, m), BF16),
        grid=(m // tn, n // tm),
        in_specs=[pl.BlockSpec((tm, k), lambda j, i: (i, 0)), pl.BlockSpec((k, tn), lambda j, i: (0, j))],
        out_specs=pl.BlockSpec((tm, tn), lambda j, i: (i, j)),
        compiler_params=_cparams(("parallel", "parallel")),
        name=name,
    )(x, w)


def _route_plan(counts, n_exp, tm, n_tiles):
    cnt = counts[:, 0, :n_exp].astype(jnp.int32)
    sizes = jnp.sum(cnt, axis=0)
    padded = (sizes + tm - 1) // tm * tm
    ends = jnp.cumsum(padded)
    tile_base = (ends - padded)[None, :] + jnp.cumsum(cnt, axis=0) - cnt
    tile_start = jnp.arange(n_tiles, dtype=jnp.int32) * tm
    tile_expert = jnp.minimum(jnp.sum(tile_start[:, None] >= ends[None, :], axis=1), n_exp - 1)
    n_used = (ends[-1] // tm).reshape(1)
    n_valid = jnp.clip((ends - padded + sizes)[tile_expert] - tile_start, 0, tm)
    return tile_base, tile_expert.astype(jnp.int32), n_used.astype(jnp.int32), n_valid.astype(jnp.int32)


PICK_ROWS = 8


def _picks_kernel(rtt_ref, base_ref, o_ref, *, n_rows):
    rtt = rtt_ref[...]
    sub = lax.broadcasted_iota(jnp.int32, rtt.shape, 0)
    expert = sub.astype(F32)
    out = jnp.zeros_like(rtt)
    for k in range(TOP_K):
        chosen = expert == rtt[k:k + 1, :]
        d_k = (jnp.sum(jnp.where(chosen, base_ref[...], 0.0), axis=0, keepdims=True)
               + rtt[2 * TOP_K + k:2 * TOP_K + k + 1, :])
        for p in range(SC_ROW_PARTS):
            out = jnp.where(sub == p * TOP_K + k, d_k + float(p * n_rows), out)
    o_ref[...] = out.astype(jnp.int32)


def _picks(rtt, tile_base, n_rows, *, tm):
    n = rtt.shape[1]
    n_tok_tiles, n_exp = tile_base.shape
    assert n_exp <= PICK_ROWS
    base = jnp.pad(tile_base.astype(F32), ((0, 0), (0, PICK_ROWS - n_exp))).reshape(n_tok_tiles, PICK_ROWS, 1)
    out = pl.pallas_call(
        functools.partial(_picks_kernel, n_rows=n_rows),
        out_shape=jax.ShapeDtypeStruct((PICK_ROWS, n), jnp.int32),
        grid=(n // tm,),
        in_specs=[pl.BlockSpec((PICK_ROWS, tm), lambda i: (0, i)),
                  pl.BlockSpec((None, PICK_ROWS, 1), lambda i: (i, 0, 0))],
        out_specs=pl.BlockSpec((PICK_ROWS, tm), lambda i: (0, i)),
        compiler_params=_cparams(("parallel",)),
        name="moe_picks",
    )(rtt, base)
    return out[:SC_ROW_PARTS * TOP_K].reshape(SC_ROW_PARTS, TOP_K, n)


def _sc_mesh():
    return plsc.VectorSubcoreMesh(core_axis_name="core", subcore_axis_name="subcore",
                                  num_cores=SC_CORES, num_subcores=SC_SUBCORES)


def _sc_scatter_rows(x, idx_list, n_rows):
    m, d = x.shape

    @functools.partial(pl.kernel, out_type=jax.ShapeDtypeStruct((n_rows, d), x.dtype), mesh=_sc_mesh(),
                       scratch_types=[], name="sc_scatter_rows")
    def scatter(x_hbm, *refs):
        i_hbms, o_hbm = refs[:-1], refs[-1]

        def body(x_vmem, *i_vmems):
            for i_vmem in i_vmems:
                pltpu.sync_copy(x_vmem, o_hbm.at[i_vmem.at[0]])

        pltpu.emit_pipeline(
            body,
            grid=(m // SC_WINDOW,),
            in_specs=[pl.BlockSpec((SC_WINDOW, d), index_map=lambda i: (i, 0))]
                     + [pl.BlockSpec((1, SC_WINDOW), index_map=lambda i: (0, i))] * len(idx_list),
            out_specs=[],
            core_axis_name=("core", "subcore"),
            dimension_semantics=(pltpu.PARALLEL,),
        )(x_hbm, *i_hbms)

    return scatter(x, *[idx.reshape(1, m) for idx in idx_list])


def _grouped_ffn_kernel(te_ref, nu_ref, nv_ref, x_ref, wg_ref, wu_ref, wd_ref, y_ref, hb_ref, acc_ref):
    del te_ref
    i = pl.program_id(0)
    f = pl.program_id(1)
    used = i < nu_ref[0]
    last_f = f == pl.num_programs(1) - 1

    @pl.when(used & (f == 0))
    def _():
        words = jnp.concatenate([x_ref[p] for p in range(SC_ROW_PARTS)], axis=1)
        row = lax.broadcasted_iota(jnp.int32, words.shape, 0)
        words = jnp.where(row < nv_ref[i], words, U32(0))
        hb_ref[...] = _unpack_bf16_pairs(words).astype(BF16)
        acc_ref[...] = jnp.zeros(acc_ref.shape, F32)

    @pl.when(used)
    def _():
        h = hb_ref[...]
        a = _silu(_dot(h, wg_ref[0])) * _dot(h, wu_ref[0])
        acc_ref[...] += _dot(a.astype(BF16), wd_ref[0])

    @pl.when(used & last_f)
    def _():
        y = _pack_bf16_pairs(acc_ref[...])
        for part in range(SC_ROW_PARTS):
            y_ref[part] = y[:, part * y_ref.shape[2]:(part + 1) * y_ref.shape[2]]

    @pl.when(jnp.logical_not(used) & last_f)
    def _():
        y_ref[...] = jnp.zeros(y_ref.shape, y_ref.dtype)


def _grouped_ffn(tile_expert, n_used, n_valid, xg, w_gu, w_down, *, tm, tf):
    parts, n_rows, dpp = xg.shape
    n_exp, d_ff, d = w_down.shape
    nf = d_ff // tf

    def wspec(shape, index):
        def index_map(i, f, te, nu, nv):
            return index(te[i], jnp.where(i < nu[0], f, nf - 1))
        return pl.BlockSpec(shape, index_map)

    rows = pl.BlockSpec((parts, tm, dpp), lambda i, f, te, nu, nv: (0, i, 0))
    return pl.pallas_call(
        _grouped_ffn_kernel,
        out_shape=jax.ShapeDtypeStruct(xg.shape, xg.dtype),
        grid_spec=pltpu.PrefetchScalarGridSpec(
            num_scalar_prefetch=3,
            grid=(n_rows // tm, nf),
            in_specs=[rows,
                      wspec((1, d, tf), lambda e, f: (e, 0, f)),
                      wspec((1, d, tf), lambda e, f: (e, 0, nf + f)),
                      wspec((1, tf, d), lambda e, f: (e, f, 0))],
            out_specs=rows,
            scratch_shapes=[pltpu.VMEM((tm, d), BF16), pltpu.VMEM((tm, d), F32)]),
        compiler_params=_cparams(("arbitrary", "arbitrary")),
        name="moe_grouped_ffn",
    )(tile_expert, n_used, n_valid, xg, w_gu, w_gu, w_down)


def _sc_gather_rows(table, idx):
    n_idx = idx.shape[0]
    d = table.shape[1]

    @functools.partial(pl.kernel, out_type=jax.ShapeDtypeStruct((n_idx, d), table.dtype), mesh=_sc_mesh(),
                       name="sc_gather_rows")
    def gather(t_hbm, i_hbm, o_hbm):
        def body(i_vmem, o_vmem):
            pltpu.sync_copy(t_hbm.at[i_vmem.at[0]], o_vmem)

        pltpu.emit_pipeline(
            body,
            grid=(n_idx // SC_WINDOW,),
            in_specs=[pl.BlockSpec((1, SC_WINDOW), index_map=lambda i: (0, i))],
            out_specs=[pl.BlockSpec((SC_WINDOW, d), index_map=lambda i: (i, 0))],
            core_axis_name=("core", "subcore"),
            dimension_semantics=(pltpu.PARALLEL,),
        )(i_hbm, o_hbm)

    return gather(table, idx.reshape(1, n_idx))


def _combine_kernel(x_ref, rt_ref, gt_ref, gf_ref, *refs):
    y_refs, o_ref = refs[:-1], refs[-1]
    rt = rt_ref[...]
    tot = None
    for k in range(TOP_K):
        words = jnp.concatenate([y_refs[p * TOP_K + k][...] for p in range(SC_ROW_PARTS)], axis=1)
        term = rt[:, TOP_K + k:TOP_K + k + 1] * _unpack_bf16_pairs(words)
        tot = term if tot is None else tot + term
    x = x_ref[...] + gt_ref[...] * tot
    o_ref[...] = _rms(x, gf_ref[...])


def _combine(x, rt, mod1, g_final, ysel, *, seq, tm):
    n, d = x.shape
    tpb = seq // tm
    nt = n // tm
    row = lambda w: pl.BlockSpec((tm, w), lambda i: (i, 0))
    piece = lambda j: pl.BlockSpec((tm, ysel.shape[1]), lambda i: (j * nt + i, 0))
    n_pieces = SC_ROW_PARTS * TOP_K
    return pl.pallas_call(
        _combine_kernel,
        out_shape=jax.ShapeDtypeStruct((n, d), F32),
        grid=(nt,),
        in_specs=[row(d), row(LANES), _mod_spec(5, tpb, d), pl.BlockSpec(g_final.shape, lambda i: (0, 0))]
                 + [piece(j) for j in range(n_pieces)],
        out_specs=row(d),
        compiler_params=_cparams(("parallel",)),
        name="moe_combine",
    )(x, rt, mod1, g_final, *([ysel] * n_pieces))


def _pad_last(a, width):
    return jnp.pad(a, [(0, 0)] * (a.ndim - 1) + [(0, width - a.shape[-1])])


def kernel(x, c, positions, w_mod, b_mod, g_mix, g_ffn, w_a_down, g_q_lat, g_kv_lat, w_uq, w_ukv, w_oa,
           w_mod_kv, b_mod_kv, g_kv, w_kv_sb, w_q_sb, w_o_sb, w_ffn_gu, w_ffn_down, w_router, b_router,
           w_exp_gu, w_exp_down, g_final):
    bsz, seq, d = x.shape
    n = bsz * seq
    q_lora, kv_lora = g_q_lat.shape[1], g_kv_lat.shape[1]
    n_exp = w_router.shape[-1]
    d_ff = w_ffn_down.shape[1]
    tm = min(TOKEN_TILE, seq)
    t_mla = min(MLA_TILE, seq)
    t_sb = min(SB_TILE, seq)
    tf = next((t for t in (FFN_CHUNK, 512) if d_ff % t == 0), d_ff)

    mod0 = _modvec(c, w_mod, 0, b_mod[0]).reshape(bsz, 6, 1, d)
    mod1 = _modvec(c, w_mod, 1, b_mod[1]).reshape(bsz, 6, 1, d)
    mod_kv = _modvec(c, w_mod_kv[None], 0, b_mod_kv).reshape(bsz, 2, 1, d)

    lat_w = q_lora + kv_lora + LANES
    wd = _pad_last(w_a_down[0], lat_w).astype(BF16)
    wq = w_uq[0].reshape(q_lora, MLA_HEADS, QK_NOPE + QK_ROPE)
    wqn = wq[:, :, :QK_NOPE].reshape(q_lora, MLA_HEADS * QK_NOPE).astype(BF16)
    wqr = _pad_last(wq[:, :, QK_NOPE:], LANES).reshape(q_lora, MLA_HEADS * LANES).astype(BF16)
    wkv = w_ukv[0].reshape(kv_lora, MLA_HEADS, QK_NOPE + V_DIM)
    wkn = wkv[:, :, :QK_NOPE].reshape(kv_lora, MLA_HEADS * QK_NOPE).astype(BF16)
    wv = wkv[:, :, QK_NOPE:].reshape(kv_lora, MLA_HEADS * V_DIM).astype(BF16)
    half = QK_ROPE // 2
    inv = ROPE_THETA ** (-jnp.arange(half, dtype=F32) / half)
    inv = jnp.tile(inv, ROPE_PACK).reshape(1, LANES)
    wr = _pad_last(w_router[0], LANES)
    wr_hi = wr.astype(BF16)
    wr_lo = (wr - wr_hi.astype(F32)).astype(BF16)
    br = _pad_last(b_router[0], LANES).reshape(1, LANES)

    xf = x.reshape(n, d)
    pos = positions.reshape(n // tm, ROPE_PACK, tm // ROPE_PACK).swapaxes(1, 2)
    pos = jnp.repeat(pos.reshape(n // ROPE_PACK, ROPE_PACK), QK_ROPE // 2, axis=1)
    row1 = lambda a: a.reshape(1, -1)

    q, k, v = _mla_proj(xf, pos, mod0, row1(g_mix[0]), wd, row1(g_q_lat[0]), row1(g_kv_lat[0]),
                        wqn, wqr, wkn, wv, inv, seq=seq, tm=tm)
    o = _mla_attn(q.reshape(bsz, seq, -1), k.reshape(bsz, seq, -1), v.reshape(bsz, seq, -1), t=t_mla)
    x2, hk, hm = _ffn(o.reshape(n, -1), w_oa[0].astype(BF16), xf, mod0, row1(g_ffn[0]),
                      w_ffn_gu[0].astype(BF16), w_ffn_down[0].astype(BF16),
                      row1(g_kv), mod_kv, row1(g_mix[1]), mod1, seq=seq, tm=tm, tf=tf)
    t_lin = min(LINEAR_TILE, seq)
    kv = _linear(hk, w_kv_sb.astype(BF16), tm=t_lin, tn=w_kv_sb.shape[1], name="kv_proj")
    q_scale = math.log2(math.e) / math.sqrt(SB_HEAD_DIM)
    qs = _linear(hm, (w_q_sb[0] * q_scale).astype(BF16), tm=t_lin, tn=1024, name="q_proj")
    o = _sb_attn(qs.reshape(bsz, seq, -1), kv.reshape(bsz, seq, -1), t=t_sb)
    x3, h, rt, rtt, counts = _out_proj_router(o.reshape(n, -1), w_o_sb[0].astype(BF16), x2, mod1,
                                              row1(g_ffn[1]), wr_hi, wr_lo, br, n_exp, seq=seq, tm=tm)
    n_tiles = TOP_K * n // tm + n_exp
    n_rows = n_tiles * tm
    tile_base, tile_expert, n_used, n_valid = _route_plan(counts, n_exp, tm, n_tiles)
    picks = _picks(rtt, tile_base, n_rows, tm=tm)
    xg = _sc_scatter_rows(h.reshape(SC_ROW_PARTS * n, -1), [picks[:, k].reshape(-1) for k in range(TOP_K)],
                          SC_ROW_PARTS * n_rows)
    yg = _grouped_ffn(tile_expert, n_used, n_valid, xg.reshape(SC_ROW_PARTS, n_rows, -1),
                      w_exp_gu[0].astype(BF16), w_exp_down[0].astype(BF16), tm=tm, tf=tf)
    ysel = _sc_gather_rows(yg.reshape(SC_ROW_PARTS * n_rows, -1), picks.reshape(-1))
    out = _combine(x3, rt, mod1, row1(g_final), ysel, seq=seq, tm=tm)
    return out.reshape(bsz, seq, d)
```

```python
import functools
import math

import jax
import jax.numpy as jnp
from jax import lax
from jax.experimental import pallas as pl
from jax.experimental.pallas import tpu as pltpu
from jax.experimental.pallas import tpu_sc as plsc

F32 = jnp.float32
BF16 = jnp.bfloat16

EPS = 1e-6
MLA_HEADS = 8
QK_NOPE = 128
QK_ROPE = 64
V_DIM = 128
ROPE_THETA = 10000.0
SB_HEADS = 8
SB_HEAD_DIM = 128
TOP_K = 2

LANES = 128
QK_PAD = 256
ROPE_PACK = LANES // (QK_ROPE // 2)
VMEM_LIMIT = 48 * 1024 * 1024
VMEM_LIMIT_LARGE = 58 * 1024 * 1024
TOKEN_TILE = 512
LINEAR_TILE = 1024
MLA_TILE = 512
MLA_BLOCKS_PER_TRIP = 4
SB_TILE = 512
SB_BLOCKS_PER_TRIP = 2
SB_ZERO_LOG2 = -140.0
FFN_CHUNK = 1792
SC_CORES = 2
SC_SUBCORES = 16
SC_WINDOW = 128
SC_ROW_PARTS = 2


def _cparams(sem, vmem_limit=VMEM_LIMIT):
    return pltpu.CompilerParams(dimension_semantics=sem, vmem_limit_bytes=vmem_limit)


def _rms(x, g):
    return x * lax.rsqrt(jnp.mean(x * x, axis=-1, keepdims=True) + EPS) * g


def _modulate(h, shift, scale):
    return h * (1.0 + scale) + shift


def _split_bf16(a):
    hi = a.astype(BF16)
    lo = (a - hi.astype(F32)).astype(BF16)
    return hi, lo


U32 = jnp.uint32
HIGH_HALF = 0xFFFF0000


def _pack_bf16_pairs(a):
    half = a.shape[1] // 2
    bits = lambda v: lax.bitcast_convert_type(v.astype(BF16).astype(F32), U32)
    return (bits(a[:, :half]) >> 16) | (bits(a[:, half:]) & U32(HIGH_HALF))


def _unpack_bf16_pairs(w):
    lo = lax.bitcast_convert_type(w << 16, F32)
    hi = lax.bitcast_convert_type(w & U32(HIGH_HALF), F32)
    return jnp.concatenate([lo, hi], axis=1)


def _dot(a, b):
    return jnp.dot(a, b, preferred_element_type=F32)


def _dot_nt(a, b):
    return lax.dot_general(a, b, (((1,), (1,)), ((), ())), preferred_element_type=F32)


def _modvec_kernel(c_ref, w_ref, b_ref, o_ref):
    c = c_ref[...]
    sc = c * (1.0 / (1.0 + jnp.exp(-c)))
    a_hi, a_lo = _split_bf16(sc)
    w_hi, w_lo = _split_bf16(w_ref[...])
    o_ref[...] = _dot(a_hi, w_hi) + _dot(a_lo, w_hi) + _dot(a_hi, w_lo) + b_ref[...]


def _modvec(c, w, layer, b, tn=512):
    bsz, d = c.shape
    n = w.shape[2]
    return pl.pallas_call(
        _modvec_kernel,
        out_shape=jax.ShapeDtypeStruct((bsz, n), F32),
        grid=(n // tn,),
        in_specs=[pl.BlockSpec((bsz, d), lambda j: (0, 0)),
                  pl.BlockSpec((None, d, tn), lambda j: (layer, 0, j)),
                  pl.BlockSpec((1, tn), lambda j: (0, j))],
        out_specs=pl.BlockSpec((bsz, tn), lambda j: (0, j)),
        compiler_params=_cparams(("arbitrary",)),
        name="modvec",
    )(c, w, b.reshape(1, n))


def _mod_spec(chunk, tiles_per_batch, d):
    return pl.BlockSpec((None, None, 1, d), lambda i, *_: (i // tiles_per_batch, chunk, 0, 0))


def _mla_proj_kernel(x_ref, pos_ref, sh_ref, sc_ref, g_ref, wd_ref, gq_ref, gkv_ref,
                     wqn_ref, wqr_ref, wkn_ref, wv_ref, inv_ref,
                     q_ref, k_ref, v_ref, *, q_lora, kv_lora):
    x = x_ref[...]
    h = _modulate(_rms(x, g_ref[...]), sh_ref[...], sc_ref[...]).astype(BF16)
    lat = _dot(h, wd_ref[...])
    c_q = _rms(lat[:, :q_lora], gq_ref[...]).astype(BF16)
    c_kv = _rms(lat[:, q_lora:q_lora + kv_lora], gkv_ref[...]).astype(BF16)
    k_rot = lat[:, q_lora + kv_lora:]

    half = QK_ROPE // 2
    ang = pos_ref[...].astype(F32) * inv_ref[...]
    cos_p = jnp.cos(ang)
    sin_p = jnp.sin(ang)
    lane = lax.broadcasted_iota(jnp.int32, ang.shape, 1)
    cos, s_a, s_b = [], [], []
    for g in range(ROPE_PACK):
        shift = (LANES - g * half) % LANES
        cg = pltpu.roll(cos_p, shift, axis=1) if shift else cos_p
        sg = pltpu.roll(sin_p, shift, axis=1) if shift else sin_p
        cos.append(jnp.where(lane < half, cg, pltpu.roll(cg, half, axis=1)))
        s_a.append(jnp.where(lane < half, -sg, 0.0))
        s_b.append(jnp.where((lane >= half) & (lane < 2 * half), pltpu.roll(sg, half, axis=1), 0.0))
    cos, s_a, s_b = (jnp.concatenate(v, axis=0) for v in (cos, s_a, s_b))

    def rope(r):
        return (r * cos + pltpu.roll(r, LANES - half, axis=1) * s_a
                + pltpu.roll(r, half, axis=1) * s_b)

    k_rot = rope(k_rot).astype(BF16)
    q_nope = _dot(c_q, wqn_ref[...])
    q_rope = _dot(c_q, wqr_ref[...])
    k_nope = _dot(c_kv, wkn_ref[...])
    v_ref[...] = _dot(c_kv, wv_ref[...]).astype(BF16)
    for hd in range(MLA_HEADS):
        a, b = hd * LANES, (hd + 1) * LANES
        q_ref[:, hd * QK_PAD:hd * QK_PAD + LANES] = q_nope[:, a:b].astype(BF16)
        q_ref[:, hd * QK_PAD + LANES:(hd + 1) * QK_PAD] = rope(q_rope[:, a:b]).astype(BF16)
        k_ref[:, hd * QK_PAD:hd * QK_PAD + LANES] = k_nope[:, a:b].astype(BF16)
        k_ref[:, hd * QK_PAD + LANES:(hd + 1) * QK_PAD] = k_rot


def _mla_proj(x, pos, mod, g_mix, wd, gq, gkv, wqn, wqr, wkn, wv, inv, *, seq, tm):
    n, d = x.shape
    tpb = seq // tm
    q_lora, kv_lora = gq.shape[1], gkv.shape[1]
    hq = MLA_HEADS * QK_PAD
    hv = MLA_HEADS * V_DIM
    full = lambda a: pl.BlockSpec(a.shape, lambda i: (0,) * a.ndim)
    row = lambda w: pl.BlockSpec((tm, w), lambda i: (i, 0))
    return pl.pallas_call(
        functools.partial(_mla_proj_kernel, q_lora=q_lora, kv_lora=kv_lora),
        out_shape=(jax.ShapeDtypeStruct((n, hq), BF16), jax.ShapeDtypeStruct((n, hq), BF16),
                   jax.ShapeDtypeStruct((n, hv), BF16)),
        grid=(n // tm,),
        in_specs=[row(d), pl.BlockSpec((tm // ROPE_PACK, LANES), lambda i: (i, 0)), _mod_spec(0, tpb, d), _mod_spec(1, tpb, d), full(g_mix), full(wd),
                  full(gq), full(gkv), full(wqn), full(wqr), full(wkn), full(wv), full(inv)],
        out_specs=(row(hq), row(hq), row(hv)),
        compiler_params=_cparams(("parallel",)),
        name="mla_proj",
    )(x, pos, mod, mod, g_mix, wd, gq, gkv, wqn, wqr, wkn, wv, inv)


def _lane_chunks(a):
    return [a[:, c * LANES:(c + 1) * LANES] for c in range(a.shape[1] // LANES)]


def _for_blocks(n, step, group):
    def trip(p, carry):
        step(group * p, group)
        return carry

    lax.fori_loop(0, n // group, trip, 0)
    size = group // 2
    while size:
        @pl.when((n // size) % 2 == 1)
        def _(size=size):
            step(n // (2 * size) * (2 * size), size)
        size //= 2


def _mla_attn_kernel(q_ref, k_ref, v_ref, o_ref, s_ref, m_ref, acc_ref, *, t, scale, heads):
    i = pl.program_id(2)

    def lane_max(s):
        m = None
        for sc in _lane_chunks(s):
            m = sc if m is None else jnp.maximum(m, sc)
        return m

    def scores(h, j):
        start = pl.multiple_of(j * t, t)
        hs = slice(h * QK_PAD, (h + 1) * QK_PAD)
        return _dot_nt(q_ref[0, :, hs], k_ref[0, pl.ds(start, t), hs])

    r_minus_c = (lax.broadcasted_iota(jnp.int32, (t, t), 0) - lax.broadcasted_iota(jnp.int32, (t, t), 1))
    m_ref[...] = jnp.full(m_ref.shape, -jnp.inf, F32)

    def pass1(j, nb):
        for h in range(heads):
            m = m_ref[h]
            for jj in range(nb):
                s = jnp.where(r_minus_c >= (j + jj - i) * t, scores(h, j + jj), -jnp.inf)
                s_ref[h, j + jj] = s
                m = jnp.maximum(m, lane_max(s))
            m_ref[h] = m

    _for_blocks(i + 1, pass1, MLA_BLOCKS_PER_TRIP)
    for h in range(heads):
        m_ref[h] = jnp.broadcast_to(jnp.max(m_ref[h], axis=1, keepdims=True), (t, LANES))
    acc_ref[...] = jnp.zeros(acc_ref.shape, F32)
    cst = scale * math.log2(math.e)

    def pass2(j, nb):
        start = pl.multiple_of(j * t, t)
        ones = jnp.ones((nb * t, LANES), BF16)
        for h in range(heads):
            m = m_ref[h]
            p = jnp.concatenate([jnp.exp2((sc - m) * cst).astype(BF16)
                                 for jj in range(nb) for sc in _lane_chunks(s_ref[h, j + jj])], axis=1)
            v_ext = jnp.concatenate([v_ref[0, pl.ds(start, nb * t), h * V_DIM:(h + 1) * V_DIM], ones], axis=1)
            acc_ref[h] += _dot(p, v_ext)

    _for_blocks(i + 1, pass2, MLA_BLOCKS_PER_TRIP)
    for h in range(heads):
        acc = acc_ref[h]
        o_ref[0, :, h * V_DIM:(h + 1) * V_DIM] = (acc[:, :V_DIM] / acc[:, V_DIM:]).astype(o_ref.dtype)


def _mla_attn(q, k, v, *, t, heads=2):
    bsz, seq, _ = q.shape
    scale = 1.0 / math.sqrt(QK_NOPE + QK_ROPE)
    wq, wv = heads * QK_PAD, heads * V_DIM
    return pl.pallas_call(
        functools.partial(_mla_attn_kernel, t=t, scale=scale, heads=heads),
        out_shape=jax.ShapeDtypeStruct((bsz, seq, MLA_HEADS * V_DIM), BF16),
        grid=(bsz, MLA_HEADS // heads, seq // t),
        in_specs=[pl.BlockSpec((1, t, wq), lambda b, g, i: (b, i, g)),
                  pl.BlockSpec((1, seq, wq), lambda b, g, i: (b, 0, g)),
                  pl.BlockSpec((1, seq, wv), lambda b, g, i: (b, 0, g))],
        out_specs=pl.BlockSpec((1, t, wv), lambda b, g, i: (b, i, g)),
        scratch_shapes=[pltpu.VMEM((heads, seq // t, t, t), F32), pltpu.VMEM((heads, t, LANES), F32),
                        pltpu.VMEM((heads, t, V_DIM + LANES), F32)],
        compiler_params=_cparams(("parallel", "parallel", "arbitrary")),
        name="mla_attn",
    )(q, k, v)


def _sb_attn_kernel(q_ref, k_ref, v_ref, w_ref, o_ref, c_ref, acc_ref, kmax_ref, bound_ref, *, t, heads):
    i = pl.program_id(2)
    d = SB_HEAD_DIM
    c_ref[...] = jnp.zeros(c_ref.shape, F32)
    acc_ref[...] = jnp.zeros(acc_ref.shape, F32)
    sub = t // 2 if t % (2 * LANES) == 0 else t
    r_iota = lax.broadcasted_iota(jnp.int32, (sub, LANES), 0)
    c_iota = lax.broadcasted_iota(jnp.int32, (sub, LANES), 1)

    def block(r0, nr, start, nk, key0):
        rows = slice(r0, r0 + nr)
        masks = []
        for cc in range(nk // LANES):
            off = None if key0 is None else key0 + cc * LANES - r0
            masks.append(None if off is None or off <= -LANES else c_iota + off < r_iota)
        for h in range(heads):
            hs = slice(h * d, (h + 1) * d)
            zz = _dot_nt(q_ref[0, rows, hs], k_ref[0, pl.ds(start, nk), hs])
            zneg = -zz
            sp = jnp.log2(1.0 + jnp.exp2(jnp.minimum(zz, zneg)))
            log_1m = jnp.minimum(zneg, 0.0) - sp
            carry = c_ref[h, rows]
            a_chunks = [None] * (nk // LANES)
            for cc in reversed(range(nk // LANES)):
                cs = slice(cc * LANES, (cc + 1) * LANES)
                l1m = log_1m[:, cs] if masks[cc] is None else jnp.where(masks[cc], log_1m[:, cs], 0.0)
                hi, lo = _split_bf16(l1m)
                y = _dot(jnp.concatenate([hi, lo], axis=1), w_ref[...])
                a = jnp.exp2(zz[:, cs] + y[:, :LANES] + carry)
                carry = carry + y[:, LANES:]
                if masks[cc] is not None:
                    a = jnp.where(masks[cc], a, 0.0)
                a_chunks[cc] = a.astype(BF16)
            c_ref[h, rows] = carry
            acc_ref[h, rows] += _dot(jnp.concatenate(a_chunks, axis=1), v_ref[0, pl.ds(start, nk), hs])

    def diagonal_tile():
        for band in range(t // sub):
            block(band * sub, sub, pl.multiple_of(i * t, t), (band + 1) * sub, 0)

    def left_blocks(n, nb):
        for jj in range(nb):
            block(0, t, pl.multiple_of((i - 1 - n - jj) * t, t), t, None)

    @pl.when(i == 0)
    def _():
        diagonal_tile()

    @pl.when(i > 0)
    def _():
        diagonal_tile()
        left_blocks(0, 1)

    @pl.when(i == 0)
    def _():
        for h in range(heads):
            kmax = jnp.max(jnp.abs(k_ref[0, :, h * d:(h + 1) * d].astype(F32)))
            kmax_ref[h] = jnp.full(kmax_ref.shape[1:], kmax, F32)

    for h in range(heads):
        q_l1 = jnp.sum(jnp.abs(q_ref[0, :, h * d:(h + 1) * d].astype(F32)), axis=1, keepdims=True)
        bound_ref[h] = q_l1 * kmax_ref[h, 0:1, :]

    def more_to_add():
        worst = None
        for h in range(heads):
            top = jnp.max(c_ref[h] + bound_ref[h])
            worst = top if worst is None else jnp.maximum(worst, top)
        return worst >= SB_ZERO_LOG2

    def trip(state):
        n, _ = state
        left_blocks(n, SB_BLOCKS_PER_TRIP)
        return n + SB_BLOCKS_PER_TRIP, more_to_add()

    n, go = lax.while_loop(lambda s: s[1] & (s[0] + SB_BLOCKS_PER_TRIP <= i), trip,
                           (jnp.int32(1), (i > 0) & more_to_add()))
    size = SB_BLOCKS_PER_TRIP // 2
    while size:
        take = go & (n + size <= i)

        @pl.when(take)
        def _(n=n, size=size):
            left_blocks(n, size)

        n = n + jnp.where(take, size, 0)
        size //= 2
    for h in range(heads):
        o_ref[0, :, h * d:(h + 1) * d] = acc_ref[h].astype(o_ref.dtype)


def _sb_attn(q, kv, *, t, heads=4):
    bsz, seq, _ = q.shape
    w = heads * SB_HEAD_DIM
    groups = SB_HEADS // heads
    tri = (jnp.arange(LANES)[:, None] >= jnp.arange(LANES)[None, :]).astype(BF16)
    half = jnp.concatenate([tri, jnp.ones((LANES, LANES), BF16)], axis=1)
    w_sum = jnp.concatenate([half, half], axis=0)
    return pl.pallas_call(
        functools.partial(_sb_attn_kernel, t=t, heads=heads),
        out_shape=jax.ShapeDtypeStruct((bsz, seq, SB_HEADS * SB_HEAD_DIM), BF16),
        grid=(bsz, groups, seq // t),
        in_specs=[pl.BlockSpec((1, t, w), lambda b, g, i: (b, i, g)),
                  pl.BlockSpec((1, seq, w), lambda b, g, i: (b, 0, g)),
                  pl.BlockSpec((1, seq, w), lambda b, g, i: (b, 0, groups + g)),
                  pl.BlockSpec(w_sum.shape, lambda b, g, i: (0, 0))],
        out_specs=pl.BlockSpec((1, t, w), lambda b, g, i: (b, i, g)),
        scratch_shapes=[pltpu.VMEM((heads, t, LANES), F32), pltpu.VMEM((heads, t, SB_HEAD_DIM), F32),
                        pltpu.VMEM((heads, 8, LANES), F32), pltpu.VMEM((heads, t, LANES), F32)],
        compiler_params=_cparams(("parallel", "parallel", "arbitrary")),
        name="sb_attn",
    )(q, kv, kv, w_sum)


def _out_proj_kernel(o_ref, w_ref, x_ref, gt_ref, g_ref, sh_ref, sc_ref, wr_hi_ref, wr_lo_ref, br_ref, tri_ref,
                     xo_ref, h_ref, rt_ref, rtt_ref, cnt_ref, *, n_experts):
    x = x_ref[...] + gt_ref[...] * _dot(o_ref[...], w_ref[...])
    xo_ref[...] = x
    h = _modulate(_rms(x, g_ref[...]), sh_ref[...], sc_ref[...])
    h_hi = h.astype(BF16)
    words = _pack_bf16_pairs(h)
    for part in range(SC_ROW_PARTS):
        h_ref[part] = words[:, part * h_ref.shape[2]:(part + 1) * h_ref.shape[2]]
    h_lo = (h - h_hi.astype(F32)).astype(BF16)
    both = _dot(h_hi, jnp.concatenate([wr_hi_ref[...], wr_lo_ref[...]], axis=1))
    logits = both[:, :LANES] + both[:, LANES:] + _dot(h_lo, wr_hi_ref[...]) + br_ref[...]
    lane = lax.broadcasted_iota(jnp.int32, logits.shape, 1).astype(F32)
    lg = jnp.where(lane < n_experts, logits, -jnp.inf)
    m1 = jnp.max(lg, axis=1, keepdims=True)
    i1 = jnp.min(jnp.where(lg == m1, lane, float(LANES)), axis=1, keepdims=True)
    lg2 = jnp.where(lane == i1, -jnp.inf, lg)
    m2 = jnp.max(lg2, axis=1, keepdims=True)
    i2 = jnp.min(jnp.where(lg2 == m2, lane, float(LANES)), axis=1, keepdims=True)
    e2 = jnp.exp(m2 - m1)
    den = 1.0 + e2
    sel = jnp.where((lane == i1) | (lane == i2), 1.0, 0.0)
    prefix = _dot(tri_ref[...], sel.astype(BF16))
    r1 = jnp.sum(jnp.where(lane == i1, prefix, 0.0), axis=1, keepdims=True)
    r2 = jnp.sum(jnp.where(lane == i2, prefix, 0.0), axis=1, keepdims=True)
    cnt_ref[...] = jnp.sum(sel, axis=0, keepdims=True)
    rt = jnp.zeros_like(logits)
    for k, val in enumerate((i1, i2, 1.0 / den, e2 / den, r1, r2)):
        rt = jnp.where(lane == k, val, rt)
    rt_ref[...] = rt
    rtt_ref[...] = rt.T[:PICK_ROWS]


def _out_proj_router(o, w, x, mod, g, wr_hi, wr_lo, br, n_experts, *, seq, tm):
    n, d = x.shape
    tpb = seq // tm
    dpp = d // 2 // SC_ROW_PARTS
    tri = (jnp.arange(tm)[:, None] > jnp.arange(tm)[None, :]).astype(BF16)
    full = lambda a: pl.BlockSpec(a.shape, lambda i: (0,) * a.ndim)
    row = lambda wd: pl.BlockSpec((tm, wd), lambda i: (i, 0))
    return pl.pallas_call(
        functools.partial(_out_proj_kernel, n_experts=n_experts),
        out_shape=(jax.ShapeDtypeStruct((n, d), F32), jax.ShapeDtypeStruct((SC_ROW_PARTS, n, dpp), U32),
                   jax.ShapeDtypeStruct((n, LANES), F32), jax.ShapeDtypeStruct((PICK_ROWS, n), F32),
                   jax.ShapeDtypeStruct((n // tm, 1, LANES), F32)),
        grid=(n // tm,),
        in_specs=[row(o.shape[1]), full(w), row(d), _mod_spec(2, tpb, d), full(g),
                  _mod_spec(3, tpb, d), _mod_spec(4, tpb, d), full(wr_hi), full(wr_lo), full(br), full(tri)],
        out_specs=(row(d), pl.BlockSpec((SC_ROW_PARTS, tm, dpp), lambda i: (0, i, 0)), row(LANES),
                   pl.BlockSpec((PICK_ROWS, tm), lambda i: (0, i)),
                   pl.BlockSpec((None, 1, LANES), lambda i: (i, 0, 0))),
        compiler_params=_cparams(("parallel",)),
        name="out_proj_router",
    )(o, w, x, mod, g, mod, mod, wr_hi, wr_lo, br, tri)


def _silu(g):
    return g * (1.0 / (1.0 + jnp.exp(-g)))


def _ffn_kernel(o_ref, wo_ref, x_ref, gt1_ref, gf_ref, sh2_ref, sc2_ref, wg_ref, wu_ref, wd_ref, gt2_ref,
                gkv_ref, shk_ref, sck_ref, gm_ref, shm_ref, scm_ref,
                xo_ref, hk_ref, hm_ref, x1_ref, h_ref, acc_ref):
    i = pl.program_id(0)
    f = pl.program_id(1)
    n_tiles = pl.num_programs(0) - 1

    def open_tile(slot):
        x1 = x_ref[...] + gt1_ref[...] * _dot(o_ref[...], wo_ref[...])
        x1_ref[slot] = x1
        h_ref[...] = _modulate(_rms(x1, gf_ref[...]), sh2_ref[...], sc2_ref[...]).astype(BF16)
        acc_ref[slot] = jnp.zeros(acc_ref.shape[1:], F32)

    def close_tile(slot):
        x = x1_ref[slot] + gt2_ref[...] * acc_ref[slot]
        xo_ref[...] = x
        y = x * lax.rsqrt(jnp.mean(x * x, axis=-1, keepdims=True) + EPS)
        hk_ref[...] = _modulate(y * gkv_ref[...], shk_ref[...], sck_ref[...]).astype(BF16)
        hm_ref[...] = _modulate(y * gm_ref[...], shm_ref[...], scm_ref[...]).astype(BF16)

    for parity in range(2):
        mine = (f == 0) & (i % 2 == parity)

        @pl.when(mine & (i == 0))
        def _(parity=parity):
            open_tile(parity)

        @pl.when(mine & (i > 0) & (i < n_tiles))
        def _(parity=parity):
            open_tile(parity)
            close_tile(1 - parity)

        @pl.when(mine & (i == n_tiles))
        def _(parity=parity):
            close_tile(1 - parity)

    @pl.when(i < n_tiles)
    def _():
        h = h_ref[...]
        a = _silu(_dot(h, wg_ref[...])) * _dot(h, wu_ref[...])
        acc_ref[i % 2] += _dot(a.astype(BF16), wd_ref[...])


def _ffn(o, w_o, x, mod0, g_ffn, w_gu, w_down, g_kv, mod_kv, g_mix1, mod1, *, seq, tm, tf):
    n, d = x.shape
    d_ff = w_down.shape[0]
    nf = d_ff // tf
    tpb = seq // tm
    nt = n // tm
    opened = lambda i: jnp.minimum(i, nt - 1)
    closed = lambda i: jnp.maximum(i - 1, 0)
    chunk = lambda i, f: jnp.where(i < nt, f, nf - 1)
    full = lambda a: pl.BlockSpec(a.shape, lambda i, f: (0,) * a.ndim)
    row_in = lambda w=d: pl.BlockSpec((tm, w), lambda i, f: (opened(i), 0))
    row_out = lambda: pl.BlockSpec((tm, d), lambda i, f: (closed(i), 0))
    mod = lambda c, tile: pl.BlockSpec((None, None, 1, d), lambda i, f: (tile(i) // tpb, c, 0, 0))
    return pl.pallas_call(
        _ffn_kernel,
        out_shape=(jax.ShapeDtypeStruct((n, d), F32), jax.ShapeDtypeStruct((n, d), BF16),
                   jax.ShapeDtypeStruct((n, d), BF16)),
        grid=(nt + 1, nf),
        in_specs=[row_in(o.shape[1]), full(w_o), row_in(), mod(2, opened), full(g_ffn),
                  mod(3, opened), mod(4, opened),
                  pl.BlockSpec((d, tf), lambda i, f: (0, chunk(i, f))),
                  pl.BlockSpec((d, tf), lambda i, f: (0, nf + chunk(i, f))),
                  pl.BlockSpec((tf, d), lambda i, f: (chunk(i, f), 0)),
                  mod(5, closed),
                  full(g_kv), mod(0, closed), mod(1, closed),
                  full(g_mix1), mod(0, closed), mod(1, closed)],
        out_specs=(row_out(), row_out(), row_out()),
        scratch_shapes=[pltpu.VMEM((2, tm, d), F32), pltpu.VMEM((tm, d), BF16), pltpu.VMEM((2, tm, d), F32)],
        compiler_params=_cparams(("arbitrary", "arbitrary"), vmem_limit=VMEM_LIMIT_LARGE),
        name="ffn_dense",
    )(o, w_o, x, mod0, g_ffn, mod0, mod0, w_gu, w_gu, w_down, mod0,
      g_kv, mod_kv, mod_kv, g_mix1, mod1, mod1)


def _linear_kernel(x_ref, w_ref, o_ref):
    o_ref[...] = _dot(x_ref[...], w_ref[...]).astype(o_ref.dtype)


def _linear(x, w, *, tm, tn, name):
    n, k = x.shape
    m = w.shape[1]
    return pl.pallas_call(
        _linear_kernel,
        out_shape=jax.ShapeDtypeStruct((n, m), BF16),
        grid=(m // tn, n // tm),
        in_specs=[pl.BlockSpec((tm, k), lambda j, i: (i, 0)), pl.BlockSpec((k, tn), lambda j, i: (0, j))],
        out_specs=pl.BlockSpec((tm, tn), lambda j, i: (i, j)),
        compiler_params=_cparams(("parallel", "parallel")),
        name=name,
    )(x, w)


def _route_plan(counts, n_exp, tm, n_tiles):
    cnt = counts[:, 0, :n_exp].astype(jnp.int32)
    sizes = jnp.sum(cnt, axis=0)
    padded = (sizes + tm - 1) // tm * tm
    ends = jnp.cumsum(padded)
    tile_base = (ends - padded)[None, :] + jnp.cumsum(cnt, axis=0) - cnt
    tile_start = jnp.arange(n_tiles, dtype=jnp.int32) * tm
    tile_expert = jnp.minimum(jnp.sum(tile_start[:, None] >= ends[None, :], axis=1), n_exp - 1)
    n_used = (ends[-1] // tm).reshape(1)
    n_valid = jnp.clip((ends - padded + sizes)[tile_expert] - tile_start, 0, tm)
    return tile_base, tile_expert.astype(jnp.int32), n_used.astype(jnp.int32), n_valid.astype(jnp.int32)


PICK_ROWS = 8


def _picks_kernel(rtt_ref, base_ref, o_ref, *, n_rows):
    rtt = rtt_ref[...]
    sub = lax.broadcasted_iota(jnp.int32, rtt.shape, 0)
    expert = sub.astype(F32)
    out = jnp.zeros_like(rtt)
    for k in range(TOP_K):
        chosen = expert == rtt[k:k + 1, :]
        d_k = (jnp.sum(jnp.where(chosen, base_ref[...], 0.0), axis=0, keepdims=True)
               + rtt[2 * TOP_K + k:2 * TOP_K + k + 1, :])
        for p in range(SC_ROW_PARTS):
            out = jnp.where(sub == p * TOP_K + k, d_k + float(p * n_rows), out)
    o_ref[...] = out.astype(jnp.int32)


def _picks(rtt, tile_base, n_rows, *, tm):
    n = rtt.shape[1]
    n_tok_tiles, n_exp = tile_base.shape
    assert n_exp <= PICK_ROWS
    base = jnp.pad(tile_base.astype(F32), ((0, 0), (0, PICK_ROWS - n_exp))).reshape(n_tok_tiles, PICK_ROWS, 1)
    out = pl.pallas_call(
        functools.partial(_picks_kernel, n_rows=n_rows),
        out_shape=jax.ShapeDtypeStruct((PICK_ROWS, n), jnp.int32),
        grid=(n // tm,),
        in_specs=[pl.BlockSpec((PICK_ROWS, tm), lambda i: (0, i)),
                  pl.BlockSpec((None, PICK_ROWS, 1), lambda i: (i, 0, 0))],
        out_specs=pl.BlockSpec((PICK_ROWS, tm), lambda i: (0, i)),
        compiler_params=_cparams(("parallel",)),
        name="moe_picks",
    )(rtt, base)
    return out[:SC_ROW_PARTS * TOP_K].reshape(SC_ROW_PARTS, TOP_K, n)


def _sc_mesh():
    return plsc.VectorSubcoreMesh(core_axis_name="core", subcore_axis_name="subcore",
                                  num_cores=SC_CORES, num_subcores=SC_SUBCORES)


def _sc_scatter_rows(x, idx_list, n_rows):
    m, d = x.shape

    @functools.partial(pl.kernel, out_type=jax.ShapeDtypeStruct((n_rows, d), x.dtype), mesh=_sc_mesh(),
                       scratch_types=[], name="sc_scatter_rows")
    def scatter(x_hbm, *refs):
        i_hbms, o_hbm = refs[:-1], refs[-1]

        def body(x_vmem, *i_vmems):
            for i_vmem in i_vmems:
                pltpu.sync_copy(x_vmem, o_hbm.at[i_vmem.at[0]])

        pltpu.emit_pipeline(
            body,
            grid=(m // SC_WINDOW,),
            in_specs=[pl.BlockSpec((SC_WINDOW, d), index_map=lambda i: (i, 0))]
                     + [pl.BlockSpec((1, SC_WINDOW), index_map=lambda i: (0, i))] * len(idx_list),
            out_specs=[],
            core_axis_name=("core", "subcore"),
            dimension_semantics=(pltpu.PARALLEL,),
        )(x_hbm, *i_hbms)

    return scatter(x, *[idx.reshape(1, m) for idx in idx_list])


def _grouped_ffn_kernel(te_ref, nu_ref, nv_ref, x_ref, wg_ref, wu_ref, wd_ref, y_ref, hb_ref, acc_ref):
    del te_ref
    i = pl.program_id(0)
    f = pl.program_id(1)
    used = i < nu_ref[0]
    last_f = f == pl.num_programs(1) - 1

    @pl.when(used & (f == 0))
    def _():
        words = jnp.concatenate([x_ref[p] for p in range(SC_ROW_PARTS)], axis=1)
        row = lax.broadcasted_iota(jnp.int32, words.shape, 0)
        words = jnp.where(row < nv_ref[i], words, U32(0))
        hb_ref[...] = _unpack_bf16_pairs(words).astype(BF16)
        acc_ref[...] = jnp.zeros(acc_ref.shape, F32)

    @pl.when(used)
    def _():
        h = hb_ref[...]
        a = _silu(_dot(h, wg_ref[0])) * _dot(h, wu_ref[0])
        acc_ref[...] += _dot(a.astype(BF16), wd_ref[0])

    @pl.when(used & last_f)
    def _():
        y = _pack_bf16_pairs(acc_ref[...])
        for part in range(SC_ROW_PARTS):
            y_ref[part] = y[:, part * y_ref.shape[2]:(part + 1) * y_ref.shape[2]]

    @pl.when(jnp.logical_not(used) & last_f)
    def _():
        y_ref[...] = jnp.zeros(y_ref.shape, y_ref.dtype)


def _grouped_ffn(tile_expert, n_used, n_valid, xg, w_gu, w_down, *, tm, tf):
    parts, n_rows, dpp = xg.shape
    n_exp, d_ff, d = w_down.shape
    nf = d_ff // tf

    def wspec(shape, index):
        def index_map(i, f, te, nu, nv):
            return index(te[i], jnp.where(i < nu[0], f, nf - 1))
        return pl.BlockSpec(shape, index_map)

    rows = pl.BlockSpec((parts, tm, dpp), lambda i, f, te, nu, nv: (0, i, 0))
    return pl.pallas_call(
        _grouped_ffn_kernel,
        out_shape=jax.ShapeDtypeStruct(xg.shape, xg.dtype),
        grid_spec=pltpu.PrefetchScalarGridSpec(
            num_scalar_prefetch=3,
            grid=(n_rows // tm, nf),
            in_specs=[rows,
                      wspec((1, d, tf), lambda e, f: (e, 0, f)),
                      wspec((1, d, tf), lambda e, f: (e, 0, nf + f)),
                      wspec((1, tf, d), lambda e, f: (e, f, 0))],
            out_specs=rows,
            scratch_shapes=[pltpu.VMEM((tm, d), BF16), pltpu.VMEM((tm, d), F32)]),
        compiler_params=_cparams(("arbitrary", "arbitrary")),
        name="moe_grouped_ffn",
    )(tile_expert, n_used, n_valid, xg, w_gu, w_gu, w_down)


def _sc_gather_rows(table, idx):
    n_idx = idx.shape[0]
    d = table.shape[1]

    @functools.partial(pl.kernel, out_type=jax.ShapeDtypeStruct((n_idx, d), table.dtype), mesh=_sc_mesh(),
                       name="sc_gather_rows")
    def gather(t_hbm, i_hbm, o_hbm):
        def body(i_vmem, o_vmem):
            pltpu.sync_copy(t_hbm.at[i_vmem.at[0]], o_vmem)

        pltpu.emit_pipeline(
            body,
            grid=(n_idx // SC_WINDOW,),
            in_specs=[pl.BlockSpec((1, SC_WINDOW), index_map=lambda i: (0, i))],
            out_specs=[pl.BlockSpec((SC_WINDOW, d), index_map=lambda i: (i, 0))],
            core_axis_name=("core", "subcore"),
            dimension_semantics=(pltpu.PARALLEL,),
        )(i_hbm, o_hbm)

    return gather(table, idx.reshape(1, n_idx))


def _combine_kernel(x_ref, rt_ref, gt_ref, gf_ref, *refs):
    y_refs, o_ref = refs[:-1], refs[-1]
    rt = rt_ref[...]
    tot = None
    for k in range(TOP_K):
        words = jnp.concatenate([y_refs[p * TOP_K + k][...] for p in range(SC_ROW_PARTS)], axis=1)
        term = rt[:, TOP_K + k:TOP_K + k + 1] * _unpack_bf16_pairs(words)
        tot = term if tot is None else tot + term
    x = x_ref[...] + gt_ref[...] * tot
    o_ref[...] = _rms(x, gf_ref[...])


def _combine(x, rt, mod1, g_final, ysel, *, seq, tm):
    n, d = x.shape
    tpb = seq // tm
    nt = n // tm
    row = lambda w: pl.BlockSpec((tm, w), lambda i: (i, 0))
    piece = lambda j: pl.BlockSpec((tm, ysel.shape[1]), lambda i: (j * nt + i, 0))
    n_pieces = SC_ROW_PARTS * TOP_K
    return pl.pallas_call(
        _combine_kernel,
        out_shape=jax.ShapeDtypeStruct((n, d), F32),
        grid=(nt,),
        in_specs=[row(d), row(LANES), _mod_spec(5, tpb, d), pl.BlockSpec(g_final.shape, lambda i: (0, 0))]
                 + [piece(j) for j in range(n_pieces)],
        out_specs=row(d),
        compiler_params=_cparams(("parallel",)),
        name="moe_combine",
    )(x, rt, mod1, g_final, *([ysel] * n_pieces))


def _pad_last(a, width):
    return jnp.pad(a, [(0, 0)] * (a.ndim - 1) + [(0, width - a.shape[-1])])


def kernel(x, c, positions, w_mod, b_mod, g_mix, g_ffn, w_a_down, g_q_lat, g_kv_lat, w_uq, w_ukv, w_oa,
           w_mod_kv, b_mod_kv, g_kv, w_kv_sb, w_q_sb, w_o_sb, w_ffn_gu, w_ffn_down, w_router, b_router,
           w_exp_gu, w_exp_down, g_final):
    bsz, seq, d = x.shape
    n = bsz * seq
    q_lora, kv_lora = g_q_lat.shape[1], g_kv_lat.shape[1]
    n_exp = w_router.shape[-1]
    d_ff = w_ffn_down.shape[1]
    tm = min(TOKEN_TILE, seq)
    t_mla = min(MLA_TILE, seq)
    t_sb = min(SB_TILE, seq)
    tf = next((t for t in (FFN_CHUNK, 512) if d_ff % t == 0), d_ff)

    mod0 = _modvec(c, w_mod, 0, b_mod[0]).reshape(bsz, 6, 1, d)
    mod1 = _modvec(c, w_mod, 1, b_mod[1]).reshape(bsz, 6, 1, d)
    mod_kv = _modvec(c, w_mod_kv[None], 0, b_mod_kv).reshape(bsz, 2, 1, d)

    lat_w = q_lora + kv_lora + LANES
    wd = _pad_last(w_a_down[0], lat_w).astype(BF16)
    wq = w_uq[0].reshape(q_lora, MLA_HEADS, QK_NOPE + QK_ROPE)
    wqn = wq[:, :, :QK_NOPE].reshape(q_lora, MLA_HEADS * QK_NOPE).astype(BF16)
    wqr = _pad_last(wq[:, :, QK_NOPE:], LANES).reshape(q_lora, MLA_HEADS * LANES).astype(BF16)
    wkv = w_ukv[0].reshape(kv_lora, MLA_HEADS, QK_NOPE + V_DIM)
    wkn = wkv[:, :, :QK_NOPE].reshape(kv_lora, MLA_HEADS * QK_NOPE).astype(BF16)
    wv = wkv[:, :, QK_NOPE:].reshape(kv_lora, MLA_HEADS * V_DIM).astype(BF16)
    half = QK_ROPE // 2
    inv = ROPE_THETA ** (-jnp.arange(half, dtype=F32) / half)
    inv = jnp.tile(inv, ROPE_PACK).reshape(1, LANES)
    wr = _pad_last(w_router[0], LANES)
    wr_hi = wr.astype(BF16)
    wr_lo = (wr - wr_hi.astype(F32)).astype(BF16)
    br = _pad_last(b_router[0], LANES).reshape(1, LANES)

    xf = x.reshape(n, d)
    t_lin = min(LINEAR_TILE, seq)
    pos = positions.reshape(n // t_lin, ROPE_PACK, t_lin // ROPE_PACK).swapaxes(1, 2)
    pos = jnp.repeat(pos.reshape(n // ROPE_PACK, ROPE_PACK), QK_ROPE // 2, axis=1)
    row1 = lambda a: a.reshape(1, -1)

    q, k, v = _mla_proj(xf, pos, mod0, row1(g_mix[0]), wd, row1(g_q_lat[0]), row1(g_kv_lat[0]),
                        wqn, wqr, wkn, wv, inv, seq=seq, tm=t_lin)
    o = _mla_attn(q.reshape(bsz, seq, -1), k.reshape(bsz, seq, -1), v.reshape(bsz, seq, -1), t=t_mla)
    x2, hk, hm = _ffn(o.reshape(n, -1), w_oa[0].astype(BF16), xf, mod0, row1(g_ffn[0]),
                      w_ffn_gu[0].astype(BF16), w_ffn_down[0].astype(BF16),
                      row1(g_kv), mod_kv, row1(g_mix[1]), mod1, seq=seq, tm=tm, tf=tf)
    kv = _linear(hk, w_kv_sb.astype(BF16), tm=t_lin, tn=w_kv_sb.shape[1], name="kv_proj")
    q_scale = math.log2(math.e) / math.sqrt(SB_HEAD_DIM)
    qs = _linear(hm, (w_q_sb[0] * q_scale).astype(BF16), tm=t_lin, tn=1024, name="q_proj")
    o = _sb_attn(qs.reshape(bsz, seq, -1), kv.reshape(bsz, seq, -1), t=t_sb)
    x3, h, rt, rtt, counts = _out_proj_router(o.reshape(n, -1), w_o_sb[0].astype(BF16), x2, mod1,
                                              row1(g_ffn[1]), wr_hi, wr_lo, br, n_exp, seq=seq, tm=tm)
    n_tiles = TOP_K * n // tm + n_exp
    n_rows = n_tiles * tm
    tile_base, tile_expert, n_used, n_valid = _route_plan(counts, n_exp, tm, n_tiles)
    picks = _picks(rtt, tile_base, n_rows, tm=tm)
    xg = _sc_scatter_rows(h.reshape(SC_ROW_PARTS * n, -1), [picks[:, k].reshape(-1) for k in range(TOP_K)],
                          SC_ROW_PARTS * n_rows)
    yg = _grouped_ffn(tile_expert, n_used, n_valid, xg.reshape(SC_ROW_PARTS, n_rows, -1),
                      w_exp_gu[0].astype(BF16), w_exp_down[0].astype(BF16), tm=tm, tf=tf)
    ysel = _sc_gather_rows(yg.reshape(SC_ROW_PARTS * n_rows, -1), picks.reshape(-1))
    out = _combine(x3, rt, mod1, row1(g_final), ysel, seq=seq, tm=tm)
    return out.reshape(bsz, seq, d)
```

```python
import functools
import math

import jax
import jax.numpy as jnp
from jax import lax
from jax.experimental import pallas as pl
from jax.experimental.pallas import tpu as pltpu
from jax.experimental.pallas import tpu_sc as plsc

F32 = jnp.float32
BF16 = jnp.bfloat16

EPS = 1e-6
MLA_HEADS = 8
QK_NOPE = 128
QK_ROPE = 64
V_DIM = 128
ROPE_THETA = 10000.0
SB_HEADS = 8
SB_HEAD_DIM = 128
TOP_K = 2

LANES = 128
QK_PAD = 256
ROPE_PACK = LANES // (QK_ROPE // 2)
VMEM_LIMIT = 48 * 1024 * 1024
VMEM_LIMIT_LARGE = 58 * 1024 * 1024
TOKEN_TILE = 512
LINEAR_TILE = 1024
MLA_TILE = 512
MLA_BLOCKS_PER_TRIP = 4
SB_TILE = 512
SB_BLOCKS_PER_TRIP = 2
SB_ZERO_LOG2 = -140.0
FFN_CHUNK = 1792
SC_CORES = 2
SC_SUBCORES = 16
SC_WINDOW = 128
SC_ROW_PARTS = 2


def _cparams(sem, vmem_limit=VMEM_LIMIT):
    return pltpu.CompilerParams(dimension_semantics=sem, vmem_limit_bytes=vmem_limit)


def _rms(x, g):
    return x * lax.rsqrt(jnp.mean(x * x, axis=-1, keepdims=True) + EPS) * g


def _modulate(h, shift, scale):
    return h * (1.0 + scale) + shift


def _split_bf16(a):
    hi = a.astype(BF16)
    lo = (a - hi.astype(F32)).astype(BF16)
    return hi, lo


U32 = jnp.uint32
HIGH_HALF = 0xFFFF0000


def _pack_bf16_pairs(a):
    half = a.shape[1] // 2
    bits = lambda v: lax.bitcast_convert_type(v.astype(BF16).astype(F32), U32)
    return (bits(a[:, :half]) >> 16) | (bits(a[:, half:]) & U32(HIGH_HALF))


def _unpack_bf16_pairs(w):
    lo = lax.bitcast_convert_type(w << 16, F32)
    hi = lax.bitcast_convert_type(w & U32(HIGH_HALF), F32)
    return jnp.concatenate([lo, hi], axis=1)


def _dot(a, b):
    return jnp.dot(a, b, preferred_element_type=F32)


def _dot_nt(a, b):
    return lax.dot_general(a, b, (((1,), (1,)), ((), ())), preferred_element_type=F32)


def _modvec_kernel(c_ref, w_ref, b_ref, o_ref):
    c = c_ref[...]
    sc = c * (1.0 / (1.0 + jnp.exp(-c)))
    a_hi, a_lo = _split_bf16(sc)
    w_hi, w_lo = _split_bf16(w_ref[...])
    o_ref[...] = _dot(a_hi, w_hi) + _dot(a_lo, w_hi) + _dot(a_hi, w_lo) + b_ref[...]


def _modvec(c, w, layer, b, tn=512):
    bsz, d = c.shape
    n = w.shape[2]
    return pl.pallas_call(
        _modvec_kernel,
        out_shape=jax.ShapeDtypeStruct((bsz, n), F32),
        grid=(n // tn,),
        in_specs=[pl.BlockSpec((bsz, d), lambda j: (0, 0)),
                  pl.BlockSpec((None, d, tn), lambda j: (layer, 0, j)),
                  pl.BlockSpec((1, tn), lambda j: (0, j))],
        out_specs=pl.BlockSpec((bsz, tn), lambda j: (0, j)),
        compiler_params=_cparams(("arbitrary",)),
        name="modvec",
    )(c, w, b.reshape(1, n))


def _mod_spec(chunk, tiles_per_batch, d):
    return pl.BlockSpec((None, None, 1, d), lambda i, *_: (i // tiles_per_batch, chunk, 0, 0))


def _mla_proj_kernel(x_ref, pos_ref, sh_ref, sc_ref, g_ref, wd_ref, gq_ref, gkv_ref,
                     wqn_ref, wqr_ref, wkn_ref, wv_ref, inv_ref,
                     q_ref, k_ref, v_ref, *, q_lora, kv_lora):
    x = x_ref[...]
    h = _modulate(_rms(x, g_ref[...]), sh_ref[...], sc_ref[...]).astype(BF16)
    lat = _dot(h, wd_ref[...])
    c_q = _rms(lat[:, :q_lora], gq_ref[...]).astype(BF16)
    c_kv = _rms(lat[:, q_lora:q_lora + kv_lora], gkv_ref[...]).astype(BF16)
    k_rot = lat[:, q_lora + kv_lora:]

    half = QK_ROPE // 2
    ang = pos_ref[...].astype(F32) * inv_ref[...]
    cos_p = jnp.cos(ang)
    sin_p = jnp.sin(ang)
    lane = lax.broadcasted_iota(jnp.int32, ang.shape, 1)
    cos, s_a, s_b = [], [], []
    for g in range(ROPE_PACK):
        shift = (LANES - g * half) % LANES
        cg = pltpu.roll(cos_p, shift, axis=1) if shift else cos_p
        sg = pltpu.roll(sin_p, shift, axis=1) if shift else sin_p
        cos.append(jnp.where(lane < half, cg, pltpu.roll(cg, half, axis=1)))
        s_a.append(jnp.where(lane < half, -sg, 0.0))
        s_b.append(jnp.where((lane >= half) & (lane < 2 * half), pltpu.roll(sg, half, axis=1), 0.0))
    cos, s_a, s_b = (jnp.concatenate(v, axis=0) for v in (cos, s_a, s_b))

    def rope(r):
        return (r * cos + pltpu.roll(r, LANES - half, axis=1) * s_a
                + pltpu.roll(r, half, axis=1) * s_b)

    k_rot = rope(k_rot).astype(BF16)
    q_nope = _dot(c_q, wqn_ref[...])
    q_rope = _dot(c_q, wqr_ref[...])
    k_nope = _dot(c_kv, wkn_ref[...])
    v_ref[...] = _dot(c_kv, wv_ref[...]).astype(BF16)
    for hd in range(MLA_HEADS):
        a, b = hd * LANES, (hd + 1) * LANES
        q_ref[:, hd * QK_PAD:hd * QK_PAD + LANES] = q_nope[:, a:b].astype(BF16)
        q_ref[:, hd * QK_PAD + LANES:(hd + 1) * QK_PAD] = rope(q_rope[:, a:b]).astype(BF16)
        k_ref[:, hd * QK_PAD:hd * QK_PAD + LANES] = k_nope[:, a:b].astype(BF16)
        k_ref[:, hd * QK_PAD + LANES:(hd + 1) * QK_PAD] = k_rot


def _mla_proj(x, pos, mod, g_mix, wd, gq, gkv, wqn, wqr, wkn, wv, inv, *, seq, tm):
    n, d = x.shape
    tpb = seq // tm
    q_lora, kv_lora = gq.shape[1], gkv.shape[1]
    hq = MLA_HEADS * QK_PAD
    hv = MLA_HEADS * V_DIM
    full = lambda a: pl.BlockSpec(a.shape, lambda i: (0,) * a.ndim)
    row = lambda w: pl.BlockSpec((tm, w), lambda i: (i, 0))
    return pl.pallas_call(
        functools.partial(_mla_proj_kernel, q_lora=q_lora, kv_lora=kv_lora),
        out_shape=(jax.ShapeDtypeStruct((n, hq), BF16), jax.ShapeDtypeStruct((n, hq), BF16),
                   jax.ShapeDtypeStruct((n, hv), BF16)),
        grid=(n // tm,),
        in_specs=[row(d), pl.BlockSpec((tm // ROPE_PACK, LANES), lambda i: (i, 0)), _mod_spec(0, tpb, d), _mod_spec(1, tpb, d), full(g_mix), full(wd),
                  full(gq), full(gkv), full(wqn), full(wqr), full(wkn), full(wv), full(inv)],
        out_specs=(row(hq), row(hq), row(hv)),
        compiler_params=_cparams(("parallel",)),
        name="mla_proj",
    )(x, pos, mod, mod, g_mix, wd, gq, gkv, wqn, wqr, wkn, wv, inv)


def _lane_chunks(a):
    return [a[:, c * LANES:(c + 1) * LANES] for c in range(a.shape[1] // LANES)]


def _for_blocks(n, step, group):
    def trip(p, carry):
        step(group * p, group)
        return carry

    lax.fori_loop(0, n // group, trip, 0)
    size = group // 2
    while size:
        @pl.when((n // size) % 2 == 1)
        def _(size=size):
            step(n // (2 * size) * (2 * size), size)
        size //= 2


def _mla_attn_kernel(q_ref, k_ref, v_ref, o_ref, s_ref, m_ref, acc_ref, *, t, scale, heads):
    i = pl.program_id(2)

    def lane_max(s):
        m = None
        for sc in _lane_chunks(s):
            m = sc if m is None else jnp.maximum(m, sc)
        return m

    def scores(h, j):
        start = pl.multiple_of(j * t, t)
        hs = slice(h * QK_PAD, (h + 1) * QK_PAD)
        return _dot_nt(q_ref[0, :, hs], k_ref[0, pl.ds(start, t), hs])

    r_minus_c = (lax.broadcasted_iota(jnp.int32, (t, t), 0) - lax.broadcasted_iota(jnp.int32, (t, t), 1))
    m_ref[...] = jnp.full(m_ref.shape, -jnp.inf, F32)

    def pass1(j, nb):
        for h in range(heads):
            m = m_ref[h]
            for jj in range(nb):
                s = jnp.where(r_minus_c >= (j + jj - i) * t, scores(h, j + jj), -jnp.inf)
                s_ref[h, j + jj] = s
                m = jnp.maximum(m, lane_max(s))
            m_ref[h] = m

    _for_blocks(i + 1, pass1, MLA_BLOCKS_PER_TRIP)
    for h in range(heads):
        m_ref[h] = jnp.broadcast_to(jnp.max(m_ref[h], axis=1, keepdims=True), (t, LANES))
    acc_ref[...] = jnp.zeros(acc_ref.shape, F32)
    cst = scale * math.log2(math.e)

    def pass2(j, nb):
        start = pl.multiple_of(j * t, t)
        ones = jnp.ones((nb * t, LANES), BF16)
        for h in range(heads):
            m = m_ref[h]
            p = jnp.concatenate([jnp.exp2((sc - m) * cst).astype(BF16)
                                 for jj in range(nb) for sc in _lane_chunks(s_ref[h, j + jj])], axis=1)
            v_ext = jnp.concatenate([v_ref[0, pl.ds(start, nb * t), h * V_DIM:(h + 1) * V_DIM], ones], axis=1)
            acc_ref[h] += _dot(p, v_ext)

    _for_blocks(i + 1, pass2, MLA_BLOCKS_PER_TRIP)
    for h in range(heads):
        acc = acc_ref[h]
        o_ref[0, :, h * V_DIM:(h + 1) * V_DIM] = (acc[:, :V_DIM] / acc[:, V_DIM:]).astype(o_ref.dtype)


def _mla_attn(q, k, v, *, t, heads=2):
    bsz, seq, _ = q.shape
    scale = 1.0 / math.sqrt(QK_NOPE + QK_ROPE)
    wq, wv = heads * QK_PAD, heads * V_DIM
    return pl.pallas_call(
        functools.partial(_mla_attn_kernel, t=t, scale=scale, heads=heads),
        out_shape=jax.ShapeDtypeStruct((bsz, seq, MLA_HEADS * V_DIM), BF16),
        grid=(bsz, MLA_HEADS // heads, seq // t),
        in_specs=[pl.BlockSpec((1, t, wq), lambda b, g, i: (b, i, g)),
                  pl.BlockSpec((1, seq, wq), lambda b, g, i: (b, 0, g)),
                  pl.BlockSpec((1, seq, wv), lambda b, g, i: (b, 0, g))],
        out_specs=pl.BlockSpec((1, t, wv), lambda b, g, i: (b, i, g)),
        scratch_shapes=[pltpu.VMEM((heads, seq // t, t, t), F32), pltpu.VMEM((heads, t, LANES), F32),
                        pltpu.VMEM((heads, t, V_DIM + LANES), F32)],
        compiler_params=_cparams(("parallel", "parallel", "arbitrary")),
        name="mla_attn",
    )(q, k, v)


def _sb_attn_kernel(q_ref, k_ref, v_ref, w_ref, o_ref, c_ref, acc_ref, kmax_ref, bound_ref, *, t, heads):
    i = pl.program_id(2)
    d = SB_HEAD_DIM
    c_ref[...] = jnp.zeros(c_ref.shape, F32)
    acc_ref[...] = jnp.zeros(acc_ref.shape, F32)
    sub = t // 2 if t % (2 * LANES) == 0 else t
    r_iota = lax.broadcasted_iota(jnp.int32, (sub, LANES), 0)
    c_iota = lax.broadcasted_iota(jnp.int32, (sub, LANES), 1)

    def block(r0, nr, start, nk, key0):
        rows = slice(r0, r0 + nr)
        masks = []
        for cc in range(nk // LANES):
            off = None if key0 is None else key0 + cc * LANES - r0
            masks.append(None if off is None or off <= -LANES else c_iota + off < r_iota)
        for h in range(heads):
            hs = slice(h * d, (h + 1) * d)
            zz = _dot_nt(q_ref[0, rows, hs], k_ref[0, pl.ds(start, nk), hs])
            zneg = -zz
            sp = jnp.log2(1.0 + jnp.exp2(jnp.minimum(zz, zneg)))
            log_1m = jnp.minimum(zneg, 0.0) - sp
            carry = c_ref[h, rows]
            a_chunks = [None] * (nk // LANES)
            for cc in reversed(range(nk // LANES)):
                cs = slice(cc * LANES, (cc + 1) * LANES)
                l1m = log_1m[:, cs] if masks[cc] is None else jnp.where(masks[cc], log_1m[:, cs], 0.0)
                hi, lo = _split_bf16(l1m)
                y = _dot(jnp.concatenate([hi, lo], axis=1), w_ref[...])
                a = jnp.exp2(zz[:, cs] + y[:, :LANES] + carry)
                carry = carry + y[:, LANES:]
                if masks[cc] is not None:
                    a = jnp.where(masks[cc], a, 0.0)
                a_chunks[cc] = a.astype(BF16)
            c_ref[h, rows] = carry
            acc_ref[h, rows] += _dot(jnp.concatenate(a_chunks, axis=1), v_ref[0, pl.ds(start, nk), hs])

    def diagonal_tile():
        for band in range(t // sub):
            block(band * sub, sub, pl.multiple_of(i * t, t), (band + 1) * sub, 0)

    def left_blocks(n, nb):
        for jj in range(nb):
            block(0, t, pl.multiple_of((i - 1 - n - jj) * t, t), t, None)

    @pl.when(i == 0)
    def _():
        diagonal_tile()

    @pl.when(i > 0)
    def _():
        diagonal_tile()
        left_blocks(0, 1)

    @pl.when(i == 0)
    def _():
        for h in range(heads):
            kmax = jnp.max(jnp.abs(k_ref[0, :, h * d:(h + 1) * d].astype(F32)))
            kmax_ref[h] = jnp.full(kmax_ref.shape[1:], kmax, F32)

    for h in range(heads):
        q_l1 = jnp.sum(jnp.abs(q_ref[0, :, h * d:(h + 1) * d].astype(F32)), axis=1, keepdims=True)
        bound_ref[h] = q_l1 * kmax_ref[h, 0:1, :]

    def more_to_add():
        worst = None
        for h in range(heads):
            top = jnp.max(c_ref[h] + bound_ref[h])
            worst = top if worst is None else jnp.maximum(worst, top)
        return worst >= SB_ZERO_LOG2

    def trip(state):
        n, _ = state
        left_blocks(n, SB_BLOCKS_PER_TRIP)
        return n + SB_BLOCKS_PER_TRIP, more_to_add()

    n, go = lax.while_loop(lambda s: s[1] & (s[0] + SB_BLOCKS_PER_TRIP <= i), trip,
                           (jnp.int32(1), (i > 0) & more_to_add()))
    size = SB_BLOCKS_PER_TRIP // 2
    while size:
        take = go & (n + size <= i)

        @pl.when(take)
        def _(n=n, size=size):
            left_blocks(n, size)

        n = n + jnp.where(take, size, 0)
        size //= 2
    for h in range(heads):
        o_ref[0, :, h * d:(h + 1) * d] = acc_ref[h].astype(o_ref.dtype)


def _sb_attn(q, kv, *, t, heads=4):
    bsz, seq, _ = q.shape
    w = heads * SB_HEAD_DIM
    groups = SB_HEADS // heads
    tri = (jnp.arange(LANES)[:, None] >= jnp.arange(LANES)[None, :]).astype(BF16)
    half = jnp.concatenate([tri, jnp.ones((LANES, LANES), BF16)], axis=1)
    w_sum = jnp.concatenate([half, half], axis=0)
    return pl.pallas_call(
        functools.partial(_sb_attn_kernel, t=t, heads=heads),
        out_shape=jax.ShapeDtypeStruct((bsz, seq, SB_HEADS * SB_HEAD_DIM), BF16),
        grid=(bsz, groups, seq // t),
        in_specs=[pl.BlockSpec((1, t, w), lambda b, g, i: (b, i, g)),
                  pl.BlockSpec((1, seq, w), lambda b, g, i: (b, 0, g)),
                  pl.BlockSpec((1, seq, w), lambda b, g, i: (b, 0, groups + g)),
                  pl.BlockSpec(w_sum.shape, lambda b, g, i: (0, 0))],
        out_specs=pl.BlockSpec((1, t, w), lambda b, g, i: (b, i, g)),
        scratch_shapes=[pltpu.VMEM((heads, t, LANES), F32), pltpu.VMEM((heads, t, SB_HEAD_DIM), F32),
                        pltpu.VMEM((heads, 8, LANES), F32), pltpu.VMEM((heads, t, LANES), F32)],
        compiler_params=_cparams(("parallel", "parallel", "arbitrary")),
        name="sb_attn",
    )(q, kv, kv, w_sum)


def _out_proj_kernel(o_ref, w_ref, x_ref, gt_ref, g_ref, sh_ref, sc_ref, wr_hi_ref, wr_lo_ref, br_ref, tri_ref,
                     xo_ref, h_ref, rt_ref, rtt_ref, cnt_ref, *, n_experts):
    x = x_ref[...] + gt_ref[...] * _dot(o_ref[...], w_ref[...])
    xo_ref[...] = x
    h = _modulate(_rms(x, g_ref[...]), sh_ref[...], sc_ref[...])
    h_hi = h.astype(BF16)
    words = _pack_bf16_pairs(h)
    for part in range(SC_ROW_PARTS):
        h_ref[part] = words[:, part * h_ref.shape[2]:(part + 1) * h_ref.shape[2]]
    h_lo = (h - h_hi.astype(F32)).astype(BF16)
    both = _dot(h_hi, jnp.concatenate([wr_hi_ref[...], wr_lo_ref[...]], axis=1))
    logits = both[:, :LANES] + both[:, LANES:] + _dot(h_lo, wr_hi_ref[...]) + br_ref[...]
    lane = lax.broadcasted_iota(jnp.int32, logits.shape, 1).astype(F32)
    lg = jnp.where(lane < n_experts, logits, -jnp.inf)
    m1 = jnp.max(lg, axis=1, keepdims=True)
    i1 = jnp.min(jnp.where(lg == m1, lane, float(LANES)), axis=1, keepdims=True)
    lg2 = jnp.where(lane == i1, -jnp.inf, lg)
    m2 = jnp.max(lg2, axis=1, keepdims=True)
    i2 = jnp.min(jnp.where(lg2 == m2, lane, float(LANES)), axis=1, keepdims=True)
    e2 = jnp.exp(m2 - m1)
    den = 1.0 + e2
    sel = jnp.where((lane == i1) | (lane == i2), 1.0, 0.0)
    prefix = _dot(tri_ref[...], sel.astype(BF16))
    r1 = jnp.sum(jnp.where(lane == i1, prefix, 0.0), axis=1, keepdims=True)
    r2 = jnp.sum(jnp.where(lane == i2, prefix, 0.0), axis=1, keepdims=True)
    cnt_ref[...] = jnp.sum(sel, axis=0, keepdims=True)
    rt = jnp.zeros_like(logits)
    for k, val in enumerate((i1, i2, 1.0 / den, e2 / den, r1, r2)):
        rt = jnp.where(lane == k, val, rt)
    rt_ref[...] = rt
    rtt_ref[...] = rt.T[:PICK_ROWS]


def _out_proj_router(o, w, x, mod, g, wr_hi, wr_lo, br, n_experts, *, seq, tm):
    n, d = x.shape
    tpb = seq // tm
    dpp = d // 2 // SC_ROW_PARTS
    tri = (jnp.arange(tm)[:, None] > jnp.arange(tm)[None, :]).astype(BF16)
    full = lambda a: pl.BlockSpec(a.shape, lambda i: (0,) * a.ndim)
    row = lambda wd: pl.BlockSpec((tm, wd), lambda i: (i, 0))
    return pl.pallas_call(
        functools.partial(_out_proj_kernel, n_experts=n_experts),
        out_shape=(jax.ShapeDtypeStruct((n, d), F32), jax.ShapeDtypeStruct((SC_ROW_PARTS, n, dpp), U32),
                   jax.ShapeDtypeStruct((n, LANES), F32), jax.ShapeDtypeStruct((PICK_ROWS, n), F32),
                   jax.ShapeDtypeStruct((n // tm, 1, LANES), F32)),
        grid=(n // tm,),
        in_specs=[row(o.shape[1]), full(w), row(d), _mod_spec(2, tpb, d), full(g),
                  _mod_spec(3, tpb, d), _mod_spec(4, tpb, d), full(wr_hi), full(wr_lo), full(br), full(tri)],
        out_specs=(row(d), pl.BlockSpec((SC_ROW_PARTS, tm, dpp), lambda i: (0, i, 0)), row(LANES),
                   pl.BlockSpec((PICK_ROWS, tm), lambda i: (0, i)),
                   pl.BlockSpec((None, 1, LANES), lambda i: (i, 0, 0))),
        compiler_params=_cparams(("parallel",)),
        name="out_proj_router",
    )(o, w, x, mod, g, mod, mod, wr_hi, wr_lo, br, tri)


def _silu(g):
    return g * (1.0 / (1.0 + jnp.exp(-g)))


def _ffn_kernel(o_ref, wo_ref, x_ref, gt1_ref, gf_ref, sh2_ref, sc2_ref, wg_ref, wu_ref, wd_ref, gt2_ref,
                gkv_ref, shk_ref, sck_ref, gm_ref, shm_ref, scm_ref,
                xo_ref, hk_ref, hm_ref, x1_ref, h_ref, acc_ref):
    i = pl.program_id(0)
    f = pl.program_id(1)
    n_tiles = pl.num_programs(0) - 1

    def open_tile(slot):
        x1 = x_ref[...] + gt1_ref[...] * _dot(o_ref[...], wo_ref[...])
        x1_ref[slot] = x1
        h_ref[...] = _modulate(_rms(x1, gf_ref[...]), sh2_ref[...], sc2_ref[...]).astype(BF16)
        acc_ref[slot] = jnp.zeros(acc_ref.shape[1:], F32)

    def close_tile(slot):
        x = x1_ref[slot] + gt2_ref[...] * acc_ref[slot]
        xo_ref[...] = x
        y = x * lax.rsqrt(jnp.mean(x * x, axis=-1, keepdims=True) + EPS)
        hk_ref[...] = _modulate(y * gkv_ref[...], shk_ref[...], sck_ref[...]).astype(BF16)
        hm_ref[...] = _modulate(y * gm_ref[...], shm_ref[...], scm_ref[...]).astype(BF16)

    for parity in range(2):
        mine = (f == 0) & (i % 2 == parity)

        @pl.when(mine & (i == 0))
        def _(parity=parity):
            open_tile(parity)

        @pl.when(mine & (i > 0) & (i < n_tiles))
        def _(parity=parity):
            open_tile(parity)
            close_tile(1 - parity)

        @pl.when(mine & (i == n_tiles))
        def _(parity=parity):
            close_tile(1 - parity)

    @pl.when(i < n_tiles)
    def _():
        h = h_ref[...]
        a = _silu(_dot(h, wg_ref[...])) * _dot(h, wu_ref[...])
        acc_ref[i % 2] += _dot(a.astype(BF16), wd_ref[...])


def _ffn(o, w_o, x, mod0, g_ffn, w_gu, w_down, g_kv, mod_kv, g_mix1, mod1, *, seq, tm, tf):
    n, d = x.shape
    d_ff = w_down.shape[0]
    nf = d_ff // tf
    tpb = seq // tm
    nt = n // tm
    opened = lambda i: jnp.minimum(i, nt - 1)
    closed = lambda i: jnp.maximum(i - 1, 0)
    chunk = lambda i, f: jnp.where(i < nt, f, nf - 1)
    full = lambda a: pl.BlockSpec(a.shape, lambda i, f: (0,) * a.ndim)
    row_in = lambda w=d: pl.BlockSpec((tm, w), lambda i, f: (opened(i), 0))
    row_out = lambda: pl.BlockSpec((tm, d), lambda i, f: (closed(i), 0))
    mod = lambda c, tile: pl.BlockSpec((None, None, 1, d), lambda i, f: (tile(i) // tpb, c, 0, 0))
    return pl.pallas_call(
        _ffn_kernel,
        out_shape=(jax.ShapeDtypeStruct((n, d), F32), jax.ShapeDtypeStruct((n, d), BF16),
                   jax.ShapeDtypeStruct((n, d), BF16)),
        grid=(nt + 1, nf),
        in_specs=[row_in(o.shape[1]), full(w_o), row_in(), mod(2, opened), full(g_ffn),
                  mod(3, opened), mod(4, opened),
                  pl.BlockSpec((d, tf), lambda i, f: (0, chunk(i, f))),
                  pl.BlockSpec((d, tf), lambda i, f: (0, nf + chunk(i, f))),
                  pl.BlockSpec((tf, d), lambda i, f: (chunk(i, f), 0)),
                  mod(5, closed),
                  full(g_kv), mod(0, closed), mod(1, closed),
                  full(g_mix1), mod(0, closed), mod(1, closed)],
        out_specs=(row_out(), row_out(), row_out()),
        scratch_shapes=[pltpu.VMEM((2, tm, d), F32), pltpu.VMEM((tm, d), BF16), pltpu.VMEM((2, tm, d), F32)],
        compiler_params=_cparams(("arbitrary", "arbitrary"), vmem_limit=VMEM_LIMIT_LARGE),
        name="ffn_dense",
    )(o, w_o, x, mod0, g_ffn, mod0, mod0, w_gu, w_gu, w_down, mod0,
      g_kv, mod_kv, mod_kv, g_mix1, mod1, mod1)


def _linear_kernel(x_ref, w_ref, o_ref):
    o_ref[...] = _dot(x_ref[...], w_ref[...]).astype(o_ref.dtype)


def _linear(x, w, *, tm, tn, name):
    n, k = x.shape
    m = w.shape[1]
    return pl.pallas_call(
        _linear_kernel,
        out_shape=jax.ShapeDtypeStruct((n, m), BF16),
        grid=(m // tn, n // tm),
        in_specs=[pl.BlockSpec((tm, k), lambda j, i: (i, 0)), pl.BlockSpec((k, tn), lambda j, i: (0, j))],
        out_specs=pl.BlockSpec((tm, tn), lambda j, i: (i, j)),
        compiler_params=_cparams(("parallel", "parallel")),
        name=name,
    )(x, w)


def _route_plan(counts, n_exp, tm, n_tiles):
    cnt = counts[:, 0, :n_exp].astype(jnp.int32)
    sizes = jnp.sum(cnt, axis=0)
    padded = (sizes + tm - 1) // tm * tm
    ends = jnp.cumsum(padded)
    tile_base = (ends - padded)[None, :] + jnp.cumsum(cnt, axis=0) - cnt
    tile_start = jnp.arange(n_tiles, dtype=jnp.int32) * tm
    tile_expert = jnp.minimum(jnp.sum(tile_start[:, None] >= ends[None, :], axis=1), n_exp - 1)
    n_used = (ends[-1] // tm).reshape(1)
    n_valid = jnp.clip((ends - padded + sizes)[tile_expert] - tile_start, 0, tm)
    return tile_base, tile_expert.astype(jnp.int32), n_used.astype(jnp.int32), n_valid.astype(jnp.int32)


PICK_ROWS = 8


def _picks_kernel(rtt_ref, base_ref, o_ref, *, n_rows):
    rtt = rtt_ref[...]
    sub = lax.broadcasted_iota(jnp.int32, rtt.shape, 0)
    expert = sub.astype(F32)
    out = jnp.zeros_like(rtt)
    for k in range(TOP_K):
        chosen = expert == rtt[k:k + 1, :]
        d_k = (jnp.sum(jnp.where(chosen, base_ref[...], 0.0), axis=0, keepdims=True)
               + rtt[2 * TOP_K + k:2 * TOP_K + k + 1, :])
        for p in range(SC_ROW_PARTS):
            out = jnp.where(sub == p * TOP_K + k, d_k + float(p * n_rows), out)
    o_ref[...] = out.astype(jnp.int32)


def _picks(rtt, tile_base, n_rows, *, tm):
    n = rtt.shape[1]
    n_tok_tiles, n_exp = tile_base.shape
    assert n_exp <= PICK_ROWS
    base = jnp.pad(tile_base.astype(F32), ((0, 0), (0, PICK_ROWS - n_exp))).reshape(n_tok_tiles, PICK_ROWS, 1)
    out = pl.pallas_call(
        functools.partial(_picks_kernel, n_rows=n_rows),
        out_shape=jax.ShapeDtypeStruct((PICK_ROWS, n), jnp.int32),
        grid=(n // tm,),
        in_specs=[pl.BlockSpec((PICK_ROWS, tm), lambda i: (0, i)),
                  pl.BlockSpec((None, PICK_ROWS, 1), lambda i: (i, 0, 0))],
        out_specs=pl.BlockSpec((PICK_ROWS, tm), lambda i: (0, i)),
        compiler_params=_cparams(("parallel",)),
        name="moe_picks",
    )(rtt, base)
    return out[:SC_ROW_PARTS * TOP_K].reshape(SC_ROW_PARTS, TOP_K, n)


def _sc_mesh():
    return plsc.VectorSubcoreMesh(core_axis_name="core", subcore_axis_name="subcore",
                                  num_cores=SC_CORES, num_subcores=SC_SUBCORES)


def _sc_scatter_rows(x, idx_list, n_rows):
    m, d = x.shape

    @functools.partial(pl.kernel, out_type=jax.ShapeDtypeStruct((n_rows, d), x.dtype), mesh=_sc_mesh(),
                       scratch_types=[], name="sc_scatter_rows")
    def scatter(x_hbm, *refs):
        i_hbms, o_hbm = refs[:-1], refs[-1]

        def body(x_vmem, *i_vmems):
            for i_vmem in i_vmems:
                pltpu.sync_copy(x_vmem, o_hbm.at[i_vmem.at[0]])

        pltpu.emit_pipeline(
            body,
            grid=(m // SC_WINDOW,),
            in_specs=[pl.BlockSpec((SC_WINDOW, d), index_map=lambda i: (i, 0))]
                     + [pl.BlockSpec((1, SC_WINDOW), index_map=lambda i: (0, i))] * len(idx_list),
            out_specs=[],
            core_axis_name=("core", "subcore"),
            dimension_semantics=(pltpu.PARALLEL,),
        )(x_hbm, *i_hbms)

    return scatter(x, *[idx.reshape(1, m) for idx in idx_list])


def _grouped_ffn_kernel(te_ref, nu_ref, nv_ref, x_ref, wg_ref, wu_ref, wd_ref, y_ref, hb_ref, acc_ref):
    del te_ref
    i = pl.program_id(0)
    f = pl.program_id(1)
    used = i < nu_ref[0]
    last_f = f == pl.num_programs(1) - 1

    @pl.when(used & (f == 0))
    def _():
        words = jnp.concatenate([x_ref[p] for p in range(SC_ROW_PARTS)], axis=1)
        row = lax.broadcasted_iota(jnp.int32, words.shape, 0)
        words = jnp.where(row < nv_ref[i], words, U32(0))
        hb_ref[...] = _unpack_bf16_pairs(words).astype(BF16)
        acc_ref[...] = jnp.zeros(acc_ref.shape, F32)

    @pl.when(used)
    def _():
        h = hb_ref[...]
        a = _silu(_dot(h, wg_ref[0])) * _dot(h, wu_ref[0])
        acc_ref[...] += _dot(a.astype(BF16), wd_ref[0])

    @pl.when(used & last_f)
    def _():
        y = _pack_bf16_pairs(acc_ref[...])
        for part in range(SC_ROW_PARTS):
            y_ref[part] = y[:, part * y_ref.shape[2]:(part + 1) * y_ref.shape[2]]

    @pl.when(jnp.logical_not(used) & last_f)
    def _():
        y_ref[...] = jnp.zeros(y_ref.shape, y_ref.dtype)


def _grouped_ffn(tile_expert, n_used, n_valid, xg, w_gu, w_down, *, tm, tf):
    parts, n_rows, dpp = xg.shape
    n_exp, d_ff, d = w_down.shape
    nf = d_ff // tf

    def wspec(shape, index):
        def index_map(i, f, te, nu, nv):
            return index(te[i], jnp.where(i < nu[0], f, nf - 1))
        return pl.BlockSpec(shape, index_map)

    rows = pl.BlockSpec((parts, tm, dpp), lambda i, f, te, nu, nv: (0, i, 0))
    return pl.pallas_call(
        _grouped_ffn_kernel,
        out_shape=jax.ShapeDtypeStruct(xg.shape, xg.dtype),
        grid_spec=pltpu.PrefetchScalarGridSpec(
            num_scalar_prefetch=3,
            grid=(n_rows // tm, nf),
            in_specs=[rows,
                      wspec((1, d, tf), lambda e, f: (e, 0, f)),
                      wspec((1, d, tf), lambda e, f: (e, 0, nf + f)),
                      wspec((1, tf, d), lambda e, f: (e, f, 0))],
            out_specs=rows,
            scratch_shapes=[pltpu.VMEM((tm, d), BF16), pltpu.VMEM((tm, d), F32)]),
        compiler_params=_cparams(("arbitrary", "arbitrary")),
        name="moe_grouped_ffn",
    )(tile_expert, n_used, n_valid, xg, w_gu, w_gu, w_down)


def _sc_gather_rows(table, idx):
    n_idx = idx.shape[0]
    d = table.shape[1]

    @functools.partial(pl.kernel, out_type=jax.ShapeDtypeStruct((n_idx, d), table.dtype), mesh=_sc_mesh(),
                       name="sc_gather_rows")
    def gather(t_hbm, i_hbm, o_hbm):
        def body(i_vmem, o_vmem):
            pltpu.sync_copy(t_hbm.at[i_vmem.at[0]], o_vmem)

        pltpu.emit_pipeline(
            body,
            grid=(n_idx // SC_WINDOW,),
            in_specs=[pl.BlockSpec((1, SC_WINDOW), index_map=lambda i: (0, i))],
            out_specs=[pl.BlockSpec((SC_WINDOW, d), index_map=lambda i: (i, 0))],
            core_axis_name=("core", "subcore"),
            dimension_semantics=(pltpu.PARALLEL,),
        )(i_hbm, o_hbm)

    return gather(table, idx.reshape(1, n_idx))


def _combine_kernel(x_ref, rt_ref, gt_ref, gf_ref, *refs):
    y_refs, o_ref = refs[:-1], refs[-1]
    rt = rt_ref[...]
    tot = None
    for k in range(TOP_K):
        words = jnp.concatenate([y_refs[p * TOP_K + k][...] for p in range(SC_ROW_PARTS)], axis=1)
        term = rt[:, TOP_K + k:TOP_K + k + 1] * _unpack_bf16_pairs(words)
        tot = term if tot is None else tot + term
    x = x_ref[...] + gt_ref[...] * tot
    o_ref[...] = _rms(x, gf_ref[...])


def _combine(x, rt, mod1, g_final, ysel, *, seq, tm):
    n, d = x.shape
    tpb = seq // tm
    nt = n // tm
    row = lambda w: pl.BlockSpec((tm, w), lambda i: (i, 0))
    piece = lambda j: pl.BlockSpec((tm, ysel.shape[1]), lambda i: (j * nt + i, 0))
    n_pieces = SC_ROW_PARTS * TOP_K
    return pl.pallas_call(
        _combine_kernel,
        out_shape=jax.ShapeDtypeStruct((n, d), F32),
        grid=(nt,),
        in_specs=[row(d), row(LANES), _mod_spec(5, tpb, d), pl.BlockSpec(g_final.shape, lambda i: (0, 0))]
                 + [piece(j) for j in range(n_pieces)],
        out_specs=row(d),
        compiler_params=_cparams(("parallel",)),
        name="moe_combine",
    )(x, rt, mod1, g_final, *([ysel] * n_pieces))


def _pad_last(a, width):
    return jnp.pad(a, [(0, 0)] * (a.ndim - 1) + [(0, width - a.shape[-1])])


def kernel(x, c, positions, w_mod, b_mod, g_mix, g_ffn, w_a_down, g_q_lat, g_kv_lat, w_uq, w_ukv, w_oa,
           w_mod_kv, b_mod_kv, g_kv, w_kv_sb, w_q_sb, w_o_sb, w_ffn_gu, w_ffn_down, w_router, b_router,
           w_exp_gu, w_exp_down, g_final):
    bsz, seq, d = x.shape
    n = bsz * seq
    q_lora, kv_lora = g_q_lat.shape[1], g_kv_lat.shape[1]
    n_exp = w_router.shape[-1]
    d_ff = w_ffn_down.shape[1]
    tm = min(TOKEN_TILE, seq)
    t_mla = min(MLA_TILE, seq)
    t_sb = min(SB_TILE, seq)
    tf = next((t for t in (FFN_CHUNK, 512) if d_ff % t == 0), d_ff)

    mod0 = _modvec(c, w_mod, 0, b_mod[0]).reshape(bsz, 6, 1, d)
    mod1 = _modvec(c, w_mod, 1, b_mod[1]).reshape(bsz, 6, 1, d)
    mod_kv = _modvec(c, w_mod_kv[None], 0, b_mod_kv).reshape(bsz, 2, 1, d)

    lat_w = q_lora + kv_lora + LANES
    wd = _pad_last(w_a_down[0], lat_w).astype(BF16)
    wq = w_uq[0].reshape(q_lora, MLA_HEADS, QK_NOPE + QK_ROPE)
    wqn = wq[:, :, :QK_NOPE].reshape(q_lora, MLA_HEADS * QK_NOPE).astype(BF16)
    wqr = _pad_last(wq[:, :, QK_NOPE:], LANES).reshape(q_lora, MLA_HEADS * LANES).astype(BF16)
    wkv = w_ukv[0].reshape(kv_lora, MLA_HEADS, QK_NOPE + V_DIM)
    wkn = wkv[:, :, :QK_NOPE].reshape(kv_lora, MLA_HEADS * QK_NOPE).astype(BF16)
    wv = wkv[:, :, QK_NOPE:].reshape(kv_lora, MLA_HEADS * V_DIM).astype(BF16)
    half = QK_ROPE // 2
    inv = ROPE_THETA ** (-jnp.arange(half, dtype=F32) / half)
    inv = jnp.tile(inv, ROPE_PACK).reshape(1, LANES)
    wr = _pad_last(w_router[0], LANES)
    wr_hi = wr.astype(BF16)
    wr_lo = (wr - wr_hi.astype(F32)).astype(BF16)
    br = _pad_last(b_router[0], LANES).reshape(1, LANES)

    xf = x.reshape(n, d)
    t_lin = min(LINEAR_TILE, seq)
    pos = positions.reshape(n // t_lin, ROPE_PACK, t_lin // ROPE_PACK).swapaxes(1, 2)
    pos = jnp.repeat(pos.reshape(n // ROPE_PACK, ROPE_PACK), QK_ROPE // 2, axis=1)
    row1 = lambda a: a.reshape(1, -1)

    q, k, v = _mla_proj(xf, pos, mod0, row1(g_mix[0]), wd, row1(g_q_lat[0]), row1(g_kv_lat[0]),
                        wqn, wqr, wkn, wv, inv, seq=seq, tm=t_lin)
    o = _mla_attn(q.reshape(bsz, seq, -1), k.reshape(bsz, seq, -1), v.reshape(bsz, seq, -1), t=t_mla)
    x2, hk, hm = _ffn(o.reshape(n, -1), w_oa[0].astype(BF16), xf, mod0, row1(g_ffn[0]),
                      w_ffn_gu[0].astype(BF16), w_ffn_down[0].astype(BF16),
                      row1(g_kv), mod_kv, row1(g_mix[1]), mod1, seq=seq, tm=tm, tf=tf)
    kv = _linear(hk, w_kv_sb.astype(BF16), tm=t_lin, tn=w_kv_sb.shape[1], name="kv_proj")
    q_scale = math.log2(math.e) / math.sqrt(SB_HEAD_DIM)
    qs = _linear(hm, (w_q_sb[0] * q_scale).astype(BF16), tm=t_lin, tn=1024, name="q_proj")
    o = _sb_attn(qs.reshape(bsz, seq, -1), kv.reshape(bsz, seq, -1), t=t_sb)
    x3, h, rt, rtt, counts = _out_proj_router(o.reshape(n, -1), w_o_sb[0].astype(BF16), x2, mod1,
                                              row1(g_ffn[1]), wr_hi, wr_lo, br, n_exp, seq=seq, tm=tm)
    n_tiles = TOP_K * n // tm + n_exp
    n_rows = n_tiles * tm
    tile_base, tile_expert, n_used, n_valid = _route_plan(counts, n_exp, tm, n_tiles)
    picks = _picks(rtt, tile_base, n_rows, tm=tm)
    xg = _sc_scatter_rows(h.reshape(SC_ROW_PARTS * n, -1), [picks[:, k].reshape(-1) for k in range(TOP_K)],
                          SC_ROW_PARTS * n_rows)
    yg = _grouped_ffn(tile_expert, n_used, n_valid, xg.reshape(SC_ROW_PARTS, n_rows, -1),
                      w_exp_gu[0].astype(BF16), w_exp_down[0].astype(BF16), tm=tm, tf=tf)
    ysel = _sc_gather_rows(yg.reshape(SC_ROW_PARTS * n_rows, -1), picks.reshape(-1))
    out = _combine(x3, rt, mod1, row1(g_final), ysel, seq=seq, tm=t_lin)
    return out.reshape(bsz, seq, d)
```

```python
import functools
import math

import jax
import jax.numpy as jnp
from jax import lax
from jax.experimental import pallas as pl
from jax.experimental.pallas import tpu as pltpu
from jax.experimental.pallas import tpu_sc as plsc

F32 = jnp.float32
BF16 = jnp.bfloat16

EPS = 1e-6
MLA_HEADS = 8
QK_NOPE = 128
QK_ROPE = 64
V_DIM = 128
ROPE_THETA = 10000.0
SB_HEADS = 8
SB_HEAD_DIM = 128
TOP_K = 2

LANES = 128
QK_PAD = 256
ROPE_PACK = LANES // (QK_ROPE // 2)
VMEM_LIMIT = 48 * 1024 * 1024
VMEM_LIMIT_LARGE = 58 * 1024 * 1024
TOKEN_TILE = 512
LINEAR_TILE = 1024
MLA_TILE = 512
MLA_BLOCKS_PER_TRIP = 4
SB_TILE = 512
SB_BLOCKS_PER_TRIP = 2
SB_ZERO_LOG2 = -140.0
FFN_CHUNK = 1792
SC_CORES = 2
SC_SUBCORES = 16
SC_WINDOW = 128
SC_ROW_PARTS = 2


def _cparams(sem, vmem_limit=VMEM_LIMIT):
    return pltpu.CompilerParams(dimension_semantics=sem, vmem_limit_bytes=vmem_limit)


def _rms(x, g):
    return x * lax.rsqrt(jnp.mean(x * x, axis=-1, keepdims=True) + EPS) * g


def _modulate(h, shift, scale):
    return h * (1.0 + scale) + shift


def _split_bf16(a):
    hi = a.astype(BF16)
    lo = (a - hi.astype(F32)).astype(BF16)
    return hi, lo


U32 = jnp.uint32
HIGH_HALF = 0xFFFF0000


def _pack_bf16_pairs(a):
    half = a.shape[1] // 2
    bits = lambda v: lax.bitcast_convert_type(v.astype(BF16).astype(F32), U32)
    return (bits(a[:, :half]) >> 16) | (bits(a[:, half:]) & U32(HIGH_HALF))


def _unpack_bf16_pairs(w):
    lo = lax.bitcast_convert_type(w << 16, F32)
    hi = lax.bitcast_convert_type(w & U32(HIGH_HALF), F32)
    return jnp.concatenate([lo, hi], axis=1)


def _dot(a, b):
    return jnp.dot(a, b, preferred_element_type=F32)


def _dot_nt(a, b):
    return lax.dot_general(a, b, (((1,), (1,)), ((), ())), preferred_element_type=F32)


def _modvec_kernel(c_ref, w_ref, b_ref, o_ref):
    c = c_ref[...]
    sc = c * (1.0 / (1.0 + jnp.exp(-c)))
    a_hi, a_lo = _split_bf16(sc)
    w_hi, w_lo = _split_bf16(w_ref[...])
    o_ref[...] = _dot(a_hi, w_hi) + _dot(a_lo, w_hi) + _dot(a_hi, w_lo) + b_ref[...]


def _modvec(c, w, layer, b, tn=512):
    bsz, d = c.shape
    n = w.shape[2]
    return pl.pallas_call(
        _modvec_kernel,
        out_shape=jax.ShapeDtypeStruct((bsz, n), F32),
        grid=(n // tn,),
        in_specs=[pl.BlockSpec((bsz, d), lambda j: (0, 0)),
                  pl.BlockSpec((None, d, tn), lambda j: (layer, 0, j)),
                  pl.BlockSpec((1, tn), lambda j: (0, j))],
        out_specs=pl.BlockSpec((bsz, tn), lambda j: (0, j)),
        compiler_params=_cparams(("arbitrary",)),
        name="modvec",
    )(c, w, b.reshape(1, n))


def _mod_spec(chunk, tiles_per_batch, d):
    return pl.BlockSpec((None, None, 1, d), lambda i, *_: (i // tiles_per_batch, chunk, 0, 0))


def _mla_proj_kernel(x_ref, pos_ref, sh_ref, sc_ref, g_ref, wd_ref, gq_ref, gkv_ref,
                     wqn_ref, wqr_ref, wkn_ref, wv_ref, inv_ref,
                     q_ref, k_ref, v_ref, *, q_lora, kv_lora):
    x = x_ref[...]
    h = _modulate(_rms(x, g_ref[...]), sh_ref[...], sc_ref[...]).astype(BF16)
    lat = _dot(h, wd_ref[...])
    c_q = _rms(lat[:, :q_lora], gq_ref[...]).astype(BF16)
    c_kv = _rms(lat[:, q_lora:q_lora + kv_lora], gkv_ref[...]).astype(BF16)
    k_rot = lat[:, q_lora + kv_lora:]

    half = QK_ROPE // 2
    ang = pos_ref[...].astype(F32) * inv_ref[...]
    cos_p = jnp.cos(ang)
    sin_p = jnp.sin(ang)
    lane = lax.broadcasted_iota(jnp.int32, ang.shape, 1)
    cos, s_a, s_b = [], [], []
    for g in range(ROPE_PACK):
        shift = (LANES - g * half) % LANES
        cg = pltpu.roll(cos_p, shift, axis=1) if shift else cos_p
        sg = pltpu.roll(sin_p, shift, axis=1) if shift else sin_p
        cos.append(jnp.where(lane < half, cg, pltpu.roll(cg, half, axis=1)))
        s_a.append(jnp.where(lane < half, -sg, 0.0))
        s_b.append(jnp.where((lane >= half) & (lane < 2 * half), pltpu.roll(sg, half, axis=1), 0.0))
    cos, s_a, s_b = (jnp.concatenate(v, axis=0) for v in (cos, s_a, s_b))

    def rope(r):
        return (r * cos + pltpu.roll(r, LANES - half, axis=1) * s_a
                + pltpu.roll(r, half, axis=1) * s_b)

    k_rot = rope(k_rot).astype(BF16)
    q_nope = _dot(c_q, wqn_ref[...])
    q_rope = _dot(c_q, wqr_ref[...])
    k_nope = _dot(c_kv, wkn_ref[...])
    v_ref[...] = _dot(c_kv, wv_ref[...]).astype(BF16)
    for hd in range(MLA_HEADS):
        a, b = hd * LANES, (hd + 1) * LANES
        q_ref[:, hd * QK_PAD:hd * QK_PAD + LANES] = q_nope[:, a:b].astype(BF16)
        q_ref[:, hd * QK_PAD + LANES:(hd + 1) * QK_PAD] = rope(q_rope[:, a:b]).astype(BF16)
        k_ref[:, hd * QK_PAD:hd * QK_PAD + LANES] = k_nope[:, a:b].astype(BF16)
        k_ref[:, hd * QK_PAD + LANES:(hd + 1) * QK_PAD] = k_rot


def _mla_proj(x, pos, mod, g_mix, wd, gq, gkv, wqn, wqr, wkn, wv, inv, *, seq, tm):
    n, d = x.shape
    tpb = seq // tm
    q_lora, kv_lora = gq.shape[1], gkv.shape[1]
    hq = MLA_HEADS * QK_PAD
    hv = MLA_HEADS * V_DIM
    full = lambda a: pl.BlockSpec(a.shape, lambda i: (0,) * a.ndim)
    row = lambda w: pl.BlockSpec((tm, w), lambda i: (i, 0))
    return pl.pallas_call(
        functools.partial(_mla_proj_kernel, q_lora=q_lora, kv_lora=kv_lora),
        out_shape=(jax.ShapeDtypeStruct((n, hq), BF16), jax.ShapeDtypeStruct((n, hq), BF16),
                   jax.ShapeDtypeStruct((n, hv), BF16)),
        grid=(n // tm,),
        in_specs=[row(d), pl.BlockSpec((tm // ROPE_PACK, LANES), lambda i: (i, 0)), _mod_spec(0, tpb, d), _mod_spec(1, tpb, d), full(g_mix), full(wd),
                  full(gq), full(gkv), full(wqn), full(wqr), full(wkn), full(wv), full(inv)],
        out_specs=(row(hq), row(hq), row(hv)),
        compiler_params=_cparams(("parallel",)),
        name="mla_proj",
    )(x, pos, mod, mod, g_mix, wd, gq, gkv, wqn, wqr, wkn, wv, inv)


def _lane_chunks(a):
    return [a[:, c * LANES:(c + 1) * LANES] for c in range(a.shape[1] // LANES)]


def _for_blocks(n, step, group):
    def trip(p, carry):
        step(group * p, group)
        return carry

    lax.fori_loop(0, n // group, trip, 0)
    size = group // 2
    while size:
        @pl.when((n // size) % 2 == 1)
        def _(size=size):
            step(n // (2 * size) * (2 * size), size)
        size //= 2


def _mla_attn_kernel(q_ref, k_ref, v_ref, o_ref, s_ref, m_ref, acc_ref, *, t, scale, heads):
    i = pl.program_id(2)

    def lane_max(s):
        m = None
        for sc in _lane_chunks(s):
            m = sc if m is None else jnp.maximum(m, sc)
        return m

    def scores(h, j):
        start = pl.multiple_of(j * t, t)
        hs = slice(h * QK_PAD, (h + 1) * QK_PAD)
        return _dot_nt(q_ref[0, :, hs], k_ref[0, pl.ds(start, t), hs])

    r_minus_c = (lax.broadcasted_iota(jnp.int32, (t, t), 0) - lax.broadcasted_iota(jnp.int32, (t, t), 1))
    m_ref[...] = jnp.full(m_ref.shape, -jnp.inf, F32)

    def pass1(j, nb):
        for h in range(heads):
            m = m_ref[h]
            for jj in range(nb):
                s = jnp.where(r_minus_c >= (j + jj - i) * t, scores(h, j + jj), -jnp.inf)
                s_ref[h, j + jj] = s
                m = jnp.maximum(m, lane_max(s))
            m_ref[h] = m

    _for_blocks(i + 1, pass1, MLA_BLOCKS_PER_TRIP)
    for h in range(heads):
        m_ref[h] = jnp.broadcast_to(jnp.max(m_ref[h], axis=1, keepdims=True), (t, LANES))
    acc_ref[...] = jnp.zeros(acc_ref.shape, F32)
    cst = scale * math.log2(math.e)

    def pass2(j, nb):
        start = pl.multiple_of(j * t, t)
        ones = jnp.ones((nb * t, LANES), BF16)
        for h in range(heads):
            m = m_ref[h]
            p = jnp.concatenate([jnp.exp2((sc - m) * cst).astype(BF16)
                                 for jj in range(nb) for sc in _lane_chunks(s_ref[h, j + jj])], axis=1)
            v_ext = jnp.concatenate([v_ref[0, pl.ds(start, nb * t), h * V_DIM:(h + 1) * V_DIM], ones], axis=1)
            acc_ref[h] += _dot(p, v_ext)

    _for_blocks(i + 1, pass2, MLA_BLOCKS_PER_TRIP)
    for h in range(heads):
        acc = acc_ref[h]
        o_ref[0, :, h * V_DIM:(h + 1) * V_DIM] = (acc[:, :V_DIM] / acc[:, V_DIM:]).astype(o_ref.dtype)


def _mla_attn(q, k, v, *, t, heads=2):
    bsz, seq, _ = q.shape
    scale = 1.0 / math.sqrt(QK_NOPE + QK_ROPE)
    wq, wv = heads * QK_PAD, heads * V_DIM
    return pl.pallas_call(
        functools.partial(_mla_attn_kernel, t=t, scale=scale, heads=heads),
        out_shape=jax.ShapeDtypeStruct((bsz, seq, MLA_HEADS * V_DIM), BF16),
        grid=(bsz, MLA_HEADS // heads, seq // t),
        in_specs=[pl.BlockSpec((1, t, wq), lambda b, g, i: (b, i, g)),
                  pl.BlockSpec((1, seq, wq), lambda b, g, i: (b, 0, g)),
                  pl.BlockSpec((1, seq, wv), lambda b, g, i: (b, 0, g))],
        out_specs=pl.BlockSpec((1, t, wv), lambda b, g, i: (b, i, g)),
        scratch_shapes=[pltpu.VMEM((heads, seq // t, t, t), F32), pltpu.VMEM((heads, t, LANES), F32),
                        pltpu.VMEM((heads, t, V_DIM + LANES), F32)],
        compiler_params=_cparams(("parallel", "parallel", "arbitrary")),
        name="mla_attn",
    )(q, k, v)


def _sb_attn_kernel(q_ref, k_ref, v_ref, w_ref, o_ref, c_ref, acc_ref, kmax_ref, bound_ref, *, t, heads):
    i = pl.program_id(2)
    d = SB_HEAD_DIM
    c_ref[...] = jnp.zeros(c_ref.shape, F32)
    acc_ref[...] = jnp.zeros(acc_ref.shape, F32)
    sub = t // 2 if t % (2 * LANES) == 0 else t
    r_iota = lax.broadcasted_iota(jnp.int32, (sub, LANES), 0)
    c_iota = lax.broadcasted_iota(jnp.int32, (sub, LANES), 1)

    def block(r0, nr, start, nk, key0):
        rows = slice(r0, r0 + nr)
        masks = []
        for cc in range(nk // LANES):
            off = None if key0 is None else key0 + cc * LANES - r0
            masks.append(None if off is None or off <= -LANES else c_iota + off < r_iota)
        for h in range(heads):
            hs = slice(h * d, (h + 1) * d)
            zz = _dot_nt(q_ref[0, rows, hs], k_ref[0, pl.ds(start, nk), hs])
            zneg = -zz
            sp = jnp.log2(1.0 + jnp.exp2(jnp.minimum(zz, zneg)))
            log_1m = jnp.minimum(zneg, 0.0) - sp
            carry = c_ref[h, rows]
            a_chunks = [None] * (nk // LANES)
            for cc in reversed(range(nk // LANES)):
                cs = slice(cc * LANES, (cc + 1) * LANES)
                l1m = log_1m[:, cs] if masks[cc] is None else jnp.where(masks[cc], log_1m[:, cs], 0.0)
                hi, lo = _split_bf16(l1m)
                y = _dot(jnp.concatenate([hi, lo], axis=1), w_ref[...])
                a = jnp.exp2(zz[:, cs] + y[:, :LANES] + carry)
                carry = carry + y[:, LANES:]
                if masks[cc] is not None:
                    a = jnp.where(masks[cc], a, 0.0)
                a_chunks[cc] = a.astype(BF16)
            c_ref[h, rows] = carry
            acc_ref[h, rows] += _dot(jnp.concatenate(a_chunks, axis=1), v_ref[0, pl.ds(start, nk), hs])

    def diagonal_tile():
        for band in range(t // sub):
            block(band * sub, sub, pl.multiple_of(i * t, t), (band + 1) * sub, 0)

    def left_blocks(n, nb):
        for jj in range(nb):
            block(0, t, pl.multiple_of((i - 1 - n - jj) * t, t), t, None)

    @pl.when(i == 0)
    def _():
        diagonal_tile()

    @pl.when(i > 0)
    def _():
        diagonal_tile()
        left_blocks(0, 1)

    @pl.when(i == 0)
    def _():
        for h in range(heads):
            kmax = jnp.max(jnp.abs(k_ref[0, :, h * d:(h + 1) * d].astype(F32)))
            kmax_ref[h] = jnp.full(kmax_ref.shape[1:], kmax, F32)

    for h in range(heads):
        q_l1 = jnp.sum(jnp.abs(q_ref[0, :, h * d:(h + 1) * d].astype(F32)), axis=1, keepdims=True)
        bound_ref[h] = q_l1 * kmax_ref[h, 0:1, :]

    def more_to_add():
        worst = None
        for h in range(heads):
            top = jnp.max(c_ref[h] + bound_ref[h])
            worst = top if worst is None else jnp.maximum(worst, top)
        return worst >= SB_ZERO_LOG2

    def trip(state):
        n, _ = state
        left_blocks(n, SB_BLOCKS_PER_TRIP)
        return n + SB_BLOCKS_PER_TRIP, more_to_add()

    n, go = lax.while_loop(lambda s: s[1] & (s[0] + SB_BLOCKS_PER_TRIP <= i), trip,
                           (jnp.int32(1), (i > 0) & more_to_add()))
    size = SB_BLOCKS_PER_TRIP // 2
    while size:
        take = go & (n + size <= i)

        @pl.when(take)
        def _(n=n, size=size):
            left_blocks(n, size)

        n = n + jnp.where(take, size, 0)
        size //= 2
    for h in range(heads):
        o_ref[0, :, h * d:(h + 1) * d] = acc_ref[h].astype(o_ref.dtype)


def _sb_attn(q, kv, *, t, heads=4):
    bsz, seq, _ = q.shape
    w = heads * SB_HEAD_DIM
    groups = SB_HEADS // heads
    tri = (jnp.arange(LANES)[:, None] >= jnp.arange(LANES)[None, :]).astype(BF16)
    half = jnp.concatenate([tri, jnp.ones((LANES, LANES), BF16)], axis=1)
    w_sum = jnp.concatenate([half, half], axis=0)
    return pl.pallas_call(
        functools.partial(_sb_attn_kernel, t=t, heads=heads),
        out_shape=jax.ShapeDtypeStruct((bsz, seq, SB_HEADS * SB_HEAD_DIM), BF16),
        grid=(bsz, groups, seq // t),
        in_specs=[pl.BlockSpec((1, t, w), lambda b, g, i: (b, i, g)),
                  pl.BlockSpec((1, seq, w), lambda b, g, i: (b, 0, g)),
                  pl.BlockSpec((1, seq, w), lambda b, g, i: (b, 0, groups + g)),
                  pl.BlockSpec(w_sum.shape, lambda b, g, i: (0, 0))],
        out_specs=pl.BlockSpec((1, t, w), lambda b, g, i: (b, i, g)),
        scratch_shapes=[pltpu.VMEM((heads, t, LANES), F32), pltpu.VMEM((heads, t, SB_HEAD_DIM), F32),
                        pltpu.VMEM((heads, 8, LANES), F32), pltpu.VMEM((heads, t, LANES), F32)],
        compiler_params=_cparams(("parallel", "parallel", "arbitrary")),
        name="sb_attn",
    )(q, kv, kv, w_sum)


def _out_proj_kernel(o_ref, w_ref, x_ref, gt_ref, g_ref, sh_ref, sc_ref, wr_hi_ref, wr_lo_ref, br_ref, tri_ref,
                     xo_ref, h_ref, rt_ref, rtt_ref, cnt_ref, *, n_experts):
    x = x_ref[...] + gt_ref[...] * _dot(o_ref[...], w_ref[...])
    xo_ref[...] = x
    h = _modulate(_rms(x, g_ref[...]), sh_ref[...], sc_ref[...])
    h_hi = h.astype(BF16)
    words = _pack_bf16_pairs(h)
    for part in range(SC_ROW_PARTS):
        h_ref[part] = words[:, part * h_ref.shape[2]:(part + 1) * h_ref.shape[2]]
    h_lo = (h - h_hi.astype(F32)).astype(BF16)
    both = _dot(h_hi, jnp.concatenate([wr_hi_ref[...], wr_lo_ref[...]], axis=1))
    logits = both[:, :LANES] + both[:, LANES:] + _dot(h_lo, wr_hi_ref[...]) + br_ref[...]
    lane = lax.broadcasted_iota(jnp.int32, logits.shape, 1).astype(F32)
    lg = jnp.where(lane < n_experts, logits, -jnp.inf)
    m1 = jnp.max(lg, axis=1, keepdims=True)
    i1 = jnp.min(jnp.where(lg == m1, lane, float(LANES)), axis=1, keepdims=True)
    lg2 = jnp.where(lane == i1, -jnp.inf, lg)
    m2 = jnp.max(lg2, axis=1, keepdims=True)
    i2 = jnp.min(jnp.where(lg2 == m2, lane, float(LANES)), axis=1, keepdims=True)
    e2 = jnp.exp(m2 - m1)
    den = 1.0 + e2
    sel = jnp.where((lane == i1) | (lane == i2), 1.0, 0.0)
    prefix = _dot(tri_ref[...], sel.astype(BF16))
    r1 = jnp.sum(jnp.where(lane == i1, prefix, 0.0), axis=1, keepdims=True)
    r2 = jnp.sum(jnp.where(lane == i2, prefix, 0.0), axis=1, keepdims=True)
    cnt_ref[...] = jnp.sum(sel, axis=0, keepdims=True)
    rt = jnp.zeros_like(logits)
    for k, val in enumerate((i1, i2, 1.0 / den, e2 / den, r1, r2)):
        rt = jnp.where(lane == k, val, rt)
    rt_ref[...] = rt
    rtt_ref[...] = rt.T[:PICK_ROWS]


def _out_proj_router(o, w, x, mod, g, wr_hi, wr_lo, br, n_experts, *, seq, tm):
    n, d = x.shape
    tpb = seq // tm
    dpp = d // 2 // SC_ROW_PARTS
    tri = (jnp.arange(tm)[:, None] > jnp.arange(tm)[None, :]).astype(BF16)
    full = lambda a: pl.BlockSpec(a.shape, lambda i: (0,) * a.ndim)
    row = lambda wd: pl.BlockSpec((tm, wd), lambda i: (i, 0))
    return pl.pallas_call(
        functools.partial(_out_proj_kernel, n_experts=n_experts),
        out_shape=(jax.ShapeDtypeStruct((n, d), F32), jax.ShapeDtypeStruct((SC_ROW_PARTS, n, dpp), U32),
                   jax.ShapeDtypeStruct((n, LANES), F32), jax.ShapeDtypeStruct((PICK_ROWS, n), F32),
                   jax.ShapeDtypeStruct((n // tm, 1, LANES), F32)),
        grid=(n // tm,),
        in_specs=[row(o.shape[1]), full(w), row(d), _mod_spec(2, tpb, d), full(g),
                  _mod_spec(3, tpb, d), _mod_spec(4, tpb, d), full(wr_hi), full(wr_lo), full(br), full(tri)],
        out_specs=(row(d), pl.BlockSpec((SC_ROW_PARTS, tm, dpp), lambda i: (0, i, 0)), row(LANES),
                   pl.BlockSpec((PICK_ROWS, tm), lambda i: (0, i)),
                   pl.BlockSpec((None, 1, LANES), lambda i: (i, 0, 0))),
        compiler_params=_cparams(("parallel",)),
        name="out_proj_router",
    )(o, w, x, mod, g, mod, mod, wr_hi, wr_lo, br, tri)


def _silu(g):
    return g * (1.0 / (1.0 + jnp.exp(-g)))


def _ffn_kernel(o_ref, wo_ref, x_ref, gt1_ref, gf_ref, sh2_ref, sc2_ref, wg_ref, wu_ref, wd_ref, gt2_ref,
                gkv_ref, shk_ref, sck_ref, gm_ref, shm_ref, scm_ref,
                xo_ref, hk_ref, hm_ref, x1_ref, h_ref, acc_ref):
    i = pl.program_id(0)
    f = pl.program_id(1)
    n_tiles = pl.num_programs(0) - 1

    def open_tile(slot):
        x1 = x_ref[...] + gt1_ref[...] * _dot(o_ref[...], wo_ref[...])
        x1_ref[slot] = x1
        h_ref[...] = _modulate(_rms(x1, gf_ref[...]), sh2_ref[...], sc2_ref[...]).astype(BF16)
        acc_ref[slot] = jnp.zeros(acc_ref.shape[1:], F32)

    def close_tile(slot):
        x = x1_ref[slot] + gt2_ref[...] * acc_ref[slot]
        xo_ref[...] = x
        y = x * lax.rsqrt(jnp.mean(x * x, axis=-1, keepdims=True) + EPS)
        hk_ref[...] = _modulate(y * gkv_ref[...], shk_ref[...], sck_ref[...]).astype(BF16)
        hm_ref[...] = _modulate(y * gm_ref[...], shm_ref[...], scm_ref[...]).astype(BF16)

    for parity in range(2):
        mine = (f == 0) & (i % 2 == parity)

        @pl.when(mine & (i == 0))
        def _(parity=parity):
            open_tile(parity)

        @pl.when(mine & (i > 0) & (i < n_tiles))
        def _(parity=parity):
            open_tile(parity)
            close_tile(1 - parity)

        @pl.when(mine & (i == n_tiles))
        def _(parity=parity):
            close_tile(1 - parity)

    @pl.when(i < n_tiles)
    def _():
        h = h_ref[...]
        a = _silu(_dot(h, wg_ref[...])) * _dot(h, wu_ref[...])
        acc_ref[i % 2] += _dot(a.astype(BF16), wd_ref[...])


def _ffn(o, w_o, x, mod0, g_ffn, w_gu, w_down, g_kv, mod_kv, g_mix1, mod1, *, seq, tm, tf):
    n, d = x.shape
    d_ff = w_down.shape[0]
    nf = d_ff // tf
    tpb = seq // tm
    nt = n // tm
    opened = lambda i: jnp.minimum(i, nt - 1)
    closed = lambda i: jnp.maximum(i - 1, 0)
    chunk = lambda i, f: jnp.where(i < nt, f, nf - 1)
    full = lambda a: pl.BlockSpec(a.shape, lambda i, f: (0,) * a.ndim)
    row_in = lambda w=d: pl.BlockSpec((tm, w), lambda i, f: (opened(i), 0))
    row_out = lambda: pl.BlockSpec((tm, d), lambda i, f: (closed(i), 0))
    mod = lambda c, tile: pl.BlockSpec((None, None, 1, d), lambda i, f: (tile(i) // tpb, c, 0, 0))
    return pl.pallas_call(
        _ffn_kernel,
        out_shape=(jax.ShapeDtypeStruct((n, d), F32), jax.ShapeDtypeStruct((n, d), BF16),
                   jax.ShapeDtypeStruct((n, d), BF16)),
        grid=(nt + 1, nf),
        in_specs=[row_in(o.shape[1]), full(w_o), row_in(), mod(2, opened), full(g_ffn),
                  mod(3, opened), mod(4, opened),
                  pl.BlockSpec((d, tf), lambda i, f: (0, chunk(i, f))),
                  pl.BlockSpec((d, tf), lambda i, f: (0, nf + chunk(i, f))),
                  pl.BlockSpec((tf, d), lambda i, f: (chunk(i, f), 0)),
                  mod(5, closed),
                  full(g_kv), mod(0, closed), mod(1, closed),
                  full(g_mix1), mod(0, closed), mod(1, closed)],
        out_specs=(row_out(), row_out(), row_out()),
        scratch_shapes=[pltpu.VMEM((2, tm, d), F32), pltpu.VMEM((tm, d), BF16), pltpu.VMEM((2, tm, d), F32)],
        compiler_params=_cparams(("arbitrary", "arbitrary"), vmem_limit=VMEM_LIMIT_LARGE),
        name="ffn_dense",
    )(o, w_o, x, mod0, g_ffn, mod0, mod0, w_gu, w_gu, w_down, mod0,
      g_kv, mod_kv, mod_kv, g_mix1, mod1, mod1)


def _linear_kernel(x_ref, w_ref, o_ref):
    o_ref[...] = _dot(x_ref[...], w_ref[...]).astype(o_ref.dtype)


def _linear(x, w, *, tm, tn, name):
    n, k = x.shape
    m = w.shape[1]
    return pl.pallas_call(
        _linear_kernel,
        out_shape=jax.ShapeDtypeStruct((n, m), BF16),
        grid=(m // tn, n // tm),
        in_specs=[pl.BlockSpec((tm, k), lambda j, i: (i, 0)), pl.BlockSpec((k, tn), lambda j, i: (0, j))],
        out_specs=pl.BlockSpec((tm, tn), lambda j, i: (i, j)),
        compiler_params=_cparams(("parallel", "parallel")),
        name=name,
    )(x, w)


def _route_plan(counts, n_exp, tm, n_tiles):
    cnt = counts[:, 0, :n_exp].astype(jnp.int32)
    sizes = jnp.sum(cnt, axis=0)
    padded = (sizes + tm - 1) // tm * tm
    ends = jnp.cumsum(padded)
    tile_base = (ends - padded)[None, :] + jnp.cumsum(cnt, axis=0) - cnt
    tile_start = jnp.arange(n_tiles, dtype=jnp.int32) * tm
    tile_expert = jnp.minimum(jnp.sum(tile_start[:, None] >= ends[None, :], axis=1), n_exp - 1)
    n_used = (ends[-1] // tm).reshape(1)
    n_valid = jnp.clip((ends - padded + sizes)[tile_expert] - tile_start, 0, tm)
    return tile_base, tile_expert.astype(jnp.int32), n_used.astype(jnp.int32), n_valid.astype(jnp.int32)


PICK_ROWS = 8


def _picks_kernel(rtt_ref, base_ref, o_ref, *, n_rows):
    rtt = rtt_ref[...]
    sub = lax.broadcasted_iota(jnp.int32, rtt.shape, 0)
    expert = sub.astype(F32)
    out = jnp.zeros_like(rtt)
    for k in range(TOP_K):
        chosen = expert == rtt[k:k + 1, :]
        d_k = (jnp.sum(jnp.where(chosen, base_ref[...], 0.0), axis=0, keepdims=True)
               + rtt[2 * TOP_K + k:2 * TOP_K + k + 1, :])
        for p in range(SC_ROW_PARTS):
            out = jnp.where(sub == p * TOP_K + k, d_k + float(p * n_rows), out)
    o_ref[...] = out.astype(jnp.int32)


def _picks(rtt, tile_base, n_rows, *, tm):
    n = rtt.shape[1]
    n_tok_tiles, n_exp = tile_base.shape
    assert n_exp <= PICK_ROWS
    base = jnp.pad(tile_base.astype(F32), ((0, 0), (0, PICK_ROWS - n_exp))).reshape(n_tok_tiles, PICK_ROWS, 1)
    out = pl.pallas_call(
        functools.partial(_picks_kernel, n_rows=n_rows),
        out_shape=jax.ShapeDtypeStruct((PICK_ROWS, n), jnp.int32),
        grid=(n // tm,),
        in_specs=[pl.BlockSpec((PICK_ROWS, tm), lambda i: (0, i)),
                  pl.BlockSpec((None, PICK_ROWS, 1), lambda i: (i, 0, 0))],
        out_specs=pl.BlockSpec((PICK_ROWS, tm), lambda i: (0, i)),
        compiler_params=_cparams(("parallel",)),
        name="moe_picks",
    )(rtt, base)
    return out[:SC_ROW_PARTS * TOP_K].reshape(SC_ROW_PARTS, TOP_K, n)


def _sc_mesh():
    return plsc.VectorSubcoreMesh(core_axis_name="core", subcore_axis_name="subcore",
                                  num_cores=SC_CORES, num_subcores=SC_SUBCORES)


def _sc_scatter_rows(x, idx_list, n_rows):
    m, d = x.shape

    @functools.partial(pl.kernel, out_type=jax.ShapeDtypeStruct((n_rows, d), x.dtype), mesh=_sc_mesh(),
                       scratch_types=[], name="sc_scatter_rows")
    def scatter(x_hbm, *refs):
        i_hbms, o_hbm = refs[:-1], refs[-1]

        def body(x_vmem, *i_vmems):
            for i_vmem in i_vmems:
                pltpu.sync_copy(x_vmem, o_hbm.at[i_vmem.at[0]])

        pltpu.emit_pipeline(
            body,
            grid=(m // SC_WINDOW,),
            in_specs=[pl.BlockSpec((SC_WINDOW, d), index_map=lambda i: (i, 0))]
                     + [pl.BlockSpec((1, SC_WINDOW), index_map=lambda i: (0, i))] * len(idx_list),
            out_specs=[],
            core_axis_name=("core", "subcore"),
            dimension_semantics=(pltpu.PARALLEL,),
        )(x_hbm, *i_hbms)

    return scatter(x, *[idx.reshape(1, m) for idx in idx_list])


def _grouped_ffn_kernel(te_ref, nu_ref, nv_ref, x_ref, wg_ref, wu_ref, wd_ref, y_ref, hb_ref, acc_ref):
    del te_ref
    i = pl.program_id(0)
    f = pl.program_id(1)
    used = i < nu_ref[0]
    last_f = f == pl.num_programs(1) - 1

    @pl.when(used & (f == 0))
    def _():
        words = jnp.concatenate([x_ref[p] for p in range(SC_ROW_PARTS)], axis=1)
        row = lax.broadcasted_iota(jnp.int32, words.shape, 0)
        words = jnp.where(row < nv_ref[i], words, U32(0))
        hb_ref[...] = _unpack_bf16_pairs(words).astype(BF16)
        acc_ref[...] = jnp.zeros(acc_ref.shape, F32)

    @pl.when(used)
    def _():
        half = hb_ref.shape[0] // 2
        for r0 in (0, half):
            h = hb_ref[r0:r0 + half, :]
            a = _silu(_dot(h, wg_ref[0])) * _dot(h, wu_ref[0])
            acc_ref[r0:r0 + half, :] += _dot(a.astype(BF16), wd_ref[0])

    @pl.when(used & last_f)
    def _():
        y = _pack_bf16_pairs(acc_ref[...])
        for part in range(SC_ROW_PARTS):
            y_ref[part] = y[:, part * y_ref.shape[2]:(part + 1) * y_ref.shape[2]]

    @pl.when(jnp.logical_not(used) & last_f)
    def _():
        y_ref[...] = jnp.zeros(y_ref.shape, y_ref.dtype)


def _grouped_ffn(tile_expert, n_used, n_valid, xg, w_gu, w_down, *, tm, tf):
    parts, n_rows, dpp = xg.shape
    n_exp, d_ff, d = w_down.shape
    nf = d_ff // tf

    def wspec(shape, index):
        def index_map(i, f, te, nu, nv):
            return index(te[i], jnp.where(i < nu[0], f, nf - 1))
        return pl.BlockSpec(shape, index_map)

    rows = pl.BlockSpec((parts, tm, dpp), lambda i, f, te, nu, nv: (0, i, 0))
    return pl.pallas_call(
        _grouped_ffn_kernel,
        out_shape=jax.ShapeDtypeStruct(xg.shape, xg.dtype),
        grid_spec=pltpu.PrefetchScalarGridSpec(
            num_scalar_prefetch=3,
            grid=(n_rows // tm, nf),
            in_specs=[rows,
                      wspec((1, d, tf), lambda e, f: (e, 0, f)),
                      wspec((1, d, tf), lambda e, f: (e, 0, nf + f)),
                      wspec((1, tf, d), lambda e, f: (e, f, 0))],
            out_specs=rows,
            scratch_shapes=[pltpu.VMEM((tm, d), BF16), pltpu.VMEM((tm, d), F32)]),
        compiler_params=_cparams(("arbitrary", "arbitrary")),
        name="moe_grouped_ffn",
    )(tile_expert, n_used, n_valid, xg, w_gu, w_gu, w_down)


def _sc_gather_rows(table, idx):
    n_idx = idx.shape[0]
    d = table.shape[1]

    @functools.partial(pl.kernel, out_type=jax.ShapeDtypeStruct((n_idx, d), table.dtype), mesh=_sc_mesh(),
                       name="sc_gather_rows")
    def gather(t_hbm, i_hbm, o_hbm):
        def body(i_vmem, o_vmem):
            pltpu.sync_copy(t_hbm.at[i_vmem.at[0]], o_vmem)

        pltpu.emit_pipeline(
            body,
            grid=(n_idx // SC_WINDOW,),
            in_specs=[pl.BlockSpec((1, SC_WINDOW), index_map=lambda i: (0, i))],
            out_specs=[pl.BlockSpec((SC_WINDOW, d), index_map=lambda i: (i, 0))],
            core_axis_name=("core", "subcore"),
            dimension_semantics=(pltpu.PARALLEL,),
        )(i_hbm, o_hbm)

    return gather(table, idx.reshape(1, n_idx))


def _combine_kernel(x_ref, rt_ref, gt_ref, gf_ref, *refs):
    y_refs, o_ref = refs[:-1], refs[-1]
    rt = rt_ref[...]
    tot = None
    for k in range(TOP_K):
        words = jnp.concatenate([y_refs[p * TOP_K + k][...] for p in range(SC_ROW_PARTS)], axis=1)
        term = rt[:, TOP_K + k:TOP_K + k + 1] * _unpack_bf16_pairs(words)
        tot = term if tot is None else tot + term
    x = x_ref[...] + gt_ref[...] * tot
    o_ref[...] = _rms(x, gf_ref[...])


def _combine(x, rt, mod1, g_final, ysel, *, seq, tm):
    n, d = x.shape
    tpb = seq // tm
    nt = n // tm
    row = lambda w: pl.BlockSpec((tm, w), lambda i: (i, 0))
    piece = lambda j: pl.BlockSpec((tm, ysel.shape[1]), lambda i: (j * nt + i, 0))
    n_pieces = SC_ROW_PARTS * TOP_K
    return pl.pallas_call(
        _combine_kernel,
        out_shape=jax.ShapeDtypeStruct((n, d), F32),
        grid=(nt,),
        in_specs=[row(d), row(LANES), _mod_spec(5, tpb, d), pl.BlockSpec(g_final.shape, lambda i: (0, 0))]
                 + [piece(j) for j in range(n_pieces)],
        out_specs=row(d),
        compiler_params=_cparams(("parallel",)),
        name="moe_combine",
    )(x, rt, mod1, g_final, *([ysel] * n_pieces))


def _pad_last(a, width):
    return jnp.pad(a, [(0, 0)] * (a.ndim - 1) + [(0, width - a.shape[-1])])


def kernel(x, c, positions, w_mod, b_mod, g_mix, g_ffn, w_a_down, g_q_lat, g_kv_lat, w_uq, w_ukv, w_oa,
           w_mod_kv, b_mod_kv, g_kv, w_kv_sb, w_q_sb, w_o_sb, w_ffn_gu, w_ffn_down, w_router, b_router,
           w_exp_gu, w_exp_down, g_final):
    bsz, seq, d = x.shape
    n = bsz * seq
    q_lora, kv_lora = g_q_lat.shape[1], g_kv_lat.shape[1]
    n_exp = w_router.shape[-1]
    d_ff = w_ffn_down.shape[1]
    tm = min(TOKEN_TILE, seq)
    t_mla = min(MLA_TILE, seq)
    t_sb = min(SB_TILE, seq)
    tf = next((t for t in (FFN_CHUNK, 512) if d_ff % t == 0), d_ff)

    mod0 = _modvec(c, w_mod, 0, b_mod[0]).reshape(bsz, 6, 1, d)
    mod1 = _modvec(c, w_mod, 1, b_mod[1]).reshape(bsz, 6, 1, d)
    mod_kv = _modvec(c, w_mod_kv[None], 0, b_mod_kv).reshape(bsz, 2, 1, d)

    lat_w = q_lora + kv_lora + LANES
    wd = _pad_last(w_a_down[0], lat_w).astype(BF16)
    wq = w_uq[0].reshape(q_lora, MLA_HEADS, QK_NOPE + QK_ROPE)
    wqn = wq[:, :, :QK_NOPE].reshape(q_lora, MLA_HEADS * QK_NOPE).astype(BF16)
    wqr = _pad_last(wq[:, :, QK_NOPE:], LANES).reshape(q_lora, MLA_HEADS * LANES).astype(BF16)
    wkv = w_ukv[0].reshape(kv_lora, MLA_HEADS, QK_NOPE + V_DIM)
    wkn = wkv[:, :, :QK_NOPE].reshape(kv_lora, MLA_HEADS * QK_NOPE).astype(BF16)
    wv = wkv[:, :, QK_NOPE:].reshape(kv_lora, MLA_HEADS * V_DIM).astype(BF16)
    half = QK_ROPE // 2
    inv = ROPE_THETA ** (-jnp.arange(half, dtype=F32) / half)
    inv = jnp.tile(inv, ROPE_PACK).reshape(1, LANES)
    wr = _pad_last(w_router[0], LANES)
    wr_hi = wr.astype(BF16)
    wr_lo = (wr - wr_hi.astype(F32)).astype(BF16)
    br = _pad_last(b_router[0], LANES).reshape(1, LANES)

    xf = x.reshape(n, d)
    t_lin = min(LINEAR_TILE, seq)
    pos = positions.reshape(n // t_lin, ROPE_PACK, t_lin // ROPE_PACK).swapaxes(1, 2)
    pos = jnp.repeat(pos.reshape(n // ROPE_PACK, ROPE_PACK), QK_ROPE // 2, axis=1)
    row1 = lambda a: a.reshape(1, -1)

    q, k, v = _mla_proj(xf, pos, mod0, row1(g_mix[0]), wd, row1(g_q_lat[0]), row1(g_kv_lat[0]),
                        wqn, wqr, wkn, wv, inv, seq=seq, tm=t_lin)
    o = _mla_attn(q.reshape(bsz, seq, -1), k.reshape(bsz, seq, -1), v.reshape(bsz, seq, -1), t=t_mla)
    x2, hk, hm = _ffn(o.reshape(n, -1), w_oa[0].astype(BF16), xf, mod0, row1(g_ffn[0]),
                      w_ffn_gu[0].astype(BF16), w_ffn_down[0].astype(BF16),
                      row1(g_kv), mod_kv, row1(g_mix[1]), mod1, seq=seq, tm=tm, tf=tf)
    kv = _linear(hk, w_kv_sb.astype(BF16), tm=t_lin, tn=w_kv_sb.shape[1], name="kv_proj")
    q_scale = math.log2(math.e) / math.sqrt(SB_HEAD_DIM)
    qs = _linear(hm, (w_q_sb[0] * q_scale).astype(BF16), tm=t_lin, tn=1024, name="q_proj")
    o = _sb_attn(qs.reshape(bsz, seq, -1), kv.reshape(bsz, seq, -1), t=t_sb)
    x3, h, rt, rtt, counts = _out_proj_router(o.reshape(n, -1), w_o_sb[0].astype(BF16), x2, mod1,
                                              row1(g_ffn[1]), wr_hi, wr_lo, br, n_exp, seq=seq, tm=tm)
    n_tiles = TOP_K * n // tm + n_exp
    n_rows = n_tiles * tm
    tile_base, tile_expert, n_used, n_valid = _route_plan(counts, n_exp, tm, n_tiles)
    picks = _picks(rtt, tile_base, n_rows, tm=tm)
    xg = _sc_scatter_rows(h.reshape(SC_ROW_PARTS * n, -1), [picks[:, k].reshape(-1) for k in range(TOP_K)],
                          SC_ROW_PARTS * n_rows)
    yg = _grouped_ffn(tile_expert, n_used, n_valid, xg.reshape(SC_ROW_PARTS, n_rows, -1),
                      w_exp_gu[0].astype(BF16), w_exp_down[0].astype(BF16), tm=tm, tf=tf)
    ysel = _sc_gather_rows(yg.reshape(SC_ROW_PARTS * n_rows, -1), picks.reshape(-1))
    out = _combine(x3, rt, mod1, row1(g_final), ysel, seq=seq, tm=tm)
    return out.reshape(bsz, seq, d)
```

```python
import functools
import math

import jax
import jax.numpy as jnp
from jax import lax
from jax.experimental import pallas as pl
from jax.experimental.pallas import tpu as pltpu
from jax.experimental.pallas import tpu_sc as plsc

F32 = jnp.float32
BF16 = jnp.bfloat16

EPS = 1e-6
MLA_HEADS = 8
QK_NOPE = 128
QK_ROPE = 64
V_DIM = 128
ROPE_THETA = 10000.0
SB_HEADS = 8
SB_HEAD_DIM = 128
TOP_K = 2

LANES = 128
QK_PAD = 256
ROPE_PACK = LANES // (QK_ROPE // 2)
VMEM_LIMIT = 48 * 1024 * 1024
VMEM_LIMIT_LARGE = 58 * 1024 * 1024
TOKEN_TILE = 512
LINEAR_TILE = 1024
MLA_TILE = 512
MLA_BLOCKS_PER_TRIP = 4
SB_TILE = 512
SB_BLOCKS_PER_TRIP = 2
SB_ZERO_LOG2 = -140.0
FFN_CHUNK = 1792
SC_CORES = 2
SC_SUBCORES = 16
SC_WINDOW = 128
SC_ROW_PARTS = 2


def _cparams(sem, vmem_limit=VMEM_LIMIT):
    return pltpu.CompilerParams(dimension_semantics=sem, vmem_limit_bytes=vmem_limit)


def _rms(x, g):
    return x * lax.rsqrt(jnp.mean(x * x, axis=-1, keepdims=True) + EPS) * g


def _modulate(h, shift, scale):
    return h * (1.0 + scale) + shift


def _split_bf16(a):
    hi = a.astype(BF16)
    lo = (a - hi.astype(F32)).astype(BF16)
    return hi, lo


U32 = jnp.uint32
HIGH_HALF = 0xFFFF0000


def _pack_bf16_pairs(a):
    half = a.shape[1] // 2
    bits = lambda v: lax.bitcast_convert_type(v.astype(BF16).astype(F32), U32)
    return (bits(a[:, :half]) >> 16) | (bits(a[:, half:]) & U32(HIGH_HALF))


def _unpack_bf16_pairs(w):
    lo = lax.bitcast_convert_type(w << 16, F32)
    hi = lax.bitcast_convert_type(w & U32(HIGH_HALF), F32)
    return jnp.concatenate([lo, hi], axis=1)


def _dot(a, b):
    return jnp.dot(a, b, preferred_element_type=F32)


def _dot_nt(a, b):
    return lax.dot_general(a, b, (((1,), (1,)), ((), ())), preferred_element_type=F32)


def _modvec_kernel(c_ref, w_ref, b_ref, o_ref):
    c = c_ref[...]
    sc = c * (1.0 / (1.0 + jnp.exp(-c)))
    a_hi, a_lo = _split_bf16(sc)
    w_hi, w_lo = _split_bf16(w_ref[...])
    o_ref[...] = _dot(a_hi, w_hi) + _dot(a_lo, w_hi) + _dot(a_hi, w_lo) + b_ref[...]


def _modvec(c, w, layer, b, tn=512):
    bsz, d = c.shape
    n = w.shape[2]
    return pl.pallas_call(
        _modvec_kernel,
        out_shape=jax.ShapeDtypeStruct((bsz, n), F32),
        grid=(n // tn,),
        in_specs=[pl.BlockSpec((bsz, d), lambda j: (0, 0)),
                  pl.BlockSpec((None, d, tn), lambda j: (layer, 0, j)),
                  pl.BlockSpec((1, tn), lambda j: (0, j))],
        out_specs=pl.BlockSpec((bsz, tn), lambda j: (0, j)),
        compiler_params=_cparams(("arbitrary",)),
        name="modvec",
    )(c, w, b.reshape(1, n))


def _mod_spec(chunk, tiles_per_batch, d):
    return pl.BlockSpec((None, None, 1, d), lambda i, *_: (i // tiles_per_batch, chunk, 0, 0))


def _mla_proj_kernel(x_ref, pos_ref, sh_ref, sc_ref, g_ref, wd_ref, gq_ref, gkv_ref,
                     wqn_ref, wqr_ref, wkn_ref, wv_ref, inv_ref,
                     q_ref, k_ref, v_ref, *, q_lora, kv_lora):
    x = x_ref[...]
    h = _modulate(_rms(x, g_ref[...]), sh_ref[...], sc_ref[...]).astype(BF16)
    lat = _dot(h, wd_ref[...])
    c_q = _rms(lat[:, :q_lora], gq_ref[...]).astype(BF16)
    c_kv = _rms(lat[:, q_lora:q_lora + kv_lora], gkv_ref[...]).astype(BF16)
    k_rot = lat[:, q_lora + kv_lora:]

    half = QK_ROPE // 2
    ang = pos_ref[...].astype(F32) * inv_ref[...]
    cos_p = jnp.cos(ang)
    sin_p = jnp.sin(ang)
    lane = lax.broadcasted_iota(jnp.int32, ang.shape, 1)
    cos, s_a, s_b = [], [], []
    for g in range(ROPE_PACK):
        shift = (LANES - g * half) % LANES
        cg = pltpu.roll(cos_p, shift, axis=1) if shift else cos_p
        sg = pltpu.roll(sin_p, shift, axis=1) if shift else sin_p
        cos.append(jnp.where(lane < half, cg, pltpu.roll(cg, half, axis=1)))
        s_a.append(jnp.where(lane < half, -sg, 0.0))
        s_b.append(jnp.where((lane >= half) & (lane < 2 * half), pltpu.roll(sg, half, axis=1), 0.0))
    cos, s_a, s_b = (jnp.concatenate(v, axis=0) for v in (cos, s_a, s_b))

    def rope(r):
        return (r * cos + pltpu.roll(r, LANES - half, axis=1) * s_a
                + pltpu.roll(r, half, axis=1) * s_b)

    k_rot = rope(k_rot).astype(BF16)
    q_nope = _dot(c_q, wqn_ref[...])
    q_rope = _dot(c_q, wqr_ref[...])
    k_nope = _dot(c_kv, wkn_ref[...])
    v_ref[...] = _dot(c_kv, wv_ref[...]).astype(BF16)
    for hd in range(MLA_HEADS):
        a, b = hd * LANES, (hd + 1) * LANES
        q_ref[:, hd * QK_PAD:hd * QK_PAD + LANES] = q_nope[:, a:b].astype(BF16)
        q_ref[:, hd * QK_PAD + LANES:(hd + 1) * QK_PAD] = rope(q_rope[:, a:b]).astype(BF16)
        k_ref[:, hd * QK_PAD:hd * QK_PAD + LANES] = k_nope[:, a:b].astype(BF16)
        k_ref[:, hd * QK_PAD + LANES:(hd + 1) * QK_PAD] = k_rot


def _mla_proj(x, pos, mod, g_mix, wd, gq, gkv, wqn, wqr, wkn, wv, inv, *, seq, tm):
    n, d = x.shape
    tpb = seq // tm
    q_lora, kv_lora = gq.shape[1], gkv.shape[1]
    hq = MLA_HEADS * QK_PAD
    hv = MLA_HEADS * V_DIM
    full = lambda a: pl.BlockSpec(a.shape, lambda i: (0,) * a.ndim)
    row = lambda w: pl.BlockSpec((tm, w), lambda i: (i, 0))
    return pl.pallas_call(
        functools.partial(_mla_proj_kernel, q_lora=q_lora, kv_lora=kv_lora),
        out_shape=(jax.ShapeDtypeStruct((n, hq), BF16), jax.ShapeDtypeStruct((n, hq), BF16),
                   jax.ShapeDtypeStruct((n, hv), BF16)),
        grid=(n // tm,),
        in_specs=[row(d), pl.BlockSpec((tm // ROPE_PACK, LANES), lambda i: (i, 0)), _mod_spec(0, tpb, d), _mod_spec(1, tpb, d), full(g_mix), full(wd),
                  full(gq), full(gkv), full(wqn), full(wqr), full(wkn), full(wv), full(inv)],
        out_specs=(row(hq), row(hq), row(hv)),
        compiler_params=_cparams(("parallel",)),
        name="mla_proj",
    )(x, pos, mod, mod, g_mix, wd, gq, gkv, wqn, wqr, wkn, wv, inv)


def _lane_chunks(a):
    return [a[:, c * LANES:(c + 1) * LANES] for c in range(a.shape[1] // LANES)]


def _for_blocks(n, step, group):
    def trip(p, carry):
        step(group * p, group)
        return carry

    lax.fori_loop(0, n // group, trip, 0)
    size = group // 2
    while size:
        @pl.when((n // size) % 2 == 1)
        def _(size=size):
            step(n // (2 * size) * (2 * size), size)
        size //= 2


def _mla_attn_kernel(q_ref, k_ref, v_ref, o_ref, s_ref, m_ref, acc_ref, *, t, scale, heads):
    i = pl.program_id(2)

    def lane_max(s):
        m = None
        for sc in _lane_chunks(s):
            m = sc if m is None else jnp.maximum(m, sc)
        return m

    def scores(h, j):
        start = pl.multiple_of(j * t, t)
        hs = slice(h * QK_PAD, (h + 1) * QK_PAD)
        return _dot_nt(q_ref[0, :, hs], k_ref[0, pl.ds(start, t), hs])

    r_minus_c = (lax.broadcasted_iota(jnp.int32, (t, t), 0) - lax.broadcasted_iota(jnp.int32, (t, t), 1))
    m_ref[...] = jnp.full(m_ref.shape, -jnp.inf, F32)

    def pass1(j, nb):
        for h in range(heads):
            m = m_ref[h]
            for jj in range(nb):
                s = jnp.where(r_minus_c >= (j + jj - i) * t, scores(h, j + jj), -jnp.inf)
                s_ref[h, j + jj] = s
                m = jnp.maximum(m, lane_max(s))
            m_ref[h] = m

    _for_blocks(i + 1, pass1, MLA_BLOCKS_PER_TRIP)
    for h in range(heads):
        m_ref[h] = jnp.broadcast_to(jnp.max(m_ref[h], axis=1, keepdims=True), (t, LANES))
    acc_ref[...] = jnp.zeros(acc_ref.shape, F32)
    cst = scale * math.log2(math.e)

    def pass2(j, nb):
        start = pl.multiple_of(j * t, t)
        ones = jnp.ones((nb * t, LANES), BF16)
        for h in range(heads):
            m = m_ref[h]
            p = jnp.concatenate([jnp.exp2((sc - m) * cst).astype(BF16)
                                 for jj in range(nb) for sc in _lane_chunks(s_ref[h, j + jj])], axis=1)
            v_ext = jnp.concatenate([v_ref[0, pl.ds(start, nb * t), h * V_DIM:(h + 1) * V_DIM], ones], axis=1)
            acc_ref[h] += _dot(p, v_ext)

    _for_blocks(i + 1, pass2, MLA_BLOCKS_PER_TRIP)
    for h in range(heads):
        acc = acc_ref[h]
        o_ref[0, :, h * V_DIM:(h + 1) * V_DIM] = (acc[:, :V_DIM] / acc[:, V_DIM:]).astype(o_ref.dtype)


def _mla_attn(q, k, v, *, t, heads=2):
    bsz, seq, _ = q.shape
    scale = 1.0 / math.sqrt(QK_NOPE + QK_ROPE)
    wq, wv = heads * QK_PAD, heads * V_DIM
    return pl.pallas_call(
        functools.partial(_mla_attn_kernel, t=t, scale=scale, heads=heads),
        out_shape=jax.ShapeDtypeStruct((bsz, seq, MLA_HEADS * V_DIM), BF16),
        grid=(bsz, MLA_HEADS // heads, seq // t),
        in_specs=[pl.BlockSpec((1, t, wq), lambda b, g, i: (b, i, g)),
                  pl.BlockSpec((1, seq, wq), lambda b, g, i: (b, 0, g)),
                  pl.BlockSpec((1, seq, wv), lambda b, g, i: (b, 0, g))],
        out_specs=pl.BlockSpec((1, t, wv), lambda b, g, i: (b, i, g)),
        scratch_shapes=[pltpu.VMEM((heads, seq // t, t, t), F32), pltpu.VMEM((heads, t, LANES), F32),
                        pltpu.VMEM((heads, t, V_DIM + LANES), F32)],
        compiler_params=_cparams(("parallel", "parallel", "arbitrary")),
        name="mla_attn",
    )(q, k, v)


def _sb_attn_kernel(q_ref, k_ref, v_ref, w_ref, o_ref, c_ref, acc_ref, kmax_ref, bound_ref, *, t, heads):
    i = pl.program_id(2)
    d = SB_HEAD_DIM
    c_ref[...] = jnp.zeros(c_ref.shape, F32)
    acc_ref[...] = jnp.zeros(acc_ref.shape, F32)
    sub = t // 2 if t % (2 * LANES) == 0 else t
    r_iota = lax.broadcasted_iota(jnp.int32, (sub, LANES), 0)
    c_iota = lax.broadcasted_iota(jnp.int32, (sub, LANES), 1)

    def block(r0, nr, start, nk, key0):
        rows = slice(r0, r0 + nr)
        masks = []
        for cc in range(nk // LANES):
            off = None if key0 is None else key0 + cc * LANES - r0
            masks.append(None if off is None or off <= -LANES else c_iota + off < r_iota)
        for h in range(heads):
            hs = slice(h * d, (h + 1) * d)
            zz = _dot_nt(q_ref[0, rows, hs], k_ref[0, pl.ds(start, nk), hs])
            zneg = -zz
            sp = jnp.log2(1.0 + jnp.exp2(jnp.minimum(zz, zneg)))
            log_1m = jnp.minimum(zneg, 0.0) - sp
            carry = c_ref[h, rows]
            a_chunks = [None] * (nk // LANES)
            for cc in reversed(range(nk // LANES)):
                cs = slice(cc * LANES, (cc + 1) * LANES)
                l1m = log_1m[:, cs] if masks[cc] is None else jnp.where(masks[cc], log_1m[:, cs], 0.0)
                hi, lo = _split_bf16(l1m)
                y = _dot(jnp.concatenate([hi, lo], axis=1), w_ref[...])
                a = jnp.exp2(zz[:, cs] + y[:, :LANES] + carry)
                carry = carry + y[:, LANES:]
                if masks[cc] is not None:
                    a = jnp.where(masks[cc], a, 0.0)
                a_chunks[cc] = a.astype(BF16)
            c_ref[h, rows] = carry
            acc_ref[h, rows] += _dot(jnp.concatenate(a_chunks, axis=1), v_ref[0, pl.ds(start, nk), hs])

    def diagonal_tile():
        for band in range(t // sub):
            block(band * sub, sub, pl.multiple_of(i * t, t), (band + 1) * sub, 0)

    def left_blocks(n, nb):
        for jj in range(nb):
            block(0, t, pl.multiple_of((i - 1 - n - jj) * t, t), t, None)

    @pl.when(i == 0)
    def _():
        diagonal_tile()

    @pl.when(i > 0)
    def _():
        diagonal_tile()
        left_blocks(0, 1)

    @pl.when(i == 0)
    def _():
        for h in range(heads):
            kmax = jnp.max(jnp.abs(k_ref[0, :, h * d:(h + 1) * d].astype(F32)))
            kmax_ref[h] = jnp.full(kmax_ref.shape[1:], kmax, F32)

    for h in range(heads):
        q_l1 = jnp.sum(jnp.abs(q_ref[0, :, h * d:(h + 1) * d].astype(F32)), axis=1, keepdims=True)
        bound_ref[h] = q_l1 * kmax_ref[h, 0:1, :]

    def more_to_add():
        worst = None
        for h in range(heads):
            top = jnp.max(c_ref[h] + bound_ref[h])
            worst = top if worst is None else jnp.maximum(worst, top)
        return worst >= SB_ZERO_LOG2

    def trip(state):
        n, _ = state
        left_blocks(n, SB_BLOCKS_PER_TRIP)
        return n + SB_BLOCKS_PER_TRIP, more_to_add()

    n, go = lax.while_loop(lambda s: s[1] & (s[0] + SB_BLOCKS_PER_TRIP <= i), trip,
                           (jnp.int32(1), (i > 0) & more_to_add()))
    size = SB_BLOCKS_PER_TRIP // 2
    while size:
        take = go & (n + size <= i)

        @pl.when(take)
        def _(n=n, size=size):
            left_blocks(n, size)

        n = n + jnp.where(take, size, 0)
        size //= 2
    for h in range(heads):
        o_ref[0, :, h * d:(h + 1) * d] = acc_ref[h].astype(o_ref.dtype)


def _sb_attn(q, kv, *, t, heads=4):
    bsz, seq, _ = q.shape
    w = heads * SB_HEAD_DIM
    groups = SB_HEADS // heads
    tri = (jnp.arange(LANES)[:, None] >= jnp.arange(LANES)[None, :]).astype(BF16)
    half = jnp.concatenate([tri, jnp.ones((LANES, LANES), BF16)], axis=1)
    w_sum = jnp.concatenate([half, half], axis=0)
    return pl.pallas_call(
        functools.partial(_sb_attn_kernel, t=t, heads=heads),
        out_shape=jax.ShapeDtypeStruct((bsz, seq, SB_HEADS * SB_HEAD_DIM), BF16),
        grid=(bsz, groups, seq // t),
        in_specs=[pl.BlockSpec((1, t, w), lambda b, g, i: (b, i, g)),
                  pl.BlockSpec((1, seq, w), lambda b, g, i: (b, 0, g)),
                  pl.BlockSpec((1, seq, w), lambda b, g, i: (b, 0, groups + g)),
                  pl.BlockSpec(w_sum.shape, lambda b, g, i: (0, 0))],
        out_specs=pl.BlockSpec((1, t, w), lambda b, g, i: (b, i, g)),
        scratch_shapes=[pltpu.VMEM((heads, t, LANES), F32), pltpu.VMEM((heads, t, SB_HEAD_DIM), F32),
                        pltpu.VMEM((heads, 8, LANES), F32), pltpu.VMEM((heads, t, LANES), F32)],
        compiler_params=_cparams(("parallel", "parallel", "arbitrary")),
        name="sb_attn",
    )(q, kv, kv, w_sum)


def _out_proj_kernel(o_ref, w_ref, x_ref, gt_ref, g_ref, sh_ref, sc_ref, wr_hi_ref, wr_lo_ref, br_ref, tri_ref,
                     xo_ref, h_ref, rt_ref, rtt_ref, cnt_ref, *, n_experts):
    x = x_ref[...] + gt_ref[...] * _dot(o_ref[...], w_ref[...])
    xo_ref[...] = x
    h = _modulate(_rms(x, g_ref[...]), sh_ref[...], sc_ref[...])
    h_hi = h.astype(BF16)
    words = _pack_bf16_pairs(h)
    for part in range(SC_ROW_PARTS):
        h_ref[part] = words[:, part * h_ref.shape[2]:(part + 1) * h_ref.shape[2]]
    h_lo = (h - h_hi.astype(F32)).astype(BF16)
    both = _dot(h_hi, jnp.concatenate([wr_hi_ref[...], wr_lo_ref[...]], axis=1))
    logits = both[:, :LANES] + both[:, LANES:] + _dot(h_lo, wr_hi_ref[...]) + br_ref[...]
    lane = lax.broadcasted_iota(jnp.int32, logits.shape, 1).astype(F32)
    lg = jnp.where(lane < n_experts, logits, -jnp.inf)
    m1 = jnp.max(lg, axis=1, keepdims=True)
    i1 = jnp.min(jnp.where(lg == m1, lane, float(LANES)), axis=1, keepdims=True)
    lg2 = jnp.where(lane == i1, -jnp.inf, lg)
    m2 = jnp.max(lg2, axis=1, keepdims=True)
    i2 = jnp.min(jnp.where(lg2 == m2, lane, float(LANES)), axis=1, keepdims=True)
    e2 = jnp.exp(m2 - m1)
    den = 1.0 + e2
    sel = jnp.where((lane == i1) | (lane == i2), 1.0, 0.0)
    prefix = _dot(tri_ref[...], sel.astype(BF16))
    r1 = jnp.sum(jnp.where(lane == i1, prefix, 0.0), axis=1, keepdims=True)
    r2 = jnp.sum(jnp.where(lane == i2, prefix, 0.0), axis=1, keepdims=True)
    cnt_ref[...] = jnp.sum(sel, axis=0, keepdims=True)
    rt = jnp.zeros_like(logits)
    for k, val in enumerate((i1, i2, 1.0 / den, e2 / den, r1, r2)):
        rt = jnp.where(lane == k, val, rt)
    rt_ref[...] = rt
    rtt_ref[...] = rt.T[:PICK_ROWS]


def _out_proj_router(o, w, x, mod, g, wr_hi, wr_lo, br, n_experts, *, seq, tm):
    n, d = x.shape
    tpb = seq // tm
    dpp = d // 2 // SC_ROW_PARTS
    tri = (jnp.arange(tm)[:, None] > jnp.arange(tm)[None, :]).astype(BF16)
    full = lambda a: pl.BlockSpec(a.shape, lambda i: (0,) * a.ndim)
    row = lambda wd: pl.BlockSpec((tm, wd), lambda i: (i, 0))
    return pl.pallas_call(
        functools.partial(_out_proj_kernel, n_experts=n_experts),
        out_shape=(jax.ShapeDtypeStruct((n, d), F32), jax.ShapeDtypeStruct((SC_ROW_PARTS, n, dpp), U32),
                   jax.ShapeDtypeStruct((n, LANES), F32), jax.ShapeDtypeStruct((PICK_ROWS, n), F32),
                   jax.ShapeDtypeStruct((n // tm, 1, LANES), F32)),
        grid=(n // tm,),
        in_specs=[row(o.shape[1]), full(w), row(d), _mod_spec(2, tpb, d), full(g),
                  _mod_spec(3, tpb, d), _mod_spec(4, tpb, d), full(wr_hi), full(wr_lo), full(br), full(tri)],
        out_specs=(row(d), pl.BlockSpec((SC_ROW_PARTS, tm, dpp), lambda i: (0, i, 0)), row(LANES),
                   pl.BlockSpec((PICK_ROWS, tm), lambda i: (0, i)),
                   pl.BlockSpec((None, 1, LANES), lambda i: (i, 0, 0))),
        compiler_params=_cparams(("parallel",)),
        name="out_proj_router",
    )(o, w, x, mod, g, mod, mod, wr_hi, wr_lo, br, tri)


def _silu(g):
    return g * (1.0 / (1.0 + jnp.exp(-g)))


def _ffn_kernel(o_ref, wo_ref, x_ref, gt1_ref, gf_ref, sh2_ref, sc2_ref, wg_ref, wu_ref, wd_ref, gt2_ref,
                gkv_ref, shk_ref, sck_ref, gm_ref, shm_ref, scm_ref,
                xo_ref, hk_ref, hm_ref, x1_ref, h_ref, acc_ref):
    i = pl.program_id(0)
    f = pl.program_id(1)
    n_tiles = pl.num_programs(0) - 1

    def open_tile(slot):
        x1 = x_ref[...] + gt1_ref[...] * _dot(o_ref[...], wo_ref[...])
        x1_ref[slot] = x1
        h_ref[...] = _modulate(_rms(x1, gf_ref[...]), sh2_ref[...], sc2_ref[...]).astype(BF16)
        acc_ref[slot] = jnp.zeros(acc_ref.shape[1:], F32)

    def close_tile(slot):
        x = x1_ref[slot] + gt2_ref[...] * acc_ref[slot]
        xo_ref[...] = x
        y = x * lax.rsqrt(jnp.mean(x * x, axis=-1, keepdims=True) + EPS)
        hk_ref[...] = _modulate(y * gkv_ref[...], shk_ref[...], sck_ref[...]).astype(BF16)
        hm_ref[...] = _modulate(y * gm_ref[...], shm_ref[...], scm_ref[...]).astype(BF16)

    for parity in range(2):
        mine = (f == 0) & (i % 2 == parity)

        @pl.when(mine & (i == 0))
        def _(parity=parity):
            open_tile(parity)

        @pl.when(mine & (i > 0) & (i < n_tiles))
        def _(parity=parity):
            open_tile(parity)
            close_tile(1 - parity)

        @pl.when(mine & (i == n_tiles))
        def _(parity=parity):
            close_tile(1 - parity)

    @pl.when(i < n_tiles)
    def _():
        half = h_ref.shape[0] // 2
        for r0 in (0, half):
            h = h_ref[r0:r0 + half, :]
            a = _silu(_dot(h, wg_ref[...])) * _dot(h, wu_ref[...])
            acc_ref[i % 2, r0:r0 + half, :] += _dot(a.astype(BF16), wd_ref[...])


def _ffn(o, w_o, x, mod0, g_ffn, w_gu, w_down, g_kv, mod_kv, g_mix1, mod1, *, seq, tm, tf):
    n, d = x.shape
    d_ff = w_down.shape[0]
    nf = d_ff // tf
    tpb = seq // tm
    nt = n // tm
    opened = lambda i: jnp.minimum(i, nt - 1)
    closed = lambda i: jnp.maximum(i - 1, 0)
    chunk = lambda i, f: jnp.where(i < nt, f, nf - 1)
    full = lambda a: pl.BlockSpec(a.shape, lambda i, f: (0,) * a.ndim)
    row_in = lambda w=d: pl.BlockSpec((tm, w), lambda i, f: (opened(i), 0))
    row_out = lambda: pl.BlockSpec((tm, d), lambda i, f: (closed(i), 0))
    mod = lambda c, tile: pl.BlockSpec((None, None, 1, d), lambda i, f: (tile(i) // tpb, c, 0, 0))
    return pl.pallas_call(
        _ffn_kernel,
        out_shape=(jax.ShapeDtypeStruct((n, d), F32), jax.ShapeDtypeStruct((n, d), BF16),
                   jax.ShapeDtypeStruct((n, d), BF16)),
        grid=(nt + 1, nf),
        in_specs=[row_in(o.shape[1]), full(w_o), row_in(), mod(2, opened), full(g_ffn),
                  mod(3, opened), mod(4, opened),
                  pl.BlockSpec((d, tf), lambda i, f: (0, chunk(i, f))),
                  pl.BlockSpec((d, tf), lambda i, f: (0, nf + chunk(i, f))),
                  pl.BlockSpec((tf, d), lambda i, f: (chunk(i, f), 0)),
                  mod(5, closed),
                  full(g_kv), mod(0, closed), mod(1, closed),
                  full(g_mix1), mod(0, closed), mod(1, closed)],
        out_specs=(row_out(), row_out(), row_out()),
        scratch_shapes=[pltpu.VMEM((2, tm, d), F32), pltpu.VMEM((tm, d), BF16), pltpu.VMEM((2, tm, d), F32)],
        compiler_params=_cparams(("arbitrary", "arbitrary"), vmem_limit=VMEM_LIMIT_LARGE),
        name="ffn_dense",
    )(o, w_o, x, mod0, g_ffn, mod0, mod0, w_gu, w_gu, w_down, mod0,
      g_kv, mod_kv, mod_kv, g_mix1, mod1, mod1)


def _linear_kernel(x_ref, w_ref, o_ref):
    o_ref[...] = _dot(x_ref[...], w_ref[...]).astype(o_ref.dtype)


def _linear(x, w, *, tm, tn, name):
    n, k = x.shape
    m = w.shape[1]
    return pl.pallas_call(
        _linear_kernel,
        out_shape=jax.ShapeDtypeStruct((n, m), BF16),
        grid=(m // tn, n // tm),
        in_specs=[pl.BlockSpec((tm, k), lambda j, i: (i, 0)), pl.BlockSpec((k, tn), lambda j, i: (0, j))],
        out_specs=pl.BlockSpec((tm, tn), lambda j, i: (i, j)),
        compiler_params=_cparams(("parallel", "parallel")),
        name=name,
    )(x, w)


def _route_plan(counts, n_exp, tm, n_tiles):
    cnt = counts[:, 0, :n_exp].astype(jnp.int32)
    sizes = jnp.sum(cnt, axis=0)
    padded = (sizes + tm - 1) // tm * tm
    ends = jnp.cumsum(padded)
    tile_base = (ends - padded)[None, :] + jnp.cumsum(cnt, axis=0) - cnt
    tile_start = jnp.arange(n_tiles, dtype=jnp.int32) * tm
    tile_expert = jnp.minimum(jnp.sum(tile_start[:, None] >= ends[None, :], axis=1), n_exp - 1)
    n_used = (ends[-1] // tm).reshape(1)
    n_valid = jnp.clip((ends - padded + sizes)[tile_expert] - tile_start, 0, tm)
    return tile_base, tile_expert.astype(jnp.int32), n_used.astype(jnp.int32), n_valid.astype(jnp.int32)


PICK_ROWS = 8


def _picks_kernel(rtt_ref, base_ref, o_ref, *, n_rows):
    rtt = rtt_ref[...]
    sub = lax.broadcasted_iota(jnp.int32, rtt.shape, 0)
    expert = sub.astype(F32)
    out = jnp.zeros_like(rtt)
    for k in range(TOP_K):
        chosen = expert == rtt[k:k + 1, :]
        d_k = (jnp.sum(jnp.where(chosen, base_ref[...], 0.0), axis=0, keepdims=True)
               + rtt[2 * TOP_K + k:2 * TOP_K + k + 1, :])
        for p in range(SC_ROW_PARTS):
            out = jnp.where(sub == p * TOP_K + k, d_k + float(p * n_rows), out)
    o_ref[...] = out.astype(jnp.int32)


def _picks(rtt, tile_base, n_rows, *, tm):
    n = rtt.shape[1]
    n_tok_tiles, n_exp = tile_base.shape
    assert n_exp <= PICK_ROWS
    base = jnp.pad(tile_base.astype(F32), ((0, 0), (0, PICK_ROWS - n_exp))).reshape(n_tok_tiles, PICK_ROWS, 1)
    out = pl.pallas_call(
        functools.partial(_picks_kernel, n_rows=n_rows),
        out_shape=jax.ShapeDtypeStruct((PICK_ROWS, n), jnp.int32),
        grid=(n // tm,),
        in_specs=[pl.BlockSpec((PICK_ROWS, tm), lambda i: (0, i)),
                  pl.BlockSpec((None, PICK_ROWS, 1), lambda i: (i, 0, 0))],
        out_specs=pl.BlockSpec((PICK_ROWS, tm), lambda i: (0, i)),
        compiler_params=_cparams(("parallel",)),
        name="moe_picks",
    )(rtt, base)
    return out[:SC_ROW_PARTS * TOP_K].reshape(SC_ROW_PARTS, TOP_K, n)


def _sc_mesh():
    return plsc.VectorSubcoreMesh(core_axis_name="core", subcore_axis_name="subcore",
                                  num_cores=SC_CORES, num_subcores=SC_SUBCORES)


def _sc_scatter_rows(x, idx_list, n_rows):
    m, d = x.shape

    @functools.partial(pl.kernel, out_type=jax.ShapeDtypeStruct((n_rows, d), x.dtype), mesh=_sc_mesh(),
                       scratch_types=[], name="sc_scatter_rows")
    def scatter(x_hbm, *refs):
        i_hbms, o_hbm = refs[:-1], refs[-1]

        def body(x_vmem, *i_vmems):
            for i_vmem in i_vmems:
                pltpu.sync_copy(x_vmem, o_hbm.at[i_vmem.at[0]])

        pltpu.emit_pipeline(
            body,
            grid=(m // SC_WINDOW,),
            in_specs=[pl.BlockSpec((SC_WINDOW, d), index_map=lambda i: (i, 0))]
                     + [pl.BlockSpec((1, SC_WINDOW), index_map=lambda i: (0, i))] * len(idx_list),
            out_specs=[],
            core_axis_name=("core", "subcore"),
            dimension_semantics=(pltpu.PARALLEL,),
        )(x_hbm, *i_hbms)

    return scatter(x, *[idx.reshape(1, m) for idx in idx_list])


def _grouped_ffn_kernel(te_ref, nu_ref, nv_ref, x_ref, wg_ref, wu_ref, wd_ref, y_ref, hb_ref, acc_ref):
    del te_ref
    i = pl.program_id(0)
    f = pl.program_id(1)
    used = i < nu_ref[0]
    last_f = f == pl.num_programs(1) - 1

    @pl.when(used & (f == 0))
    def _():
        words = jnp.concatenate([x_ref[p] for p in range(SC_ROW_PARTS)], axis=1)
        row = lax.broadcasted_iota(jnp.int32, words.shape, 0)
        words = jnp.where(row < nv_ref[i], words, U32(0))
        hb_ref[...] = _unpack_bf16_pairs(words).astype(BF16)
        acc_ref[...] = jnp.zeros(acc_ref.shape, F32)

    @pl.when(used)
    def _():
        half = hb_ref.shape[0] // 2
        for r0 in (0, half):
            h = hb_ref[r0:r0 + half, :]
            a = _silu(_dot(h, wg_ref[0])) * _dot(h, wu_ref[0])
            acc_ref[r0:r0 + half, :] += _dot(a.astype(BF16), wd_ref[0])

    @pl.when(used & last_f)
    def _():
        y = _pack_bf16_pairs(acc_ref[...])
        for part in range(SC_ROW_PARTS):
            y_ref[part] = y[:, part * y_ref.shape[2]:(part + 1) * y_ref.shape[2]]

    @pl.when(jnp.logical_not(used) & last_f)
    def _():
        y_ref[...] = jnp.zeros(y_ref.shape, y_ref.dtype)


def _grouped_ffn(tile_expert, n_used, n_valid, xg, w_gu, w_down, *, tm, tf):
    parts, n_rows, dpp = xg.shape
    n_exp, d_ff, d = w_down.shape
    nf = d_ff // tf

    def wspec(shape, index):
        def index_map(i, f, te, nu, nv):
            return index(te[i], jnp.where(i < nu[0], f, nf - 1))
        return pl.BlockSpec(shape, index_map)

    rows = pl.BlockSpec((parts, tm, dpp), lambda i, f, te, nu, nv: (0, i, 0))
    return pl.pallas_call(
        _grouped_ffn_kernel,
        out_shape=jax.ShapeDtypeStruct(xg.shape, xg.dtype),
        grid_spec=pltpu.PrefetchScalarGridSpec(
            num_scalar_prefetch=3,
            grid=(n_rows // tm, nf),
            in_specs=[rows,
                      wspec((1, d, tf), lambda e, f: (e, 0, f)),
                      wspec((1, d, tf), lambda e, f: (e, 0, nf + f)),
                      wspec((1, tf, d), lambda e, f: (e, f, 0))],
            out_specs=rows,
            scratch_shapes=[pltpu.VMEM((tm, d), BF16), pltpu.VMEM((tm, d), F32)]),
        compiler_params=_cparams(("arbitrary", "arbitrary")),
        name="moe_grouped_ffn",
    )(tile_expert, n_used, n_valid, xg, w_gu, w_gu, w_down)


def _sc_gather_rows(table, idx):
    n_idx = idx.shape[0]
    d = table.shape[1]

    @functools.partial(pl.kernel, out_type=jax.ShapeDtypeStruct((n_idx, d), table.dtype), mesh=_sc_mesh(),
                       name="sc_gather_rows")
    def gather(t_hbm, i_hbm, o_hbm):
        def body(i_vmem, o_vmem):
            pltpu.sync_copy(t_hbm.at[i_vmem.at[0]], o_vmem)

        pltpu.emit_pipeline(
            body,
            grid=(n_idx // SC_WINDOW,),
            in_specs=[pl.BlockSpec((1, SC_WINDOW), index_map=lambda i: (0, i))],
            out_specs=[pl.BlockSpec((SC_WINDOW, d), index_map=lambda i: (i, 0))],
            core_axis_name=("core", "subcore"),
            dimension_semantics=(pltpu.PARALLEL,),
        )(i_hbm, o_hbm)

    return gather(table, idx.reshape(1, n_idx))


def _combine_kernel(x_ref, rt_ref, gt_ref, gf_ref, *refs):
    y_refs, o_ref = refs[:-1], refs[-1]
    rt = rt_ref[...]
    tot = None
    for k in range(TOP_K):
        words = jnp.concatenate([y_refs[p * TOP_K + k][...] for p in range(SC_ROW_PARTS)], axis=1)
        term = rt[:, TOP_K + k:TOP_K + k + 1] * _unpack_bf16_pairs(words)
        tot = term if tot is None else tot + term
    x = x_ref[...] + gt_ref[...] * tot
    o_ref[...] = _rms(x, gf_ref[...])


def _combine(x, rt, mod1, g_final, ysel, *, seq, tm):
    n, d = x.shape
    tpb = seq // tm
    nt = n // tm
    row = lambda w: pl.BlockSpec((tm, w), lambda i: (i, 0))
    piece = lambda j: pl.BlockSpec((tm, ysel.shape[1]), lambda i: (j * nt + i, 0))
    n_pieces = SC_ROW_PARTS * TOP_K
    return pl.pallas_call(
        _combine_kernel,
        out_shape=jax.ShapeDtypeStruct((n, d), F32),
        grid=(nt,),
        in_specs=[row(d), row(LANES), _mod_spec(5, tpb, d), pl.BlockSpec(g_final.shape, lambda i: (0, 0))]
                 + [piece(j) for j in range(n_pieces)],
        out_specs=row(d),
        compiler_params=_cparams(("parallel",)),
        name="moe_combine",
    )(x, rt, mod1, g_final, *([ysel] * n_pieces))


def _pad_last(a, width):
    return jnp.pad(a, [(0, 0)] * (a.ndim - 1) + [(0, width - a.shape[-1])])


def kernel(x, c, positions, w_mod, b_mod, g_mix, g_ffn, w_a_down, g_q_lat, g_kv_lat, w_uq, w_ukv, w_oa,
           w_mod_kv, b_mod_kv, g_kv, w_kv_sb, w_q_sb, w_o_sb, w_ffn_gu, w_ffn_down, w_router, b_router,
           w_exp_gu, w_exp_down, g_final):
    bsz, seq, d = x.shape
    n = bsz * seq
    q_lora, kv_lora = g_q_lat.shape[1], g_kv_lat.shape[1]
    n_exp = w_router.shape[-1]
    d_ff = w_ffn_down.shape[1]
    tm = min(TOKEN_TILE, seq)
    t_mla = min(MLA_TILE, seq)
    t_sb = min(SB_TILE, seq)
    tf = next((t for t in (FFN_CHUNK, 512) if d_ff % t == 0), d_ff)

    mod0 = _modvec(c, w_mod, 0, b_mod[0]).reshape(bsz, 6, 1, d)
    mod1 = _modvec(c, w_mod, 1, b_mod[1]).reshape(bsz, 6, 1, d)
    mod_kv = _modvec(c, w_mod_kv[None], 0, b_mod_kv).reshape(bsz, 2, 1, d)

    lat_w = q_lora + kv_lora + LANES
    wd = _pad_last(w_a_down[0], lat_w).astype(BF16)
    wq = w_uq[0].reshape(q_lora, MLA_HEADS, QK_NOPE + QK_ROPE)
    wqn = wq[:, :, :QK_NOPE].reshape(q_lora, MLA_HEADS * QK_NOPE).astype(BF16)
    wqr = _pad_last(wq[:, :, QK_NOPE:], LANES).reshape(q_lora, MLA_HEADS * LANES).astype(BF16)
    wkv = w_ukv[0].reshape(kv_lora, MLA_HEADS, QK_NOPE + V_DIM)
    wkn = wkv[:, :, :QK_NOPE].reshape(kv_lora, MLA_HEADS * QK_NOPE).astype(BF16)
    wv = wkv[:, :, QK_NOPE:].reshape(kv_lora, MLA_HEADS * V_DIM).astype(BF16)
    half = QK_ROPE // 2
    inv = ROPE_THETA ** (-jnp.arange(half, dtype=F32) / half)
    inv = jnp.tile(inv, ROPE_PACK).reshape(1, LANES)
    wr = _pad_last(w_router[0], LANES)
    wr_hi = wr.astype(BF16)
    wr_lo = (wr - wr_hi.astype(F32)).astype(BF16)
    br = _pad_last(b_router[0], LANES).reshape(1, LANES)

    xf = x.reshape(n, d)
    t_lin = min(LINEAR_TILE, seq)
    pos = positions.reshape(n // t_lin, ROPE_PACK, t_lin // ROPE_PACK).swapaxes(1, 2)
    pos = jnp.repeat(pos.reshape(n // ROPE_PACK, ROPE_PACK), QK_ROPE // 2, axis=1)
    row1 = lambda a: a.reshape(1, -1)

    q, k, v = _mla_proj(xf, pos, mod0, row1(g_mix[0]), wd, row1(g_q_lat[0]), row1(g_kv_lat[0]),
                        wqn, wqr, wkn, wv, inv, seq=seq, tm=t_lin)
    o = _mla_attn(q.reshape(bsz, seq, -1), k.reshape(bsz, seq, -1), v.reshape(bsz, seq, -1), t=t_mla)
    x2, hk, hm = _ffn(o.reshape(n, -1), w_oa[0].astype(BF16), xf, mod0, row1(g_ffn[0]),
                      w_ffn_gu[0].astype(BF16), w_ffn_down[0].astype(BF16),
                      row1(g_kv), mod_kv, row1(g_mix[1]), mod1, seq=seq, tm=tm, tf=tf)
    kv = _linear(hk, w_kv_sb.astype(BF16), tm=t_lin, tn=w_kv_sb.shape[1], name="kv_proj")
    q_scale = math.log2(math.e) / math.sqrt(SB_HEAD_DIM)
    qs = _linear(hm, (w_q_sb[0] * q_scale).astype(BF16), tm=t_lin, tn=1024, name="q_proj")
    o = _sb_attn(qs.reshape(bsz, seq, -1), kv.reshape(bsz, seq, -1), t=t_sb)
    x3, h, rt, rtt, counts = _out_proj_router(o.reshape(n, -1), w_o_sb[0].astype(BF16), x2, mod1,
                                              row1(g_ffn[1]), wr_hi, wr_lo, br, n_exp, seq=seq, tm=tm)
    n_tiles = TOP_K * n // tm + n_exp
    n_rows = n_tiles * tm
    tile_base, tile_expert, n_used, n_valid = _route_plan(counts, n_exp, tm, n_tiles)
    picks = _picks(rtt, tile_base, n_rows, tm=tm)
    xg = _sc_scatter_rows(h.reshape(SC_ROW_PARTS * n, -1), [picks[:, k].reshape(-1) for k in range(TOP_K)],
                          SC_ROW_PARTS * n_rows)
    yg = _grouped_ffn(tile_expert, n_used, n_valid, xg.reshape(SC_ROW_PARTS, n_rows, -1),
                      w_exp_gu[0].astype(BF16), w_exp_down[0].astype(BF16), tm=tm, tf=tf)
    ysel = _sc_gather_rows(yg.reshape(SC_ROW_PARTS * n_rows, -1), picks.reshape(-1))
    out = _combine(x3, rt, mod1, row1(g_final), ysel, seq=seq, tm=tm)
    return out.reshape(bsz, seq, d)
```
